```python
import math
import jax
import jax.numpy as jnp
from jax import lax
import numpy as np

D_MODEL = 1024
BATCH = 4
SEQ = 4096
DEPTH = 4

GRID_W = 64
CTX_LEN = 256
N_MIXERS = 3
N_S5_LAYERS = (DEPTH + 2) // 3
N_SSD_LAYERS = (DEPTH + 1) // 3
N_MLA_LAYERS = DEPTH // 3
DEEPNORM_ALPHA = (2.0 * DEPTH) ** 0.25
DEEPNORM_BETA = (8.0 * DEPTH) ** -0.25
LN_EPS = 1e-5
RMS_EPS = 1e-6
ROPE_BASE = 10000.0

S5_GROUP = 16
S5_GROUPS = D_MODEL // S5_GROUP
S5_STATE = 64
S5_DT_MIN = 1e-3
S5_DT_MAX = 1e-1

SSD_D_INNER = 2 * D_MODEL
SSD_HEADDIM = 64
SSD_HEADS = SSD_D_INNER // SSD_HEADDIM
SSD_GROUPS = 4
SSD_STATE = 128
SSD_CONV = 3
SSD_CHUNK = 128
SSD_CONV_DIM = SSD_D_INNER + 2 * SSD_GROUPS * SSD_STATE
SSD_IN_DIM = SSD_D_INNER + SSD_CONV_DIM + 2 * SSD_HEADS

MLA_HEADS = 16
MLA_Q_RANK = 256
MLA_KV_RANK = 128
MLA_NOPE = 64
MLA_ROPE = 32
MLA_V = 64
MLA_BLOCK = 128

PEER_HEADS = 8
PEER_KEYS = 128
PEER_EXPERTS = PEER_KEYS * PEER_KEYS
PEER_QDIM = 256
PEER_TOPK = 16
PEER_BLOCK = 128

kernel_name = 'hybrid_s5_ssd_mla_peer_dit'


def layer_norm(x, g, b):
    xf = x.astype(jnp.float32)
    mu = jnp.mean(xf, axis=-1, keepdims=True)
    var = jnp.mean(jnp.square(xf - mu), axis=-1, keepdims=True)
    return ((xf - mu) * lax.rsqrt(var + LN_EPS)).astype(x.dtype) * g + b


def rms_norm(x, g):
    xf = x.astype(jnp.float32)
    return (xf * lax.rsqrt(jnp.mean(jnp.square(xf), axis=-1, keepdims=True) + RMS_EPS)).astype(x.dtype) * g


def modulate(x, shift, scale):
    return x * (1.0 + scale) + shift


def grid_positions(rows):
    row = jnp.repeat(jnp.arange(rows, dtype=jnp.float32), GRID_W)
    col = jnp.tile(jnp.arange(GRID_W, dtype=jnp.float32), rows)
    return row, col


def rope_axial(x, row, col):
    half = x.shape[-1] // 2
    quarter = half // 2
    freqs = ROPE_BASE ** (-jnp.arange(quarter, dtype=jnp.float32) / quarter)

    def rot(xp, pos):
        ang = pos[:, None] * freqs
        cos = jnp.cos(ang)[None, :, None, :].astype(x.dtype)
        sin = jnp.sin(ang)[None, :, None, :].astype(x.dtype)
        x1, x2 = xp[..., :quarter], xp[..., quarter:]
        return jnp.concatenate([x1 * cos - x2 * sin, x2 * cos + x1 * sin], axis=-1)

    return jnp.concatenate([rot(x[..., :half], row), rot(x[..., half:], col)], axis=-1)


def _lin_rec(left, right):
    a1, b1 = left
    a2, b2 = right
    return a1 * a2, a2 * b1 + b2


def s5_scan(u, a_bar, b_bar, c_mat, init, reverse):
    bu = jnp.einsum('lgc,gpc->lgp', u.astype(jnp.float32), b_bar)
    a = jnp.broadcast_to(a_bar, bu.shape)
    a_cum, s = lax.associative_scan(_lin_rec, (a, bu), reverse=reverse, axis=0)
    s = s + a_cum * init
    y = jnp.einsum('lgp,gcp->lgc', s, c_mat).real
    final = s[0] if reverse else s[-1]
    return y, final


def s5_direction(u_ctx, u_lat, a_re, a_im, log_dt, b_re, b_im, c_re, c_im, reverse):
    lam = lax.complex(a_re.astype(jnp.float32), a_im.astype(jnp.float32))
    a_bar = jnp.exp(lam * jnp.exp(log_dt.astype(jnp.float32))[:, None])
    b_mat = lax.complex(b_re.astype(jnp.float32), b_im.astype(jnp.float32))
    b_bar = ((a_bar - 1.0) / lam)[..., None] * b_mat
    c_mat = lax.complex(c_re.astype(jnp.float32), c_im.astype(jnp.float32))

    def per_sample(args):
        uc, ul = args
        yc, sc = s5_scan(uc, a_bar, b_bar, c_mat, jnp.zeros_like(a_bar), reverse)
        yl, _ = s5_scan(ul, a_bar, b_bar, c_mat, sc, reverse)
        return yc, yl

    return lax.map(per_sample, (u_ctx, u_lat))


def s5_mixer(h, hc, a_re, a_im, log_dt, b_re, b_im, c_re, c_im, d, w_gate, w_val):
    bsz, seq_len, _ = h.shape
    ctx_len = hc.shape[1]
    u = h.reshape(bsz, seq_len, S5_GROUPS, S5_GROUP)
    uc = hc.reshape(bsz, ctx_len, S5_GROUPS, S5_GROUP)
    y_l = d * h
    y_c = d * hc
    for direction in range(2):
        yc, yl = s5_direction(uc, u, a_re[direction], a_im[direction], log_dt[direction],
                              b_re[direction], b_im[direction], c_re[direction], c_im[direction],
                              direction == 1)
        y_l = y_l + yl.reshape(bsz, seq_len, D_MODEL).astype(h.dtype)
        y_c = y_c + yc.reshape(bsz, ctx_len, D_MODEL).astype(h.dtype)

    def glu(y):
        g = jax.nn.gelu(y)
        return (g @ w_val) * jax.nn.sigmoid(g @ w_gate)

    return glu(y_l), glu(y_c)


def depthwise_conv_centred(x, w, b):
    k_w = w.shape[0]
    pad = k_w // 2
    seq_len = x.shape[1]
    xp = jnp.pad(x, ((0, 0), (pad, pad), (0, 0)))
    out = b
    for k in range(k_w):
        out = out + xp[:, k:k + seq_len] * w[k]
    return out


def ssd_scan(x, dt, a, bm, cm, init):
    bsz, seq_len, n_heads, p_dim = x.shape
    n_grp, n_st = bm.shape[2], bm.shape[3]
    hg = n_heads // n_grp
    q_len = SSD_CHUNK
    nc = seq_len // q_len
    xc = x.reshape(bsz, nc, q_len, n_grp, hg, p_dim)
    dtc = dt.reshape(bsz, nc, q_len, n_grp, hg).astype(jnp.float32)
    bc = bm.reshape(bsz, nc, q_len, n_grp, n_st)
    cc = cm.reshape(bsz, nc, q_len, n_grp, n_st)
    a_cum = jnp.cumsum(dtc * a.reshape(n_grp, hg), axis=2)
    xdt = xc * dtc[..., None].astype(x.dtype)
    tri = jnp.tril(jnp.ones((q_len, q_len), dtype=bool))
    seg = a_cum[:, :, :, None] - a_cum[:, :, None]
    decay = jnp.exp(jnp.where(tri[:, :, None, None], seg, -jnp.inf)).astype(x.dtype)
    cb = jnp.einsum('bcqgn,bcsgn->bcgqs', cc, bc)
    y_diag = jnp.einsum('bcgqs,bcqsgh,bcsghp->bcqghp', cb, decay, xdt)
    decay_states = jnp.exp(a_cum[:, :, -1:] - a_cum).astype(x.dtype)
    states = jnp.einsum('bcsgn,bcsgh,bcsghp->bcghpn', bc, decay_states, xdt)
    chunk_decay = jnp.exp(a_cum[:, :, -1]).astype(x.dtype)

    def step(carry, inp):
        dec, st = inp
        return carry * dec[..., None, None] + st, carry

    final, prev = lax.scan(step, init, (jnp.moveaxis(chunk_decay, 1, 0), jnp.moveaxis(states, 1, 0)))
    y_off = jnp.einsum('bcqgn,cbghpn,bcqgh->bcqghp', cc, prev, jnp.exp(a_cum).astype(x.dtype))
    return (y_diag + y_off).reshape(bsz, seq_len, n_heads, p_dim), final


def ssd_mixer(h, hc, w_in, conv_w, conv_b, dt_bias, a_log, d, norm_w, w_out):
    a = -jnp.exp(a_log.astype(jnp.float32))

    def flip(t):
        return jnp.flip(t, axis=1)

    def bidir(t, inits):
        bsz, seq_len = t.shape[0], t.shape[1]
        z, xbc, dt = jnp.split(t @ w_in, [SSD_D_INNER, SSD_D_INNER + SSD_CONV_DIM], axis=-1)
        xbc = jax.nn.silu(depthwise_conv_centred(xbc, conv_w, conv_b))
        xs, bm, cm = jnp.split(xbc, [SSD_D_INNER, SSD_D_INNER + SSD_GROUPS * SSD_STATE], axis=-1)
        xs = xs.reshape(bsz, seq_len, SSD_HEADS, SSD_HEADDIM)
        bm = bm.reshape(bsz, seq_len, SSD_GROUPS, SSD_STATE)
        cm = cm.reshape(bsz, seq_len, SSD_GROUPS, SSD_STATE)
        dt = jax.nn.softplus(dt.reshape(bsz, seq_len, 2, SSD_HEADS) + dt_bias)
        y_f, s_f = ssd_scan(xs, dt[:, :, 0], a[0], bm, cm, inits[0])
        y_b, s_b = ssd_scan(flip(xs), flip(dt[:, :, 1]), a[1], flip(bm), flip(cm), inits[1])
        y = y_f + flip(y_b) + d[:, None] * xs
        y = rms_norm(y.reshape(z.shape) * jax.nn.silu(z), norm_w)
        return y @ w_out, (s_f, s_b)

    zero = jnp.zeros((h.shape[0], SSD_GROUPS, SSD_HEADS // SSD_GROUPS, SSD_HEADDIM, SSD_STATE), h.dtype)
    out_c, states_c = bidir(hc, (zero, zero))
    out_l, _ = bidir(h, states_c)
    return out_l, out_c


def mla_project(t, w_down, q_norm, kv_norm, w_uq, w_uk, w_uv, pos):
    cq, ckv, kr = jnp.split(t @ w_down, [MLA_Q_RANK, MLA_Q_RANK + MLA_KV_RANK], axis=-1)
    q = jnp.einsum('btr,rhd->bthd', rms_norm(cq, q_norm), w_uq)
    ckv = rms_norm(ckv, kv_norm)
    k_nope = jnp.einsum('btr,rhd->bthd', ckv, w_uk)
    v = jnp.einsum('btr,rhd->bthd', ckv, w_uv)
    q_nope, q_rope = q[..., :MLA_NOPE], q[..., MLA_NOPE:]
    kr = kr[:, :, None, :]
    if pos is not None:
        q_rope = rope_axial(q_rope, pos[0], pos[1])
        kr = rope_axial(kr, pos[0], pos[1])
    k = jnp.concatenate([k_nope, jnp.broadcast_to(kr, k_nope.shape[:3] + (MLA_ROPE,))], axis=-1)
    q = jnp.concatenate([q_nope, q_rope], axis=-1)
    return q, k, v


def attend(q, k, v):
    s = jnp.einsum('bqhd,bkhd->bhqk', q, k).astype(jnp.float32) * (MLA_NOPE + MLA_ROPE) ** -0.5
    p = jax.nn.softmax(s, axis=-1).astype(v.dtype)
    return jnp.einsum('bhqk,bkhd->bqhd', p, v)


def mla_mixer(h, hc, pos, w_down, q_norm, kv_norm, w_uq, w_uk, w_uv, w_o):
    bsz, seq_len, _ = h.shape
    qc, kc, vc = mla_project(hc, w_down, q_norm, kv_norm, w_uq, w_uk, w_uv, None)
    ql, kl, vl = mla_project(h, w_down, q_norm, kv_norm, w_uq, w_uk, w_uv, pos)
    out_c = attend(qc, kc, vc).reshape(bsz, hc.shape[1], MLA_HEADS * MLA_V)
    k_all = jnp.concatenate([kc, kl], axis=1)
    v_all = jnp.concatenate([vc, vl], axis=1)
    nb = seq_len // MLA_BLOCK
    qb = jnp.moveaxis(ql.reshape(bsz, nb, MLA_BLOCK, MLA_HEADS, MLA_NOPE + MLA_ROPE), 1, 0)
    out_l = lax.map(lambda qblk: attend(qblk, k_all, v_all), qb)
    out_l = jnp.moveaxis(out_l, 0, 1).reshape(bsz, seq_len, MLA_HEADS * MLA_V)
    return out_l @ w_o, out_c @ w_o


def peer_ffn(h, w_q, subkeys, u_tab, v_tab):
    bsz, seq_len, dm = h.shape
    half = PEER_QDIM // 2
    q = jnp.einsum('btd,dhk->bthk', h, w_q)
    s1 = jnp.einsum('bthk,hnk->bthn', q[..., :half], subkeys[:, 0])
    s2 = jnp.einsum('bthk,hnk->bthn', q[..., half:], subkeys[:, 1])
    v1, i1 = lax.top_k(s1, PEER_TOPK)
    v2, i2 = lax.top_k(s2, PEER_TOPK)
    n_cand = PEER_TOPK * PEER_TOPK
    cand = (v1[..., :, None] + v2[..., None, :]).reshape(bsz, seq_len, PEER_HEADS, n_cand)
    cidx = (i1[..., :, None] * PEER_KEYS + i2[..., None, :]).reshape(bsz, seq_len, PEER_HEADS, n_cand)
    best, sel = lax.top_k(cand, PEER_TOPK)
    eidx = jnp.take_along_axis(cidx, sel, axis=-1)
    gate = jax.nn.softmax(best.astype(jnp.float32), axis=-1).astype(h.dtype)
    n_blk = (bsz * seq_len) // PEER_BLOCK
    n_sel = PEER_HEADS * PEER_TOPK
    hb = h.reshape(n_blk, PEER_BLOCK, dm)
    ib = eidx.reshape(n_blk, PEER_BLOCK, n_sel)
    gb = gate.reshape(n_blk, PEER_BLOCK, n_sel)

    def block(args):
        hk, ik, gk = args
        u = jnp.take(u_tab, ik, axis=0)
        act = jax.nn.gelu(jnp.einsum('td,ted->te', hk, u))
        v = jnp.take(v_tab, ik, axis=0)
        return jnp.einsum('te,ted->td', gk * act, v)

    return lax.map(block, (hb, ib, gb)).reshape(bsz, seq_len, dm)


def setup_inputs(seed: int = 0) -> dict:
    key = jax.random.key(seed)
    ks = iter(jax.random.split(key, 48))
    f32 = jnp.float32
    dm = D_MODEL
    na, nb, nc = N_S5_LAYERS, N_SSD_LAYERS, N_MLA_LAYERS

    def nrm(shape, scale):
        return jax.random.normal(next(ks), shape, f32) * scale

    def unif(shape, lo, hi):
        return jax.random.uniform(next(ks), shape, f32, lo, hi)

    x = nrm((BATCH, SEQ, dm), 1.0)
    c = nrm((BATCH, dm), 1.0)
    ctx = nrm((BATCH, CTX_LEN, dm), 1.0)
    c_ctx = nrm((dm,), 1.0)
    mod_w = nrm((DEPTH, dm, 6 * dm), dm ** -0.5)
    mod_b = nrm((DEPTH, 6 * dm), 0.02)
    ln_g = 1.0 + nrm((DEPTH, 2, dm), 0.02)
    ln_b = nrm((DEPTH, 2, dm), 0.02)

    s5_a_re = -0.5 + nrm((na, 2, S5_GROUPS, S5_STATE), 0.01)
    s5_a_im = jnp.pi * jnp.arange(S5_STATE, dtype=f32) + nrm((na, 2, S5_GROUPS, S5_STATE), 0.01)
    s5_log_dt = unif((na, 2, S5_GROUPS), math.log(S5_DT_MIN), math.log(S5_DT_MAX))
    s5_b_re = nrm((na, 2, S5_GROUPS, S5_STATE, S5_GROUP), (2 * S5_GROUP) ** -0.5)
    s5_b_im = nrm((na, 2, S5_GROUPS, S5_STATE, S5_GROUP), (2 * S5_GROUP) ** -0.5)
    s5_c_re = nrm((na, 2, S5_GROUPS, S5_GROUP, S5_STATE), S5_STATE ** -0.5)
    s5_c_im = nrm((na, 2, S5_GROUPS, S5_GROUP, S5_STATE), S5_STATE ** -0.5)
    s5_d = nrm((na, dm), 1.0)
    s5_w_gate = nrm((na, dm, dm), dm ** -0.5)
    s5_w_val = nrm((na, dm, dm), dm ** -0.5 * DEEPNORM_BETA)

    ssd_w_in = nrm((nb, dm, SSD_IN_DIM), dm ** -0.5)
    ssd_conv_w = nrm((nb, SSD_CONV, SSD_CONV_DIM), SSD_CONV ** -0.5)
    ssd_conv_b = nrm((nb, SSD_CONV_DIM), 0.02)
    dt0 = jnp.exp(unif((nb, 2, SSD_HEADS), math.log(1e-3), math.log(1e-1)))
    ssd_dt_bias = dt0 + jnp.log(-jnp.expm1(-dt0))
    ssd_a_log = jnp.log(unif((nb, 2, SSD_HEADS), 1.0, 16.0))
    ssd_d = 1.0 + nrm((nb, SSD_HEADS), 0.1)
    ssd_norm_w = 1.0 + nrm((nb, SSD_D_INNER), 0.02)
    ssd_w_out = nrm((nb, SSD_D_INNER, dm), SSD_D_INNER ** -0.5 * DEEPNORM_BETA)

    mla_w_down = nrm((nc, dm, MLA_Q_RANK + MLA_KV_RANK + MLA_ROPE), dm ** -0.5)
    mla_q_norm = 1.0 + nrm((nc, MLA_Q_RANK), 0.02)
    mla_kv_norm = 1.0 + nrm((nc, MLA_KV_RANK), 0.02)
    mla_w_uq = nrm((nc, MLA_Q_RANK, MLA_HEADS, MLA_NOPE + MLA_ROPE), MLA_Q_RANK ** -0.5)
    mla_w_uk = nrm((nc, MLA_KV_RANK, MLA_HEADS, MLA_NOPE), MLA_KV_RANK ** -0.5)
    mla_w_uv = nrm((nc, MLA_KV_RANK, MLA_HEADS, MLA_V), MLA_KV_RANK ** -0.5)
    mla_w_o = nrm((nc, MLA_HEADS * MLA_V, dm), (MLA_HEADS * MLA_V) ** -0.5 * DEEPNORM_BETA)

    peer_w_q = nrm((DEPTH, dm, PEER_HEADS, PEER_QDIM), dm ** -0.5)
    peer_subkeys = nrm((DEPTH, PEER_HEADS, 2, PEER_KEYS, PEER_QDIM // 2), (PEER_QDIM // 2) ** -0.5)
    peer_u = nrm((DEPTH, PEER_EXPERTS, dm), dm ** -0.5)
    peer_v = nrm((DEPTH, PEER_EXPERTS, dm), DEEPNORM_BETA)

    return {'x': x, 'c': c, 'ctx': ctx, 'c_ctx': c_ctx,
            'mod_w': mod_w, 'mod_b': mod_b, 'ln_g': ln_g, 'ln_b': ln_b,
            's5_a_re': s5_a_re, 's5_a_im': s5_a_im, 's5_log_dt': s5_log_dt,
            's5_b_re': s5_b_re, 's5_b_im': s5_b_im, 's5_c_re': s5_c_re, 's5_c_im': s5_c_im,
            's5_d': s5_d, 's5_w_gate': s5_w_gate, 's5_w_val': s5_w_val,
            'ssd_w_in': ssd_w_in, 'ssd_conv_w': ssd_conv_w, 'ssd_conv_b': ssd_conv_b,
            'ssd_dt_bias': ssd_dt_bias, 'ssd_a_log': ssd_a_log, 'ssd_d': ssd_d,
            'ssd_norm_w': ssd_norm_w, 'ssd_w_out': ssd_w_out,
            'mla_w_down': mla_w_down, 'mla_q_norm': mla_q_norm, 'mla_kv_norm': mla_kv_norm,
            'mla_w_uq': mla_w_uq, 'mla_w_uk': mla_w_uk, 'mla_w_uv': mla_w_uv, 'mla_w_o': mla_w_o,
            'peer_w_q': peer_w_q, 'peer_subkeys': peer_subkeys, 'peer_u': peer_u, 'peer_v': peer_v}


def reference(x, c, ctx, c_ctx, mod_w, mod_b, ln_g, ln_b,
              s5_a_re, s5_a_im, s5_log_dt, s5_b_re, s5_b_im, s5_c_re, s5_c_im, s5_d, s5_w_gate, s5_w_val,
              ssd_w_in, ssd_conv_w, ssd_conv_b, ssd_dt_bias, ssd_a_log, ssd_d, ssd_norm_w, ssd_w_out,
              mla_w_down, mla_q_norm, mla_kv_norm, mla_w_uq, mla_w_uk, mla_w_uv, mla_w_o,
              peer_w_q, peer_subkeys, peer_u, peer_v):
    ROWS = x.shape[1] // GRID_W
    pos = grid_positions(ROWS)
    ctx_len = ctx.shape[1]
    c_act = jax.nn.silu(c)
    c_ctx_act = jax.nn.silu(c_ctx)
    for i in range(DEPTH):
        mod_l = jnp.split((c_act @ mod_w[i] + mod_b[i])[:, None, :], 6, axis=-1)
        mod_c = jnp.split(c_ctx_act @ mod_w[i] + mod_b[i], 6, axis=-1)
        h = modulate(x, mod_l[0], mod_l[1])
        hc = modulate(ctx, mod_c[0], mod_c[1])
        kind, j = i % N_MIXERS, i // N_MIXERS
        if kind == 0:
            out_l, out_c = s5_mixer(h, hc, s5_a_re[j], s5_a_im[j], s5_log_dt[j], s5_b_re[j], s5_b_im[j],
                                    s5_c_re[j], s5_c_im[j], s5_d[j], s5_w_gate[j], s5_w_val[j])
        elif kind == 1:
            out_l, out_c = ssd_mixer(h, hc, ssd_w_in[j], ssd_conv_w[j], ssd_conv_b[j], ssd_dt_bias[j],
                                     ssd_a_log[j], ssd_d[j], ssd_norm_w[j], ssd_w_out[j])
        else:
            out_l, out_c = mla_mixer(h, hc, pos, mla_w_down[j], mla_q_norm[j], mla_kv_norm[j],
                                     mla_w_uq[j], mla_w_uk[j], mla_w_uv[j], mla_w_o[j])
        x = layer_norm(DEEPNORM_ALPHA * x + mod_l[2] * out_l, ln_g[i, 0], ln_b[i, 0])
        h = modulate(x, mod_l[3], mod_l[4])
        if i == DEPTH - 1:
            ffn = peer_ffn(h, peer_w_q[i], peer_subkeys[i], peer_u[i], peer_v[i])
            x = layer_norm(DEEPNORM_ALPHA * x + mod_l[5] * ffn, ln_g[i, 1], ln_b[i, 1])
        else:
            ctx = layer_norm(DEEPNORM_ALPHA * ctx + mod_c[2] * out_c, ln_g[i, 0], ln_b[i, 0])
            hc = modulate(ctx, mod_c[3], mod_c[4])
            ffn = peer_ffn(jnp.concatenate([hc, h], axis=1), peer_w_q[i], peer_subkeys[i], peer_u[i], peer_v[i])
            ctx = layer_norm(DEEPNORM_ALPHA * ctx + mod_c[5] * ffn[:, :ctx_len], ln_g[i, 1], ln_b[i, 1])
            x = layer_norm(DEEPNORM_ALPHA * x + mod_l[5] * ffn[:, ctx_len:], ln_g[i, 1], ln_b[i, 1])
    return x
```

```python
import math
import functools
import jax
import jax.numpy as jnp
from jax import lax
import numpy as np
from jax.experimental import pallas as pl
from jax.experimental.pallas import tpu as pltpu

D_MODEL = 1024
BATCH = 4
SEQ = 4096
DEPTH = 4

GRID_W = 64
CTX_LEN = 256
N_MIXERS = 3
DEEPNORM_ALPHA = (2.0 * DEPTH) ** 0.25
LN_EPS = 1e-5
RMS_EPS = 1e-6
ROPE_BASE = 10000.0

S5_GROUP = 16
S5_GROUPS = D_MODEL // S5_GROUP
S5_STATE = 64

SSD_D_INNER = 2 * D_MODEL
SSD_HEADDIM = 64
SSD_HEADS = SSD_D_INNER // SSD_HEADDIM
SSD_GROUPS = 4
SSD_STATE = 128
SSD_CONV = 3
SSD_CHUNK = 128
SSD_CONV_DIM = SSD_D_INNER + 2 * SSD_GROUPS * SSD_STATE
SSD_IN_DIM = SSD_D_INNER + SSD_CONV_DIM + 2 * SSD_HEADS

MLA_HEADS = 16
MLA_Q_RANK = 256
MLA_KV_RANK = 128
MLA_NOPE = 64
MLA_ROPE = 32
MLA_V = 64
MLA_BLOCK = 128

PEER_HEADS = 8
PEER_KEYS = 128
PEER_EXPERTS = PEER_KEYS * PEER_KEYS
PEER_QDIM = 256
PEER_TOPK = 16
PEER_BLOCK = 128


def layer_norm(x, g, b):
    xf = x.astype(jnp.float32)
    mu = jnp.mean(xf, axis=-1, keepdims=True)
    var = jnp.mean(jnp.square(xf - mu), axis=-1, keepdims=True)
    return ((xf - mu) * lax.rsqrt(var + LN_EPS)).astype(x.dtype) * g + b


def _ln_kernel(x_ref, g_ref, b_ref, o_ref):
    xf = x_ref[...]
    mu = jnp.mean(xf, axis=-1, keepdims=True)
    xc = xf - mu
    var = jnp.mean(xc * xc, axis=-1, keepdims=True)
    o_ref[...] = xc * lax.rsqrt(var + LN_EPS) * g_ref[...] + b_ref[...]


def layer_norm_pallas(x, g, b):
    shp = x.shape
    x2 = x.reshape(-1, shp[-1])
    n, d = x2.shape
    tb = 512
    out = pl.pallas_call(
        _ln_kernel,
        grid=(n // tb,),
        in_specs=[pl.BlockSpec((tb, d), lambda i: (i, 0)),
                  pl.BlockSpec((1, d), lambda i: (0, 0)),
                  pl.BlockSpec((1, d), lambda i: (0, 0))],
        out_specs=pl.BlockSpec((tb, d), lambda i: (i, 0)),
        out_shape=jax.ShapeDtypeStruct((n, d), jnp.float32),
        name="final_ln",
    )(x2, g.reshape(1, d), b.reshape(1, d))
    return out.reshape(shp)


def rms_norm(x, g):
    xf = x.astype(jnp.float32)
    return (xf * lax.rsqrt(jnp.mean(jnp.square(xf), axis=-1, keepdims=True) + RMS_EPS)).astype(x.dtype) * g


def modulate(x, shift, scale):
    return x * (1.0 + scale) + shift


def grid_positions(rows):
    row = jnp.repeat(jnp.arange(rows, dtype=jnp.float32), GRID_W)
    col = jnp.tile(jnp.arange(GRID_W, dtype=jnp.float32), rows)
    return row, col


def rope_axial(x, row, col):
    half = x.shape[-1] // 2
    quarter = half // 2
    freqs = ROPE_BASE ** (-jnp.arange(quarter, dtype=jnp.float32) / quarter)

    def rot(xp, pos):
        ang = pos[:, None] * freqs
        cos = jnp.cos(ang)[None, :, None, :].astype(x.dtype)
        sin = jnp.sin(ang)[None, :, None, :].astype(x.dtype)
        x1, x2 = xp[..., :quarter], xp[..., quarter:]
        return jnp.concatenate([x1 * cos - x2 * sin, x2 * cos + x1 * sin], axis=-1)

    return jnp.concatenate([rot(x[..., :half], row), rot(x[..., half:], col)], axis=-1)


def _lin_rec(left, right):
    a1, b1 = left
    a2, b2 = right
    return a1 * a2, a2 * b1 + b2


def s5_scan(u, a_bar, b_bar, c_mat, init, reverse):
    bu = jnp.einsum('lgc,gpc->lgp', u.astype(jnp.float32), b_bar)
    a = jnp.broadcast_to(a_bar, bu.shape)
    a_cum, s = lax.associative_scan(_lin_rec, (a, bu), reverse=reverse, axis=0)
    s = s + a_cum * init
    y = jnp.einsum('lgp,gcp->lgc', s, c_mat).real
    final = s[0] if reverse else s[-1]
    return y, final


def s5_direction(u_ctx, u_lat, a_re, a_im, log_dt, b_re, b_im, c_re, c_im, reverse):
    lam = lax.complex(a_re.astype(jnp.float32), a_im.astype(jnp.float32))
    a_bar = jnp.exp(lam * jnp.exp(log_dt.astype(jnp.float32))[:, None])
    b_mat = lax.complex(b_re.astype(jnp.float32), b_im.astype(jnp.float32))
    b_bar = ((a_bar - 1.0) / lam)[..., None] * b_mat
    c_mat = lax.complex(c_re.astype(jnp.float32), c_im.astype(jnp.float32))

    def per_sample(args):
        uc, ul = args
        yc, sc = s5_scan(uc, a_bar, b_bar, c_mat, jnp.zeros_like(a_bar), reverse)
        yl, _ = s5_scan(ul, a_bar, b_bar, c_mat, sc, reverse)
        return yc, yl

    return lax.map(per_sample, (u_ctx, u_lat))


def s5_mixer(h, hc, a_re, a_im, log_dt, b_re, b_im, c_re, c_im, d, w_gate, w_val):
    bsz, seq_len, _ = h.shape
    ctx_len = hc.shape[1]
    u = h.reshape(bsz, seq_len, S5_GROUPS, S5_GROUP)
    uc = hc.reshape(bsz, ctx_len, S5_GROUPS, S5_GROUP)
    y_l = d * h
    y_c = d * hc
    for direction in range(2):
        yc, yl = s5_direction(uc, u, a_re[direction], a_im[direction], log_dt[direction],
                              b_re[direction], b_im[direction], c_re[direction], c_im[direction],
                              direction == 1)
        y_l = y_l + yl.reshape(bsz, seq_len, D_MODEL).astype(h.dtype)
        y_c = y_c + yc.reshape(bsz, ctx_len, D_MODEL).astype(h.dtype)

    def glu(y):
        g = jax.nn.gelu(y)
        return (g @ w_val) * jax.nn.sigmoid(g @ w_gate)

    return glu(y_l), glu(y_c)


def depthwise_conv_centred(x, w, b):
    k_w = w.shape[0]
    pad = k_w // 2
    seq_len = x.shape[1]
    xp = jnp.pad(x, ((0, 0), (pad, pad), (0, 0)))
    out = b
    for k in range(k_w):
        out = out + xp[:, k:k + seq_len] * w[k]
    return out


def ssd_scan(x, dt, a, bm, cm, init):
    bsz, seq_len, n_heads, p_dim = x.shape
    n_grp, n_st = bm.shape[2], bm.shape[3]
    hg = n_heads // n_grp
    q_len = SSD_CHUNK
    nc = seq_len // q_len
    xc = x.reshape(bsz, nc, q_len, n_grp, hg, p_dim)
    dtc = dt.reshape(bsz, nc, q_len, n_grp, hg).astype(jnp.float32)
    bc = bm.reshape(bsz, nc, q_len, n_grp, n_st)
    cc = cm.reshape(bsz, nc, q_len, n_grp, n_st)
    a_cum = jnp.cumsum(dtc * a.reshape(n_grp, hg), axis=2)
    xdt = xc * dtc[..., None].astype(x.dtype)
    tri = jnp.tril(jnp.ones((q_len, q_len), dtype=bool))
    seg = a_cum[:, :, :, None] - a_cum[:, :, None]
    decay = jnp.exp(jnp.where(tri[:, :, None, None], seg, -jnp.inf)).astype(x.dtype)
    cb = jnp.einsum('bcqgn,bcsgn->bcgqs', cc, bc)
    y_diag = jnp.einsum('bcgqs,bcqsgh,bcsghp->bcqghp', cb, decay, xdt)
    decay_states = jnp.exp(a_cum[:, :, -1:] - a_cum).astype(x.dtype)
    states = jnp.einsum('bcsgn,bcsgh,bcsghp->bcghpn', bc, decay_states, xdt)
    chunk_decay = jnp.exp(a_cum[:, :, -1]).astype(x.dtype)

    def step(carry, inp):
        dec, st = inp
        return carry * dec[..., None, None] + st, carry

    final, prev = lax.scan(step, init, (jnp.moveaxis(chunk_decay, 1, 0), jnp.moveaxis(states, 1, 0)))
    y_off = jnp.einsum('bcqgn,cbghpn,bcqgh->bcqghp', cc, prev, jnp.exp(a_cum).astype(x.dtype))
    return (y_diag + y_off).reshape(bsz, seq_len, n_heads, p_dim), final


def ssd_mixer(h, hc, w_in, conv_w, conv_b, dt_bias, a_log, d, norm_w, w_out):
    a = -jnp.exp(a_log.astype(jnp.float32))

    def flip(t):
        return jnp.flip(t, axis=1)

    def bidir(t, inits):
        bsz, seq_len = t.shape[0], t.shape[1]
        z, xbc, dt = jnp.split(t @ w_in, [SSD_D_INNER, SSD_D_INNER + SSD_CONV_DIM], axis=-1)
        xbc = jax.nn.silu(depthwise_conv_centred(xbc, conv_w, conv_b))
        xs, bm, cm = jnp.split(xbc, [SSD_D_INNER, SSD_D_INNER + SSD_GROUPS * SSD_STATE], axis=-1)
        xs = xs.reshape(bsz, seq_len, SSD_HEADS, SSD_HEADDIM)
        bm = bm.reshape(bsz, seq_len, SSD_GROUPS, SSD_STATE)
        cm = cm.reshape(bsz, seq_len, SSD_GROUPS, SSD_STATE)
        dt = jax.nn.softplus(dt.reshape(bsz, seq_len, 2, SSD_HEADS) + dt_bias)
        y_f, s_f = ssd_scan(xs, dt[:, :, 0], a[0], bm, cm, inits[0])
        y_b, s_b = ssd_scan(flip(xs), flip(dt[:, :, 1]), a[1], flip(bm), flip(cm), inits[1])
        y = y_f + flip(y_b) + d[:, None] * xs
        y = rms_norm(y.reshape(z.shape) * jax.nn.silu(z), norm_w)
        return y @ w_out, (s_f, s_b)

    zero = jnp.zeros((h.shape[0], SSD_GROUPS, SSD_HEADS // SSD_GROUPS, SSD_HEADDIM, SSD_STATE), h.dtype)
    out_c, states_c = bidir(hc, (zero, zero))
    out_l, _ = bidir(h, states_c)
    return out_l, out_c


def mla_project(t, w_down, q_norm, kv_norm, w_uq, w_uk, w_uv, pos):
    cq, ckv, kr = jnp.split(t @ w_down, [MLA_Q_RANK, MLA_Q_RANK + MLA_KV_RANK], axis=-1)
    q = jnp.einsum('btr,rhd->bthd', rms_norm(cq, q_norm), w_uq)
    ckv = rms_norm(ckv, kv_norm)
    k_nope = jnp.einsum('btr,rhd->bthd', ckv, w_uk)
    v = jnp.einsum('btr,rhd->bthd', ckv, w_uv)
    q_nope, q_rope = q[..., :MLA_NOPE], q[..., MLA_NOPE:]
    kr = kr[:, :, None, :]
    if pos is not None:
        q_rope = rope_axial(q_rope, pos[0], pos[1])
        kr = rope_axial(kr, pos[0], pos[1])
    k = jnp.concatenate([k_nope, jnp.broadcast_to(kr, k_nope.shape[:3] + (MLA_ROPE,))], axis=-1)
    q = jnp.concatenate([q_nope, q_rope], axis=-1)
    return q, k, v


def attend(q, k, v):
    s = jnp.einsum('bqhd,bkhd->bhqk', q, k).astype(jnp.float32) * (MLA_NOPE + MLA_ROPE) ** -0.5
    p = jax.nn.softmax(s, axis=-1).astype(v.dtype)
    return jnp.einsum('bhqk,bkhd->bqhd', p, v)


def mla_mixer(h, hc, pos, w_down, q_norm, kv_norm, w_uq, w_uk, w_uv, w_o):
    bsz, seq_len, _ = h.shape
    qc, kc, vc = mla_project(hc, w_down, q_norm, kv_norm, w_uq, w_uk, w_uv, None)
    ql, kl, vl = mla_project(h, w_down, q_norm, kv_norm, w_uq, w_uk, w_uv, pos)
    out_c = attend(qc, kc, vc).reshape(bsz, hc.shape[1], MLA_HEADS * MLA_V)
    k_all = jnp.concatenate([kc, kl], axis=1)
    v_all = jnp.concatenate([vc, vl], axis=1)
    nb = seq_len // MLA_BLOCK
    qb = jnp.moveaxis(ql.reshape(bsz, nb, MLA_BLOCK, MLA_HEADS, MLA_NOPE + MLA_ROPE), 1, 0)
    out_l = lax.map(lambda qblk: attend(qblk, k_all, v_all), qb)
    out_l = jnp.moveaxis(out_l, 0, 1).reshape(bsz, seq_len, MLA_HEADS * MLA_V)
    return out_l @ w_o, out_c @ w_o


def peer_ffn(h, w_q, subkeys, u_tab, v_tab):
    bsz, seq_len, dm = h.shape
    half = PEER_QDIM // 2
    q = jnp.einsum('btd,dhk->bthk', h, w_q)
    s1 = jnp.einsum('bthk,hnk->bthn', q[..., :half], subkeys[:, 0])
    s2 = jnp.einsum('bthk,hnk->bthn', q[..., half:], subkeys[:, 1])
    v1, i1 = lax.top_k(s1, PEER_TOPK)
    v2, i2 = lax.top_k(s2, PEER_TOPK)
    n_cand = PEER_TOPK * PEER_TOPK
    cand = (v1[..., :, None] + v2[..., None, :]).reshape(bsz, seq_len, PEER_HEADS, n_cand)
    cidx = (i1[..., :, None] * PEER_KEYS + i2[..., None, :]).reshape(bsz, seq_len, PEER_HEADS, n_cand)
    best, sel = lax.top_k(cand, PEER_TOPK)
    eidx = jnp.take_along_axis(cidx, sel, axis=-1)
    gate = jax.nn.softmax(best.astype(jnp.float32), axis=-1).astype(h.dtype)
    n_blk = (bsz * seq_len) // PEER_BLOCK
    n_sel = PEER_HEADS * PEER_TOPK
    hb = h.reshape(n_blk, PEER_BLOCK, dm)
    ib = eidx.reshape(n_blk, PEER_BLOCK, n_sel)
    gb = gate.reshape(n_blk, PEER_BLOCK, n_sel)

    def block(args):
        hk, ik, gk = args
        u = jnp.take(u_tab, ik, axis=0)
        act = jax.nn.gelu(jnp.einsum('td,ted->te', hk, u))
        v = jnp.take(v_tab, ik, axis=0)
        return jnp.einsum('te,ted->td', gk * act, v)

    return lax.map(block, (hb, ib, gb)).reshape(bsz, seq_len, dm)


def kernel(x, c, ctx, c_ctx, mod_w, mod_b, ln_g, ln_b,
           s5_a_re, s5_a_im, s5_log_dt, s5_b_re, s5_b_im, s5_c_re, s5_c_im, s5_d, s5_w_gate, s5_w_val,
           ssd_w_in, ssd_conv_w, ssd_conv_b, ssd_dt_bias, ssd_a_log, ssd_d, ssd_norm_w, ssd_w_out,
           mla_w_down, mla_q_norm, mla_kv_norm, mla_w_uq, mla_w_uk, mla_w_uv, mla_w_o,
           peer_w_q, peer_subkeys, peer_u, peer_v):
    ROWS = x.shape[1] // GRID_W
    pos = grid_positions(ROWS)
    ctx_len = ctx.shape[1]
    c_act = jax.nn.silu(c)
    c_ctx_act = jax.nn.silu(c_ctx)
    for i in range(DEPTH):
        mod_l = jnp.split((c_act @ mod_w[i] + mod_b[i])[:, None, :], 6, axis=-1)
        mod_c = jnp.split(c_ctx_act @ mod_w[i] + mod_b[i], 6, axis=-1)
        h = modulate(x, mod_l[0], mod_l[1])
        hc = modulate(ctx, mod_c[0], mod_c[1])
        kind, j = i % N_MIXERS, i // N_MIXERS
        if kind == 0:
            out_l, out_c = s5_mixer(h, hc, s5_a_re[j], s5_a_im[j], s5_log_dt[j], s5_b_re[j], s5_b_im[j],
                                    s5_c_re[j], s5_c_im[j], s5_d[j], s5_w_gate[j], s5_w_val[j])
        elif kind == 1:
            out_l, out_c = ssd_mixer(h, hc, ssd_w_in[j], ssd_conv_w[j], ssd_conv_b[j], ssd_dt_bias[j],
                                     ssd_a_log[j], ssd_d[j], ssd_norm_w[j], ssd_w_out[j])
        else:
            out_l, out_c = mla_mixer(h, hc, pos, mla_w_down[j], mla_q_norm[j], mla_kv_norm[j],
                                     mla_w_uq[j], mla_w_uk[j], mla_w_uv[j], mla_w_o[j])
        x = layer_norm(DEEPNORM_ALPHA * x + mod_l[2] * out_l, ln_g[i, 0], ln_b[i, 0])
        h = modulate(x, mod_l[3], mod_l[4])
        if i == DEPTH - 1:
            ffn = peer_ffn(h, peer_w_q[i], peer_subkeys[i], peer_u[i], peer_v[i])
            x = layer_norm_pallas(DEEPNORM_ALPHA * x + mod_l[5] * ffn, ln_g[i, 1], ln_b[i, 1])
        else:
            ctx = layer_norm(DEEPNORM_ALPHA * ctx + mod_c[2] * out_c, ln_g[i, 0], ln_b[i, 0])
            hc = modulate(ctx, mod_c[3], mod_c[4])
            ffn = peer_ffn(jnp.concatenate([hc, h], axis=1), peer_w_q[i], peer_subkeys[i], peer_u[i], peer_v[i])
            ctx = layer_norm(DEEPNORM_ALPHA * ctx + mod_c[5] * ffn[:, :ctx_len], ln_g[i, 1], ln_b[i, 1])
            x = layer_norm(DEEPNORM_ALPHA * x + mod_l[5] * ffn[:, ctx_len:], ln_g[i, 1], ln_b[i, 1])
    return x
```

```python
import math
import functools
import jax
import jax.numpy as jnp
from jax import lax
import numpy as np
from jax.experimental import pallas as pl
from jax.experimental.pallas import tpu as pltpu

D_MODEL = 1024
BATCH = 4
SEQ = 4096
DEPTH = 4

GRID_W = 64
CTX_LEN = 256
N_MIXERS = 3
DEEPNORM_ALPHA = (2.0 * DEPTH) ** 0.25
LN_EPS = 1e-5
RMS_EPS = 1e-6
ROPE_BASE = 10000.0

S5_GROUP = 16
S5_GROUPS = D_MODEL // S5_GROUP
S5_STATE = 64

SSD_D_INNER = 2 * D_MODEL
SSD_HEADDIM = 64
SSD_HEADS = SSD_D_INNER // SSD_HEADDIM
SSD_GROUPS = 4
SSD_STATE = 128
SSD_CONV = 3
SSD_CHUNK = 128
SSD_CONV_DIM = SSD_D_INNER + 2 * SSD_GROUPS * SSD_STATE
SSD_IN_DIM = SSD_D_INNER + SSD_CONV_DIM + 2 * SSD_HEADS

MLA_HEADS = 16
MLA_Q_RANK = 256
MLA_KV_RANK = 128
MLA_NOPE = 64
MLA_ROPE = 32
MLA_V = 64
MLA_BLOCK = 128

PEER_HEADS = 8
PEER_KEYS = 128
PEER_EXPERTS = PEER_KEYS * PEER_KEYS
PEER_QDIM = 256
PEER_TOPK = 16
PEER_BLOCK = 128


def layer_norm(x, g, b):
    xf = x.astype(jnp.float32)
    mu = jnp.mean(xf, axis=-1, keepdims=True)
    var = jnp.mean(jnp.square(xf - mu), axis=-1, keepdims=True)
    return ((xf - mu) * lax.rsqrt(var + LN_EPS)).astype(x.dtype) * g + b


def _ln_kernel(x_ref, g_ref, b_ref, o_ref):
    xf = x_ref[...]
    mu = jnp.mean(xf, axis=-1, keepdims=True)
    xc = xf - mu
    var = jnp.mean(xc * xc, axis=-1, keepdims=True)
    o_ref[...] = xc * lax.rsqrt(var + LN_EPS) * g_ref[...] + b_ref[...]


def layer_norm_pallas(x, g, b):
    shp = x.shape
    x2 = x.reshape(-1, shp[-1])
    n, d = x2.shape
    tb = 512
    out = pl.pallas_call(
        _ln_kernel,
        grid=(n // tb,),
        in_specs=[pl.BlockSpec((tb, d), lambda i: (i, 0)),
                  pl.BlockSpec((1, d), lambda i: (0, 0)),
                  pl.BlockSpec((1, d), lambda i: (0, 0))],
        out_specs=pl.BlockSpec((tb, d), lambda i: (i, 0)),
        out_shape=jax.ShapeDtypeStruct((n, d), jnp.float32),
        name="final_ln",
    )(x2, g.reshape(1, d), b.reshape(1, d))
    return out.reshape(shp)


def rms_norm(x, g):
    xf = x.astype(jnp.float32)
    return (xf * lax.rsqrt(jnp.mean(jnp.square(xf), axis=-1, keepdims=True) + RMS_EPS)).astype(x.dtype) * g


def modulate(x, shift, scale):
    return x * (1.0 + scale) + shift


def grid_positions(rows):
    row = jnp.repeat(jnp.arange(rows, dtype=jnp.float32), GRID_W)
    col = jnp.tile(jnp.arange(GRID_W, dtype=jnp.float32), rows)
    return row, col


def rope_axial(x, row, col):
    half = x.shape[-1] // 2
    quarter = half // 2
    freqs = ROPE_BASE ** (-jnp.arange(quarter, dtype=jnp.float32) / quarter)

    def rot(xp, pos):
        ang = pos[:, None] * freqs
        cos = jnp.cos(ang)[None, :, None, :].astype(x.dtype)
        sin = jnp.sin(ang)[None, :, None, :].astype(x.dtype)
        x1, x2 = xp[..., :quarter], xp[..., quarter:]
        return jnp.concatenate([x1 * cos - x2 * sin, x2 * cos + x1 * sin], axis=-1)

    return jnp.concatenate([rot(x[..., :half], row), rot(x[..., half:], col)], axis=-1)


def _lin_rec(left, right):
    a1, b1 = left
    a2, b2 = right
    return a1 * a2, a2 * b1 + b2


def s5_scan(u, a_bar, b_bar, c_mat, init, reverse):
    bu = jnp.einsum('lgc,gpc->lgp', u.astype(jnp.float32), b_bar)
    a = jnp.broadcast_to(a_bar, bu.shape)
    a_cum, s = lax.associative_scan(_lin_rec, (a, bu), reverse=reverse, axis=0)
    s = s + a_cum * init
    y = jnp.einsum('lgp,gcp->lgc', s, c_mat).real
    final = s[0] if reverse else s[-1]
    return y, final


def s5_direction(u_ctx, u_lat, a_re, a_im, log_dt, b_re, b_im, c_re, c_im, reverse):
    lam = lax.complex(a_re.astype(jnp.float32), a_im.astype(jnp.float32))
    a_bar = jnp.exp(lam * jnp.exp(log_dt.astype(jnp.float32))[:, None])
    b_mat = lax.complex(b_re.astype(jnp.float32), b_im.astype(jnp.float32))
    b_bar = ((a_bar - 1.0) / lam)[..., None] * b_mat
    c_mat = lax.complex(c_re.astype(jnp.float32), c_im.astype(jnp.float32))

    def per_sample(args):
        uc, ul = args
        yc, sc = s5_scan(uc, a_bar, b_bar, c_mat, jnp.zeros_like(a_bar), reverse)
        yl, _ = s5_scan(ul, a_bar, b_bar, c_mat, sc, reverse)
        return yc, yl

    return lax.map(per_sample, (u_ctx, u_lat))


def s5_mixer(h, hc, a_re, a_im, log_dt, b_re, b_im, c_re, c_im, d, w_gate, w_val):
    bsz, seq_len, _ = h.shape
    ctx_len = hc.shape[1]
    u = h.reshape(bsz, seq_len, S5_GROUPS, S5_GROUP)
    uc = hc.reshape(bsz, ctx_len, S5_GROUPS, S5_GROUP)
    y_l = d * h
    y_c = d * hc
    for direction in range(2):
        yc, yl = s5_direction(uc, u, a_re[direction], a_im[direction], log_dt[direction],
                              b_re[direction], b_im[direction], c_re[direction], c_im[direction],
                              direction == 1)
        y_l = y_l + yl.reshape(bsz, seq_len, D_MODEL).astype(h.dtype)
        y_c = y_c + yc.reshape(bsz, ctx_len, D_MODEL).astype(h.dtype)

    def glu(y):
        g = jax.nn.gelu(y)
        return (g @ w_val) * jax.nn.sigmoid(g @ w_gate)

    return glu(y_l), glu(y_c)


def depthwise_conv_centred(x, w, b):
    k_w = w.shape[0]
    pad = k_w // 2
    seq_len = x.shape[1]
    xp = jnp.pad(x, ((0, 0), (pad, pad), (0, 0)))
    out = b
    for k in range(k_w):
        out = out + xp[:, k:k + seq_len] * w[k]
    return out


def ssd_scan(x, dt, a, bm, cm, init):
    bsz, seq_len, n_heads, p_dim = x.shape
    n_grp, n_st = bm.shape[2], bm.shape[3]
    hg = n_heads // n_grp
    q_len = SSD_CHUNK
    nc = seq_len // q_len
    xc = x.reshape(bsz, nc, q_len, n_grp, hg, p_dim)
    dtc = dt.reshape(bsz, nc, q_len, n_grp, hg).astype(jnp.float32)
    bc = bm.reshape(bsz, nc, q_len, n_grp, n_st)
    cc = cm.reshape(bsz, nc, q_len, n_grp, n_st)
    a_cum = jnp.cumsum(dtc * a.reshape(n_grp, hg), axis=2)
    xdt = xc * dtc[..., None].astype(x.dtype)
    tri = jnp.tril(jnp.ones((q_len, q_len), dtype=bool))
    seg = a_cum[:, :, :, None] - a_cum[:, :, None]
    decay = jnp.exp(jnp.where(tri[:, :, None, None], seg, -jnp.inf)).astype(x.dtype)
    cb = jnp.einsum('bcqgn,bcsgn->bcgqs', cc, bc)
    y_diag = jnp.einsum('bcgqs,bcqsgh,bcsghp->bcqghp', cb, decay, xdt)
    decay_states = jnp.exp(a_cum[:, :, -1:] - a_cum).astype(x.dtype)
    states = jnp.einsum('bcsgn,bcsgh,bcsghp->bcghpn', bc, decay_states, xdt)
    chunk_decay = jnp.exp(a_cum[:, :, -1]).astype(x.dtype)

    def step(carry, inp):
        dec, st = inp
        return carry * dec[..., None, None] + st, carry

    final, prev = lax.scan(step, init, (jnp.moveaxis(chunk_decay, 1, 0), jnp.moveaxis(states, 1, 0)))
    y_off = jnp.einsum('bcqgn,cbghpn,bcqgh->bcqghp', cc, prev, jnp.exp(a_cum).astype(x.dtype))
    return (y_diag + y_off).reshape(bsz, seq_len, n_heads, p_dim), final


def ssd_mixer(h, hc, w_in, conv_w, conv_b, dt_bias, a_log, d, norm_w, w_out):
    a = -jnp.exp(a_log.astype(jnp.float32))

    def flip(t):
        return jnp.flip(t, axis=1)

    def bidir(t, inits):
        bsz, seq_len = t.shape[0], t.shape[1]
        z, xbc, dt = jnp.split(t @ w_in, [SSD_D_INNER, SSD_D_INNER + SSD_CONV_DIM], axis=-1)
        xbc = jax.nn.silu(depthwise_conv_centred(xbc, conv_w, conv_b))
        xs, bm, cm = jnp.split(xbc, [SSD_D_INNER, SSD_D_INNER + SSD_GROUPS * SSD_STATE], axis=-1)
        xs = xs.reshape(bsz, seq_len, SSD_HEADS, SSD_HEADDIM)
        bm = bm.reshape(bsz, seq_len, SSD_GROUPS, SSD_STATE)
        cm = cm.reshape(bsz, seq_len, SSD_GROUPS, SSD_STATE)
        dt = jax.nn.softplus(dt.reshape(bsz, seq_len, 2, SSD_HEADS) + dt_bias)
        y_f, s_f = ssd_scan(xs, dt[:, :, 0], a[0], bm, cm, inits[0])
        y_b, s_b = ssd_scan(flip(xs), flip(dt[:, :, 1]), a[1], flip(bm), flip(cm), inits[1])
        y = y_f + flip(y_b) + d[:, None] * xs
        y = rms_norm(y.reshape(z.shape) * jax.nn.silu(z), norm_w)
        return y @ w_out, (s_f, s_b)

    zero = jnp.zeros((h.shape[0], SSD_GROUPS, SSD_HEADS // SSD_GROUPS, SSD_HEADDIM, SSD_STATE), h.dtype)
    out_c, states_c = bidir(hc, (zero, zero))
    out_l, _ = bidir(h, states_c)
    return out_l, out_c


def mla_project(t, w_down, q_norm, kv_norm, w_uq, w_uk, w_uv, pos):
    cq, ckv, kr = jnp.split(t @ w_down, [MLA_Q_RANK, MLA_Q_RANK + MLA_KV_RANK], axis=-1)
    q = jnp.einsum('btr,rhd->bthd', rms_norm(cq, q_norm), w_uq)
    ckv = rms_norm(ckv, kv_norm)
    k_nope = jnp.einsum('btr,rhd->bthd', ckv, w_uk)
    v = jnp.einsum('btr,rhd->bthd', ckv, w_uv)
    q_nope, q_rope = q[..., :MLA_NOPE], q[..., MLA_NOPE:]
    kr = kr[:, :, None, :]
    if pos is not None:
        q_rope = rope_axial(q_rope, pos[0], pos[1])
        kr = rope_axial(kr, pos[0], pos[1])
    k = jnp.concatenate([k_nope, jnp.broadcast_to(kr, k_nope.shape[:3] + (MLA_ROPE,))], axis=-1)
    q = jnp.concatenate([q_nope, q_rope], axis=-1)
    return q, k, v


def attend(q, k, v):
    s = jnp.einsum('bqhd,bkhd->bhqk', q, k).astype(jnp.float32) * (MLA_NOPE + MLA_ROPE) ** -0.5
    p = jax.nn.softmax(s, axis=-1).astype(v.dtype)
    return jnp.einsum('bhqk,bkhd->bqhd', p, v)


def mla_mixer(h, hc, pos, w_down, q_norm, kv_norm, w_uq, w_uk, w_uv, w_o):
    bsz, seq_len, _ = h.shape
    qc, kc, vc = mla_project(hc, w_down, q_norm, kv_norm, w_uq, w_uk, w_uv, None)
    ql, kl, vl = mla_project(h, w_down, q_norm, kv_norm, w_uq, w_uk, w_uv, pos)
    out_c = attend(qc, kc, vc).reshape(bsz, hc.shape[1], MLA_HEADS * MLA_V)
    k_all = jnp.concatenate([kc, kl], axis=1)
    v_all = jnp.concatenate([vc, vl], axis=1)
    nb = seq_len // MLA_BLOCK
    qb = jnp.moveaxis(ql.reshape(bsz, nb, MLA_BLOCK, MLA_HEADS, MLA_NOPE + MLA_ROPE), 1, 0)
    out_l = lax.map(lambda qblk: attend(qblk, k_all, v_all), qb)
    out_l = jnp.moveaxis(out_l, 0, 1).reshape(bsz, seq_len, MLA_HEADS * MLA_V)
    return out_l @ w_o, out_c @ w_o


def peer_ffn(h, w_q, subkeys, u_tab, v_tab):
    bsz, seq_len, dm = h.shape
    half = PEER_QDIM // 2
    q = jnp.einsum('btd,dhk->bthk', h, w_q)
    s1 = jnp.einsum('bthk,hnk->bthn', q[..., :half], subkeys[:, 0])
    s2 = jnp.einsum('bthk,hnk->bthn', q[..., half:], subkeys[:, 1])
    v1, i1 = lax.top_k(s1, PEER_TOPK)
    v2, i2 = lax.top_k(s2, PEER_TOPK)
    n_cand = PEER_TOPK * PEER_TOPK
    cand = (v1[..., :, None] + v2[..., None, :]).reshape(bsz, seq_len, PEER_HEADS, n_cand)
    cidx = (i1[..., :, None] * PEER_KEYS + i2[..., None, :]).reshape(bsz, seq_len, PEER_HEADS, n_cand)
    best, sel = lax.top_k(cand, PEER_TOPK)
    eidx = jnp.take_along_axis(cidx, sel, axis=-1)
    gate = jax.nn.softmax(best.astype(jnp.float32), axis=-1).astype(h.dtype)
    n_blk = (bsz * seq_len) // PEER_BLOCK
    n_sel = PEER_HEADS * PEER_TOPK
    hb = h.reshape(n_blk, PEER_BLOCK, dm)
    ib = eidx.reshape(n_blk, PEER_BLOCK, n_sel)
    gb = gate.reshape(n_blk, PEER_BLOCK, n_sel)

    def block(args):
        hk, ik, gk = args
        u = jnp.take(u_tab, ik, axis=0)
        act = jax.nn.gelu(jnp.einsum('td,ted->te', hk, u))
        v = jnp.take(v_tab, ik, axis=0)
        return jnp.einsum('te,ted->td', gk * act, v)

    return lax.map(block, (hb, ib, gb)).reshape(bsz, seq_len, dm)


PEER_TM = 512
PEER_EB = 1024
PEER_LANES = 128
VMEM_LIMIT = 56 * 1024 * 1024
NEG_INF = float("-inf")


def _extract_top(s, n_take):
    n = s.shape[0]
    row = lax.broadcasted_iota(jnp.int32, s.shape, 0)
    vals = []
    for k in range(n_take):
        m = jnp.max(s, axis=0, keepdims=True)
        vals.append(m)
        if k + 1 < n_take:
            first = jnp.min(jnp.where(s == m, row, n), axis=0, keepdims=True)
            s = jnp.where(row == first, NEG_INF, s)
    return vals


def _top16_sorted(s):
    n = PEER_TOPK
    x = [s[8 * r:8 * r + 8, :] for r in range(n)]

    def exchange(i, l, descending):
        hi, lo = jnp.maximum(x[i], x[l]), jnp.minimum(x[i], x[l])
        x[i], x[l] = (hi, lo) if descending else (lo, hi)

    def merge_bitonic():
        j = n // 2
        while j >= 1:
            for i in range(n):
                if i ^ j > i:
                    exchange(i, i ^ j, True)
            j //= 2

    k = 2
    while k < n:
        j = k // 2
        while j >= 1:
            for i in range(n):
                if i ^ j > i:
                    exchange(i, i ^ j, (i & k) == 0)
            j //= 2
        k *= 2
    merge_bitonic()
    for shift in (4, 2, 1):
        other = [pltpu.roll(x[n - 1 - r], shift, axis=0) for r in range(n)]
        for r in range(n):
            x[r] = jnp.maximum(x[r], other[r])
        merge_bitonic()
    return [x[r][0:1, :] for r in range(n)]


def _dot3(a_hi, a_lo, b_hi, b_lo):
    f32 = jnp.float32
    return (jnp.dot(a_hi, b_hi, preferred_element_type=f32)
            + jnp.dot(a_hi, b_lo, preferred_element_type=f32)
            + jnp.dot(a_lo, b_hi, preferred_element_type=f32))


def _split_bf16(v):
    hi = v.astype(jnp.bfloat16)
    lo = (v - hi.astype(jnp.float32)).astype(jnp.bfloat16)
    return hi, lo


def _peer_kernel(x_ref, mod_ref, wqh_ref, wql_ref, sk_ref, u_ref, vt_ref, lng_ref, lnb_ref,
                 o_ref,
                 ht_ref, s1_ref, e1_ref, s2_ref, e2_ref, tau_ref, v1_ref, v2_ref, cand_ref,
                 act_ref, gact_ref, acc_ref):
    f32 = jnp.float32
    bf16 = jnp.bfloat16
    eb = pl.program_id(1)
    n_eb = pl.num_programs(1)
    dm = x_ref.shape[1]
    a_per = PEER_EB // PEER_KEYS
    half = PEER_QDIM // 2

    @pl.when(eb == 0)
    def _prologue():
        mod = mod_ref[0]
        h = x_ref[...] * (1.0 + mod[:, 4 * dm:5 * dm]) + mod[:, 3 * dm:4 * dm]
        ht = h.T
        ht_hi, ht_lo = _split_bf16(ht)
        ht_ref[...] = ht_hi
        qt = _dot3(wqh_ref[...], wql_ref[...], ht_hi, ht_lo)
        cand_ref[...] = jnp.full(cand_ref.shape, NEG_INF, f32)
        for hd in range(PEER_HEADS):
            q1 = qt[hd * PEER_QDIM: hd * PEER_QDIM + half]
            q2 = qt[hd * PEER_QDIM + half: (hd + 1) * PEER_QDIM]
            s1 = jnp.dot(sk_ref[hd, 0], q1, precision=lax.Precision.HIGHEST, preferred_element_type=f32)
            s2 = jnp.dot(sk_ref[hd, 1], q2, precision=lax.Precision.HIGHEST, preferred_element_type=f32)
            top1 = _top16_sorted(s1)
            top2 = _top16_sorted(s2)
            for k in range(PEER_TOPK):
                v1_ref[k:k + 1, :] = top1[k]
                v2_ref[k:k + 1, :] = top2[k]
            off = 0
            for i in range(PEER_TOPK):
                cnt = PEER_TOPK // (i + 1)
                cand_ref[off:off + cnt, :] = v1_ref[i:i + 1, :] + v2_ref[0:cnt, :]
                off += cnt
            best = _extract_top(cand_ref[...], PEER_TOPK)
            z = jnp.zeros_like(best[0])
            for k in range(PEER_TOPK):
                z = z + jnp.exp(best[k] - best[0])
            s1_ref[hd] = s1
            s2_ref[hd] = s2
            e1_ref[hd] = jnp.exp(s1 - top1[0])
            e2_ref[hd] = jnp.exp(s2 - top2[0]) / z
            tau_ref[hd:hd + 1, :] = best[PEER_TOPK - 1]
        acc_ref[...] = jnp.zeros(acc_ref.shape, f32)

    act_ref[...] = jnp.dot(u_ref[...], ht_ref[...], preferred_element_type=f32)
    a_base = pl.multiple_of(eb * a_per, a_per)
    blk = (PEER_KEYS, PEER_LANES)
    for lg in range(0, x_ref.shape[0], PEER_LANES):
        lanes = slice(lg, lg + PEER_LANES)
        for a in range(a_per):
            g = None
            for hd in range(PEER_HEADS):
                s1a = jnp.broadcast_to(s1_ref[hd, pl.ds(a_base, a_per), lanes][a:a + 1, :], blk)
                e1a = jnp.broadcast_to(e1_ref[hd, pl.ds(a_base, a_per), lanes][a:a + 1, :], blk)
                tau = jnp.broadcast_to(tau_ref[hd:hd + 1, lanes], blk)
                t = s2_ref[hd, :, lanes] + s1a
                w = jnp.where(t >= tau, e2_ref[hd, :, lanes], 0.0) * e1a
                g = w if g is None else g + w
            rows = slice(a * PEER_KEYS, (a + 1) * PEER_KEYS)
            gact_ref[rows, lanes] = (g * jax.nn.gelu(act_ref[rows, lanes])).astype(bf16)
    acc_ref[...] += jnp.dot(vt_ref[...], gact_ref[...], preferred_element_type=f32)

    @pl.when(eb == n_eb - 1)
    def _epilogue():
        mod = mod_ref[0]
        y = DEEPNORM_ALPHA * x_ref[...] + mod[:, 5 * dm:6 * dm] * acc_ref[...].T
        mu = jnp.mean(y, axis=-1, keepdims=True)
        yc = y - mu
        var = jnp.mean(yc * yc, axis=-1, keepdims=True)
        o_ref[...] = yc * lax.rsqrt(var + LN_EPS) * lng_ref[...] + lnb_ref[...]


def _mod_row_index(i, tm, n_lat, seq, bsz):
    return jnp.where(i * tm < n_lat, (i * tm) // seq, bsz)


def peer_ln_pallas(xa, mod3, n_lat, seq, bsz, w_q, subkeys, u_tab, v_tab, ln_g, ln_b):
    n_tok, dm = xa.shape
    tm = PEER_TM
    f32, bf16 = jnp.float32, jnp.bfloat16
    wq_t = w_q.reshape(dm, PEER_HEADS * PEER_QDIM).T
    wq_hi = wq_t.astype(bf16)
    wq_lo = (wq_t - wq_hi.astype(f32)).astype(bf16)
    u_bf = u_tab.astype(bf16)
    vt_bf = v_tab.T.astype(bf16)
    n_eb = PEER_EXPERTS // PEER_EB
    qd = PEER_HEADS * PEER_QDIM
    n_cand = sum(PEER_TOPK // (i + 1) for i in range(PEER_TOPK))
    n_cand_pad = -(-n_cand // 8) * 8
    mod_idx = functools.partial(_mod_row_index, tm=tm, n_lat=n_lat, seq=seq, bsz=bsz)
    return pl.pallas_call(
        _peer_kernel,
        grid=(n_tok // tm, n_eb),
        in_specs=[
            pl.BlockSpec((tm, dm), lambda i, e: (i, 0)),
            pl.BlockSpec((1, 1, 6 * dm), lambda i, e: (mod_idx(i), 0, 0)),
            pl.BlockSpec((qd, dm), lambda i, e: (0, 0)),
            pl.BlockSpec((qd, dm), lambda i, e: (0, 0)),
            pl.BlockSpec((PEER_HEADS, 2, PEER_KEYS, PEER_QDIM // 2), lambda i, e: (0, 0, 0, 0)),
            pl.BlockSpec((PEER_EB, dm), lambda i, e: (e, 0)),
            pl.BlockSpec((dm, PEER_EB), lambda i, e: (0, e)),
            pl.BlockSpec((1, dm), lambda i, e: (0, 0)),
            pl.BlockSpec((1, dm), lambda i, e: (0, 0)),
        ],
        out_specs=pl.BlockSpec((tm, dm), lambda i, e: (i, 0)),
        out_shape=jax.ShapeDtypeStruct((n_tok, dm), f32),
        scratch_shapes=[
            pltpu.VMEM((dm, tm), bf16),
            pltpu.VMEM((PEER_HEADS, PEER_KEYS, tm), f32),
            pltpu.VMEM((PEER_HEADS, PEER_KEYS, tm), f32),
            pltpu.VMEM((PEER_HEADS, PEER_KEYS, tm), f32),
            pltpu.VMEM((PEER_HEADS, PEER_KEYS, tm), f32),
            pltpu.VMEM((PEER_HEADS, tm), f32),
            pltpu.VMEM((PEER_TOPK, tm), f32),
            pltpu.VMEM((PEER_TOPK, tm), f32),
            pltpu.VMEM((n_cand_pad, tm), f32),
            pltpu.VMEM((PEER_EB, tm), f32),
            pltpu.VMEM((PEER_EB, tm), bf16),
            pltpu.VMEM((dm, tm), f32),
        ],
        compiler_params=pltpu.CompilerParams(
            dimension_semantics=("arbitrary", "arbitrary"),
            vmem_limit_bytes=VMEM_LIMIT),
        name="peer_ln",
    )(xa, mod3, wq_hi, wq_lo, subkeys, u_bf, vt_bf, ln_g.reshape(1, dm), ln_b.reshape(1, dm))


S5_CHUNK = 16
S5_LEVELS = 9


def _s5_prep(a_re, a_im, log_dt, b_re, b_im, c_re, c_im, d):
    f32, bf16 = jnp.float32, jnp.bfloat16
    hp = lax.Precision.HIGHEST
    n_g, n_p, q = S5_GROUPS, S5_STATE, S5_CHUNK
    dt = jnp.exp(log_dt.astype(f32))[..., None]

    def apow(n):
        mag = jnp.exp(a_re * dt * n)
        ang = a_im * dt * n
        return mag * jnp.cos(ang), mag * jnp.sin(ang)

    ar1, ai1 = apow(1.0)
    den = a_re * a_re + a_im * a_im
    nr, ni = ar1 - 1.0, ai1
    cr = (nr * a_re + ni * a_im) / den
    ci = (ni * a_re - nr * a_im) / den
    bbr = cr[..., None] * b_re - ci[..., None] * b_im
    bbi = cr[..., None] * b_im + ci[..., None] * b_re
    lag = jnp.arange(q + 1, dtype=f32)[:, None, None, None]
    pr, pi = apow(lag)
    mr = pr[..., None] * bbr - pi[..., None] * bbi
    mi = pr[..., None] * bbi + pi[..., None] * bbr
    kmat = (jnp.einsum('dgcp,ndgpk->ndgck', c_re, mr, precision=hp)
            - jnp.einsum('dgcp,ndgpk->ndgck', c_im, mi, precision=hp))
    r_idx = jnp.arange(q)[:, None]
    t_idx = jnp.arange(q)[None, :]

    def toeplitz(kd, lagm):
        blk = kd[jnp.clip(lagm, 0, q)]
        blk = jnp.where((lagm >= 0)[:, :, None, None, None], blk, 0.0)
        return blk.transpose(2, 0, 4, 1, 3).reshape(n_g, q * S5_GROUP, q * S5_GROUP)

    eye = jnp.eye(q * S5_GROUP, dtype=f32)
    dvec = jnp.tile(d.reshape(n_g, 1, S5_GROUP), (1, q, 1)).reshape(n_g, q * S5_GROUP)
    tsum = toeplitz(kmat[:, 0], t_idx - r_idx) + toeplitz(kmat[:, 1], r_idx - t_idx) + eye[None] * dvec[:, None, :]

    def w_in(direction, exps):
        wr = mr[exps, direction].transpose(1, 0, 3, 2).reshape(n_g, q * S5_GROUP, n_p)
        wi = mi[exps, direction].transpose(1, 0, 3, 2).reshape(n_g, q * S5_GROUP, n_p)
        return wr, wi

    def w_out(direction, exps):
        pre, pim = pr[exps, direction], pi[exps, direction]
        cre, cim = c_re[direction], c_im[direction]
        wre = cre[None] * pre[:, :, None, :] - cim[None] * pim[:, :, None, :]
        wim = -(cre[None] * pim[:, :, None, :] + cim[None] * pre[:, :, None, :])
        return (wre.transpose(1, 3, 0, 2).reshape(n_g, n_p, q * S5_GROUP),
                wim.transpose(1, 3, 0, 2).reshape(n_g, n_p, q * S5_GROUP))

    steps = jnp.arange(q)
    win = [w_in(0, q - 1 - steps), w_in(1, steps)]
    wout = [w_out(0, steps + 1), w_out(1, q - steps)]
    zc = jnp.zeros((n_g, q * S5_GROUP, n_p), f32)
    zr = jnp.zeros((n_g, n_p, q * S5_GROUP), f32)
    win_p, wout_p = [], []
    for direction in range(2):
        wr, wi = win[direction]
        even = jnp.concatenate([wr, zc, wi, zc], axis=2)
        odd = jnp.concatenate([zc, wr, zc, wi], axis=2)
        is_odd = (jnp.arange(n_g) % 2 == 1)[:, None, None]
        win_p.append(jnp.where(is_odd, odd, even))
        vr, vi = wout[direction]
        even = jnp.concatenate([vr, zr, vi, zr], axis=1)
        odd = jnp.concatenate([zr, vr, zr, vi], axis=1)
        wout_p.append(jnp.where(is_odd, odd, even))
    win_p = jnp.stack(win_p, axis=1).reshape(n_g // 2, 2, 2, 4 * n_p, 4 * n_p).transpose(0, 2, 1, 3, 4)
    wout_p = jnp.stack(wout_p, axis=1).reshape(n_g // 2, 2, 2, 4 * n_p, 4 * n_p).transpose(0, 2, 1, 3, 4)
    lvl = (q * 2.0 ** jnp.arange(S5_LEVELS, dtype=f32))[:, None, None, None]
    lr, li = apow(lvl)
    pw = jnp.stack([lr, li], axis=1)
    pw = pw.transpose(3, 2, 0, 1, 4).reshape(n_g // 2, 2, 2, S5_LEVELS, 2, n_p)
    pw = pw.transpose(0, 2, 3, 4, 1, 5).reshape(n_g // 2, 2, 2 * S5_LEVELS, 2 * n_p)
    return tsum.astype(bf16).reshape(n_g // 2, 2, q * S5_GROUP, q * S5_GROUP), win_p.astype(bf16), wout_p.astype(bf16), pw


def _shift_rows(v, r, up):
    n = v.shape[0]
    row = lax.broadcasted_iota(jnp.int32, v.shape, 0)
    if up:
        return jnp.where(row < n - r, pltpu.roll(v, n - r, axis=0), 0.0)
    return jnp.where(row >= r, pltpu.roll(v, r, axis=0), 0.0)


def _chunk_scan(re, im, pw, rows_per_chunk, up):
    n_chunks = re.shape[0] // rows_per_chunk
    level, s = 0, 1
    while s < n_chunks:
        ar, ai = pw[2 * level:2 * level + 1, :], pw[2 * level + 1:2 * level + 2, :]
        sre = _shift_rows(re, s * rows_per_chunk, up)
        sim = _shift_rows(im, s * rows_per_chunk, up)
        re, im = re + ar * sre - ai * sim, im + ar * sim + ai * sre
        level, s = level + 1, 2 * s
    return re, im


def _s5_kernel(u_ref, scl_ref, shl_ref, scc_ref, shc_ref, t_ref, win_ref, wout_ref, pw_ref, y_ref, *, bsz, ctx_chunks):
    f32, bf16 = jnp.float32, jnp.bfloat16
    n = u_ref.shape[1]
    width = u_ref.shape[2]
    rc = ctx_chunks * bsz
    hs = []
    for gi in range(2):
        xv = u_ref[gi].reshape(n // 8, 8, width)
        hl = xv * scl_ref[gi][None] + shl_ref[gi][None]
        hc = xv * scc_ref[gi][None] + shc_ref[gi][None]
        slab = lax.broadcasted_iota(jnp.int32, xv.shape, 0)
        hs.append(jnp.where(slab < rc // 8, hc, hl).reshape(n, width).astype(bf16))
    y = [jnp.dot(hs[gi], t_ref[0, gi], preferred_element_type=f32) for gi in range(2)]
    half = width // 2
    for direction in range(2):
        sloc = (jnp.dot(hs[0], win_ref[0, direction, 0], preferred_element_type=f32)
                + jnp.dot(hs[1], win_ref[0, direction, 1], preferred_element_type=f32))
        re, im = sloc[:, :half], sloc[:, half:]
        pw = pw_ref[0, direction]
        if direction == 0:
            sre, sim = _chunk_scan(_shift_rows(re, bsz, False), _shift_rows(im, bsz, False), pw, bsz, False)
        else:
            cre, cim = _chunk_scan(_shift_rows(re[:rc], bsz, True), _shift_rows(im[:rc], bsz, True), pw, bsz, True)
            ar, ai = pw[0:1, :], pw[1:2, :]
            fre = ar * cre[0:8] - ai * cim[0:8] + re[0:8]
            fim = ar * cim[0:8] + ai * cre[0:8] + im[0:8]
            row8 = lax.broadcasted_iota(jnp.int32, fre.shape, 0)
            tre = jnp.where(row8 >= 8 - bsz, pltpu.roll(fre, 8 - bsz, axis=0), 0.0)
            tim = jnp.where(row8 >= 8 - bsz, pltpu.roll(fim, 8 - bsz, axis=0), 0.0)
            lre, lim = _shift_rows(re[rc:], bsz, True), _shift_rows(im[rc:], bsz, True)
            lre = jnp.concatenate([lre[:-8], lre[-8:] + tre], axis=0)
            lim = jnp.concatenate([lim[:-8], lim[-8:] + tim], axis=0)
            lre, lim = _chunk_scan(lre, lim, pw, bsz, True)
            sre = jnp.concatenate([cre, lre], axis=0)
            sim = jnp.concatenate([cim, lim], axis=0)
        s_in = jnp.concatenate([sre, sim], axis=1).astype(bf16)
        for gi in range(2):
            y[gi] = y[gi] + jnp.dot(s_in, wout_ref[0, direction, gi], preferred_element_type=f32)
    for gi in range(2):
        y_ref[gi] = y[gi]


def s5_ssm_pallas(xa, mod_tab, n_lat, seq, bsz, prep):
    tsum, win_p, wout_p, pw = prep
    f32 = jnp.float32
    n_tok, dm = xa.shape
    ctx_len = (n_tok - n_lat) // bsz
    q = S5_CHUNK
    n_chunks = (seq + ctx_len) // q
    width = q * S5_GROUP
    full = jnp.concatenate([xa[n_lat:].reshape(bsz, ctx_len, dm), xa[:n_lat].reshape(bsz, seq, dm)], axis=1)
    u = full.reshape(bsz, n_chunks, q, S5_GROUPS, S5_GROUP).transpose(3, 1, 0, 2, 4).reshape(S5_GROUPS, n_chunks * bsz, width)

    def tile(vec_rows):
        rows = vec_rows.shape[0]
        t = vec_rows.reshape(rows, S5_GROUPS, 1, S5_GROUP)
        t = jnp.broadcast_to(t, (rows, S5_GROUPS, q, S5_GROUP)).reshape(rows, S5_GROUPS, width)
        return jnp.tile(t.transpose(1, 0, 2), (1, 8 // rows, 1))

    scl, shl = tile(1.0 + mod_tab[:bsz, dm:2 * dm]), tile(mod_tab[:bsz, 0:dm])
    scc = tile(jnp.broadcast_to(1.0 + mod_tab[bsz:bsz + 1, dm:2 * dm], (bsz, dm)))
    shc = tile(jnp.broadcast_to(mod_tab[bsz:bsz + 1, 0:dm], (bsz, dm)))
    n_rows = n_chunks * bsz
    tile_spec = pl.BlockSpec((2, 8, width), lambda p: (p, 0, 0))
    y = pl.pallas_call(
        functools.partial(_s5_kernel, bsz=bsz, ctx_chunks=ctx_len // q),
        grid=(S5_GROUPS // 2,),
        in_specs=[
            pl.BlockSpec((2, n_rows, width), lambda p: (p, 0, 0)),
            tile_spec, tile_spec, tile_spec, tile_spec,
            pl.BlockSpec((1, 2, width, width), lambda p: (p, 0, 0, 0)),
            pl.BlockSpec((1, 2, 2, width, width), lambda p: (p, 0, 0, 0, 0)),
            pl.BlockSpec((1, 2, 2, width, width), lambda p: (p, 0, 0, 0, 0)),
            pl.BlockSpec((1, 2, 2 * S5_LEVELS, 2 * S5_STATE), lambda p: (p, 0, 0, 0)),
        ],
        out_specs=pl.BlockSpec((2, n_rows, width), lambda p: (p, 0, 0)),
        out_shape=jax.ShapeDtypeStruct((S5_GROUPS, n_rows, width), f32),
        compiler_params=pltpu.CompilerParams(dimension_semantics=("arbitrary",), vmem_limit_bytes=VMEM_LIMIT),
        name="s5_ssm",
    )(u, scl, shl, scc, shc, tsum, win_p, wout_p, pw)
    yf = y.reshape(S5_GROUPS, n_chunks, bsz, q, S5_GROUP).transpose(2, 1, 3, 0, 4).reshape(bsz, seq + ctx_len, dm)
    return jnp.concatenate([yf[:, ctx_len:].reshape(n_lat, dm), yf[:, :ctx_len].reshape(bsz * ctx_len, dm)], axis=0)


def _glu_ln_kernel(y_ref, x_ref, mod_ref, wv_ref, wg_ref, lng_ref, lnb_ref, o_ref):
    f32 = jnp.float32
    dm = x_ref.shape[1]
    g = jax.nn.gelu(y_ref[...]).astype(jnp.bfloat16)
    val = jnp.dot(g, wv_ref[...], preferred_element_type=f32)
    gate = jnp.dot(g, wg_ref[...], preferred_element_type=f32)
    out = val * jax.nn.sigmoid(gate)
    mod = mod_ref[0]
    z = DEEPNORM_ALPHA * x_ref[...] + mod[:, 2 * dm:3 * dm] * out
    mu = jnp.mean(z, axis=-1, keepdims=True)
    zc = z - mu
    var = jnp.mean(zc * zc, axis=-1, keepdims=True)
    o_ref[...] = zc * lax.rsqrt(var + LN_EPS) * lng_ref[...] + lnb_ref[...]


TOK_TM = 512


def glu_ln_pallas(ya, xa, mod3, n_lat, seq, bsz, w_val, w_gate, ln_g, ln_b):
    n_tok, dm = ya.shape
    tm = TOK_TM
    bf16 = jnp.bfloat16
    mod_idx = functools.partial(_mod_row_index, tm=tm, n_lat=n_lat, seq=seq, bsz=bsz)
    return pl.pallas_call(
        _glu_ln_kernel,
        grid=(n_tok // tm,),
        in_specs=[
            pl.BlockSpec((tm, dm), lambda i: (i, 0)),
            pl.BlockSpec((tm, dm), lambda i: (i, 0)),
            pl.BlockSpec((1, 1, 6 * dm), lambda i: (mod_idx(i), 0, 0)),
            pl.BlockSpec((dm, dm), lambda i: (0, 0)),
            pl.BlockSpec((dm, dm), lambda i: (0, 0)),
            pl.BlockSpec((1, dm), lambda i: (0, 0)),
            pl.BlockSpec((1, dm), lambda i: (0, 0)),
        ],
        out_specs=pl.BlockSpec((tm, dm), lambda i: (i, 0)),
        out_shape=jax.ShapeDtypeStruct((n_tok, dm), jnp.float32),
        compiler_params=pltpu.CompilerParams(dimension_semantics=("arbitrary",), vmem_limit_bytes=VMEM_LIMIT),
        name="glu_ln",
    )(ya, xa, mod3, w_val.astype(bf16), w_gate.astype(bf16), ln_g.reshape(1, dm), ln_b.reshape(1, dm))


def kernel(x, c, ctx, c_ctx, mod_w, mod_b, ln_g, ln_b,
           s5_a_re, s5_a_im, s5_log_dt, s5_b_re, s5_b_im, s5_c_re, s5_c_im, s5_d, s5_w_gate, s5_w_val,
           ssd_w_in, ssd_conv_w, ssd_conv_b, ssd_dt_bias, ssd_a_log, ssd_d, ssd_norm_w, ssd_w_out,
           mla_w_down, mla_q_norm, mla_kv_norm, mla_w_uq, mla_w_uk, mla_w_uv, mla_w_o,
           peer_w_q, peer_subkeys, peer_u, peer_v):
    ROWS = x.shape[1] // GRID_W
    pos = grid_positions(ROWS)
    ctx_len = ctx.shape[1]
    c_act = jax.nn.silu(c)
    c_ctx_act = jax.nn.silu(c_ctx)
    bsz, seq_len, dm = x.shape
    n_lat, n_ctx = bsz * seq_len, bsz * ctx_len
    xa = jnp.concatenate([x.reshape(n_lat, dm), ctx.reshape(n_ctx, dm)], axis=0)
    for i in range(DEPTH):
        last = i == DEPTH - 1
        mod_tab = jnp.concatenate([c_act, c_ctx_act[None], jnp.zeros((7 - bsz, dm), x.dtype)], axis=0) @ mod_w[i] + mod_b[i]
        mod3 = mod_tab.reshape(8, 1, 6 * dm)
        kind, j = i % N_MIXERS, i // N_MIXERS
        if kind == 0:
            prep = _s5_prep(s5_a_re[j], s5_a_im[j], s5_log_dt[j], s5_b_re[j], s5_b_im[j], s5_c_re[j], s5_c_im[j], s5_d[j])
            ya = s5_ssm_pallas(xa, mod_tab, n_lat, seq_len, bsz, prep)
            if last:
                ya, xa = ya[:n_lat], xa[:n_lat]
            xa = glu_ln_pallas(ya, xa, mod3, n_lat, seq_len, bsz, s5_w_val[j], s5_w_gate[j], ln_g[i, 0], ln_b[i, 0])
        else:
            x = xa[:n_lat].reshape(bsz, seq_len, dm)
            ctx = xa[n_lat:].reshape(bsz, ctx_len, dm)
            mod_l = jnp.split(mod_tab[:bsz][:, None, :], 6, axis=-1)
            mod_c = jnp.split(mod_tab[bsz], 6, axis=-1)
            h = modulate(x, mod_l[0], mod_l[1])
            hc = modulate(ctx, mod_c[0], mod_c[1])
            if kind == 1:
                out_l, out_c = ssd_mixer(h, hc, ssd_w_in[j], ssd_conv_w[j], ssd_conv_b[j], ssd_dt_bias[j],
                                         ssd_a_log[j], ssd_d[j], ssd_norm_w[j], ssd_w_out[j])
            else:
                out_l, out_c = mla_mixer(h, hc, pos, mla_w_down[j], mla_q_norm[j], mla_kv_norm[j],
                                         mla_w_uq[j], mla_w_uk[j], mla_w_uv[j], mla_w_o[j])
            x = layer_norm(DEEPNORM_ALPHA * x + mod_l[2] * out_l, ln_g[i, 0], ln_b[i, 0])
            ctx = layer_norm(DEEPNORM_ALPHA * ctx + mod_c[2] * out_c, ln_g[i, 0], ln_b[i, 0])
            xa = jnp.concatenate([x.reshape(n_lat, dm), ctx.reshape(n_ctx, dm)], axis=0)
            if last:
                xa = xa[:n_lat]
        xa = peer_ln_pallas(xa, mod3, n_lat, seq_len, bsz, peer_w_q[i], peer_subkeys[i], peer_u[i], peer_v[i],
                            ln_g[i, 1], ln_b[i, 1])
    return xa[:n_lat].reshape(bsz, seq_len, dm)
```

```python
import math
import functools
import jax
import jax.numpy as jnp
from jax import lax
import numpy as np
from jax.experimental import pallas as pl
from jax.experimental.pallas import tpu as pltpu

D_MODEL = 1024
BATCH = 4
SEQ = 4096
DEPTH = 4

GRID_W = 64
CTX_LEN = 256
N_MIXERS = 3
DEEPNORM_ALPHA = (2.0 * DEPTH) ** 0.25
LN_EPS = 1e-5
RMS_EPS = 1e-6
ROPE_BASE = 10000.0

S5_GROUP = 16
S5_GROUPS = D_MODEL // S5_GROUP
S5_STATE = 64

SSD_D_INNER = 2 * D_MODEL
SSD_HEADDIM = 64
SSD_HEADS = SSD_D_INNER // SSD_HEADDIM
SSD_GROUPS = 4
SSD_STATE = 128
SSD_CONV = 3
SSD_CHUNK = 128
SSD_CONV_DIM = SSD_D_INNER + 2 * SSD_GROUPS * SSD_STATE
SSD_IN_DIM = SSD_D_INNER + SSD_CONV_DIM + 2 * SSD_HEADS

MLA_HEADS = 16
MLA_Q_RANK = 256
MLA_KV_RANK = 128
MLA_NOPE = 64
MLA_ROPE = 32
MLA_V = 64
MLA_BLOCK = 128

PEER_HEADS = 8
PEER_KEYS = 128
PEER_EXPERTS = PEER_KEYS * PEER_KEYS
PEER_QDIM = 256
PEER_TOPK = 16
PEER_BLOCK = 128


def layer_norm(x, g, b):
    xf = x.astype(jnp.float32)
    mu = jnp.mean(xf, axis=-1, keepdims=True)
    var = jnp.mean(jnp.square(xf - mu), axis=-1, keepdims=True)
    return ((xf - mu) * lax.rsqrt(var + LN_EPS)).astype(x.dtype) * g + b


def _ln_kernel(x_ref, g_ref, b_ref, o_ref):
    xf = x_ref[...]
    mu = jnp.mean(xf, axis=-1, keepdims=True)
    xc = xf - mu
    var = jnp.mean(xc * xc, axis=-1, keepdims=True)
    o_ref[...] = xc * lax.rsqrt(var + LN_EPS) * g_ref[...] + b_ref[...]


def layer_norm_pallas(x, g, b):
    shp = x.shape
    x2 = x.reshape(-1, shp[-1])
    n, d = x2.shape
    tb = 512
    out = pl.pallas_call(
        _ln_kernel,
        grid=(n // tb,),
        in_specs=[pl.BlockSpec((tb, d), lambda i: (i, 0)),
                  pl.BlockSpec((1, d), lambda i: (0, 0)),
                  pl.BlockSpec((1, d), lambda i: (0, 0))],
        out_specs=pl.BlockSpec((tb, d), lambda i: (i, 0)),
        out_shape=jax.ShapeDtypeStruct((n, d), jnp.float32),
        name="final_ln",
    )(x2, g.reshape(1, d), b.reshape(1, d))
    return out.reshape(shp)


def rms_norm(x, g):
    xf = x.astype(jnp.float32)
    return (xf * lax.rsqrt(jnp.mean(jnp.square(xf), axis=-1, keepdims=True) + RMS_EPS)).astype(x.dtype) * g


def modulate(x, shift, scale):
    return x * (1.0 + scale) + shift


def grid_positions(rows):
    row = jnp.repeat(jnp.arange(rows, dtype=jnp.float32), GRID_W)
    col = jnp.tile(jnp.arange(GRID_W, dtype=jnp.float32), rows)
    return row, col


def rope_axial(x, row, col):
    half = x.shape[-1] // 2
    quarter = half // 2
    freqs = ROPE_BASE ** (-jnp.arange(quarter, dtype=jnp.float32) / quarter)

    def rot(xp, pos):
        ang = pos[:, None] * freqs
        cos = jnp.cos(ang)[None, :, None, :].astype(x.dtype)
        sin = jnp.sin(ang)[None, :, None, :].astype(x.dtype)
        x1, x2 = xp[..., :quarter], xp[..., quarter:]
        return jnp.concatenate([x1 * cos - x2 * sin, x2 * cos + x1 * sin], axis=-1)

    return jnp.concatenate([rot(x[..., :half], row), rot(x[..., half:], col)], axis=-1)


def _lin_rec(left, right):
    a1, b1 = left
    a2, b2 = right
    return a1 * a2, a2 * b1 + b2


def s5_scan(u, a_bar, b_bar, c_mat, init, reverse):
    bu = jnp.einsum('lgc,gpc->lgp', u.astype(jnp.float32), b_bar)
    a = jnp.broadcast_to(a_bar, bu.shape)
    a_cum, s = lax.associative_scan(_lin_rec, (a, bu), reverse=reverse, axis=0)
    s = s + a_cum * init
    y = jnp.einsum('lgp,gcp->lgc', s, c_mat).real
    final = s[0] if reverse else s[-1]
    return y, final


def s5_direction(u_ctx, u_lat, a_re, a_im, log_dt, b_re, b_im, c_re, c_im, reverse):
    lam = lax.complex(a_re.astype(jnp.float32), a_im.astype(jnp.float32))
    a_bar = jnp.exp(lam * jnp.exp(log_dt.astype(jnp.float32))[:, None])
    b_mat = lax.complex(b_re.astype(jnp.float32), b_im.astype(jnp.float32))
    b_bar = ((a_bar - 1.0) / lam)[..., None] * b_mat
    c_mat = lax.complex(c_re.astype(jnp.float32), c_im.astype(jnp.float32))

    def per_sample(args):
        uc, ul = args
        yc, sc = s5_scan(uc, a_bar, b_bar, c_mat, jnp.zeros_like(a_bar), reverse)
        yl, _ = s5_scan(ul, a_bar, b_bar, c_mat, sc, reverse)
        return yc, yl

    return lax.map(per_sample, (u_ctx, u_lat))


def s5_mixer(h, hc, a_re, a_im, log_dt, b_re, b_im, c_re, c_im, d, w_gate, w_val):
    bsz, seq_len, _ = h.shape
    ctx_len = hc.shape[1]
    u = h.reshape(bsz, seq_len, S5_GROUPS, S5_GROUP)
    uc = hc.reshape(bsz, ctx_len, S5_GROUPS, S5_GROUP)
    y_l = d * h
    y_c = d * hc
    for direction in range(2):
        yc, yl = s5_direction(uc, u, a_re[direction], a_im[direction], log_dt[direction],
                              b_re[direction], b_im[direction], c_re[direction], c_im[direction],
                              direction == 1)
        y_l = y_l + yl.reshape(bsz, seq_len, D_MODEL).astype(h.dtype)
        y_c = y_c + yc.reshape(bsz, ctx_len, D_MODEL).astype(h.dtype)

    def glu(y):
        g = jax.nn.gelu(y)
        return (g @ w_val) * jax.nn.sigmoid(g @ w_gate)

    return glu(y_l), glu(y_c)


def depthwise_conv_centred(x, w, b):
    k_w = w.shape[0]
    pad = k_w // 2
    seq_len = x.shape[1]
    xp = jnp.pad(x, ((0, 0), (pad, pad), (0, 0)))
    out = b
    for k in range(k_w):
        out = out + xp[:, k:k + seq_len] * w[k]
    return out


def ssd_scan(x, dt, a, bm, cm, init):
    bsz, seq_len, n_heads, p_dim = x.shape
    n_grp, n_st = bm.shape[2], bm.shape[3]
    hg = n_heads // n_grp
    q_len = SSD_CHUNK
    nc = seq_len // q_len
    xc = x.reshape(bsz, nc, q_len, n_grp, hg, p_dim)
    dtc = dt.reshape(bsz, nc, q_len, n_grp, hg).astype(jnp.float32)
    bc = bm.reshape(bsz, nc, q_len, n_grp, n_st)
    cc = cm.reshape(bsz, nc, q_len, n_grp, n_st)
    a_cum = jnp.cumsum(dtc * a.reshape(n_grp, hg), axis=2)
    xdt = xc * dtc[..., None].astype(x.dtype)
    tri = jnp.tril(jnp.ones((q_len, q_len), dtype=bool))
    seg = a_cum[:, :, :, None] - a_cum[:, :, None]
    decay = jnp.exp(jnp.where(tri[:, :, None, None], seg, -jnp.inf)).astype(x.dtype)
    cb = jnp.einsum('bcqgn,bcsgn->bcgqs', cc, bc)
    y_diag = jnp.einsum('bcgqs,bcqsgh,bcsghp->bcqghp', cb, decay, xdt)
    decay_states = jnp.exp(a_cum[:, :, -1:] - a_cum).astype(x.dtype)
    states = jnp.einsum('bcsgn,bcsgh,bcsghp->bcghpn', bc, decay_states, xdt)
    chunk_decay = jnp.exp(a_cum[:, :, -1]).astype(x.dtype)

    def step(carry, inp):
        dec, st = inp
        return carry * dec[..., None, None] + st, carry

    final, prev = lax.scan(step, init, (jnp.moveaxis(chunk_decay, 1, 0), jnp.moveaxis(states, 1, 0)))
    y_off = jnp.einsum('bcqgn,cbghpn,bcqgh->bcqghp', cc, prev, jnp.exp(a_cum).astype(x.dtype))
    return (y_diag + y_off).reshape(bsz, seq_len, n_heads, p_dim), final


def ssd_mixer(h, hc, w_in, conv_w, conv_b, dt_bias, a_log, d, norm_w, w_out):
    a = -jnp.exp(a_log.astype(jnp.float32))

    def flip(t):
        return jnp.flip(t, axis=1)

    def bidir(t, inits):
        bsz, seq_len = t.shape[0], t.shape[1]
        z, xbc, dt = jnp.split(t @ w_in, [SSD_D_INNER, SSD_D_INNER + SSD_CONV_DIM], axis=-1)
        xbc = jax.nn.silu(depthwise_conv_centred(xbc, conv_w, conv_b))
        xs, bm, cm = jnp.split(xbc, [SSD_D_INNER, SSD_D_INNER + SSD_GROUPS * SSD_STATE], axis=-1)
        xs = xs.reshape(bsz, seq_len, SSD_HEADS, SSD_HEADDIM)
        bm = bm.reshape(bsz, seq_len, SSD_GROUPS, SSD_STATE)
        cm = cm.reshape(bsz, seq_len, SSD_GROUPS, SSD_STATE)
        dt = jax.nn.softplus(dt.reshape(bsz, seq_len, 2, SSD_HEADS) + dt_bias)
        y_f, s_f = ssd_scan(xs, dt[:, :, 0], a[0], bm, cm, inits[0])
        y_b, s_b = ssd_scan(flip(xs), flip(dt[:, :, 1]), a[1], flip(bm), flip(cm), inits[1])
        y = y_f + flip(y_b) + d[:, None] * xs
        y = rms_norm(y.reshape(z.shape) * jax.nn.silu(z), norm_w)
        return y @ w_out, (s_f, s_b)

    zero = jnp.zeros((h.shape[0], SSD_GROUPS, SSD_HEADS // SSD_GROUPS, SSD_HEADDIM, SSD_STATE), h.dtype)
    out_c, states_c = bidir(hc, (zero, zero))
    out_l, _ = bidir(h, states_c)
    return out_l, out_c


def mla_project(t, w_down, q_norm, kv_norm, w_uq, w_uk, w_uv, pos):
    cq, ckv, kr = jnp.split(t @ w_down, [MLA_Q_RANK, MLA_Q_RANK + MLA_KV_RANK], axis=-1)
    q = jnp.einsum('btr,rhd->bthd', rms_norm(cq, q_norm), w_uq)
    ckv = rms_norm(ckv, kv_norm)
    k_nope = jnp.einsum('btr,rhd->bthd', ckv, w_uk)
    v = jnp.einsum('btr,rhd->bthd', ckv, w_uv)
    q_nope, q_rope = q[..., :MLA_NOPE], q[..., MLA_NOPE:]
    kr = kr[:, :, None, :]
    if pos is not None:
        q_rope = rope_axial(q_rope, pos[0], pos[1])
        kr = rope_axial(kr, pos[0], pos[1])
    k = jnp.concatenate([k_nope, jnp.broadcast_to(kr, k_nope.shape[:3] + (MLA_ROPE,))], axis=-1)
    q = jnp.concatenate([q_nope, q_rope], axis=-1)
    return q, k, v


def attend(q, k, v):
    s = jnp.einsum('bqhd,bkhd->bhqk', q, k).astype(jnp.float32) * (MLA_NOPE + MLA_ROPE) ** -0.5
    p = jax.nn.softmax(s, axis=-1).astype(v.dtype)
    return jnp.einsum('bhqk,bkhd->bqhd', p, v)


def mla_mixer(h, hc, pos, w_down, q_norm, kv_norm, w_uq, w_uk, w_uv, w_o):
    bsz, seq_len, _ = h.shape
    qc, kc, vc = mla_project(hc, w_down, q_norm, kv_norm, w_uq, w_uk, w_uv, None)
    ql, kl, vl = mla_project(h, w_down, q_norm, kv_norm, w_uq, w_uk, w_uv, pos)
    out_c = attend(qc, kc, vc).reshape(bsz, hc.shape[1], MLA_HEADS * MLA_V)
    k_all = jnp.concatenate([kc, kl], axis=1)
    v_all = jnp.concatenate([vc, vl], axis=1)
    nb = seq_len // MLA_BLOCK
    qb = jnp.moveaxis(ql.reshape(bsz, nb, MLA_BLOCK, MLA_HEADS, MLA_NOPE + MLA_ROPE), 1, 0)
    out_l = lax.map(lambda qblk: attend(qblk, k_all, v_all), qb)
    out_l = jnp.moveaxis(out_l, 0, 1).reshape(bsz, seq_len, MLA_HEADS * MLA_V)
    return out_l @ w_o, out_c @ w_o


def peer_ffn(h, w_q, subkeys, u_tab, v_tab):
    bsz, seq_len, dm = h.shape
    half = PEER_QDIM // 2
    q = jnp.einsum('btd,dhk->bthk', h, w_q)
    s1 = jnp.einsum('bthk,hnk->bthn', q[..., :half], subkeys[:, 0])
    s2 = jnp.einsum('bthk,hnk->bthn', q[..., half:], subkeys[:, 1])
    v1, i1 = lax.top_k(s1, PEER_TOPK)
    v2, i2 = lax.top_k(s2, PEER_TOPK)
    n_cand = PEER_TOPK * PEER_TOPK
    cand = (v1[..., :, None] + v2[..., None, :]).reshape(bsz, seq_len, PEER_HEADS, n_cand)
    cidx = (i1[..., :, None] * PEER_KEYS + i2[..., None, :]).reshape(bsz, seq_len, PEER_HEADS, n_cand)
    best, sel = lax.top_k(cand, PEER_TOPK)
    eidx = jnp.take_along_axis(cidx, sel, axis=-1)
    gate = jax.nn.softmax(best.astype(jnp.float32), axis=-1).astype(h.dtype)
    n_blk = (bsz * seq_len) // PEER_BLOCK
    n_sel = PEER_HEADS * PEER_TOPK
    hb = h.reshape(n_blk, PEER_BLOCK, dm)
    ib = eidx.reshape(n_blk, PEER_BLOCK, n_sel)
    gb = gate.reshape(n_blk, PEER_BLOCK, n_sel)

    def block(args):
        hk, ik, gk = args
        u = jnp.take(u_tab, ik, axis=0)
        act = jax.nn.gelu(jnp.einsum('td,ted->te', hk, u))
        v = jnp.take(v_tab, ik, axis=0)
        return jnp.einsum('te,ted->td', gk * act, v)

    return lax.map(block, (hb, ib, gb)).reshape(bsz, seq_len, dm)


PEER_TM = 512
PEER_EB = 1024
PEER_LANES = 128
VMEM_LIMIT = 56 * 1024 * 1024
NEG_INF = float("-inf")


def _extract_top(s, n_take):
    n = s.shape[0]
    row = lax.broadcasted_iota(jnp.int32, s.shape, 0)
    vals = []
    for k in range(n_take):
        m = jnp.max(s, axis=0, keepdims=True)
        vals.append(m)
        if k + 1 < n_take:
            first = jnp.min(jnp.where(s == m, row, n), axis=0, keepdims=True)
            s = jnp.where(row == first, NEG_INF, s)
    return vals


def _top16_sorted(s):
    n = PEER_TOPK
    x = [s[8 * r:8 * r + 8, :] for r in range(n)]

    def exchange(i, l, descending):
        hi, lo = jnp.maximum(x[i], x[l]), jnp.minimum(x[i], x[l])
        x[i], x[l] = (hi, lo) if descending else (lo, hi)

    def merge_bitonic():
        j = n // 2
        while j >= 1:
            for i in range(n):
                if i ^ j > i:
                    exchange(i, i ^ j, True)
            j //= 2

    k = 2
    while k < n:
        j = k // 2
        while j >= 1:
            for i in range(n):
                if i ^ j > i:
                    exchange(i, i ^ j, (i & k) == 0)
            j //= 2
        k *= 2
    merge_bitonic()
    for shift in (4, 2, 1):
        other = [pltpu.roll(x[n - 1 - r], shift, axis=0) for r in range(n)]
        for r in range(n):
            x[r] = jnp.maximum(x[r], other[r])
        merge_bitonic()
    return [x[r][0:1, :] for r in range(n)]


def _dot3(a_hi, a_lo, b_hi, b_lo):
    f32 = jnp.float32
    return (jnp.dot(a_hi, b_hi, preferred_element_type=f32)
            + jnp.dot(a_hi, b_lo, preferred_element_type=f32)
            + jnp.dot(a_lo, b_hi, preferred_element_type=f32))


def _split_bf16(v):
    hi = v.astype(jnp.bfloat16)
    lo = (v - hi.astype(jnp.float32)).astype(jnp.bfloat16)
    return hi, lo


def _peer_kernel(x_ref, mod_ref, wqh_ref, wql_ref, sk_ref, u_ref, vt_ref, lng_ref, lnb_ref,
                 o_ref,
                 ht_ref, s1_ref, e1_ref, s2_ref, e2_ref, tau_ref, v1_ref, v2_ref, cand_ref,
                 act_ref, gact_ref, acc_ref):
    f32 = jnp.float32
    bf16 = jnp.bfloat16
    eb = pl.program_id(1)
    n_eb = pl.num_programs(1)
    dm = x_ref.shape[1]
    a_per = PEER_EB // PEER_KEYS
    half = PEER_QDIM // 2

    @pl.when(eb == 0)
    def _prologue():
        mod = mod_ref[0]
        h = x_ref[...] * (1.0 + mod[:, 4 * dm:5 * dm]) + mod[:, 3 * dm:4 * dm]
        ht = h.T
        ht_hi, ht_lo = _split_bf16(ht)
        ht_ref[...] = ht_hi
        qt = _dot3(wqh_ref[...], wql_ref[...], ht_hi, ht_lo)
        cand_ref[...] = jnp.full(cand_ref.shape, NEG_INF, f32)
        for hd in range(PEER_HEADS):
            q1 = qt[hd * PEER_QDIM: hd * PEER_QDIM + half]
            q2 = qt[hd * PEER_QDIM + half: (hd + 1) * PEER_QDIM]
            s1 = jnp.dot(sk_ref[hd, 0], q1, precision=lax.Precision.HIGHEST, preferred_element_type=f32)
            s2 = jnp.dot(sk_ref[hd, 1], q2, precision=lax.Precision.HIGHEST, preferred_element_type=f32)
            top1 = _top16_sorted(s1)
            top2 = _top16_sorted(s2)
            for k in range(PEER_TOPK):
                v1_ref[k:k + 1, :] = top1[k]
                v2_ref[k:k + 1, :] = top2[k]
            off = 0
            for i in range(PEER_TOPK):
                cnt = PEER_TOPK // (i + 1)
                cand_ref[off:off + cnt, :] = v1_ref[i:i + 1, :] + v2_ref[0:cnt, :]
                off += cnt
            best = _extract_top(cand_ref[...], PEER_TOPK)
            z = jnp.zeros_like(best[0])
            for k in range(PEER_TOPK):
                z = z + jnp.exp(best[k] - best[0])
            s1_ref[hd] = s1
            s2_ref[hd] = s2
            e1_ref[hd] = jnp.exp(s1 - top1[0])
            e2_ref[hd] = jnp.exp(s2 - top2[0]) / z
            tau_ref[hd:hd + 1, :] = best[PEER_TOPK - 1]
        acc_ref[...] = jnp.zeros(acc_ref.shape, f32)

    act_ref[...] = jnp.dot(u_ref[...], ht_ref[...], preferred_element_type=f32)
    a_base = pl.multiple_of(eb * a_per, a_per)
    blk = (PEER_KEYS, PEER_LANES)
    for lg in range(0, x_ref.shape[0], PEER_LANES):
        lanes = slice(lg, lg + PEER_LANES)
        for a in range(a_per):
            g = None
            for hd in range(PEER_HEADS):
                s1a = jnp.broadcast_to(s1_ref[hd, pl.ds(a_base, a_per), lanes][a:a + 1, :], blk)
                e1a = jnp.broadcast_to(e1_ref[hd, pl.ds(a_base, a_per), lanes][a:a + 1, :], blk)
                tau = jnp.broadcast_to(tau_ref[hd:hd + 1, lanes], blk)
                t = s2_ref[hd, :, lanes] + s1a
                w = jnp.where(t >= tau, e2_ref[hd, :, lanes], 0.0) * e1a
                g = w if g is None else g + w
            rows = slice(a * PEER_KEYS, (a + 1) * PEER_KEYS)
            gact_ref[rows, lanes] = (g * jax.nn.gelu(act_ref[rows, lanes])).astype(bf16)
    acc_ref[...] += jnp.dot(vt_ref[...], gact_ref[...], preferred_element_type=f32)

    @pl.when(eb == n_eb - 1)
    def _epilogue():
        mod = mod_ref[0]
        y = DEEPNORM_ALPHA * x_ref[...] + mod[:, 5 * dm:6 * dm] * acc_ref[...].T
        mu = jnp.mean(y, axis=-1, keepdims=True)
        yc = y - mu
        var = jnp.mean(yc * yc, axis=-1, keepdims=True)
        o_ref[...] = yc * lax.rsqrt(var + LN_EPS) * lng_ref[...] + lnb_ref[...]


def _mod_row_index(i, tm, n_lat, seq, bsz):
    return jnp.where(i * tm < n_lat, (i * tm) // seq, bsz)


def peer_ln_pallas(xa, mod3, n_lat, seq, bsz, w_q, subkeys, u_tab, v_tab, ln_g, ln_b):
    n_tok, dm = xa.shape
    tm = PEER_TM
    f32, bf16 = jnp.float32, jnp.bfloat16
    wq_t = w_q.reshape(dm, PEER_HEADS * PEER_QDIM).T
    wq_hi = wq_t.astype(bf16)
    wq_lo = (wq_t - wq_hi.astype(f32)).astype(bf16)
    u_bf = u_tab.astype(bf16)
    vt_bf = v_tab.T.astype(bf16)
    n_eb = PEER_EXPERTS // PEER_EB
    qd = PEER_HEADS * PEER_QDIM
    n_cand = sum(PEER_TOPK // (i + 1) for i in range(PEER_TOPK))
    n_cand_pad = -(-n_cand // 8) * 8
    mod_idx = functools.partial(_mod_row_index, tm=tm, n_lat=n_lat, seq=seq, bsz=bsz)
    return pl.pallas_call(
        _peer_kernel,
        grid=(n_tok // tm, n_eb),
        in_specs=[
            pl.BlockSpec((tm, dm), lambda i, e: (i, 0)),
            pl.BlockSpec((1, 1, 6 * dm), lambda i, e: (mod_idx(i), 0, 0)),
            pl.BlockSpec((qd, dm), lambda i, e: (0, 0)),
            pl.BlockSpec((qd, dm), lambda i, e: (0, 0)),
            pl.BlockSpec((PEER_HEADS, 2, PEER_KEYS, PEER_QDIM // 2), lambda i, e: (0, 0, 0, 0)),
            pl.BlockSpec((PEER_EB, dm), lambda i, e: (e, 0)),
            pl.BlockSpec((dm, PEER_EB), lambda i, e: (0, e)),
            pl.BlockSpec((1, dm), lambda i, e: (0, 0)),
            pl.BlockSpec((1, dm), lambda i, e: (0, 0)),
        ],
        out_specs=pl.BlockSpec((tm, dm), lambda i, e: (i, 0)),
        out_shape=jax.ShapeDtypeStruct((n_tok, dm), f32),
        scratch_shapes=[
            pltpu.VMEM((dm, tm), bf16),
            pltpu.VMEM((PEER_HEADS, PEER_KEYS, tm), f32),
            pltpu.VMEM((PEER_HEADS, PEER_KEYS, tm), f32),
            pltpu.VMEM((PEER_HEADS, PEER_KEYS, tm), f32),
            pltpu.VMEM((PEER_HEADS, PEER_KEYS, tm), f32),
            pltpu.VMEM((PEER_HEADS, tm), f32),
            pltpu.VMEM((PEER_TOPK, tm), f32),
            pltpu.VMEM((PEER_TOPK, tm), f32),
            pltpu.VMEM((n_cand_pad, tm), f32),
            pltpu.VMEM((PEER_EB, tm), f32),
            pltpu.VMEM((PEER_EB, tm), bf16),
            pltpu.VMEM((dm, tm), f32),
        ],
        compiler_params=pltpu.CompilerParams(
            dimension_semantics=("arbitrary", "arbitrary"),
            vmem_limit_bytes=VMEM_LIMIT),
        name="peer_ln",
    )(xa, mod3, wq_hi, wq_lo, subkeys, u_bf, vt_bf, ln_g.reshape(1, dm), ln_b.reshape(1, dm))


S5_CHUNK = 16
S5_LEVELS = 9


def _s5_prep(a_re, a_im, log_dt, b_re, b_im, c_re, c_im, d):
    f32, bf16 = jnp.float32, jnp.bfloat16
    hp = lax.Precision.HIGHEST
    n_g, n_p, q = S5_GROUPS, S5_STATE, S5_CHUNK
    dt = jnp.exp(log_dt.astype(f32))[..., None]

    def apow(n):
        mag = jnp.exp(a_re * dt * n)
        ang = a_im * dt * n
        return mag * jnp.cos(ang), mag * jnp.sin(ang)

    ar1, ai1 = apow(1.0)
    den = a_re * a_re + a_im * a_im
    nr, ni = ar1 - 1.0, ai1
    cr = (nr * a_re + ni * a_im) / den
    ci = (ni * a_re - nr * a_im) / den
    bbr = cr[..., None] * b_re - ci[..., None] * b_im
    bbi = cr[..., None] * b_im + ci[..., None] * b_re
    lag = jnp.arange(q + 1, dtype=f32)[:, None, None, None]
    pr, pi = apow(lag)
    mr = pr[..., None] * bbr - pi[..., None] * bbi
    mi = pr[..., None] * bbi + pi[..., None] * bbr
    kmat = (jnp.einsum('dgcp,ndgpk->ndgck', c_re, mr, precision=hp)
            - jnp.einsum('dgcp,ndgpk->ndgck', c_im, mi, precision=hp))
    r_idx = jnp.arange(q)[:, None]
    t_idx = jnp.arange(q)[None, :]

    def toeplitz(kd, lagm):
        blk = kd[jnp.clip(lagm, 0, q)]
        blk = jnp.where((lagm >= 0)[:, :, None, None, None], blk, 0.0)
        return blk.transpose(2, 0, 4, 1, 3).reshape(n_g, q * S5_GROUP, q * S5_GROUP)

    eye = jnp.eye(q * S5_GROUP, dtype=f32)
    dvec = jnp.tile(d.reshape(n_g, 1, S5_GROUP), (1, q, 1)).reshape(n_g, q * S5_GROUP)
    tsum = toeplitz(kmat[:, 0], t_idx - r_idx) + toeplitz(kmat[:, 1], r_idx - t_idx) + eye[None] * dvec[:, None, :]

    def w_in(direction, exps):
        wr = mr[exps, direction].transpose(1, 0, 3, 2).reshape(n_g, q * S5_GROUP, n_p)
        wi = mi[exps, direction].transpose(1, 0, 3, 2).reshape(n_g, q * S5_GROUP, n_p)
        return wr, wi

    def w_out(direction, exps):
        pre, pim = pr[exps, direction], pi[exps, direction]
        cre, cim = c_re[direction], c_im[direction]
        wre = cre[None] * pre[:, :, None, :] - cim[None] * pim[:, :, None, :]
        wim = -(cre[None] * pim[:, :, None, :] + cim[None] * pre[:, :, None, :])
        return (wre.transpose(1, 3, 0, 2).reshape(n_g, n_p, q * S5_GROUP),
                wim.transpose(1, 3, 0, 2).reshape(n_g, n_p, q * S5_GROUP))

    steps = jnp.arange(q)
    win = [w_in(0, q - 1 - steps), w_in(1, steps)]
    wout = [w_out(0, steps + 1), w_out(1, q - steps)]
    zc = jnp.zeros((n_g, q * S5_GROUP, n_p), f32)
    zr = jnp.zeros((n_g, n_p, q * S5_GROUP), f32)
    win_p, wout_p = [], []
    for direction in range(2):
        wr, wi = win[direction]
        even = jnp.concatenate([wr, zc, wi, zc], axis=2)
        odd = jnp.concatenate([zc, wr, zc, wi], axis=2)
        is_odd = (jnp.arange(n_g) % 2 == 1)[:, None, None]
        win_p.append(jnp.where(is_odd, odd, even))
        vr, vi = wout[direction]
        even = jnp.concatenate([vr, zr, vi, zr], axis=1)
        odd = jnp.concatenate([zr, vr, zr, vi], axis=1)
        wout_p.append(jnp.where(is_odd, odd, even))
    win_p = jnp.stack(win_p, axis=1).reshape(n_g // 2, 2, 2, 4 * n_p, 4 * n_p).transpose(0, 2, 1, 3, 4)
    wout_p = jnp.stack(wout_p, axis=1).reshape(n_g // 2, 2, 2, 4 * n_p, 4 * n_p).transpose(0, 2, 1, 3, 4)
    lvl = (q * 2.0 ** jnp.arange(S5_LEVELS, dtype=f32))[:, None, None, None]
    lr, li = apow(lvl)
    pw = jnp.stack([lr, li], axis=1)
    pw = pw.transpose(3, 2, 0, 1, 4).reshape(n_g // 2, 2, 2, S5_LEVELS, 2, n_p)
    pw = pw.transpose(0, 2, 3, 4, 1, 5).reshape(n_g // 2, 2, 2 * S5_LEVELS, 2 * n_p)
    return tsum.astype(bf16).reshape(n_g // 2, 2, q * S5_GROUP, q * S5_GROUP), win_p.astype(bf16), wout_p.astype(bf16), pw


def _shift_rows(v, r, up):
    n = v.shape[0]
    row = lax.broadcasted_iota(jnp.int32, v.shape, 0)
    if up:
        return jnp.where(row < n - r, pltpu.roll(v, n - r, axis=0), 0.0)
    return jnp.where(row >= r, pltpu.roll(v, r, axis=0), 0.0)


def _chunk_scan(re, im, pw, rows_per_chunk, up):
    n_chunks = re.shape[0] // rows_per_chunk
    level, s = 0, 1
    while s < n_chunks:
        ar, ai = pw[2 * level:2 * level + 1, :], pw[2 * level + 1:2 * level + 2, :]
        sre = _shift_rows(re, s * rows_per_chunk, up)
        sim = _shift_rows(im, s * rows_per_chunk, up)
        re, im = re + ar * sre - ai * sim, im + ar * sim + ai * sre
        level, s = level + 1, 2 * s
    return re, im


def _s5_kernel(u_ref, scl_ref, shl_ref, scc_ref, shc_ref, t_ref, win_ref, wout_ref, pw_ref, y_ref, *, bsz, ctx_chunks):
    f32, bf16 = jnp.float32, jnp.bfloat16
    n = u_ref.shape[1]
    width = u_ref.shape[2]
    rc = ctx_chunks * bsz
    hs = []
    for gi in range(2):
        xv = u_ref[gi].reshape(n // 8, 8, width)
        hl = xv * scl_ref[gi][None] + shl_ref[gi][None]
        hc = xv * scc_ref[gi][None] + shc_ref[gi][None]
        slab = lax.broadcasted_iota(jnp.int32, xv.shape, 0)
        hs.append(jnp.where(slab < rc // 8, hc, hl).reshape(n, width).astype(bf16))
    y = [jnp.dot(hs[gi], t_ref[0, gi], preferred_element_type=f32) for gi in range(2)]
    half = width // 2
    for direction in range(2):
        sloc = (jnp.dot(hs[0], win_ref[0, direction, 0], preferred_element_type=f32)
                + jnp.dot(hs[1], win_ref[0, direction, 1], preferred_element_type=f32))
        re, im = sloc[:, :half], sloc[:, half:]
        pw = pw_ref[0, direction]
        if direction == 0:
            sre, sim = _chunk_scan(_shift_rows(re, bsz, False), _shift_rows(im, bsz, False), pw, bsz, False)
        else:
            cre, cim = _chunk_scan(_shift_rows(re[:rc], bsz, True), _shift_rows(im[:rc], bsz, True), pw, bsz, True)
            ar, ai = pw[0:1, :], pw[1:2, :]
            fre = ar * cre[0:8] - ai * cim[0:8] + re[0:8]
            fim = ar * cim[0:8] + ai * cre[0:8] + im[0:8]
            row8 = lax.broadcasted_iota(jnp.int32, fre.shape, 0)
            tre = jnp.where(row8 >= 8 - bsz, pltpu.roll(fre, 8 - bsz, axis=0), 0.0)
            tim = jnp.where(row8 >= 8 - bsz, pltpu.roll(fim, 8 - bsz, axis=0), 0.0)
            lre, lim = _shift_rows(re[rc:], bsz, True), _shift_rows(im[rc:], bsz, True)
            lre = jnp.concatenate([lre[:-8], lre[-8:] + tre], axis=0)
            lim = jnp.concatenate([lim[:-8], lim[-8:] + tim], axis=0)
            lre, lim = _chunk_scan(lre, lim, pw, bsz, True)
            sre = jnp.concatenate([cre, lre], axis=0)
            sim = jnp.concatenate([cim, lim], axis=0)
        s_in = jnp.concatenate([sre, sim], axis=1).astype(bf16)
        for gi in range(2):
            y[gi] = y[gi] + jnp.dot(s_in, wout_ref[0, direction, gi], preferred_element_type=f32)
    for gi in range(2):
        y_ref[gi] = y[gi]


def s5_ssm_pallas(xa, mod_tab, n_lat, seq, bsz, prep):
    tsum, win_p, wout_p, pw = prep
    f32 = jnp.float32
    n_tok, dm = xa.shape
    ctx_len = (n_tok - n_lat) // bsz
    q = S5_CHUNK
    n_chunks = (seq + ctx_len) // q
    width = q * S5_GROUP
    full = jnp.concatenate([xa[n_lat:].reshape(bsz, ctx_len, dm), xa[:n_lat].reshape(bsz, seq, dm)], axis=1)
    u = full.reshape(bsz, n_chunks, q, S5_GROUPS, S5_GROUP).transpose(3, 1, 0, 2, 4).reshape(S5_GROUPS, n_chunks * bsz, width)

    def tile(vec_rows):
        rows = vec_rows.shape[0]
        t = vec_rows.reshape(rows, S5_GROUPS, 1, S5_GROUP)
        t = jnp.broadcast_to(t, (rows, S5_GROUPS, q, S5_GROUP)).reshape(rows, S5_GROUPS, width)
        return jnp.tile(t.transpose(1, 0, 2), (1, 8 // rows, 1))

    scl, shl = tile(1.0 + mod_tab[:bsz, dm:2 * dm]), tile(mod_tab[:bsz, 0:dm])
    scc = tile(jnp.broadcast_to(1.0 + mod_tab[bsz:bsz + 1, dm:2 * dm], (bsz, dm)))
    shc = tile(jnp.broadcast_to(mod_tab[bsz:bsz + 1, 0:dm], (bsz, dm)))
    n_rows = n_chunks * bsz
    tile_spec = pl.BlockSpec((2, 8, width), lambda p: (p, 0, 0))
    y = pl.pallas_call(
        functools.partial(_s5_kernel, bsz=bsz, ctx_chunks=ctx_len // q),
        grid=(S5_GROUPS // 2,),
        in_specs=[
            pl.BlockSpec((2, n_rows, width), lambda p: (p, 0, 0)),
            tile_spec, tile_spec, tile_spec, tile_spec,
            pl.BlockSpec((1, 2, width, width), lambda p: (p, 0, 0, 0)),
            pl.BlockSpec((1, 2, 2, width, width), lambda p: (p, 0, 0, 0, 0)),
            pl.BlockSpec((1, 2, 2, width, width), lambda p: (p, 0, 0, 0, 0)),
            pl.BlockSpec((1, 2, 2 * S5_LEVELS, 2 * S5_STATE), lambda p: (p, 0, 0, 0)),
        ],
        out_specs=pl.BlockSpec((2, n_rows, width), lambda p: (p, 0, 0)),
        out_shape=jax.ShapeDtypeStruct((S5_GROUPS, n_rows, width), f32),
        compiler_params=pltpu.CompilerParams(dimension_semantics=("arbitrary",), vmem_limit_bytes=VMEM_LIMIT),
        name="s5_ssm",
    )(u, scl, shl, scc, shc, tsum, win_p, wout_p, pw)
    yf = y.reshape(S5_GROUPS, n_chunks, bsz, q, S5_GROUP).transpose(2, 1, 3, 0, 4).reshape(bsz, seq + ctx_len, dm)
    return jnp.concatenate([yf[:, ctx_len:].reshape(n_lat, dm), yf[:, :ctx_len].reshape(bsz * ctx_len, dm)], axis=0)


def _glu_ln_kernel(y_ref, x_ref, mod_ref, wv_ref, wg_ref, lng_ref, lnb_ref, o_ref):
    f32 = jnp.float32
    dm = x_ref.shape[1]
    g = jax.nn.gelu(y_ref[...]).astype(jnp.bfloat16)
    val = jnp.dot(g, wv_ref[...], preferred_element_type=f32)
    gate = jnp.dot(g, wg_ref[...], preferred_element_type=f32)
    out = val * jax.nn.sigmoid(gate)
    mod = mod_ref[0]
    z = DEEPNORM_ALPHA * x_ref[...] + mod[:, 2 * dm:3 * dm] * out
    mu = jnp.mean(z, axis=-1, keepdims=True)
    zc = z - mu
    var = jnp.mean(zc * zc, axis=-1, keepdims=True)
    o_ref[...] = zc * lax.rsqrt(var + LN_EPS) * lng_ref[...] + lnb_ref[...]


TOK_TM = 512


def glu_ln_pallas(ya, xa, mod3, n_lat, seq, bsz, w_val, w_gate, ln_g, ln_b):
    n_tok, dm = ya.shape
    tm = TOK_TM
    bf16 = jnp.bfloat16
    mod_idx = functools.partial(_mod_row_index, tm=tm, n_lat=n_lat, seq=seq, bsz=bsz)
    return pl.pallas_call(
        _glu_ln_kernel,
        grid=(n_tok // tm,),
        in_specs=[
            pl.BlockSpec((tm, dm), lambda i: (i, 0)),
            pl.BlockSpec((tm, dm), lambda i: (i, 0)),
            pl.BlockSpec((1, 1, 6 * dm), lambda i: (mod_idx(i), 0, 0)),
            pl.BlockSpec((dm, dm), lambda i: (0, 0)),
            pl.BlockSpec((dm, dm), lambda i: (0, 0)),
            pl.BlockSpec((1, dm), lambda i: (0, 0)),
            pl.BlockSpec((1, dm), lambda i: (0, 0)),
        ],
        out_specs=pl.BlockSpec((tm, dm), lambda i: (i, 0)),
        out_shape=jax.ShapeDtypeStruct((n_tok, dm), jnp.float32),
        compiler_params=pltpu.CompilerParams(dimension_semantics=("arbitrary",), vmem_limit_bytes=VMEM_LIMIT),
        name="glu_ln",
    )(ya, xa, mod3, w_val.astype(bf16), w_gate.astype(bf16), ln_g.reshape(1, dm), ln_b.reshape(1, dm))


MLA_HD = 128
MLA_TQ = 256


def _mla_prep(w_down, w_uq, w_uk, w_uv):
    bf16 = jnp.bfloat16
    quarter = MLA_ROPE // 4
    swap = np.concatenate([np.arange(quarter, 2 * quarter), np.arange(0, quarter),
                           np.arange(3 * quarter, 4 * quarter), np.arange(2 * quarter, 3 * quarter)])
    dm = w_down.shape[0]
    w_cq = w_down[:, :MLA_Q_RANK]
    w_ckv = w_down[:, MLA_Q_RANK:MLA_Q_RANK + MLA_KV_RANK]
    w_kr = w_down[:, MLA_Q_RANK + MLA_KV_RANK:]
    wd_t = jnp.concatenate([w_cq, w_ckv, jnp.zeros((dm, MLA_NOPE), w_down.dtype), w_kr, w_kr[:, swap]], axis=1).T
    rope = w_uq[:, :, MLA_NOPE:]
    wuq_t = jnp.concatenate([w_uq, rope[:, :, swap]], axis=2).reshape(MLA_Q_RANK, MLA_HEADS * MLA_HD).T
    wuk = jnp.concatenate([w_uk, jnp.zeros((MLA_KV_RANK, MLA_HEADS, MLA_HD - MLA_NOPE), w_uk.dtype)], axis=2)
    wuk = wuk.reshape(MLA_KV_RANK, MLA_HEADS * MLA_HD)
    wuv_t = w_uv.reshape(MLA_KV_RANK, MLA_HEADS * MLA_V).T
    return wd_t.astype(bf16), wuq_t.astype(bf16), wuk.astype(bf16), wuv_t.astype(bf16)


def _rope_tables(seq, tm):
    quarter = MLA_ROPE // 4
    freqs = ROPE_BASE ** (-jnp.arange(quarter, dtype=jnp.float32) / quarter)
    t = jnp.arange(seq, dtype=jnp.float32)
    row, col = jnp.floor(t / GRID_W), t - GRID_W * jnp.floor(t / GRID_W)
    ang_r, ang_c = freqs[:, None] * row[None, :], freqs[:, None] * col[None, :]
    cos32 = jnp.concatenate([jnp.cos(ang_r), jnp.cos(ang_r), jnp.cos(ang_c), jnp.cos(ang_c)], axis=0)
    sin32 = jnp.concatenate([-jnp.sin(ang_r), jnp.sin(ang_r), -jnp.sin(ang_c), jnp.sin(ang_c)], axis=0)
    ones = jnp.ones((MLA_NOPE, seq + tm), jnp.float32)
    zeros = jnp.zeros((MLA_HD - MLA_NOPE - MLA_ROPE, seq + tm), jnp.float32)
    cos_t = jnp.concatenate([ones, jnp.concatenate([cos32, jnp.ones((MLA_ROPE, tm))], axis=1), zeros], axis=0)
    sin_t = jnp.concatenate([0.0 * ones, jnp.concatenate([sin32, jnp.zeros((MLA_ROPE, tm))], axis=1), zeros], axis=0)
    return cos_t, sin_t


def _mla_proj_kernel(x_ref, mod_ref, cos_ref, sin_ref, wd_ref, wuq_ref, wuk_ref, wuv_ref, qn_ref, kvn_ref,
                     qt_ref, k_ref, vt_ref):
    f32, bf16 = jnp.float32, jnp.bfloat16
    dm = x_ref.shape[1]
    mod = mod_ref[0]
    h = x_ref[...] * (1.0 + mod[:, dm:2 * dm]) + mod[:, 0:dm]
    ht = h.T.astype(bf16)
    dt = jnp.dot(wd_ref[...], ht, preferred_element_type=f32)
    cq, ckv, kr = dt[:MLA_Q_RANK], dt[MLA_Q_RANK:MLA_Q_RANK + MLA_KV_RANK], dt[MLA_Q_RANK + MLA_KV_RANK:]
    cqn = cq * lax.rsqrt(jnp.mean(cq * cq, axis=0, keepdims=True) + RMS_EPS) * qn_ref[...]
    ckvn = ckv * lax.rsqrt(jnp.mean(ckv * ckv, axis=0, keepdims=True) + RMS_EPS) * kvn_ref[...]
    cos_t, sin_t = cos_ref[...], sin_ref[...]

    def rope(v):
        shifted = jnp.concatenate([v[MLA_ROPE:], v[:MLA_ROPE]], axis=0)
        return v * cos_t + shifted * sin_t

    scale = (MLA_NOPE + MLA_ROPE) ** -0.5
    q_all = jnp.dot(wuq_ref[...], cqn.astype(bf16), preferred_element_type=f32)
    for hd in range(MLA_HEADS):
        qt_ref[hd] = (rope(q_all[hd * MLA_HD:(hd + 1) * MLA_HD]) * scale).astype(bf16)
    ckvn_bf = ckvn.astype(bf16)
    vt_ref[...] = jnp.dot(wuv_ref[...], ckvn_bf, preferred_element_type=f32).astype(bf16)
    k_all = jnp.dot(ckvn.T.astype(bf16), wuk_ref[...], preferred_element_type=f32)
    kr_rows = rope(kr).T
    for hd in range(MLA_HEADS):
        k_ref[hd] = (k_all[:, hd * MLA_HD:(hd + 1) * MLA_HD] + kr_rows).astype(bf16)


def _mla_attn_kernel(*refs, with_latent):
    f32, bf16 = jnp.float32, jnp.bfloat16
    if with_latent:
        q_ref, kl_ref, kc_ref, vl_ref, vc_ref, o_ref = refs
    else:
        q_ref, kc_ref, vc_ref, o_ref = refs
    q = q_ref[0]
    s_c = jnp.dot(kc_ref[0], q, preferred_element_type=f32)
    m = jnp.max(s_c, axis=0, keepdims=True)
    if with_latent:
        s_l = jnp.dot(kl_ref[0], q, preferred_element_type=f32)
        m = jnp.maximum(m, jnp.max(s_l, axis=0, keepdims=True))
    p_c = jnp.exp(s_c - m)
    den = jnp.sum(p_c, axis=0, keepdims=True)
    o = jnp.dot(vc_ref[...], p_c.astype(bf16), preferred_element_type=f32)
    if with_latent:
        p_l = jnp.exp(s_l - m)
        den = den + jnp.sum(p_l, axis=0, keepdims=True)
        o = o + jnp.dot(vl_ref[...], p_l.astype(bf16), preferred_element_type=f32)
    o_ref[...] = (o / den).astype(bf16)


def _mla_out_kernel(ot_ref, x_ref, mod_ref, wo_ref, lng_ref, lnb_ref, o_ref):
    f32 = jnp.float32
    dm = x_ref.shape[1]
    attn = ot_ref[...].astype(f32).T.astype(jnp.bfloat16)
    out = jnp.dot(attn, wo_ref[...], preferred_element_type=f32)
    mod = mod_ref[0]
    z = DEEPNORM_ALPHA * x_ref[...] + mod[:, 2 * dm:3 * dm] * out
    mu = jnp.mean(z, axis=-1, keepdims=True)
    zc = z - mu
    var = jnp.mean(zc * zc, axis=-1, keepdims=True)
    o_ref[...] = zc * lax.rsqrt(var + LN_EPS) * lng_ref[...] + lnb_ref[...]


def mla_ln_pallas(xa, mod3, n_lat, seq, bsz, w_down, q_norm, kv_norm, w_uq, w_uk, w_uv, w_o, ln_g, ln_b):
    f32, bf16 = jnp.float32, jnp.bfloat16
    n_tok, dm = xa.shape
    n_ctx = n_tok - n_lat
    ctx_len = n_ctx // bsz
    tm = TOK_TM
    wd_t, wuq_t, wuk, wuv_t = _mla_prep(w_down, w_uq, w_uk, w_uv)
    cos_t, sin_t = _rope_tables(seq, tm)
    mod_idx = functools.partial(_mod_row_index, tm=tm, n_lat=n_lat, seq=seq, bsz=bsz)
    tiles_per_sample = seq // tm
    pos_idx = lambda i: jnp.where(i * tm < n_lat, i % tiles_per_sample, tiles_per_sample)
    n_hd, n_dn = MLA_HEADS * MLA_HD, wd_t.shape[0]
    whole = lambda shape: pl.BlockSpec(shape, lambda i: (0,) * len(shape))
    params = pltpu.CompilerParams(dimension_semantics=("arbitrary",), vmem_limit_bytes=VMEM_LIMIT)
    qt, k, vt = pl.pallas_call(
        _mla_proj_kernel,
        grid=(n_tok // tm,),
        in_specs=[
            pl.BlockSpec((tm, dm), lambda i: (i, 0)),
            pl.BlockSpec((1, 1, 6 * dm), lambda i: (mod_idx(i), 0, 0)),
            pl.BlockSpec((MLA_HD, tm), lambda i: (0, pos_idx(i))),
            pl.BlockSpec((MLA_HD, tm), lambda i: (0, pos_idx(i))),
            whole((n_dn, dm)), whole((n_hd, MLA_Q_RANK)), whole((MLA_KV_RANK, n_hd)),
            whole((MLA_HEADS * MLA_V, MLA_KV_RANK)), whole((MLA_Q_RANK, 1)), whole((MLA_KV_RANK, 1)),
        ],
        out_specs=[
            pl.BlockSpec((MLA_HEADS, MLA_HD, tm), lambda i: (0, 0, i)),
            pl.BlockSpec((MLA_HEADS, tm, MLA_HD), lambda i: (0, i, 0)),
            pl.BlockSpec((MLA_HEADS * MLA_V, tm), lambda i: (0, i)),
        ],
        out_shape=[
            jax.ShapeDtypeStruct((MLA_HEADS, MLA_HD, n_tok), bf16),
            jax.ShapeDtypeStruct((MLA_HEADS, n_tok, MLA_HD), bf16),
            jax.ShapeDtypeStruct((MLA_HEADS * MLA_V, n_tok), bf16),
        ],
        compiler_params=params,
        name="mla_proj",
    )(xa, mod3, cos_t, sin_t, wd_t, wuq_t, wuk, wuv_t, q_norm.reshape(-1, 1), kv_norm.reshape(-1, 1))

    tq = MLA_TQ
    n_qt = seq // tq
    ctx_blk = n_lat // ctx_len
    params3 = pltpu.CompilerParams(dimension_semantics=("arbitrary",) * 3, vmem_limit_bytes=VMEM_LIMIT)
    ot_lat = pl.pallas_call(
        functools.partial(_mla_attn_kernel, with_latent=True),
        grid=(bsz, MLA_HEADS, n_qt),
        in_specs=[
            pl.BlockSpec((1, MLA_HD, tq), lambda b, hd, t: (hd, 0, b * n_qt + t)),
            pl.BlockSpec((1, seq, MLA_HD), lambda b, hd, t: (hd, b, 0)),
            pl.BlockSpec((1, ctx_len, MLA_HD), lambda b, hd, t: (hd, ctx_blk + b, 0)),
            pl.BlockSpec((MLA_V, seq), lambda b, hd, t: (hd, b)),
            pl.BlockSpec((MLA_V, ctx_len), lambda b, hd, t: (hd, ctx_blk + b)),
        ],
        out_specs=pl.BlockSpec((MLA_V, tq), lambda b, hd, t: (hd, b * n_qt + t)),
        out_shape=jax.ShapeDtypeStruct((MLA_HEADS * MLA_V, n_lat), bf16),
        compiler_params=params3,
        name="mla_attn_latent",
    )(qt, k, k, vt, vt)
    ot_ctx = pl.pallas_call(
        functools.partial(_mla_attn_kernel, with_latent=False),
        grid=(bsz, MLA_HEADS, 1),
        in_specs=[
            pl.BlockSpec((1, MLA_HD, ctx_len), lambda b, hd, t: (hd, 0, ctx_blk + b)),
            pl.BlockSpec((1, ctx_len, MLA_HD), lambda b, hd, t: (hd, ctx_blk + b, 0)),
            pl.BlockSpec((MLA_V, ctx_len), lambda b, hd, t: (hd, ctx_blk + b)),
        ],
        out_specs=pl.BlockSpec((MLA_V, ctx_len), lambda b, hd, t: (hd, b)),
        out_shape=jax.ShapeDtypeStruct((MLA_HEADS * MLA_V, n_ctx), bf16),
        compiler_params=params3,
        name="mla_attn_context",
    )(qt, k, vt)
    ot = jnp.concatenate([ot_lat, ot_ctx], axis=1)
    return pl.pallas_call(
        _mla_out_kernel,
        grid=(n_tok // tm,),
        in_specs=[
            pl.BlockSpec((MLA_HEADS * MLA_V, tm), lambda i: (0, i)),
            pl.BlockSpec((tm, dm), lambda i: (i, 0)),
            pl.BlockSpec((1, 1, 6 * dm), lambda i: (mod_idx(i), 0, 0)),
            whole((MLA_HEADS * MLA_V, dm)), whole((1, dm)), whole((1, dm)),
        ],
        out_specs=pl.BlockSpec((tm, dm), lambda i: (i, 0)),
        out_shape=jax.ShapeDtypeStruct((n_tok, dm), f32),
        compiler_params=params,
        name="mla_out_ln",
    )(ot, xa, mod3, w_o.astype(bf16), ln_g.reshape(1, dm), ln_b.reshape(1, dm))


SSD_DT_PAD = 128
SSD_HPG = SSD_HEADS // SSD_GROUPS
SSD_GW = SSD_HPG * SSD_HEADDIM


def _ssd_in_kernel(x_ref, mod_ref, w_ref, wdt_ref, z_ref, xbc_ref, dt_ref, dtt_ref):
    f32, bf16 = jnp.float32, jnp.bfloat16
    dm = x_ref.shape[1]
    mod = mod_ref[0]
    h = x_ref[...] * (1.0 + mod[:, dm:2 * dm]) + mod[:, 0:dm]
    proj = jnp.dot(h.astype(bf16), w_ref[...], preferred_element_type=f32)
    z_ref[...] = proj[:, :SSD_D_INNER]
    xbc_ref[...] = proj[:, SSD_D_INNER:SSD_D_INNER + SSD_CONV_DIM]
    dt_ref[...] = proj[:, SSD_D_INNER + SSD_CONV_DIM:]
    dtt_ref[...] = jnp.dot(wdt_ref[...], h.T.astype(bf16), preferred_element_type=f32)


def _softplus(v):
    return jnp.maximum(v, 0.0) + jnp.log(1.0 + jnp.exp(-jnp.abs(v)))


def _split3(v):
    f32, bf16 = jnp.float32, jnp.bfloat16
    hi = v.astype(bf16)
    r1 = v - hi.astype(f32)
    mid = r1.astype(bf16)
    lo = (r1 - mid.astype(f32)).astype(bf16)
    return jnp.concatenate([hi, mid, lo], axis=1)


def _ssd_direction(direction, xbc_ref, prev_ref, next_ref, dt_ref, dtt_ref, has_prev, has_next,
                   cw_ref, cb_ref, bias_r_ref, a_r_ref, bias_c_ref, a_c_ref, e_ref, dskip_ref, state_ref, y_ref):
    f32, bf16 = jnp.float32, jnp.bfloat16
    hp = lax.Precision.HIGHEST
    q = xbc_ref.shape[0]
    nh = SSD_HEADS
    xm = xbc_ref[...]
    row = lax.broadcasted_iota(jnp.int32, xm.shape, 0)
    before = jnp.where(has_prev, prev_ref[7:8, :], 0.0)
    after = jnp.where(has_next, next_ref[0:1, :], 0.0)
    x_dn = jnp.where(row == 0, before, pltpu.roll(xm, 1, axis=0))
    x_up = jnp.where(row == q - 1, after, pltpu.roll(xm, q - 1, axis=0))
    conv = cb_ref[...] + x_dn * cw_ref[0:1, :] + xm * cw_ref[1:2, :] + x_up * cw_ref[2:3, :]
    conv = conv * jax.nn.sigmoid(conv)
    xs = conv[:, :SSD_D_INNER]
    gn = SSD_GROUPS * SSD_STATE
    bm, cm = conv[:, SSD_D_INNER:SSD_D_INNER + gn], conv[:, SSD_D_INNER + gn:]
    hs = slice(direction * nh, (direction + 1) * nh)
    dt = _softplus(dt_ref[:, hs] + bias_r_ref[direction:direction + 1, :])
    dtt = _softplus(dtt_ref[hs, :] + bias_c_ref[direction])
    r_i = lax.broadcasted_iota(jnp.int32, (q, q), 0)
    c_i = lax.broadcasted_iota(jnp.int32, (q, q), 1)
    causal = (r_i >= c_i) if direction == 0 else (r_i <= c_i)
    tri = causal.astype(f32)
    tri_t = ((c_i >= r_i) if direction == 0 else (c_i <= r_i)).astype(f32)
    a_cum = jnp.dot(tri, dt * a_r_ref[direction:direction + 1, :], precision=hp, preferred_element_type=f32)
    a_cum_t = jnp.dot(dtt * a_c_ref[direction], tri_t, precision=hp, preferred_element_type=f32)
    last = q - 1 if direction == 0 else 0
    a_exp = jnp.dot(_split3(a_cum), e_ref[...], preferred_element_type=f32)
    dt_exp = jnp.dot(_split3(dt), e_ref[...], preferred_element_type=f32)
    xdt = xs * dt_exp
    a_tot = a_exp[last:last + 1, :]
    xw = (xdt * jnp.exp(a_tot - a_exp)).astype(bf16)
    grow = jnp.exp(a_exp)
    carry = jnp.exp(a_tot)
    xdt_bf = xdt.astype(bf16)
    for g in range(SSD_GROUPS):
        gl = slice(g * SSD_GW, (g + 1) * SSD_GW)
        nl = slice(g * SSD_STATE, (g + 1) * SSD_STATE)
        bm_g, cm_g = bm[:, nl], cm[:, nl].astype(bf16)
        prev = state_ref[direction, g]
        y_g = jnp.dot(cm_g, prev.astype(bf16), preferred_element_type=f32) * grow[:, gl]
        states = jnp.dot(bm_g.T.astype(bf16), xw[:, gl], preferred_element_type=f32)
        state_ref[direction, g] = prev * carry[:, gl] + states
        cb = lax.dot_general(cm_g, bm_g.astype(bf16), (((1,), (1,)), ((), ())), preferred_element_type=f32)
        parts = []
        for hh in range(SSD_HPG):
            hd = g * SSD_HPG + hh
            seg = jnp.broadcast_to(a_cum[:, hd:hd + 1], (q, q)) - a_cum_t[hd:hd + 1, :]
            lmat = (jnp.where(causal, jnp.exp(seg), 0.0) * cb).astype(bf16)
            parts.append(jnp.dot(lmat, xdt_bf[:, hd * SSD_HEADDIM:(hd + 1) * SSD_HEADDIM], preferred_element_type=f32))
        y_g = y_g + jnp.concatenate(parts, axis=1)
        if direction == 0:
            y_g = y_g + dskip_ref[:, gl] * xs[:, gl]
        y_ref[:, gl] = y_g


def _ssd_scan_kernel(xf_ref, pf_ref, nf_ref, dtf_ref, dttf_ref, xb_ref, pb_ref, nb_ref, dtb_ref, dttb_ref,
                     cw_ref, cb_ref, bias_r_ref, a_r_ref, bias_c_ref, a_c_ref, e_ref, dskip_ref,
                     yf_ref, yb_ref, state_ref, *, ctx_chunks, lat_chunks):
    j = pl.program_id(1)

    @pl.when(j == 0)
    def _reset():
        state_ref[...] = jnp.zeros(state_ref.shape, jnp.float32)

    in_ctx = j < ctx_chunks
    pos_f = jnp.where(in_ctx, j, j - ctx_chunks)
    seg_len = jnp.where(in_ctx, ctx_chunks, lat_chunks)
    pos_b = seg_len - 1 - pos_f
    shared = (cw_ref, cb_ref, bias_r_ref, a_r_ref, bias_c_ref, a_c_ref, e_ref, dskip_ref, state_ref)
    _ssd_direction(0, xf_ref, pf_ref, nf_ref, dtf_ref, dttf_ref, pos_f > 0, pos_f < seg_len - 1, *shared, yf_ref)
    _ssd_direction(1, xb_ref, pb_ref, nb_ref, dtb_ref, dttb_ref, pos_b > 0, pos_b < seg_len - 1, *shared, yb_ref)


def _ssd_out_kernel(yf_ref, yb_ref, z_ref, x_ref, mod_ref, nw_ref, wo_ref, lng_ref, lnb_ref, o_ref):
    f32 = jnp.float32
    dm = x_ref.shape[1]
    z = z_ref[...]
    y = (yf_ref[...] + yb_ref[...]) * (z * jax.nn.sigmoid(z))
    y = y * lax.rsqrt(jnp.mean(y * y, axis=-1, keepdims=True) + RMS_EPS) * nw_ref[...]
    out = jnp.dot(y.astype(jnp.bfloat16), wo_ref[...], preferred_element_type=f32)
    mod = mod_ref[0]
    v = DEEPNORM_ALPHA * x_ref[...] + mod[:, 2 * dm:3 * dm] * out
    mu = jnp.mean(v, axis=-1, keepdims=True)
    vc = v - mu
    var = jnp.mean(vc * vc, axis=-1, keepdims=True)
    o_ref[...] = vc * lax.rsqrt(var + LN_EPS) * lng_ref[...] + lnb_ref[...]


def ssd_ln_pallas(xa, mod3, n_lat, seq, bsz, w_in, conv_w, conv_b, dt_bias, a_log, d, norm_w, w_out, ln_g, ln_b):
    f32, bf16 = jnp.float32, jnp.bfloat16
    n_tok, dm = xa.shape
    ctx_len = (n_tok - n_lat) // bsz
    tm = TOK_TM
    q = SSD_CHUNK
    nh = SSD_HEADS
    pad = SSD_DT_PAD - 2 * nh
    w_ext = jnp.concatenate([w_in, jnp.zeros((dm, pad), w_in.dtype)], axis=1).astype(bf16)
    n_in = w_ext.shape[1]
    wdt_t = w_ext[:, SSD_D_INNER + SSD_CONV_DIM:].T
    mod_idx = functools.partial(_mod_row_index, tm=tm, n_lat=n_lat, seq=seq, bsz=bsz)
    whole = lambda shape: pl.BlockSpec(shape, lambda *_: (0,) * len(shape))
    params = pltpu.CompilerParams(dimension_semantics=("arbitrary",), vmem_limit_bytes=VMEM_LIMIT)
    z, xbc, dt, dtt = pl.pallas_call(
        _ssd_in_kernel,
        grid=(n_tok // tm,),
        in_specs=[
            pl.BlockSpec((tm, dm), lambda i: (i, 0)),
            pl.BlockSpec((1, 1, 6 * dm), lambda i: (mod_idx(i), 0, 0)),
            whole((dm, n_in)), whole((SSD_DT_PAD, dm)),
        ],
        out_specs=[
            pl.BlockSpec((tm, SSD_D_INNER), lambda i: (i, 0)),
            pl.BlockSpec((tm, SSD_CONV_DIM), lambda i: (i, 0)),
            pl.BlockSpec((tm, SSD_DT_PAD), lambda i: (i, 0)),
            pl.BlockSpec((SSD_DT_PAD, tm), lambda i: (0, i)),
        ],
        out_shape=[
            jax.ShapeDtypeStruct((n_tok, SSD_D_INNER), f32),
            jax.ShapeDtypeStruct((n_tok, SSD_CONV_DIM), f32),
            jax.ShapeDtypeStruct((n_tok, SSD_DT_PAD), f32),
            jax.ShapeDtypeStruct((SSD_DT_PAD, n_tok), f32),
        ],
        compiler_params=params,
        name="ssd_in_proj",
    )(xa, mod3, w_ext, wdt_t)

    ctx_chunks, lat_chunks = ctx_len // q, seq // q
    ctx_base = n_lat // q
    n_chunk_total = n_tok // q

    def chunk_of(b, j, backward):
        in_ctx = j < ctx_chunks
        pos = jnp.where(in_ctx, j, j - ctx_chunks)
        seg = jnp.where(in_ctx, ctx_chunks, lat_chunks)
        pos = jnp.where(backward, seg - 1 - pos, pos)
        return jnp.where(in_ctx, ctx_base + b * ctx_chunks + pos, b * lat_chunks + pos)

    sub = q // 8
    def dir_specs(backward):
        ch = lambda b, j: chunk_of(b, j, backward)
        return [
            pl.BlockSpec((q, SSD_CONV_DIM), lambda b, j: (ch(b, j), 0)),
            pl.BlockSpec((8, SSD_CONV_DIM), lambda b, j: (jnp.maximum(ch(b, j) * sub - 1, 0), 0)),
            pl.BlockSpec((8, SSD_CONV_DIM), lambda b, j: (jnp.minimum((ch(b, j) + 1) * sub, n_chunk_total * sub - 1), 0)),
            pl.BlockSpec((q, SSD_DT_PAD), lambda b, j: (ch(b, j), 0)),
            pl.BlockSpec((SSD_DT_PAD, q), lambda b, j: (0, ch(b, j))),
        ]

    a = -jnp.exp(a_log.astype(f32))
    expand = jnp.repeat(jnp.eye(nh, dtype=f32), SSD_HEADDIM, axis=1)
    e3 = jnp.concatenate([expand, expand, expand], axis=0).astype(bf16)
    dskip = jnp.repeat(d, SSD_HEADDIM)[None, :]
    yf, yb = pl.pallas_call(
        functools.partial(_ssd_scan_kernel, ctx_chunks=ctx_chunks, lat_chunks=lat_chunks),
        grid=(bsz, ctx_chunks + lat_chunks),
        in_specs=dir_specs(False) + dir_specs(True) + [
            whole((SSD_CONV, SSD_CONV_DIM)), whole((1, SSD_CONV_DIM)),
            whole((2, nh)), whole((2, nh)), whole((2, nh, 1)), whole((2, nh, 1)),
            whole((3 * nh, SSD_D_INNER)), whole((1, SSD_D_INNER)),
        ],
        out_specs=[
            pl.BlockSpec((q, SSD_D_INNER), lambda b, j: (chunk_of(b, j, False), 0)),
            pl.BlockSpec((q, SSD_D_INNER), lambda b, j: (chunk_of(b, j, True), 0)),
        ],
        out_shape=[jax.ShapeDtypeStruct((n_tok, SSD_D_INNER), f32)] * 2,
        scratch_shapes=[pltpu.VMEM((2, SSD_GROUPS, SSD_STATE, SSD_GW), f32)],
        compiler_params=pltpu.CompilerParams(dimension_semantics=("arbitrary", "arbitrary"), vmem_limit_bytes=VMEM_LIMIT),
        name="ssd_scan",
    )(xbc, xbc, xbc, dt, dtt, xbc, xbc, xbc, dt, dtt,
      conv_w, conv_b.reshape(1, -1), dt_bias, a, dt_bias.reshape(2, nh, 1), a.reshape(2, nh, 1), e3, dskip)

    return pl.pallas_call(
        _ssd_out_kernel,
        grid=(n_tok // tm,),
        in_specs=[
            pl.BlockSpec((tm, SSD_D_INNER), lambda i: (i, 0)),
            pl.BlockSpec((tm, SSD_D_INNER), lambda i: (i, 0)),
            pl.BlockSpec((tm, SSD_D_INNER), lambda i: (i, 0)),
            pl.BlockSpec((tm, dm), lambda i: (i, 0)),
            pl.BlockSpec((1, 1, 6 * dm), lambda i: (mod_idx(i), 0, 0)),
            whole((1, SSD_D_INNER)), whole((SSD_D_INNER, dm)), whole((1, dm)), whole((1, dm)),
        ],
        out_specs=pl.BlockSpec((tm, dm), lambda i: (i, 0)),
        out_shape=jax.ShapeDtypeStruct((n_tok, dm), f32),
        compiler_params=params,
        name="ssd_out_ln",
    )(yf, yb, z, xa, mod3, norm_w.reshape(1, -1), w_out.astype(bf16), ln_g.reshape(1, dm), ln_b.reshape(1, dm))


def kernel(x, c, ctx, c_ctx, mod_w, mod_b, ln_g, ln_b,
           s5_a_re, s5_a_im, s5_log_dt, s5_b_re, s5_b_im, s5_c_re, s5_c_im, s5_d, s5_w_gate, s5_w_val,
           ssd_w_in, ssd_conv_w, ssd_conv_b, ssd_dt_bias, ssd_a_log, ssd_d, ssd_norm_w, ssd_w_out,
           mla_w_down, mla_q_norm, mla_kv_norm, mla_w_uq, mla_w_uk, mla_w_uv, mla_w_o,
           peer_w_q, peer_subkeys, peer_u, peer_v):
    ROWS = x.shape[1] // GRID_W
    pos = grid_positions(ROWS)
    ctx_len = ctx.shape[1]
    c_act = jax.nn.silu(c)
    c_ctx_act = jax.nn.silu(c_ctx)
    bsz, seq_len, dm = x.shape
    n_lat, n_ctx = bsz * seq_len, bsz * ctx_len
    xa = jnp.concatenate([x.reshape(n_lat, dm), ctx.reshape(n_ctx, dm)], axis=0)
    for i in range(DEPTH):
        last = i == DEPTH - 1
        mod_tab = jnp.concatenate([c_act, c_ctx_act[None], jnp.zeros((7 - bsz, dm), x.dtype)], axis=0) @ mod_w[i] + mod_b[i]
        mod3 = mod_tab.reshape(8, 1, 6 * dm)
        kind, j = i % N_MIXERS, i // N_MIXERS
        if kind == 0:
            prep = _s5_prep(s5_a_re[j], s5_a_im[j], s5_log_dt[j], s5_b_re[j], s5_b_im[j], s5_c_re[j], s5_c_im[j], s5_d[j])
            ya = s5_ssm_pallas(xa, mod_tab, n_lat, seq_len, bsz, prep)
            if last:
                ya, xa = ya[:n_lat], xa[:n_lat]
            xa = glu_ln_pallas(ya, xa, mod3, n_lat, seq_len, bsz, s5_w_val[j], s5_w_gate[j], ln_g[i, 0], ln_b[i, 0])
        elif kind == 1:
            xa = ssd_ln_pallas(xa, mod3, n_lat, seq_len, bsz, ssd_w_in[j], ssd_conv_w[j], ssd_conv_b[j], ssd_dt_bias[j],
                               ssd_a_log[j], ssd_d[j], ssd_norm_w[j], ssd_w_out[j], ln_g[i, 0], ln_b[i, 0])
            if last:
                xa = xa[:n_lat]
        else:
            xa = mla_ln_pallas(xa, mod3, n_lat, seq_len, bsz, mla_w_down[j], mla_q_norm[j], mla_kv_norm[j],
                               mla_w_uq[j], mla_w_uk[j], mla_w_uv[j], mla_w_o[j], ln_g[i, 0], ln_b[i, 0])
            if last:
                xa = xa[:n_lat]
        xa = peer_ln_pallas(xa, mod3, n_lat, seq_len, bsz, peer_w_q[i], peer_subkeys[i], peer_u[i], peer_v[i],
                            ln_g[i, 1], ln_b[i, 1])
    return xa[:n_lat].reshape(bsz, seq_len, dm)
```

```python
import math
import functools
import jax
import jax.numpy as jnp
from jax import lax
import numpy as np
from jax.experimental import pallas as pl
from jax.experimental.pallas import tpu as pltpu

D_MODEL = 1024
BATCH = 4
SEQ = 4096
DEPTH = 4

GRID_W = 64
CTX_LEN = 256
N_MIXERS = 3
DEEPNORM_ALPHA = (2.0 * DEPTH) ** 0.25
LN_EPS = 1e-5
RMS_EPS = 1e-6
ROPE_BASE = 10000.0

S5_GROUP = 16
S5_GROUPS = D_MODEL // S5_GROUP
S5_STATE = 64

SSD_D_INNER = 2 * D_MODEL
SSD_HEADDIM = 64
SSD_HEADS = SSD_D_INNER // SSD_HEADDIM
SSD_GROUPS = 4
SSD_STATE = 128
SSD_CONV = 3
SSD_CHUNK = 128
SSD_CONV_DIM = SSD_D_INNER + 2 * SSD_GROUPS * SSD_STATE
SSD_IN_DIM = SSD_D_INNER + SSD_CONV_DIM + 2 * SSD_HEADS

MLA_HEADS = 16
MLA_Q_RANK = 256
MLA_KV_RANK = 128
MLA_NOPE = 64
MLA_ROPE = 32
MLA_V = 64
MLA_BLOCK = 128

PEER_HEADS = 8
PEER_KEYS = 128
PEER_EXPERTS = PEER_KEYS * PEER_KEYS
PEER_QDIM = 256
PEER_TOPK = 16
PEER_BLOCK = 128


def layer_norm(x, g, b):
    xf = x.astype(jnp.float32)
    mu = jnp.mean(xf, axis=-1, keepdims=True)
    var = jnp.mean(jnp.square(xf - mu), axis=-1, keepdims=True)
    return ((xf - mu) * lax.rsqrt(var + LN_EPS)).astype(x.dtype) * g + b


def _ln_kernel(x_ref, g_ref, b_ref, o_ref):
    xf = x_ref[...]
    mu = jnp.mean(xf, axis=-1, keepdims=True)
    xc = xf - mu
    var = jnp.mean(xc * xc, axis=-1, keepdims=True)
    o_ref[...] = xc * lax.rsqrt(var + LN_EPS) * g_ref[...] + b_ref[...]


def layer_norm_pallas(x, g, b):
    shp = x.shape
    x2 = x.reshape(-1, shp[-1])
    n, d = x2.shape
    tb = 512
    out = pl.pallas_call(
        _ln_kernel,
        grid=(n // tb,),
        in_specs=[pl.BlockSpec((tb, d), lambda i: (i, 0)),
                  pl.BlockSpec((1, d), lambda i: (0, 0)),
                  pl.BlockSpec((1, d), lambda i: (0, 0))],
        out_specs=pl.BlockSpec((tb, d), lambda i: (i, 0)),
        out_shape=jax.ShapeDtypeStruct((n, d), jnp.float32),
        name="final_ln",
    )(x2, g.reshape(1, d), b.reshape(1, d))
    return out.reshape(shp)


def rms_norm(x, g):
    xf = x.astype(jnp.float32)
    return (xf * lax.rsqrt(jnp.mean(jnp.square(xf), axis=-1, keepdims=True) + RMS_EPS)).astype(x.dtype) * g


def modulate(x, shift, scale):
    return x * (1.0 + scale) + shift


def grid_positions(rows):
    row = jnp.repeat(jnp.arange(rows, dtype=jnp.float32), GRID_W)
    col = jnp.tile(jnp.arange(GRID_W, dtype=jnp.float32), rows)
    return row, col


def rope_axial(x, row, col):
    half = x.shape[-1] // 2
    quarter = half // 2
    freqs = ROPE_BASE ** (-jnp.arange(quarter, dtype=jnp.float32) / quarter)

    def rot(xp, pos):
        ang = pos[:, None] * freqs
        cos = jnp.cos(ang)[None, :, None, :].astype(x.dtype)
        sin = jnp.sin(ang)[None, :, None, :].astype(x.dtype)
        x1, x2 = xp[..., :quarter], xp[..., quarter:]
        return jnp.concatenate([x1 * cos - x2 * sin, x2 * cos + x1 * sin], axis=-1)

    return jnp.concatenate([rot(x[..., :half], row), rot(x[..., half:], col)], axis=-1)


def _lin_rec(left, right):
    a1, b1 = left
    a2, b2 = right
    return a1 * a2, a2 * b1 + b2


def s5_scan(u, a_bar, b_bar, c_mat, init, reverse):
    bu = jnp.einsum('lgc,gpc->lgp', u.astype(jnp.float32), b_bar)
    a = jnp.broadcast_to(a_bar, bu.shape)
    a_cum, s = lax.associative_scan(_lin_rec, (a, bu), reverse=reverse, axis=0)
    s = s + a_cum * init
    y = jnp.einsum('lgp,gcp->lgc', s, c_mat).real
    final = s[0] if reverse else s[-1]
    return y, final


def s5_direction(u_ctx, u_lat, a_re, a_im, log_dt, b_re, b_im, c_re, c_im, reverse):
    lam = lax.complex(a_re.astype(jnp.float32), a_im.astype(jnp.float32))
    a_bar = jnp.exp(lam * jnp.exp(log_dt.astype(jnp.float32))[:, None])
    b_mat = lax.complex(b_re.astype(jnp.float32), b_im.astype(jnp.float32))
    b_bar = ((a_bar - 1.0) / lam)[..., None] * b_mat
    c_mat = lax.complex(c_re.astype(jnp.float32), c_im.astype(jnp.float32))

    def per_sample(args):
        uc, ul = args
        yc, sc = s5_scan(uc, a_bar, b_bar, c_mat, jnp.zeros_like(a_bar), reverse)
        yl, _ = s5_scan(ul, a_bar, b_bar, c_mat, sc, reverse)
        return yc, yl

    return lax.map(per_sample, (u_ctx, u_lat))


def s5_mixer(h, hc, a_re, a_im, log_dt, b_re, b_im, c_re, c_im, d, w_gate, w_val):
    bsz, seq_len, _ = h.shape
    ctx_len = hc.shape[1]
    u = h.reshape(bsz, seq_len, S5_GROUPS, S5_GROUP)
    uc = hc.reshape(bsz, ctx_len, S5_GROUPS, S5_GROUP)
    y_l = d * h
    y_c = d * hc
    for direction in range(2):
        yc, yl = s5_direction(uc, u, a_re[direction], a_im[direction], log_dt[direction],
                              b_re[direction], b_im[direction], c_re[direction], c_im[direction],
                              direction == 1)
        y_l = y_l + yl.reshape(bsz, seq_len, D_MODEL).astype(h.dtype)
        y_c = y_c + yc.reshape(bsz, ctx_len, D_MODEL).astype(h.dtype)

    def glu(y):
        g = jax.nn.gelu(y)
        return (g @ w_val) * jax.nn.sigmoid(g @ w_gate)

    return glu(y_l), glu(y_c)


def depthwise_conv_centred(x, w, b):
    k_w = w.shape[0]
    pad = k_w // 2
    seq_len = x.shape[1]
    xp = jnp.pad(x, ((0, 0), (pad, pad), (0, 0)))
    out = b
    for k in range(k_w):
        out = out + xp[:, k:k + seq_len] * w[k]
    return out


def ssd_scan(x, dt, a, bm, cm, init):
    bsz, seq_len, n_heads, p_dim = x.shape
    n_grp, n_st = bm.shape[2], bm.shape[3]
    hg = n_heads // n_grp
    q_len = SSD_CHUNK
    nc = seq_len // q_len
    xc = x.reshape(bsz, nc, q_len, n_grp, hg, p_dim)
    dtc = dt.reshape(bsz, nc, q_len, n_grp, hg).astype(jnp.float32)
    bc = bm.reshape(bsz, nc, q_len, n_grp, n_st)
    cc = cm.reshape(bsz, nc, q_len, n_grp, n_st)
    a_cum = jnp.cumsum(dtc * a.reshape(n_grp, hg), axis=2)
    xdt = xc * dtc[..., None].astype(x.dtype)
    tri = jnp.tril(jnp.ones((q_len, q_len), dtype=bool))
    seg = a_cum[:, :, :, None] - a_cum[:, :, None]
    decay = jnp.exp(jnp.where(tri[:, :, None, None], seg, -jnp.inf)).astype(x.dtype)
    cb = jnp.einsum('bcqgn,bcsgn->bcgqs', cc, bc)
    y_diag = jnp.einsum('bcgqs,bcqsgh,bcsghp->bcqghp', cb, decay, xdt)
    decay_states = jnp.exp(a_cum[:, :, -1:] - a_cum).astype(x.dtype)
    states = jnp.einsum('bcsgn,bcsgh,bcsghp->bcghpn', bc, decay_states, xdt)
    chunk_decay = jnp.exp(a_cum[:, :, -1]).astype(x.dtype)

    def step(carry, inp):
        dec, st = inp
        return carry * dec[..., None, None] + st, carry

    final, prev = lax.scan(step, init, (jnp.moveaxis(chunk_decay, 1, 0), jnp.moveaxis(states, 1, 0)))
    y_off = jnp.einsum('bcqgn,cbghpn,bcqgh->bcqghp', cc, prev, jnp.exp(a_cum).astype(x.dtype))
    return (y_diag + y_off).reshape(bsz, seq_len, n_heads, p_dim), final


def ssd_mixer(h, hc, w_in, conv_w, conv_b, dt_bias, a_log, d, norm_w, w_out):
    a = -jnp.exp(a_log.astype(jnp.float32))

    def flip(t):
        return jnp.flip(t, axis=1)

    def bidir(t, inits):
        bsz, seq_len = t.shape[0], t.shape[1]
        z, xbc, dt = jnp.split(t @ w_in, [SSD_D_INNER, SSD_D_INNER + SSD_CONV_DIM], axis=-1)
        xbc = jax.nn.silu(depthwise_conv_centred(xbc, conv_w, conv_b))
        xs, bm, cm = jnp.split(xbc, [SSD_D_INNER, SSD_D_INNER + SSD_GROUPS * SSD_STATE], axis=-1)
        xs = xs.reshape(bsz, seq_len, SSD_HEADS, SSD_HEADDIM)
        bm = bm.reshape(bsz, seq_len, SSD_GROUPS, SSD_STATE)
        cm = cm.reshape(bsz, seq_len, SSD_GROUPS, SSD_STATE)
        dt = jax.nn.softplus(dt.reshape(bsz, seq_len, 2, SSD_HEADS) + dt_bias)
        y_f, s_f = ssd_scan(xs, dt[:, :, 0], a[0], bm, cm, inits[0])
        y_b, s_b = ssd_scan(flip(xs), flip(dt[:, :, 1]), a[1], flip(bm), flip(cm), inits[1])
        y = y_f + flip(y_b) + d[:, None] * xs
        y = rms_norm(y.reshape(z.shape) * jax.nn.silu(z), norm_w)
        return y @ w_out, (s_f, s_b)

    zero = jnp.zeros((h.shape[0], SSD_GROUPS, SSD_HEADS // SSD_GROUPS, SSD_HEADDIM, SSD_STATE), h.dtype)
    out_c, states_c = bidir(hc, (zero, zero))
    out_l, _ = bidir(h, states_c)
    return out_l, out_c


def mla_project(t, w_down, q_norm, kv_norm, w_uq, w_uk, w_uv, pos):
    cq, ckv, kr = jnp.split(t @ w_down, [MLA_Q_RANK, MLA_Q_RANK + MLA_KV_RANK], axis=-1)
    q = jnp.einsum('btr,rhd->bthd', rms_norm(cq, q_norm), w_uq)
    ckv = rms_norm(ckv, kv_norm)
    k_nope = jnp.einsum('btr,rhd->bthd', ckv, w_uk)
    v = jnp.einsum('btr,rhd->bthd', ckv, w_uv)
    q_nope, q_rope = q[..., :MLA_NOPE], q[..., MLA_NOPE:]
    kr = kr[:, :, None, :]
    if pos is not None:
        q_rope = rope_axial(q_rope, pos[0], pos[1])
        kr = rope_axial(kr, pos[0], pos[1])
    k = jnp.concatenate([k_nope, jnp.broadcast_to(kr, k_nope.shape[:3] + (MLA_ROPE,))], axis=-1)
    q = jnp.concatenate([q_nope, q_rope], axis=-1)
    return q, k, v


def attend(q, k, v):
    s = jnp.einsum('bqhd,bkhd->bhqk', q, k).astype(jnp.float32) * (MLA_NOPE + MLA_ROPE) ** -0.5
    p = jax.nn.softmax(s, axis=-1).astype(v.dtype)
    return jnp.einsum('bhqk,bkhd->bqhd', p, v)


def mla_mixer(h, hc, pos, w_down, q_norm, kv_norm, w_uq, w_uk, w_uv, w_o):
    bsz, seq_len, _ = h.shape
    qc, kc, vc = mla_project(hc, w_down, q_norm, kv_norm, w_uq, w_uk, w_uv, None)
    ql, kl, vl = mla_project(h, w_down, q_norm, kv_norm, w_uq, w_uk, w_uv, pos)
    out_c = attend(qc, kc, vc).reshape(bsz, hc.shape[1], MLA_HEADS * MLA_V)
    k_all = jnp.concatenate([kc, kl], axis=1)
    v_all = jnp.concatenate([vc, vl], axis=1)
    nb = seq_len // MLA_BLOCK
    qb = jnp.moveaxis(ql.reshape(bsz, nb, MLA_BLOCK, MLA_HEADS, MLA_NOPE + MLA_ROPE), 1, 0)
    out_l = lax.map(lambda qblk: attend(qblk, k_all, v_all), qb)
    out_l = jnp.moveaxis(out_l, 0, 1).reshape(bsz, seq_len, MLA_HEADS * MLA_V)
    return out_l @ w_o, out_c @ w_o


def peer_ffn(h, w_q, subkeys, u_tab, v_tab):
    bsz, seq_len, dm = h.shape
    half = PEER_QDIM // 2
    q = jnp.einsum('btd,dhk->bthk', h, w_q)
    s1 = jnp.einsum('bthk,hnk->bthn', q[..., :half], subkeys[:, 0])
    s2 = jnp.einsum('bthk,hnk->bthn', q[..., half:], subkeys[:, 1])
    v1, i1 = lax.top_k(s1, PEER_TOPK)
    v2, i2 = lax.top_k(s2, PEER_TOPK)
    n_cand = PEER_TOPK * PEER_TOPK
    cand = (v1[..., :, None] + v2[..., None, :]).reshape(bsz, seq_len, PEER_HEADS, n_cand)
    cidx = (i1[..., :, None] * PEER_KEYS + i2[..., None, :]).reshape(bsz, seq_len, PEER_HEADS, n_cand)
    best, sel = lax.top_k(cand, PEER_TOPK)
    eidx = jnp.take_along_axis(cidx, sel, axis=-1)
    gate = jax.nn.softmax(best.astype(jnp.float32), axis=-1).astype(h.dtype)
    n_blk = (bsz * seq_len) // PEER_BLOCK
    n_sel = PEER_HEADS * PEER_TOPK
    hb = h.reshape(n_blk, PEER_BLOCK, dm)
    ib = eidx.reshape(n_blk, PEER_BLOCK, n_sel)
    gb = gate.reshape(n_blk, PEER_BLOCK, n_sel)

    def block(args):
        hk, ik, gk = args
        u = jnp.take(u_tab, ik, axis=0)
        act = jax.nn.gelu(jnp.einsum('td,ted->te', hk, u))
        v = jnp.take(v_tab, ik, axis=0)
        return jnp.einsum('te,ted->td', gk * act, v)

    return lax.map(block, (hb, ib, gb)).reshape(bsz, seq_len, dm)


PEER_TM = 512
PEER_EB = 1024
PEER_LANES = 128
PEER_ACT_ROWS = 256
VMEM_LIMIT = 56 * 1024 * 1024
NEG_INF = float("-inf")


def _extract_top(s, n_take):
    n = s.shape[0]
    row = lax.broadcasted_iota(jnp.int32, s.shape, 0)
    vals = []
    for k in range(n_take):
        m = jnp.max(s, axis=0, keepdims=True)
        vals.append(m)
        if k + 1 < n_take:
            first = jnp.min(jnp.where(s == m, row, n), axis=0, keepdims=True)
            s = jnp.where(row == first, NEG_INF, s)
    return vals


def _top16_sorted(s):
    n = PEER_TOPK
    x = [s[8 * r:8 * r + 8, :] for r in range(n)]

    def exchange(i, l, descending):
        hi, lo = jnp.maximum(x[i], x[l]), jnp.minimum(x[i], x[l])
        x[i], x[l] = (hi, lo) if descending else (lo, hi)

    def merge_bitonic():
        j = n // 2
        while j >= 1:
            for i in range(n):
                if i ^ j > i:
                    exchange(i, i ^ j, True)
            j //= 2

    k = 2
    while k < n:
        j = k // 2
        while j >= 1:
            for i in range(n):
                if i ^ j > i:
                    exchange(i, i ^ j, (i & k) == 0)
            j //= 2
        k *= 2
    merge_bitonic()
    for shift in (4, 2, 1):
        other = [pltpu.roll(x[n - 1 - r], shift, axis=0) for r in range(n)]
        for r in range(n):
            x[r] = jnp.maximum(x[r], other[r])
        merge_bitonic()
    return [x[r][0:1, :] for r in range(n)]


def _dot3(a_hi, a_lo, b_hi, b_lo):
    f32 = jnp.float32
    return (jnp.dot(a_hi, b_hi, preferred_element_type=f32)
            + jnp.dot(a_hi, b_lo, preferred_element_type=f32)
            + jnp.dot(a_lo, b_hi, preferred_element_type=f32))


def _split_bf16(v):
    hi = v.astype(jnp.bfloat16)
    lo = (v - hi.astype(jnp.float32)).astype(jnp.bfloat16)
    return hi, lo


def _peer_kernel(x_ref, mod_ref, wq_ref, skh_ref, skl_ref, u_ref, vt_ref, vtl_ref, lng_ref, lnb_ref,
                 o_ref,
                 ht_ref, thr_ref, e1_ref, s2_ref, e2_ref, v1_ref, v2_ref, cand_ref,
                 act_ref, gact_ref, acc_ref):
    f32 = jnp.float32
    bf16 = jnp.bfloat16
    eb = pl.program_id(1)
    n_eb = pl.num_programs(1)
    dm = x_ref.shape[1]
    a_per = PEER_EB // PEER_KEYS
    half = PEER_QDIM // 2

    @pl.when(eb == 0)
    def _prologue():
        mod = mod_ref[0]
        h = x_ref[...] * (1.0 + mod[:, 4 * dm:5 * dm]) + mod[:, 3 * dm:4 * dm]
        ht = h.T.astype(bf16)
        ht_ref[...] = ht
        qt = jnp.dot(wq_ref[...], ht, preferred_element_type=f32)
        cand_ref[...] = jnp.full(cand_ref.shape, NEG_INF, f32)
        gact_ref[...] = jnp.zeros(gact_ref.shape, bf16)
        for hd in range(PEER_HEADS):
            q1_hi, q1_lo = _split_bf16(qt[hd * PEER_QDIM: hd * PEER_QDIM + half])
            q2_hi, q2_lo = _split_bf16(qt[hd * PEER_QDIM + half: (hd + 1) * PEER_QDIM])
            s1 = _dot3(skh_ref[hd, 0], skl_ref[hd, 0], q1_hi, q1_lo)
            s2 = _dot3(skh_ref[hd, 1], skl_ref[hd, 1], q2_hi, q2_lo)
            top1 = _top16_sorted(s1)
            top2 = _top16_sorted(s2)
            for k in range(PEER_TOPK):
                v1_ref[k:k + 1, :] = top1[k]
                v2_ref[k:k + 1, :] = top2[k]
            off = 0
            for i in range(PEER_TOPK):
                cnt = PEER_TOPK // (i + 1)
                cand_ref[off:off + cnt, :] = v1_ref[i:i + 1, :] + v2_ref[0:cnt, :]
                off += cnt
            best = _extract_top(cand_ref[...], PEER_TOPK)
            z = jnp.zeros_like(best[0])
            for k in range(PEER_TOPK):
                z = z + jnp.exp(best[k] - best[0])
            tau = best[PEER_TOPK - 1]
            thr = jnp.full(s1.shape, jnp.inf, f32)
            for k in range(PEER_TOPK):
                thr = jnp.where(s1 + top2[k] >= tau, top2[k], thr)
            thr_ref[hd] = thr
            s2_ref[hd] = s2
            e1_ref[hd] = jnp.exp(s1 - top1[0])
            e2_ref[hd] = jnp.exp(s2 - top2[0]) / z
        acc_ref[...] = jnp.zeros(acc_ref.shape, f32)

    par = eb % 2
    for r0 in range(0, PEER_EB, PEER_ACT_ROWS):
        act_ref[r0:r0 + PEER_ACT_ROWS, :] = jnp.dot(u_ref[r0:r0 + PEER_ACT_ROWS, :], ht_ref[...],
                                                    preferred_element_type=f32)
    acc_ref[...] += jnp.dot(vt_ref[...], gact_ref[1 - par], preferred_element_type=f32)
    a_base = pl.multiple_of(eb * a_per, a_per)
    blk = (PEER_KEYS, PEER_LANES)
    for a in range(a_per):
        for lg in range(0, x_ref.shape[0], PEER_LANES):
            lanes = slice(lg, lg + PEER_LANES)
            g = None
            for hd in range(PEER_HEADS):
                thr = jnp.broadcast_to(thr_ref[hd, pl.ds(a_base, a_per), lanes][a:a + 1, :], blk)
                e1a = jnp.broadcast_to(e1_ref[hd, pl.ds(a_base, a_per), lanes][a:a + 1, :], blk)
                w = jnp.where(s2_ref[hd, :, lanes] >= thr, e2_ref[hd, :, lanes], 0.0) * e1a
                g = w if g is None else g + w
            rows = slice(a * PEER_KEYS, (a + 1) * PEER_KEYS)
            gact_ref[par, rows, lanes] = (g * jax.nn.gelu(act_ref[rows, lanes])).astype(bf16)

    @pl.when(eb == n_eb - 1)
    def _epilogue():
        mod = mod_ref[0]
        ffn_t = acc_ref[...] + jnp.dot(vtl_ref[...], gact_ref[par], preferred_element_type=f32)
        y = DEEPNORM_ALPHA * x_ref[...] + mod[:, 5 * dm:6 * dm] * ffn_t.T
        mu = jnp.mean(y, axis=-1, keepdims=True)
        yc = y - mu
        var = jnp.mean(yc * yc, axis=-1, keepdims=True)
        o_ref[...] = yc * lax.rsqrt(var + LN_EPS) * lng_ref[...] + lnb_ref[...]


def _mod_row_index(i, tm, n_lat, seq, bsz):
    return jnp.where(i * tm < n_lat, (i * tm) // seq, bsz)


def peer_ln_pallas(xa, mod3, n_lat, seq, bsz, w_q, subkeys, u_tab, v_tab, ln_g, ln_b):
    n_tok, dm = xa.shape
    tm = PEER_TM
    f32, bf16 = jnp.float32, jnp.bfloat16
    wq_t = w_q.reshape(dm, PEER_HEADS * PEER_QDIM).T
    wq_bf = wq_t.astype(bf16)
    sk_hi = subkeys.astype(bf16)
    sk_lo = (subkeys - sk_hi.astype(f32)).astype(bf16)
    u_bf = u_tab.astype(bf16)
    vt_bf = v_tab.T.astype(bf16)
    n_eb = PEER_EXPERTS // PEER_EB
    qd = PEER_HEADS * PEER_QDIM
    n_cand = sum(PEER_TOPK // (i + 1) for i in range(PEER_TOPK))
    n_cand_pad = -(-n_cand // 8) * 8
    mod_idx = functools.partial(_mod_row_index, tm=tm, n_lat=n_lat, seq=seq, bsz=bsz)
    return pl.pallas_call(
        _peer_kernel,
        grid=(n_tok // tm, n_eb),
        in_specs=[
            pl.BlockSpec((tm, dm), lambda i, e: (i, 0)),
            pl.BlockSpec((1, 1, 6 * dm), lambda i, e: (mod_idx(i), 0, 0)),
            pl.BlockSpec((qd, dm), lambda i, e: (0, 0)),
            pl.BlockSpec((PEER_HEADS, 2, PEER_KEYS, PEER_QDIM // 2), lambda i, e: (0, 0, 0, 0)),
            pl.BlockSpec((PEER_HEADS, 2, PEER_KEYS, PEER_QDIM // 2), lambda i, e: (0, 0, 0, 0)),
            pl.BlockSpec((PEER_EB, dm), lambda i, e: (e, 0)),
            pl.BlockSpec((dm, PEER_EB), lambda i, e: (0, jnp.maximum(e - 1, 0))),
            pl.BlockSpec((dm, PEER_EB), lambda i, e: (0, n_eb - 1)),
            pl.BlockSpec((1, dm), lambda i, e: (0, 0)),
            pl.BlockSpec((1, dm), lambda i, e: (0, 0)),
        ],
        out_specs=pl.BlockSpec((tm, dm), lambda i, e: (i, 0)),
        out_shape=jax.ShapeDtypeStruct((n_tok, dm), f32),
        scratch_shapes=[
            pltpu.VMEM((dm, tm), bf16),
            pltpu.VMEM((PEER_HEADS, PEER_KEYS, tm), f32),
            pltpu.VMEM((PEER_HEADS, PEER_KEYS, tm), f32),
            pltpu.VMEM((PEER_HEADS, PEER_KEYS, tm), f32),
            pltpu.VMEM((PEER_HEADS, PEER_KEYS, tm), f32),
            pltpu.VMEM((PEER_TOPK, tm), f32),
            pltpu.VMEM((PEER_TOPK, tm), f32),
            pltpu.VMEM((n_cand_pad, tm), f32),
            pltpu.VMEM((PEER_EB, tm), f32),
            pltpu.VMEM((2, PEER_EB, tm), bf16),
            pltpu.VMEM((dm, tm), f32),
        ],
        compiler_params=pltpu.CompilerParams(
            dimension_semantics=("arbitrary", "arbitrary"),
            vmem_limit_bytes=VMEM_LIMIT),
        name="peer_ln",
    )(xa, mod3, wq_bf, sk_hi, sk_lo, u_bf, vt_bf, vt_bf, ln_g.reshape(1, dm), ln_b.reshape(1, dm))


S5_CHUNK = 16
S5_LEVELS = 9


def _s5_prep(a_re, a_im, log_dt, b_re, b_im, c_re, c_im, d):
    f32, bf16 = jnp.float32, jnp.bfloat16
    hp = lax.Precision.HIGHEST
    n_g, n_p, q = S5_GROUPS, S5_STATE, S5_CHUNK
    dt = jnp.exp(log_dt.astype(f32))[..., None]

    def apow(n):
        mag = jnp.exp(a_re * dt * n)
        ang = a_im * dt * n
        return mag * jnp.cos(ang), mag * jnp.sin(ang)

    ar1, ai1 = apow(1.0)
    den = a_re * a_re + a_im * a_im
    nr, ni = ar1 - 1.0, ai1
    cr = (nr * a_re + ni * a_im) / den
    ci = (ni * a_re - nr * a_im) / den
    bbr = cr[..., None] * b_re - ci[..., None] * b_im
    bbi = cr[..., None] * b_im + ci[..., None] * b_re
    lag = jnp.arange(q + 1, dtype=f32)[:, None, None, None]
    pr, pi = apow(lag)
    mr = pr[..., None] * bbr - pi[..., None] * bbi
    mi = pr[..., None] * bbi + pi[..., None] * bbr
    kmat = (jnp.einsum('dgcp,ndgpk->ndgck', c_re, mr, precision=hp)
            - jnp.einsum('dgcp,ndgpk->ndgck', c_im, mi, precision=hp))
    r_idx = jnp.arange(q)[:, None]
    t_idx = jnp.arange(q)[None, :]

    def toeplitz(kd, lagm):
        blk = kd[jnp.clip(lagm, 0, q)]
        blk = jnp.where((lagm >= 0)[:, :, None, None, None], blk, 0.0)
        return blk.transpose(2, 0, 4, 1, 3).reshape(n_g, q * S5_GROUP, q * S5_GROUP)

    eye = jnp.eye(q * S5_GROUP, dtype=f32)
    dvec = jnp.tile(d.reshape(n_g, 1, S5_GROUP), (1, q, 1)).reshape(n_g, q * S5_GROUP)
    tsum = toeplitz(kmat[:, 0], t_idx - r_idx) + toeplitz(kmat[:, 1], r_idx - t_idx) + eye[None] * dvec[:, None, :]

    def w_in(direction, exps):
        wr = mr[exps, direction].transpose(1, 0, 3, 2).reshape(n_g, q * S5_GROUP, n_p)
        wi = mi[exps, direction].transpose(1, 0, 3, 2).reshape(n_g, q * S5_GROUP, n_p)
        return wr, wi

    def w_out(direction, exps):
        pre, pim = pr[exps, direction], pi[exps, direction]
        cre, cim = c_re[direction], c_im[direction]
        wre = cre[None] * pre[:, :, None, :] - cim[None] * pim[:, :, None, :]
        wim = -(cre[None] * pim[:, :, None, :] + cim[None] * pre[:, :, None, :])
        return (wre.transpose(1, 3, 0, 2).reshape(n_g, n_p, q * S5_GROUP),
                wim.transpose(1, 3, 0, 2).reshape(n_g, n_p, q * S5_GROUP))

    steps = jnp.arange(q)
    win = [w_in(0, q - 1 - steps), w_in(1, steps)]
    wout = [w_out(0, steps + 1), w_out(1, q - steps)]
    zc = jnp.zeros((n_g, q * S5_GROUP, n_p), f32)
    zr = jnp.zeros((n_g, n_p, q * S5_GROUP), f32)
    win_p, wout_p = [], []
    for direction in range(2):
        wr, wi = win[direction]
        even = jnp.concatenate([wr, zc, wi, zc], axis=2)
        odd = jnp.concatenate([zc, wr, zc, wi], axis=2)
        is_odd = (jnp.arange(n_g) % 2 == 1)[:, None, None]
        win_p.append(jnp.where(is_odd, odd, even))
        vr, vi = wout[direction]
        even = jnp.concatenate([vr, zr, vi, zr], axis=1)
        odd = jnp.concatenate([zr, vr, zr, vi], axis=1)
        wout_p.append(jnp.where(is_odd, odd, even))
    win_p = jnp.stack(win_p, axis=1).reshape(n_g // 2, 2, 2, 4 * n_p, 4 * n_p).transpose(0, 2, 1, 3, 4)
    wout_p = jnp.stack(wout_p, axis=1).reshape(n_g // 2, 2, 2, 4 * n_p, 4 * n_p).transpose(0, 2, 1, 3, 4)
    lvl = (q * 2.0 ** jnp.arange(S5_LEVELS, dtype=f32))[:, None, None, None]
    lr, li = apow(lvl)
    pw = jnp.stack([lr, li], axis=1)
    pw = pw.transpose(3, 2, 0, 1, 4).reshape(n_g // 2, 2, 2, S5_LEVELS, 2, n_p)
    pw = pw.transpose(0, 2, 3, 4, 1, 5).reshape(n_g // 2, 2, 2 * S5_LEVELS, 2 * n_p)
    return tsum.astype(bf16).reshape(n_g // 2, 2, q * S5_GROUP, q * S5_GROUP), win_p.astype(bf16), wout_p.astype(bf16), pw


def _shift_rows(v, r, up):
    n = v.shape[0]
    row = lax.broadcasted_iota(jnp.int32, v.shape, 0)
    if up:
        return jnp.where(row < n - r, pltpu.roll(v, n - r, axis=0), 0.0)
    return jnp.where(row >= r, pltpu.roll(v, r, axis=0), 0.0)


def _chunk_scan(re, im, pw, rows_per_chunk, up):
    n_chunks = re.shape[0] // rows_per_chunk
    level, s = 0, 1
    while s < n_chunks:
        ar, ai = pw[2 * level:2 * level + 1, :], pw[2 * level + 1:2 * level + 2, :]
        sre = _shift_rows(re, s * rows_per_chunk, up)
        sim = _shift_rows(im, s * rows_per_chunk, up)
        re, im = re + ar * sre - ai * sim, im + ar * sim + ai * sre
        level, s = level + 1, 2 * s
    return re, im


def _s5_kernel(u_ref, scl_ref, shl_ref, scc_ref, shc_ref, t_ref, win_ref, wout_ref, pw_ref, y_ref, *, bsz, ctx_chunks):
    f32, bf16 = jnp.float32, jnp.bfloat16
    n = u_ref.shape[1]
    width = u_ref.shape[2]
    rc = ctx_chunks * bsz
    hs = []
    for gi in range(2):
        xv = u_ref[gi].reshape(n // 8, 8, width)
        hl = xv * scl_ref[gi][None] + shl_ref[gi][None]
        hc = xv * scc_ref[gi][None] + shc_ref[gi][None]
        slab = lax.broadcasted_iota(jnp.int32, xv.shape, 0)
        hs.append(jnp.where(slab < rc // 8, hc, hl).reshape(n, width).astype(bf16))
    y = [jnp.dot(hs[gi], t_ref[0, gi], preferred_element_type=f32) for gi in range(2)]
    half = width // 2
    for direction in range(2):
        sloc = (jnp.dot(hs[0], win_ref[0, direction, 0], preferred_element_type=f32)
                + jnp.dot(hs[1], win_ref[0, direction, 1], preferred_element_type=f32))
        re, im = sloc[:, :half], sloc[:, half:]
        pw = pw_ref[0, direction]
        if direction == 0:
            sre, sim = _chunk_scan(_shift_rows(re, bsz, False), _shift_rows(im, bsz, False), pw, bsz, False)
        else:
            cre, cim = _chunk_scan(_shift_rows(re[:rc], bsz, True), _shift_rows(im[:rc], bsz, True), pw, bsz, True)
            ar, ai = pw[0:1, :], pw[1:2, :]
            fre = ar * cre[0:8] - ai * cim[0:8] + re[0:8]
            fim = ar * cim[0:8] + ai * cre[0:8] + im[0:8]
            row8 = lax.broadcasted_iota(jnp.int32, fre.shape, 0)
            tre = jnp.where(row8 >= 8 - bsz, pltpu.roll(fre, 8 - bsz, axis=0), 0.0)
            tim = jnp.where(row8 >= 8 - bsz, pltpu.roll(fim, 8 - bsz, axis=0), 0.0)
            lre, lim = _shift_rows(re[rc:], bsz, True), _shift_rows(im[rc:], bsz, True)
            lre = jnp.concatenate([lre[:-8], lre[-8:] + tre], axis=0)
            lim = jnp.concatenate([lim[:-8], lim[-8:] + tim], axis=0)
            lre, lim = _chunk_scan(lre, lim, pw, bsz, True)
            sre = jnp.concatenate([cre, lre], axis=0)
            sim = jnp.concatenate([cim, lim], axis=0)
        s_in = jnp.concatenate([sre, sim], axis=1).astype(bf16)
        for gi in range(2):
            y[gi] = y[gi] + jnp.dot(s_in, wout_ref[0, direction, gi], preferred_element_type=f32)
    for gi in range(2):
        y_ref[gi] = y[gi]


def s5_ssm_pallas(xa, mod_tab, n_lat, seq, bsz, prep):
    tsum, win_p, wout_p, pw = prep
    f32 = jnp.float32
    n_tok, dm = xa.shape
    ctx_len = (n_tok - n_lat) // bsz
    q = S5_CHUNK
    n_chunks = (seq + ctx_len) // q
    width = q * S5_GROUP
    full = jnp.concatenate([xa[n_lat:].reshape(bsz, ctx_len, dm), xa[:n_lat].reshape(bsz, seq, dm)], axis=1)
    u = full.reshape(bsz, n_chunks, q, S5_GROUPS, S5_GROUP).transpose(3, 1, 0, 2, 4).reshape(S5_GROUPS, n_chunks * bsz, width)

    def tile(vec_rows):
        rows = vec_rows.shape[0]
        t = vec_rows.reshape(rows, S5_GROUPS, 1, S5_GROUP)
        t = jnp.broadcast_to(t, (rows, S5_GROUPS, q, S5_GROUP)).reshape(rows, S5_GROUPS, width)
        return jnp.tile(t.transpose(1, 0, 2), (1, 8 // rows, 1))

    scl, shl = tile(1.0 + mod_tab[:bsz, dm:2 * dm]), tile(mod_tab[:bsz, 0:dm])
    scc = tile(jnp.broadcast_to(1.0 + mod_tab[bsz:bsz + 1, dm:2 * dm], (bsz, dm)))
    shc = tile(jnp.broadcast_to(mod_tab[bsz:bsz + 1, 0:dm], (bsz, dm)))
    n_rows = n_chunks * bsz
    tile_spec = pl.BlockSpec((2, 8, width), lambda p: (p, 0, 0))
    y = pl.pallas_call(
        functools.partial(_s5_kernel, bsz=bsz, ctx_chunks=ctx_len // q),
        grid=(S5_GROUPS // 2,),
        in_specs=[
            pl.BlockSpec((2, n_rows, width), lambda p: (p, 0, 0)),
            tile_spec, tile_spec, tile_spec, tile_spec,
            pl.BlockSpec((1, 2, width, width), lambda p: (p, 0, 0, 0)),
            pl.BlockSpec((1, 2, 2, width, width), lambda p: (p, 0, 0, 0, 0)),
            pl.BlockSpec((1, 2, 2, width, width), lambda p: (p, 0, 0, 0, 0)),
            pl.BlockSpec((1, 2, 2 * S5_LEVELS, 2 * S5_STATE), lambda p: (p, 0, 0, 0)),
        ],
        out_specs=pl.BlockSpec((2, n_rows, width), lambda p: (p, 0, 0)),
        out_shape=jax.ShapeDtypeStruct((S5_GROUPS, n_rows, width), f32),
        compiler_params=pltpu.CompilerParams(dimension_semantics=("arbitrary",), vmem_limit_bytes=VMEM_LIMIT),
        name="s5_ssm",
    )(u, scl, shl, scc, shc, tsum, win_p, wout_p, pw)
    yf = y.reshape(S5_GROUPS, n_chunks, bsz, q, S5_GROUP).transpose(2, 1, 3, 0, 4).reshape(bsz, seq + ctx_len, dm)
    return jnp.concatenate([yf[:, ctx_len:].reshape(n_lat, dm), yf[:, :ctx_len].reshape(bsz * ctx_len, dm)], axis=0)


def _glu_ln_kernel(y_ref, x_ref, mod_ref, wv_ref, wg_ref, lng_ref, lnb_ref, o_ref):
    f32 = jnp.float32
    dm = x_ref.shape[1]
    g = jax.nn.gelu(y_ref[...]).astype(jnp.bfloat16)
    val = jnp.dot(g, wv_ref[...], preferred_element_type=f32)
    gate = jnp.dot(g, wg_ref[...], preferred_element_type=f32)
    out = val * jax.nn.sigmoid(gate)
    mod = mod_ref[0]
    z = DEEPNORM_ALPHA * x_ref[...] + mod[:, 2 * dm:3 * dm] * out
    mu = jnp.mean(z, axis=-1, keepdims=True)
    zc = z - mu
    var = jnp.mean(zc * zc, axis=-1, keepdims=True)
    o_ref[...] = zc * lax.rsqrt(var + LN_EPS) * lng_ref[...] + lnb_ref[...]


TOK_TM = 512


def glu_ln_pallas(ya, xa, mod3, n_lat, seq, bsz, w_val, w_gate, ln_g, ln_b):
    n_tok, dm = ya.shape
    tm = TOK_TM
    bf16 = jnp.bfloat16
    mod_idx = functools.partial(_mod_row_index, tm=tm, n_lat=n_lat, seq=seq, bsz=bsz)
    return pl.pallas_call(
        _glu_ln_kernel,
        grid=(n_tok // tm,),
        in_specs=[
            pl.BlockSpec((tm, dm), lambda i: (i, 0)),
            pl.BlockSpec((tm, dm), lambda i: (i, 0)),
            pl.BlockSpec((1, 1, 6 * dm), lambda i: (mod_idx(i), 0, 0)),
            pl.BlockSpec((dm, dm), lambda i: (0, 0)),
            pl.BlockSpec((dm, dm), lambda i: (0, 0)),
            pl.BlockSpec((1, dm), lambda i: (0, 0)),
            pl.BlockSpec((1, dm), lambda i: (0, 0)),
        ],
        out_specs=pl.BlockSpec((tm, dm), lambda i: (i, 0)),
        out_shape=jax.ShapeDtypeStruct((n_tok, dm), jnp.float32),
        compiler_params=pltpu.CompilerParams(dimension_semantics=("arbitrary",), vmem_limit_bytes=VMEM_LIMIT),
        name="glu_ln",
    )(ya, xa, mod3, w_val.astype(bf16), w_gate.astype(bf16), ln_g.reshape(1, dm), ln_b.reshape(1, dm))


MLA_HD = 128
MLA_TQ = 256
MLA_HB = 4


def _mla_prep(w_down, w_uq, w_uk, w_uv):
    bf16 = jnp.bfloat16
    quarter = MLA_ROPE // 4
    swap = np.concatenate([np.arange(quarter, 2 * quarter), np.arange(0, quarter),
                           np.arange(3 * quarter, 4 * quarter), np.arange(2 * quarter, 3 * quarter)])
    dm = w_down.shape[0]
    w_cq = w_down[:, :MLA_Q_RANK]
    w_ckv = w_down[:, MLA_Q_RANK:MLA_Q_RANK + MLA_KV_RANK]
    w_kr = w_down[:, MLA_Q_RANK + MLA_KV_RANK:]
    wd_t = jnp.concatenate([w_cq, w_ckv, jnp.zeros((dm, MLA_NOPE), w_down.dtype), w_kr, w_kr[:, swap]], axis=1).T
    rope = w_uq[:, :, MLA_NOPE:]
    wuq_t = jnp.concatenate([w_uq, rope[:, :, swap]], axis=2).reshape(MLA_Q_RANK, MLA_HEADS * MLA_HD).T
    wuk = jnp.concatenate([w_uk, jnp.zeros((MLA_KV_RANK, MLA_HEADS, MLA_HD - MLA_NOPE), w_uk.dtype)], axis=2)
    wuk = wuk.reshape(MLA_KV_RANK, MLA_HEADS * MLA_HD)
    wuv_t = w_uv.reshape(MLA_KV_RANK, MLA_HEADS * MLA_V).T
    return wd_t.astype(bf16), wuq_t.astype(bf16), wuk.astype(bf16), wuv_t.astype(bf16)


def _rope_tables(seq, tm):
    quarter = MLA_ROPE // 4
    freqs = ROPE_BASE ** (-jnp.arange(quarter, dtype=jnp.float32) / quarter)
    t = jnp.arange(seq, dtype=jnp.float32)
    row, col = jnp.floor(t / GRID_W), t - GRID_W * jnp.floor(t / GRID_W)
    ang_r, ang_c = freqs[:, None] * row[None, :], freqs[:, None] * col[None, :]
    cos32 = jnp.concatenate([jnp.cos(ang_r), jnp.cos(ang_r), jnp.cos(ang_c), jnp.cos(ang_c)], axis=0)
    sin32 = jnp.concatenate([-jnp.sin(ang_r), jnp.sin(ang_r), -jnp.sin(ang_c), jnp.sin(ang_c)], axis=0)
    ones = jnp.ones((MLA_NOPE, seq + tm), jnp.float32)
    zeros = jnp.zeros((MLA_HD - MLA_NOPE - MLA_ROPE, seq + tm), jnp.float32)
    cos_t = jnp.concatenate([ones, jnp.concatenate([cos32, jnp.ones((MLA_ROPE, tm))], axis=1), zeros], axis=0)
    sin_t = jnp.concatenate([0.0 * ones, jnp.concatenate([sin32, jnp.zeros((MLA_ROPE, tm))], axis=1), zeros], axis=0)
    return cos_t, sin_t


def _mla_proj_kernel(x_ref, mod_ref, cos_ref, sin_ref, wd_ref, wuq_ref, wuk_ref, wuv_ref, qn_ref, kvn_ref,
                     qt_ref, k_ref, vt_ref):
    f32, bf16 = jnp.float32, jnp.bfloat16
    dm = x_ref.shape[1]
    mod = mod_ref[0]
    h = x_ref[...] * (1.0 + mod[:, dm:2 * dm]) + mod[:, 0:dm]
    ht = h.T.astype(bf16)
    dt = jnp.dot(wd_ref[...], ht, preferred_element_type=f32)
    cq, ckv, kr = dt[:MLA_Q_RANK], dt[MLA_Q_RANK:MLA_Q_RANK + MLA_KV_RANK], dt[MLA_Q_RANK + MLA_KV_RANK:]
    cqn = cq * lax.rsqrt(jnp.mean(cq * cq, axis=0, keepdims=True) + RMS_EPS) * qn_ref[...]
    ckvn = ckv * lax.rsqrt(jnp.mean(ckv * ckv, axis=0, keepdims=True) + RMS_EPS) * kvn_ref[...]
    cos_t, sin_t = cos_ref[...], sin_ref[...]

    def rope(v):
        shifted = jnp.concatenate([v[MLA_ROPE:], v[:MLA_ROPE]], axis=0)
        return v * cos_t + shifted * sin_t

    scale = (MLA_NOPE + MLA_ROPE) ** -0.5
    q_all = jnp.dot(wuq_ref[...], cqn.astype(bf16), preferred_element_type=f32)
    for hd in range(MLA_HEADS):
        qt_ref[hd] = (rope(q_all[hd * MLA_HD:(hd + 1) * MLA_HD]) * scale).astype(bf16)
    ckvn_bf = ckvn.astype(bf16)
    vt_ref[...] = jnp.dot(wuv_ref[...], ckvn_bf, preferred_element_type=f32).astype(bf16)
    k_all = jnp.dot(ckvn.T.astype(bf16), wuk_ref[...], preferred_element_type=f32)
    kr_rows = rope(kr).T
    for hd in range(MLA_HEADS):
        k_ref[hd] = (k_all[:, hd * MLA_HD:(hd + 1) * MLA_HD] + kr_rows).astype(bf16)


def _mla_attn_kernel(*refs, with_latent):
    f32, bf16 = jnp.float32, jnp.bfloat16
    if with_latent:
        q_ref, kl_ref, kc_ref, vl_ref, vc_ref, o_ref = refs
    else:
        q_ref, kc_ref, vc_ref, o_ref = refs
    for i in range(q_ref.shape[0]):
        rows = slice(i * MLA_V, (i + 1) * MLA_V)
        q = q_ref[i]
        s_c = jnp.dot(kc_ref[i], q, preferred_element_type=f32)
        m = jnp.max(s_c, axis=0, keepdims=True)
        if with_latent:
            s_l = jnp.dot(kl_ref[i], q, preferred_element_type=f32)
            m = jnp.maximum(m, jnp.max(s_l, axis=0, keepdims=True))
        p_c = jnp.exp(s_c - m)
        den = jnp.sum(p_c, axis=0, keepdims=True)
        o = jnp.dot(vc_ref[rows, :], p_c.astype(bf16), preferred_element_type=f32)
        if with_latent:
            p_l = jnp.exp(s_l - m)
            den = den + jnp.sum(p_l, axis=0, keepdims=True)
            o = o + jnp.dot(vl_ref[rows, :], p_l.astype(bf16), preferred_element_type=f32)
        o_ref[rows, :] = (o / den).astype(bf16)


def _mla_out_kernel(ot_ref, x_ref, mod_ref, wo_ref, lng_ref, lnb_ref, o_ref):
    f32 = jnp.float32
    dm = x_ref.shape[1]
    attn = ot_ref[...].astype(f32).T.astype(jnp.bfloat16)
    out = jnp.dot(attn, wo_ref[...], preferred_element_type=f32)
    mod = mod_ref[0]
    z = DEEPNORM_ALPHA * x_ref[...] + mod[:, 2 * dm:3 * dm] * out
    mu = jnp.mean(z, axis=-1, keepdims=True)
    zc = z - mu
    var = jnp.mean(zc * zc, axis=-1, keepdims=True)
    o_ref[...] = zc * lax.rsqrt(var + LN_EPS) * lng_ref[...] + lnb_ref[...]


def mla_ln_pallas(xa, mod3, n_lat, seq, bsz, w_down, q_norm, kv_norm, w_uq, w_uk, w_uv, w_o, ln_g, ln_b):
    f32, bf16 = jnp.float32, jnp.bfloat16
    n_tok, dm = xa.shape
    n_ctx = n_tok - n_lat
    ctx_len = n_ctx // bsz
    tm = TOK_TM
    wd_t, wuq_t, wuk, wuv_t = _mla_prep(w_down, w_uq, w_uk, w_uv)
    cos_t, sin_t = _rope_tables(seq, tm)
    mod_idx = functools.partial(_mod_row_index, tm=tm, n_lat=n_lat, seq=seq, bsz=bsz)
    tiles_per_sample = seq // tm
    pos_idx = lambda i: jnp.where(i * tm < n_lat, i % tiles_per_sample, tiles_per_sample)
    n_hd, n_dn = MLA_HEADS * MLA_HD, wd_t.shape[0]
    whole = lambda shape: pl.BlockSpec(shape, lambda i: (0,) * len(shape))
    params = pltpu.CompilerParams(dimension_semantics=("arbitrary",), vmem_limit_bytes=VMEM_LIMIT)
    qt, k, vt = pl.pallas_call(
        _mla_proj_kernel,
        grid=(n_tok // tm,),
        in_specs=[
            pl.BlockSpec((tm, dm), lambda i: (i, 0)),
            pl.BlockSpec((1, 1, 6 * dm), lambda i: (mod_idx(i), 0, 0)),
            pl.BlockSpec((MLA_HD, tm), lambda i: (0, pos_idx(i))),
            pl.BlockSpec((MLA_HD, tm), lambda i: (0, pos_idx(i))),
            whole((n_dn, dm)), whole((n_hd, MLA_Q_RANK)), whole((MLA_KV_RANK, n_hd)),
            whole((MLA_HEADS * MLA_V, MLA_KV_RANK)), whole((MLA_Q_RANK, 1)), whole((MLA_KV_RANK, 1)),
        ],
        out_specs=[
            pl.BlockSpec((MLA_HEADS, MLA_HD, tm), lambda i: (0, 0, i)),
            pl.BlockSpec((MLA_HEADS, tm, MLA_HD), lambda i: (0, i, 0)),
            pl.BlockSpec((MLA_HEADS * MLA_V, tm), lambda i: (0, i)),
        ],
        out_shape=[
            jax.ShapeDtypeStruct((MLA_HEADS, MLA_HD, n_tok), bf16),
            jax.ShapeDtypeStruct((MLA_HEADS, n_tok, MLA_HD), bf16),
            jax.ShapeDtypeStruct((MLA_HEADS * MLA_V, n_tok), bf16),
        ],
        compiler_params=params,
        name="mla_proj",
    )(xa, mod3, cos_t, sin_t, wd_t, wuq_t, wuk, wuv_t, q_norm.reshape(-1, 1), kv_norm.reshape(-1, 1))

    tq, hb = MLA_TQ, MLA_HB
    n_qt = seq // tq
    ctx_blk = n_lat // ctx_len
    params3 = pltpu.CompilerParams(dimension_semantics=("arbitrary",) * 3, vmem_limit_bytes=VMEM_LIMIT)
    ot_lat = pl.pallas_call(
        functools.partial(_mla_attn_kernel, with_latent=True),
        grid=(bsz, MLA_HEADS // hb, n_qt),
        in_specs=[
            pl.BlockSpec((hb, MLA_HD, tq), lambda b, hd, t: (hd, 0, b * n_qt + t)),
            pl.BlockSpec((hb, seq, MLA_HD), lambda b, hd, t: (hd, b, 0)),
            pl.BlockSpec((hb, ctx_len, MLA_HD), lambda b, hd, t: (hd, ctx_blk + b, 0)),
            pl.BlockSpec((hb * MLA_V, seq), lambda b, hd, t: (hd, b)),
            pl.BlockSpec((hb * MLA_V, ctx_len), lambda b, hd, t: (hd, ctx_blk + b)),
        ],
        out_specs=pl.BlockSpec((hb * MLA_V, tq), lambda b, hd, t: (hd, b * n_qt + t)),
        out_shape=jax.ShapeDtypeStruct((MLA_HEADS * MLA_V, n_lat), bf16),
        compiler_params=params3,
        name="mla_attn_latent",
    )(qt, k, k, vt, vt)
    ot_ctx = pl.pallas_call(
        functools.partial(_mla_attn_kernel, with_latent=False),
        grid=(bsz, MLA_HEADS // hb, 1),
        in_specs=[
            pl.BlockSpec((hb, MLA_HD, ctx_len), lambda b, hd, t: (hd, 0, ctx_blk + b)),
            pl.BlockSpec((hb, ctx_len, MLA_HD), lambda b, hd, t: (hd, ctx_blk + b, 0)),
            pl.BlockSpec((hb * MLA_V, ctx_len), lambda b, hd, t: (hd, ctx_blk + b)),
        ],
        out_specs=pl.BlockSpec((hb * MLA_V, ctx_len), lambda b, hd, t: (hd, b)),
        out_shape=jax.ShapeDtypeStruct((MLA_HEADS * MLA_V, n_ctx), bf16),
        compiler_params=params3,
        name="mla_attn_context",
    )(qt, k, vt)
    ot = jnp.concatenate([ot_lat, ot_ctx], axis=1)
    return pl.pallas_call(
        _mla_out_kernel,
        grid=(n_tok // tm,),
        in_specs=[
            pl.BlockSpec((MLA_HEADS * MLA_V, tm), lambda i: (0, i)),
            pl.BlockSpec((tm, dm), lambda i: (i, 0)),
            pl.BlockSpec((1, 1, 6 * dm), lambda i: (mod_idx(i), 0, 0)),
            whole((MLA_HEADS * MLA_V, dm)), whole((1, dm)), whole((1, dm)),
        ],
        out_specs=pl.BlockSpec((tm, dm), lambda i: (i, 0)),
        out_shape=jax.ShapeDtypeStruct((n_tok, dm), f32),
        compiler_params=params,
        name="mla_out_ln",
    )(ot, xa, mod3, w_o.astype(bf16), ln_g.reshape(1, dm), ln_b.reshape(1, dm))


SSD_DT_PAD = 128
SSD_HPG = SSD_HEADS // SSD_GROUPS
SSD_GW = SSD_HPG * SSD_HEADDIM


def _ssd_in_kernel(x_ref, mod_ref, w_ref, wdt_ref, z_ref, xbc_ref, dt_ref, dtt_ref):
    f32, bf16 = jnp.float32, jnp.bfloat16
    dm = x_ref.shape[1]
    mod = mod_ref[0]
    h = x_ref[...] * (1.0 + mod[:, dm:2 * dm]) + mod[:, 0:dm]
    proj = jnp.dot(h.astype(bf16), w_ref[...], preferred_element_type=f32)
    z_ref[...] = proj[:, :SSD_D_INNER]
    xbc_ref[...] = proj[:, SSD_D_INNER:SSD_D_INNER + SSD_CONV_DIM]
    dt_ref[...] = proj[:, SSD_D_INNER + SSD_CONV_DIM:]
    dtt_ref[...] = jnp.dot(wdt_ref[...], h.T.astype(bf16), preferred_element_type=f32)


def _softplus(v):
    return jnp.maximum(v, 0.0) + jnp.log(1.0 + jnp.exp(-jnp.abs(v)))


def _split3(v):
    f32, bf16 = jnp.float32, jnp.bfloat16
    hi = v.astype(bf16)
    r1 = v - hi.astype(f32)
    mid = r1.astype(bf16)
    lo = (r1 - mid.astype(f32)).astype(bf16)
    return jnp.concatenate([hi, mid, lo], axis=1)


def _ssd_direction(direction, xbc_ref, prev_ref, next_ref, dt_ref, dtt_ref, has_prev, has_next,
                   cw_ref, cb_ref, bias_r_ref, a_r_ref, bias_c_ref, a_c_ref, e_ref, dskip_ref, state_ref, y_ref):
    f32, bf16 = jnp.float32, jnp.bfloat16
    hp = lax.Precision.HIGHEST
    q = xbc_ref.shape[0]
    nh = SSD_HEADS
    xm = xbc_ref[...]
    row = lax.broadcasted_iota(jnp.int32, xm.shape, 0)
    before = jnp.where(has_prev, prev_ref[7:8, :], 0.0)
    after = jnp.where(has_next, next_ref[0:1, :], 0.0)
    x_dn = jnp.where(row == 0, before, pltpu.roll(xm, 1, axis=0))
    x_up = jnp.where(row == q - 1, after, pltpu.roll(xm, q - 1, axis=0))
    conv = cb_ref[...] + x_dn * cw_ref[0:1, :] + xm * cw_ref[1:2, :] + x_up * cw_ref[2:3, :]
    conv = conv * jax.nn.sigmoid(conv)
    xs = conv[:, :SSD_D_INNER]
    gn = SSD_GROUPS * SSD_STATE
    bm, cm = conv[:, SSD_D_INNER:SSD_D_INNER + gn], conv[:, SSD_D_INNER + gn:]
    hs = slice(direction * nh, (direction + 1) * nh)
    dt = _softplus(dt_ref[:, hs] + bias_r_ref[direction:direction + 1, :])
    dtt = _softplus(dtt_ref[hs, :] + bias_c_ref[direction])
    r_i = lax.broadcasted_iota(jnp.int32, (q, q), 0)
    c_i = lax.broadcasted_iota(jnp.int32, (q, q), 1)
    causal = (r_i >= c_i) if direction == 0 else (r_i <= c_i)
    tri = causal.astype(f32)
    tri_t = ((c_i >= r_i) if direction == 0 else (c_i <= r_i)).astype(f32)
    a_cum = jnp.dot(tri, dt * a_r_ref[direction:direction + 1, :], precision=hp, preferred_element_type=f32)
    a_cum_t = jnp.dot(dtt * a_c_ref[direction], tri_t, precision=hp, preferred_element_type=f32)
    last = q - 1 if direction == 0 else 0
    a_exp = jnp.dot(_split3(a_cum), e_ref[...], preferred_element_type=f32)
    dt_exp = jnp.dot(_split3(dt), e_ref[...], preferred_element_type=f32)
    xdt = xs * dt_exp
    a_tot = a_exp[last:last + 1, :]
    xw = (xdt * jnp.exp(a_tot - a_exp)).astype(bf16)
    grow = jnp.exp(a_exp)
    carry = jnp.exp(a_tot)
    xdt_bf = xdt.astype(bf16)
    for g in range(SSD_GROUPS):
        gl = slice(g * SSD_GW, (g + 1) * SSD_GW)
        nl = slice(g * SSD_STATE, (g + 1) * SSD_STATE)
        bm_g, cm_g = bm[:, nl], cm[:, nl].astype(bf16)
        prev = state_ref[direction, g]
        y_g = jnp.dot(cm_g, prev.astype(bf16), preferred_element_type=f32) * grow[:, gl]
        states = jnp.dot(bm_g.T.astype(bf16), xw[:, gl], preferred_element_type=f32)
        state_ref[direction, g] = prev * carry[:, gl] + states
        cb = lax.dot_general(cm_g, bm_g.astype(bf16), (((1,), (1,)), ((), ())), preferred_element_type=f32)
        parts = []
        for hh in range(SSD_HPG):
            hd = g * SSD_HPG + hh
            seg = jnp.broadcast_to(a_cum[:, hd:hd + 1], (q, q)) - a_cum_t[hd:hd + 1, :]
            lmat = (jnp.where(causal, jnp.exp(seg), 0.0) * cb).astype(bf16)
            parts.append(jnp.dot(lmat, xdt_bf[:, hd * SSD_HEADDIM:(hd + 1) * SSD_HEADDIM], preferred_element_type=f32))
        y_g = y_g + jnp.concatenate(parts, axis=1)
        if direction == 0:
            y_g = y_g + dskip_ref[:, gl] * xs[:, gl]
        y_ref[:, gl] = y_g


def _ssd_scan_kernel(xf_ref, pf_ref, nf_ref, dtf_ref, dttf_ref, xb_ref, pb_ref, nb_ref, dtb_ref, dttb_ref,
                     cw_ref, cb_ref, bias_r_ref, a_r_ref, bias_c_ref, a_c_ref, e_ref, dskip_ref,
                     yf_ref, yb_ref, state_ref, *, ctx_chunks, lat_chunks):
    j = pl.program_id(1)

    @pl.when(j == 0)
    def _reset():
        state_ref[...] = jnp.zeros(state_ref.shape, jnp.float32)

    in_ctx = j < ctx_chunks
    pos_f = jnp.where(in_ctx, j, j - ctx_chunks)
    seg_len = jnp.where(in_ctx, ctx_chunks, lat_chunks)
    pos_b = seg_len - 1 - pos_f
    shared = (cw_ref, cb_ref, bias_r_ref, a_r_ref, bias_c_ref, a_c_ref, e_ref, dskip_ref, state_ref)
    _ssd_direction(0, xf_ref, pf_ref, nf_ref, dtf_ref, dttf_ref, pos_f > 0, pos_f < seg_len - 1, *shared, yf_ref)
    _ssd_direction(1, xb_ref, pb_ref, nb_ref, dtb_ref, dttb_ref, pos_b > 0, pos_b < seg_len - 1, *shared, yb_ref)


def _ssd_out_kernel(yf_ref, yb_ref, z_ref, x_ref, mod_ref, nw_ref, wo_ref, lng_ref, lnb_ref, o_ref):
    f32 = jnp.float32
    dm = x_ref.shape[1]
    z = z_ref[...]
    y = (yf_ref[...] + yb_ref[...]) * (z * jax.nn.sigmoid(z))
    y = y * lax.rsqrt(jnp.mean(y * y, axis=-1, keepdims=True) + RMS_EPS) * nw_ref[...]
    out = jnp.dot(y.astype(jnp.bfloat16), wo_ref[...], preferred_element_type=f32)
    mod = mod_ref[0]
    v = DEEPNORM_ALPHA * x_ref[...] + mod[:, 2 * dm:3 * dm] * out
    mu = jnp.mean(v, axis=-1, keepdims=True)
    vc = v - mu
    var = jnp.mean(vc * vc, axis=-1, keepdims=True)
    o_ref[...] = vc * lax.rsqrt(var + LN_EPS) * lng_ref[...] + lnb_ref[...]


def ssd_ln_pallas(xa, mod3, n_lat, seq, bsz, w_in, conv_w, conv_b, dt_bias, a_log, d, norm_w, w_out, ln_g, ln_b):
    f32, bf16 = jnp.float32, jnp.bfloat16
    n_tok, dm = xa.shape
    ctx_len = (n_tok - n_lat) // bsz
    tm = TOK_TM
    q = SSD_CHUNK
    nh = SSD_HEADS
    pad = SSD_DT_PAD - 2 * nh
    w_ext = jnp.concatenate([w_in, jnp.zeros((dm, pad), w_in.dtype)], axis=1).astype(bf16)
    n_in = w_ext.shape[1]
    wdt_t = w_ext[:, SSD_D_INNER + SSD_CONV_DIM:].T
    mod_idx = functools.partial(_mod_row_index, tm=tm, n_lat=n_lat, seq=seq, bsz=bsz)
    whole = lambda shape: pl.BlockSpec(shape, lambda *_: (0,) * len(shape))
    params = pltpu.CompilerParams(dimension_semantics=("arbitrary",), vmem_limit_bytes=VMEM_LIMIT)
    z, xbc, dt, dtt = pl.pallas_call(
        _ssd_in_kernel,
        grid=(n_tok // tm,),
        in_specs=[
            pl.BlockSpec((tm, dm), lambda i: (i, 0)),
            pl.BlockSpec((1, 1, 6 * dm), lambda i: (mod_idx(i), 0, 0)),
            whole((dm, n_in)), whole((SSD_DT_PAD, dm)),
        ],
        out_specs=[
            pl.BlockSpec((tm, SSD_D_INNER), lambda i: (i, 0)),
            pl.BlockSpec((tm, SSD_CONV_DIM), lambda i: (i, 0)),
            pl.BlockSpec((tm, SSD_DT_PAD), lambda i: (i, 0)),
            pl.BlockSpec((SSD_DT_PAD, tm), lambda i: (0, i)),
        ],
        out_shape=[
            jax.ShapeDtypeStruct((n_tok, SSD_D_INNER), f32),
            jax.ShapeDtypeStruct((n_tok, SSD_CONV_DIM), f32),
            jax.ShapeDtypeStruct((n_tok, SSD_DT_PAD), f32),
            jax.ShapeDtypeStruct((SSD_DT_PAD, n_tok), f32),
        ],
        compiler_params=params,
        name="ssd_in_proj",
    )(xa, mod3, w_ext, wdt_t)

    ctx_chunks, lat_chunks = ctx_len // q, seq // q
    ctx_base = n_lat // q
    n_chunk_total = n_tok // q

    def chunk_of(b, j, backward):
        in_ctx = j < ctx_chunks
        pos = jnp.where(in_ctx, j, j - ctx_chunks)
        seg = jnp.where(in_ctx, ctx_chunks, lat_chunks)
        pos = jnp.where(backward, seg - 1 - pos, pos)
        return jnp.where(in_ctx, ctx_base + b * ctx_chunks + pos, b * lat_chunks + pos)

    sub = q // 8
    def dir_specs(backward):
        ch = lambda b, j: chunk_of(b, j, backward)
        return [
            pl.BlockSpec((q, SSD_CONV_DIM), lambda b, j: (ch(b, j), 0)),
            pl.BlockSpec((8, SSD_CONV_DIM), lambda b, j: (jnp.maximum(ch(b, j) * sub - 1, 0), 0)),
            pl.BlockSpec((8, SSD_CONV_DIM), lambda b, j: (jnp.minimum((ch(b, j) + 1) * sub, n_chunk_total * sub - 1), 0)),
            pl.BlockSpec((q, SSD_DT_PAD), lambda b, j: (ch(b, j), 0)),
            pl.BlockSpec((SSD_DT_PAD, q), lambda b, j: (0, ch(b, j))),
        ]

    a = -jnp.exp(a_log.astype(f32))
    expand = jnp.repeat(jnp.eye(nh, dtype=f32), SSD_HEADDIM, axis=1)
    e3 = jnp.concatenate([expand, expand, expand], axis=0).astype(bf16)
    dskip = jnp.repeat(d, SSD_HEADDIM)[None, :]
    yf, yb = pl.pallas_call(
        functools.partial(_ssd_scan_kernel, ctx_chunks=ctx_chunks, lat_chunks=lat_chunks),
        grid=(bsz, ctx_chunks + lat_chunks),
        in_specs=dir_specs(False) + dir_specs(True) + [
            whole((SSD_CONV, SSD_CONV_DIM)), whole((1, SSD_CONV_DIM)),
            whole((2, nh)), whole((2, nh)), whole((2, nh, 1)), whole((2, nh, 1)),
            whole((3 * nh, SSD_D_INNER)), whole((1, SSD_D_INNER)),
        ],
        out_specs=[
            pl.BlockSpec((q, SSD_D_INNER), lambda b, j: (chunk_of(b, j, False), 0)),
            pl.BlockSpec((q, SSD_D_INNER), lambda b, j: (chunk_of(b, j, True), 0)),
        ],
        out_shape=[jax.ShapeDtypeStruct((n_tok, SSD_D_INNER), f32)] * 2,
        scratch_shapes=[pltpu.VMEM((2, SSD_GROUPS, SSD_STATE, SSD_GW), f32)],
        compiler_params=pltpu.CompilerParams(dimension_semantics=("arbitrary", "arbitrary"), vmem_limit_bytes=VMEM_LIMIT),
        name="ssd_scan",
    )(xbc, xbc, xbc, dt, dtt, xbc, xbc, xbc, dt, dtt,
      conv_w, conv_b.reshape(1, -1), dt_bias, a, dt_bias.reshape(2, nh, 1), a.reshape(2, nh, 1), e3, dskip)

    return pl.pallas_call(
        _ssd_out_kernel,
        grid=(n_tok // tm,),
        in_specs=[
            pl.BlockSpec((tm, SSD_D_INNER), lambda i: (i, 0)),
            pl.BlockSpec((tm, SSD_D_INNER), lambda i: (i, 0)),
            pl.BlockSpec((tm, SSD_D_INNER), lambda i: (i, 0)),
            pl.BlockSpec((tm, dm), lambda i: (i, 0)),
            pl.BlockSpec((1, 1, 6 * dm), lambda i: (mod_idx(i), 0, 0)),
            whole((1, SSD_D_INNER)), whole((SSD_D_INNER, dm)), whole((1, dm)), whole((1, dm)),
        ],
        out_specs=pl.BlockSpec((tm, dm), lambda i: (i, 0)),
        out_shape=jax.ShapeDtypeStruct((n_tok, dm), f32),
        compiler_params=params,
        name="ssd_out_ln",
    )(yf, yb, z, xa, mod3, norm_w.reshape(1, -1), w_out.astype(bf16), ln_g.reshape(1, dm), ln_b.reshape(1, dm))


def kernel(x, c, ctx, c_ctx, mod_w, mod_b, ln_g, ln_b,
           s5_a_re, s5_a_im, s5_log_dt, s5_b_re, s5_b_im, s5_c_re, s5_c_im, s5_d, s5_w_gate, s5_w_val,
           ssd_w_in, ssd_conv_w, ssd_conv_b, ssd_dt_bias, ssd_a_log, ssd_d, ssd_norm_w, ssd_w_out,
           mla_w_down, mla_q_norm, mla_kv_norm, mla_w_uq, mla_w_uk, mla_w_uv, mla_w_o,
           peer_w_q, peer_subkeys, peer_u, peer_v):
    ROWS = x.shape[1] // GRID_W
    pos = grid_positions(ROWS)
    ctx_len = ctx.shape[1]
    c_act = jax.nn.silu(c)
    c_ctx_act = jax.nn.silu(c_ctx)
    bsz, seq_len, dm = x.shape
    n_lat, n_ctx = bsz * seq_len, bsz * ctx_len
    xa = jnp.concatenate([x.reshape(n_lat, dm), ctx.reshape(n_ctx, dm)], axis=0)
    for i in range(DEPTH):
        last = i == DEPTH - 1
        mod_tab = jnp.concatenate([c_act, c_ctx_act[None], jnp.zeros((7 - bsz, dm), x.dtype)], axis=0) @ mod_w[i] + mod_b[i]
        mod3 = mod_tab.reshape(8, 1, 6 * dm)
        kind, j = i % N_MIXERS, i // N_MIXERS
        if kind == 0:
            prep = _s5_prep(s5_a_re[j], s5_a_im[j], s5_log_dt[j], s5_b_re[j], s5_b_im[j], s5_c_re[j], s5_c_im[j], s5_d[j])
            ya = s5_ssm_pallas(xa, mod_tab, n_lat, seq_len, bsz, prep)
            if last:
                ya, xa = ya[:n_lat], xa[:n_lat]
            xa = glu_ln_pallas(ya, xa, mod3, n_lat, seq_len, bsz, s5_w_val[j], s5_w_gate[j], ln_g[i, 0], ln_b[i, 0])
        elif kind == 1:
            xa = ssd_ln_pallas(xa, mod3, n_lat, seq_len, bsz, ssd_w_in[j], ssd_conv_w[j], ssd_conv_b[j], ssd_dt_bias[j],
                               ssd_a_log[j], ssd_d[j], ssd_norm_w[j], ssd_w_out[j], ln_g[i, 0], ln_b[i, 0])
            if last:
                xa = xa[:n_lat]
        else:
            xa = mla_ln_pallas(xa, mod3, n_lat, seq_len, bsz, mla_w_down[j], mla_q_norm[j], mla_kv_norm[j],
                               mla_w_uq[j], mla_w_uk[j], mla_w_uv[j], mla_w_o[j], ln_g[i, 0], ln_b[i, 0])
            if last:
                xa = xa[:n_lat]
        xa = peer_ln_pallas(xa, mod3, n_lat, seq_len, bsz, peer_w_q[i], peer_subkeys[i], peer_u[i], peer_v[i],
                            ln_g[i, 1], ln_b[i, 1])
    return xa[:n_lat].reshape(bsz, seq_len, dm)
```

```python
import math
import functools
import jax
import jax.numpy as jnp
from jax import lax
import numpy as np
from jax.experimental import pallas as pl
from jax.experimental.pallas import tpu as pltpu

D_MODEL = 1024
BATCH = 4
SEQ = 4096
DEPTH = 4

GRID_W = 64
CTX_LEN = 256
N_MIXERS = 3
DEEPNORM_ALPHA = (2.0 * DEPTH) ** 0.25
LN_EPS = 1e-5
RMS_EPS = 1e-6
ROPE_BASE = 10000.0

S5_GROUP = 16
S5_GROUPS = D_MODEL // S5_GROUP
S5_STATE = 64

SSD_D_INNER = 2 * D_MODEL
SSD_HEADDIM = 64
SSD_HEADS = SSD_D_INNER // SSD_HEADDIM
SSD_GROUPS = 4
SSD_STATE = 128
SSD_CONV = 3
SSD_CHUNK = 128
SSD_CONV_DIM = SSD_D_INNER + 2 * SSD_GROUPS * SSD_STATE
SSD_IN_DIM = SSD_D_INNER + SSD_CONV_DIM + 2 * SSD_HEADS

MLA_HEADS = 16
MLA_Q_RANK = 256
MLA_KV_RANK = 128
MLA_NOPE = 64
MLA_ROPE = 32
MLA_V = 64
MLA_BLOCK = 128

PEER_HEADS = 8
PEER_KEYS = 128
PEER_EXPERTS = PEER_KEYS * PEER_KEYS
PEER_QDIM = 256
PEER_TOPK = 16
PEER_BLOCK = 128


def layer_norm(x, g, b):
    xf = x.astype(jnp.float32)
    mu = jnp.mean(xf, axis=-1, keepdims=True)
    var = jnp.mean(jnp.square(xf - mu), axis=-1, keepdims=True)
    return ((xf - mu) * lax.rsqrt(var + LN_EPS)).astype(x.dtype) * g + b


def _ln_kernel(x_ref, g_ref, b_ref, o_ref):
    xf = x_ref[...]
    mu = jnp.mean(xf, axis=-1, keepdims=True)
    xc = xf - mu
    var = jnp.mean(xc * xc, axis=-1, keepdims=True)
    o_ref[...] = xc * lax.rsqrt(var + LN_EPS) * g_ref[...] + b_ref[...]


def layer_norm_pallas(x, g, b):
    shp = x.shape
    x2 = x.reshape(-1, shp[-1])
    n, d = x2.shape
    tb = 512
    out = pl.pallas_call(
        _ln_kernel,
        grid=(n // tb,),
        in_specs=[pl.BlockSpec((tb, d), lambda i: (i, 0)),
                  pl.BlockSpec((1, d), lambda i: (0, 0)),
                  pl.BlockSpec((1, d), lambda i: (0, 0))],
        out_specs=pl.BlockSpec((tb, d), lambda i: (i, 0)),
        out_shape=jax.ShapeDtypeStruct((n, d), jnp.float32),
        name="final_ln",
    )(x2, g.reshape(1, d), b.reshape(1, d))
    return out.reshape(shp)


def rms_norm(x, g):
    xf = x.astype(jnp.float32)
    return (xf * lax.rsqrt(jnp.mean(jnp.square(xf), axis=-1, keepdims=True) + RMS_EPS)).astype(x.dtype) * g


def modulate(x, shift, scale):
    return x * (1.0 + scale) + shift


def grid_positions(rows):
    row = jnp.repeat(jnp.arange(rows, dtype=jnp.float32), GRID_W)
    col = jnp.tile(jnp.arange(GRID_W, dtype=jnp.float32), rows)
    return row, col


def rope_axial(x, row, col):
    half = x.shape[-1] // 2
    quarter = half // 2
    freqs = ROPE_BASE ** (-jnp.arange(quarter, dtype=jnp.float32) / quarter)

    def rot(xp, pos):
        ang = pos[:, None] * freqs
        cos = jnp.cos(ang)[None, :, None, :].astype(x.dtype)
        sin = jnp.sin(ang)[None, :, None, :].astype(x.dtype)
        x1, x2 = xp[..., :quarter], xp[..., quarter:]
        return jnp.concatenate([x1 * cos - x2 * sin, x2 * cos + x1 * sin], axis=-1)

    return jnp.concatenate([rot(x[..., :half], row), rot(x[..., half:], col)], axis=-1)


def _lin_rec(left, right):
    a1, b1 = left
    a2, b2 = right
    return a1 * a2, a2 * b1 + b2


def s5_scan(u, a_bar, b_bar, c_mat, init, reverse):
    bu = jnp.einsum('lgc,gpc->lgp', u.astype(jnp.float32), b_bar)
    a = jnp.broadcast_to(a_bar, bu.shape)
    a_cum, s = lax.associative_scan(_lin_rec, (a, bu), reverse=reverse, axis=0)
    s = s + a_cum * init
    y = jnp.einsum('lgp,gcp->lgc', s, c_mat).real
    final = s[0] if reverse else s[-1]
    return y, final


def s5_direction(u_ctx, u_lat, a_re, a_im, log_dt, b_re, b_im, c_re, c_im, reverse):
    lam = lax.complex(a_re.astype(jnp.float32), a_im.astype(jnp.float32))
    a_bar = jnp.exp(lam * jnp.exp(log_dt.astype(jnp.float32))[:, None])
    b_mat = lax.complex(b_re.astype(jnp.float32), b_im.astype(jnp.float32))
    b_bar = ((a_bar - 1.0) / lam)[..., None] * b_mat
    c_mat = lax.complex(c_re.astype(jnp.float32), c_im.astype(jnp.float32))

    def per_sample(args):
        uc, ul = args
        yc, sc = s5_scan(uc, a_bar, b_bar, c_mat, jnp.zeros_like(a_bar), reverse)
        yl, _ = s5_scan(ul, a_bar, b_bar, c_mat, sc, reverse)
        return yc, yl

    return lax.map(per_sample, (u_ctx, u_lat))


def s5_mixer(h, hc, a_re, a_im, log_dt, b_re, b_im, c_re, c_im, d, w_gate, w_val):
    bsz, seq_len, _ = h.shape
    ctx_len = hc.shape[1]
    u = h.reshape(bsz, seq_len, S5_GROUPS, S5_GROUP)
    uc = hc.reshape(bsz, ctx_len, S5_GROUPS, S5_GROUP)
    y_l = d * h
    y_c = d * hc
    for direction in range(2):
        yc, yl = s5_direction(uc, u, a_re[direction], a_im[direction], log_dt[direction],
                              b_re[direction], b_im[direction], c_re[direction], c_im[direction],
                              direction == 1)
        y_l = y_l + yl.reshape(bsz, seq_len, D_MODEL).astype(h.dtype)
        y_c = y_c + yc.reshape(bsz, ctx_len, D_MODEL).astype(h.dtype)

    def glu(y):
        g = jax.nn.gelu(y)
        return (g @ w_val) * jax.nn.sigmoid(g @ w_gate)

    return glu(y_l), glu(y_c)


def depthwise_conv_centred(x, w, b):
    k_w = w.shape[0]
    pad = k_w // 2
    seq_len = x.shape[1]
    xp = jnp.pad(x, ((0, 0), (pad, pad), (0, 0)))
    out = b
    for k in range(k_w):
        out = out + xp[:, k:k + seq_len] * w[k]
    return out


def ssd_scan(x, dt, a, bm, cm, init):
    bsz, seq_len, n_heads, p_dim = x.shape
    n_grp, n_st = bm.shape[2], bm.shape[3]
    hg = n_heads // n_grp
    q_len = SSD_CHUNK
    nc = seq_len // q_len
    xc = x.reshape(bsz, nc, q_len, n_grp, hg, p_dim)
    dtc = dt.reshape(bsz, nc, q_len, n_grp, hg).astype(jnp.float32)
    bc = bm.reshape(bsz, nc, q_len, n_grp, n_st)
    cc = cm.reshape(bsz, nc, q_len, n_grp, n_st)
    a_cum = jnp.cumsum(dtc * a.reshape(n_grp, hg), axis=2)
    xdt = xc * dtc[..., None].astype(x.dtype)
    tri = jnp.tril(jnp.ones((q_len, q_len), dtype=bool))
    seg = a_cum[:, :, :, None] - a_cum[:, :, None]
    decay = jnp.exp(jnp.where(tri[:, :, None, None], seg, -jnp.inf)).astype(x.dtype)
    cb = jnp.einsum('bcqgn,bcsgn->bcgqs', cc, bc)
    y_diag = jnp.einsum('bcgqs,bcqsgh,bcsghp->bcqghp', cb, decay, xdt)
    decay_states = jnp.exp(a_cum[:, :, -1:] - a_cum).astype(x.dtype)
    states = jnp.einsum('bcsgn,bcsgh,bcsghp->bcghpn', bc, decay_states, xdt)
    chunk_decay = jnp.exp(a_cum[:, :, -1]).astype(x.dtype)

    def step(carry, inp):
        dec, st = inp
        return carry * dec[..., None, None] + st, carry

    final, prev = lax.scan(step, init, (jnp.moveaxis(chunk_decay, 1, 0), jnp.moveaxis(states, 1, 0)))
    y_off = jnp.einsum('bcqgn,cbghpn,bcqgh->bcqghp', cc, prev, jnp.exp(a_cum).astype(x.dtype))
    return (y_diag + y_off).reshape(bsz, seq_len, n_heads, p_dim), final


def ssd_mixer(h, hc, w_in, conv_w, conv_b, dt_bias, a_log, d, norm_w, w_out):
    a = -jnp.exp(a_log.astype(jnp.float32))

    def flip(t):
        return jnp.flip(t, axis=1)

    def bidir(t, inits):
        bsz, seq_len = t.shape[0], t.shape[1]
        z, xbc, dt = jnp.split(t @ w_in, [SSD_D_INNER, SSD_D_INNER + SSD_CONV_DIM], axis=-1)
        xbc = jax.nn.silu(depthwise_conv_centred(xbc, conv_w, conv_b))
        xs, bm, cm = jnp.split(xbc, [SSD_D_INNER, SSD_D_INNER + SSD_GROUPS * SSD_STATE], axis=-1)
        xs = xs.reshape(bsz, seq_len, SSD_HEADS, SSD_HEADDIM)
        bm = bm.reshape(bsz, seq_len, SSD_GROUPS, SSD_STATE)
        cm = cm.reshape(bsz, seq_len, SSD_GROUPS, SSD_STATE)
        dt = jax.nn.softplus(dt.reshape(bsz, seq_len, 2, SSD_HEADS) + dt_bias)
        y_f, s_f = ssd_scan(xs, dt[:, :, 0], a[0], bm, cm, inits[0])
        y_b, s_b = ssd_scan(flip(xs), flip(dt[:, :, 1]), a[1], flip(bm), flip(cm), inits[1])
        y = y_f + flip(y_b) + d[:, None] * xs
        y = rms_norm(y.reshape(z.shape) * jax.nn.silu(z), norm_w)
        return y @ w_out, (s_f, s_b)

    zero = jnp.zeros((h.shape[0], SSD_GROUPS, SSD_HEADS // SSD_GROUPS, SSD_HEADDIM, SSD_STATE), h.dtype)
    out_c, states_c = bidir(hc, (zero, zero))
    out_l, _ = bidir(h, states_c)
    return out_l, out_c


def mla_project(t, w_down, q_norm, kv_norm, w_uq, w_uk, w_uv, pos):
    cq, ckv, kr = jnp.split(t @ w_down, [MLA_Q_RANK, MLA_Q_RANK + MLA_KV_RANK], axis=-1)
    q = jnp.einsum('btr,rhd->bthd', rms_norm(cq, q_norm), w_uq)
    ckv = rms_norm(ckv, kv_norm)
    k_nope = jnp.einsum('btr,rhd->bthd', ckv, w_uk)
    v = jnp.einsum('btr,rhd->bthd', ckv, w_uv)
    q_nope, q_rope = q[..., :MLA_NOPE], q[..., MLA_NOPE:]
    kr = kr[:, :, None, :]
    if pos is not None:
        q_rope = rope_axial(q_rope, pos[0], pos[1])
        kr = rope_axial(kr, pos[0], pos[1])
    k = jnp.concatenate([k_nope, jnp.broadcast_to(kr, k_nope.shape[:3] + (MLA_ROPE,))], axis=-1)
    q = jnp.concatenate([q_nope, q_rope], axis=-1)
    return q, k, v


def attend(q, k, v):
    s = jnp.einsum('bqhd,bkhd->bhqk', q, k).astype(jnp.float32) * (MLA_NOPE + MLA_ROPE) ** -0.5
    p = jax.nn.softmax(s, axis=-1).astype(v.dtype)
    return jnp.einsum('bhqk,bkhd->bqhd', p, v)


def mla_mixer(h, hc, pos, w_down, q_norm, kv_norm, w_uq, w_uk, w_uv, w_o):
    bsz, seq_len, _ = h.shape
    qc, kc, vc = mla_project(hc, w_down, q_norm, kv_norm, w_uq, w_uk, w_uv, None)
    ql, kl, vl = mla_project(h, w_down, q_norm, kv_norm, w_uq, w_uk, w_uv, pos)
    out_c = attend(qc, kc, vc).reshape(bsz, hc.shape[1], MLA_HEADS * MLA_V)
    k_all = jnp.concatenate([kc, kl], axis=1)
    v_all = jnp.concatenate([vc, vl], axis=1)
    nb = seq_len // MLA_BLOCK
    qb = jnp.moveaxis(ql.reshape(bsz, nb, MLA_BLOCK, MLA_HEADS, MLA_NOPE + MLA_ROPE), 1, 0)
    out_l = lax.map(lambda qblk: attend(qblk, k_all, v_all), qb)
    out_l = jnp.moveaxis(out_l, 0, 1).reshape(bsz, seq_len, MLA_HEADS * MLA_V)
    return out_l @ w_o, out_c @ w_o


def peer_ffn(h, w_q, subkeys, u_tab, v_tab):
    bsz, seq_len, dm = h.shape
    half = PEER_QDIM // 2
    q = jnp.einsum('btd,dhk->bthk', h, w_q)
    s1 = jnp.einsum('bthk,hnk->bthn', q[..., :half], subkeys[:, 0])
    s2 = jnp.einsum('bthk,hnk->bthn', q[..., half:], subkeys[:, 1])
    v1, i1 = lax.top_k(s1, PEER_TOPK)
    v2, i2 = lax.top_k(s2, PEER_TOPK)
    n_cand = PEER_TOPK * PEER_TOPK
    cand = (v1[..., :, None] + v2[..., None, :]).reshape(bsz, seq_len, PEER_HEADS, n_cand)
    cidx = (i1[..., :, None] * PEER_KEYS + i2[..., None, :]).reshape(bsz, seq_len, PEER_HEADS, n_cand)
    best, sel = lax.top_k(cand, PEER_TOPK)
    eidx = jnp.take_along_axis(cidx, sel, axis=-1)
    gate = jax.nn.softmax(best.astype(jnp.float32), axis=-1).astype(h.dtype)
    n_blk = (bsz * seq_len) // PEER_BLOCK
    n_sel = PEER_HEADS * PEER_TOPK
    hb = h.reshape(n_blk, PEER_BLOCK, dm)
    ib = eidx.reshape(n_blk, PEER_BLOCK, n_sel)
    gb = gate.reshape(n_blk, PEER_BLOCK, n_sel)

    def block(args):
        hk, ik, gk = args
        u = jnp.take(u_tab, ik, axis=0)
        act = jax.nn.gelu(jnp.einsum('td,ted->te', hk, u))
        v = jnp.take(v_tab, ik, axis=0)
        return jnp.einsum('te,ted->td', gk * act, v)

    return lax.map(block, (hb, ib, gb)).reshape(bsz, seq_len, dm)


PEER_TM = 512
PEER_EB = 1024
PEER_LANES = 128
PEER_ACT_ROWS = 256
VMEM_LIMIT = 56 * 1024 * 1024
NEG_INF = float("-inf")
GELU_C0 = math.sqrt(2.0 / math.pi)
GELU_C1 = 0.044715 * GELU_C0


def _top16_sorted(s):
    n = PEER_TOPK
    m = s.shape[0] // 8
    x = [s[8 * r:8 * r + 8, :] for r in range(m)]

    def exchange(i, l, descending):
        hi, lo = jnp.maximum(x[i], x[l]), jnp.minimum(x[i], x[l])
        x[i], x[l] = (hi, lo) if descending else (lo, hi)

    def merge_bitonic(size):
        j = size // 2
        while j >= 1:
            for i in range(size):
                if i ^ j > i:
                    exchange(i, i ^ j, True)
            j //= 2

    k = 2
    while k < m:
        j = k // 2
        while j >= 1:
            for i in range(m):
                if i ^ j > i:
                    exchange(i, i ^ j, (i & k) == 0)
            j //= 2
        k *= 2
    merge_bitonic(m)
    shifts = [4, 2, 1]
    if m < n:
        first = shifts.pop(0)
        x = x + [pltpu.roll(x[m - 1 - r], first, axis=0) for r in range(m)]
        merge_bitonic(n)
    for shift in shifts:
        other = [pltpu.roll(x[n - 1 - r], shift, axis=0) for r in range(n)]
        for r in range(n):
            x[r] = jnp.maximum(x[r], other[r])
        merge_bitonic(n)
    return [x[r][0:1, :] for r in range(n)]


def _dot3(a_hi, a_lo, b_hi, b_lo):
    f32 = jnp.float32
    return (jnp.dot(a_hi, b_hi, preferred_element_type=f32)
            + jnp.dot(a_hi, b_lo, preferred_element_type=f32)
            + jnp.dot(a_lo, b_hi, preferred_element_type=f32))


def _split_bf16(v):
    hi = v.astype(jnp.bfloat16)
    lo = (v - hi.astype(jnp.float32)).astype(jnp.bfloat16)
    return hi, lo


def _peer_kernel(x_ref, mod_ref, wq_ref, skh_ref, skl_ref, u_ref, vt_ref, vtl_ref, lng_ref, lnb_ref,
                 o_ref,
                 ht_ref, n1_ref, e1_ref, r2_ref, e2_ref, v1_ref, v2_ref, cand_ref,
                 act_ref, gact_ref, acc_ref):
    f32 = jnp.float32
    bf16 = jnp.bfloat16
    eb = pl.program_id(1)
    n_eb = pl.num_programs(1)
    dm = x_ref.shape[1]
    a_per = PEER_EB // PEER_KEYS
    half = PEER_QDIM // 2

    @pl.when(eb == 0)
    def _prologue():
        mod = mod_ref[0]
        h = x_ref[...] * (1.0 + mod[:, 4 * dm:5 * dm]) + mod[:, 3 * dm:4 * dm]
        ht = h.T.astype(bf16)
        ht_ref[...] = ht
        qt = jnp.dot(wq_ref[...], ht, preferred_element_type=f32)
        cand_ref[...] = jnp.full(cand_ref.shape, NEG_INF, f32)
        gact_ref[...] = jnp.zeros(gact_ref.shape, bf16)
        for hd in range(PEER_HEADS):
            q1_hi, q1_lo = _split_bf16(qt[hd * PEER_QDIM: hd * PEER_QDIM + half])
            q2_hi, q2_lo = _split_bf16(qt[hd * PEER_QDIM + half: (hd + 1) * PEER_QDIM])
            s1 = _dot3(skh_ref[hd, 0], skl_ref[hd, 0], q1_hi, q1_lo)
            s2 = _dot3(skh_ref[hd, 1], skl_ref[hd, 1], q2_hi, q2_lo)
            top1 = _top16_sorted(s1)
            top2 = _top16_sorted(s2)
            for k in range(PEER_TOPK):
                v1_ref[k:k + 1, :] = top1[k]
                v2_ref[k:k + 1, :] = top2[k]
            off = 0
            for i in range(PEER_TOPK):
                cnt = PEER_TOPK // (i + 1)
                cand_ref[off:off + cnt, :] = v1_ref[i:i + 1, :] + v2_ref[0:cnt, :]
                off += cnt
            best = _top16_sorted(cand_ref[...])
            z = jnp.zeros_like(best[0])
            for k in range(PEER_TOPK):
                z = z + jnp.exp(best[k] - best[0])
            tau = best[PEER_TOPK - 1]
            n1 = jnp.zeros(s1.shape, f32)
            rank2 = jnp.zeros(s2.shape, f32)
            for k in range(PEER_TOPK):
                n1 = jnp.where(s1 + top2[k] >= tau, k + 1.0, n1)
                rank2 = jnp.where(top2[k] > s2, k + 1.0, rank2)
            n1_ref[hd] = n1
            r2_ref[hd] = rank2.astype(bf16)
            e1_ref[hd] = jnp.exp(s1 - top1[0])
            e2_ref[hd] = (jnp.exp(s2 - top2[0]) * (0.5 / z)).astype(bf16)
        acc_ref[...] = jnp.zeros(acc_ref.shape, f32)

    par = eb % 2
    for r0 in range(0, PEER_EB, PEER_ACT_ROWS):
        act_ref[r0:r0 + PEER_ACT_ROWS, :] = jnp.dot(u_ref[r0:r0 + PEER_ACT_ROWS, :], ht_ref[...],
                                                    preferred_element_type=f32)
    acc_ref[...] += jnp.dot(vt_ref[...], gact_ref[1 - par], preferred_element_type=f32)
    a_base = pl.multiple_of(eb * a_per, a_per)
    blk = (PEER_KEYS, PEER_LANES)
    for a in range(a_per):
        for lg in range(0, x_ref.shape[0], PEER_LANES):
            lanes = slice(lg, lg + PEER_LANES)
            g = None
            for hd in range(PEER_HEADS):
                n1a = jnp.broadcast_to(n1_ref[hd, pl.ds(a_base, a_per), lanes][a:a + 1, :].astype(bf16), blk)
                e1a = jnp.broadcast_to(e1_ref[hd, pl.ds(a_base, a_per), lanes][a:a + 1, :].astype(bf16), blk)
                w = jnp.where(r2_ref[hd, :, lanes] < n1a, e2_ref[hd, :, lanes], jnp.zeros(blk, bf16)) * e1a
                g = w if g is None else g + w
            rows = slice(a * PEER_KEYS, (a + 1) * PEER_KEYS)
            x = act_ref[rows, lanes]
            t = jnp.tanh(x * (GELU_C0 + GELU_C1 * (x * x)))
            gact_ref[par, rows, lanes] = g * (x + x * t).astype(bf16)

    @pl.when(eb == n_eb - 1)
    def _epilogue():
        mod = mod_ref[0]
        ffn_t = acc_ref[...] + jnp.dot(vtl_ref[...], gact_ref[par], preferred_element_type=f32)
        y = DEEPNORM_ALPHA * x_ref[...] + mod[:, 5 * dm:6 * dm] * ffn_t.T
        mu = jnp.mean(y, axis=-1, keepdims=True)
        yc = y - mu
        var = jnp.mean(yc * yc, axis=-1, keepdims=True)
        o_ref[...] = yc * lax.rsqrt(var + LN_EPS) * lng_ref[...] + lnb_ref[...]


def _mod_row_index(i, tm, n_lat, seq, bsz):
    return jnp.where(i * tm < n_lat, (i * tm) // seq, bsz)


def peer_ln_pallas(xa, mod3, n_lat, seq, bsz, w_q, subkeys, u_tab, v_tab, ln_g, ln_b):
    n_tok, dm = xa.shape
    tm = PEER_TM
    f32, bf16 = jnp.float32, jnp.bfloat16
    wq_t = w_q.reshape(dm, PEER_HEADS * PEER_QDIM).T
    wq_bf = wq_t.astype(bf16)
    sk_hi = subkeys.astype(bf16)
    sk_lo = (subkeys - sk_hi.astype(f32)).astype(bf16)
    u_bf = u_tab.astype(bf16)
    vt_bf = v_tab.T.astype(bf16)
    n_eb = PEER_EXPERTS // PEER_EB
    qd = PEER_HEADS * PEER_QDIM
    n_cand = sum(PEER_TOPK // (i + 1) for i in range(PEER_TOPK))
    n_cand_pad = 64
    assert n_cand <= n_cand_pad
    mod_idx = functools.partial(_mod_row_index, tm=tm, n_lat=n_lat, seq=seq, bsz=bsz)
    return pl.pallas_call(
        _peer_kernel,
        grid=(n_tok // tm, n_eb),
        in_specs=[
            pl.BlockSpec((tm, dm), lambda i, e: (i, 0)),
            pl.BlockSpec((1, 1, 6 * dm), lambda i, e: (mod_idx(i), 0, 0)),
            pl.BlockSpec((qd, dm), lambda i, e: (0, 0)),
            pl.BlockSpec((PEER_HEADS, 2, PEER_KEYS, PEER_QDIM // 2), lambda i, e: (0, 0, 0, 0)),
            pl.BlockSpec((PEER_HEADS, 2, PEER_KEYS, PEER_QDIM // 2), lambda i, e: (0, 0, 0, 0)),
            pl.BlockSpec((PEER_EB, dm), lambda i, e: (e, 0)),
            pl.BlockSpec((dm, PEER_EB), lambda i, e: (0, jnp.maximum(e - 1, 0))),
            pl.BlockSpec((dm, PEER_EB), lambda i, e: (0, n_eb - 1)),
            pl.BlockSpec((1, dm), lambda i, e: (0, 0)),
            pl.BlockSpec((1, dm), lambda i, e: (0, 0)),
        ],
        out_specs=pl.BlockSpec((tm, dm), lambda i, e: (i, 0)),
        out_shape=jax.ShapeDtypeStruct((n_tok, dm), f32),
        scratch_shapes=[
            pltpu.VMEM((dm, tm), bf16),
            pltpu.VMEM((PEER_HEADS, PEER_KEYS, tm), f32),
            pltpu.VMEM((PEER_HEADS, PEER_KEYS, tm), f32),
            pltpu.VMEM((PEER_HEADS, PEER_KEYS, tm), bf16),
            pltpu.VMEM((PEER_HEADS, PEER_KEYS, tm), bf16),
            pltpu.VMEM((PEER_TOPK, tm), f32),
            pltpu.VMEM((PEER_TOPK, tm), f32),
            pltpu.VMEM((n_cand_pad, tm), f32),
            pltpu.VMEM((PEER_EB, tm), f32),
            pltpu.VMEM((2, PEER_EB, tm), bf16),
            pltpu.VMEM((dm, tm), f32),
        ],
        compiler_params=pltpu.CompilerParams(
            dimension_semantics=("arbitrary", "arbitrary"),
            vmem_limit_bytes=VMEM_LIMIT),
        name="peer_ln",
    )(xa, mod3, wq_bf, sk_hi, sk_lo, u_bf, vt_bf, vt_bf, ln_g.reshape(1, dm), ln_b.reshape(1, dm))


S5_CHUNK = 16
S5_LEVELS = 9


def _s5_prep(a_re, a_im, log_dt, b_re, b_im, c_re, c_im, d):
    f32, bf16 = jnp.float32, jnp.bfloat16
    hp = lax.Precision.HIGHEST
    n_g, n_p, q = S5_GROUPS, S5_STATE, S5_CHUNK
    dt = jnp.exp(log_dt.astype(f32))[..., None]

    def apow(n):
        mag = jnp.exp(a_re * dt * n)
        ang = a_im * dt * n
        return mag * jnp.cos(ang), mag * jnp.sin(ang)

    ar1, ai1 = apow(1.0)
    den = a_re * a_re + a_im * a_im
    nr, ni = ar1 - 1.0, ai1
    cr = (nr * a_re + ni * a_im) / den
    ci = (ni * a_re - nr * a_im) / den
    bbr = cr[..., None] * b_re - ci[..., None] * b_im
    bbi = cr[..., None] * b_im + ci[..., None] * b_re
    lag = jnp.arange(q + 1, dtype=f32)[:, None, None, None]
    pr, pi = apow(lag)
    mr = pr[..., None] * bbr - pi[..., None] * bbi
    mi = pr[..., None] * bbi + pi[..., None] * bbr
    kmat = (jnp.einsum('dgcp,ndgpk->ndgck', c_re, mr, precision=hp)
            - jnp.einsum('dgcp,ndgpk->ndgck', c_im, mi, precision=hp))
    r_idx = jnp.arange(q)[:, None]
    t_idx = jnp.arange(q)[None, :]

    def toeplitz(kd, lagm):
        blk = kd[jnp.clip(lagm, 0, q)]
        blk = jnp.where((lagm >= 0)[:, :, None, None, None], blk, 0.0)
        return blk.transpose(2, 0, 4, 1, 3).reshape(n_g, q * S5_GROUP, q * S5_GROUP)

    eye = jnp.eye(q * S5_GROUP, dtype=f32)
    dvec = jnp.tile(d.reshape(n_g, 1, S5_GROUP), (1, q, 1)).reshape(n_g, q * S5_GROUP)
    tsum = toeplitz(kmat[:, 0], t_idx - r_idx) + toeplitz(kmat[:, 1], r_idx - t_idx) + eye[None] * dvec[:, None, :]

    def w_in(direction, exps):
        wr = mr[exps, direction].transpose(1, 0, 3, 2).reshape(n_g, q * S5_GROUP, n_p)
        wi = mi[exps, direction].transpose(1, 0, 3, 2).reshape(n_g, q * S5_GROUP, n_p)
        return wr, wi

    def w_out(direction, exps):
        pre, pim = pr[exps, direction], pi[exps, direction]
        cre, cim = c_re[direction], c_im[direction]
        wre = cre[None] * pre[:, :, None, :] - cim[None] * pim[:, :, None, :]
        wim = -(cre[None] * pim[:, :, None, :] + cim[None] * pre[:, :, None, :])
        return (wre.transpose(1, 3, 0, 2).reshape(n_g, n_p, q * S5_GROUP),
                wim.transpose(1, 3, 0, 2).reshape(n_g, n_p, q * S5_GROUP))

    steps = jnp.arange(q)
    win = [w_in(0, q - 1 - steps), w_in(1, steps)]
    wout = [w_out(0, steps + 1), w_out(1, q - steps)]
    zc = jnp.zeros((n_g, q * S5_GROUP, n_p), f32)
    zr = jnp.zeros((n_g, n_p, q * S5_GROUP), f32)
    win_p, wout_p = [], []
    for direction in range(2):
        wr, wi = win[direction]
        even = jnp.concatenate([wr, zc, wi, zc], axis=2)
        odd = jnp.concatenate([zc, wr, zc, wi], axis=2)
        is_odd = (jnp.arange(n_g) % 2 == 1)[:, None, None]
        win_p.append(jnp.where(is_odd, odd, even))
        vr, vi = wout[direction]
        even = jnp.concatenate([vr, zr, vi, zr], axis=1)
        odd = jnp.concatenate([zr, vr, zr, vi], axis=1)
        wout_p.append(jnp.where(is_odd, odd, even))
    win_p = jnp.stack(win_p, axis=1).reshape(n_g // 2, 2, 2, 4 * n_p, 4 * n_p).transpose(0, 2, 1, 3, 4)
    wout_p = jnp.stack(wout_p, axis=1).reshape(n_g // 2, 2, 2, 4 * n_p, 4 * n_p).transpose(0, 2, 1, 3, 4)
    lvl = (q * 2.0 ** jnp.arange(S5_LEVELS, dtype=f32))[:, None, None, None]
    lr, li = apow(lvl)
    pw = jnp.stack([lr, li], axis=1)
    pw = pw.transpose(3, 2, 0, 1, 4).reshape(n_g // 2, 2, 2, S5_LEVELS, 2, n_p)
    pw = pw.transpose(0, 2, 3, 4, 1, 5).reshape(n_g // 2, 2, 2 * S5_LEVELS, 2 * n_p)
    return tsum.astype(bf16).reshape(n_g // 2, 2, q * S5_GROUP, q * S5_GROUP), win_p.astype(bf16), wout_p.astype(bf16), pw


def _shift_rows(v, r, up):
    n = v.shape[0]
    row = lax.broadcasted_iota(jnp.int32, v.shape, 0)
    if up:
        return jnp.where(row < n - r, pltpu.roll(v, n - r, axis=0), 0.0)
    return jnp.where(row >= r, pltpu.roll(v, r, axis=0), 0.0)


def _chunk_scan(re, im, pw, rows_per_chunk, up):
    n_chunks = re.shape[0] // rows_per_chunk
    level, s = 0, 1
    while s < n_chunks:
        ar, ai = pw[2 * level:2 * level + 1, :], pw[2 * level + 1:2 * level + 2, :]
        sre = _shift_rows(re, s * rows_per_chunk, up)
        sim = _shift_rows(im, s * rows_per_chunk, up)
        re, im = re + ar * sre - ai * sim, im + ar * sim + ai * sre
        level, s = level + 1, 2 * s
    return re, im


def _s5_kernel(u_ref, scl_ref, shl_ref, scc_ref, shc_ref, t_ref, win_ref, wout_ref, pw_ref, y_ref, *, bsz, ctx_chunks):
    f32, bf16 = jnp.float32, jnp.bfloat16
    n = u_ref.shape[1]
    width = u_ref.shape[2]
    rc = ctx_chunks * bsz
    hs = []
    for gi in range(2):
        xv = u_ref[gi].reshape(n // 8, 8, width)
        hl = xv * scl_ref[gi][None] + shl_ref[gi][None]
        hc = xv * scc_ref[gi][None] + shc_ref[gi][None]
        slab = lax.broadcasted_iota(jnp.int32, xv.shape, 0)
        hs.append(jnp.where(slab < rc // 8, hc, hl).reshape(n, width).astype(bf16))
    y = [jnp.dot(hs[gi], t_ref[0, gi], preferred_element_type=f32) for gi in range(2)]
    half = width // 2
    for direction in range(2):
        sloc = (jnp.dot(hs[0], win_ref[0, direction, 0], preferred_element_type=f32)
                + jnp.dot(hs[1], win_ref[0, direction, 1], preferred_element_type=f32))
        re, im = sloc[:, :half], sloc[:, half:]
        pw = pw_ref[0, direction]
        if direction == 0:
            sre, sim = _chunk_scan(_shift_rows(re, bsz, False), _shift_rows(im, bsz, False), pw, bsz, False)
        else:
            cre, cim = _chunk_scan(_shift_rows(re[:rc], bsz, True), _shift_rows(im[:rc], bsz, True), pw, bsz, True)
            ar, ai = pw[0:1, :], pw[1:2, :]
            fre = ar * cre[0:8] - ai * cim[0:8] + re[0:8]
            fim = ar * cim[0:8] + ai * cre[0:8] + im[0:8]
            row8 = lax.broadcasted_iota(jnp.int32, fre.shape, 0)
            tre = jnp.where(row8 >= 8 - bsz, pltpu.roll(fre, 8 - bsz, axis=0), 0.0)
            tim = jnp.where(row8 >= 8 - bsz, pltpu.roll(fim, 8 - bsz, axis=0), 0.0)
            lre, lim = _shift_rows(re[rc:], bsz, True), _shift_rows(im[rc:], bsz, True)
            lre = jnp.concatenate([lre[:-8], lre[-8:] + tre], axis=0)
            lim = jnp.concatenate([lim[:-8], lim[-8:] + tim], axis=0)
            lre, lim = _chunk_scan(lre, lim, pw, bsz, True)
            sre = jnp.concatenate([cre, lre], axis=0)
            sim = jnp.concatenate([cim, lim], axis=0)
        s_in = jnp.concatenate([sre, sim], axis=1).astype(bf16)
        for gi in range(2):
            y[gi] = y[gi] + jnp.dot(s_in, wout_ref[0, direction, gi], preferred_element_type=f32)
    for gi in range(2):
        y_ref[gi] = y[gi]


def s5_ssm_pallas(xa, mod_tab, n_lat, seq, bsz, prep):
    tsum, win_p, wout_p, pw = prep
    f32 = jnp.float32
    n_tok, dm = xa.shape
    ctx_len = (n_tok - n_lat) // bsz
    q = S5_CHUNK
    n_chunks = (seq + ctx_len) // q
    width = q * S5_GROUP
    full = jnp.concatenate([xa[n_lat:].reshape(bsz, ctx_len, dm), xa[:n_lat].reshape(bsz, seq, dm)], axis=1)
    u = full.reshape(bsz, n_chunks, q, S5_GROUPS, S5_GROUP).transpose(3, 1, 0, 2, 4).reshape(S5_GROUPS, n_chunks * bsz, width)

    def tile(vec_rows):
        rows = vec_rows.shape[0]
        t = vec_rows.reshape(rows, S5_GROUPS, 1, S5_GROUP)
        t = jnp.broadcast_to(t, (rows, S5_GROUPS, q, S5_GROUP)).reshape(rows, S5_GROUPS, width)
        return jnp.tile(t.transpose(1, 0, 2), (1, 8 // rows, 1))

    scl, shl = tile(1.0 + mod_tab[:bsz, dm:2 * dm]), tile(mod_tab[:bsz, 0:dm])
    scc = tile(jnp.broadcast_to(1.0 + mod_tab[bsz:bsz + 1, dm:2 * dm], (bsz, dm)))
    shc = tile(jnp.broadcast_to(mod_tab[bsz:bsz + 1, 0:dm], (bsz, dm)))
    n_rows = n_chunks * bsz
    tile_spec = pl.BlockSpec((2, 8, width), lambda p: (p, 0, 0))
    y = pl.pallas_call(
        functools.partial(_s5_kernel, bsz=bsz, ctx_chunks=ctx_len // q),
        grid=(S5_GROUPS // 2,),
        in_specs=[
            pl.BlockSpec((2, n_rows, width), lambda p: (p, 0, 0)),
            tile_spec, tile_spec, tile_spec, tile_spec,
            pl.BlockSpec((1, 2, width, width), lambda p: (p, 0, 0, 0)),
            pl.BlockSpec((1, 2, 2, width, width), lambda p: (p, 0, 0, 0, 0)),
            pl.BlockSpec((1, 2, 2, width, width), lambda p: (p, 0, 0, 0, 0)),
            pl.BlockSpec((1, 2, 2 * S5_LEVELS, 2 * S5_STATE), lambda p: (p, 0, 0, 0)),
        ],
        out_specs=pl.BlockSpec((2, n_rows, width), lambda p: (p, 0, 0)),
        out_shape=jax.ShapeDtypeStruct((S5_GROUPS, n_rows, width), f32),
        compiler_params=pltpu.CompilerParams(dimension_semantics=("arbitrary",), vmem_limit_bytes=VMEM_LIMIT),
        name="s5_ssm",
    )(u, scl, shl, scc, shc, tsum, win_p, wout_p, pw)
    yf = y.reshape(S5_GROUPS, n_chunks, bsz, q, S5_GROUP).transpose(2, 1, 3, 0, 4).reshape(bsz, seq + ctx_len, dm)
    return jnp.concatenate([yf[:, ctx_len:].reshape(n_lat, dm), yf[:, :ctx_len].reshape(bsz * ctx_len, dm)], axis=0)


def _glu_ln_kernel(y_ref, x_ref, mod_ref, wv_ref, wg_ref, lng_ref, lnb_ref, o_ref):
    f32 = jnp.float32
    dm = x_ref.shape[1]
    g = jax.nn.gelu(y_ref[...]).astype(jnp.bfloat16)
    val = jnp.dot(g, wv_ref[...], preferred_element_type=f32)
    gate = jnp.dot(g, wg_ref[...], preferred_element_type=f32)
    out = val * jax.nn.sigmoid(gate)
    mod = mod_ref[0]
    z = DEEPNORM_ALPHA * x_ref[...] + mod[:, 2 * dm:3 * dm] * out
    mu = jnp.mean(z, axis=-1, keepdims=True)
    zc = z - mu
    var = jnp.mean(zc * zc, axis=-1, keepdims=True)
    o_ref[...] = zc * lax.rsqrt(var + LN_EPS) * lng_ref[...] + lnb_ref[...]


TOK_TM = 512


def glu_ln_pallas(ya, xa, mod3, n_lat, seq, bsz, w_val, w_gate, ln_g, ln_b):
    n_tok, dm = ya.shape
    tm = TOK_TM
    bf16 = jnp.bfloat16
    mod_idx = functools.partial(_mod_row_index, tm=tm, n_lat=n_lat, seq=seq, bsz=bsz)
    return pl.pallas_call(
        _glu_ln_kernel,
        grid=(n_tok // tm,),
        in_specs=[
            pl.BlockSpec((tm, dm), lambda i: (i, 0)),
            pl.BlockSpec((tm, dm), lambda i: (i, 0)),
            pl.BlockSpec((1, 1, 6 * dm), lambda i: (mod_idx(i), 0, 0)),
            pl.BlockSpec((dm, dm), lambda i: (0, 0)),
            pl.BlockSpec((dm, dm), lambda i: (0, 0)),
            pl.BlockSpec((1, dm), lambda i: (0, 0)),
            pl.BlockSpec((1, dm), lambda i: (0, 0)),
        ],
        out_specs=pl.BlockSpec((tm, dm), lambda i: (i, 0)),
        out_shape=jax.ShapeDtypeStruct((n_tok, dm), jnp.float32),
        compiler_params=pltpu.CompilerParams(dimension_semantics=("arbitrary",), vmem_limit_bytes=VMEM_LIMIT),
        name="glu_ln",
    )(ya, xa, mod3, w_val.astype(bf16), w_gate.astype(bf16), ln_g.reshape(1, dm), ln_b.reshape(1, dm))


MLA_HD = 128
MLA_TQ = 256
MLA_HB = 4


def _mla_prep(w_down, w_uq, w_uk, w_uv):
    bf16 = jnp.bfloat16
    quarter = MLA_ROPE // 4
    swap = np.concatenate([np.arange(quarter, 2 * quarter), np.arange(0, quarter),
                           np.arange(3 * quarter, 4 * quarter), np.arange(2 * quarter, 3 * quarter)])
    dm = w_down.shape[0]
    w_cq = w_down[:, :MLA_Q_RANK]
    w_ckv = w_down[:, MLA_Q_RANK:MLA_Q_RANK + MLA_KV_RANK]
    w_kr = w_down[:, MLA_Q_RANK + MLA_KV_RANK:]
    wd_t = jnp.concatenate([w_cq, w_ckv, jnp.zeros((dm, MLA_NOPE), w_down.dtype), w_kr, w_kr[:, swap]], axis=1).T
    rope = w_uq[:, :, MLA_NOPE:]
    wuq_t = jnp.concatenate([w_uq, rope[:, :, swap]], axis=2).reshape(MLA_Q_RANK, MLA_HEADS * MLA_HD).T
    wuk = jnp.concatenate([w_uk, jnp.zeros((MLA_KV_RANK, MLA_HEADS, MLA_HD - MLA_NOPE), w_uk.dtype)], axis=2)
    wuk = wuk.reshape(MLA_KV_RANK, MLA_HEADS * MLA_HD)
    wuv_t = w_uv.reshape(MLA_KV_RANK, MLA_HEADS * MLA_V).T
    return wd_t.astype(bf16), wuq_t.astype(bf16), wuk.astype(bf16), wuv_t.astype(bf16)


def _rope_tables(seq, tm):
    quarter = MLA_ROPE // 4
    freqs = ROPE_BASE ** (-jnp.arange(quarter, dtype=jnp.float32) / quarter)
    t = jnp.arange(seq, dtype=jnp.float32)
    row, col = jnp.floor(t / GRID_W), t - GRID_W * jnp.floor(t / GRID_W)
    ang_r, ang_c = freqs[:, None] * row[None, :], freqs[:, None] * col[None, :]
    cos32 = jnp.concatenate([jnp.cos(ang_r), jnp.cos(ang_r), jnp.cos(ang_c), jnp.cos(ang_c)], axis=0)
    sin32 = jnp.concatenate([-jnp.sin(ang_r), jnp.sin(ang_r), -jnp.sin(ang_c), jnp.sin(ang_c)], axis=0)
    ones = jnp.ones((MLA_NOPE, seq + tm), jnp.float32)
    zeros = jnp.zeros((MLA_HD - MLA_NOPE - MLA_ROPE, seq + tm), jnp.float32)
    cos_t = jnp.concatenate([ones, jnp.concatenate([cos32, jnp.ones((MLA_ROPE, tm))], axis=1), zeros], axis=0)
    sin_t = jnp.concatenate([0.0 * ones, jnp.concatenate([sin32, jnp.zeros((MLA_ROPE, tm))], axis=1), zeros], axis=0)
    return cos_t, sin_t


def _mla_proj_kernel(x_ref, mod_ref, cos_ref, sin_ref, wd_ref, wuq_ref, wuk_ref, wuv_ref, qn_ref, kvn_ref,
                     qt_ref, k_ref, vt_ref):
    f32, bf16 = jnp.float32, jnp.bfloat16
    dm = x_ref.shape[1]
    mod = mod_ref[0]
    h = x_ref[...] * (1.0 + mod[:, dm:2 * dm]) + mod[:, 0:dm]
    ht = h.T.astype(bf16)
    dt = jnp.dot(wd_ref[...], ht, preferred_element_type=f32)
    cq, ckv, kr = dt[:MLA_Q_RANK], dt[MLA_Q_RANK:MLA_Q_RANK + MLA_KV_RANK], dt[MLA_Q_RANK + MLA_KV_RANK:]
    cqn = cq * lax.rsqrt(jnp.mean(cq * cq, axis=0, keepdims=True) + RMS_EPS) * qn_ref[...]
    ckvn = ckv * lax.rsqrt(jnp.mean(ckv * ckv, axis=0, keepdims=True) + RMS_EPS) * kvn_ref[...]
    cos_t, sin_t = cos_ref[...], sin_ref[...]

    def rope(v):
        shifted = jnp.concatenate([v[MLA_ROPE:], v[:MLA_ROPE]], axis=0)
        return v * cos_t + shifted * sin_t

    scale = (MLA_NOPE + MLA_ROPE) ** -0.5
    q_all = jnp.dot(wuq_ref[...], cqn.astype(bf16), preferred_element_type=f32)
    for hd in range(MLA_HEADS):
        qt_ref[hd] = (rope(q_all[hd * MLA_HD:(hd + 1) * MLA_HD]) * scale).astype(bf16)
    ckvn_bf = ckvn.astype(bf16)
    vt_ref[...] = jnp.dot(wuv_ref[...], ckvn_bf, preferred_element_type=f32).astype(bf16)
    k_all = jnp.dot(ckvn.T.astype(bf16), wuk_ref[...], preferred_element_type=f32)
    kr_rows = rope(kr).T
    for hd in range(MLA_HEADS):
        k_ref[hd] = (k_all[:, hd * MLA_HD:(hd + 1) * MLA_HD] + kr_rows).astype(bf16)


def _mla_attn_kernel(*refs, with_latent):
    f32, bf16 = jnp.float32, jnp.bfloat16
    if with_latent:
        q_ref, kl_ref, kc_ref, vl_ref, vc_ref, o_ref = refs
    else:
        q_ref, kc_ref, vc_ref, o_ref = refs
    for i in range(q_ref.shape[0]):
        rows = slice(i * MLA_V, (i + 1) * MLA_V)
        q = q_ref[i]
        s_c = jnp.dot(kc_ref[i], q, preferred_element_type=f32)
        m = jnp.max(s_c, axis=0, keepdims=True)
        if with_latent:
            s_l = jnp.dot(kl_ref[i], q, preferred_element_type=f32)
            m = jnp.maximum(m, jnp.max(s_l, axis=0, keepdims=True))
        p_c = jnp.exp(s_c - m)
        den = jnp.sum(p_c, axis=0, keepdims=True)
        o = jnp.dot(vc_ref[rows, :], p_c.astype(bf16), preferred_element_type=f32)
        if with_latent:
            p_l = jnp.exp(s_l - m)
            den = den + jnp.sum(p_l, axis=0, keepdims=True)
            o = o + jnp.dot(vl_ref[rows, :], p_l.astype(bf16), preferred_element_type=f32)
        o_ref[rows, :] = (o / den).astype(bf16)


def _mla_out_kernel(ot_ref, x_ref, mod_ref, wo_ref, lng_ref, lnb_ref, o_ref):
    f32 = jnp.float32
    dm = x_ref.shape[1]
    attn = ot_ref[...].astype(f32).T.astype(jnp.bfloat16)
    out = jnp.dot(attn, wo_ref[...], preferred_element_type=f32)
    mod = mod_ref[0]
    z = DEEPNORM_ALPHA * x_ref[...] + mod[:, 2 * dm:3 * dm] * out
    mu = jnp.mean(z, axis=-1, keepdims=True)
    zc = z - mu
    var = jnp.mean(zc * zc, axis=-1, keepdims=True)
    o_ref[...] = zc * lax.rsqrt(var + LN_EPS) * lng_ref[...] + lnb_ref[...]


def mla_ln_pallas(xa, mod3, n_lat, seq, bsz, w_down, q_norm, kv_norm, w_uq, w_uk, w_uv, w_o, ln_g, ln_b):
    f32, bf16 = jnp.float32, jnp.bfloat16
    n_tok, dm = xa.shape
    n_ctx = n_tok - n_lat
    ctx_len = n_ctx // bsz
    tm = TOK_TM
    wd_t, wuq_t, wuk, wuv_t = _mla_prep(w_down, w_uq, w_uk, w_uv)
    cos_t, sin_t = _rope_tables(seq, tm)
    mod_idx = functools.partial(_mod_row_index, tm=tm, n_lat=n_lat, seq=seq, bsz=bsz)
    tiles_per_sample = seq // tm
    pos_idx = lambda i: jnp.where(i * tm < n_lat, i % tiles_per_sample, tiles_per_sample)
    n_hd, n_dn = MLA_HEADS * MLA_HD, wd_t.shape[0]
    whole = lambda shape: pl.BlockSpec(shape, lambda i: (0,) * len(shape))
    params = pltpu.CompilerParams(dimension_semantics=("arbitrary",), vmem_limit_bytes=VMEM_LIMIT)
    qt, k, vt = pl.pallas_call(
        _mla_proj_kernel,
        grid=(n_tok // tm,),
        in_specs=[
            pl.BlockSpec((tm, dm), lambda i: (i, 0)),
            pl.BlockSpec((1, 1, 6 * dm), lambda i: (mod_idx(i), 0, 0)),
            pl.BlockSpec((MLA_HD, tm), lambda i: (0, pos_idx(i))),
            pl.BlockSpec((MLA_HD, tm), lambda i: (0, pos_idx(i))),
            whole((n_dn, dm)), whole((n_hd, MLA_Q_RANK)), whole((MLA_KV_RANK, n_hd)),
            whole((MLA_HEADS * MLA_V, MLA_KV_RANK)), whole((MLA_Q_RANK, 1)), whole((MLA_KV_RANK, 1)),
        ],
        out_specs=[
            pl.BlockSpec((MLA_HEADS, MLA_HD, tm), lambda i: (0, 0, i)),
            pl.BlockSpec((MLA_HEADS, tm, MLA_HD), lambda i: (0, i, 0)),
            pl.BlockSpec((MLA_HEADS * MLA_V, tm), lambda i: (0, i)),
        ],
        out_shape=[
            jax.ShapeDtypeStruct((MLA_HEADS, MLA_HD, n_tok), bf16),
            jax.ShapeDtypeStruct((MLA_HEADS, n_tok, MLA_HD), bf16),
            jax.ShapeDtypeStruct((MLA_HEADS * MLA_V, n_tok), bf16),
        ],
        compiler_params=params,
        name="mla_proj",
    )(xa, mod3, cos_t, sin_t, wd_t, wuq_t, wuk, wuv_t, q_norm.reshape(-1, 1), kv_norm.reshape(-1, 1))

    tq, hb = MLA_TQ, MLA_HB
    n_qt = seq // tq
    ctx_blk = n_lat // ctx_len
    params3 = pltpu.CompilerParams(dimension_semantics=("arbitrary",) * 3, vmem_limit_bytes=VMEM_LIMIT)
    ot_lat = pl.pallas_call(
        functools.partial(_mla_attn_kernel, with_latent=True),
        grid=(bsz, MLA_HEADS // hb, n_qt),
        in_specs=[
            pl.BlockSpec((hb, MLA_HD, tq), lambda b, hd, t: (hd, 0, b * n_qt + t)),
            pl.BlockSpec((hb, seq, MLA_HD), lambda b, hd, t: (hd, b, 0)),
            pl.BlockSpec((hb, ctx_len, MLA_HD), lambda b, hd, t: (hd, ctx_blk + b, 0)),
            pl.BlockSpec((hb * MLA_V, seq), lambda b, hd, t: (hd, b)),
            pl.BlockSpec((hb * MLA_V, ctx_len), lambda b, hd, t: (hd, ctx_blk + b)),
        ],
        out_specs=pl.BlockSpec((hb * MLA_V, tq), lambda b, hd, t: (hd, b * n_qt + t)),
        out_shape=jax.ShapeDtypeStruct((MLA_HEADS * MLA_V, n_lat), bf16),
        compiler_params=params3,
        name="mla_attn_latent",
    )(qt, k, k, vt, vt)
    ot_ctx = pl.pallas_call(
        functools.partial(_mla_attn_kernel, with_latent=False),
        grid=(bsz, MLA_HEADS // hb, 1),
        in_specs=[
            pl.BlockSpec((hb, MLA_HD, ctx_len), lambda b, hd, t: (hd, 0, ctx_blk + b)),
            pl.BlockSpec((hb, ctx_len, MLA_HD), lambda b, hd, t: (hd, ctx_blk + b, 0)),
            pl.BlockSpec((hb * MLA_V, ctx_len), lambda b, hd, t: (hd, ctx_blk + b)),
        ],
        out_specs=pl.BlockSpec((hb * MLA_V, ctx_len), lambda b, hd, t: (hd, b)),
        out_shape=jax.ShapeDtypeStruct((MLA_HEADS * MLA_V, n_ctx), bf16),
        compiler_params=params3,
        name="mla_attn_context",
    )(qt, k, vt)
    ot = jnp.concatenate([ot_lat, ot_ctx], axis=1)
    return pl.pallas_call(
        _mla_out_kernel,
        grid=(n_tok // tm,),
        in_specs=[
            pl.BlockSpec((MLA_HEADS * MLA_V, tm), lambda i: (0, i)),
            pl.BlockSpec((tm, dm), lambda i: (i, 0)),
            pl.BlockSpec((1, 1, 6 * dm), lambda i: (mod_idx(i), 0, 0)),
            whole((MLA_HEADS * MLA_V, dm)), whole((1, dm)), whole((1, dm)),
        ],
        out_specs=pl.BlockSpec((tm, dm), lambda i: (i, 0)),
        out_shape=jax.ShapeDtypeStruct((n_tok, dm), f32),
        compiler_params=params,
        name="mla_out_ln",
    )(ot, xa, mod3, w_o.astype(bf16), ln_g.reshape(1, dm), ln_b.reshape(1, dm))


SSD_DT_PAD = 128
SSD_HPG = SSD_HEADS // SSD_GROUPS
SSD_GW = SSD_HPG * SSD_HEADDIM


def _ssd_in_kernel(x_ref, mod_ref, w_ref, wdt_ref, z_ref, xbc_ref, dt_ref, dtt_ref):
    f32, bf16 = jnp.float32, jnp.bfloat16
    dm = x_ref.shape[1]
    mod = mod_ref[0]
    h = x_ref[...] * (1.0 + mod[:, dm:2 * dm]) + mod[:, 0:dm]
    proj = jnp.dot(h.astype(bf16), w_ref[...], preferred_element_type=f32)
    z_ref[...] = proj[:, :SSD_D_INNER]
    xbc_ref[...] = proj[:, SSD_D_INNER:SSD_D_INNER + SSD_CONV_DIM]
    dt_ref[...] = proj[:, SSD_D_INNER + SSD_CONV_DIM:]
    dtt_ref[...] = jnp.dot(wdt_ref[...], h.T.astype(bf16), preferred_element_type=f32)


def _softplus(v):
    return jnp.maximum(v, 0.0) + jnp.log(1.0 + jnp.exp(-jnp.abs(v)))


def _split3(v):
    f32, bf16 = jnp.float32, jnp.bfloat16
    hi = v.astype(bf16)
    r1 = v - hi.astype(f32)
    mid = r1.astype(bf16)
    lo = (r1 - mid.astype(f32)).astype(bf16)
    return jnp.concatenate([hi, mid, lo], axis=1)


def _ssd_direction(direction, xbc_ref, prev_ref, next_ref, dt_ref, dtt_ref, has_prev, has_next,
                   cw_ref, cb_ref, bias_r_ref, a_r_ref, bias_c_ref, a_c_ref, e_ref, dskip_ref, state_ref, y_ref):
    f32, bf16 = jnp.float32, jnp.bfloat16
    hp = lax.Precision.HIGHEST
    q = xbc_ref.shape[0]
    nh = SSD_HEADS
    xm = xbc_ref[...]
    row = lax.broadcasted_iota(jnp.int32, xm.shape, 0)
    before = jnp.where(has_prev, prev_ref[7:8, :], 0.0)
    after = jnp.where(has_next, next_ref[0:1, :], 0.0)
    x_dn = jnp.where(row == 0, before, pltpu.roll(xm, 1, axis=0))
    x_up = jnp.where(row == q - 1, after, pltpu.roll(xm, q - 1, axis=0))
    conv = cb_ref[...] + x_dn * cw_ref[0:1, :] + xm * cw_ref[1:2, :] + x_up * cw_ref[2:3, :]
    conv = conv * jax.nn.sigmoid(conv)
    xs = conv[:, :SSD_D_INNER]
    gn = SSD_GROUPS * SSD_STATE
    bm, cm = conv[:, SSD_D_INNER:SSD_D_INNER + gn], conv[:, SSD_D_INNER + gn:]
    hs = slice(direction * nh, (direction + 1) * nh)
    dt = _softplus(dt_ref[:, hs] + bias_r_ref[direction:direction + 1, :])
    dtt = _softplus(dtt_ref[hs, :] + bias_c_ref[direction])
    r_i = lax.broadcasted_iota(jnp.int32, (q, q), 0)
    c_i = lax.broadcasted_iota(jnp.int32, (q, q), 1)
    causal = (r_i >= c_i) if direction == 0 else (r_i <= c_i)
    tri = causal.astype(f32)
    tri_t = ((c_i >= r_i) if direction == 0 else (c_i <= r_i)).astype(f32)
    a_cum = jnp.dot(tri, dt * a_r_ref[direction:direction + 1, :], precision=hp, preferred_element_type=f32)
    a_cum_t = jnp.dot(dtt * a_c_ref[direction], tri_t, precision=hp, preferred_element_type=f32)
    last = q - 1 if direction == 0 else 0
    a_exp = jnp.dot(_split3(a_cum), e_ref[...], preferred_element_type=f32)
    dt_exp = jnp.dot(_split3(dt), e_ref[...], preferred_element_type=f32)
    xdt = xs * dt_exp
    a_tot = a_exp[last:last + 1, :]
    xw = (xdt * jnp.exp(a_tot - a_exp)).astype(bf16)
    grow = jnp.exp(a_exp)
    carry = jnp.exp(a_tot)
    xdt_bf = xdt.astype(bf16)
    for g in range(SSD_GROUPS):
        gl = slice(g * SSD_GW, (g + 1) * SSD_GW)
        nl = slice(g * SSD_STATE, (g + 1) * SSD_STATE)
        bm_g, cm_g = bm[:, nl], cm[:, nl].astype(bf16)
        prev = state_ref[direction, g]
        y_g = jnp.dot(cm_g, prev.astype(bf16), preferred_element_type=f32) * grow[:, gl]
        states = jnp.dot(bm_g.T.astype(bf16), xw[:, gl], preferred_element_type=f32)
        state_ref[direction, g] = prev * carry[:, gl] + states
        cb = lax.dot_general(cm_g, bm_g.astype(bf16), (((1,), (1,)), ((), ())), preferred_element_type=f32)
        parts = []
        for hh in range(SSD_HPG):
            hd = g * SSD_HPG + hh
            seg = jnp.broadcast_to(a_cum[:, hd:hd + 1], (q, q)) - a_cum_t[hd:hd + 1, :]
            lmat = (jnp.where(causal, jnp.exp(seg), 0.0) * cb).astype(bf16)
            parts.append(jnp.dot(lmat, xdt_bf[:, hd * SSD_HEADDIM:(hd + 1) * SSD_HEADDIM], preferred_element_type=f32))
        y_g = y_g + jnp.concatenate(parts, axis=1)
        if direction == 0:
            y_g = y_g + dskip_ref[:, gl] * xs[:, gl]
        y_ref[:, gl] = y_g


def _ssd_scan_kernel(xf_ref, pf_ref, nf_ref, dtf_ref, dttf_ref, xb_ref, pb_ref, nb_ref, dtb_ref, dttb_ref,
                     cw_ref, cb_ref, bias_r_ref, a_r_ref, bias_c_ref, a_c_ref, e_ref, dskip_ref,
                     yf_ref, yb_ref, state_ref, *, ctx_chunks, lat_chunks):
    j = pl.program_id(1)

    @pl.when(j == 0)
    def _reset():
        state_ref[...] = jnp.zeros(state_ref.shape, jnp.float32)

    in_ctx = j < ctx_chunks
    pos_f = jnp.where(in_ctx, j, j - ctx_chunks)
    seg_len = jnp.where(in_ctx, ctx_chunks, lat_chunks)
    pos_b = seg_len - 1 - pos_f
    shared = (cw_ref, cb_ref, bias_r_ref, a_r_ref, bias_c_ref, a_c_ref, e_ref, dskip_ref, state_ref)
    _ssd_direction(0, xf_ref, pf_ref, nf_ref, dtf_ref, dttf_ref, pos_f > 0, pos_f < seg_len - 1, *shared, yf_ref)
    _ssd_direction(1, xb_ref, pb_ref, nb_ref, dtb_ref, dttb_ref, pos_b > 0, pos_b < seg_len - 1, *shared, yb_ref)


def _ssd_out_kernel(yf_ref, yb_ref, z_ref, x_ref, mod_ref, nw_ref, wo_ref, lng_ref, lnb_ref, o_ref):
    f32 = jnp.float32
    dm = x_ref.shape[1]
    z = z_ref[...]
    y = (yf_ref[...] + yb_ref[...]) * (z * jax.nn.sigmoid(z))
    y = y * lax.rsqrt(jnp.mean(y * y, axis=-1, keepdims=True) + RMS_EPS) * nw_ref[...]
    out = jnp.dot(y.astype(jnp.bfloat16), wo_ref[...], preferred_element_type=f32)
    mod = mod_ref[0]
    v = DEEPNORM_ALPHA * x_ref[...] + mod[:, 2 * dm:3 * dm] * out
    mu = jnp.mean(v, axis=-1, keepdims=True)
    vc = v - mu
    var = jnp.mean(vc * vc, axis=-1, keepdims=True)
    o_ref[...] = vc * lax.rsqrt(var + LN_EPS) * lng_ref[...] + lnb_ref[...]


def ssd_ln_pallas(xa, mod3, n_lat, seq, bsz, w_in, conv_w, conv_b, dt_bias, a_log, d, norm_w, w_out, ln_g, ln_b):
    f32, bf16 = jnp.float32, jnp.bfloat16
    n_tok, dm = xa.shape
    ctx_len = (n_tok - n_lat) // bsz
    tm = TOK_TM
    q = SSD_CHUNK
    nh = SSD_HEADS
    pad = SSD_DT_PAD - 2 * nh
    w_ext = jnp.concatenate([w_in, jnp.zeros((dm, pad), w_in.dtype)], axis=1).astype(bf16)
    n_in = w_ext.shape[1]
    wdt_t = w_ext[:, SSD_D_INNER + SSD_CONV_DIM:].T
    mod_idx = functools.partial(_mod_row_index, tm=tm, n_lat=n_lat, seq=seq, bsz=bsz)
    whole = lambda shape: pl.BlockSpec(shape, lambda *_: (0,) * len(shape))
    params = pltpu.CompilerParams(dimension_semantics=("arbitrary",), vmem_limit_bytes=VMEM_LIMIT)
    z, xbc, dt, dtt = pl.pallas_call(
        _ssd_in_kernel,
        grid=(n_tok // tm,),
        in_specs=[
            pl.BlockSpec((tm, dm), lambda i: (i, 0)),
            pl.BlockSpec((1, 1, 6 * dm), lambda i: (mod_idx(i), 0, 0)),
            whole((dm, n_in)), whole((SSD_DT_PAD, dm)),
        ],
        out_specs=[
            pl.BlockSpec((tm, SSD_D_INNER), lambda i: (i, 0)),
            pl.BlockSpec((tm, SSD_CONV_DIM), lambda i: (i, 0)),
            pl.BlockSpec((tm, SSD_DT_PAD), lambda i: (i, 0)),
            pl.BlockSpec((SSD_DT_PAD, tm), lambda i: (0, i)),
        ],
        out_shape=[
            jax.ShapeDtypeStruct((n_tok, SSD_D_INNER), f32),
            jax.ShapeDtypeStruct((n_tok, SSD_CONV_DIM), f32),
            jax.ShapeDtypeStruct((n_tok, SSD_DT_PAD), f32),
            jax.ShapeDtypeStruct((SSD_DT_PAD, n_tok), f32),
        ],
        compiler_params=params,
        name="ssd_in_proj",
    )(xa, mod3, w_ext, wdt_t)

    ctx_chunks, lat_chunks = ctx_len // q, seq // q
    ctx_base = n_lat // q
    n_chunk_total = n_tok // q

    def chunk_of(b, j, backward):
        in_ctx = j < ctx_chunks
        pos = jnp.where(in_ctx, j, j - ctx_chunks)
        seg = jnp.where(in_ctx, ctx_chunks, lat_chunks)
        pos = jnp.where(backward, seg - 1 - pos, pos)
        return jnp.where(in_ctx, ctx_base + b * ctx_chunks + pos, b * lat_chunks + pos)

    sub = q // 8
    def dir_specs(backward):
        ch = lambda b, j: chunk_of(b, j, backward)
        return [
            pl.BlockSpec((q, SSD_CONV_DIM), lambda b, j: (ch(b, j), 0)),
            pl.BlockSpec((8, SSD_CONV_DIM), lambda b, j: (jnp.maximum(ch(b, j) * sub - 1, 0), 0)),
            pl.BlockSpec((8, SSD_CONV_DIM), lambda b, j: (jnp.minimum((ch(b, j) + 1) * sub, n_chunk_total * sub - 1), 0)),
            pl.BlockSpec((q, SSD_DT_PAD), lambda b, j: (ch(b, j), 0)),
            pl.BlockSpec((SSD_DT_PAD, q), lambda b, j: (0, ch(b, j))),
        ]

    a = -jnp.exp(a_log.astype(f32))
    expand = jnp.repeat(jnp.eye(nh, dtype=f32), SSD_HEADDIM, axis=1)
    e3 = jnp.concatenate([expand, expand, expand], axis=0).astype(bf16)
    dskip = jnp.repeat(d, SSD_HEADDIM)[None, :]
    yf, yb = pl.pallas_call(
        functools.partial(_ssd_scan_kernel, ctx_chunks=ctx_chunks, lat_chunks=lat_chunks),
        grid=(bsz, ctx_chunks + lat_chunks),
        in_specs=dir_specs(False) + dir_specs(True) + [
            whole((SSD_CONV, SSD_CONV_DIM)), whole((1, SSD_CONV_DIM)),
            whole((2, nh)), whole((2, nh)), whole((2, nh, 1)), whole((2, nh, 1)),
            whole((3 * nh, SSD_D_INNER)), whole((1, SSD_D_INNER)),
        ],
        out_specs=[
            pl.BlockSpec((q, SSD_D_INNER), lambda b, j: (chunk_of(b, j, False), 0)),
            pl.BlockSpec((q, SSD_D_INNER), lambda b, j: (chunk_of(b, j, True), 0)),
        ],
        out_shape=[jax.ShapeDtypeStruct((n_tok, SSD_D_INNER), f32)] * 2,
        scratch_shapes=[pltpu.VMEM((2, SSD_GROUPS, SSD_STATE, SSD_GW), f32)],
        compiler_params=pltpu.CompilerParams(dimension_semantics=("arbitrary", "arbitrary"), vmem_limit_bytes=VMEM_LIMIT),
        name="ssd_scan",
    )(xbc, xbc, xbc, dt, dtt, xbc, xbc, xbc, dt, dtt,
      conv_w, conv_b.reshape(1, -1), dt_bias, a, dt_bias.reshape(2, nh, 1), a.reshape(2, nh, 1), e3, dskip)

    return pl.pallas_call(
        _ssd_out_kernel,
        grid=(n_tok // tm,),
        in_specs=[
            pl.BlockSpec((tm, SSD_D_INNER), lambda i: (i, 0)),
            pl.BlockSpec((tm, SSD_D_INNER), lambda i: (i, 0)),
            pl.BlockSpec((tm, SSD_D_INNER), lambda i: (i, 0)),
            pl.BlockSpec((tm, dm), lambda i: (i, 0)),
            pl.BlockSpec((1, 1, 6 * dm), lambda i: (mod_idx(i), 0, 0)),
            whole((1, SSD_D_INNER)), whole((SSD_D_INNER, dm)), whole((1, dm)), whole((1, dm)),
        ],
        out_specs=pl.BlockSpec((tm, dm), lambda i: (i, 0)),
        out_shape=jax.ShapeDtypeStruct((n_tok, dm), f32),
        compiler_params=params,
        name="ssd_out_ln",
    )(yf, yb, z, xa, mod3, norm_w.reshape(1, -1), w_out.astype(bf16), ln_g.reshape(1, dm), ln_b.reshape(1, dm))


def kernel(x, c, ctx, c_ctx, mod_w, mod_b, ln_g, ln_b,
           s5_a_re, s5_a_im, s5_log_dt, s5_b_re, s5_b_im, s5_c_re, s5_c_im, s5_d, s5_w_gate, s5_w_val,
           ssd_w_in, ssd_conv_w, ssd_conv_b, ssd_dt_bias, ssd_a_log, ssd_d, ssd_norm_w, ssd_w_out,
           mla_w_down, mla_q_norm, mla_kv_norm, mla_w_uq, mla_w_uk, mla_w_uv, mla_w_o,
           peer_w_q, peer_subkeys, peer_u, peer_v):
    ROWS = x.shape[1] // GRID_W
    pos = grid_positions(ROWS)
    ctx_len = ctx.shape[1]
    c_act = jax.nn.silu(c)
    c_ctx_act = jax.nn.silu(c_ctx)
    bsz, seq_len, dm = x.shape
    n_lat, n_ctx = bsz * seq_len, bsz * ctx_len
    xa = jnp.concatenate([x.reshape(n_lat, dm), ctx.reshape(n_ctx, dm)], axis=0)
    for i in range(DEPTH):
        last = i == DEPTH - 1
        mod_tab = jnp.concatenate([c_act, c_ctx_act[None], jnp.zeros((7 - bsz, dm), x.dtype)], axis=0) @ mod_w[i] + mod_b[i]
        mod3 = mod_tab.reshape(8, 1, 6 * dm)
        kind, j = i % N_MIXERS, i // N_MIXERS
        if kind == 0:
            prep = _s5_prep(s5_a_re[j], s5_a_im[j], s5_log_dt[j], s5_b_re[j], s5_b_im[j], s5_c_re[j], s5_c_im[j], s5_d[j])
            ya = s5_ssm_pallas(xa, mod_tab, n_lat, seq_len, bsz, prep)
            if last:
                ya, xa = ya[:n_lat], xa[:n_lat]
            xa = glu_ln_pallas(ya, xa, mod3, n_lat, seq_len, bsz, s5_w_val[j], s5_w_gate[j], ln_g[i, 0], ln_b[i, 0])
        elif kind == 1:
            xa = ssd_ln_pallas(xa, mod3, n_lat, seq_len, bsz, ssd_w_in[j], ssd_conv_w[j], ssd_conv_b[j], ssd_dt_bias[j],
                               ssd_a_log[j], ssd_d[j], ssd_norm_w[j], ssd_w_out[j], ln_g[i, 0], ln_b[i, 0])
            if last:
                xa = xa[:n_lat]
        else:
            xa = mla_ln_pallas(xa, mod3, n_lat, seq_len, bsz, mla_w_down[j], mla_q_norm[j], mla_kv_norm[j],
                               mla_w_uq[j], mla_w_uk[j], mla_w_uv[j], mla_w_o[j], ln_g[i, 0], ln_b[i, 0])
            if last:
                xa = xa[:n_lat]
        xa = peer_ln_pallas(xa, mod3, n_lat, seq_len, bsz, peer_w_q[i], peer_subkeys[i], peer_u[i], peer_v[i],
                            ln_g[i, 1], ln_b[i, 1])
    return xa[:n_lat].reshape(bsz, seq_len, dm)
```

```python
import math
import functools
import jax
import jax.numpy as jnp
from jax import lax
import numpy as np
from jax.experimental import pallas as pl
from jax.experimental.pallas import tpu as pltpu

D_MODEL = 1024
BATCH = 4
SEQ = 4096
DEPTH = 4

GRID_W = 64
CTX_LEN = 256
N_MIXERS = 3
DEEPNORM_ALPHA = (2.0 * DEPTH) ** 0.25
LN_EPS = 1e-5
RMS_EPS = 1e-6
ROPE_BASE = 10000.0

S5_GROUP = 16
S5_GROUPS = D_MODEL // S5_GROUP
S5_STATE = 64

SSD_D_INNER = 2 * D_MODEL
SSD_HEADDIM = 64
SSD_HEADS = SSD_D_INNER // SSD_HEADDIM
SSD_GROUPS = 4
SSD_STATE = 128
SSD_CONV = 3
SSD_CHUNK = 128
SSD_CONV_DIM = SSD_D_INNER + 2 * SSD_GROUPS * SSD_STATE
SSD_IN_DIM = SSD_D_INNER + SSD_CONV_DIM + 2 * SSD_HEADS

MLA_HEADS = 16
MLA_Q_RANK = 256
MLA_KV_RANK = 128
MLA_NOPE = 64
MLA_ROPE = 32
MLA_V = 64
MLA_BLOCK = 128

PEER_HEADS = 8
PEER_KEYS = 128
PEER_EXPERTS = PEER_KEYS * PEER_KEYS
PEER_QDIM = 256
PEER_TOPK = 16
PEER_BLOCK = 128


def layer_norm(x, g, b):
    xf = x.astype(jnp.float32)
    mu = jnp.mean(xf, axis=-1, keepdims=True)
    var = jnp.mean(jnp.square(xf - mu), axis=-1, keepdims=True)
    return ((xf - mu) * lax.rsqrt(var + LN_EPS)).astype(x.dtype) * g + b


def _ln_kernel(x_ref, g_ref, b_ref, o_ref):
    xf = x_ref[...]
    mu = jnp.mean(xf, axis=-1, keepdims=True)
    xc = xf - mu
    var = jnp.mean(xc * xc, axis=-1, keepdims=True)
    o_ref[...] = xc * lax.rsqrt(var + LN_EPS) * g_ref[...] + b_ref[...]


def layer_norm_pallas(x, g, b):
    shp = x.shape
    x2 = x.reshape(-1, shp[-1])
    n, d = x2.shape
    tb = 512
    out = pl.pallas_call(
        _ln_kernel,
        grid=(n // tb,),
        in_specs=[pl.BlockSpec((tb, d), lambda i: (i, 0)),
                  pl.BlockSpec((1, d), lambda i: (0, 0)),
                  pl.BlockSpec((1, d), lambda i: (0, 0))],
        out_specs=pl.BlockSpec((tb, d), lambda i: (i, 0)),
        out_shape=jax.ShapeDtypeStruct((n, d), jnp.float32),
        name="final_ln",
    )(x2, g.reshape(1, d), b.reshape(1, d))
    return out.reshape(shp)


def rms_norm(x, g):
    xf = x.astype(jnp.float32)
    return (xf * lax.rsqrt(jnp.mean(jnp.square(xf), axis=-1, keepdims=True) + RMS_EPS)).astype(x.dtype) * g


def modulate(x, shift, scale):
    return x * (1.0 + scale) + shift


def grid_positions(rows):
    row = jnp.repeat(jnp.arange(rows, dtype=jnp.float32), GRID_W)
    col = jnp.tile(jnp.arange(GRID_W, dtype=jnp.float32), rows)
    return row, col


def rope_axial(x, row, col):
    half = x.shape[-1] // 2
    quarter = half // 2
    freqs = ROPE_BASE ** (-jnp.arange(quarter, dtype=jnp.float32) / quarter)

    def rot(xp, pos):
        ang = pos[:, None] * freqs
        cos = jnp.cos(ang)[None, :, None, :].astype(x.dtype)
        sin = jnp.sin(ang)[None, :, None, :].astype(x.dtype)
        x1, x2 = xp[..., :quarter], xp[..., quarter:]
        return jnp.concatenate([x1 * cos - x2 * sin, x2 * cos + x1 * sin], axis=-1)

    return jnp.concatenate([rot(x[..., :half], row), rot(x[..., half:], col)], axis=-1)


def _lin_rec(left, right):
    a1, b1 = left
    a2, b2 = right
    return a1 * a2, a2 * b1 + b2


def s5_scan(u, a_bar, b_bar, c_mat, init, reverse):
    bu = jnp.einsum('lgc,gpc->lgp', u.astype(jnp.float32), b_bar)
    a = jnp.broadcast_to(a_bar, bu.shape)
    a_cum, s = lax.associative_scan(_lin_rec, (a, bu), reverse=reverse, axis=0)
    s = s + a_cum * init
    y = jnp.einsum('lgp,gcp->lgc', s, c_mat).real
    final = s[0] if reverse else s[-1]
    return y, final


def s5_direction(u_ctx, u_lat, a_re, a_im, log_dt, b_re, b_im, c_re, c_im, reverse):
    lam = lax.complex(a_re.astype(jnp.float32), a_im.astype(jnp.float32))
    a_bar = jnp.exp(lam * jnp.exp(log_dt.astype(jnp.float32))[:, None])
    b_mat = lax.complex(b_re.astype(jnp.float32), b_im.astype(jnp.float32))
    b_bar = ((a_bar - 1.0) / lam)[..., None] * b_mat
    c_mat = lax.complex(c_re.astype(jnp.float32), c_im.astype(jnp.float32))

    def per_sample(args):
        uc, ul = args
        yc, sc = s5_scan(uc, a_bar, b_bar, c_mat, jnp.zeros_like(a_bar), reverse)
        yl, _ = s5_scan(ul, a_bar, b_bar, c_mat, sc, reverse)
        return yc, yl

    return lax.map(per_sample, (u_ctx, u_lat))


def s5_mixer(h, hc, a_re, a_im, log_dt, b_re, b_im, c_re, c_im, d, w_gate, w_val):
    bsz, seq_len, _ = h.shape
    ctx_len = hc.shape[1]
    u = h.reshape(bsz, seq_len, S5_GROUPS, S5_GROUP)
    uc = hc.reshape(bsz, ctx_len, S5_GROUPS, S5_GROUP)
    y_l = d * h
    y_c = d * hc
    for direction in range(2):
        yc, yl = s5_direction(uc, u, a_re[direction], a_im[direction], log_dt[direction],
                              b_re[direction], b_im[direction], c_re[direction], c_im[direction],
                              direction == 1)
        y_l = y_l + yl.reshape(bsz, seq_len, D_MODEL).astype(h.dtype)
        y_c = y_c + yc.reshape(bsz, ctx_len, D_MODEL).astype(h.dtype)

    def glu(y):
        g = jax.nn.gelu(y)
        return (g @ w_val) * jax.nn.sigmoid(g @ w_gate)

    return glu(y_l), glu(y_c)


def depthwise_conv_centred(x, w, b):
    k_w = w.shape[0]
    pad = k_w // 2
    seq_len = x.shape[1]
    xp = jnp.pad(x, ((0, 0), (pad, pad), (0, 0)))
    out = b
    for k in range(k_w):
        out = out + xp[:, k:k + seq_len] * w[k]
    return out


def ssd_scan(x, dt, a, bm, cm, init):
    bsz, seq_len, n_heads, p_dim = x.shape
    n_grp, n_st = bm.shape[2], bm.shape[3]
    hg = n_heads // n_grp
    q_len = SSD_CHUNK
    nc = seq_len // q_len
    xc = x.reshape(bsz, nc, q_len, n_grp, hg, p_dim)
    dtc = dt.reshape(bsz, nc, q_len, n_grp, hg).astype(jnp.float32)
    bc = bm.reshape(bsz, nc, q_len, n_grp, n_st)
    cc = cm.reshape(bsz, nc, q_len, n_grp, n_st)
    a_cum = jnp.cumsum(dtc * a.reshape(n_grp, hg), axis=2)
    xdt = xc * dtc[..., None].astype(x.dtype)
    tri = jnp.tril(jnp.ones((q_len, q_len), dtype=bool))
    seg = a_cum[:, :, :, None] - a_cum[:, :, None]
    decay = jnp.exp(jnp.where(tri[:, :, None, None], seg, -jnp.inf)).astype(x.dtype)
    cb = jnp.einsum('bcqgn,bcsgn->bcgqs', cc, bc)
    y_diag = jnp.einsum('bcgqs,bcqsgh,bcsghp->bcqghp', cb, decay, xdt)
    decay_states = jnp.exp(a_cum[:, :, -1:] - a_cum).astype(x.dtype)
    states = jnp.einsum('bcsgn,bcsgh,bcsghp->bcghpn', bc, decay_states, xdt)
    chunk_decay = jnp.exp(a_cum[:, :, -1]).astype(x.dtype)

    def step(carry, inp):
        dec, st = inp
        return carry * dec[..., None, None] + st, carry

    final, prev = lax.scan(step, init, (jnp.moveaxis(chunk_decay, 1, 0), jnp.moveaxis(states, 1, 0)))
    y_off = jnp.einsum('bcqgn,cbghpn,bcqgh->bcqghp', cc, prev, jnp.exp(a_cum).astype(x.dtype))
    return (y_diag + y_off).reshape(bsz, seq_len, n_heads, p_dim), final


def ssd_mixer(h, hc, w_in, conv_w, conv_b, dt_bias, a_log, d, norm_w, w_out):
    a = -jnp.exp(a_log.astype(jnp.float32))

    def flip(t):
        return jnp.flip(t, axis=1)

    def bidir(t, inits):
        bsz, seq_len = t.shape[0], t.shape[1]
        z, xbc, dt = jnp.split(t @ w_in, [SSD_D_INNER, SSD_D_INNER + SSD_CONV_DIM], axis=-1)
        xbc = jax.nn.silu(depthwise_conv_centred(xbc, conv_w, conv_b))
        xs, bm, cm = jnp.split(xbc, [SSD_D_INNER, SSD_D_INNER + SSD_GROUPS * SSD_STATE], axis=-1)
        xs = xs.reshape(bsz, seq_len, SSD_HEADS, SSD_HEADDIM)
        bm = bm.reshape(bsz, seq_len, SSD_GROUPS, SSD_STATE)
        cm = cm.reshape(bsz, seq_len, SSD_GROUPS, SSD_STATE)
        dt = jax.nn.softplus(dt.reshape(bsz, seq_len, 2, SSD_HEADS) + dt_bias)
        y_f, s_f = ssd_scan(xs, dt[:, :, 0], a[0], bm, cm, inits[0])
        y_b, s_b = ssd_scan(flip(xs), flip(dt[:, :, 1]), a[1], flip(bm), flip(cm), inits[1])
        y = y_f + flip(y_b) + d[:, None] * xs
        y = rms_norm(y.reshape(z.shape) * jax.nn.silu(z), norm_w)
        return y @ w_out, (s_f, s_b)

    zero = jnp.zeros((h.shape[0], SSD_GROUPS, SSD_HEADS // SSD_GROUPS, SSD_HEADDIM, SSD_STATE), h.dtype)
    out_c, states_c = bidir(hc, (zero, zero))
    out_l, _ = bidir(h, states_c)
    return out_l, out_c


def mla_project(t, w_down, q_norm, kv_norm, w_uq, w_uk, w_uv, pos):
    cq, ckv, kr = jnp.split(t @ w_down, [MLA_Q_RANK, MLA_Q_RANK + MLA_KV_RANK], axis=-1)
    q = jnp.einsum('btr,rhd->bthd', rms_norm(cq, q_norm), w_uq)
    ckv = rms_norm(ckv, kv_norm)
    k_nope = jnp.einsum('btr,rhd->bthd', ckv, w_uk)
    v = jnp.einsum('btr,rhd->bthd', ckv, w_uv)
    q_nope, q_rope = q[..., :MLA_NOPE], q[..., MLA_NOPE:]
    kr = kr[:, :, None, :]
    if pos is not None:
        q_rope = rope_axial(q_rope, pos[0], pos[1])
        kr = rope_axial(kr, pos[0], pos[1])
    k = jnp.concatenate([k_nope, jnp.broadcast_to(kr, k_nope.shape[:3] + (MLA_ROPE,))], axis=-1)
    q = jnp.concatenate([q_nope, q_rope], axis=-1)
    return q, k, v


def attend(q, k, v):
    s = jnp.einsum('bqhd,bkhd->bhqk', q, k).astype(jnp.float32) * (MLA_NOPE + MLA_ROPE) ** -0.5
    p = jax.nn.softmax(s, axis=-1).astype(v.dtype)
    return jnp.einsum('bhqk,bkhd->bqhd', p, v)


def mla_mixer(h, hc, pos, w_down, q_norm, kv_norm, w_uq, w_uk, w_uv, w_o):
    bsz, seq_len, _ = h.shape
    qc, kc, vc = mla_project(hc, w_down, q_norm, kv_norm, w_uq, w_uk, w_uv, None)
    ql, kl, vl = mla_project(h, w_down, q_norm, kv_norm, w_uq, w_uk, w_uv, pos)
    out_c = attend(qc, kc, vc).reshape(bsz, hc.shape[1], MLA_HEADS * MLA_V)
    k_all = jnp.concatenate([kc, kl], axis=1)
    v_all = jnp.concatenate([vc, vl], axis=1)
    nb = seq_len // MLA_BLOCK
    qb = jnp.moveaxis(ql.reshape(bsz, nb, MLA_BLOCK, MLA_HEADS, MLA_NOPE + MLA_ROPE), 1, 0)
    out_l = lax.map(lambda qblk: attend(qblk, k_all, v_all), qb)
    out_l = jnp.moveaxis(out_l, 0, 1).reshape(bsz, seq_len, MLA_HEADS * MLA_V)
    return out_l @ w_o, out_c @ w_o


def peer_ffn(h, w_q, subkeys, u_tab, v_tab):
    bsz, seq_len, dm = h.shape
    half = PEER_QDIM // 2
    q = jnp.einsum('btd,dhk->bthk', h, w_q)
    s1 = jnp.einsum('bthk,hnk->bthn', q[..., :half], subkeys[:, 0])
    s2 = jnp.einsum('bthk,hnk->bthn', q[..., half:], subkeys[:, 1])
    v1, i1 = lax.top_k(s1, PEER_TOPK)
    v2, i2 = lax.top_k(s2, PEER_TOPK)
    n_cand = PEER_TOPK * PEER_TOPK
    cand = (v1[..., :, None] + v2[..., None, :]).reshape(bsz, seq_len, PEER_HEADS, n_cand)
    cidx = (i1[..., :, None] * PEER_KEYS + i2[..., None, :]).reshape(bsz, seq_len, PEER_HEADS, n_cand)
    best, sel = lax.top_k(cand, PEER_TOPK)
    eidx = jnp.take_along_axis(cidx, sel, axis=-1)
    gate = jax.nn.softmax(best.astype(jnp.float32), axis=-1).astype(h.dtype)
    n_blk = (bsz * seq_len) // PEER_BLOCK
    n_sel = PEER_HEADS * PEER_TOPK
    hb = h.reshape(n_blk, PEER_BLOCK, dm)
    ib = eidx.reshape(n_blk, PEER_BLOCK, n_sel)
    gb = gate.reshape(n_blk, PEER_BLOCK, n_sel)

    def block(args):
        hk, ik, gk = args
        u = jnp.take(u_tab, ik, axis=0)
        act = jax.nn.gelu(jnp.einsum('td,ted->te', hk, u))
        v = jnp.take(v_tab, ik, axis=0)
        return jnp.einsum('te,ted->td', gk * act, v)

    return lax.map(block, (hb, ib, gb)).reshape(bsz, seq_len, dm)


PEER_TM = 512
PEER_EB = 1024
PEER_LANES = 128
PEER_ACT_ROWS = 256
VMEM_LIMIT = 56 * 1024 * 1024
NEG_INF = float("-inf")
GELU_C0 = math.sqrt(2.0 / math.pi)
GELU_C1 = 0.044715 * GELU_C0


def _top16_sorted(s):
    n = PEER_TOPK
    m = s.shape[0] // 8
    x = [s[8 * r:8 * r + 8, :] for r in range(m)]

    def exchange(i, l, descending):
        hi, lo = jnp.maximum(x[i], x[l]), jnp.minimum(x[i], x[l])
        x[i], x[l] = (hi, lo) if descending else (lo, hi)

    def merge_bitonic(size):
        j = size // 2
        while j >= 1:
            for i in range(size):
                if i ^ j > i:
                    exchange(i, i ^ j, True)
            j //= 2

    k = 2
    while k < m:
        j = k // 2
        while j >= 1:
            for i in range(m):
                if i ^ j > i:
                    exchange(i, i ^ j, (i & k) == 0)
            j //= 2
        k *= 2
    merge_bitonic(m)
    shifts = [4, 2, 1]
    if m < n:
        first = shifts.pop(0)
        x = x + [pltpu.roll(x[m - 1 - r], first, axis=0) for r in range(m)]
        merge_bitonic(n)
    for shift in shifts:
        other = [pltpu.roll(x[n - 1 - r], shift, axis=0) for r in range(n)]
        for r in range(n):
            x[r] = jnp.maximum(x[r], other[r])
        merge_bitonic(n)
    return [x[r][0:1, :] for r in range(n)]


def _dot3(a_hi, a_lo, b_hi, b_lo):
    f32 = jnp.float32
    return (jnp.dot(a_hi, b_hi, preferred_element_type=f32)
            + jnp.dot(a_hi, b_lo, preferred_element_type=f32)
            + jnp.dot(a_lo, b_hi, preferred_element_type=f32))


def _split_bf16(v):
    hi = v.astype(jnp.bfloat16)
    lo = (v - hi.astype(jnp.float32)).astype(jnp.bfloat16)
    return hi, lo


def _peer_kernel(x_ref, mod_ref, wq_ref, skh_ref, skl_ref, u_ref, vt_ref, vtl_ref, lng_ref, lnb_ref,
                 o_ref,
                 ht_ref, n1_ref, e1_ref, r2_ref, e2_ref, v1_ref, v2_ref, cand_ref,
                 act_ref, gact_ref, acc_ref):
    f32 = jnp.float32
    bf16 = jnp.bfloat16
    eb = pl.program_id(1)
    n_eb = pl.num_programs(1)
    dm = x_ref.shape[1]
    a_per = PEER_EB // PEER_KEYS
    half = PEER_QDIM // 2

    @pl.when(eb == 0)
    def _prologue():
        mod = mod_ref[0]
        h = x_ref[...] * (1.0 + mod[:, 4 * dm:5 * dm]) + mod[:, 3 * dm:4 * dm]
        ht = h.T.astype(bf16)
        ht_ref[...] = ht
        qt = jnp.dot(wq_ref[...], ht, preferred_element_type=f32)
        cand_ref[...] = jnp.full(cand_ref.shape, NEG_INF, f32)
        gact_ref[...] = jnp.zeros(gact_ref.shape, bf16)
        for hd in range(PEER_HEADS):
            q1_hi, q1_lo = _split_bf16(qt[hd * PEER_QDIM: hd * PEER_QDIM + half])
            q2_hi, q2_lo = _split_bf16(qt[hd * PEER_QDIM + half: (hd + 1) * PEER_QDIM])
            s1 = _dot3(skh_ref[hd, 0], skl_ref[hd, 0], q1_hi, q1_lo)
            s2 = _dot3(skh_ref[hd, 1], skl_ref[hd, 1], q2_hi, q2_lo)
            top1 = _top16_sorted(s1)
            top2 = _top16_sorted(s2)
            for k in range(PEER_TOPK):
                v1_ref[k:k + 1, :] = top1[k]
                v2_ref[k:k + 1, :] = top2[k]
            off = 0
            for i in range(PEER_TOPK):
                cnt = PEER_TOPK // (i + 1)
                cand_ref[off:off + cnt, :] = v1_ref[i:i + 1, :] + v2_ref[0:cnt, :]
                off += cnt
            best = _top16_sorted(cand_ref[...])
            z = jnp.zeros_like(best[0])
            for k in range(PEER_TOPK):
                z = z + jnp.exp(best[k] - best[0])
            tau = best[PEER_TOPK - 1]
            n1 = jnp.zeros(s1.shape, f32)
            rank2 = jnp.zeros(s2.shape, f32)
            for k in range(PEER_TOPK):
                n1 = jnp.where(s1 + top2[k] >= tau, k + 1.0, n1)
                rank2 = jnp.where(top2[k] > s2, k + 1.0, rank2)
            n1_ref[hd] = n1
            r2_ref[hd] = rank2.astype(bf16)
            e1_ref[hd] = jnp.exp(s1 - top1[0])
            e2_ref[hd] = (jnp.exp(s2 - top2[0]) * (0.5 / z)).astype(bf16)
        acc_ref[...] = jnp.zeros(acc_ref.shape, f32)

    par = eb % 2
    for r0 in range(0, PEER_EB, PEER_ACT_ROWS):
        act_ref[r0:r0 + PEER_ACT_ROWS, :] = jnp.dot(u_ref[r0:r0 + PEER_ACT_ROWS, :], ht_ref[...],
                                                    preferred_element_type=f32)
    acc_ref[...] += jnp.dot(vt_ref[0], gact_ref[1 - par], preferred_element_type=f32)
    a_base = pl.multiple_of(eb * a_per, a_per)
    blk = (PEER_KEYS, PEER_LANES)
    for a in range(a_per):
        for lg in range(0, x_ref.shape[0], PEER_LANES):
            lanes = slice(lg, lg + PEER_LANES)
            g = None
            for hd in range(PEER_HEADS):
                n1a = jnp.broadcast_to(n1_ref[hd, pl.ds(a_base, a_per), lanes][a:a + 1, :].astype(bf16), blk)
                e1a = jnp.broadcast_to(e1_ref[hd, pl.ds(a_base, a_per), lanes][a:a + 1, :].astype(bf16), blk)
                w = jnp.where(r2_ref[hd, :, lanes] < n1a, e2_ref[hd, :, lanes], jnp.zeros(blk, bf16)) * e1a
                g = w if g is None else g + w
            rows = slice(a * PEER_KEYS, (a + 1) * PEER_KEYS)
            x = act_ref[rows, lanes]
            t = jnp.tanh(x * (GELU_C0 + GELU_C1 * (x * x)))
            gact_ref[par, rows, lanes] = g * (x + x * t).astype(bf16)

    @pl.when(eb == n_eb - 1)
    def _epilogue():
        mod = mod_ref[0]
        ffn_t = acc_ref[...] + jnp.dot(vtl_ref[0], gact_ref[par], preferred_element_type=f32)
        y = DEEPNORM_ALPHA * x_ref[...] + mod[:, 5 * dm:6 * dm] * ffn_t.T
        mu = jnp.mean(y, axis=-1, keepdims=True)
        yc = y - mu
        var = jnp.mean(yc * yc, axis=-1, keepdims=True)
        o_ref[...] = yc * lax.rsqrt(var + LN_EPS) * lng_ref[...] + lnb_ref[...]


def _mod_row_index(i, tm, n_lat, seq, bsz):
    return jnp.where(i * tm < n_lat, (i * tm) // seq, bsz)


def peer_ln_pallas(xa, mod3, n_lat, seq, bsz, w_q, subkeys, u_tab, v_tab, ln_g, ln_b):
    n_tok, dm = xa.shape
    tm = PEER_TM
    f32, bf16 = jnp.float32, jnp.bfloat16
    wq_t = w_q.reshape(dm, PEER_HEADS * PEER_QDIM).T
    wq_bf = wq_t.astype(bf16)
    sk_hi = subkeys.astype(bf16)
    sk_lo = (subkeys - sk_hi.astype(f32)).astype(bf16)
    u_bf = u_tab.astype(bf16)
    n_eb = PEER_EXPERTS // PEER_EB
    vt_bf = v_tab.reshape(n_eb, PEER_EB, dm).transpose(0, 2, 1).astype(bf16)
    qd = PEER_HEADS * PEER_QDIM
    n_cand = sum(PEER_TOPK // (i + 1) for i in range(PEER_TOPK))
    n_cand_pad = 64
    assert n_cand <= n_cand_pad
    mod_idx = functools.partial(_mod_row_index, tm=tm, n_lat=n_lat, seq=seq, bsz=bsz)
    return pl.pallas_call(
        _peer_kernel,
        grid=(n_tok // tm, n_eb),
        in_specs=[
            pl.BlockSpec((tm, dm), lambda i, e: (i, 0)),
            pl.BlockSpec((1, 1, 6 * dm), lambda i, e: (mod_idx(i), 0, 0)),
            pl.BlockSpec((qd, dm), lambda i, e: (0, 0)),
            pl.BlockSpec((PEER_HEADS, 2, PEER_KEYS, PEER_QDIM // 2), lambda i, e: (0, 0, 0, 0)),
            pl.BlockSpec((PEER_HEADS, 2, PEER_KEYS, PEER_QDIM // 2), lambda i, e: (0, 0, 0, 0)),
            pl.BlockSpec((PEER_EB, dm), lambda i, e: (e, 0)),
            pl.BlockSpec((1, dm, PEER_EB), lambda i, e: (jnp.maximum(e - 1, 0), 0, 0)),
            pl.BlockSpec((1, dm, PEER_EB), lambda i, e: (n_eb - 1, 0, 0)),
            pl.BlockSpec((1, dm), lambda i, e: (0, 0)),
            pl.BlockSpec((1, dm), lambda i, e: (0, 0)),
        ],
        out_specs=pl.BlockSpec((tm, dm), lambda i, e: (i, 0)),
        out_shape=jax.ShapeDtypeStruct((n_tok, dm), f32),
        scratch_shapes=[
            pltpu.VMEM((dm, tm), bf16),
            pltpu.VMEM((PEER_HEADS, PEER_KEYS, tm), f32),
            pltpu.VMEM((PEER_HEADS, PEER_KEYS, tm), f32),
            pltpu.VMEM((PEER_HEADS, PEER_KEYS, tm), bf16),
            pltpu.VMEM((PEER_HEADS, PEER_KEYS, tm), bf16),
            pltpu.VMEM((PEER_TOPK, tm), f32),
            pltpu.VMEM((PEER_TOPK, tm), f32),
            pltpu.VMEM((n_cand_pad, tm), f32),
            pltpu.VMEM((PEER_EB, tm), f32),
            pltpu.VMEM((2, PEER_EB, tm), bf16),
            pltpu.VMEM((dm, tm), f32),
        ],
        compiler_params=pltpu.CompilerParams(
            dimension_semantics=("arbitrary", "arbitrary"),
            vmem_limit_bytes=VMEM_LIMIT),
        name="peer_ln",
    )(xa, mod3, wq_bf, sk_hi, sk_lo, u_bf, vt_bf, vt_bf, ln_g.reshape(1, dm), ln_b.reshape(1, dm))


S5_CHUNK = 16
S5_LEVELS = 9


def _s5_prep(a_re, a_im, log_dt, b_re, b_im, c_re, c_im, d):
    f32, bf16 = jnp.float32, jnp.bfloat16
    hp = lax.Precision.HIGHEST
    n_g, n_p, q = S5_GROUPS, S5_STATE, S5_CHUNK
    dt = jnp.exp(log_dt.astype(f32))[..., None]

    def apow(n):
        mag = jnp.exp(a_re * dt * n)
        ang = a_im * dt * n
        return mag * jnp.cos(ang), mag * jnp.sin(ang)

    ar1, ai1 = apow(1.0)
    den = a_re * a_re + a_im * a_im
    nr, ni = ar1 - 1.0, ai1
    cr = (nr * a_re + ni * a_im) / den
    ci = (ni * a_re - nr * a_im) / den
    bbr = cr[..., None] * b_re - ci[..., None] * b_im
    bbi = cr[..., None] * b_im + ci[..., None] * b_re
    lag = jnp.arange(q + 1, dtype=f32)[:, None, None, None]
    pr, pi = apow(lag)
    mr = pr[..., None] * bbr - pi[..., None] * bbi
    mi = pr[..., None] * bbi + pi[..., None] * bbr
    kmat = (jnp.einsum('dgcp,ndgpk->ndgck', c_re, mr, precision=hp)
            - jnp.einsum('dgcp,ndgpk->ndgck', c_im, mi, precision=hp))
    r_idx = jnp.arange(q)[:, None]
    t_idx = jnp.arange(q)[None, :]

    def toeplitz(kd, lagm):
        blk = kd[jnp.clip(lagm, 0, q)]
        blk = jnp.where((lagm >= 0)[:, :, None, None, None], blk, 0.0)
        return blk.transpose(2, 0, 4, 1, 3).reshape(n_g, q * S5_GROUP, q * S5_GROUP)

    eye = jnp.eye(q * S5_GROUP, dtype=f32)
    dvec = jnp.tile(d.reshape(n_g, 1, S5_GROUP), (1, q, 1)).reshape(n_g, q * S5_GROUP)
    tsum = toeplitz(kmat[:, 0], t_idx - r_idx) + toeplitz(kmat[:, 1], r_idx - t_idx) + eye[None] * dvec[:, None, :]

    def w_in(direction, exps):
        wr = mr[exps, direction].transpose(1, 0, 3, 2).reshape(n_g, q * S5_GROUP, n_p)
        wi = mi[exps, direction].transpose(1, 0, 3, 2).reshape(n_g, q * S5_GROUP, n_p)
        return wr, wi

    def w_out(direction, exps):
        pre, pim = pr[exps, direction], pi[exps, direction]
        cre, cim = c_re[direction], c_im[direction]
        wre = cre[None] * pre[:, :, None, :] - cim[None] * pim[:, :, None, :]
        wim = -(cre[None] * pim[:, :, None, :] + cim[None] * pre[:, :, None, :])
        return (wre.transpose(1, 3, 0, 2).reshape(n_g, n_p, q * S5_GROUP),
                wim.transpose(1, 3, 0, 2).reshape(n_g, n_p, q * S5_GROUP))

    steps = jnp.arange(q)
    win = [w_in(0, q - 1 - steps), w_in(1, steps)]
    wout = [w_out(0, steps + 1), w_out(1, q - steps)]
    zc = jnp.zeros((n_g, q * S5_GROUP, n_p), f32)
    zr = jnp.zeros((n_g, n_p, q * S5_GROUP), f32)
    win_p, wout_p = [], []
    for direction in range(2):
        wr, wi = win[direction]
        even = jnp.concatenate([wr, zc, wi, zc], axis=2)
        odd = jnp.concatenate([zc, wr, zc, wi], axis=2)
        is_odd = (jnp.arange(n_g) % 2 == 1)[:, None, None]
        win_p.append(jnp.where(is_odd, odd, even))
        vr, vi = wout[direction]
        even = jnp.concatenate([vr, zr, vi, zr], axis=1)
        odd = jnp.concatenate([zr, vr, zr, vi], axis=1)
        wout_p.append(jnp.where(is_odd, odd, even))
    win_p = jnp.stack(win_p, axis=1).reshape(n_g // 2, 2, 2, 4 * n_p, 4 * n_p).transpose(0, 2, 1, 3, 4)
    wout_p = jnp.stack(wout_p, axis=1).reshape(n_g // 2, 2, 2, 4 * n_p, 4 * n_p).transpose(0, 2, 1, 3, 4)
    lvl = (q * 2.0 ** jnp.arange(S5_LEVELS, dtype=f32))[:, None, None, None]
    lr, li = apow(lvl)
    pw = jnp.stack([lr, li], axis=1)
    pw = pw.transpose(3, 2, 0, 1, 4).reshape(n_g // 2, 2, 2, S5_LEVELS, 2, n_p)
    pw = pw.transpose(0, 2, 3, 4, 1, 5).reshape(n_g // 2, 2, 2 * S5_LEVELS, 2 * n_p)
    return tsum.astype(bf16).reshape(n_g // 2, 2, q * S5_GROUP, q * S5_GROUP), win_p.astype(bf16), wout_p.astype(bf16), pw


def _shift_rows(v, r, up):
    n = v.shape[0]
    row = lax.broadcasted_iota(jnp.int32, v.shape, 0)
    if up:
        return jnp.where(row < n - r, pltpu.roll(v, n - r, axis=0), 0.0)
    return jnp.where(row >= r, pltpu.roll(v, r, axis=0), 0.0)


def _chunk_scan(re, im, pw, rows_per_chunk, up):
    n_chunks = re.shape[0] // rows_per_chunk
    level, s = 0, 1
    while s < n_chunks:
        ar, ai = pw[2 * level:2 * level + 1, :], pw[2 * level + 1:2 * level + 2, :]
        sre = _shift_rows(re, s * rows_per_chunk, up)
        sim = _shift_rows(im, s * rows_per_chunk, up)
        re, im = re + ar * sre - ai * sim, im + ar * sim + ai * sre
        level, s = level + 1, 2 * s
    return re, im


def _s5_kernel(u_ref, scl_ref, shl_ref, scc_ref, shc_ref, t_ref, win_ref, wout_ref, pw_ref, y_ref, *, bsz, ctx_chunks):
    f32, bf16 = jnp.float32, jnp.bfloat16
    n = u_ref.shape[1]
    width = u_ref.shape[2]
    rc = ctx_chunks * bsz
    hs = []
    for gi in range(2):
        xv = u_ref[gi].reshape(n // 8, 8, width)
        hl = xv * scl_ref[gi][None] + shl_ref[gi][None]
        hc = xv * scc_ref[gi][None] + shc_ref[gi][None]
        slab = lax.broadcasted_iota(jnp.int32, xv.shape, 0)
        hs.append(jnp.where(slab < rc // 8, hc, hl).reshape(n, width).astype(bf16))
    y = [jnp.dot(hs[gi], t_ref[0, gi], preferred_element_type=f32) for gi in range(2)]
    half = width // 2
    for direction in range(2):
        sloc = (jnp.dot(hs[0], win_ref[0, direction, 0], preferred_element_type=f32)
                + jnp.dot(hs[1], win_ref[0, direction, 1], preferred_element_type=f32))
        re, im = sloc[:, :half], sloc[:, half:]
        pw = pw_ref[0, direction]
        if direction == 0:
            sre, sim = _chunk_scan(_shift_rows(re, bsz, False), _shift_rows(im, bsz, False), pw, bsz, False)
        else:
            cre, cim = _chunk_scan(_shift_rows(re[:rc], bsz, True), _shift_rows(im[:rc], bsz, True), pw, bsz, True)
            ar, ai = pw[0:1, :], pw[1:2, :]
            fre = ar * cre[0:8] - ai * cim[0:8] + re[0:8]
            fim = ar * cim[0:8] + ai * cre[0:8] + im[0:8]
            row8 = lax.broadcasted_iota(jnp.int32, fre.shape, 0)
            tre = jnp.where(row8 >= 8 - bsz, pltpu.roll(fre, 8 - bsz, axis=0), 0.0)
            tim = jnp.where(row8 >= 8 - bsz, pltpu.roll(fim, 8 - bsz, axis=0), 0.0)
            lre, lim = _shift_rows(re[rc:], bsz, True), _shift_rows(im[rc:], bsz, True)
            lre = jnp.concatenate([lre[:-8], lre[-8:] + tre], axis=0)
            lim = jnp.concatenate([lim[:-8], lim[-8:] + tim], axis=0)
            lre, lim = _chunk_scan(lre, lim, pw, bsz, True)
            sre = jnp.concatenate([cre, lre], axis=0)
            sim = jnp.concatenate([cim, lim], axis=0)
        s_in = jnp.concatenate([sre, sim], axis=1).astype(bf16)
        for gi in range(2):
            y[gi] = y[gi] + jnp.dot(s_in, wout_ref[0, direction, gi], preferred_element_type=f32)
    for gi in range(2):
        y_ref[gi] = y[gi]


def s5_ssm_pallas(xa, mod_tab, n_lat, seq, bsz, prep):
    tsum, win_p, wout_p, pw = prep
    f32 = jnp.float32
    n_tok, dm = xa.shape
    ctx_len = (n_tok - n_lat) // bsz
    q = S5_CHUNK
    n_chunks = (seq + ctx_len) // q
    width = q * S5_GROUP
    full = jnp.concatenate([xa[n_lat:].reshape(bsz, ctx_len, dm), xa[:n_lat].reshape(bsz, seq, dm)], axis=1)
    u = full.reshape(bsz, n_chunks, q, S5_GROUPS, S5_GROUP).transpose(3, 1, 0, 2, 4).reshape(S5_GROUPS, n_chunks * bsz, width)

    def tile(vec_rows):
        rows = vec_rows.shape[0]
        t = vec_rows.reshape(rows, S5_GROUPS, 1, S5_GROUP)
        t = jnp.broadcast_to(t, (rows, S5_GROUPS, q, S5_GROUP)).reshape(rows, S5_GROUPS, width)
        return jnp.tile(t.transpose(1, 0, 2), (1, 8 // rows, 1))

    scl, shl = tile(1.0 + mod_tab[:bsz, dm:2 * dm]), tile(mod_tab[:bsz, 0:dm])
    scc = tile(jnp.broadcast_to(1.0 + mod_tab[bsz:bsz + 1, dm:2 * dm], (bsz, dm)))
    shc = tile(jnp.broadcast_to(mod_tab[bsz:bsz + 1, 0:dm], (bsz, dm)))
    n_rows = n_chunks * bsz
    tile_spec = pl.BlockSpec((2, 8, width), lambda p: (p, 0, 0))
    y = pl.pallas_call(
        functools.partial(_s5_kernel, bsz=bsz, ctx_chunks=ctx_len // q),
        grid=(S5_GROUPS // 2,),
        in_specs=[
            pl.BlockSpec((2, n_rows, width), lambda p: (p, 0, 0)),
            tile_spec, tile_spec, tile_spec, tile_spec,
            pl.BlockSpec((1, 2, width, width), lambda p: (p, 0, 0, 0)),
            pl.BlockSpec((1, 2, 2, width, width), lambda p: (p, 0, 0, 0, 0)),
            pl.BlockSpec((1, 2, 2, width, width), lambda p: (p, 0, 0, 0, 0)),
            pl.BlockSpec((1, 2, 2 * S5_LEVELS, 2 * S5_STATE), lambda p: (p, 0, 0, 0)),
        ],
        out_specs=pl.BlockSpec((2, n_rows, width), lambda p: (p, 0, 0)),
        out_shape=jax.ShapeDtypeStruct((S5_GROUPS, n_rows, width), f32),
        compiler_params=pltpu.CompilerParams(dimension_semantics=("arbitrary",), vmem_limit_bytes=VMEM_LIMIT),
        name="s5_ssm",
    )(u, scl, shl, scc, shc, tsum, win_p, wout_p, pw)
    yf = y.reshape(S5_GROUPS, n_chunks, bsz, q, S5_GROUP).transpose(2, 1, 3, 0, 4).reshape(bsz, seq + ctx_len, dm)
    return jnp.concatenate([yf[:, ctx_len:].reshape(n_lat, dm), yf[:, :ctx_len].reshape(bsz * ctx_len, dm)], axis=0)


def _glu_ln_kernel(y_ref, x_ref, mod_ref, wv_ref, wg_ref, lng_ref, lnb_ref, o_ref):
    f32 = jnp.float32
    dm = x_ref.shape[1]
    g = jax.nn.gelu(y_ref[...]).astype(jnp.bfloat16)
    val = jnp.dot(g, wv_ref[...], preferred_element_type=f32)
    gate = jnp.dot(g, wg_ref[...], preferred_element_type=f32)
    out = val * jax.nn.sigmoid(gate)
    mod = mod_ref[0]
    z = DEEPNORM_ALPHA * x_ref[...] + mod[:, 2 * dm:3 * dm] * out
    mu = jnp.mean(z, axis=-1, keepdims=True)
    zc = z - mu
    var = jnp.mean(zc * zc, axis=-1, keepdims=True)
    o_ref[...] = zc * lax.rsqrt(var + LN_EPS) * lng_ref[...] + lnb_ref[...]


TOK_TM = 512


def glu_ln_pallas(ya, xa, mod3, n_lat, seq, bsz, w_val, w_gate, ln_g, ln_b):
    n_tok, dm = ya.shape
    tm = TOK_TM
    bf16 = jnp.bfloat16
    mod_idx = functools.partial(_mod_row_index, tm=tm, n_lat=n_lat, seq=seq, bsz=bsz)
    return pl.pallas_call(
        _glu_ln_kernel,
        grid=(n_tok // tm,),
        in_specs=[
            pl.BlockSpec((tm, dm), lambda i: (i, 0)),
            pl.BlockSpec((tm, dm), lambda i: (i, 0)),
            pl.BlockSpec((1, 1, 6 * dm), lambda i: (mod_idx(i), 0, 0)),
            pl.BlockSpec((dm, dm), lambda i: (0, 0)),
            pl.BlockSpec((dm, dm), lambda i: (0, 0)),
            pl.BlockSpec((1, dm), lambda i: (0, 0)),
            pl.BlockSpec((1, dm), lambda i: (0, 0)),
        ],
        out_specs=pl.BlockSpec((tm, dm), lambda i: (i, 0)),
        out_shape=jax.ShapeDtypeStruct((n_tok, dm), jnp.float32),
        compiler_params=pltpu.CompilerParams(dimension_semantics=("arbitrary",), vmem_limit_bytes=VMEM_LIMIT),
        name="glu_ln",
    )(ya, xa, mod3, w_val.astype(bf16), w_gate.astype(bf16), ln_g.reshape(1, dm), ln_b.reshape(1, dm))


MLA_HD = 128
MLA_TQ = 256
MLA_HB = 4


def _mla_prep(w_down, w_uq, w_uk, w_uv):
    bf16 = jnp.bfloat16
    quarter = MLA_ROPE // 4
    swap = np.concatenate([np.arange(quarter, 2 * quarter), np.arange(0, quarter),
                           np.arange(3 * quarter, 4 * quarter), np.arange(2 * quarter, 3 * quarter)])
    dm = w_down.shape[0]
    w_cq = w_down[:, :MLA_Q_RANK]
    w_ckv = w_down[:, MLA_Q_RANK:MLA_Q_RANK + MLA_KV_RANK]
    w_kr = w_down[:, MLA_Q_RANK + MLA_KV_RANK:]
    wd_t = jnp.concatenate([w_cq, w_ckv, jnp.zeros((dm, MLA_NOPE), w_down.dtype), w_kr, w_kr[:, swap]], axis=1).T
    rope = w_uq[:, :, MLA_NOPE:]
    wuq_t = jnp.concatenate([w_uq, rope[:, :, swap]], axis=2).reshape(MLA_Q_RANK, MLA_HEADS * MLA_HD).T
    wuk = jnp.concatenate([w_uk, jnp.zeros((MLA_KV_RANK, MLA_HEADS, MLA_HD - MLA_NOPE), w_uk.dtype)], axis=2)
    wuk = wuk.reshape(MLA_KV_RANK, MLA_HEADS * MLA_HD)
    wuv_t = w_uv.reshape(MLA_KV_RANK, MLA_HEADS * MLA_V).T
    return wd_t.astype(bf16), wuq_t.astype(bf16), wuk.astype(bf16), wuv_t.astype(bf16)


def _rope_tables(seq, tm):
    quarter = MLA_ROPE // 4
    freqs = ROPE_BASE ** (-jnp.arange(quarter, dtype=jnp.float32) / quarter)
    t = jnp.arange(seq, dtype=jnp.float32)
    row, col = jnp.floor(t / GRID_W), t - GRID_W * jnp.floor(t / GRID_W)
    ang_r, ang_c = freqs[:, None] * row[None, :], freqs[:, None] * col[None, :]
    cos32 = jnp.concatenate([jnp.cos(ang_r), jnp.cos(ang_r), jnp.cos(ang_c), jnp.cos(ang_c)], axis=0)
    sin32 = jnp.concatenate([-jnp.sin(ang_r), jnp.sin(ang_r), -jnp.sin(ang_c), jnp.sin(ang_c)], axis=0)
    ones = jnp.ones((MLA_NOPE, seq + tm), jnp.float32)
    zeros = jnp.zeros((MLA_HD - MLA_NOPE - MLA_ROPE, seq + tm), jnp.float32)
    cos_t = jnp.concatenate([ones, jnp.concatenate([cos32, jnp.ones((MLA_ROPE, tm))], axis=1), zeros], axis=0)
    sin_t = jnp.concatenate([0.0 * ones, jnp.concatenate([sin32, jnp.zeros((MLA_ROPE, tm))], axis=1), zeros], axis=0)
    return cos_t, sin_t


def _mla_proj_kernel(x_ref, mod_ref, cos_ref, sin_ref, wd_ref, wuq_ref, wuk_ref, wuv_ref, qn_ref, kvn_ref,
                     qt_ref, k_ref, vt_ref):
    f32, bf16 = jnp.float32, jnp.bfloat16
    dm = x_ref.shape[1]
    mod = mod_ref[0]
    h = x_ref[...] * (1.0 + mod[:, dm:2 * dm]) + mod[:, 0:dm]
    ht = h.T.astype(bf16)
    dt = jnp.dot(wd_ref[...], ht, preferred_element_type=f32)
    cq, ckv, kr = dt[:MLA_Q_RANK], dt[MLA_Q_RANK:MLA_Q_RANK + MLA_KV_RANK], dt[MLA_Q_RANK + MLA_KV_RANK:]
    cqn = cq * lax.rsqrt(jnp.mean(cq * cq, axis=0, keepdims=True) + RMS_EPS) * qn_ref[...]
    ckvn = ckv * lax.rsqrt(jnp.mean(ckv * ckv, axis=0, keepdims=True) + RMS_EPS) * kvn_ref[...]
    cos_t, sin_t = cos_ref[...], sin_ref[...]

    def rope(v):
        shifted = jnp.concatenate([v[MLA_ROPE:], v[:MLA_ROPE]], axis=0)
        return v * cos_t + shifted * sin_t

    scale = (MLA_NOPE + MLA_ROPE) ** -0.5
    q_all = jnp.dot(wuq_ref[...], cqn.astype(bf16), preferred_element_type=f32)
    for hd in range(MLA_HEADS):
        qt_ref[hd] = (rope(q_all[hd * MLA_HD:(hd + 1) * MLA_HD]) * scale).astype(bf16)
    ckvn_bf = ckvn.astype(bf16)
    vt_ref[...] = jnp.dot(wuv_ref[...], ckvn_bf, preferred_element_type=f32).astype(bf16)
    k_all = jnp.dot(ckvn.T.astype(bf16), wuk_ref[...], preferred_element_type=f32)
    kr_rows = rope(kr).T
    for hd in range(MLA_HEADS):
        k_ref[hd] = (k_all[:, hd * MLA_HD:(hd + 1) * MLA_HD] + kr_rows).astype(bf16)


def _mla_attn_kernel(*refs, with_latent):
    f32, bf16 = jnp.float32, jnp.bfloat16
    if with_latent:
        q_ref, kl_ref, kc_ref, vl_ref, vc_ref, o_ref = refs
    else:
        q_ref, kc_ref, vc_ref, o_ref = refs
    for i in range(q_ref.shape[0]):
        rows = slice(i * MLA_V, (i + 1) * MLA_V)
        q = q_ref[i]
        s_c = jnp.dot(kc_ref[i], q, preferred_element_type=f32)
        m = jnp.max(s_c, axis=0, keepdims=True)
        if with_latent:
            s_l = jnp.dot(kl_ref[i], q, preferred_element_type=f32)
            m = jnp.maximum(m, jnp.max(s_l, axis=0, keepdims=True))
        p_c = jnp.exp(s_c - m)
        den = jnp.sum(p_c, axis=0, keepdims=True)
        o = jnp.dot(vc_ref[rows, :], p_c.astype(bf16), preferred_element_type=f32)
        if with_latent:
            p_l = jnp.exp(s_l - m)
            den = den + jnp.sum(p_l, axis=0, keepdims=True)
            o = o + jnp.dot(vl_ref[rows, :], p_l.astype(bf16), preferred_element_type=f32)
        o_ref[rows, :] = (o / den).astype(bf16)


def _mla_out_kernel(ot_ref, x_ref, mod_ref, wo_ref, lng_ref, lnb_ref, o_ref):
    f32 = jnp.float32
    dm = x_ref.shape[1]
    attn = ot_ref[...].astype(f32).T.astype(jnp.bfloat16)
    out = jnp.dot(attn, wo_ref[...], preferred_element_type=f32)
    mod = mod_ref[0]
    z = DEEPNORM_ALPHA * x_ref[...] + mod[:, 2 * dm:3 * dm] * out
    mu = jnp.mean(z, axis=-1, keepdims=True)
    zc = z - mu
    var = jnp.mean(zc * zc, axis=-1, keepdims=True)
    o_ref[...] = zc * lax.rsqrt(var + LN_EPS) * lng_ref[...] + lnb_ref[...]


def mla_ln_pallas(xa, mod3, n_lat, seq, bsz, w_down, q_norm, kv_norm, w_uq, w_uk, w_uv, w_o, ln_g, ln_b):
    f32, bf16 = jnp.float32, jnp.bfloat16
    n_tok, dm = xa.shape
    n_ctx = n_tok - n_lat
    ctx_len = n_ctx // bsz
    tm = TOK_TM
    wd_t, wuq_t, wuk, wuv_t = _mla_prep(w_down, w_uq, w_uk, w_uv)
    cos_t, sin_t = _rope_tables(seq, tm)
    mod_idx = functools.partial(_mod_row_index, tm=tm, n_lat=n_lat, seq=seq, bsz=bsz)
    tiles_per_sample = seq // tm
    pos_idx = lambda i: jnp.where(i * tm < n_lat, i % tiles_per_sample, tiles_per_sample)
    n_hd, n_dn = MLA_HEADS * MLA_HD, wd_t.shape[0]
    whole = lambda shape: pl.BlockSpec(shape, lambda i: (0,) * len(shape))
    params = pltpu.CompilerParams(dimension_semantics=("arbitrary",), vmem_limit_bytes=VMEM_LIMIT)
    qt, k, vt = pl.pallas_call(
        _mla_proj_kernel,
        grid=(n_tok // tm,),
        in_specs=[
            pl.BlockSpec((tm, dm), lambda i: (i, 0)),
            pl.BlockSpec((1, 1, 6 * dm), lambda i: (mod_idx(i), 0, 0)),
            pl.BlockSpec((MLA_HD, tm), lambda i: (0, pos_idx(i))),
            pl.BlockSpec((MLA_HD, tm), lambda i: (0, pos_idx(i))),
            whole((n_dn, dm)), whole((n_hd, MLA_Q_RANK)), whole((MLA_KV_RANK, n_hd)),
            whole((MLA_HEADS * MLA_V, MLA_KV_RANK)), whole((MLA_Q_RANK, 1)), whole((MLA_KV_RANK, 1)),
        ],
        out_specs=[
            pl.BlockSpec((MLA_HEADS, MLA_HD, tm), lambda i: (0, 0, i)),
            pl.BlockSpec((MLA_HEADS, tm, MLA_HD), lambda i: (0, i, 0)),
            pl.BlockSpec((MLA_HEADS * MLA_V, tm), lambda i: (0, i)),
        ],
        out_shape=[
            jax.ShapeDtypeStruct((MLA_HEADS, MLA_HD, n_tok), bf16),
            jax.ShapeDtypeStruct((MLA_HEADS, n_tok, MLA_HD), bf16),
            jax.ShapeDtypeStruct((MLA_HEADS * MLA_V, n_tok), bf16),
        ],
        compiler_params=params,
        name="mla_proj",
    )(xa, mod3, cos_t, sin_t, wd_t, wuq_t, wuk, wuv_t, q_norm.reshape(-1, 1), kv_norm.reshape(-1, 1))

    tq, hb = MLA_TQ, MLA_HB
    n_qt = seq // tq
    ctx_blk = n_lat // ctx_len
    params3 = pltpu.CompilerParams(dimension_semantics=("arbitrary",) * 3, vmem_limit_bytes=VMEM_LIMIT)
    ot_lat = pl.pallas_call(
        functools.partial(_mla_attn_kernel, with_latent=True),
        grid=(bsz, MLA_HEADS // hb, n_qt),
        in_specs=[
            pl.BlockSpec((hb, MLA_HD, tq), lambda b, hd, t: (hd, 0, b * n_qt + t)),
            pl.BlockSpec((hb, seq, MLA_HD), lambda b, hd, t: (hd, b, 0)),
            pl.BlockSpec((hb, ctx_len, MLA_HD), lambda b, hd, t: (hd, ctx_blk + b, 0)),
            pl.BlockSpec((hb * MLA_V, seq), lambda b, hd, t: (hd, b)),
            pl.BlockSpec((hb * MLA_V, ctx_len), lambda b, hd, t: (hd, ctx_blk + b)),
        ],
        out_specs=pl.BlockSpec((hb * MLA_V, tq), lambda b, hd, t: (hd, b * n_qt + t)),
        out_shape=jax.ShapeDtypeStruct((MLA_HEADS * MLA_V, n_lat), bf16),
        compiler_params=params3,
        name="mla_attn_latent",
    )(qt, k, k, vt, vt)
    ot_ctx = pl.pallas_call(
        functools.partial(_mla_attn_kernel, with_latent=False),
        grid=(bsz, MLA_HEADS // hb, 1),
        in_specs=[
            pl.BlockSpec((hb, MLA_HD, ctx_len), lambda b, hd, t: (hd, 0, ctx_blk + b)),
            pl.BlockSpec((hb, ctx_len, MLA_HD), lambda b, hd, t: (hd, ctx_blk + b, 0)),
            pl.BlockSpec((hb * MLA_V, ctx_len), lambda b, hd, t: (hd, ctx_blk + b)),
        ],
        out_specs=pl.BlockSpec((hb * MLA_V, ctx_len), lambda b, hd, t: (hd, b)),
        out_shape=jax.ShapeDtypeStruct((MLA_HEADS * MLA_V, n_ctx), bf16),
        compiler_params=params3,
        name="mla_attn_context",
    )(qt, k, vt)
    ot = jnp.concatenate([ot_lat, ot_ctx], axis=1)
    return pl.pallas_call(
        _mla_out_kernel,
        grid=(n_tok // tm,),
        in_specs=[
            pl.BlockSpec((MLA_HEADS * MLA_V, tm), lambda i: (0, i)),
            pl.BlockSpec((tm, dm), lambda i: (i, 0)),
            pl.BlockSpec((1, 1, 6 * dm), lambda i: (mod_idx(i), 0, 0)),
            whole((MLA_HEADS * MLA_V, dm)), whole((1, dm)), whole((1, dm)),
        ],
        out_specs=pl.BlockSpec((tm, dm), lambda i: (i, 0)),
        out_shape=jax.ShapeDtypeStruct((n_tok, dm), f32),
        compiler_params=params,
        name="mla_out_ln",
    )(ot, xa, mod3, w_o.astype(bf16), ln_g.reshape(1, dm), ln_b.reshape(1, dm))


SSD_DT_PAD = 128
SSD_HPG = SSD_HEADS // SSD_GROUPS
SSD_GW = SSD_HPG * SSD_HEADDIM


def _ssd_in_kernel(x_ref, mod_ref, w_ref, wdt_ref, z_ref, xbc_ref, dt_ref, dtt_ref):
    f32, bf16 = jnp.float32, jnp.bfloat16
    dm = x_ref.shape[1]
    mod = mod_ref[0]
    h = x_ref[...] * (1.0 + mod[:, dm:2 * dm]) + mod[:, 0:dm]
    proj = jnp.dot(h.astype(bf16), w_ref[...], preferred_element_type=f32)
    z_ref[...] = proj[:, :SSD_D_INNER]
    xbc_ref[...] = proj[:, SSD_D_INNER:SSD_D_INNER + SSD_CONV_DIM]
    dt_ref[...] = proj[:, SSD_D_INNER + SSD_CONV_DIM:]
    dtt_ref[...] = jnp.dot(wdt_ref[...], h.T.astype(bf16), preferred_element_type=f32)


def _softplus(v):
    return jnp.maximum(v, 0.0) + jnp.log(1.0 + jnp.exp(-jnp.abs(v)))


def _split3(v):
    f32, bf16 = jnp.float32, jnp.bfloat16
    hi = v.astype(bf16)
    r1 = v - hi.astype(f32)
    mid = r1.astype(bf16)
    lo = (r1 - mid.astype(f32)).astype(bf16)
    return jnp.concatenate([hi, mid, lo], axis=1)


def _ssd_direction(direction, xbc_ref, prev_ref, next_ref, dt_ref, dtt_ref, has_prev, has_next,
                   cw_ref, cb_ref, bias_r_ref, a_r_ref, bias_c_ref, a_c_ref, e_ref, dskip_ref, state_ref, y_ref):
    f32, bf16 = jnp.float32, jnp.bfloat16
    hp = lax.Precision.HIGHEST
    q = xbc_ref.shape[0]
    nh = SSD_HEADS
    xm = xbc_ref[...]
    row = lax.broadcasted_iota(jnp.int32, xm.shape, 0)
    before = jnp.where(has_prev, prev_ref[7:8, :], 0.0)
    after = jnp.where(has_next, next_ref[0:1, :], 0.0)
    x_dn = jnp.where(row == 0, before, pltpu.roll(xm, 1, axis=0))
    x_up = jnp.where(row == q - 1, after, pltpu.roll(xm, q - 1, axis=0))
    conv = cb_ref[...] + x_dn * cw_ref[0:1, :] + xm * cw_ref[1:2, :] + x_up * cw_ref[2:3, :]
    conv = conv * jax.nn.sigmoid(conv)
    xs = conv[:, :SSD_D_INNER]
    gn = SSD_GROUPS * SSD_STATE
    bm, cm = conv[:, SSD_D_INNER:SSD_D_INNER + gn], conv[:, SSD_D_INNER + gn:]
    hs = slice(direction * nh, (direction + 1) * nh)
    dt = _softplus(dt_ref[:, hs] + bias_r_ref[direction:direction + 1, :])
    dtt = _softplus(dtt_ref[hs, :] + bias_c_ref[direction])
    r_i = lax.broadcasted_iota(jnp.int32, (q, q), 0)
    c_i = lax.broadcasted_iota(jnp.int32, (q, q), 1)
    causal = (r_i >= c_i) if direction == 0 else (r_i <= c_i)
    tri = causal.astype(f32)
    tri_t = ((c_i >= r_i) if direction == 0 else (c_i <= r_i)).astype(f32)
    a_cum = jnp.dot(tri, dt * a_r_ref[direction:direction + 1, :], precision=hp, preferred_element_type=f32)
    a_cum_t = jnp.dot(dtt * a_c_ref[direction], tri_t, precision=hp, preferred_element_type=f32)
    last = q - 1 if direction == 0 else 0
    a_exp = jnp.dot(_split3(a_cum), e_ref[...], preferred_element_type=f32)
    dt_exp = jnp.dot(_split3(dt), e_ref[...], preferred_element_type=f32)
    xdt = xs * dt_exp
    a_tot = a_exp[last:last + 1, :]
    xw = (xdt * jnp.exp(a_tot - a_exp)).astype(bf16)
    grow = jnp.exp(a_exp)
    carry = jnp.exp(a_tot)
    xdt_bf = xdt.astype(bf16)
    for g in range(SSD_GROUPS):
        gl = slice(g * SSD_GW, (g + 1) * SSD_GW)
        nl = slice(g * SSD_STATE, (g + 1) * SSD_STATE)
        bm_g, cm_g = bm[:, nl], cm[:, nl].astype(bf16)
        prev = state_ref[direction, g]
        y_g = jnp.dot(cm_g, prev.astype(bf16), preferred_element_type=f32) * grow[:, gl]
        states = jnp.dot(bm_g.T.astype(bf16), xw[:, gl], preferred_element_type=f32)
        state_ref[direction, g] = prev * carry[:, gl] + states
        cb = lax.dot_general(cm_g, bm_g.astype(bf16), (((1,), (1,)), ((), ())), preferred_element_type=f32)
        parts = []
        for hh in range(SSD_HPG):
            hd = g * SSD_HPG + hh
            seg = jnp.broadcast_to(a_cum[:, hd:hd + 1], (q, q)) - a_cum_t[hd:hd + 1, :]
            lmat = (jnp.where(causal, jnp.exp(seg), 0.0) * cb).astype(bf16)
            parts.append(jnp.dot(lmat, xdt_bf[:, hd * SSD_HEADDIM:(hd + 1) * SSD_HEADDIM], preferred_element_type=f32))
        y_g = y_g + jnp.concatenate(parts, axis=1)
        if direction == 0:
            y_g = y_g + dskip_ref[:, gl] * xs[:, gl]
        y_ref[:, gl] = y_g


def _ssd_scan_kernel(xf_ref, pf_ref, nf_ref, dtf_ref, dttf_ref, xb_ref, pb_ref, nb_ref, dtb_ref, dttb_ref,
                     cw_ref, cb_ref, bias_r_ref, a_r_ref, bias_c_ref, a_c_ref, e_ref, dskip_ref,
                     yf_ref, yb_ref, state_ref, *, ctx_chunks, lat_chunks):
    j = pl.program_id(1)

    @pl.when(j == 0)
    def _reset():
        state_ref[...] = jnp.zeros(state_ref.shape, jnp.float32)

    in_ctx = j < ctx_chunks
    pos_f = jnp.where(in_ctx, j, j - ctx_chunks)
    seg_len = jnp.where(in_ctx, ctx_chunks, lat_chunks)
    pos_b = seg_len - 1 - pos_f
    shared = (cw_ref, cb_ref, bias_r_ref, a_r_ref, bias_c_ref, a_c_ref, e_ref, dskip_ref, state_ref)
    _ssd_direction(0, xf_ref, pf_ref, nf_ref, dtf_ref, dttf_ref, pos_f > 0, pos_f < seg_len - 1, *shared, yf_ref)
    _ssd_direction(1, xb_ref, pb_ref, nb_ref, dtb_ref, dttb_ref, pos_b > 0, pos_b < seg_len - 1, *shared, yb_ref)


def _ssd_out_kernel(yf_ref, yb_ref, z_ref, x_ref, mod_ref, nw_ref, wo_ref, lng_ref, lnb_ref, o_ref):
    f32 = jnp.float32
    dm = x_ref.shape[1]
    z = z_ref[...]
    y = (yf_ref[...] + yb_ref[...]) * (z * jax.nn.sigmoid(z))
    y = y * lax.rsqrt(jnp.mean(y * y, axis=-1, keepdims=True) + RMS_EPS) * nw_ref[...]
    out = jnp.dot(y.astype(jnp.bfloat16), wo_ref[...], preferred_element_type=f32)
    mod = mod_ref[0]
    v = DEEPNORM_ALPHA * x_ref[...] + mod[:, 2 * dm:3 * dm] * out
    mu = jnp.mean(v, axis=-1, keepdims=True)
    vc = v - mu
    var = jnp.mean(vc * vc, axis=-1, keepdims=True)
    o_ref[...] = vc * lax.rsqrt(var + LN_EPS) * lng_ref[...] + lnb_ref[...]


def ssd_ln_pallas(xa, mod3, n_lat, seq, bsz, w_in, conv_w, conv_b, dt_bias, a_log, d, norm_w, w_out, ln_g, ln_b):
    f32, bf16 = jnp.float32, jnp.bfloat16
    n_tok, dm = xa.shape
    ctx_len = (n_tok - n_lat) // bsz
    tm = TOK_TM
    q = SSD_CHUNK
    nh = SSD_HEADS
    pad = SSD_DT_PAD - 2 * nh
    w_ext = jnp.concatenate([w_in, jnp.zeros((dm, pad), w_in.dtype)], axis=1).astype(bf16)
    n_in = w_ext.shape[1]
    wdt_t = w_ext[:, SSD_D_INNER + SSD_CONV_DIM:].T
    mod_idx = functools.partial(_mod_row_index, tm=tm, n_lat=n_lat, seq=seq, bsz=bsz)
    whole = lambda shape: pl.BlockSpec(shape, lambda *_: (0,) * len(shape))
    params = pltpu.CompilerParams(dimension_semantics=("arbitrary",), vmem_limit_bytes=VMEM_LIMIT)
    z, xbc, dt, dtt = pl.pallas_call(
        _ssd_in_kernel,
        grid=(n_tok // tm,),
        in_specs=[
            pl.BlockSpec((tm, dm), lambda i: (i, 0)),
            pl.BlockSpec((1, 1, 6 * dm), lambda i: (mod_idx(i), 0, 0)),
            whole((dm, n_in)), whole((SSD_DT_PAD, dm)),
        ],
        out_specs=[
            pl.BlockSpec((tm, SSD_D_INNER), lambda i: (i, 0)),
            pl.BlockSpec((tm, SSD_CONV_DIM), lambda i: (i, 0)),
            pl.BlockSpec((tm, SSD_DT_PAD), lambda i: (i, 0)),
            pl.BlockSpec((SSD_DT_PAD, tm), lambda i: (0, i)),
        ],
        out_shape=[
            jax.ShapeDtypeStruct((n_tok, SSD_D_INNER), f32),
            jax.ShapeDtypeStruct((n_tok, SSD_CONV_DIM), f32),
            jax.ShapeDtypeStruct((n_tok, SSD_DT_PAD), f32),
            jax.ShapeDtypeStruct((SSD_DT_PAD, n_tok), f32),
        ],
        compiler_params=params,
        name="ssd_in_proj",
    )(xa, mod3, w_ext, wdt_t)

    ctx_chunks, lat_chunks = ctx_len // q, seq // q
    ctx_base = n_lat // q
    n_chunk_total = n_tok // q

    def chunk_of(b, j, backward):
        in_ctx = j < ctx_chunks
        pos = jnp.where(in_ctx, j, j - ctx_chunks)
        seg = jnp.where(in_ctx, ctx_chunks, lat_chunks)
        pos = jnp.where(backward, seg - 1 - pos, pos)
        return jnp.where(in_ctx, ctx_base + b * ctx_chunks + pos, b * lat_chunks + pos)

    sub = q // 8
    def dir_specs(backward):
        ch = lambda b, j: chunk_of(b, j, backward)
        return [
            pl.BlockSpec((q, SSD_CONV_DIM), lambda b, j: (ch(b, j), 0)),
            pl.BlockSpec((8, SSD_CONV_DIM), lambda b, j: (jnp.maximum(ch(b, j) * sub - 1, 0), 0)),
            pl.BlockSpec((8, SSD_CONV_DIM), lambda b, j: (jnp.minimum((ch(b, j) + 1) * sub, n_chunk_total * sub - 1), 0)),
            pl.BlockSpec((q, SSD_DT_PAD), lambda b, j: (ch(b, j), 0)),
            pl.BlockSpec((SSD_DT_PAD, q), lambda b, j: (0, ch(b, j))),
        ]

    a = -jnp.exp(a_log.astype(f32))
    expand = jnp.repeat(jnp.eye(nh, dtype=f32), SSD_HEADDIM, axis=1)
    e3 = jnp.concatenate([expand, expand, expand], axis=0).astype(bf16)
    dskip = jnp.repeat(d, SSD_HEADDIM)[None, :]
    yf, yb = pl.pallas_call(
        functools.partial(_ssd_scan_kernel, ctx_chunks=ctx_chunks, lat_chunks=lat_chunks),
        grid=(bsz, ctx_chunks + lat_chunks),
        in_specs=dir_specs(False) + dir_specs(True) + [
            whole((SSD_CONV, SSD_CONV_DIM)), whole((1, SSD_CONV_DIM)),
            whole((2, nh)), whole((2, nh)), whole((2, nh, 1)), whole((2, nh, 1)),
            whole((3 * nh, SSD_D_INNER)), whole((1, SSD_D_INNER)),
        ],
        out_specs=[
            pl.BlockSpec((q, SSD_D_INNER), lambda b, j: (chunk_of(b, j, False), 0)),
            pl.BlockSpec((q, SSD_D_INNER), lambda b, j: (chunk_of(b, j, True), 0)),
        ],
        out_shape=[jax.ShapeDtypeStruct((n_tok, SSD_D_INNER), f32)] * 2,
        scratch_shapes=[pltpu.VMEM((2, SSD_GROUPS, SSD_STATE, SSD_GW), f32)],
        compiler_params=pltpu.CompilerParams(dimension_semantics=("arbitrary", "arbitrary"), vmem_limit_bytes=VMEM_LIMIT),
        name="ssd_scan",
    )(xbc, xbc, xbc, dt, dtt, xbc, xbc, xbc, dt, dtt,
      conv_w, conv_b.reshape(1, -1), dt_bias, a, dt_bias.reshape(2, nh, 1), a.reshape(2, nh, 1), e3, dskip)

    return pl.pallas_call(
        _ssd_out_kernel,
        grid=(n_tok // tm,),
        in_specs=[
            pl.BlockSpec((tm, SSD_D_INNER), lambda i: (i, 0)),
            pl.BlockSpec((tm, SSD_D_INNER), lambda i: (i, 0)),
            pl.BlockSpec((tm, SSD_D_INNER), lambda i: (i, 0)),
            pl.BlockSpec((tm, dm), lambda i: (i, 0)),
            pl.BlockSpec((1, 1, 6 * dm), lambda i: (mod_idx(i), 0, 0)),
            whole((1, SSD_D_INNER)), whole((SSD_D_INNER, dm)), whole((1, dm)), whole((1, dm)),
        ],
        out_specs=pl.BlockSpec((tm, dm), lambda i: (i, 0)),
        out_shape=jax.ShapeDtypeStruct((n_tok, dm), f32),
        compiler_params=params,
        name="ssd_out_ln",
    )(yf, yb, z, xa, mod3, norm_w.reshape(1, -1), w_out.astype(bf16), ln_g.reshape(1, dm), ln_b.reshape(1, dm))


def kernel(x, c, ctx, c_ctx, mod_w, mod_b, ln_g, ln_b,
           s5_a_re, s5_a_im, s5_log_dt, s5_b_re, s5_b_im, s5_c_re, s5_c_im, s5_d, s5_w_gate, s5_w_val,
           ssd_w_in, ssd_conv_w, ssd_conv_b, ssd_dt_bias, ssd_a_log, ssd_d, ssd_norm_w, ssd_w_out,
           mla_w_down, mla_q_norm, mla_kv_norm, mla_w_uq, mla_w_uk, mla_w_uv, mla_w_o,
           peer_w_q, peer_subkeys, peer_u, peer_v):
    ROWS = x.shape[1] // GRID_W
    pos = grid_positions(ROWS)
    ctx_len = ctx.shape[1]
    c_act = jax.nn.silu(c)
    c_ctx_act = jax.nn.silu(c_ctx)
    bsz, seq_len, dm = x.shape
    n_lat, n_ctx = bsz * seq_len, bsz * ctx_len
    xa = jnp.concatenate([x.reshape(n_lat, dm), ctx.reshape(n_ctx, dm)], axis=0)
    for i in range(DEPTH):
        last = i == DEPTH - 1
        mod_tab = jnp.concatenate([c_act, c_ctx_act[None], jnp.zeros((7 - bsz, dm), x.dtype)], axis=0) @ mod_w[i] + mod_b[i]
        mod3 = mod_tab.reshape(8, 1, 6 * dm)
        kind, j = i % N_MIXERS, i // N_MIXERS
        if kind == 0:
            prep = _s5_prep(s5_a_re[j], s5_a_im[j], s5_log_dt[j], s5_b_re[j], s5_b_im[j], s5_c_re[j], s5_c_im[j], s5_d[j])
            ya = s5_ssm_pallas(xa, mod_tab, n_lat, seq_len, bsz, prep)
            if last:
                ya, xa = ya[:n_lat], xa[:n_lat]
            xa = glu_ln_pallas(ya, xa, mod3, n_lat, seq_len, bsz, s5_w_val[j], s5_w_gate[j], ln_g[i, 0], ln_b[i, 0])
        elif kind == 1:
            xa = ssd_ln_pallas(xa, mod3, n_lat, seq_len, bsz, ssd_w_in[j], ssd_conv_w[j], ssd_conv_b[j], ssd_dt_bias[j],
                               ssd_a_log[j], ssd_d[j], ssd_norm_w[j], ssd_w_out[j], ln_g[i, 0], ln_b[i, 0])
            if last:
                xa = xa[:n_lat]
        else:
            xa = mla_ln_pallas(xa, mod3, n_lat, seq_len, bsz, mla_w_down[j], mla_q_norm[j], mla_kv_norm[j],
                               mla_w_uq[j], mla_w_uk[j], mla_w_uv[j], mla_w_o[j], ln_g[i, 0], ln_b[i, 0])
            if last:
                xa = xa[:n_lat]
        xa = peer_ln_pallas(xa, mod3, n_lat, seq_len, bsz, peer_w_q[i], peer_subkeys[i], peer_u[i], peer_v[i],
                            ln_g[i, 1], ln_b[i, 1])
    return xa[:n_lat].reshape(bsz, seq_len, dm)
```

```python
import math
import functools
import jax
import jax.numpy as jnp
from jax import lax
import numpy as np
from jax.experimental import pallas as pl
from jax.experimental.pallas import tpu as pltpu

D_MODEL = 1024
BATCH = 4
SEQ = 4096
DEPTH = 4

GRID_W = 64
CTX_LEN = 256
N_MIXERS = 3
DEEPNORM_ALPHA = (2.0 * DEPTH) ** 0.25
LN_EPS = 1e-5
RMS_EPS = 1e-6
ROPE_BASE = 10000.0

S5_GROUP = 16
S5_GROUPS = D_MODEL // S5_GROUP
S5_STATE = 64

SSD_D_INNER = 2 * D_MODEL
SSD_HEADDIM = 64
SSD_HEADS = SSD_D_INNER // SSD_HEADDIM
SSD_GROUPS = 4
SSD_STATE = 128
SSD_CONV = 3
SSD_CHUNK = 128
SSD_CONV_DIM = SSD_D_INNER + 2 * SSD_GROUPS * SSD_STATE
SSD_IN_DIM = SSD_D_INNER + SSD_CONV_DIM + 2 * SSD_HEADS

MLA_HEADS = 16
MLA_Q_RANK = 256
MLA_KV_RANK = 128
MLA_NOPE = 64
MLA_ROPE = 32
MLA_V = 64
MLA_BLOCK = 128

PEER_HEADS = 8
PEER_KEYS = 128
PEER_EXPERTS = PEER_KEYS * PEER_KEYS
PEER_QDIM = 256
PEER_TOPK = 16
PEER_BLOCK = 128


def layer_norm(x, g, b):
    xf = x.astype(jnp.float32)
    mu = jnp.mean(xf, axis=-1, keepdims=True)
    var = jnp.mean(jnp.square(xf - mu), axis=-1, keepdims=True)
    return ((xf - mu) * lax.rsqrt(var + LN_EPS)).astype(x.dtype) * g + b


def _ln_kernel(x_ref, g_ref, b_ref, o_ref):
    xf = x_ref[...]
    mu = jnp.mean(xf, axis=-1, keepdims=True)
    xc = xf - mu
    var = jnp.mean(xc * xc, axis=-1, keepdims=True)
    o_ref[...] = xc * lax.rsqrt(var + LN_EPS) * g_ref[...] + b_ref[...]


def layer_norm_pallas(x, g, b):
    shp = x.shape
    x2 = x.reshape(-1, shp[-1])
    n, d = x2.shape
    tb = 512
    out = pl.pallas_call(
        _ln_kernel,
        grid=(n // tb,),
        in_specs=[pl.BlockSpec((tb, d), lambda i: (i, 0)),
                  pl.BlockSpec((1, d), lambda i: (0, 0)),
                  pl.BlockSpec((1, d), lambda i: (0, 0))],
        out_specs=pl.BlockSpec((tb, d), lambda i: (i, 0)),
        out_shape=jax.ShapeDtypeStruct((n, d), jnp.float32),
        name="final_ln",
    )(x2, g.reshape(1, d), b.reshape(1, d))
    return out.reshape(shp)


def rms_norm(x, g):
    xf = x.astype(jnp.float32)
    return (xf * lax.rsqrt(jnp.mean(jnp.square(xf), axis=-1, keepdims=True) + RMS_EPS)).astype(x.dtype) * g


def modulate(x, shift, scale):
    return x * (1.0 + scale) + shift


def grid_positions(rows):
    row = jnp.repeat(jnp.arange(rows, dtype=jnp.float32), GRID_W)
    col = jnp.tile(jnp.arange(GRID_W, dtype=jnp.float32), rows)
    return row, col


def rope_axial(x, row, col):
    half = x.shape[-1] // 2
    quarter = half // 2
    freqs = ROPE_BASE ** (-jnp.arange(quarter, dtype=jnp.float32) / quarter)

    def rot(xp, pos):
        ang = pos[:, None] * freqs
        cos = jnp.cos(ang)[None, :, None, :].astype(x.dtype)
        sin = jnp.sin(ang)[None, :, None, :].astype(x.dtype)
        x1, x2 = xp[..., :quarter], xp[..., quarter:]
        return jnp.concatenate([x1 * cos - x2 * sin, x2 * cos + x1 * sin], axis=-1)

    return jnp.concatenate([rot(x[..., :half], row), rot(x[..., half:], col)], axis=-1)


def _lin_rec(left, right):
    a1, b1 = left
    a2, b2 = right
    return a1 * a2, a2 * b1 + b2


def s5_scan(u, a_bar, b_bar, c_mat, init, reverse):
    bu = jnp.einsum('lgc,gpc->lgp', u.astype(jnp.float32), b_bar)
    a = jnp.broadcast_to(a_bar, bu.shape)
    a_cum, s = lax.associative_scan(_lin_rec, (a, bu), reverse=reverse, axis=0)
    s = s + a_cum * init
    y = jnp.einsum('lgp,gcp->lgc', s, c_mat).real
    final = s[0] if reverse else s[-1]
    return y, final


def s5_direction(u_ctx, u_lat, a_re, a_im, log_dt, b_re, b_im, c_re, c_im, reverse):
    lam = lax.complex(a_re.astype(jnp.float32), a_im.astype(jnp.float32))
    a_bar = jnp.exp(lam * jnp.exp(log_dt.astype(jnp.float32))[:, None])
    b_mat = lax.complex(b_re.astype(jnp.float32), b_im.astype(jnp.float32))
    b_bar = ((a_bar - 1.0) / lam)[..., None] * b_mat
    c_mat = lax.complex(c_re.astype(jnp.float32), c_im.astype(jnp.float32))

    def per_sample(args):
        uc, ul = args
        yc, sc = s5_scan(uc, a_bar, b_bar, c_mat, jnp.zeros_like(a_bar), reverse)
        yl, _ = s5_scan(ul, a_bar, b_bar, c_mat, sc, reverse)
        return yc, yl

    return lax.map(per_sample, (u_ctx, u_lat))


def s5_mixer(h, hc, a_re, a_im, log_dt, b_re, b_im, c_re, c_im, d, w_gate, w_val):
    bsz, seq_len, _ = h.shape
    ctx_len = hc.shape[1]
    u = h.reshape(bsz, seq_len, S5_GROUPS, S5_GROUP)
    uc = hc.reshape(bsz, ctx_len, S5_GROUPS, S5_GROUP)
    y_l = d * h
    y_c = d * hc
    for direction in range(2):
        yc, yl = s5_direction(uc, u, a_re[direction], a_im[direction], log_dt[direction],
                              b_re[direction], b_im[direction], c_re[direction], c_im[direction],
                              direction == 1)
        y_l = y_l + yl.reshape(bsz, seq_len, D_MODEL).astype(h.dtype)
        y_c = y_c + yc.reshape(bsz, ctx_len, D_MODEL).astype(h.dtype)

    def glu(y):
        g = jax.nn.gelu(y)
        return (g @ w_val) * jax.nn.sigmoid(g @ w_gate)

    return glu(y_l), glu(y_c)


def depthwise_conv_centred(x, w, b):
    k_w = w.shape[0]
    pad = k_w // 2
    seq_len = x.shape[1]
    xp = jnp.pad(x, ((0, 0), (pad, pad), (0, 0)))
    out = b
    for k in range(k_w):
        out = out + xp[:, k:k + seq_len] * w[k]
    return out


def ssd_scan(x, dt, a, bm, cm, init):
    bsz, seq_len, n_heads, p_dim = x.shape
    n_grp, n_st = bm.shape[2], bm.shape[3]
    hg = n_heads // n_grp
    q_len = SSD_CHUNK
    nc = seq_len // q_len
    xc = x.reshape(bsz, nc, q_len, n_grp, hg, p_dim)
    dtc = dt.reshape(bsz, nc, q_len, n_grp, hg).astype(jnp.float32)
    bc = bm.reshape(bsz, nc, q_len, n_grp, n_st)
    cc = cm.reshape(bsz, nc, q_len, n_grp, n_st)
    a_cum = jnp.cumsum(dtc * a.reshape(n_grp, hg), axis=2)
    xdt = xc * dtc[..., None].astype(x.dtype)
    tri = jnp.tril(jnp.ones((q_len, q_len), dtype=bool))
    seg = a_cum[:, :, :, None] - a_cum[:, :, None]
    decay = jnp.exp(jnp.where(tri[:, :, None, None], seg, -jnp.inf)).astype(x.dtype)
    cb = jnp.einsum('bcqgn,bcsgn->bcgqs', cc, bc)
    y_diag = jnp.einsum('bcgqs,bcqsgh,bcsghp->bcqghp', cb, decay, xdt)
    decay_states = jnp.exp(a_cum[:, :, -1:] - a_cum).astype(x.dtype)
    states = jnp.einsum('bcsgn,bcsgh,bcsghp->bcghpn', bc, decay_states, xdt)
    chunk_decay = jnp.exp(a_cum[:, :, -1]).astype(x.dtype)

    def step(carry, inp):
        dec, st = inp
        return carry * dec[..., None, None] + st, carry

    final, prev = lax.scan(step, init, (jnp.moveaxis(chunk_decay, 1, 0), jnp.moveaxis(states, 1, 0)))
    y_off = jnp.einsum('bcqgn,cbghpn,bcqgh->bcqghp', cc, prev, jnp.exp(a_cum).astype(x.dtype))
    return (y_diag + y_off).reshape(bsz, seq_len, n_heads, p_dim), final


def ssd_mixer(h, hc, w_in, conv_w, conv_b, dt_bias, a_log, d, norm_w, w_out):
    a = -jnp.exp(a_log.astype(jnp.float32))

    def flip(t):
        return jnp.flip(t, axis=1)

    def bidir(t, inits):
        bsz, seq_len = t.shape[0], t.shape[1]
        z, xbc, dt = jnp.split(t @ w_in, [SSD_D_INNER, SSD_D_INNER + SSD_CONV_DIM], axis=-1)
        xbc = jax.nn.silu(depthwise_conv_centred(xbc, conv_w, conv_b))
        xs, bm, cm = jnp.split(xbc, [SSD_D_INNER, SSD_D_INNER + SSD_GROUPS * SSD_STATE], axis=-1)
        xs = xs.reshape(bsz, seq_len, SSD_HEADS, SSD_HEADDIM)
        bm = bm.reshape(bsz, seq_len, SSD_GROUPS, SSD_STATE)
        cm = cm.reshape(bsz, seq_len, SSD_GROUPS, SSD_STATE)
        dt = jax.nn.softplus(dt.reshape(bsz, seq_len, 2, SSD_HEADS) + dt_bias)
        y_f, s_f = ssd_scan(xs, dt[:, :, 0], a[0], bm, cm, inits[0])
        y_b, s_b = ssd_scan(flip(xs), flip(dt[:, :, 1]), a[1], flip(bm), flip(cm), inits[1])
        y = y_f + flip(y_b) + d[:, None] * xs
        y = rms_norm(y.reshape(z.shape) * jax.nn.silu(z), norm_w)
        return y @ w_out, (s_f, s_b)

    zero = jnp.zeros((h.shape[0], SSD_GROUPS, SSD_HEADS // SSD_GROUPS, SSD_HEADDIM, SSD_STATE), h.dtype)
    out_c, states_c = bidir(hc, (zero, zero))
    out_l, _ = bidir(h, states_c)
    return out_l, out_c


def mla_project(t, w_down, q_norm, kv_norm, w_uq, w_uk, w_uv, pos):
    cq, ckv, kr = jnp.split(t @ w_down, [MLA_Q_RANK, MLA_Q_RANK + MLA_KV_RANK], axis=-1)
    q = jnp.einsum('btr,rhd->bthd', rms_norm(cq, q_norm), w_uq)
    ckv = rms_norm(ckv, kv_norm)
    k_nope = jnp.einsum('btr,rhd->bthd', ckv, w_uk)
    v = jnp.einsum('btr,rhd->bthd', ckv, w_uv)
    q_nope, q_rope = q[..., :MLA_NOPE], q[..., MLA_NOPE:]
    kr = kr[:, :, None, :]
    if pos is not None:
        q_rope = rope_axial(q_rope, pos[0], pos[1])
        kr = rope_axial(kr, pos[0], pos[1])
    k = jnp.concatenate([k_nope, jnp.broadcast_to(kr, k_nope.shape[:3] + (MLA_ROPE,))], axis=-1)
    q = jnp.concatenate([q_nope, q_rope], axis=-1)
    return q, k, v


def attend(q, k, v):
    s = jnp.einsum('bqhd,bkhd->bhqk', q, k).astype(jnp.float32) * (MLA_NOPE + MLA_ROPE) ** -0.5
    p = jax.nn.softmax(s, axis=-1).astype(v.dtype)
    return jnp.einsum('bhqk,bkhd->bqhd', p, v)


def mla_mixer(h, hc, pos, w_down, q_norm, kv_norm, w_uq, w_uk, w_uv, w_o):
    bsz, seq_len, _ = h.shape
    qc, kc, vc = mla_project(hc, w_down, q_norm, kv_norm, w_uq, w_uk, w_uv, None)
    ql, kl, vl = mla_project(h, w_down, q_norm, kv_norm, w_uq, w_uk, w_uv, pos)
    out_c = attend(qc, kc, vc).reshape(bsz, hc.shape[1], MLA_HEADS * MLA_V)
    k_all = jnp.concatenate([kc, kl], axis=1)
    v_all = jnp.concatenate([vc, vl], axis=1)
    nb = seq_len // MLA_BLOCK
    qb = jnp.moveaxis(ql.reshape(bsz, nb, MLA_BLOCK, MLA_HEADS, MLA_NOPE + MLA_ROPE), 1, 0)
    out_l = lax.map(lambda qblk: attend(qblk, k_all, v_all), qb)
    out_l = jnp.moveaxis(out_l, 0, 1).reshape(bsz, seq_len, MLA_HEADS * MLA_V)
    return out_l @ w_o, out_c @ w_o


def peer_ffn(h, w_q, subkeys, u_tab, v_tab):
    bsz, seq_len, dm = h.shape
    half = PEER_QDIM // 2
    q = jnp.einsum('btd,dhk->bthk', h, w_q)
    s1 = jnp.einsum('bthk,hnk->bthn', q[..., :half], subkeys[:, 0])
    s2 = jnp.einsum('bthk,hnk->bthn', q[..., half:], subkeys[:, 1])
    v1, i1 = lax.top_k(s1, PEER_TOPK)
    v2, i2 = lax.top_k(s2, PEER_TOPK)
    n_cand = PEER_TOPK * PEER_TOPK
    cand = (v1[..., :, None] + v2[..., None, :]).reshape(bsz, seq_len, PEER_HEADS, n_cand)
    cidx = (i1[..., :, None] * PEER_KEYS + i2[..., None, :]).reshape(bsz, seq_len, PEER_HEADS, n_cand)
    best, sel = lax.top_k(cand, PEER_TOPK)
    eidx = jnp.take_along_axis(cidx, sel, axis=-1)
    gate = jax.nn.softmax(best.astype(jnp.float32), axis=-1).astype(h.dtype)
    n_blk = (bsz * seq_len) // PEER_BLOCK
    n_sel = PEER_HEADS * PEER_TOPK
    hb = h.reshape(n_blk, PEER_BLOCK, dm)
    ib = eidx.reshape(n_blk, PEER_BLOCK, n_sel)
    gb = gate.reshape(n_blk, PEER_BLOCK, n_sel)

    def block(args):
        hk, ik, gk = args
        u = jnp.take(u_tab, ik, axis=0)
        act = jax.nn.gelu(jnp.einsum('td,ted->te', hk, u))
        v = jnp.take(v_tab, ik, axis=0)
        return jnp.einsum('te,ted->td', gk * act, v)

    return lax.map(block, (hb, ib, gb)).reshape(bsz, seq_len, dm)


PEER_TM = 512
PEER_EB = 1024
PEER_ROWS = 16
PEER_ACT_ROWS = 256
VMEM_LIMIT = 56 * 1024 * 1024
NEG_INF = float("-inf")
GELU_C0 = math.sqrt(2.0 / math.pi)
GELU_C1 = 0.044715 * GELU_C0


def _top16_sorted(s):
    n = PEER_TOPK
    m = s.shape[0] // 8
    x = [s[8 * r:8 * r + 8, :] for r in range(m)]

    def exchange(i, l, descending):
        hi, lo = jnp.maximum(x[i], x[l]), jnp.minimum(x[i], x[l])
        x[i], x[l] = (hi, lo) if descending else (lo, hi)

    def merge_bitonic(size):
        j = size // 2
        while j >= 1:
            for i in range(size):
                if i ^ j > i:
                    exchange(i, i ^ j, True)
            j //= 2

    k = 2
    while k < m:
        j = k // 2
        while j >= 1:
            for i in range(m):
                if i ^ j > i:
                    exchange(i, i ^ j, (i & k) == 0)
            j //= 2
        k *= 2
    merge_bitonic(m)
    shifts = [4, 2, 1]
    if m < n:
        first = shifts.pop(0)
        x = x + [pltpu.roll(x[m - 1 - r], first, axis=0) for r in range(m)]
        merge_bitonic(n)
    for shift in shifts:
        other = [pltpu.roll(x[n - 1 - r], shift, axis=0) for r in range(n)]
        for r in range(n):
            x[r] = jnp.maximum(x[r], other[r])
        merge_bitonic(n)
    return [x[r][0:1, :] for r in range(n)]


def _dot3(a_hi, a_lo, b_hi, b_lo):
    f32 = jnp.float32
    return (jnp.dot(a_hi, b_hi, preferred_element_type=f32)
            + jnp.dot(a_hi, b_lo, preferred_element_type=f32)
            + jnp.dot(a_lo, b_hi, preferred_element_type=f32))


def _split_bf16(v):
    hi = v.astype(jnp.bfloat16)
    lo = (v - hi.astype(jnp.float32)).astype(jnp.bfloat16)
    return hi, lo


def _peer_kernel(x_ref, mod_ref, wq_ref, skh_ref, skl_ref, u_ref, vt_ref, vtl_ref, lng_ref, lnb_ref,
                 o_ref,
                 ht_ref, n1_ref, e1_ref, r2_ref, e2_ref, v1_ref, v2_ref, cand_ref,
                 act_ref, gact_ref, acc_ref):
    f32 = jnp.float32
    bf16 = jnp.bfloat16
    eb = pl.program_id(1)
    n_eb = pl.num_programs(1)
    dm = x_ref.shape[1]
    a_per = PEER_EB // PEER_KEYS
    half = PEER_QDIM // 2

    @pl.when(eb == 0)
    def _prologue():
        mod = mod_ref[0]
        h = x_ref[...] * (1.0 + mod[:, 4 * dm:5 * dm]) + mod[:, 3 * dm:4 * dm]
        ht = h.T.astype(bf16)
        ht_ref[...] = ht
        qt = jnp.dot(wq_ref[...], ht, preferred_element_type=f32)
        cand_ref[...] = jnp.full(cand_ref.shape, NEG_INF, f32)
        gact_ref[...] = jnp.zeros(gact_ref.shape, bf16)
        for hd in range(PEER_HEADS):
            q1_hi, q1_lo = _split_bf16(qt[hd * PEER_QDIM: hd * PEER_QDIM + half])
            q2_hi, q2_lo = _split_bf16(qt[hd * PEER_QDIM + half: (hd + 1) * PEER_QDIM])
            s1 = _dot3(skh_ref[hd, 0], skl_ref[hd, 0], q1_hi, q1_lo)
            s2 = _dot3(skh_ref[hd, 1], skl_ref[hd, 1], q2_hi, q2_lo)
            top1 = _top16_sorted(s1)
            top2 = _top16_sorted(s2)
            for k in range(PEER_TOPK):
                v1_ref[k:k + 1, :] = top1[k]
                v2_ref[k:k + 1, :] = top2[k]
            off = 0
            for i in range(PEER_TOPK):
                cnt = PEER_TOPK // (i + 1)
                cand_ref[off:off + cnt, :] = v1_ref[i:i + 1, :] + v2_ref[0:cnt, :]
                off += cnt
            best = _top16_sorted(cand_ref[...])
            z = jnp.zeros_like(best[0])
            for k in range(PEER_TOPK):
                z = z + jnp.exp(best[k] - best[0])
            tau = best[PEER_TOPK - 1]
            n1 = jnp.zeros(s1.shape, f32)
            rank2 = jnp.zeros(s2.shape, f32)
            for k in range(PEER_TOPK):
                n1 = jnp.where(s1 + top2[k] >= tau, k + 1.0, n1)
                rank2 = jnp.where(top2[k] > s2, k + 1.0, rank2)
            n1_ref[hd] = n1
            r2_ref[hd] = rank2.astype(bf16)
            e1_ref[hd] = jnp.exp(s1 - top1[0])
            e2_ref[hd] = (jnp.exp(s2 - top2[0]) * (0.5 / z)).astype(bf16)
        acc_ref[...] = jnp.zeros(acc_ref.shape, f32)

    par = eb % 2
    for r0 in range(0, PEER_EB, PEER_ACT_ROWS):
        act_ref[r0:r0 + PEER_ACT_ROWS, :] = jnp.dot(u_ref[r0:r0 + PEER_ACT_ROWS, :], ht_ref[...],
                                                    preferred_element_type=f32)
    acc_ref[...] += jnp.dot(vt_ref[0], gact_ref[1 - par], preferred_element_type=f32)
    a_base = pl.multiple_of(eb * a_per, a_per)
    blk = (PEER_ROWS, x_ref.shape[0])
    n_slab = PEER_KEYS // PEER_ROWS
    for a in range(a_per):
        g = [None] * n_slab
        for hd in range(PEER_HEADS):
            n1a = jnp.broadcast_to(n1_ref[hd, pl.ds(a_base, a_per), :][a:a + 1, :].astype(bf16), blk)
            e1a = jnp.broadcast_to(e1_ref[hd, pl.ds(a_base, a_per), :][a:a + 1, :].astype(bf16), blk)
            for i in range(n_slab):
                rs = slice(i * PEER_ROWS, (i + 1) * PEER_ROWS)
                w = jnp.where(r2_ref[hd, rs, :] < n1a, e2_ref[hd, rs, :], jnp.zeros(blk, bf16)) * e1a
                g[i] = w if g[i] is None else g[i] + w
        for i in range(n_slab):
            rows = slice(a * PEER_KEYS + i * PEER_ROWS, a * PEER_KEYS + (i + 1) * PEER_ROWS)
            x = act_ref[rows, :]
            t = jnp.tanh(x * (GELU_C0 + GELU_C1 * (x * x)))
            gact_ref[par, rows, :] = g[i] * (x + x * t).astype(bf16)

    @pl.when(eb == n_eb - 1)
    def _epilogue():
        mod = mod_ref[0]
        ffn_t = acc_ref[...] + jnp.dot(vtl_ref[0], gact_ref[par], preferred_element_type=f32)
        y = DEEPNORM_ALPHA * x_ref[...] + mod[:, 5 * dm:6 * dm] * ffn_t.T
        mu = jnp.mean(y, axis=-1, keepdims=True)
        yc = y - mu
        var = jnp.mean(yc * yc, axis=-1, keepdims=True)
        o_ref[...] = yc * lax.rsqrt(var + LN_EPS) * lng_ref[...] + lnb_ref[...]


def _mod_row_index(i, tm, n_lat, seq, bsz):
    return jnp.where(i * tm < n_lat, (i * tm) // seq, bsz)


def peer_ln_pallas(xa, mod3, n_lat, seq, bsz, w_q, subkeys, u_tab, v_tab, ln_g, ln_b):
    n_tok, dm = xa.shape
    tm = PEER_TM
    f32, bf16 = jnp.float32, jnp.bfloat16
    wq_t = w_q.reshape(dm, PEER_HEADS * PEER_QDIM).T
    wq_bf = wq_t.astype(bf16)
    sk_hi = subkeys.astype(bf16)
    sk_lo = (subkeys - sk_hi.astype(f32)).astype(bf16)
    u_bf = u_tab.astype(bf16)
    n_eb = PEER_EXPERTS // PEER_EB
    vt_bf = v_tab.reshape(n_eb, PEER_EB, dm).transpose(0, 2, 1).astype(bf16)
    qd = PEER_HEADS * PEER_QDIM
    n_cand = sum(PEER_TOPK // (i + 1) for i in range(PEER_TOPK))
    n_cand_pad = 64
    assert n_cand <= n_cand_pad
    mod_idx = functools.partial(_mod_row_index, tm=tm, n_lat=n_lat, seq=seq, bsz=bsz)
    return pl.pallas_call(
        _peer_kernel,
        grid=(n_tok // tm, n_eb),
        in_specs=[
            pl.BlockSpec((tm, dm), lambda i, e: (i, 0)),
            pl.BlockSpec((1, 1, 6 * dm), lambda i, e: (mod_idx(i), 0, 0)),
            pl.BlockSpec((qd, dm), lambda i, e: (0, 0)),
            pl.BlockSpec((PEER_HEADS, 2, PEER_KEYS, PEER_QDIM // 2), lambda i, e: (0, 0, 0, 0)),
            pl.BlockSpec((PEER_HEADS, 2, PEER_KEYS, PEER_QDIM // 2), lambda i, e: (0, 0, 0, 0)),
            pl.BlockSpec((PEER_EB, dm), lambda i, e: (e, 0)),
            pl.BlockSpec((1, dm, PEER_EB), lambda i, e: (jnp.maximum(e - 1, 0), 0, 0)),
            pl.BlockSpec((1, dm, PEER_EB), lambda i, e: (n_eb - 1, 0, 0)),
            pl.BlockSpec((1, dm), lambda i, e: (0, 0)),
            pl.BlockSpec((1, dm), lambda i, e: (0, 0)),
        ],
        out_specs=pl.BlockSpec((tm, dm), lambda i, e: (i, 0)),
        out_shape=jax.ShapeDtypeStruct((n_tok, dm), f32),
        scratch_shapes=[
            pltpu.VMEM((dm, tm), bf16),
            pltpu.VMEM((PEER_HEADS, PEER_KEYS, tm), f32),
            pltpu.VMEM((PEER_HEADS, PEER_KEYS, tm), f32),
            pltpu.VMEM((PEER_HEADS, PEER_KEYS, tm), bf16),
            pltpu.VMEM((PEER_HEADS, PEER_KEYS, tm), bf16),
            pltpu.VMEM((PEER_TOPK, tm), f32),
            pltpu.VMEM((PEER_TOPK, tm), f32),
            pltpu.VMEM((n_cand_pad, tm), f32),
            pltpu.VMEM((PEER_EB, tm), f32),
            pltpu.VMEM((2, PEER_EB, tm), bf16),
            pltpu.VMEM((dm, tm), f32),
        ],
        compiler_params=pltpu.CompilerParams(
            dimension_semantics=("arbitrary", "arbitrary"),
            vmem_limit_bytes=VMEM_LIMIT),
        name="peer_ln",
    )(xa, mod3, wq_bf, sk_hi, sk_lo, u_bf, vt_bf, vt_bf, ln_g.reshape(1, dm), ln_b.reshape(1, dm))


S5_CHUNK = 16
S5_LEVELS = 9


def _s5_prep(a_re, a_im, log_dt, b_re, b_im, c_re, c_im, d):
    f32, bf16 = jnp.float32, jnp.bfloat16
    hp = lax.Precision.HIGHEST
    n_g, n_p, q = S5_GROUPS, S5_STATE, S5_CHUNK
    dt = jnp.exp(log_dt.astype(f32))[..., None]

    def apow(n):
        mag = jnp.exp(a_re * dt * n)
        ang = a_im * dt * n
        return mag * jnp.cos(ang), mag * jnp.sin(ang)

    ar1, ai1 = apow(1.0)
    den = a_re * a_re + a_im * a_im
    nr, ni = ar1 - 1.0, ai1
    cr = (nr * a_re + ni * a_im) / den
    ci = (ni * a_re - nr * a_im) / den
    bbr = cr[..., None] * b_re - ci[..., None] * b_im
    bbi = cr[..., None] * b_im + ci[..., None] * b_re
    lag = jnp.arange(q + 1, dtype=f32)[:, None, None, None]
    pr, pi = apow(lag)
    mr = pr[..., None] * bbr - pi[..., None] * bbi
    mi = pr[..., None] * bbi + pi[..., None] * bbr
    kmat = (jnp.einsum('dgcp,ndgpk->ndgck', c_re, mr, precision=hp)
            - jnp.einsum('dgcp,ndgpk->ndgck', c_im, mi, precision=hp))
    r_idx = jnp.arange(q)[:, None]
    t_idx = jnp.arange(q)[None, :]

    def toeplitz(kd, lagm):
        blk = kd[jnp.clip(lagm, 0, q)]
        blk = jnp.where((lagm >= 0)[:, :, None, None, None], blk, 0.0)
        return blk.transpose(2, 0, 4, 1, 3).reshape(n_g, q * S5_GROUP, q * S5_GROUP)

    eye = jnp.eye(q * S5_GROUP, dtype=f32)
    dvec = jnp.tile(d.reshape(n_g, 1, S5_GROUP), (1, q, 1)).reshape(n_g, q * S5_GROUP)
    tsum = toeplitz(kmat[:, 0], t_idx - r_idx) + toeplitz(kmat[:, 1], r_idx - t_idx) + eye[None] * dvec[:, None, :]

    def w_in(direction, exps):
        wr = mr[exps, direction].transpose(1, 0, 3, 2).reshape(n_g, q * S5_GROUP, n_p)
        wi = mi[exps, direction].transpose(1, 0, 3, 2).reshape(n_g, q * S5_GROUP, n_p)
        return wr, wi

    def w_out(direction, exps):
        pre, pim = pr[exps, direction], pi[exps, direction]
        cre, cim = c_re[direction], c_im[direction]
        wre = cre[None] * pre[:, :, None, :] - cim[None] * pim[:, :, None, :]
        wim = -(cre[None] * pim[:, :, None, :] + cim[None] * pre[:, :, None, :])
        return (wre.transpose(1, 3, 0, 2).reshape(n_g, n_p, q * S5_GROUP),
                wim.transpose(1, 3, 0, 2).reshape(n_g, n_p, q * S5_GROUP))

    steps = jnp.arange(q)
    win = [w_in(0, q - 1 - steps), w_in(1, steps)]
    wout = [w_out(0, steps + 1), w_out(1, q - steps)]
    zc = jnp.zeros((n_g, q * S5_GROUP, n_p), f32)
    zr = jnp.zeros((n_g, n_p, q * S5_GROUP), f32)
    win_p, wout_p = [], []
    for direction in range(2):
        wr, wi = win[direction]
        even = jnp.concatenate([wr, zc, wi, zc], axis=2)
        odd = jnp.concatenate([zc, wr, zc, wi], axis=2)
        is_odd = (jnp.arange(n_g) % 2 == 1)[:, None, None]
        win_p.append(jnp.where(is_odd, odd, even))
        vr, vi = wout[direction]
        even = jnp.concatenate([vr, zr, vi, zr], axis=1)
        odd = jnp.concatenate([zr, vr, zr, vi], axis=1)
        wout_p.append(jnp.where(is_odd, odd, even))
    win_p = jnp.stack(win_p, axis=1).reshape(n_g // 2, 2, 2, 4 * n_p, 4 * n_p).transpose(0, 2, 1, 3, 4)
    wout_p = jnp.stack(wout_p, axis=1).reshape(n_g // 2, 2, 2, 4 * n_p, 4 * n_p).transpose(0, 2, 1, 3, 4)
    lvl = (q * 2.0 ** jnp.arange(S5_LEVELS, dtype=f32))[:, None, None, None]
    lr, li = apow(lvl)
    pw = jnp.stack([lr, li], axis=1)
    pw = pw.transpose(3, 2, 0, 1, 4).reshape(n_g // 2, 2, 2, S5_LEVELS, 2, n_p)
    pw = pw.transpose(0, 2, 3, 4, 1, 5).reshape(n_g // 2, 2, 2 * S5_LEVELS, 2 * n_p)
    return tsum.astype(bf16).reshape(n_g // 2, 2, q * S5_GROUP, q * S5_GROUP), win_p.astype(bf16), wout_p.astype(bf16), pw


def _shift_rows(v, r, up):
    n = v.shape[0]
    row = lax.broadcasted_iota(jnp.int32, v.shape, 0)
    if up:
        return jnp.where(row < n - r, pltpu.roll(v, n - r, axis=0), 0.0)
    return jnp.where(row >= r, pltpu.roll(v, r, axis=0), 0.0)


def _chunk_scan(re, im, pw, rows_per_chunk, up):
    n_chunks = re.shape[0] // rows_per_chunk
    level, s = 0, 1
    while s < n_chunks:
        ar, ai = pw[2 * level:2 * level + 1, :], pw[2 * level + 1:2 * level + 2, :]
        sre = _shift_rows(re, s * rows_per_chunk, up)
        sim = _shift_rows(im, s * rows_per_chunk, up)
        re, im = re + ar * sre - ai * sim, im + ar * sim + ai * sre
        level, s = level + 1, 2 * s
    return re, im


def _s5_kernel(u_ref, scl_ref, shl_ref, scc_ref, shc_ref, t_ref, win_ref, wout_ref, pw_ref, y_ref, *, bsz, ctx_chunks):
    f32, bf16 = jnp.float32, jnp.bfloat16
    n = u_ref.shape[1]
    width = u_ref.shape[2]
    rc = ctx_chunks * bsz
    hs = []
    for gi in range(2):
        xv = u_ref[gi].reshape(n // 8, 8, width)
        hl = xv * scl_ref[gi][None] + shl_ref[gi][None]
        hc = xv * scc_ref[gi][None] + shc_ref[gi][None]
        slab = lax.broadcasted_iota(jnp.int32, xv.shape, 0)
        hs.append(jnp.where(slab < rc // 8, hc, hl).reshape(n, width).astype(bf16))
    y = [jnp.dot(hs[gi], t_ref[0, gi], preferred_element_type=f32) for gi in range(2)]
    half = width // 2
    for direction in range(2):
        sloc = (jnp.dot(hs[0], win_ref[0, direction, 0], preferred_element_type=f32)
                + jnp.dot(hs[1], win_ref[0, direction, 1], preferred_element_type=f32))
        re, im = sloc[:, :half], sloc[:, half:]
        pw = pw_ref[0, direction]
        if direction == 0:
            sre, sim = _chunk_scan(_shift_rows(re, bsz, False), _shift_rows(im, bsz, False), pw, bsz, False)
        else:
            cre, cim = _chunk_scan(_shift_rows(re[:rc], bsz, True), _shift_rows(im[:rc], bsz, True), pw, bsz, True)
            ar, ai = pw[0:1, :], pw[1:2, :]
            fre = ar * cre[0:8] - ai * cim[0:8] + re[0:8]
            fim = ar * cim[0:8] + ai * cre[0:8] + im[0:8]
            row8 = lax.broadcasted_iota(jnp.int32, fre.shape, 0)
            tre = jnp.where(row8 >= 8 - bsz, pltpu.roll(fre, 8 - bsz, axis=0), 0.0)
            tim = jnp.where(row8 >= 8 - bsz, pltpu.roll(fim, 8 - bsz, axis=0), 0.0)
            lre, lim = _shift_rows(re[rc:], bsz, True), _shift_rows(im[rc:], bsz, True)
            lre = jnp.concatenate([lre[:-8], lre[-8:] + tre], axis=0)
            lim = jnp.concatenate([lim[:-8], lim[-8:] + tim], axis=0)
            lre, lim = _chunk_scan(lre, lim, pw, bsz, True)
            sre = jnp.concatenate([cre, lre], axis=0)
            sim = jnp.concatenate([cim, lim], axis=0)
        s_in = jnp.concatenate([sre, sim], axis=1).astype(bf16)
        for gi in range(2):
            y[gi] = y[gi] + jnp.dot(s_in, wout_ref[0, direction, gi], preferred_element_type=f32)
    for gi in range(2):
        y_ref[gi] = y[gi]


def s5_ssm_pallas(xa, mod_tab, n_lat, seq, bsz, prep):
    tsum, win_p, wout_p, pw = prep
    f32 = jnp.float32
    n_tok, dm = xa.shape
    ctx_len = (n_tok - n_lat) // bsz
    q = S5_CHUNK
    n_chunks = (seq + ctx_len) // q
    width = q * S5_GROUP
    full = jnp.concatenate([xa[n_lat:].reshape(bsz, ctx_len, dm), xa[:n_lat].reshape(bsz, seq, dm)], axis=1)
    u = full.reshape(bsz, n_chunks, q, S5_GROUPS, S5_GROUP).transpose(3, 1, 0, 2, 4).reshape(S5_GROUPS, n_chunks * bsz, width)

    def tile(vec_rows):
        rows = vec_rows.shape[0]
        t = vec_rows.reshape(rows, S5_GROUPS, 1, S5_GROUP)
        t = jnp.broadcast_to(t, (rows, S5_GROUPS, q, S5_GROUP)).reshape(rows, S5_GROUPS, width)
        return jnp.tile(t.transpose(1, 0, 2), (1, 8 // rows, 1))

    scl, shl = tile(1.0 + mod_tab[:bsz, dm:2 * dm]), tile(mod_tab[:bsz, 0:dm])
    scc = tile(jnp.broadcast_to(1.0 + mod_tab[bsz:bsz + 1, dm:2 * dm], (bsz, dm)))
    shc = tile(jnp.broadcast_to(mod_tab[bsz:bsz + 1, 0:dm], (bsz, dm)))
    n_rows = n_chunks * bsz
    tile_spec = pl.BlockSpec((2, 8, width), lambda p: (p, 0, 0))
    y = pl.pallas_call(
        functools.partial(_s5_kernel, bsz=bsz, ctx_chunks=ctx_len // q),
        grid=(S5_GROUPS // 2,),
        in_specs=[
            pl.BlockSpec((2, n_rows, width), lambda p: (p, 0, 0)),
            tile_spec, tile_spec, tile_spec, tile_spec,
            pl.BlockSpec((1, 2, width, width), lambda p: (p, 0, 0, 0)),
            pl.BlockSpec((1, 2, 2, width, width), lambda p: (p, 0, 0, 0, 0)),
            pl.BlockSpec((1, 2, 2, width, width), lambda p: (p, 0, 0, 0, 0)),
            pl.BlockSpec((1, 2, 2 * S5_LEVELS, 2 * S5_STATE), lambda p: (p, 0, 0, 0)),
        ],
        out_specs=pl.BlockSpec((2, n_rows, width), lambda p: (p, 0, 0)),
        out_shape=jax.ShapeDtypeStruct((S5_GROUPS, n_rows, width), f32),
        compiler_params=pltpu.CompilerParams(dimension_semantics=("arbitrary",), vmem_limit_bytes=VMEM_LIMIT),
        name="s5_ssm",
    )(u, scl, shl, scc, shc, tsum, win_p, wout_p, pw)
    yf = y.reshape(S5_GROUPS, n_chunks, bsz, q, S5_GROUP).transpose(2, 1, 3, 0, 4).reshape(bsz, seq + ctx_len, dm)
    return jnp.concatenate([yf[:, ctx_len:].reshape(n_lat, dm), yf[:, :ctx_len].reshape(bsz * ctx_len, dm)], axis=0)


def _glu_ln_kernel(y_ref, x_ref, mod_ref, wv_ref, wg_ref, lng_ref, lnb_ref, o_ref):
    f32 = jnp.float32
    dm = x_ref.shape[1]
    g = jax.nn.gelu(y_ref[...]).astype(jnp.bfloat16)
    val = jnp.dot(g, wv_ref[...], preferred_element_type=f32)
    gate = jnp.dot(g, wg_ref[...], preferred_element_type=f32)
    out = val * jax.nn.sigmoid(gate)
    mod = mod_ref[0]
    z = DEEPNORM_ALPHA * x_ref[...] + mod[:, 2 * dm:3 * dm] * out
    mu = jnp.mean(z, axis=-1, keepdims=True)
    zc = z - mu
    var = jnp.mean(zc * zc, axis=-1, keepdims=True)
    o_ref[...] = zc * lax.rsqrt(var + LN_EPS) * lng_ref[...] + lnb_ref[...]


TOK_TM = 512


def glu_ln_pallas(ya, xa, mod3, n_lat, seq, bsz, w_val, w_gate, ln_g, ln_b):
    n_tok, dm = ya.shape
    tm = TOK_TM
    bf16 = jnp.bfloat16
    mod_idx = functools.partial(_mod_row_index, tm=tm, n_lat=n_lat, seq=seq, bsz=bsz)
    return pl.pallas_call(
        _glu_ln_kernel,
        grid=(n_tok // tm,),
        in_specs=[
            pl.BlockSpec((tm, dm), lambda i: (i, 0)),
            pl.BlockSpec((tm, dm), lambda i: (i, 0)),
            pl.BlockSpec((1, 1, 6 * dm), lambda i: (mod_idx(i), 0, 0)),
            pl.BlockSpec((dm, dm), lambda i: (0, 0)),
            pl.BlockSpec((dm, dm), lambda i: (0, 0)),
            pl.BlockSpec((1, dm), lambda i: (0, 0)),
            pl.BlockSpec((1, dm), lambda i: (0, 0)),
        ],
        out_specs=pl.BlockSpec((tm, dm), lambda i: (i, 0)),
        out_shape=jax.ShapeDtypeStruct((n_tok, dm), jnp.float32),
        compiler_params=pltpu.CompilerParams(dimension_semantics=("arbitrary",), vmem_limit_bytes=VMEM_LIMIT),
        name="glu_ln",
    )(ya, xa, mod3, w_val.astype(bf16), w_gate.astype(bf16), ln_g.reshape(1, dm), ln_b.reshape(1, dm))


MLA_HD = 128
MLA_TQ = 256
MLA_HB = 4
MLA_KB = 512


def _mla_prep(w_down, w_uq, w_uk, w_uv):
    bf16 = jnp.bfloat16
    quarter = MLA_ROPE // 4
    swap = np.concatenate([np.arange(quarter, 2 * quarter), np.arange(0, quarter),
                           np.arange(3 * quarter, 4 * quarter), np.arange(2 * quarter, 3 * quarter)])
    dm = w_down.shape[0]
    w_cq = w_down[:, :MLA_Q_RANK]
    w_ckv = w_down[:, MLA_Q_RANK:MLA_Q_RANK + MLA_KV_RANK]
    w_kr = w_down[:, MLA_Q_RANK + MLA_KV_RANK:]
    wd_t = jnp.concatenate([w_cq, w_ckv, jnp.zeros((dm, MLA_NOPE), w_down.dtype), w_kr, w_kr[:, swap]], axis=1).T
    rope = w_uq[:, :, MLA_NOPE:]
    wuq_t = jnp.concatenate([w_uq, rope[:, :, swap]], axis=2).reshape(MLA_Q_RANK, MLA_HEADS * MLA_HD).T
    wuk = jnp.concatenate([w_uk, jnp.zeros((MLA_KV_RANK, MLA_HEADS, MLA_HD - MLA_NOPE), w_uk.dtype)], axis=2)
    wuk = wuk.reshape(MLA_KV_RANK, MLA_HEADS * MLA_HD)
    wuv_t = w_uv.reshape(MLA_KV_RANK, MLA_HEADS * MLA_V).T
    return wd_t.astype(bf16), wuq_t.astype(bf16), wuk.astype(bf16), wuv_t.astype(bf16)


def _rope_tables(seq, tm):
    quarter = MLA_ROPE // 4
    freqs = ROPE_BASE ** (-jnp.arange(quarter, dtype=jnp.float32) / quarter)
    t = jnp.arange(seq, dtype=jnp.float32)
    row, col = jnp.floor(t / GRID_W), t - GRID_W * jnp.floor(t / GRID_W)
    ang_r, ang_c = freqs[:, None] * row[None, :], freqs[:, None] * col[None, :]
    cos32 = jnp.concatenate([jnp.cos(ang_r), jnp.cos(ang_r), jnp.cos(ang_c), jnp.cos(ang_c)], axis=0)
    sin32 = jnp.concatenate([-jnp.sin(ang_r), jnp.sin(ang_r), -jnp.sin(ang_c), jnp.sin(ang_c)], axis=0)
    ones = jnp.ones((MLA_NOPE, seq + tm), jnp.float32)
    zeros = jnp.zeros((MLA_HD - MLA_NOPE - MLA_ROPE, seq + tm), jnp.float32)
    cos_t = jnp.concatenate([ones, jnp.concatenate([cos32, jnp.ones((MLA_ROPE, tm))], axis=1), zeros], axis=0)
    sin_t = jnp.concatenate([0.0 * ones, jnp.concatenate([sin32, jnp.zeros((MLA_ROPE, tm))], axis=1), zeros], axis=0)
    return cos_t, sin_t


def _mla_proj_kernel(x_ref, mod_ref, cos_ref, sin_ref, wd_ref, wuq_ref, wuk_ref, wuv_ref, qn_ref, kvn_ref,
                     qt_ref, k_ref, vt_ref):
    f32, bf16 = jnp.float32, jnp.bfloat16
    dm = x_ref.shape[1]
    mod = mod_ref[0]
    h = x_ref[...] * (1.0 + mod[:, dm:2 * dm]) + mod[:, 0:dm]
    ht = h.T.astype(bf16)
    dt = jnp.dot(wd_ref[...], ht, preferred_element_type=f32)
    cq, ckv, kr = dt[:MLA_Q_RANK], dt[MLA_Q_RANK:MLA_Q_RANK + MLA_KV_RANK], dt[MLA_Q_RANK + MLA_KV_RANK:]
    cqn = cq * lax.rsqrt(jnp.mean(cq * cq, axis=0, keepdims=True) + RMS_EPS) * qn_ref[...]
    ckvn = ckv * lax.rsqrt(jnp.mean(ckv * ckv, axis=0, keepdims=True) + RMS_EPS) * kvn_ref[...]
    cos_t, sin_t = cos_ref[...], sin_ref[...]

    def rope(v):
        shifted = jnp.concatenate([v[MLA_ROPE:], v[:MLA_ROPE]], axis=0)
        return v * cos_t + shifted * sin_t

    scale = (MLA_NOPE + MLA_ROPE) ** -0.5
    q_all = jnp.dot(wuq_ref[...], cqn.astype(bf16), preferred_element_type=f32)
    for hd in range(MLA_HEADS):
        qt_ref[hd] = (rope(q_all[hd * MLA_HD:(hd + 1) * MLA_HD]) * scale).astype(bf16)
    ckvn_bf = ckvn.astype(bf16)
    vt_ref[...] = jnp.dot(wuv_ref[...], ckvn_bf, preferred_element_type=f32).astype(bf16)
    k_all = jnp.dot(ckvn.T.astype(bf16), wuk_ref[...], preferred_element_type=f32)
    kr_rows = rope(kr).T
    for hd in range(MLA_HEADS):
        k_ref[hd] = (k_all[:, hd * MLA_HD:(hd + 1) * MLA_HD] + kr_rows).astype(bf16)


def _mla_attn_kernel(*refs, with_latent):
    f32, bf16 = jnp.float32, jnp.bfloat16
    if with_latent:
        q_ref, kl_ref, kc_ref, vl_ref, vc_ref, o_ref = refs
    else:
        q_ref, kc_ref, vc_ref, o_ref = refs
    for i in range(q_ref.shape[0]):
        rows = slice(i * MLA_V, (i + 1) * MLA_V)
        q = q_ref[i]
        s_c = jnp.dot(kc_ref[i], q, preferred_element_type=f32)
        m = jnp.max(s_c, axis=0, keepdims=True)
        p_c = jnp.exp(s_c - m)
        den = jnp.sum(p_c, axis=0, keepdims=True)
        o = jnp.dot(vc_ref[rows, :], p_c.astype(bf16), preferred_element_type=f32)
        if with_latent:
            for k0 in range(0, kl_ref.shape[1], MLA_KB):
                s = jnp.dot(kl_ref[i, k0:k0 + MLA_KB, :], q, preferred_element_type=f32)
                m_new = jnp.maximum(m, jnp.max(s, axis=0, keepdims=True))
                alpha = jnp.exp(m - m_new)
                p = jnp.exp(s - m_new)
                den = den * alpha + jnp.sum(p, axis=0, keepdims=True)
                o = o * alpha + jnp.dot(vl_ref[rows, k0:k0 + MLA_KB], p.astype(bf16), preferred_element_type=f32)
                m = m_new
        o_ref[rows, :] = (o / den).astype(bf16)


def _mla_out_kernel(ot_ref, x_ref, mod_ref, wo_ref, lng_ref, lnb_ref, o_ref):
    f32 = jnp.float32
    dm = x_ref.shape[1]
    attn = ot_ref[...].astype(f32).T.astype(jnp.bfloat16)
    out = jnp.dot(attn, wo_ref[...], preferred_element_type=f32)
    mod = mod_ref[0]
    z = DEEPNORM_ALPHA * x_ref[...] + mod[:, 2 * dm:3 * dm] * out
    mu = jnp.mean(z, axis=-1, keepdims=True)
    zc = z - mu
    var = jnp.mean(zc * zc, axis=-1, keepdims=True)
    o_ref[...] = zc * lax.rsqrt(var + LN_EPS) * lng_ref[...] + lnb_ref[...]


def mla_ln_pallas(xa, mod3, n_lat, seq, bsz, w_down, q_norm, kv_norm, w_uq, w_uk, w_uv, w_o, ln_g, ln_b):
    f32, bf16 = jnp.float32, jnp.bfloat16
    n_tok, dm = xa.shape
    n_ctx = n_tok - n_lat
    ctx_len = n_ctx // bsz
    tm = TOK_TM
    wd_t, wuq_t, wuk, wuv_t = _mla_prep(w_down, w_uq, w_uk, w_uv)
    cos_t, sin_t = _rope_tables(seq, tm)
    mod_idx = functools.partial(_mod_row_index, tm=tm, n_lat=n_lat, seq=seq, bsz=bsz)
    tiles_per_sample = seq // tm
    pos_idx = lambda i: jnp.where(i * tm < n_lat, i % tiles_per_sample, tiles_per_sample)
    n_hd, n_dn = MLA_HEADS * MLA_HD, wd_t.shape[0]
    whole = lambda shape: pl.BlockSpec(shape, lambda i: (0,) * len(shape))
    params = pltpu.CompilerParams(dimension_semantics=("arbitrary",), vmem_limit_bytes=VMEM_LIMIT)
    qt, k, vt = pl.pallas_call(
        _mla_proj_kernel,
        grid=(n_tok // tm,),
        in_specs=[
            pl.BlockSpec((tm, dm), lambda i: (i, 0)),
            pl.BlockSpec((1, 1, 6 * dm), lambda i: (mod_idx(i), 0, 0)),
            pl.BlockSpec((MLA_HD, tm), lambda i: (0, pos_idx(i))),
            pl.BlockSpec((MLA_HD, tm), lambda i: (0, pos_idx(i))),
            whole((n_dn, dm)), whole((n_hd, MLA_Q_RANK)), whole((MLA_KV_RANK, n_hd)),
            whole((MLA_HEADS * MLA_V, MLA_KV_RANK)), whole((MLA_Q_RANK, 1)), whole((MLA_KV_RANK, 1)),
        ],
        out_specs=[
            pl.BlockSpec((MLA_HEADS, MLA_HD, tm), lambda i: (0, 0, i)),
            pl.BlockSpec((MLA_HEADS, tm, MLA_HD), lambda i: (0, i, 0)),
            pl.BlockSpec((MLA_HEADS * MLA_V, tm), lambda i: (0, i)),
        ],
        out_shape=[
            jax.ShapeDtypeStruct((MLA_HEADS, MLA_HD, n_tok), bf16),
            jax.ShapeDtypeStruct((MLA_HEADS, n_tok, MLA_HD), bf16),
            jax.ShapeDtypeStruct((MLA_HEADS * MLA_V, n_tok), bf16),
        ],
        compiler_params=params,
        name="mla_proj",
    )(xa, mod3, cos_t, sin_t, wd_t, wuq_t, wuk, wuv_t, q_norm.reshape(-1, 1), kv_norm.reshape(-1, 1))

    tq, hb = MLA_TQ, MLA_HB
    n_qt = seq // tq
    ctx_blk = n_lat // ctx_len
    params3 = pltpu.CompilerParams(dimension_semantics=("arbitrary",) * 3, vmem_limit_bytes=VMEM_LIMIT)
    ot_lat = pl.pallas_call(
        functools.partial(_mla_attn_kernel, with_latent=True),
        grid=(bsz, MLA_HEADS // hb, n_qt),
        in_specs=[
            pl.BlockSpec((hb, MLA_HD, tq), lambda b, hd, t: (hd, 0, b * n_qt + t)),
            pl.BlockSpec((hb, seq, MLA_HD), lambda b, hd, t: (hd, b, 0)),
            pl.BlockSpec((hb, ctx_len, MLA_HD), lambda b, hd, t: (hd, ctx_blk + b, 0)),
            pl.BlockSpec((hb * MLA_V, seq), lambda b, hd, t: (hd, b)),
            pl.BlockSpec((hb * MLA_V, ctx_len), lambda b, hd, t: (hd, ctx_blk + b)),
        ],
        out_specs=pl.BlockSpec((hb * MLA_V, tq), lambda b, hd, t: (hd, b * n_qt + t)),
        out_shape=jax.ShapeDtypeStruct((MLA_HEADS * MLA_V, n_lat), bf16),
        compiler_params=params3,
        name="mla_attn_latent",
    )(qt, k, k, vt, vt)
    ot_ctx = pl.pallas_call(
        functools.partial(_mla_attn_kernel, with_latent=False),
        grid=(bsz, MLA_HEADS // hb, 1),
        in_specs=[
            pl.BlockSpec((hb, MLA_HD, ctx_len), lambda b, hd, t: (hd, 0, ctx_blk + b)),
            pl.BlockSpec((hb, ctx_len, MLA_HD), lambda b, hd, t: (hd, ctx_blk + b, 0)),
            pl.BlockSpec((hb * MLA_V, ctx_len), lambda b, hd, t: (hd, ctx_blk + b)),
        ],
        out_specs=pl.BlockSpec((hb * MLA_V, ctx_len), lambda b, hd, t: (hd, b)),
        out_shape=jax.ShapeDtypeStruct((MLA_HEADS * MLA_V, n_ctx), bf16),
        compiler_params=params3,
        name="mla_attn_context",
    )(qt, k, vt)
    ot = jnp.concatenate([ot_lat, ot_ctx], axis=1)
    return pl.pallas_call(
        _mla_out_kernel,
        grid=(n_tok // tm,),
        in_specs=[
            pl.BlockSpec((MLA_HEADS * MLA_V, tm), lambda i: (0, i)),
            pl.BlockSpec((tm, dm), lambda i: (i, 0)),
            pl.BlockSpec((1, 1, 6 * dm), lambda i: (mod_idx(i), 0, 0)),
            whole((MLA_HEADS * MLA_V, dm)), whole((1, dm)), whole((1, dm)),
        ],
        out_specs=pl.BlockSpec((tm, dm), lambda i: (i, 0)),
        out_shape=jax.ShapeDtypeStruct((n_tok, dm), f32),
        compiler_params=params,
        name="mla_out_ln",
    )(ot, xa, mod3, w_o.astype(bf16), ln_g.reshape(1, dm), ln_b.reshape(1, dm))


SSD_DT_PAD = 128
SSD_HPG = SSD_HEADS // SSD_GROUPS
SSD_GW = SSD_HPG * SSD_HEADDIM


def _ssd_in_kernel(x_ref, mod_ref, w_ref, wdt_ref, z_ref, xbc_ref, dt_ref, dtt_ref):
    f32, bf16 = jnp.float32, jnp.bfloat16
    dm = x_ref.shape[1]
    mod = mod_ref[0]
    h = x_ref[...] * (1.0 + mod[:, dm:2 * dm]) + mod[:, 0:dm]
    proj = jnp.dot(h.astype(bf16), w_ref[...], preferred_element_type=f32)
    z_ref[...] = proj[:, :SSD_D_INNER]
    xbc_ref[...] = proj[:, SSD_D_INNER:SSD_D_INNER + SSD_CONV_DIM]
    dt_ref[...] = proj[:, SSD_D_INNER + SSD_CONV_DIM:]
    dtt_ref[...] = jnp.dot(wdt_ref[...], h.T.astype(bf16), preferred_element_type=f32)


def _softplus(v):
    return jnp.maximum(v, 0.0) + jnp.log(1.0 + jnp.exp(-jnp.abs(v)))


def _split3(v):
    f32, bf16 = jnp.float32, jnp.bfloat16
    hi = v.astype(bf16)
    r1 = v - hi.astype(f32)
    mid = r1.astype(bf16)
    lo = (r1 - mid.astype(f32)).astype(bf16)
    return jnp.concatenate([hi, mid, lo], axis=1)


def _ssd_direction(direction, xbc_ref, prev_ref, next_ref, dt_ref, dtt_ref, has_prev, has_next,
                   cw_ref, cb_ref, bias_r_ref, a_r_ref, bias_c_ref, a_c_ref, e_ref, dskip_ref, state_ref, y_ref):
    f32, bf16 = jnp.float32, jnp.bfloat16
    hp = lax.Precision.HIGHEST
    q = xbc_ref.shape[0]
    nh = SSD_HEADS
    xm = xbc_ref[...]
    row = lax.broadcasted_iota(jnp.int32, xm.shape, 0)
    before = jnp.where(has_prev, prev_ref[7:8, :], 0.0)
    after = jnp.where(has_next, next_ref[0:1, :], 0.0)
    x_dn = jnp.where(row == 0, before, pltpu.roll(xm, 1, axis=0))
    x_up = jnp.where(row == q - 1, after, pltpu.roll(xm, q - 1, axis=0))
    conv = cb_ref[...] + x_dn * cw_ref[0:1, :] + xm * cw_ref[1:2, :] + x_up * cw_ref[2:3, :]
    conv = conv * jax.nn.sigmoid(conv)
    xs = conv[:, :SSD_D_INNER]
    gn = SSD_GROUPS * SSD_STATE
    bm, cm = conv[:, SSD_D_INNER:SSD_D_INNER + gn], conv[:, SSD_D_INNER + gn:]
    hs = slice(direction * nh, (direction + 1) * nh)
    dt = _softplus(dt_ref[:, hs] + bias_r_ref[direction:direction + 1, :])
    dtt = _softplus(dtt_ref[hs, :] + bias_c_ref[direction])
    r_i = lax.broadcasted_iota(jnp.int32, (q, q), 0)
    c_i = lax.broadcasted_iota(jnp.int32, (q, q), 1)
    causal = (r_i >= c_i) if direction == 0 else (r_i <= c_i)
    tri = causal.astype(f32)
    tri_t = ((c_i >= r_i) if direction == 0 else (c_i <= r_i)).astype(f32)
    a_cum = jnp.dot(tri, dt * a_r_ref[direction:direction + 1, :], precision=hp, preferred_element_type=f32)
    a_cum_t = jnp.dot(dtt * a_c_ref[direction], tri_t, precision=hp, preferred_element_type=f32)
    last = q - 1 if direction == 0 else 0
    a_exp = jnp.dot(_split3(a_cum), e_ref[...], preferred_element_type=f32)
    dt_exp = jnp.dot(_split3(dt), e_ref[...], preferred_element_type=f32)
    xdt = xs * dt_exp
    a_tot = a_exp[last:last + 1, :]
    xw = (xdt * jnp.exp(a_tot - a_exp)).astype(bf16)
    grow = jnp.exp(a_exp)
    carry = jnp.exp(a_tot)
    xdt_bf = xdt.astype(bf16)
    for g in range(SSD_GROUPS):
        gl = slice(g * SSD_GW, (g + 1) * SSD_GW)
        nl = slice(g * SSD_STATE, (g + 1) * SSD_STATE)
        bm_g, cm_g = bm[:, nl], cm[:, nl].astype(bf16)
        prev = state_ref[direction, g]
        y_g = jnp.dot(cm_g, prev.astype(bf16), preferred_element_type=f32) * grow[:, gl]
        states = jnp.dot(bm_g.T.astype(bf16), xw[:, gl], preferred_element_type=f32)
        state_ref[direction, g] = prev * carry[:, gl] + states
        cb = lax.dot_general(cm_g, bm_g.astype(bf16), (((1,), (1,)), ((), ())), preferred_element_type=f32)
        parts = []
        for hh in range(SSD_HPG):
            hd = g * SSD_HPG + hh
            seg = jnp.broadcast_to(a_cum[:, hd:hd + 1], (q, q)) - a_cum_t[hd:hd + 1, :]
            lmat = (jnp.where(causal, jnp.exp(seg), 0.0) * cb).astype(bf16)
            parts.append(jnp.dot(lmat, xdt_bf[:, hd * SSD_HEADDIM:(hd + 1) * SSD_HEADDIM], preferred_element_type=f32))
        y_g = y_g + jnp.concatenate(parts, axis=1)
        if direction == 0:
            y_g = y_g + dskip_ref[:, gl] * xs[:, gl]
        y_ref[:, gl] = y_g


def _ssd_scan_kernel(xf_ref, pf_ref, nf_ref, dtf_ref, dttf_ref, xb_ref, pb_ref, nb_ref, dtb_ref, dttb_ref,
                     cw_ref, cb_ref, bias_r_ref, a_r_ref, bias_c_ref, a_c_ref, e_ref, dskip_ref,
                     yf_ref, yb_ref, state_ref, *, ctx_chunks, lat_chunks):
    j = pl.program_id(1)

    @pl.when(j == 0)
    def _reset():
        state_ref[...] = jnp.zeros(state_ref.shape, jnp.float32)

    in_ctx = j < ctx_chunks
    pos_f = jnp.where(in_ctx, j, j - ctx_chunks)
    seg_len = jnp.where(in_ctx, ctx_chunks, lat_chunks)
    pos_b = seg_len - 1 - pos_f
    shared = (cw_ref, cb_ref, bias_r_ref, a_r_ref, bias_c_ref, a_c_ref, e_ref, dskip_ref, state_ref)
    _ssd_direction(0, xf_ref, pf_ref, nf_ref, dtf_ref, dttf_ref, pos_f > 0, pos_f < seg_len - 1, *shared, yf_ref)
    _ssd_direction(1, xb_ref, pb_ref, nb_ref, dtb_ref, dttb_ref, pos_b > 0, pos_b < seg_len - 1, *shared, yb_ref)


def _ssd_out_kernel(yf_ref, yb_ref, z_ref, x_ref, mod_ref, nw_ref, wo_ref, lng_ref, lnb_ref, o_ref):
    f32 = jnp.float32
    dm = x_ref.shape[1]
    z = z_ref[...]
    y = (yf_ref[...] + yb_ref[...]) * (z * jax.nn.sigmoid(z))
    y = y * lax.rsqrt(jnp.mean(y * y, axis=-1, keepdims=True) + RMS_EPS) * nw_ref[...]
    out = jnp.dot(y.astype(jnp.bfloat16), wo_ref[...], preferred_element_type=f32)
    mod = mod_ref[0]
    v = DEEPNORM_ALPHA * x_ref[...] + mod[:, 2 * dm:3 * dm] * out
    mu = jnp.mean(v, axis=-1, keepdims=True)
    vc = v - mu
    var = jnp.mean(vc * vc, axis=-1, keepdims=True)
    o_ref[...] = vc * lax.rsqrt(var + LN_EPS) * lng_ref[...] + lnb_ref[...]


def ssd_ln_pallas(xa, mod3, n_lat, seq, bsz, w_in, conv_w, conv_b, dt_bias, a_log, d, norm_w, w_out, ln_g, ln_b):
    f32, bf16 = jnp.float32, jnp.bfloat16
    n_tok, dm = xa.shape
    ctx_len = (n_tok - n_lat) // bsz
    tm = TOK_TM
    q = SSD_CHUNK
    nh = SSD_HEADS
    pad = SSD_DT_PAD - 2 * nh
    w_ext = jnp.concatenate([w_in, jnp.zeros((dm, pad), w_in.dtype)], axis=1).astype(bf16)
    n_in = w_ext.shape[1]
    wdt_t = w_ext[:, SSD_D_INNER + SSD_CONV_DIM:].T
    mod_idx = functools.partial(_mod_row_index, tm=tm, n_lat=n_lat, seq=seq, bsz=bsz)
    whole = lambda shape: pl.BlockSpec(shape, lambda *_: (0,) * len(shape))
    params = pltpu.CompilerParams(dimension_semantics=("arbitrary",), vmem_limit_bytes=VMEM_LIMIT)
    z, xbc, dt, dtt = pl.pallas_call(
        _ssd_in_kernel,
        grid=(n_tok // tm,),
        in_specs=[
            pl.BlockSpec((tm, dm), lambda i: (i, 0)),
            pl.BlockSpec((1, 1, 6 * dm), lambda i: (mod_idx(i), 0, 0)),
            whole((dm, n_in)), whole((SSD_DT_PAD, dm)),
        ],
        out_specs=[
            pl.BlockSpec((tm, SSD_D_INNER), lambda i: (i, 0)),
            pl.BlockSpec((tm, SSD_CONV_DIM), lambda i: (i, 0)),
            pl.BlockSpec((tm, SSD_DT_PAD), lambda i: (i, 0)),
            pl.BlockSpec((SSD_DT_PAD, tm), lambda i: (0, i)),
        ],
        out_shape=[
            jax.ShapeDtypeStruct((n_tok, SSD_D_INNER), f32),
            jax.ShapeDtypeStruct((n_tok, SSD_CONV_DIM), f32),
            jax.ShapeDtypeStruct((n_tok, SSD_DT_PAD), f32),
            jax.ShapeDtypeStruct((SSD_DT_PAD, n_tok), f32),
        ],
        compiler_params=params,
        name="ssd_in_proj",
    )(xa, mod3, w_ext, wdt_t)

    ctx_chunks, lat_chunks = ctx_len // q, seq // q
    ctx_base = n_lat // q
    n_chunk_total = n_tok // q

    def chunk_of(b, j, backward):
        in_ctx = j < ctx_chunks
        pos = jnp.where(in_ctx, j, j - ctx_chunks)
        seg = jnp.where(in_ctx, ctx_chunks, lat_chunks)
        pos = jnp.where(backward, seg - 1 - pos, pos)
        return jnp.where(in_ctx, ctx_base + b * ctx_chunks + pos, b * lat_chunks + pos)

    sub = q // 8
    def dir_specs(backward):
        ch = lambda b, j: chunk_of(b, j, backward)
        return [
            pl.BlockSpec((q, SSD_CONV_DIM), lambda b, j: (ch(b, j), 0)),
            pl.BlockSpec((8, SSD_CONV_DIM), lambda b, j: (jnp.maximum(ch(b, j) * sub - 1, 0), 0)),
            pl.BlockSpec((8, SSD_CONV_DIM), lambda b, j: (jnp.minimum((ch(b, j) + 1) * sub, n_chunk_total * sub - 1), 0)),
            pl.BlockSpec((q, SSD_DT_PAD), lambda b, j: (ch(b, j), 0)),
            pl.BlockSpec((SSD_DT_PAD, q), lambda b, j: (0, ch(b, j))),
        ]

    a = -jnp.exp(a_log.astype(f32))
    expand = jnp.repeat(jnp.eye(nh, dtype=f32), SSD_HEADDIM, axis=1)
    e3 = jnp.concatenate([expand, expand, expand], axis=0).astype(bf16)
    dskip = jnp.repeat(d, SSD_HEADDIM)[None, :]
    yf, yb = pl.pallas_call(
        functools.partial(_ssd_scan_kernel, ctx_chunks=ctx_chunks, lat_chunks=lat_chunks),
        grid=(bsz, ctx_chunks + lat_chunks),
        in_specs=dir_specs(False) + dir_specs(True) + [
            whole((SSD_CONV, SSD_CONV_DIM)), whole((1, SSD_CONV_DIM)),
            whole((2, nh)), whole((2, nh)), whole((2, nh, 1)), whole((2, nh, 1)),
            whole((3 * nh, SSD_D_INNER)), whole((1, SSD_D_INNER)),
        ],
        out_specs=[
            pl.BlockSpec((q, SSD_D_INNER), lambda b, j: (chunk_of(b, j, False), 0)),
            pl.BlockSpec((q, SSD_D_INNER), lambda b, j: (chunk_of(b, j, True), 0)),
        ],
        out_shape=[jax.ShapeDtypeStruct((n_tok, SSD_D_INNER), f32)] * 2,
        scratch_shapes=[pltpu.VMEM((2, SSD_GROUPS, SSD_STATE, SSD_GW), f32)],
        compiler_params=pltpu.CompilerParams(dimension_semantics=("arbitrary", "arbitrary"), vmem_limit_bytes=VMEM_LIMIT),
        name="ssd_scan",
    )(xbc, xbc, xbc, dt, dtt, xbc, xbc, xbc, dt, dtt,
      conv_w, conv_b.reshape(1, -1), dt_bias, a, dt_bias.reshape(2, nh, 1), a.reshape(2, nh, 1), e3, dskip)

    return pl.pallas_call(
        _ssd_out_kernel,
        grid=(n_tok // tm,),
        in_specs=[
            pl.BlockSpec((tm, SSD_D_INNER), lambda i: (i, 0)),
            pl.BlockSpec((tm, SSD_D_INNER), lambda i: (i, 0)),
            pl.BlockSpec((tm, SSD_D_INNER), lambda i: (i, 0)),
            pl.BlockSpec((tm, dm), lambda i: (i, 0)),
            pl.BlockSpec((1, 1, 6 * dm), lambda i: (mod_idx(i), 0, 0)),
            whole((1, SSD_D_INNER)), whole((SSD_D_INNER, dm)), whole((1, dm)), whole((1, dm)),
        ],
        out_specs=pl.BlockSpec((tm, dm), lambda i: (i, 0)),
        out_shape=jax.ShapeDtypeStruct((n_tok, dm), f32),
        compiler_params=params,
        name="ssd_out_ln",
    )(yf, yb, z, xa, mod3, norm_w.reshape(1, -1), w_out.astype(bf16), ln_g.reshape(1, dm), ln_b.reshape(1, dm))


def kernel(x, c, ctx, c_ctx, mod_w, mod_b, ln_g, ln_b,
           s5_a_re, s5_a_im, s5_log_dt, s5_b_re, s5_b_im, s5_c_re, s5_c_im, s5_d, s5_w_gate, s5_w_val,
           ssd_w_in, ssd_conv_w, ssd_conv_b, ssd_dt_bias, ssd_a_log, ssd_d, ssd_norm_w, ssd_w_out,
           mla_w_down, mla_q_norm, mla_kv_norm, mla_w_uq, mla_w_uk, mla_w_uv, mla_w_o,
           peer_w_q, peer_subkeys, peer_u, peer_v):
    ROWS = x.shape[1] // GRID_W
    pos = grid_positions(ROWS)
    ctx_len = ctx.shape[1]
    c_act = jax.nn.silu(c)
    c_ctx_act = jax.nn.silu(c_ctx)
    bsz, seq_len, dm = x.shape
    n_lat, n_ctx = bsz * seq_len, bsz * ctx_len
    xa = jnp.concatenate([x.reshape(n_lat, dm), ctx.reshape(n_ctx, dm)], axis=0)
    for i in range(DEPTH):
        last = i == DEPTH - 1
        mod_tab = jnp.concatenate([c_act, c_ctx_act[None], jnp.zeros((7 - bsz, dm), x.dtype)], axis=0) @ mod_w[i] + mod_b[i]
        mod3 = mod_tab.reshape(8, 1, 6 * dm)
        kind, j = i % N_MIXERS, i // N_MIXERS
        if kind == 0:
            prep = _s5_prep(s5_a_re[j], s5_a_im[j], s5_log_dt[j], s5_b_re[j], s5_b_im[j], s5_c_re[j], s5_c_im[j], s5_d[j])
            ya = s5_ssm_pallas(xa, mod_tab, n_lat, seq_len, bsz, prep)
            if last:
                ya, xa = ya[:n_lat], xa[:n_lat]
            xa = glu_ln_pallas(ya, xa, mod3, n_lat, seq_len, bsz, s5_w_val[j], s5_w_gate[j], ln_g[i, 0], ln_b[i, 0])
        elif kind == 1:
            xa = ssd_ln_pallas(xa, mod3, n_lat, seq_len, bsz, ssd_w_in[j], ssd_conv_w[j], ssd_conv_b[j], ssd_dt_bias[j],
                               ssd_a_log[j], ssd_d[j], ssd_norm_w[j], ssd_w_out[j], ln_g[i, 0], ln_b[i, 0])
            if last:
                xa = xa[:n_lat]
        else:
            xa = mla_ln_pallas(xa, mod3, n_lat, seq_len, bsz, mla_w_down[j], mla_q_norm[j], mla_kv_norm[j],
                               mla_w_uq[j], mla_w_uk[j], mla_w_uv[j], mla_w_o[j], ln_g[i, 0], ln_b[i, 0])
            if last:
                xa = xa[:n_lat]
        xa = peer_ln_pallas(xa, mod3, n_lat, seq_len, bsz, peer_w_q[i], peer_subkeys[i], peer_u[i], peer_v[i],
                            ln_g[i, 1], ln_b[i, 1])
    return xa[:n_lat].reshape(bsz, seq_len, dm)
```

```python
import math
import functools
import jax
import jax.numpy as jnp
from jax import lax
import numpy as np
from jax.experimental import pallas as pl
from jax.experimental.pallas import tpu as pltpu

D_MODEL = 1024
BATCH = 4
SEQ = 4096
DEPTH = 4

GRID_W = 64
CTX_LEN = 256
N_MIXERS = 3
DEEPNORM_ALPHA = (2.0 * DEPTH) ** 0.25
LN_EPS = 1e-5
RMS_EPS = 1e-6
ROPE_BASE = 10000.0

S5_GROUP = 16
S5_GROUPS = D_MODEL // S5_GROUP
S5_STATE = 64

SSD_D_INNER = 2 * D_MODEL
SSD_HEADDIM = 64
SSD_HEADS = SSD_D_INNER // SSD_HEADDIM
SSD_GROUPS = 4
SSD_STATE = 128
SSD_CONV = 3
SSD_CHUNK = 128
SSD_CONV_DIM = SSD_D_INNER + 2 * SSD_GROUPS * SSD_STATE
SSD_IN_DIM = SSD_D_INNER + SSD_CONV_DIM + 2 * SSD_HEADS

MLA_HEADS = 16
MLA_Q_RANK = 256
MLA_KV_RANK = 128
MLA_NOPE = 64
MLA_ROPE = 32
MLA_V = 64
MLA_BLOCK = 128

PEER_HEADS = 8
PEER_KEYS = 128
PEER_EXPERTS = PEER_KEYS * PEER_KEYS
PEER_QDIM = 256
PEER_TOPK = 16
PEER_BLOCK = 128


def layer_norm(x, g, b):
    xf = x.astype(jnp.float32)
    mu = jnp.mean(xf, axis=-1, keepdims=True)
    var = jnp.mean(jnp.square(xf - mu), axis=-1, keepdims=True)
    return ((xf - mu) * lax.rsqrt(var + LN_EPS)).astype(x.dtype) * g + b


def _ln_kernel(x_ref, g_ref, b_ref, o_ref):
    xf = x_ref[...]
    mu = jnp.mean(xf, axis=-1, keepdims=True)
    xc = xf - mu
    var = jnp.mean(xc * xc, axis=-1, keepdims=True)
    o_ref[...] = xc * lax.rsqrt(var + LN_EPS) * g_ref[...] + b_ref[...]


def layer_norm_pallas(x, g, b):
    shp = x.shape
    x2 = x.reshape(-1, shp[-1])
    n, d = x2.shape
    tb = 512
    out = pl.pallas_call(
        _ln_kernel,
        grid=(n // tb,),
        in_specs=[pl.BlockSpec((tb, d), lambda i: (i, 0)),
                  pl.BlockSpec((1, d), lambda i: (0, 0)),
                  pl.BlockSpec((1, d), lambda i: (0, 0))],
        out_specs=pl.BlockSpec((tb, d), lambda i: (i, 0)),
        out_shape=jax.ShapeDtypeStruct((n, d), jnp.float32),
        name="final_ln",
    )(x2, g.reshape(1, d), b.reshape(1, d))
    return out.reshape(shp)


def rms_norm(x, g):
    xf = x.astype(jnp.float32)
    return (xf * lax.rsqrt(jnp.mean(jnp.square(xf), axis=-1, keepdims=True) + RMS_EPS)).astype(x.dtype) * g


def modulate(x, shift, scale):
    return x * (1.0 + scale) + shift


def grid_positions(rows):
    row = jnp.repeat(jnp.arange(rows, dtype=jnp.float32), GRID_W)
    col = jnp.tile(jnp.arange(GRID_W, dtype=jnp.float32), rows)
    return row, col


def rope_axial(x, row, col):
    half = x.shape[-1] // 2
    quarter = half // 2
    freqs = ROPE_BASE ** (-jnp.arange(quarter, dtype=jnp.float32) / quarter)

    def rot(xp, pos):
        ang = pos[:, None] * freqs
        cos = jnp.cos(ang)[None, :, None, :].astype(x.dtype)
        sin = jnp.sin(ang)[None, :, None, :].astype(x.dtype)
        x1, x2 = xp[..., :quarter], xp[..., quarter:]
        return jnp.concatenate([x1 * cos - x2 * sin, x2 * cos + x1 * sin], axis=-1)

    return jnp.concatenate([rot(x[..., :half], row), rot(x[..., half:], col)], axis=-1)


def _lin_rec(left, right):
    a1, b1 = left
    a2, b2 = right
    return a1 * a2, a2 * b1 + b2


def s5_scan(u, a_bar, b_bar, c_mat, init, reverse):
    bu = jnp.einsum('lgc,gpc->lgp', u.astype(jnp.float32), b_bar)
    a = jnp.broadcast_to(a_bar, bu.shape)
    a_cum, s = lax.associative_scan(_lin_rec, (a, bu), reverse=reverse, axis=0)
    s = s + a_cum * init
    y = jnp.einsum('lgp,gcp->lgc', s, c_mat).real
    final = s[0] if reverse else s[-1]
    return y, final


def s5_direction(u_ctx, u_lat, a_re, a_im, log_dt, b_re, b_im, c_re, c_im, reverse):
    lam = lax.complex(a_re.astype(jnp.float32), a_im.astype(jnp.float32))
    a_bar = jnp.exp(lam * jnp.exp(log_dt.astype(jnp.float32))[:, None])
    b_mat = lax.complex(b_re.astype(jnp.float32), b_im.astype(jnp.float32))
    b_bar = ((a_bar - 1.0) / lam)[..., None] * b_mat
    c_mat = lax.complex(c_re.astype(jnp.float32), c_im.astype(jnp.float32))

    def per_sample(args):
        uc, ul = args
        yc, sc = s5_scan(uc, a_bar, b_bar, c_mat, jnp.zeros_like(a_bar), reverse)
        yl, _ = s5_scan(ul, a_bar, b_bar, c_mat, sc, reverse)
        return yc, yl

    return lax.map(per_sample, (u_ctx, u_lat))


def s5_mixer(h, hc, a_re, a_im, log_dt, b_re, b_im, c_re, c_im, d, w_gate, w_val):
    bsz, seq_len, _ = h.shape
    ctx_len = hc.shape[1]
    u = h.reshape(bsz, seq_len, S5_GROUPS, S5_GROUP)
    uc = hc.reshape(bsz, ctx_len, S5_GROUPS, S5_GROUP)
    y_l = d * h
    y_c = d * hc
    for direction in range(2):
        yc, yl = s5_direction(uc, u, a_re[direction], a_im[direction], log_dt[direction],
                              b_re[direction], b_im[direction], c_re[direction], c_im[direction],
                              direction == 1)
        y_l = y_l + yl.reshape(bsz, seq_len, D_MODEL).astype(h.dtype)
        y_c = y_c + yc.reshape(bsz, ctx_len, D_MODEL).astype(h.dtype)

    def glu(y):
        g = jax.nn.gelu(y)
        return (g @ w_val) * jax.nn.sigmoid(g @ w_gate)

    return glu(y_l), glu(y_c)


def depthwise_conv_centred(x, w, b):
    k_w = w.shape[0]
    pad = k_w // 2
    seq_len = x.shape[1]
    xp = jnp.pad(x, ((0, 0), (pad, pad), (0, 0)))
    out = b
    for k in range(k_w):
        out = out + xp[:, k:k + seq_len] * w[k]
    return out


def ssd_scan(x, dt, a, bm, cm, init):
    bsz, seq_len, n_heads, p_dim = x.shape
    n_grp, n_st = bm.shape[2], bm.shape[3]
    hg = n_heads // n_grp
    q_len = SSD_CHUNK
    nc = seq_len // q_len
    xc = x.reshape(bsz, nc, q_len, n_grp, hg, p_dim)
    dtc = dt.reshape(bsz, nc, q_len, n_grp, hg).astype(jnp.float32)
    bc = bm.reshape(bsz, nc, q_len, n_grp, n_st)
    cc = cm.reshape(bsz, nc, q_len, n_grp, n_st)
    a_cum = jnp.cumsum(dtc * a.reshape(n_grp, hg), axis=2)
    xdt = xc * dtc[..., None].astype(x.dtype)
    tri = jnp.tril(jnp.ones((q_len, q_len), dtype=bool))
    seg = a_cum[:, :, :, None] - a_cum[:, :, None]
    decay = jnp.exp(jnp.where(tri[:, :, None, None], seg, -jnp.inf)).astype(x.dtype)
    cb = jnp.einsum('bcqgn,bcsgn->bcgqs', cc, bc)
    y_diag = jnp.einsum('bcgqs,bcqsgh,bcsghp->bcqghp', cb, decay, xdt)
    decay_states = jnp.exp(a_cum[:, :, -1:] - a_cum).astype(x.dtype)
    states = jnp.einsum('bcsgn,bcsgh,bcsghp->bcghpn', bc, decay_states, xdt)
    chunk_decay = jnp.exp(a_cum[:, :, -1]).astype(x.dtype)

    def step(carry, inp):
        dec, st = inp
        return carry * dec[..., None, None] + st, carry

    final, prev = lax.scan(step, init, (jnp.moveaxis(chunk_decay, 1, 0), jnp.moveaxis(states, 1, 0)))
    y_off = jnp.einsum('bcqgn,cbghpn,bcqgh->bcqghp', cc, prev, jnp.exp(a_cum).astype(x.dtype))
    return (y_diag + y_off).reshape(bsz, seq_len, n_heads, p_dim), final


def ssd_mixer(h, hc, w_in, conv_w, conv_b, dt_bias, a_log, d, norm_w, w_out):
    a = -jnp.exp(a_log.astype(jnp.float32))

    def flip(t):
        return jnp.flip(t, axis=1)

    def bidir(t, inits):
        bsz, seq_len = t.shape[0], t.shape[1]
        z, xbc, dt = jnp.split(t @ w_in, [SSD_D_INNER, SSD_D_INNER + SSD_CONV_DIM], axis=-1)
        xbc = jax.nn.silu(depthwise_conv_centred(xbc, conv_w, conv_b))
        xs, bm, cm = jnp.split(xbc, [SSD_D_INNER, SSD_D_INNER + SSD_GROUPS * SSD_STATE], axis=-1)
        xs = xs.reshape(bsz, seq_len, SSD_HEADS, SSD_HEADDIM)
        bm = bm.reshape(bsz, seq_len, SSD_GROUPS, SSD_STATE)
        cm = cm.reshape(bsz, seq_len, SSD_GROUPS, SSD_STATE)
        dt = jax.nn.softplus(dt.reshape(bsz, seq_len, 2, SSD_HEADS) + dt_bias)
        y_f, s_f = ssd_scan(xs, dt[:, :, 0], a[0], bm, cm, inits[0])
        y_b, s_b = ssd_scan(flip(xs), flip(dt[:, :, 1]), a[1], flip(bm), flip(cm), inits[1])
        y = y_f + flip(y_b) + d[:, None] * xs
        y = rms_norm(y.reshape(z.shape) * jax.nn.silu(z), norm_w)
        return y @ w_out, (s_f, s_b)

    zero = jnp.zeros((h.shape[0], SSD_GROUPS, SSD_HEADS // SSD_GROUPS, SSD_HEADDIM, SSD_STATE), h.dtype)
    out_c, states_c = bidir(hc, (zero, zero))
    out_l, _ = bidir(h, states_c)
    return out_l, out_c


def mla_project(t, w_down, q_norm, kv_norm, w_uq, w_uk, w_uv, pos):
    cq, ckv, kr = jnp.split(t @ w_down, [MLA_Q_RANK, MLA_Q_RANK + MLA_KV_RANK], axis=-1)
    q = jnp.einsum('btr,rhd->bthd', rms_norm(cq, q_norm), w_uq)
    ckv = rms_norm(ckv, kv_norm)
    k_nope = jnp.einsum('btr,rhd->bthd', ckv, w_uk)
    v = jnp.einsum('btr,rhd->bthd', ckv, w_uv)
    q_nope, q_rope = q[..., :MLA_NOPE], q[..., MLA_NOPE:]
    kr = kr[:, :, None, :]
    if pos is not None:
        q_rope = rope_axial(q_rope, pos[0], pos[1])
        kr = rope_axial(kr, pos[0], pos[1])
    k = jnp.concatenate([k_nope, jnp.broadcast_to(kr, k_nope.shape[:3] + (MLA_ROPE,))], axis=-1)
    q = jnp.concatenate([q_nope, q_rope], axis=-1)
    return q, k, v


def attend(q, k, v):
    s = jnp.einsum('bqhd,bkhd->bhqk', q, k).astype(jnp.float32) * (MLA_NOPE + MLA_ROPE) ** -0.5
    p = jax.nn.softmax(s, axis=-1).astype(v.dtype)
    return jnp.einsum('bhqk,bkhd->bqhd', p, v)


def mla_mixer(h, hc, pos, w_down, q_norm, kv_norm, w_uq, w_uk, w_uv, w_o):
    bsz, seq_len, _ = h.shape
    qc, kc, vc = mla_project(hc, w_down, q_norm, kv_norm, w_uq, w_uk, w_uv, None)
    ql, kl, vl = mla_project(h, w_down, q_norm, kv_norm, w_uq, w_uk, w_uv, pos)
    out_c = attend(qc, kc, vc).reshape(bsz, hc.shape[1], MLA_HEADS * MLA_V)
    k_all = jnp.concatenate([kc, kl], axis=1)
    v_all = jnp.concatenate([vc, vl], axis=1)
    nb = seq_len // MLA_BLOCK
    qb = jnp.moveaxis(ql.reshape(bsz, nb, MLA_BLOCK, MLA_HEADS, MLA_NOPE + MLA_ROPE), 1, 0)
    out_l = lax.map(lambda qblk: attend(qblk, k_all, v_all), qb)
    out_l = jnp.moveaxis(out_l, 0, 1).reshape(bsz, seq_len, MLA_HEADS * MLA_V)
    return out_l @ w_o, out_c @ w_o


def peer_ffn(h, w_q, subkeys, u_tab, v_tab):
    bsz, seq_len, dm = h.shape
    half = PEER_QDIM // 2
    q = jnp.einsum('btd,dhk->bthk', h, w_q)
    s1 = jnp.einsum('bthk,hnk->bthn', q[..., :half], subkeys[:, 0])
    s2 = jnp.einsum('bthk,hnk->bthn', q[..., half:], subkeys[:, 1])
    v1, i1 = lax.top_k(s1, PEER_TOPK)
    v2, i2 = lax.top_k(s2, PEER_TOPK)
    n_cand = PEER_TOPK * PEER_TOPK
    cand = (v1[..., :, None] + v2[..., None, :]).reshape(bsz, seq_len, PEER_HEADS, n_cand)
    cidx = (i1[..., :, None] * PEER_KEYS + i2[..., None, :]).reshape(bsz, seq_len, PEER_HEADS, n_cand)
    best, sel = lax.top_k(cand, PEER_TOPK)
    eidx = jnp.take_along_axis(cidx, sel, axis=-1)
    gate = jax.nn.softmax(best.astype(jnp.float32), axis=-1).astype(h.dtype)
    n_blk = (bsz * seq_len) // PEER_BLOCK
    n_sel = PEER_HEADS * PEER_TOPK
    hb = h.reshape(n_blk, PEER_BLOCK, dm)
    ib = eidx.reshape(n_blk, PEER_BLOCK, n_sel)
    gb = gate.reshape(n_blk, PEER_BLOCK, n_sel)

    def block(args):
        hk, ik, gk = args
        u = jnp.take(u_tab, ik, axis=0)
        act = jax.nn.gelu(jnp.einsum('td,ted->te', hk, u))
        v = jnp.take(v_tab, ik, axis=0)
        return jnp.einsum('te,ted->td', gk * act, v)

    return lax.map(block, (hb, ib, gb)).reshape(bsz, seq_len, dm)


PEER_TM = 512
PEER_EB = 1024
PEER_ROWS = 16
PEER_LANES = 128
PEER_ACT_ROWS = 256
VMEM_LIMIT = 56 * 1024 * 1024
NEG_INF = float("-inf")
GELU_C0 = math.sqrt(2.0 / math.pi)
GELU_C1 = 0.044715 * GELU_C0


def _top16_sorted(s):
    n = PEER_TOPK
    m = s.shape[0] // 8
    x = [s[8 * r:8 * r + 8, :] for r in range(m)]

    def exchange(i, l, descending):
        hi, lo = jnp.maximum(x[i], x[l]), jnp.minimum(x[i], x[l])
        x[i], x[l] = (hi, lo) if descending else (lo, hi)

    def merge_bitonic(size):
        j = size // 2
        while j >= 1:
            for i in range(size):
                if i ^ j > i:
                    exchange(i, i ^ j, True)
            j //= 2

    k = 2
    while k < m:
        j = k // 2
        while j >= 1:
            for i in range(m):
                if i ^ j > i:
                    exchange(i, i ^ j, (i & k) == 0)
            j //= 2
        k *= 2
    merge_bitonic(m)
    shifts = [4, 2, 1]
    if m < n:
        first = shifts.pop(0)
        x = x + [pltpu.roll(x[m - 1 - r], first, axis=0) for r in range(m)]
        merge_bitonic(n)
    for shift in shifts:
        other = [pltpu.roll(x[n - 1 - r], shift, axis=0) for r in range(n)]
        for r in range(n):
            x[r] = jnp.maximum(x[r], other[r])
        merge_bitonic(n)
    return [x[r][0:1, :] for r in range(n)]


def _dot3(a_hi, a_lo, b_hi, b_lo):
    f32 = jnp.float32
    return (jnp.dot(a_hi, b_hi, preferred_element_type=f32)
            + jnp.dot(a_hi, b_lo, preferred_element_type=f32)
            + jnp.dot(a_lo, b_hi, preferred_element_type=f32))


def _split_bf16(v):
    hi = v.astype(jnp.bfloat16)
    lo = (v - hi.astype(jnp.float32)).astype(jnp.bfloat16)
    return hi, lo


def _peer_kernel(x_ref, mod_ref, wq_ref, skh_ref, skl_ref, u_ref, vt_ref, vtl_ref, lng_ref, lnb_ref,
                 o_ref,
                 ht_ref, s_ref, n1_ref, e1_ref, r2_ref, e2_ref, v1_ref, v2_ref, cand_ref,
                 act_ref, gact_ref, acc_ref):
    f32 = jnp.float32
    bf16 = jnp.bfloat16
    eb = pl.program_id(1)
    n_eb = pl.num_programs(1)
    dm = x_ref.shape[1]
    a_per = PEER_EB // PEER_KEYS
    half = PEER_QDIM // 2

    @pl.when(eb == 0)
    def _prologue():
        mod = mod_ref[0]
        h = x_ref[...] * (1.0 + mod[:, 4 * dm:5 * dm]) + mod[:, 3 * dm:4 * dm]
        ht = h.T.astype(bf16)
        ht_ref[...] = ht
        qt = jnp.dot(wq_ref[...], ht, preferred_element_type=f32)
        cand_ref[...] = jnp.full(cand_ref.shape, NEG_INF, f32)
        gact_ref[...] = jnp.zeros(gact_ref.shape, bf16)
        for hd in range(PEER_HEADS):
            q1_hi, q1_lo = _split_bf16(qt[hd * PEER_QDIM: hd * PEER_QDIM + half])
            q2_hi, q2_lo = _split_bf16(qt[hd * PEER_QDIM + half: (hd + 1) * PEER_QDIM])
            s_ref[0] = _dot3(skh_ref[hd, 0], skl_ref[hd, 0], q1_hi, q1_lo)
            s_ref[1] = _dot3(skh_ref[hd, 1], skl_ref[hd, 1], q2_hi, q2_lo)
            for lg in range(0, x_ref.shape[0], PEER_LANES):
                lanes = slice(lg, lg + PEER_LANES)
                s1, s2 = s_ref[0, :, lanes], s_ref[1, :, lanes]
                top1 = _top16_sorted(s1)
                top2 = _top16_sorted(s2)
                for k in range(PEER_TOPK):
                    v1_ref[k:k + 1, lanes] = top1[k]
                    v2_ref[k:k + 1, lanes] = top2[k]
                off = 0
                for i in range(PEER_TOPK):
                    cnt = PEER_TOPK // (i + 1)
                    cand_ref[off:off + cnt, lanes] = v1_ref[i:i + 1, lanes] + v2_ref[0:cnt, lanes]
                    off += cnt
                best = _top16_sorted(cand_ref[:, lanes])
                z = jnp.zeros_like(best[0])
                for k in range(PEER_TOPK):
                    z = z + jnp.exp(best[k] - best[0])
                tau = best[PEER_TOPK - 1]
                n1 = jnp.zeros(s1.shape, f32)
                for k in range(PEER_TOPK):
                    n1 = jnp.where(s1 + top2[k] >= tau, k + 1.0, n1)
                n1_ref[hd, :, lanes] = n1
                e1_ref[hd, :, lanes] = jnp.exp(s1 - top1[0])
                rank2 = jnp.zeros(s2.shape, f32)
                for k in range(PEER_TOPK):
                    rank2 = jnp.where(top2[k] > s2, k + 1.0, rank2)
                r2_ref[hd, :, lanes] = rank2.astype(bf16)
                e2_ref[hd, :, lanes] = (jnp.exp(s2 - top2[0]) * (0.5 / z)).astype(bf16)
        acc_ref[...] = jnp.zeros(acc_ref.shape, f32)

    par = eb % 2
    for r0 in range(0, PEER_EB, PEER_ACT_ROWS):
        act_ref[r0:r0 + PEER_ACT_ROWS, :] = jnp.dot(u_ref[r0:r0 + PEER_ACT_ROWS, :], ht_ref[...],
                                                    preferred_element_type=f32)
    acc_ref[...] += jnp.dot(vt_ref[0], gact_ref[1 - par], preferred_element_type=f32)
    a_base = pl.multiple_of(eb * a_per, a_per)
    blk = (PEER_ROWS, x_ref.shape[0])
    n_slab = PEER_KEYS // PEER_ROWS
    for a in range(a_per):
        g = [None] * n_slab
        for hd in range(PEER_HEADS):
            n1a = jnp.broadcast_to(n1_ref[hd, pl.ds(a_base, a_per), :][a:a + 1, :].astype(bf16), blk)
            e1a = jnp.broadcast_to(e1_ref[hd, pl.ds(a_base, a_per), :][a:a + 1, :].astype(bf16), blk)
            for i in range(n_slab):
                rs = slice(i * PEER_ROWS, (i + 1) * PEER_ROWS)
                w = jnp.where(r2_ref[hd, rs, :] < n1a, e2_ref[hd, rs, :], jnp.zeros(blk, bf16)) * e1a
                g[i] = w if g[i] is None else g[i] + w
        for i in range(n_slab):
            rows = slice(a * PEER_KEYS + i * PEER_ROWS, a * PEER_KEYS + (i + 1) * PEER_ROWS)
            x = act_ref[rows, :]
            t = jnp.tanh(x * (GELU_C0 + GELU_C1 * (x * x)))
            gact_ref[par, rows, :] = g[i] * (x + x * t).astype(bf16)

    @pl.when(eb == n_eb - 1)
    def _epilogue():
        mod = mod_ref[0]
        ffn_t = acc_ref[...] + jnp.dot(vtl_ref[0], gact_ref[par], preferred_element_type=f32)
        y = DEEPNORM_ALPHA * x_ref[...] + mod[:, 5 * dm:6 * dm] * ffn_t.T
        mu = jnp.mean(y, axis=-1, keepdims=True)
        yc = y - mu
        var = jnp.mean(yc * yc, axis=-1, keepdims=True)
        o_ref[...] = yc * lax.rsqrt(var + LN_EPS) * lng_ref[...] + lnb_ref[...]


def _mod_row_index(i, tm, n_lat, seq, bsz):
    return jnp.where(i * tm < n_lat, (i * tm) // seq, bsz)


def peer_ln_pallas(xa, mod3, n_lat, seq, bsz, w_q, subkeys, u_tab, v_tab, ln_g, ln_b):
    n_tok, dm = xa.shape
    tm = PEER_TM
    f32, bf16 = jnp.float32, jnp.bfloat16
    wq_t = w_q.reshape(dm, PEER_HEADS * PEER_QDIM).T
    wq_bf = wq_t.astype(bf16)
    sk_hi = subkeys.astype(bf16)
    sk_lo = (subkeys - sk_hi.astype(f32)).astype(bf16)
    u_bf = u_tab.astype(bf16)
    n_eb = PEER_EXPERTS // PEER_EB
    vt_bf = v_tab.reshape(n_eb, PEER_EB, dm).transpose(0, 2, 1).astype(bf16)
    qd = PEER_HEADS * PEER_QDIM
    n_cand = sum(PEER_TOPK // (i + 1) for i in range(PEER_TOPK))
    n_cand_pad = 64
    assert n_cand <= n_cand_pad
    mod_idx = functools.partial(_mod_row_index, tm=tm, n_lat=n_lat, seq=seq, bsz=bsz)
    return pl.pallas_call(
        _peer_kernel,
        grid=(n_tok // tm, n_eb),
        in_specs=[
            pl.BlockSpec((tm, dm), lambda i, e: (i, 0)),
            pl.BlockSpec((1, 1, 6 * dm), lambda i, e: (mod_idx(i), 0, 0)),
            pl.BlockSpec((qd, dm), lambda i, e: (0, 0)),
            pl.BlockSpec((PEER_HEADS, 2, PEER_KEYS, PEER_QDIM // 2), lambda i, e: (0, 0, 0, 0)),
            pl.BlockSpec((PEER_HEADS, 2, PEER_KEYS, PEER_QDIM // 2), lambda i, e: (0, 0, 0, 0)),
            pl.BlockSpec((PEER_EB, dm), lambda i, e: (e, 0)),
            pl.BlockSpec((1, dm, PEER_EB), lambda i, e: (jnp.maximum(e - 1, 0), 0, 0)),
            pl.BlockSpec((1, dm, PEER_EB), lambda i, e: (n_eb - 1, 0, 0)),
            pl.BlockSpec((1, dm), lambda i, e: (0, 0)),
            pl.BlockSpec((1, dm), lambda i, e: (0, 0)),
        ],
        out_specs=pl.BlockSpec((tm, dm), lambda i, e: (i, 0)),
        out_shape=jax.ShapeDtypeStruct((n_tok, dm), f32),
        scratch_shapes=[
            pltpu.VMEM((dm, tm), bf16),
            pltpu.VMEM((2, PEER_KEYS, tm), f32),
            pltpu.VMEM((PEER_HEADS, PEER_KEYS, tm), f32),
            pltpu.VMEM((PEER_HEADS, PEER_KEYS, tm), f32),
            pltpu.VMEM((PEER_HEADS, PEER_KEYS, tm), bf16),
            pltpu.VMEM((PEER_HEADS, PEER_KEYS, tm), bf16),
            pltpu.VMEM((PEER_TOPK, tm), f32),
            pltpu.VMEM((PEER_TOPK, tm), f32),
            pltpu.VMEM((n_cand_pad, tm), f32),
            pltpu.VMEM((PEER_EB, tm), f32),
            pltpu.VMEM((2, PEER_EB, tm), bf16),
            pltpu.VMEM((dm, tm), f32),
        ],
        compiler_params=pltpu.CompilerParams(
            dimension_semantics=("arbitrary", "arbitrary"),
            vmem_limit_bytes=VMEM_LIMIT),
        name="peer_ln",
    )(xa, mod3, wq_bf, sk_hi, sk_lo, u_bf, vt_bf, vt_bf, ln_g.reshape(1, dm), ln_b.reshape(1, dm))


S5_CHUNK = 16
S5_LEVELS = 9


def _s5_prep(a_re, a_im, log_dt, b_re, b_im, c_re, c_im, d):
    f32, bf16 = jnp.float32, jnp.bfloat16
    hp = lax.Precision.HIGHEST
    n_g, n_p, q = S5_GROUPS, S5_STATE, S5_CHUNK
    dt = jnp.exp(log_dt.astype(f32))[..., None]

    def apow(n):
        mag = jnp.exp(a_re * dt * n)
        ang = a_im * dt * n
        return mag * jnp.cos(ang), mag * jnp.sin(ang)

    ar1, ai1 = apow(1.0)
    den = a_re * a_re + a_im * a_im
    nr, ni = ar1 - 1.0, ai1
    cr = (nr * a_re + ni * a_im) / den
    ci = (ni * a_re - nr * a_im) / den
    bbr = cr[..., None] * b_re - ci[..., None] * b_im
    bbi = cr[..., None] * b_im + ci[..., None] * b_re
    lag = jnp.arange(q + 1, dtype=f32)[:, None, None, None]
    pr, pi = apow(lag)
    mr = pr[..., None] * bbr - pi[..., None] * bbi
    mi = pr[..., None] * bbi + pi[..., None] * bbr
    kmat = (jnp.einsum('dgcp,ndgpk->ndgck', c_re, mr, precision=hp)
            - jnp.einsum('dgcp,ndgpk->ndgck', c_im, mi, precision=hp))
    r_idx = jnp.arange(q)[:, None]
    t_idx = jnp.arange(q)[None, :]

    def toeplitz(kd, lagm):
        blk = kd[jnp.clip(lagm, 0, q)]
        blk = jnp.where((lagm >= 0)[:, :, None, None, None], blk, 0.0)
        return blk.transpose(2, 0, 4, 1, 3).reshape(n_g, q * S5_GROUP, q * S5_GROUP)

    eye = jnp.eye(q * S5_GROUP, dtype=f32)
    dvec = jnp.tile(d.reshape(n_g, 1, S5_GROUP), (1, q, 1)).reshape(n_g, q * S5_GROUP)
    tsum = toeplitz(kmat[:, 0], t_idx - r_idx) + toeplitz(kmat[:, 1], r_idx - t_idx) + eye[None] * dvec[:, None, :]

    def w_in(direction, exps):
        wr = mr[exps, direction].transpose(1, 0, 3, 2).reshape(n_g, q * S5_GROUP, n_p)
        wi = mi[exps, direction].transpose(1, 0, 3, 2).reshape(n_g, q * S5_GROUP, n_p)
        return wr, wi

    def w_out(direction, exps):
        pre, pim = pr[exps, direction], pi[exps, direction]
        cre, cim = c_re[direction], c_im[direction]
        wre = cre[None] * pre[:, :, None, :] - cim[None] * pim[:, :, None, :]
        wim = -(cre[None] * pim[:, :, None, :] + cim[None] * pre[:, :, None, :])
        return (wre.transpose(1, 3, 0, 2).reshape(n_g, n_p, q * S5_GROUP),
                wim.transpose(1, 3, 0, 2).reshape(n_g, n_p, q * S5_GROUP))

    steps = jnp.arange(q)
    win = [w_in(0, q - 1 - steps), w_in(1, steps)]
    wout = [w_out(0, steps + 1), w_out(1, q - steps)]
    zc = jnp.zeros((n_g, q * S5_GROUP, n_p), f32)
    zr = jnp.zeros((n_g, n_p, q * S5_GROUP), f32)
    win_p, wout_p = [], []
    for direction in range(2):
        wr, wi = win[direction]
        even = jnp.concatenate([wr, zc, wi, zc], axis=2)
        odd = jnp.concatenate([zc, wr, zc, wi], axis=2)
        is_odd = (jnp.arange(n_g) % 2 == 1)[:, None, None]
        win_p.append(jnp.where(is_odd, odd, even))
        vr, vi = wout[direction]
        even = jnp.concatenate([vr, zr, vi, zr], axis=1)
        odd = jnp.concatenate([zr, vr, zr, vi], axis=1)
        wout_p.append(jnp.where(is_odd, odd, even))
    win_p = jnp.stack(win_p, axis=1).reshape(n_g // 2, 2, 2, 4 * n_p, 4 * n_p).transpose(0, 2, 1, 3, 4)
    wout_p = jnp.stack(wout_p, axis=1).reshape(n_g // 2, 2, 2, 4 * n_p, 4 * n_p).transpose(0, 2, 1, 3, 4)
    lvl = (q * 2.0 ** jnp.arange(S5_LEVELS, dtype=f32))[:, None, None, None]
    lr, li = apow(lvl)
    pw = jnp.stack([lr, li], axis=1)
    pw = pw.transpose(3, 2, 0, 1, 4).reshape(n_g // 2, 2, 2, S5_LEVELS, 2, n_p)
    pw = pw.transpose(0, 2, 3, 4, 1, 5).reshape(n_g // 2, 2, 2 * S5_LEVELS, 2 * n_p)
    return tsum.astype(bf16).reshape(n_g // 2, 2, q * S5_GROUP, q * S5_GROUP), win_p.astype(bf16), wout_p.astype(bf16), pw


def _shift_rows(v, r, up):
    n = v.shape[0]
    row = lax.broadcasted_iota(jnp.int32, v.shape, 0)
    if up:
        return jnp.where(row < n - r, pltpu.roll(v, n - r, axis=0), 0.0)
    return jnp.where(row >= r, pltpu.roll(v, r, axis=0), 0.0)


def _chunk_scan(re, im, pw, rows_per_chunk, up):
    n_chunks = re.shape[0] // rows_per_chunk
    level, s = 0, 1
    while s < n_chunks:
        ar, ai = pw[2 * level:2 * level + 1, :], pw[2 * level + 1:2 * level + 2, :]
        sre = _shift_rows(re, s * rows_per_chunk, up)
        sim = _shift_rows(im, s * rows_per_chunk, up)
        re, im = re + ar * sre - ai * sim, im + ar * sim + ai * sre
        level, s = level + 1, 2 * s
    return re, im


def _s5_kernel(u_ref, scl_ref, shl_ref, scc_ref, shc_ref, t_ref, win_ref, wout_ref, pw_ref, y_ref, *, bsz, ctx_chunks):
    f32, bf16 = jnp.float32, jnp.bfloat16
    n = u_ref.shape[1]
    width = u_ref.shape[2]
    rc = ctx_chunks * bsz
    hs = []
    for gi in range(2):
        xv = u_ref[gi].reshape(n // 8, 8, width)
        hl = xv * scl_ref[gi][None] + shl_ref[gi][None]
        hc = xv * scc_ref[gi][None] + shc_ref[gi][None]
        slab = lax.broadcasted_iota(jnp.int32, xv.shape, 0)
        hs.append(jnp.where(slab < rc // 8, hc, hl).reshape(n, width).astype(bf16))
    y = [jnp.dot(hs[gi], t_ref[0, gi], preferred_element_type=f32) for gi in range(2)]
    half = width // 2
    for direction in range(2):
        sloc = (jnp.dot(hs[0], win_ref[0, direction, 0], preferred_element_type=f32)
                + jnp.dot(hs[1], win_ref[0, direction, 1], preferred_element_type=f32))
        re, im = sloc[:, :half], sloc[:, half:]
        pw = pw_ref[0, direction]
        if direction == 0:
            sre, sim = _chunk_scan(_shift_rows(re, bsz, False), _shift_rows(im, bsz, False), pw, bsz, False)
        else:
            cre, cim = _chunk_scan(_shift_rows(re[:rc], bsz, True), _shift_rows(im[:rc], bsz, True), pw, bsz, True)
            ar, ai = pw[0:1, :], pw[1:2, :]
            fre = ar * cre[0:8] - ai * cim[0:8] + re[0:8]
            fim = ar * cim[0:8] + ai * cre[0:8] + im[0:8]
            row8 = lax.broadcasted_iota(jnp.int32, fre.shape, 0)
            tre = jnp.where(row8 >= 8 - bsz, pltpu.roll(fre, 8 - bsz, axis=0), 0.0)
            tim = jnp.where(row8 >= 8 - bsz, pltpu.roll(fim, 8 - bsz, axis=0), 0.0)
            lre, lim = _shift_rows(re[rc:], bsz, True), _shift_rows(im[rc:], bsz, True)
            lre = jnp.concatenate([lre[:-8], lre[-8:] + tre], axis=0)
            lim = jnp.concatenate([lim[:-8], lim[-8:] + tim], axis=0)
            lre, lim = _chunk_scan(lre, lim, pw, bsz, True)
            sre = jnp.concatenate([cre, lre], axis=0)
            sim = jnp.concatenate([cim, lim], axis=0)
        s_in = jnp.concatenate([sre, sim], axis=1).astype(bf16)
        for gi in range(2):
            y[gi] = y[gi] + jnp.dot(s_in, wout_ref[0, direction, gi], preferred_element_type=f32)
    for gi in range(2):
        y_ref[gi] = y[gi]


def s5_ssm_pallas(xa, mod_tab, n_lat, seq, bsz, prep):
    tsum, win_p, wout_p, pw = prep
    f32 = jnp.float32
    n_tok, dm = xa.shape
    ctx_len = (n_tok - n_lat) // bsz
    q = S5_CHUNK
    n_chunks = (seq + ctx_len) // q
    width = q * S5_GROUP
    full = jnp.concatenate([xa[n_lat:].reshape(bsz, ctx_len, dm), xa[:n_lat].reshape(bsz, seq, dm)], axis=1)
    u = full.reshape(bsz, n_chunks, q, S5_GROUPS, S5_GROUP).transpose(3, 1, 0, 2, 4).reshape(S5_GROUPS, n_chunks * bsz, width)

    def tile(vec_rows):
        rows = vec_rows.shape[0]
        t = vec_rows.reshape(rows, S5_GROUPS, 1, S5_GROUP)
        t = jnp.broadcast_to(t, (rows, S5_GROUPS, q, S5_GROUP)).reshape(rows, S5_GROUPS, width)
        return jnp.tile(t.transpose(1, 0, 2), (1, 8 // rows, 1))

    scl, shl = tile(1.0 + mod_tab[:bsz, dm:2 * dm]), tile(mod_tab[:bsz, 0:dm])
    scc = tile(jnp.broadcast_to(1.0 + mod_tab[bsz:bsz + 1, dm:2 * dm], (bsz, dm)))
    shc = tile(jnp.broadcast_to(mod_tab[bsz:bsz + 1, 0:dm], (bsz, dm)))
    n_rows = n_chunks * bsz
    tile_spec = pl.BlockSpec((2, 8, width), lambda p: (p, 0, 0))
    y = pl.pallas_call(
        functools.partial(_s5_kernel, bsz=bsz, ctx_chunks=ctx_len // q),
        grid=(S5_GROUPS // 2,),
        in_specs=[
            pl.BlockSpec((2, n_rows, width), lambda p: (p, 0, 0)),
            tile_spec, tile_spec, tile_spec, tile_spec,
            pl.BlockSpec((1, 2, width, width), lambda p: (p, 0, 0, 0)),
            pl.BlockSpec((1, 2, 2, width, width), lambda p: (p, 0, 0, 0, 0)),
            pl.BlockSpec((1, 2, 2, width, width), lambda p: (p, 0, 0, 0, 0)),
            pl.BlockSpec((1, 2, 2 * S5_LEVELS, 2 * S5_STATE), lambda p: (p, 0, 0, 0)),
        ],
        out_specs=pl.BlockSpec((2, n_rows, width), lambda p: (p, 0, 0)),
        out_shape=jax.ShapeDtypeStruct((S5_GROUPS, n_rows, width), f32),
        compiler_params=pltpu.CompilerParams(dimension_semantics=("arbitrary",), vmem_limit_bytes=VMEM_LIMIT),
        name="s5_ssm",
    )(u, scl, shl, scc, shc, tsum, win_p, wout_p, pw)
    yf = y.reshape(S5_GROUPS, n_chunks, bsz, q, S5_GROUP).transpose(2, 1, 3, 0, 4).reshape(bsz, seq + ctx_len, dm)
    return jnp.concatenate([yf[:, ctx_len:].reshape(n_lat, dm), yf[:, :ctx_len].reshape(bsz * ctx_len, dm)], axis=0)


def _glu_ln_kernel(y_ref, x_ref, mod_ref, wv_ref, wg_ref, lng_ref, lnb_ref, o_ref):
    f32 = jnp.float32
    dm = x_ref.shape[1]
    g = jax.nn.gelu(y_ref[...]).astype(jnp.bfloat16)
    val = jnp.dot(g, wv_ref[...], preferred_element_type=f32)
    gate = jnp.dot(g, wg_ref[...], preferred_element_type=f32)
    out = val * jax.nn.sigmoid(gate)
    mod = mod_ref[0]
    z = DEEPNORM_ALPHA * x_ref[...] + mod[:, 2 * dm:3 * dm] * out
    mu = jnp.mean(z, axis=-1, keepdims=True)
    zc = z - mu
    var = jnp.mean(zc * zc, axis=-1, keepdims=True)
    o_ref[...] = zc * lax.rsqrt(var + LN_EPS) * lng_ref[...] + lnb_ref[...]


TOK_TM = 512


def glu_ln_pallas(ya, xa, mod3, n_lat, seq, bsz, w_val, w_gate, ln_g, ln_b):
    n_tok, dm = ya.shape
    tm = TOK_TM
    bf16 = jnp.bfloat16
    mod_idx = functools.partial(_mod_row_index, tm=tm, n_lat=n_lat, seq=seq, bsz=bsz)
    return pl.pallas_call(
        _glu_ln_kernel,
        grid=(n_tok // tm,),
        in_specs=[
            pl.BlockSpec((tm, dm), lambda i: (i, 0)),
            pl.BlockSpec((tm, dm), lambda i: (i, 0)),
            pl.BlockSpec((1, 1, 6 * dm), lambda i: (mod_idx(i), 0, 0)),
            pl.BlockSpec((dm, dm), lambda i: (0, 0)),
            pl.BlockSpec((dm, dm), lambda i: (0, 0)),
            pl.BlockSpec((1, dm), lambda i: (0, 0)),
            pl.BlockSpec((1, dm), lambda i: (0, 0)),
        ],
        out_specs=pl.BlockSpec((tm, dm), lambda i: (i, 0)),
        out_shape=jax.ShapeDtypeStruct((n_tok, dm), jnp.float32),
        compiler_params=pltpu.CompilerParams(dimension_semantics=("arbitrary",), vmem_limit_bytes=VMEM_LIMIT),
        name="glu_ln",
    )(ya, xa, mod3, w_val.astype(bf16), w_gate.astype(bf16), ln_g.reshape(1, dm), ln_b.reshape(1, dm))


MLA_HD = 128
MLA_TQ = 256
MLA_HB = 4


def _mla_prep(w_down, w_uq, w_uk, w_uv):
    bf16 = jnp.bfloat16
    quarter = MLA_ROPE // 4
    swap = np.concatenate([np.arange(quarter, 2 * quarter), np.arange(0, quarter),
                           np.arange(3 * quarter, 4 * quarter), np.arange(2 * quarter, 3 * quarter)])
    dm = w_down.shape[0]
    w_cq = w_down[:, :MLA_Q_RANK]
    w_ckv = w_down[:, MLA_Q_RANK:MLA_Q_RANK + MLA_KV_RANK]
    w_kr = w_down[:, MLA_Q_RANK + MLA_KV_RANK:]
    wd_t = jnp.concatenate([w_cq, w_ckv, jnp.zeros((dm, MLA_NOPE), w_down.dtype), w_kr, w_kr[:, swap]], axis=1).T
    rope = w_uq[:, :, MLA_NOPE:]
    wuq_t = jnp.concatenate([w_uq, rope[:, :, swap]], axis=2).reshape(MLA_Q_RANK, MLA_HEADS * MLA_HD).T
    wuk = jnp.concatenate([w_uk, jnp.zeros((MLA_KV_RANK, MLA_HEADS, MLA_HD - MLA_NOPE), w_uk.dtype)], axis=2)
    wuk = wuk.reshape(MLA_KV_RANK, MLA_HEADS * MLA_HD)
    wuv_t = w_uv.reshape(MLA_KV_RANK, MLA_HEADS * MLA_V).T
    return wd_t.astype(bf16), wuq_t.astype(bf16), wuk.astype(bf16), wuv_t.astype(bf16)


def _rope_tables(seq, tm):
    quarter = MLA_ROPE // 4
    freqs = ROPE_BASE ** (-jnp.arange(quarter, dtype=jnp.float32) / quarter)
    t = jnp.arange(seq, dtype=jnp.float32)
    row, col = jnp.floor(t / GRID_W), t - GRID_W * jnp.floor(t / GRID_W)
    ang_r, ang_c = freqs[:, None] * row[None, :], freqs[:, None] * col[None, :]
    cos32 = jnp.concatenate([jnp.cos(ang_r), jnp.cos(ang_r), jnp.cos(ang_c), jnp.cos(ang_c)], axis=0)
    sin32 = jnp.concatenate([-jnp.sin(ang_r), jnp.sin(ang_r), -jnp.sin(ang_c), jnp.sin(ang_c)], axis=0)
    ones = jnp.ones((MLA_NOPE, seq + tm), jnp.float32)
    zeros = jnp.zeros((MLA_HD - MLA_NOPE - MLA_ROPE, seq + tm), jnp.float32)
    cos_t = jnp.concatenate([ones, jnp.concatenate([cos32, jnp.ones((MLA_ROPE, tm))], axis=1), zeros], axis=0)
    sin_t = jnp.concatenate([0.0 * ones, jnp.concatenate([sin32, jnp.zeros((MLA_ROPE, tm))], axis=1), zeros], axis=0)
    return cos_t, sin_t


def _mla_proj_kernel(x_ref, mod_ref, cos_ref, sin_ref, wd_ref, wuq_ref, wuk_ref, wuv_ref, qn_ref, kvn_ref,
                     qt_ref, k_ref, vt_ref):
    f32, bf16 = jnp.float32, jnp.bfloat16
    dm = x_ref.shape[1]
    mod = mod_ref[0]
    h = x_ref[...] * (1.0 + mod[:, dm:2 * dm]) + mod[:, 0:dm]
    ht = h.T.astype(bf16)
    dt = jnp.dot(wd_ref[...], ht, preferred_element_type=f32)
    cq, ckv, kr = dt[:MLA_Q_RANK], dt[MLA_Q_RANK:MLA_Q_RANK + MLA_KV_RANK], dt[MLA_Q_RANK + MLA_KV_RANK:]
    cqn = cq * lax.rsqrt(jnp.mean(cq * cq, axis=0, keepdims=True) + RMS_EPS) * qn_ref[...]
    ckvn = ckv * lax.rsqrt(jnp.mean(ckv * ckv, axis=0, keepdims=True) + RMS_EPS) * kvn_ref[...]
    cos_t, sin_t = cos_ref[...], sin_ref[...]

    def rope(v):
        shifted = jnp.concatenate([v[MLA_ROPE:], v[:MLA_ROPE]], axis=0)
        return v * cos_t + shifted * sin_t

    scale = (MLA_NOPE + MLA_ROPE) ** -0.5
    q_all = jnp.dot(wuq_ref[...], cqn.astype(bf16), preferred_element_type=f32)
    for hd in range(MLA_HEADS):
        qt_ref[hd] = (rope(q_all[hd * MLA_HD:(hd + 1) * MLA_HD]) * scale).astype(bf16)
    ckvn_bf = ckvn.astype(bf16)
    vt_ref[...] = jnp.dot(wuv_ref[...], ckvn_bf, preferred_element_type=f32).astype(bf16)
    k_all = jnp.dot(ckvn.T.astype(bf16), wuk_ref[...], preferred_element_type=f32)
    kr_rows = rope(kr).T
    for hd in range(MLA_HEADS):
        k_ref[hd] = (k_all[:, hd * MLA_HD:(hd + 1) * MLA_HD] + kr_rows).astype(bf16)


def _mla_attn_kernel(*refs, with_latent):
    f32, bf16 = jnp.float32, jnp.bfloat16
    if with_latent:
        q_ref, kl_ref, kc_ref, vl_ref, vc_ref, o_ref = refs
    else:
        q_ref, kc_ref, vc_ref, o_ref = refs
    for i in range(q_ref.shape[0]):
        rows = slice(i * MLA_V, (i + 1) * MLA_V)
        q = q_ref[i]
        s_c = jnp.dot(kc_ref[i], q, preferred_element_type=f32)
        m = jnp.max(s_c, axis=0, keepdims=True)
        if with_latent:
            s_l = jnp.dot(kl_ref[i], q, preferred_element_type=f32)
            m = jnp.maximum(m, jnp.max(s_l, axis=0, keepdims=True))
        p_c = jnp.exp(s_c - m)
        den = jnp.sum(p_c, axis=0, keepdims=True)
        o = jnp.dot(vc_ref[rows, :], p_c.astype(bf16), preferred_element_type=f32)
        if with_latent:
            p_l = jnp.exp(s_l - m)
            den = den + jnp.sum(p_l, axis=0, keepdims=True)
            o = o + jnp.dot(vl_ref[rows, :], p_l.astype(bf16), preferred_element_type=f32)
        o_ref[rows, :] = (o / den).astype(bf16)


def _mla_out_kernel(ot_ref, x_ref, mod_ref, wo_ref, lng_ref, lnb_ref, o_ref):
    f32 = jnp.float32
    dm = x_ref.shape[1]
    attn = ot_ref[...].astype(f32).T.astype(jnp.bfloat16)
    out = jnp.dot(attn, wo_ref[...], preferred_element_type=f32)
    mod = mod_ref[0]
    z = DEEPNORM_ALPHA * x_ref[...] + mod[:, 2 * dm:3 * dm] * out
    mu = jnp.mean(z, axis=-1, keepdims=True)
    zc = z - mu
    var = jnp.mean(zc * zc, axis=-1, keepdims=True)
    o_ref[...] = zc * lax.rsqrt(var + LN_EPS) * lng_ref[...] + lnb_ref[...]


def mla_ln_pallas(xa, mod3, n_lat, seq, bsz, w_down, q_norm, kv_norm, w_uq, w_uk, w_uv, w_o, ln_g, ln_b):
    f32, bf16 = jnp.float32, jnp.bfloat16
    n_tok, dm = xa.shape
    n_ctx = n_tok - n_lat
    ctx_len = n_ctx // bsz
    tm = TOK_TM
    wd_t, wuq_t, wuk, wuv_t = _mla_prep(w_down, w_uq, w_uk, w_uv)
    cos_t, sin_t = _rope_tables(seq, tm)
    mod_idx = functools.partial(_mod_row_index, tm=tm, n_lat=n_lat, seq=seq, bsz=bsz)
    tiles_per_sample = seq // tm
    pos_idx = lambda i: jnp.where(i * tm < n_lat, i % tiles_per_sample, tiles_per_sample)
    n_hd, n_dn = MLA_HEADS * MLA_HD, wd_t.shape[0]
    whole = lambda shape: pl.BlockSpec(shape, lambda i: (0,) * len(shape))
    params = pltpu.CompilerParams(dimension_semantics=("arbitrary",), vmem_limit_bytes=VMEM_LIMIT)
    qt, k, vt = pl.pallas_call(
        _mla_proj_kernel,
        grid=(n_tok // tm,),
        in_specs=[
            pl.BlockSpec((tm, dm), lambda i: (i, 0)),
            pl.BlockSpec((1, 1, 6 * dm), lambda i: (mod_idx(i), 0, 0)),
            pl.BlockSpec((MLA_HD, tm), lambda i: (0, pos_idx(i))),
            pl.BlockSpec((MLA_HD, tm), lambda i: (0, pos_idx(i))),
            whole((n_dn, dm)), whole((n_hd, MLA_Q_RANK)), whole((MLA_KV_RANK, n_hd)),
            whole((MLA_HEADS * MLA_V, MLA_KV_RANK)), whole((MLA_Q_RANK, 1)), whole((MLA_KV_RANK, 1)),
        ],
        out_specs=[
            pl.BlockSpec((MLA_HEADS, MLA_HD, tm), lambda i: (0, 0, i)),
            pl.BlockSpec((MLA_HEADS, tm, MLA_HD), lambda i: (0, i, 0)),
            pl.BlockSpec((MLA_HEADS * MLA_V, tm), lambda i: (0, i)),
        ],
        out_shape=[
            jax.ShapeDtypeStruct((MLA_HEADS, MLA_HD, n_tok), bf16),
            jax.ShapeDtypeStruct((MLA_HEADS, n_tok, MLA_HD), bf16),
            jax.ShapeDtypeStruct((MLA_HEADS * MLA_V, n_tok), bf16),
        ],
        compiler_params=params,
        name="mla_proj",
    )(xa, mod3, cos_t, sin_t, wd_t, wuq_t, wuk, wuv_t, q_norm.reshape(-1, 1), kv_norm.reshape(-1, 1))

    tq, hb = MLA_TQ, MLA_HB
    n_qt = seq // tq
    ctx_blk = n_lat // ctx_len
    params3 = pltpu.CompilerParams(dimension_semantics=("arbitrary",) * 3, vmem_limit_bytes=VMEM_LIMIT)
    ot_lat = pl.pallas_call(
        functools.partial(_mla_attn_kernel, with_latent=True),
        grid=(bsz, MLA_HEADS // hb, n_qt),
        in_specs=[
            pl.BlockSpec((hb, MLA_HD, tq), lambda b, hd, t: (hd, 0, b * n_qt + t)),
            pl.BlockSpec((hb, seq, MLA_HD), lambda b, hd, t: (hd, b, 0)),
            pl.BlockSpec((hb, ctx_len, MLA_HD), lambda b, hd, t: (hd, ctx_blk + b, 0)),
            pl.BlockSpec((hb * MLA_V, seq), lambda b, hd, t: (hd, b)),
            pl.BlockSpec((hb * MLA_V, ctx_len), lambda b, hd, t: (hd, ctx_blk + b)),
        ],
        out_specs=pl.BlockSpec((hb * MLA_V, tq), lambda b, hd, t: (hd, b * n_qt + t)),
        out_shape=jax.ShapeDtypeStruct((MLA_HEADS * MLA_V, n_lat), bf16),
        compiler_params=params3,
        name="mla_attn_latent",
    )(qt, k, k, vt, vt)
    ot_ctx = pl.pallas_call(
        functools.partial(_mla_attn_kernel, with_latent=False),
        grid=(bsz, MLA_HEADS // hb, 1),
        in_specs=[
            pl.BlockSpec((hb, MLA_HD, ctx_len), lambda b, hd, t: (hd, 0, ctx_blk + b)),
            pl.BlockSpec((hb, ctx_len, MLA_HD), lambda b, hd, t: (hd, ctx_blk + b, 0)),
            pl.BlockSpec((hb * MLA_V, ctx_len), lambda b, hd, t: (hd, ctx_blk + b)),
        ],
        out_specs=pl.BlockSpec((hb * MLA_V, ctx_len), lambda b, hd, t: (hd, b)),
        out_shape=jax.ShapeDtypeStruct((MLA_HEADS * MLA_V, n_ctx), bf16),
        compiler_params=params3,
        name="mla_attn_context",
    )(qt, k, vt)
    ot = jnp.concatenate([ot_lat, ot_ctx], axis=1)
    return pl.pallas_call(
        _mla_out_kernel,
        grid=(n_tok // tm,),
        in_specs=[
            pl.BlockSpec((MLA_HEADS * MLA_V, tm), lambda i: (0, i)),
            pl.BlockSpec((tm, dm), lambda i: (i, 0)),
            pl.BlockSpec((1, 1, 6 * dm), lambda i: (mod_idx(i), 0, 0)),
            whole((MLA_HEADS * MLA_V, dm)), whole((1, dm)), whole((1, dm)),
        ],
        out_specs=pl.BlockSpec((tm, dm), lambda i: (i, 0)),
        out_shape=jax.ShapeDtypeStruct((n_tok, dm), f32),
        compiler_params=params,
        name="mla_out_ln",
    )(ot, xa, mod3, w_o.astype(bf16), ln_g.reshape(1, dm), ln_b.reshape(1, dm))


SSD_DT_PAD = 128
SSD_HPG = SSD_HEADS // SSD_GROUPS
SSD_GW = SSD_HPG * SSD_HEADDIM


def _ssd_in_kernel(x_ref, mod_ref, w_ref, wdt_ref, z_ref, xbc_ref, dt_ref, dtt_ref):
    f32, bf16 = jnp.float32, jnp.bfloat16
    dm = x_ref.shape[1]
    mod = mod_ref[0]
    h = x_ref[...] * (1.0 + mod[:, dm:2 * dm]) + mod[:, 0:dm]
    proj = jnp.dot(h.astype(bf16), w_ref[...], preferred_element_type=f32)
    z_ref[...] = proj[:, :SSD_D_INNER]
    xbc_ref[...] = proj[:, SSD_D_INNER:SSD_D_INNER + SSD_CONV_DIM]
    dt_ref[...] = proj[:, SSD_D_INNER + SSD_CONV_DIM:]
    dtt_ref[...] = jnp.dot(wdt_ref[...], h.T.astype(bf16), preferred_element_type=f32)


def _softplus(v):
    return jnp.maximum(v, 0.0) + jnp.log(1.0 + jnp.exp(-jnp.abs(v)))


def _split3(v):
    f32, bf16 = jnp.float32, jnp.bfloat16
    hi = v.astype(bf16)
    r1 = v - hi.astype(f32)
    mid = r1.astype(bf16)
    lo = (r1 - mid.astype(f32)).astype(bf16)
    return jnp.concatenate([hi, mid, lo], axis=1)


def _ssd_direction(direction, xbc_ref, prev_ref, next_ref, dt_ref, dtt_ref, has_prev, has_next,
                   cw_ref, cb_ref, bias_r_ref, a_r_ref, bias_c_ref, a_c_ref, e_ref, dskip_ref, state_ref, y_ref):
    f32, bf16 = jnp.float32, jnp.bfloat16
    hp = lax.Precision.HIGHEST
    q = xbc_ref.shape[0]
    nh = SSD_HEADS
    xm = xbc_ref[...]
    row = lax.broadcasted_iota(jnp.int32, xm.shape, 0)
    before = jnp.where(has_prev, prev_ref[7:8, :], 0.0)
    after = jnp.where(has_next, next_ref[0:1, :], 0.0)
    x_dn = jnp.where(row == 0, before, pltpu.roll(xm, 1, axis=0))
    x_up = jnp.where(row == q - 1, after, pltpu.roll(xm, q - 1, axis=0))
    conv = cb_ref[...] + x_dn * cw_ref[0:1, :] + xm * cw_ref[1:2, :] + x_up * cw_ref[2:3, :]
    conv = conv * jax.nn.sigmoid(conv)
    xs = conv[:, :SSD_D_INNER]
    gn = SSD_GROUPS * SSD_STATE
    bm, cm = conv[:, SSD_D_INNER:SSD_D_INNER + gn], conv[:, SSD_D_INNER + gn:]
    hs = slice(direction * nh, (direction + 1) * nh)
    dt = _softplus(dt_ref[:, hs] + bias_r_ref[direction:direction + 1, :])
    dtt = _softplus(dtt_ref[hs, :] + bias_c_ref[direction])
    r_i = lax.broadcasted_iota(jnp.int32, (q, q), 0)
    c_i = lax.broadcasted_iota(jnp.int32, (q, q), 1)
    causal = (r_i >= c_i) if direction == 0 else (r_i <= c_i)
    tri = causal.astype(f32)
    tri_t = ((c_i >= r_i) if direction == 0 else (c_i <= r_i)).astype(f32)
    a_cum = jnp.dot(tri, dt * a_r_ref[direction:direction + 1, :], precision=hp, preferred_element_type=f32)
    a_cum_t = jnp.dot(dtt * a_c_ref[direction], tri_t, precision=hp, preferred_element_type=f32)
    last = q - 1 if direction == 0 else 0
    a_exp = jnp.dot(_split3(a_cum), e_ref[...], preferred_element_type=f32)
    dt_exp = jnp.dot(_split3(dt), e_ref[...], preferred_element_type=f32)
    xdt = xs * dt_exp
    a_tot = a_exp[last:last + 1, :]
    xw = (xdt * jnp.exp(a_tot - a_exp)).astype(bf16)
    grow = jnp.exp(a_exp)
    carry = jnp.exp(a_tot)
    xdt_bf = xdt.astype(bf16)
    for g in range(SSD_GROUPS):
        gl = slice(g * SSD_GW, (g + 1) * SSD_GW)
        nl = slice(g * SSD_STATE, (g + 1) * SSD_STATE)
        bm_g, cm_g = bm[:, nl], cm[:, nl].astype(bf16)
        prev = state_ref[direction, g]
        y_g = jnp.dot(cm_g, prev.astype(bf16), preferred_element_type=f32) * grow[:, gl]
        states = jnp.dot(bm_g.T.astype(bf16), xw[:, gl], preferred_element_type=f32)
        state_ref[direction, g] = prev * carry[:, gl] + states
        cb = lax.dot_general(cm_g, bm_g.astype(bf16), (((1,), (1,)), ((), ())), preferred_element_type=f32)
        parts = []
        for hh in range(SSD_HPG):
            hd = g * SSD_HPG + hh
            seg = jnp.broadcast_to(a_cum[:, hd:hd + 1], (q, q)) - a_cum_t[hd:hd + 1, :]
            lmat = (jnp.where(causal, jnp.exp(seg), 0.0) * cb).astype(bf16)
            parts.append(jnp.dot(lmat, xdt_bf[:, hd * SSD_HEADDIM:(hd + 1) * SSD_HEADDIM], preferred_element_type=f32))
        y_g = y_g + jnp.concatenate(parts, axis=1)
        if direction == 0:
            y_g = y_g + dskip_ref[:, gl] * xs[:, gl]
        y_ref[:, gl] = y_g


def _ssd_scan_kernel(xf_ref, pf_ref, nf_ref, dtf_ref, dttf_ref, xb_ref, pb_ref, nb_ref, dtb_ref, dttb_ref,
                     cw_ref, cb_ref, bias_r_ref, a_r_ref, bias_c_ref, a_c_ref, e_ref, dskip_ref,
                     yf_ref, yb_ref, state_ref, *, ctx_chunks, lat_chunks):
    j = pl.program_id(1)

    @pl.when(j == 0)
    def _reset():
        state_ref[...] = jnp.zeros(state_ref.shape, jnp.float32)

    in_ctx = j < ctx_chunks
    pos_f = jnp.where(in_ctx, j, j - ctx_chunks)
    seg_len = jnp.where(in_ctx, ctx_chunks, lat_chunks)
    pos_b = seg_len - 1 - pos_f
    shared = (cw_ref, cb_ref, bias_r_ref, a_r_ref, bias_c_ref, a_c_ref, e_ref, dskip_ref, state_ref)
    _ssd_direction(0, xf_ref, pf_ref, nf_ref, dtf_ref, dttf_ref, pos_f > 0, pos_f < seg_len - 1, *shared, yf_ref)
    _ssd_direction(1, xb_ref, pb_ref, nb_ref, dtb_ref, dttb_ref, pos_b > 0, pos_b < seg_len - 1, *shared, yb_ref)


def _ssd_out_kernel(yf_ref, yb_ref, z_ref, x_ref, mod_ref, nw_ref, wo_ref, lng_ref, lnb_ref, o_ref):
    f32 = jnp.float32
    dm = x_ref.shape[1]
    z = z_ref[...]
    y = (yf_ref[...] + yb_ref[...]) * (z * jax.nn.sigmoid(z))
    y = y * lax.rsqrt(jnp.mean(y * y, axis=-1, keepdims=True) + RMS_EPS) * nw_ref[...]
    out = jnp.dot(y.astype(jnp.bfloat16), wo_ref[...], preferred_element_type=f32)
    mod = mod_ref[0]
    v = DEEPNORM_ALPHA * x_ref[...] + mod[:, 2 * dm:3 * dm] * out
    mu = jnp.mean(v, axis=-1, keepdims=True)
    vc = v - mu
    var = jnp.mean(vc * vc, axis=-1, keepdims=True)
    o_ref[...] = vc * lax.rsqrt(var + LN_EPS) * lng_ref[...] + lnb_ref[...]


def ssd_ln_pallas(xa, mod3, n_lat, seq, bsz, w_in, conv_w, conv_b, dt_bias, a_log, d, norm_w, w_out, ln_g, ln_b):
    f32, bf16 = jnp.float32, jnp.bfloat16
    n_tok, dm = xa.shape
    ctx_len = (n_tok - n_lat) // bsz
    tm = TOK_TM
    q = SSD_CHUNK
    nh = SSD_HEADS
    pad = SSD_DT_PAD - 2 * nh
    w_ext = jnp.concatenate([w_in, jnp.zeros((dm, pad), w_in.dtype)], axis=1).astype(bf16)
    n_in = w_ext.shape[1]
    wdt_t = w_ext[:, SSD_D_INNER + SSD_CONV_DIM:].T
    mod_idx = functools.partial(_mod_row_index, tm=tm, n_lat=n_lat, seq=seq, bsz=bsz)
    whole = lambda shape: pl.BlockSpec(shape, lambda *_: (0,) * len(shape))
    params = pltpu.CompilerParams(dimension_semantics=("arbitrary",), vmem_limit_bytes=VMEM_LIMIT)
    z, xbc, dt, dtt = pl.pallas_call(
        _ssd_in_kernel,
        grid=(n_tok // tm,),
        in_specs=[
            pl.BlockSpec((tm, dm), lambda i: (i, 0)),
            pl.BlockSpec((1, 1, 6 * dm), lambda i: (mod_idx(i), 0, 0)),
            whole((dm, n_in)), whole((SSD_DT_PAD, dm)),
        ],
        out_specs=[
            pl.BlockSpec((tm, SSD_D_INNER), lambda i: (i, 0)),
            pl.BlockSpec((tm, SSD_CONV_DIM), lambda i: (i, 0)),
            pl.BlockSpec((tm, SSD_DT_PAD), lambda i: (i, 0)),
            pl.BlockSpec((SSD_DT_PAD, tm), lambda i: (0, i)),
        ],
        out_shape=[
            jax.ShapeDtypeStruct((n_tok, SSD_D_INNER), f32),
            jax.ShapeDtypeStruct((n_tok, SSD_CONV_DIM), f32),
            jax.ShapeDtypeStruct((n_tok, SSD_DT_PAD), f32),
            jax.ShapeDtypeStruct((SSD_DT_PAD, n_tok), f32),
        ],
        compiler_params=params,
        name="ssd_in_proj",
    )(xa, mod3, w_ext, wdt_t)

    ctx_chunks, lat_chunks = ctx_len // q, seq // q
    ctx_base = n_lat // q
    n_chunk_total = n_tok // q

    def chunk_of(b, j, backward):
        in_ctx = j < ctx_chunks
        pos = jnp.where(in_ctx, j, j - ctx_chunks)
        seg = jnp.where(in_ctx, ctx_chunks, lat_chunks)
        pos = jnp.where(backward, seg - 1 - pos, pos)
        return jnp.where(in_ctx, ctx_base + b * ctx_chunks + pos, b * lat_chunks + pos)

    sub = q // 8
    def dir_specs(backward):
        ch = lambda b, j: chunk_of(b, j, backward)
        return [
            pl.BlockSpec((q, SSD_CONV_DIM), lambda b, j: (ch(b, j), 0)),
            pl.BlockSpec((8, SSD_CONV_DIM), lambda b, j: (jnp.maximum(ch(b, j) * sub - 1, 0), 0)),
            pl.BlockSpec((8, SSD_CONV_DIM), lambda b, j: (jnp.minimum((ch(b, j) + 1) * sub, n_chunk_total * sub - 1), 0)),
            pl.BlockSpec((q, SSD_DT_PAD), lambda b, j: (ch(b, j), 0)),
            pl.BlockSpec((SSD_DT_PAD, q), lambda b, j: (0, ch(b, j))),
        ]

    a = -jnp.exp(a_log.astype(f32))
    expand = jnp.repeat(jnp.eye(nh, dtype=f32), SSD_HEADDIM, axis=1)
    e3 = jnp.concatenate([expand, expand, expand], axis=0).astype(bf16)
    dskip = jnp.repeat(d, SSD_HEADDIM)[None, :]
    yf, yb = pl.pallas_call(
        functools.partial(_ssd_scan_kernel, ctx_chunks=ctx_chunks, lat_chunks=lat_chunks),
        grid=(bsz, ctx_chunks + lat_chunks),
        in_specs=dir_specs(False) + dir_specs(True) + [
            whole((SSD_CONV, SSD_CONV_DIM)), whole((1, SSD_CONV_DIM)),
            whole((2, nh)), whole((2, nh)), whole((2, nh, 1)), whole((2, nh, 1)),
            whole((3 * nh, SSD_D_INNER)), whole((1, SSD_D_INNER)),
        ],
        out_specs=[
            pl.BlockSpec((q, SSD_D_INNER), lambda b, j: (chunk_of(b, j, False), 0)),
            pl.BlockSpec((q, SSD_D_INNER), lambda b, j: (chunk_of(b, j, True), 0)),
        ],
        out_shape=[jax.ShapeDtypeStruct((n_tok, SSD_D_INNER), f32)] * 2,
        scratch_shapes=[pltpu.VMEM((2, SSD_GROUPS, SSD_STATE, SSD_GW), f32)],
        compiler_params=pltpu.CompilerParams(dimension_semantics=("arbitrary", "arbitrary"), vmem_limit_bytes=VMEM_LIMIT),
        name="ssd_scan",
    )(xbc, xbc, xbc, dt, dtt, xbc, xbc, xbc, dt, dtt,
      conv_w, conv_b.reshape(1, -1), dt_bias, a, dt_bias.reshape(2, nh, 1), a.reshape(2, nh, 1), e3, dskip)

    return pl.pallas_call(
        _ssd_out_kernel,
        grid=(n_tok // tm,),
        in_specs=[
            pl.BlockSpec((tm, SSD_D_INNER), lambda i: (i, 0)),
            pl.BlockSpec((tm, SSD_D_INNER), lambda i: (i, 0)),
            pl.BlockSpec((tm, SSD_D_INNER), lambda i: (i, 0)),
            pl.BlockSpec((tm, dm), lambda i: (i, 0)),
            pl.BlockSpec((1, 1, 6 * dm), lambda i: (mod_idx(i), 0, 0)),
            whole((1, SSD_D_INNER)), whole((SSD_D_INNER, dm)), whole((1, dm)), whole((1, dm)),
        ],
        out_specs=pl.BlockSpec((tm, dm), lambda i: (i, 0)),
        out_shape=jax.ShapeDtypeStruct((n_tok, dm), f32),
        compiler_params=params,
        name="ssd_out_ln",
    )(yf, yb, z, xa, mod3, norm_w.reshape(1, -1), w_out.astype(bf16), ln_g.reshape(1, dm), ln_b.reshape(1, dm))


def kernel(x, c, ctx, c_ctx, mod_w, mod_b, ln_g, ln_b,
           s5_a_re, s5_a_im, s5_log_dt, s5_b_re, s5_b_im, s5_c_re, s5_c_im, s5_d, s5_w_gate, s5_w_val,
           ssd_w_in, ssd_conv_w, ssd_conv_b, ssd_dt_bias, ssd_a_log, ssd_d, ssd_norm_w, ssd_w_out,
           mla_w_down, mla_q_norm, mla_kv_norm, mla_w_uq, mla_w_uk, mla_w_uv, mla_w_o,
           peer_w_q, peer_subkeys, peer_u, peer_v):
    ROWS = x.shape[1] // GRID_W
    pos = grid_positions(ROWS)
    ctx_len = ctx.shape[1]
    c_act = jax.nn.silu(c)
    c_ctx_act = jax.nn.silu(c_ctx)
    bsz, seq_len, dm = x.shape
    n_lat, n_ctx = bsz * seq_len, bsz * ctx_len
    xa = jnp.concatenate([x.reshape(n_lat, dm), ctx.reshape(n_ctx, dm)], axis=0)
    for i in range(DEPTH):
        last = i == DEPTH - 1
        mod_tab = jnp.concatenate([c_act, c_ctx_act[None], jnp.zeros((7 - bsz, dm), x.dtype)], axis=0) @ mod_w[i] + mod_b[i]
        mod3 = mod_tab.reshape(8, 1, 6 * dm)
        kind, j = i % N_MIXERS, i // N_MIXERS
        if kind == 0:
            prep = _s5_prep(s5_a_re[j], s5_a_im[j], s5_log_dt[j], s5_b_re[j], s5_b_im[j], s5_c_re[j], s5_c_im[j], s5_d[j])
            ya = s5_ssm_pallas(xa, mod_tab, n_lat, seq_len, bsz, prep)
            if last:
                ya, xa = ya[:n_lat], xa[:n_lat]
            xa = glu_ln_pallas(ya, xa, mod3, n_lat, seq_len, bsz, s5_w_val[j], s5_w_gate[j], ln_g[i, 0], ln_b[i, 0])
        elif kind == 1:
            xa = ssd_ln_pallas(xa, mod3, n_lat, seq_len, bsz, ssd_w_in[j], ssd_conv_w[j], ssd_conv_b[j], ssd_dt_bias[j],
                               ssd_a_log[j], ssd_d[j], ssd_norm_w[j], ssd_w_out[j], ln_g[i, 0], ln_b[i, 0])
            if last:
                xa = xa[:n_lat]
        else:
            xa = mla_ln_pallas(xa, mod3, n_lat, seq_len, bsz, mla_w_down[j], mla_q_norm[j], mla_kv_norm[j],
                               mla_w_uq[j], mla_w_uk[j], mla_w_uv[j], mla_w_o[j], ln_g[i, 0], ln_b[i, 0])
            if last:
                xa = xa[:n_lat]
        xa = peer_ln_pallas(xa, mod3, n_lat, seq_len, bsz, peer_w_q[i], peer_subkeys[i], peer_u[i], peer_v[i],
                            ln_g[i, 1], ln_b[i, 1])
    return xa[:n_lat].reshape(bsz, seq_len, dm)
```

```python
import math
import functools
import jax
import jax.numpy as jnp
from jax import lax
import numpy as np
from jax.experimental import pallas as pl
from jax.experimental.pallas import tpu as pltpu

D_MODEL = 1024
BATCH = 4
SEQ = 4096
DEPTH = 4

GRID_W = 64
CTX_LEN = 256
N_MIXERS = 3
DEEPNORM_ALPHA = (2.0 * DEPTH) ** 0.25
LN_EPS = 1e-5
RMS_EPS = 1e-6
ROPE_BASE = 10000.0

S5_GROUP = 16
S5_GROUPS = D_MODEL // S5_GROUP
S5_STATE = 64

SSD_D_INNER = 2 * D_MODEL
SSD_HEADDIM = 64
SSD_HEADS = SSD_D_INNER // SSD_HEADDIM
SSD_GROUPS = 4
SSD_STATE = 128
SSD_CONV = 3
SSD_CHUNK = 128
SSD_CONV_DIM = SSD_D_INNER + 2 * SSD_GROUPS * SSD_STATE
SSD_IN_DIM = SSD_D_INNER + SSD_CONV_DIM + 2 * SSD_HEADS

MLA_HEADS = 16
MLA_Q_RANK = 256
MLA_KV_RANK = 128
MLA_NOPE = 64
MLA_ROPE = 32
MLA_V = 64
MLA_BLOCK = 128

PEER_HEADS = 8
PEER_KEYS = 128
PEER_EXPERTS = PEER_KEYS * PEER_KEYS
PEER_QDIM = 256
PEER_TOPK = 16
PEER_BLOCK = 128


def layer_norm(x, g, b):
    xf = x.astype(jnp.float32)
    mu = jnp.mean(xf, axis=-1, keepdims=True)
    var = jnp.mean(jnp.square(xf - mu), axis=-1, keepdims=True)
    return ((xf - mu) * lax.rsqrt(var + LN_EPS)).astype(x.dtype) * g + b


def _ln_kernel(x_ref, g_ref, b_ref, o_ref):
    xf = x_ref[...]
    mu = jnp.mean(xf, axis=-1, keepdims=True)
    xc = xf - mu
    var = jnp.mean(xc * xc, axis=-1, keepdims=True)
    o_ref[...] = xc * lax.rsqrt(var + LN_EPS) * g_ref[...] + b_ref[...]


def layer_norm_pallas(x, g, b):
    shp = x.shape
    x2 = x.reshape(-1, shp[-1])
    n, d = x2.shape
    tb = 512
    out = pl.pallas_call(
        _ln_kernel,
        grid=(n // tb,),
        in_specs=[pl.BlockSpec((tb, d), lambda i: (i, 0)),
                  pl.BlockSpec((1, d), lambda i: (0, 0)),
                  pl.BlockSpec((1, d), lambda i: (0, 0))],
        out_specs=pl.BlockSpec((tb, d), lambda i: (i, 0)),
        out_shape=jax.ShapeDtypeStruct((n, d), jnp.float32),
        name="final_ln",
    )(x2, g.reshape(1, d), b.reshape(1, d))
    return out.reshape(shp)


def rms_norm(x, g):
    xf = x.astype(jnp.float32)
    return (xf * lax.rsqrt(jnp.mean(jnp.square(xf), axis=-1, keepdims=True) + RMS_EPS)).astype(x.dtype) * g


def modulate(x, shift, scale):
    return x * (1.0 + scale) + shift


def grid_positions(rows):
    row = jnp.repeat(jnp.arange(rows, dtype=jnp.float32), GRID_W)
    col = jnp.tile(jnp.arange(GRID_W, dtype=jnp.float32), rows)
    return row, col


def rope_axial(x, row, col):
    half = x.shape[-1] // 2
    quarter = half // 2
    freqs = ROPE_BASE ** (-jnp.arange(quarter, dtype=jnp.float32) / quarter)

    def rot(xp, pos):
        ang = pos[:, None] * freqs
        cos = jnp.cos(ang)[None, :, None, :].astype(x.dtype)
        sin = jnp.sin(ang)[None, :, None, :].astype(x.dtype)
        x1, x2 = xp[..., :quarter], xp[..., quarter:]
        return jnp.concatenate([x1 * cos - x2 * sin, x2 * cos + x1 * sin], axis=-1)

    return jnp.concatenate([rot(x[..., :half], row), rot(x[..., half:], col)], axis=-1)


def _lin_rec(left, right):
    a1, b1 = left
    a2, b2 = right
    return a1 * a2, a2 * b1 + b2


def s5_scan(u, a_bar, b_bar, c_mat, init, reverse):
    bu = jnp.einsum('lgc,gpc->lgp', u.astype(jnp.float32), b_bar)
    a = jnp.broadcast_to(a_bar, bu.shape)
    a_cum, s = lax.associative_scan(_lin_rec, (a, bu), reverse=reverse, axis=0)
    s = s + a_cum * init
    y = jnp.einsum('lgp,gcp->lgc', s, c_mat).real
    final = s[0] if reverse else s[-1]
    return y, final


def s5_direction(u_ctx, u_lat, a_re, a_im, log_dt, b_re, b_im, c_re, c_im, reverse):
    lam = lax.complex(a_re.astype(jnp.float32), a_im.astype(jnp.float32))
    a_bar = jnp.exp(lam * jnp.exp(log_dt.astype(jnp.float32))[:, None])
    b_mat = lax.complex(b_re.astype(jnp.float32), b_im.astype(jnp.float32))
    b_bar = ((a_bar - 1.0) / lam)[..., None] * b_mat
    c_mat = lax.complex(c_re.astype(jnp.float32), c_im.astype(jnp.float32))

    def per_sample(args):
        uc, ul = args
        yc, sc = s5_scan(uc, a_bar, b_bar, c_mat, jnp.zeros_like(a_bar), reverse)
        yl, _ = s5_scan(ul, a_bar, b_bar, c_mat, sc, reverse)
        return yc, yl

    return lax.map(per_sample, (u_ctx, u_lat))


def s5_mixer(h, hc, a_re, a_im, log_dt, b_re, b_im, c_re, c_im, d, w_gate, w_val):
    bsz, seq_len, _ = h.shape
    ctx_len = hc.shape[1]
    u = h.reshape(bsz, seq_len, S5_GROUPS, S5_GROUP)
    uc = hc.reshape(bsz, ctx_len, S5_GROUPS, S5_GROUP)
    y_l = d * h
    y_c = d * hc
    for direction in range(2):
        yc, yl = s5_direction(uc, u, a_re[direction], a_im[direction], log_dt[direction],
                              b_re[direction], b_im[direction], c_re[direction], c_im[direction],
                              direction == 1)
        y_l = y_l + yl.reshape(bsz, seq_len, D_MODEL).astype(h.dtype)
        y_c = y_c + yc.reshape(bsz, ctx_len, D_MODEL).astype(h.dtype)

    def glu(y):
        g = jax.nn.gelu(y)
        return (g @ w_val) * jax.nn.sigmoid(g @ w_gate)

    return glu(y_l), glu(y_c)


def depthwise_conv_centred(x, w, b):
    k_w = w.shape[0]
    pad = k_w // 2
    seq_len = x.shape[1]
    xp = jnp.pad(x, ((0, 0), (pad, pad), (0, 0)))
    out = b
    for k in range(k_w):
        out = out + xp[:, k:k + seq_len] * w[k]
    return out


def ssd_scan(x, dt, a, bm, cm, init):
    bsz, seq_len, n_heads, p_dim = x.shape
    n_grp, n_st = bm.shape[2], bm.shape[3]
    hg = n_heads // n_grp
    q_len = SSD_CHUNK
    nc = seq_len // q_len
    xc = x.reshape(bsz, nc, q_len, n_grp, hg, p_dim)
    dtc = dt.reshape(bsz, nc, q_len, n_grp, hg).astype(jnp.float32)
    bc = bm.reshape(bsz, nc, q_len, n_grp, n_st)
    cc = cm.reshape(bsz, nc, q_len, n_grp, n_st)
    a_cum = jnp.cumsum(dtc * a.reshape(n_grp, hg), axis=2)
    xdt = xc * dtc[..., None].astype(x.dtype)
    tri = jnp.tril(jnp.ones((q_len, q_len), dtype=bool))
    seg = a_cum[:, :, :, None] - a_cum[:, :, None]
    decay = jnp.exp(jnp.where(tri[:, :, None, None], seg, -jnp.inf)).astype(x.dtype)
    cb = jnp.einsum('bcqgn,bcsgn->bcgqs', cc, bc)
    y_diag = jnp.einsum('bcgqs,bcqsgh,bcsghp->bcqghp', cb, decay, xdt)
    decay_states = jnp.exp(a_cum[:, :, -1:] - a_cum).astype(x.dtype)
    states = jnp.einsum('bcsgn,bcsgh,bcsghp->bcghpn', bc, decay_states, xdt)
    chunk_decay = jnp.exp(a_cum[:, :, -1]).astype(x.dtype)

    def step(carry, inp):
        dec, st = inp
        return carry * dec[..., None, None] + st, carry

    final, prev = lax.scan(step, init, (jnp.moveaxis(chunk_decay, 1, 0), jnp.moveaxis(states, 1, 0)))
    y_off = jnp.einsum('bcqgn,cbghpn,bcqgh->bcqghp', cc, prev, jnp.exp(a_cum).astype(x.dtype))
    return (y_diag + y_off).reshape(bsz, seq_len, n_heads, p_dim), final


def ssd_mixer(h, hc, w_in, conv_w, conv_b, dt_bias, a_log, d, norm_w, w_out):
    a = -jnp.exp(a_log.astype(jnp.float32))

    def flip(t):
        return jnp.flip(t, axis=1)

    def bidir(t, inits):
        bsz, seq_len = t.shape[0], t.shape[1]
        z, xbc, dt = jnp.split(t @ w_in, [SSD_D_INNER, SSD_D_INNER + SSD_CONV_DIM], axis=-1)
        xbc = jax.nn.silu(depthwise_conv_centred(xbc, conv_w, conv_b))
        xs, bm, cm = jnp.split(xbc, [SSD_D_INNER, SSD_D_INNER + SSD_GROUPS * SSD_STATE], axis=-1)
        xs = xs.reshape(bsz, seq_len, SSD_HEADS, SSD_HEADDIM)
        bm = bm.reshape(bsz, seq_len, SSD_GROUPS, SSD_STATE)
        cm = cm.reshape(bsz, seq_len, SSD_GROUPS, SSD_STATE)
        dt = jax.nn.softplus(dt.reshape(bsz, seq_len, 2, SSD_HEADS) + dt_bias)
        y_f, s_f = ssd_scan(xs, dt[:, :, 0], a[0], bm, cm, inits[0])
        y_b, s_b = ssd_scan(flip(xs), flip(dt[:, :, 1]), a[1], flip(bm), flip(cm), inits[1])
        y = y_f + flip(y_b) + d[:, None] * xs
        y = rms_norm(y.reshape(z.shape) * jax.nn.silu(z), norm_w)
        return y @ w_out, (s_f, s_b)

    zero = jnp.zeros((h.shape[0], SSD_GROUPS, SSD_HEADS // SSD_GROUPS, SSD_HEADDIM, SSD_STATE), h.dtype)
    out_c, states_c = bidir(hc, (zero, zero))
    out_l, _ = bidir(h, states_c)
    return out_l, out_c


def mla_project(t, w_down, q_norm, kv_norm, w_uq, w_uk, w_uv, pos):
    cq, ckv, kr = jnp.split(t @ w_down, [MLA_Q_RANK, MLA_Q_RANK + MLA_KV_RANK], axis=-1)
    q = jnp.einsum('btr,rhd->bthd', rms_norm(cq, q_norm), w_uq)
    ckv = rms_norm(ckv, kv_norm)
    k_nope = jnp.einsum('btr,rhd->bthd', ckv, w_uk)
    v = jnp.einsum('btr,rhd->bthd', ckv, w_uv)
    q_nope, q_rope = q[..., :MLA_NOPE], q[..., MLA_NOPE:]
    kr = kr[:, :, None, :]
    if pos is not None:
        q_rope = rope_axial(q_rope, pos[0], pos[1])
        kr = rope_axial(kr, pos[0], pos[1])
    k = jnp.concatenate([k_nope, jnp.broadcast_to(kr, k_nope.shape[:3] + (MLA_ROPE,))], axis=-1)
    q = jnp.concatenate([q_nope, q_rope], axis=-1)
    return q, k, v


def attend(q, k, v):
    s = jnp.einsum('bqhd,bkhd->bhqk', q, k).astype(jnp.float32) * (MLA_NOPE + MLA_ROPE) ** -0.5
    p = jax.nn.softmax(s, axis=-1).astype(v.dtype)
    return jnp.einsum('bhqk,bkhd->bqhd', p, v)


def mla_mixer(h, hc, pos, w_down, q_norm, kv_norm, w_uq, w_uk, w_uv, w_o):
    bsz, seq_len, _ = h.shape
    qc, kc, vc = mla_project(hc, w_down, q_norm, kv_norm, w_uq, w_uk, w_uv, None)
    ql, kl, vl = mla_project(h, w_down, q_norm, kv_norm, w_uq, w_uk, w_uv, pos)
    out_c = attend(qc, kc, vc).reshape(bsz, hc.shape[1], MLA_HEADS * MLA_V)
    k_all = jnp.concatenate([kc, kl], axis=1)
    v_all = jnp.concatenate([vc, vl], axis=1)
    nb = seq_len // MLA_BLOCK
    qb = jnp.moveaxis(ql.reshape(bsz, nb, MLA_BLOCK, MLA_HEADS, MLA_NOPE + MLA_ROPE), 1, 0)
    out_l = lax.map(lambda qblk: attend(qblk, k_all, v_all), qb)
    out_l = jnp.moveaxis(out_l, 0, 1).reshape(bsz, seq_len, MLA_HEADS * MLA_V)
    return out_l @ w_o, out_c @ w_o


def peer_ffn(h, w_q, subkeys, u_tab, v_tab):
    bsz, seq_len, dm = h.shape
    half = PEER_QDIM // 2
    q = jnp.einsum('btd,dhk->bthk', h, w_q)
    s1 = jnp.einsum('bthk,hnk->bthn', q[..., :half], subkeys[:, 0])
    s2 = jnp.einsum('bthk,hnk->bthn', q[..., half:], subkeys[:, 1])
    v1, i1 = lax.top_k(s1, PEER_TOPK)
    v2, i2 = lax.top_k(s2, PEER_TOPK)
    n_cand = PEER_TOPK * PEER_TOPK
    cand = (v1[..., :, None] + v2[..., None, :]).reshape(bsz, seq_len, PEER_HEADS, n_cand)
    cidx = (i1[..., :, None] * PEER_KEYS + i2[..., None, :]).reshape(bsz, seq_len, PEER_HEADS, n_cand)
    best, sel = lax.top_k(cand, PEER_TOPK)
    eidx = jnp.take_along_axis(cidx, sel, axis=-1)
    gate = jax.nn.softmax(best.astype(jnp.float32), axis=-1).astype(h.dtype)
    n_blk = (bsz * seq_len) // PEER_BLOCK
    n_sel = PEER_HEADS * PEER_TOPK
    hb = h.reshape(n_blk, PEER_BLOCK, dm)
    ib = eidx.reshape(n_blk, PEER_BLOCK, n_sel)
    gb = gate.reshape(n_blk, PEER_BLOCK, n_sel)

    def block(args):
        hk, ik, gk = args
        u = jnp.take(u_tab, ik, axis=0)
        act = jax.nn.gelu(jnp.einsum('td,ted->te', hk, u))
        v = jnp.take(v_tab, ik, axis=0)
        return jnp.einsum('te,ted->td', gk * act, v)

    return lax.map(block, (hb, ib, gb)).reshape(bsz, seq_len, dm)


PEER_TM = 512
PEER_EB = 1024
PEER_ROWS = 16
PEER_LANES = 128
PEER_ACT_ROWS = 256
VMEM_LIMIT = 56 * 1024 * 1024
NEG_INF = float("-inf")
GELU_C0 = math.sqrt(2.0 / math.pi)
GELU_C1 = 0.044715 * GELU_C0


def _top16_sorted(s):
    n = PEER_TOPK
    m = s.shape[0] // 8
    x = [s[8 * r:8 * r + 8, :] for r in range(m)]

    def exchange(i, l, descending):
        hi, lo = jnp.maximum(x[i], x[l]), jnp.minimum(x[i], x[l])
        x[i], x[l] = (hi, lo) if descending else (lo, hi)

    def merge_bitonic(size):
        j = size // 2
        while j >= 1:
            for i in range(size):
                if i ^ j > i:
                    exchange(i, i ^ j, True)
            j //= 2

    k = 2
    while k < m:
        j = k // 2
        while j >= 1:
            for i in range(m):
                if i ^ j > i:
                    exchange(i, i ^ j, (i & k) == 0)
            j //= 2
        k *= 2
    merge_bitonic(m)
    shifts = [4, 2, 1]
    if m < n:
        first = shifts.pop(0)
        x = x + [pltpu.roll(x[m - 1 - r], first, axis=0) for r in range(m)]
        merge_bitonic(n)
    for shift in shifts:
        other = [pltpu.roll(x[n - 1 - r], shift, axis=0) for r in range(n)]
        for r in range(n):
            x[r] = jnp.maximum(x[r], other[r])
        merge_bitonic(n)
    return [x[r][0:1, :] for r in range(n)]


def _dot3(a_hi, a_lo, b_hi, b_lo):
    f32 = jnp.float32
    return (jnp.dot(a_hi, b_hi, preferred_element_type=f32)
            + jnp.dot(a_hi, b_lo, preferred_element_type=f32)
            + jnp.dot(a_lo, b_hi, preferred_element_type=f32))


def _split_bf16(v):
    hi = v.astype(jnp.bfloat16)
    lo = (v - hi.astype(jnp.float32)).astype(jnp.bfloat16)
    return hi, lo


def _peer_kernel(x_ref, mod_ref, wq_ref, skh_ref, skl_ref, u0_ref, u_ref, vt_ref, vtl_ref, lng_ref, lnb_ref,
                 o_ref,
                 ht_ref, s_ref, n1_ref, e1_ref, r2_ref, e2_ref, v1_ref, v2_ref, cand_ref,
                 act_ref, gact_ref, acc_ref):
    f32 = jnp.float32
    bf16 = jnp.bfloat16
    eb = pl.program_id(1)
    n_eb = pl.num_programs(1)
    dm = x_ref.shape[1]
    a_per = PEER_EB // PEER_KEYS
    half = PEER_QDIM // 2

    @pl.when(eb == 0)
    def _prologue():
        mod = mod_ref[0]
        h = x_ref[...] * (1.0 + mod[:, 4 * dm:5 * dm]) + mod[:, 3 * dm:4 * dm]
        ht = h.T.astype(bf16)
        ht_ref[...] = ht
        qt = jnp.dot(wq_ref[...], ht, preferred_element_type=f32)
        cand_ref[...] = jnp.full(cand_ref.shape, NEG_INF, f32)
        gact_ref[...] = jnp.zeros(gact_ref.shape, bf16)
        for hd in range(PEER_HEADS):
            q1_hi, q1_lo = _split_bf16(qt[hd * PEER_QDIM: hd * PEER_QDIM + half])
            q2_hi, q2_lo = _split_bf16(qt[hd * PEER_QDIM + half: (hd + 1) * PEER_QDIM])
            s_ref[0] = _dot3(skh_ref[hd, 0], skl_ref[hd, 0], q1_hi, q1_lo)
            s_ref[1] = _dot3(skh_ref[hd, 1], skl_ref[hd, 1], q2_hi, q2_lo)
            for lg in range(0, x_ref.shape[0], PEER_LANES):
                lanes = slice(lg, lg + PEER_LANES)
                s1, s2 = s_ref[0, :, lanes], s_ref[1, :, lanes]
                top1 = _top16_sorted(s1)
                top2 = _top16_sorted(s2)
                for k in range(PEER_TOPK):
                    v1_ref[k:k + 1, lanes] = top1[k]
                    v2_ref[k:k + 1, lanes] = top2[k]
                off = 0
                for i in range(PEER_TOPK):
                    cnt = PEER_TOPK // (i + 1)
                    cand_ref[off:off + cnt, lanes] = v1_ref[i:i + 1, lanes] + v2_ref[0:cnt, lanes]
                    off += cnt
                best = _top16_sorted(cand_ref[:, lanes])
                z = jnp.zeros_like(best[0])
                for k in range(PEER_TOPK):
                    z = z + jnp.exp(best[k] - best[0])
                tau = best[PEER_TOPK - 1]
                n1 = jnp.zeros(s1.shape, f32)
                for k in range(PEER_TOPK):
                    n1 = jnp.where(s1 + top2[k] >= tau, k + 1.0, n1)
                n1_ref[hd, :, lanes] = n1
                e1_ref[hd, :, lanes] = jnp.exp(s1 - top1[0])
                rank2 = jnp.zeros(s2.shape, f32)
                for k in range(PEER_TOPK):
                    rank2 = jnp.where(top2[k] > s2, k + 1.0, rank2)
                r2_ref[hd, :, lanes] = rank2.astype(bf16)
                e2_ref[hd, :, lanes] = (jnp.exp(s2 - top2[0]) * (0.5 / z)).astype(bf16)
        acc_ref[...] = jnp.zeros(acc_ref.shape, f32)
        act_ref[0] = jnp.dot(u0_ref[...], ht_ref[...], preferred_element_type=f32)

    par = eb % 2
    acc_ref[...] += jnp.dot(vt_ref[0], gact_ref[1 - par], preferred_element_type=f32)
    a_base = pl.multiple_of(eb * a_per, a_per)
    blk = (PEER_ROWS, x_ref.shape[0])
    n_slab = PEER_KEYS // PEER_ROWS
    a_per_act = PEER_ACT_ROWS // PEER_KEYS
    for a in range(a_per):
        if a % a_per_act == 0:
            r0 = (a // a_per_act) * PEER_ACT_ROWS
            act_ref[1 - par, r0:r0 + PEER_ACT_ROWS, :] = jnp.dot(u_ref[r0:r0 + PEER_ACT_ROWS, :], ht_ref[...],
                                                                 preferred_element_type=f32)
        g = [None] * n_slab
        for hd in range(PEER_HEADS):
            n1a = jnp.broadcast_to(n1_ref[hd, pl.ds(a_base, a_per), :][a:a + 1, :].astype(bf16), blk)
            e1a = jnp.broadcast_to(e1_ref[hd, pl.ds(a_base, a_per), :][a:a + 1, :].astype(bf16), blk)
            for i in range(n_slab):
                rs = slice(i * PEER_ROWS, (i + 1) * PEER_ROWS)
                w = jnp.where(r2_ref[hd, rs, :] < n1a, e2_ref[hd, rs, :], jnp.zeros(blk, bf16)) * e1a
                g[i] = w if g[i] is None else g[i] + w
        for i in range(n_slab):
            rows = slice(a * PEER_KEYS + i * PEER_ROWS, a * PEER_KEYS + (i + 1) * PEER_ROWS)
            x = act_ref[par, rows, :]
            t = jnp.tanh(x * (GELU_C0 + GELU_C1 * (x * x)))
            gact_ref[par, rows, :] = g[i] * (x + x * t).astype(bf16)

    @pl.when(eb == n_eb - 1)
    def _epilogue():
        mod = mod_ref[0]
        ffn_t = acc_ref[...] + jnp.dot(vtl_ref[0], gact_ref[par], preferred_element_type=f32)
        y = DEEPNORM_ALPHA * x_ref[...] + mod[:, 5 * dm:6 * dm] * ffn_t.T
        mu = jnp.mean(y, axis=-1, keepdims=True)
        yc = y - mu
        var = jnp.mean(yc * yc, axis=-1, keepdims=True)
        o_ref[...] = yc * lax.rsqrt(var + LN_EPS) * lng_ref[...] + lnb_ref[...]


def _mod_row_index(i, tm, n_lat, seq, bsz):
    return jnp.where(i * tm < n_lat, (i * tm) // seq, bsz)


def peer_ln_pallas(xa, mod3, n_lat, seq, bsz, w_q, subkeys, u_tab, v_tab, ln_g, ln_b):
    n_tok, dm = xa.shape
    tm = PEER_TM
    f32, bf16 = jnp.float32, jnp.bfloat16
    wq_t = w_q.reshape(dm, PEER_HEADS * PEER_QDIM).T
    wq_bf = wq_t.astype(bf16)
    sk_hi = subkeys.astype(bf16)
    sk_lo = (subkeys - sk_hi.astype(f32)).astype(bf16)
    u_bf = u_tab.astype(bf16)
    n_eb = PEER_EXPERTS // PEER_EB
    vt_bf = v_tab.reshape(n_eb, PEER_EB, dm).transpose(0, 2, 1).astype(bf16)
    qd = PEER_HEADS * PEER_QDIM
    n_cand = sum(PEER_TOPK // (i + 1) for i in range(PEER_TOPK))
    n_cand_pad = 64
    assert n_cand <= n_cand_pad
    mod_idx = functools.partial(_mod_row_index, tm=tm, n_lat=n_lat, seq=seq, bsz=bsz)
    return pl.pallas_call(
        _peer_kernel,
        grid=(n_tok // tm, n_eb),
        in_specs=[
            pl.BlockSpec((tm, dm), lambda i, e: (i, 0)),
            pl.BlockSpec((1, 1, 6 * dm), lambda i, e: (mod_idx(i), 0, 0)),
            pl.BlockSpec((qd, dm), lambda i, e: (0, 0)),
            pl.BlockSpec((PEER_HEADS, 2, PEER_KEYS, PEER_QDIM // 2), lambda i, e: (0, 0, 0, 0)),
            pl.BlockSpec((PEER_HEADS, 2, PEER_KEYS, PEER_QDIM // 2), lambda i, e: (0, 0, 0, 0)),
            pl.BlockSpec((PEER_EB, dm), lambda i, e: (0, 0)),
            pl.BlockSpec((PEER_EB, dm), lambda i, e: (jnp.minimum(e + 1, n_eb - 1), 0)),
            pl.BlockSpec((1, dm, PEER_EB), lambda i, e: (jnp.maximum(e - 1, 0), 0, 0)),
            pl.BlockSpec((1, dm, PEER_EB), lambda i, e: (n_eb - 1, 0, 0)),
            pl.BlockSpec((1, dm), lambda i, e: (0, 0)),
            pl.BlockSpec((1, dm), lambda i, e: (0, 0)),
        ],
        out_specs=pl.BlockSpec((tm, dm), lambda i, e: (i, 0)),
        out_shape=jax.ShapeDtypeStruct((n_tok, dm), f32),
        scratch_shapes=[
            pltpu.VMEM((dm, tm), bf16),
            pltpu.VMEM((2, PEER_KEYS, tm), f32),
            pltpu.VMEM((PEER_HEADS, PEER_KEYS, tm), f32),
            pltpu.VMEM((PEER_HEADS, PEER_KEYS, tm), f32),
            pltpu.VMEM((PEER_HEADS, PEER_KEYS, tm), bf16),
            pltpu.VMEM((PEER_HEADS, PEER_KEYS, tm), bf16),
            pltpu.VMEM((PEER_TOPK, tm), f32),
            pltpu.VMEM((PEER_TOPK, tm), f32),
            pltpu.VMEM((n_cand_pad, tm), f32),
            pltpu.VMEM((2, PEER_EB, tm), f32),
            pltpu.VMEM((2, PEER_EB, tm), bf16),
            pltpu.VMEM((dm, tm), f32),
        ],
        compiler_params=pltpu.CompilerParams(
            dimension_semantics=("arbitrary", "arbitrary"),
            vmem_limit_bytes=VMEM_LIMIT),
        name="peer_ln",
    )(xa, mod3, wq_bf, sk_hi, sk_lo, u_bf, u_bf, vt_bf, vt_bf, ln_g.reshape(1, dm), ln_b.reshape(1, dm))


S5_CHUNK = 16
S5_LEVELS = 9


def _s5_prep(a_re, a_im, log_dt, b_re, b_im, c_re, c_im, d):
    f32, bf16 = jnp.float32, jnp.bfloat16
    hp = lax.Precision.HIGHEST
    n_g, n_p, q = S5_GROUPS, S5_STATE, S5_CHUNK
    dt = jnp.exp(log_dt.astype(f32))[..., None]

    def apow(n):
        mag = jnp.exp(a_re * dt * n)
        ang = a_im * dt * n
        return mag * jnp.cos(ang), mag * jnp.sin(ang)

    ar1, ai1 = apow(1.0)
    den = a_re * a_re + a_im * a_im
    nr, ni = ar1 - 1.0, ai1
    cr = (nr * a_re + ni * a_im) / den
    ci = (ni * a_re - nr * a_im) / den
    bbr = cr[..., None] * b_re - ci[..., None] * b_im
    bbi = cr[..., None] * b_im + ci[..., None] * b_re
    lag = jnp.arange(q + 1, dtype=f32)[:, None, None, None]
    pr, pi = apow(lag)
    mr = pr[..., None] * bbr - pi[..., None] * bbi
    mi = pr[..., None] * bbi + pi[..., None] * bbr
    kmat = (jnp.einsum('dgcp,ndgpk->ndgck', c_re, mr, precision=hp)
            - jnp.einsum('dgcp,ndgpk->ndgck', c_im, mi, precision=hp))
    r_idx = jnp.arange(q)[:, None]
    t_idx = jnp.arange(q)[None, :]

    def toeplitz(kd, lagm):
        blk = kd[jnp.clip(lagm, 0, q)]
        blk = jnp.where((lagm >= 0)[:, :, None, None, None], blk, 0.0)
        return blk.transpose(2, 0, 4, 1, 3).reshape(n_g, q * S5_GROUP, q * S5_GROUP)

    eye = jnp.eye(q * S5_GROUP, dtype=f32)
    dvec = jnp.tile(d.reshape(n_g, 1, S5_GROUP), (1, q, 1)).reshape(n_g, q * S5_GROUP)
    tsum = toeplitz(kmat[:, 0], t_idx - r_idx) + toeplitz(kmat[:, 1], r_idx - t_idx) + eye[None] * dvec[:, None, :]

    def w_in(direction, exps):
        wr = mr[exps, direction].transpose(1, 0, 3, 2).reshape(n_g, q * S5_GROUP, n_p)
        wi = mi[exps, direction].transpose(1, 0, 3, 2).reshape(n_g, q * S5_GROUP, n_p)
        return wr, wi

    def w_out(direction, exps):
        pre, pim = pr[exps, direction], pi[exps, direction]
        cre, cim = c_re[direction], c_im[direction]
        wre = cre[None] * pre[:, :, None, :] - cim[None] * pim[:, :, None, :]
        wim = -(cre[None] * pim[:, :, None, :] + cim[None] * pre[:, :, None, :])
        return (wre.transpose(1, 3, 0, 2).reshape(n_g, n_p, q * S5_GROUP),
                wim.transpose(1, 3, 0, 2).reshape(n_g, n_p, q * S5_GROUP))

    steps = jnp.arange(q)
    win = [w_in(0, q - 1 - steps), w_in(1, steps)]
    wout = [w_out(0, steps + 1), w_out(1, q - steps)]
    zc = jnp.zeros((n_g, q * S5_GROUP, n_p), f32)
    zr = jnp.zeros((n_g, n_p, q * S5_GROUP), f32)
    win_p, wout_p = [], []
    for direction in range(2):
        wr, wi = win[direction]
        even = jnp.concatenate([wr, zc, wi, zc], axis=2)
        odd = jnp.concatenate([zc, wr, zc, wi], axis=2)
        is_odd = (jnp.arange(n_g) % 2 == 1)[:, None, None]
        win_p.append(jnp.where(is_odd, odd, even))
        vr, vi = wout[direction]
        even = jnp.concatenate([vr, zr, vi, zr], axis=1)
        odd = jnp.concatenate([zr, vr, zr, vi], axis=1)
        wout_p.append(jnp.where(is_odd, odd, even))
    win_p = jnp.stack(win_p, axis=1).reshape(n_g // 2, 2, 2, 4 * n_p, 4 * n_p).transpose(0, 2, 1, 3, 4)
    wout_p = jnp.stack(wout_p, axis=1).reshape(n_g // 2, 2, 2, 4 * n_p, 4 * n_p).transpose(0, 2, 1, 3, 4)
    lvl = (q * 2.0 ** jnp.arange(S5_LEVELS, dtype=f32))[:, None, None, None]
    lr, li = apow(lvl)
    pw = jnp.stack([lr, li], axis=1)
    pw = pw.transpose(3, 2, 0, 1, 4).reshape(n_g // 2, 2, 2, S5_LEVELS, 2, n_p)
    pw = pw.transpose(0, 2, 3, 4, 1, 5).reshape(n_g // 2, 2, 2 * S5_LEVELS, 2 * n_p)
    return tsum.astype(bf16).reshape(n_g // 2, 2, q * S5_GROUP, q * S5_GROUP), win_p.astype(bf16), wout_p.astype(bf16), pw


def _shift_rows(v, r, up):
    n = v.shape[0]
    row = lax.broadcasted_iota(jnp.int32, v.shape, 0)
    if up:
        return jnp.where(row < n - r, pltpu.roll(v, n - r, axis=0), 0.0)
    return jnp.where(row >= r, pltpu.roll(v, r, axis=0), 0.0)


def _chunk_scan(re, im, pw, rows_per_chunk, up):
    n_chunks = re.shape[0] // rows_per_chunk
    level, s = 0, 1
    while s < n_chunks:
        ar, ai = pw[2 * level:2 * level + 1, :], pw[2 * level + 1:2 * level + 2, :]
        sre = _shift_rows(re, s * rows_per_chunk, up)
        sim = _shift_rows(im, s * rows_per_chunk, up)
        re, im = re + ar * sre - ai * sim, im + ar * sim + ai * sre
        level, s = level + 1, 2 * s
    return re, im


def _s5_kernel(u_ref, scl_ref, shl_ref, scc_ref, shc_ref, t_ref, win_ref, wout_ref, pw_ref, y_ref, *, bsz, ctx_chunks):
    f32, bf16 = jnp.float32, jnp.bfloat16
    n = u_ref.shape[1]
    width = u_ref.shape[2]
    rc = ctx_chunks * bsz
    hs = []
    for gi in range(2):
        xv = u_ref[gi].reshape(n // 8, 8, width)
        hl = xv * scl_ref[gi][None] + shl_ref[gi][None]
        hc = xv * scc_ref[gi][None] + shc_ref[gi][None]
        slab = lax.broadcasted_iota(jnp.int32, xv.shape, 0)
        hs.append(jnp.where(slab < rc // 8, hc, hl).reshape(n, width).astype(bf16))
    y = [jnp.dot(hs[gi], t_ref[0, gi], preferred_element_type=f32) for gi in range(2)]
    half = width // 2
    for direction in range(2):
        sloc = (jnp.dot(hs[0], win_ref[0, direction, 0], preferred_element_type=f32)
                + jnp.dot(hs[1], win_ref[0, direction, 1], preferred_element_type=f32))
        re, im = sloc[:, :half], sloc[:, half:]
        pw = pw_ref[0, direction]
        if direction == 0:
            sre, sim = _chunk_scan(_shift_rows(re, bsz, False), _shift_rows(im, bsz, False), pw, bsz, False)
        else:
            cre, cim = _chunk_scan(_shift_rows(re[:rc], bsz, True), _shift_rows(im[:rc], bsz, True), pw, bsz, True)
            ar, ai = pw[0:1, :], pw[1:2, :]
            fre = ar * cre[0:8] - ai * cim[0:8] + re[0:8]
            fim = ar * cim[0:8] + ai * cre[0:8] + im[0:8]
            row8 = lax.broadcasted_iota(jnp.int32, fre.shape, 0)
            tre = jnp.where(row8 >= 8 - bsz, pltpu.roll(fre, 8 - bsz, axis=0), 0.0)
            tim = jnp.where(row8 >= 8 - bsz, pltpu.roll(fim, 8 - bsz, axis=0), 0.0)
            lre, lim = _shift_rows(re[rc:], bsz, True), _shift_rows(im[rc:], bsz, True)
            lre = jnp.concatenate([lre[:-8], lre[-8:] + tre], axis=0)
            lim = jnp.concatenate([lim[:-8], lim[-8:] + tim], axis=0)
            lre, lim = _chunk_scan(lre, lim, pw, bsz, True)
            sre = jnp.concatenate([cre, lre], axis=0)
            sim = jnp.concatenate([cim, lim], axis=0)
        s_in = jnp.concatenate([sre, sim], axis=1).astype(bf16)
        for gi in range(2):
            y[gi] = y[gi] + jnp.dot(s_in, wout_ref[0, direction, gi], preferred_element_type=f32)
    for gi in range(2):
        y_ref[gi] = y[gi]


def s5_ssm_pallas(xa, mod_tab, n_lat, seq, bsz, prep):
    tsum, win_p, wout_p, pw = prep
    f32 = jnp.float32
    n_tok, dm = xa.shape
    ctx_len = (n_tok - n_lat) // bsz
    q = S5_CHUNK
    n_chunks = (seq + ctx_len) // q
    width = q * S5_GROUP
    full = jnp.concatenate([xa[n_lat:].reshape(bsz, ctx_len, dm), xa[:n_lat].reshape(bsz, seq, dm)], axis=1)
    u = full.reshape(bsz, n_chunks, q, S5_GROUPS, S5_GROUP).transpose(3, 1, 0, 2, 4).reshape(S5_GROUPS, n_chunks * bsz, width)

    def tile(vec_rows):
        rows = vec_rows.shape[0]
        t = vec_rows.reshape(rows, S5_GROUPS, 1, S5_GROUP)
        t = jnp.broadcast_to(t, (rows, S5_GROUPS, q, S5_GROUP)).reshape(rows, S5_GROUPS, width)
        return jnp.tile(t.transpose(1, 0, 2), (1, 8 // rows, 1))

    scl, shl = tile(1.0 + mod_tab[:bsz, dm:2 * dm]), tile(mod_tab[:bsz, 0:dm])
    scc = tile(jnp.broadcast_to(1.0 + mod_tab[bsz:bsz + 1, dm:2 * dm], (bsz, dm)))
    shc = tile(jnp.broadcast_to(mod_tab[bsz:bsz + 1, 0:dm], (bsz, dm)))
    n_rows = n_chunks * bsz
    tile_spec = pl.BlockSpec((2, 8, width), lambda p: (p, 0, 0))
    y = pl.pallas_call(
        functools.partial(_s5_kernel, bsz=bsz, ctx_chunks=ctx_len // q),
        grid=(S5_GROUPS // 2,),
        in_specs=[
            pl.BlockSpec((2, n_rows, width), lambda p: (p, 0, 0)),
            tile_spec, tile_spec, tile_spec, tile_spec,
            pl.BlockSpec((1, 2, width, width), lambda p: (p, 0, 0, 0)),
            pl.BlockSpec((1, 2, 2, width, width), lambda p: (p, 0, 0, 0, 0)),
            pl.BlockSpec((1, 2, 2, width, width), lambda p: (p, 0, 0, 0, 0)),
            pl.BlockSpec((1, 2, 2 * S5_LEVELS, 2 * S5_STATE), lambda p: (p, 0, 0, 0)),
        ],
        out_specs=pl.BlockSpec((2, n_rows, width), lambda p: (p, 0, 0)),
        out_shape=jax.ShapeDtypeStruct((S5_GROUPS, n_rows, width), f32),
        compiler_params=pltpu.CompilerParams(dimension_semantics=("arbitrary",), vmem_limit_bytes=VMEM_LIMIT),
        name="s5_ssm",
    )(u, scl, shl, scc, shc, tsum, win_p, wout_p, pw)
    yf = y.reshape(S5_GROUPS, n_chunks, bsz, q, S5_GROUP).transpose(2, 1, 3, 0, 4).reshape(bsz, seq + ctx_len, dm)
    return jnp.concatenate([yf[:, ctx_len:].reshape(n_lat, dm), yf[:, :ctx_len].reshape(bsz * ctx_len, dm)], axis=0)


def _glu_ln_kernel(y_ref, x_ref, mod_ref, wv_ref, wg_ref, lng_ref, lnb_ref, o_ref):
    f32 = jnp.float32
    dm = x_ref.shape[1]
    g = jax.nn.gelu(y_ref[...]).astype(jnp.bfloat16)
    val = jnp.dot(g, wv_ref[...], preferred_element_type=f32)
    gate = jnp.dot(g, wg_ref[...], preferred_element_type=f32)
    out = val * jax.nn.sigmoid(gate)
    mod = mod_ref[0]
    z = DEEPNORM_ALPHA * x_ref[...] + mod[:, 2 * dm:3 * dm] * out
    mu = jnp.mean(z, axis=-1, keepdims=True)
    zc = z - mu
    var = jnp.mean(zc * zc, axis=-1, keepdims=True)
    o_ref[...] = zc * lax.rsqrt(var + LN_EPS) * lng_ref[...] + lnb_ref[...]


TOK_TM = 512


def glu_ln_pallas(ya, xa, mod3, n_lat, seq, bsz, w_val, w_gate, ln_g, ln_b):
    n_tok, dm = ya.shape
    tm = TOK_TM
    bf16 = jnp.bfloat16
    mod_idx = functools.partial(_mod_row_index, tm=tm, n_lat=n_lat, seq=seq, bsz=bsz)
    return pl.pallas_call(
        _glu_ln_kernel,
        grid=(n_tok // tm,),
        in_specs=[
            pl.BlockSpec((tm, dm), lambda i: (i, 0)),
            pl.BlockSpec((tm, dm), lambda i: (i, 0)),
            pl.BlockSpec((1, 1, 6 * dm), lambda i: (mod_idx(i), 0, 0)),
            pl.BlockSpec((dm, dm), lambda i: (0, 0)),
            pl.BlockSpec((dm, dm), lambda i: (0, 0)),
            pl.BlockSpec((1, dm), lambda i: (0, 0)),
            pl.BlockSpec((1, dm), lambda i: (0, 0)),
        ],
        out_specs=pl.BlockSpec((tm, dm), lambda i: (i, 0)),
        out_shape=jax.ShapeDtypeStruct((n_tok, dm), jnp.float32),
        compiler_params=pltpu.CompilerParams(dimension_semantics=("arbitrary",), vmem_limit_bytes=VMEM_LIMIT),
        name="glu_ln",
    )(ya, xa, mod3, w_val.astype(bf16), w_gate.astype(bf16), ln_g.reshape(1, dm), ln_b.reshape(1, dm))


MLA_HD = 128
MLA_TQ = 256
MLA_HB = 4


def _mla_prep(w_down, w_uq, w_uk, w_uv):
    bf16 = jnp.bfloat16
    quarter = MLA_ROPE // 4
    swap = np.concatenate([np.arange(quarter, 2 * quarter), np.arange(0, quarter),
                           np.arange(3 * quarter, 4 * quarter), np.arange(2 * quarter, 3 * quarter)])
    dm = w_down.shape[0]
    w_cq = w_down[:, :MLA_Q_RANK]
    w_ckv = w_down[:, MLA_Q_RANK:MLA_Q_RANK + MLA_KV_RANK]
    w_kr = w_down[:, MLA_Q_RANK + MLA_KV_RANK:]
    wd_t = jnp.concatenate([w_cq, w_ckv, jnp.zeros((dm, MLA_NOPE), w_down.dtype), w_kr, w_kr[:, swap]], axis=1).T
    rope = w_uq[:, :, MLA_NOPE:]
    wuq_t = jnp.concatenate([w_uq, rope[:, :, swap]], axis=2).reshape(MLA_Q_RANK, MLA_HEADS * MLA_HD).T
    wuk = jnp.concatenate([w_uk, jnp.zeros((MLA_KV_RANK, MLA_HEADS, MLA_HD - MLA_NOPE), w_uk.dtype)], axis=2)
    wuk = wuk.reshape(MLA_KV_RANK, MLA_HEADS * MLA_HD)
    wuv_t = w_uv.reshape(MLA_KV_RANK, MLA_HEADS * MLA_V).T
    return wd_t.astype(bf16), wuq_t.astype(bf16), wuk.astype(bf16), wuv_t.astype(bf16)


def _rope_tables(seq, tm):
    quarter = MLA_ROPE // 4
    freqs = ROPE_BASE ** (-jnp.arange(quarter, dtype=jnp.float32) / quarter)
    t = jnp.arange(seq, dtype=jnp.float32)
    row, col = jnp.floor(t / GRID_W), t - GRID_W * jnp.floor(t / GRID_W)
    ang_r, ang_c = freqs[:, None] * row[None, :], freqs[:, None] * col[None, :]
    cos32 = jnp.concatenate([jnp.cos(ang_r), jnp.cos(ang_r), jnp.cos(ang_c), jnp.cos(ang_c)], axis=0)
    sin32 = jnp.concatenate([-jnp.sin(ang_r), jnp.sin(ang_r), -jnp.sin(ang_c), jnp.sin(ang_c)], axis=0)
    ones = jnp.ones((MLA_NOPE, seq + tm), jnp.float32)
    zeros = jnp.zeros((MLA_HD - MLA_NOPE - MLA_ROPE, seq + tm), jnp.float32)
    cos_t = jnp.concatenate([ones, jnp.concatenate([cos32, jnp.ones((MLA_ROPE, tm))], axis=1), zeros], axis=0)
    sin_t = jnp.concatenate([0.0 * ones, jnp.concatenate([sin32, jnp.zeros((MLA_ROPE, tm))], axis=1), zeros], axis=0)
    return cos_t, sin_t


def _mla_proj_kernel(x_ref, mod_ref, cos_ref, sin_ref, wd_ref, wuq_ref, wuk_ref, wuv_ref, qn_ref, kvn_ref,
                     qt_ref, k_ref, vt_ref):
    f32, bf16 = jnp.float32, jnp.bfloat16
    dm = x_ref.shape[1]
    mod = mod_ref[0]
    h = x_ref[...] * (1.0 + mod[:, dm:2 * dm]) + mod[:, 0:dm]
    ht = h.T.astype(bf16)
    dt = jnp.dot(wd_ref[...], ht, preferred_element_type=f32)
    cq, ckv, kr = dt[:MLA_Q_RANK], dt[MLA_Q_RANK:MLA_Q_RANK + MLA_KV_RANK], dt[MLA_Q_RANK + MLA_KV_RANK:]
    cqn = cq * lax.rsqrt(jnp.mean(cq * cq, axis=0, keepdims=True) + RMS_EPS) * qn_ref[...]
    ckvn = ckv * lax.rsqrt(jnp.mean(ckv * ckv, axis=0, keepdims=True) + RMS_EPS) * kvn_ref[...]
    cos_t, sin_t = cos_ref[...], sin_ref[...]

    def rope(v):
        shifted = jnp.concatenate([v[MLA_ROPE:], v[:MLA_ROPE]], axis=0)
        return v * cos_t + shifted * sin_t

    scale = (MLA_NOPE + MLA_ROPE) ** -0.5
    q_all = jnp.dot(wuq_ref[...], cqn.astype(bf16), preferred_element_type=f32)
    for hd in range(MLA_HEADS):
        qt_ref[hd] = (rope(q_all[hd * MLA_HD:(hd + 1) * MLA_HD]) * scale).astype(bf16)
    ckvn_bf = ckvn.astype(bf16)
    vt_ref[...] = jnp.dot(wuv_ref[...], ckvn_bf, preferred_element_type=f32).astype(bf16)
    k_all = jnp.dot(ckvn.T.astype(bf16), wuk_ref[...], preferred_element_type=f32)
    kr_rows = rope(kr).T
    for hd in range(MLA_HEADS):
        k_ref[hd] = (k_all[:, hd * MLA_HD:(hd + 1) * MLA_HD] + kr_rows).astype(bf16)


def _mla_attn_kernel(*refs, with_latent):
    f32, bf16 = jnp.float32, jnp.bfloat16
    if with_latent:
        q_ref, kl_ref, kc_ref, vl_ref, vc_ref, o_ref = refs
    else:
        q_ref, kc_ref, vc_ref, o_ref = refs
    for i in range(q_ref.shape[0]):
        rows = slice(i * MLA_V, (i + 1) * MLA_V)
        q = q_ref[i]
        s_c = jnp.dot(kc_ref[i], q, preferred_element_type=f32)
        m = jnp.max(s_c, axis=0, keepdims=True)
        if with_latent:
            s_l = jnp.dot(kl_ref[i], q, preferred_element_type=f32)
            m = jnp.maximum(m, jnp.max(s_l, axis=0, keepdims=True))
        p_c = jnp.exp(s_c - m)
        den = jnp.sum(p_c, axis=0, keepdims=True)
        o = jnp.dot(vc_ref[rows, :], p_c.astype(bf16), preferred_element_type=f32)
        if with_latent:
            p_l = jnp.exp(s_l - m)
            den = den + jnp.sum(p_l, axis=0, keepdims=True)
            o = o + jnp.dot(vl_ref[rows, :], p_l.astype(bf16), preferred_element_type=f32)
        o_ref[rows, :] = (o / den).astype(bf16)


def _mla_out_kernel(ot_ref, x_ref, mod_ref, wo_ref, lng_ref, lnb_ref, o_ref):
    f32 = jnp.float32
    dm = x_ref.shape[1]
    attn = ot_ref[...].astype(f32).T.astype(jnp.bfloat16)
    out = jnp.dot(attn, wo_ref[...], preferred_element_type=f32)
    mod = mod_ref[0]
    z = DEEPNORM_ALPHA * x_ref[...] + mod[:, 2 * dm:3 * dm] * out
    mu = jnp.mean(z, axis=-1, keepdims=True)
    zc = z - mu
    var = jnp.mean(zc * zc, axis=-1, keepdims=True)
    o_ref[...] = zc * lax.rsqrt(var + LN_EPS) * lng_ref[...] + lnb_ref[...]


def mla_ln_pallas(xa, mod3, n_lat, seq, bsz, w_down, q_norm, kv_norm, w_uq, w_uk, w_uv, w_o, ln_g, ln_b):
    f32, bf16 = jnp.float32, jnp.bfloat16
    n_tok, dm = xa.shape
    n_ctx = n_tok - n_lat
    ctx_len = n_ctx // bsz
    tm = TOK_TM
    wd_t, wuq_t, wuk, wuv_t = _mla_prep(w_down, w_uq, w_uk, w_uv)
    cos_t, sin_t = _rope_tables(seq, tm)
    mod_idx = functools.partial(_mod_row_index, tm=tm, n_lat=n_lat, seq=seq, bsz=bsz)
    tiles_per_sample = seq // tm
    pos_idx = lambda i: jnp.where(i * tm < n_lat, i % tiles_per_sample, tiles_per_sample)
    n_hd, n_dn = MLA_HEADS * MLA_HD, wd_t.shape[0]
    whole = lambda shape: pl.BlockSpec(shape, lambda i: (0,) * len(shape))
    params = pltpu.CompilerParams(dimension_semantics=("arbitrary",), vmem_limit_bytes=VMEM_LIMIT)
    qt, k, vt = pl.pallas_call(
        _mla_proj_kernel,
        grid=(n_tok // tm,),
        in_specs=[
            pl.BlockSpec((tm, dm), lambda i: (i, 0)),
            pl.BlockSpec((1, 1, 6 * dm), lambda i: (mod_idx(i), 0, 0)),
            pl.BlockSpec((MLA_HD, tm), lambda i: (0, pos_idx(i))),
            pl.BlockSpec((MLA_HD, tm), lambda i: (0, pos_idx(i))),
            whole((n_dn, dm)), whole((n_hd, MLA_Q_RANK)), whole((MLA_KV_RANK, n_hd)),
            whole((MLA_HEADS * MLA_V, MLA_KV_RANK)), whole((MLA_Q_RANK, 1)), whole((MLA_KV_RANK, 1)),
        ],
        out_specs=[
            pl.BlockSpec((MLA_HEADS, MLA_HD, tm), lambda i: (0, 0, i)),
            pl.BlockSpec((MLA_HEADS, tm, MLA_HD), lambda i: (0, i, 0)),
            pl.BlockSpec((MLA_HEADS * MLA_V, tm), lambda i: (0, i)),
        ],
        out_shape=[
            jax.ShapeDtypeStruct((MLA_HEADS, MLA_HD, n_tok), bf16),
            jax.ShapeDtypeStruct((MLA_HEADS, n_tok, MLA_HD), bf16),
            jax.ShapeDtypeStruct((MLA_HEADS * MLA_V, n_tok), bf16),
        ],
        compiler_params=params,
        name="mla_proj",
    )(xa, mod3, cos_t, sin_t, wd_t, wuq_t, wuk, wuv_t, q_norm.reshape(-1, 1), kv_norm.reshape(-1, 1))

    tq, hb = MLA_TQ, MLA_HB
    n_qt = seq // tq
    ctx_blk = n_lat // ctx_len
    params3 = pltpu.CompilerParams(dimension_semantics=("arbitrary",) * 3, vmem_limit_bytes=VMEM_LIMIT)
    ot_lat = pl.pallas_call(
        functools.partial(_mla_attn_kernel, with_latent=True),
        grid=(bsz, MLA_HEADS // hb, n_qt),
        in_specs=[
            pl.BlockSpec((hb, MLA_HD, tq), lambda b, hd, t: (hd, 0, b * n_qt + t)),
            pl.BlockSpec((hb, seq, MLA_HD), lambda b, hd, t: (hd, b, 0)),
            pl.BlockSpec((hb, ctx_len, MLA_HD), lambda b, hd, t: (hd, ctx_blk + b, 0)),
            pl.BlockSpec((hb * MLA_V, seq), lambda b, hd, t: (hd, b)),
            pl.BlockSpec((hb * MLA_V, ctx_len), lambda b, hd, t: (hd, ctx_blk + b)),
        ],
        out_specs=pl.BlockSpec((hb * MLA_V, tq), lambda b, hd, t: (hd, b * n_qt + t)),
        out_shape=jax.ShapeDtypeStruct((MLA_HEADS * MLA_V, n_lat), bf16),
        compiler_params=params3,
        name="mla_attn_latent",
    )(qt, k, k, vt, vt)
    ot_ctx = pl.pallas_call(
        functools.partial(_mla_attn_kernel, with_latent=False),
        grid=(bsz, MLA_HEADS // hb, 1),
        in_specs=[
            pl.BlockSpec((hb, MLA_HD, ctx_len), lambda b, hd, t: (hd, 0, ctx_blk + b)),
            pl.BlockSpec((hb, ctx_len, MLA_HD), lambda b, hd, t: (hd, ctx_blk + b, 0)),
            pl.BlockSpec((hb * MLA_V, ctx_len), lambda b, hd, t: (hd, ctx_blk + b)),
        ],
        out_specs=pl.BlockSpec((hb * MLA_V, ctx_len), lambda b, hd, t: (hd, b)),
        out_shape=jax.ShapeDtypeStruct((MLA_HEADS * MLA_V, n_ctx), bf16),
        compiler_params=params3,
        name="mla_attn_context",
    )(qt, k, vt)
    ot = jnp.concatenate([ot_lat, ot_ctx], axis=1)
    return pl.pallas_call(
        _mla_out_kernel,
        grid=(n_tok // tm,),
        in_specs=[
            pl.BlockSpec((MLA_HEADS * MLA_V, tm), lambda i: (0, i)),
            pl.BlockSpec((tm, dm), lambda i: (i, 0)),
            pl.BlockSpec((1, 1, 6 * dm), lambda i: (mod_idx(i), 0, 0)),
            whole((MLA_HEADS * MLA_V, dm)), whole((1, dm)), whole((1, dm)),
        ],
        out_specs=pl.BlockSpec((tm, dm), lambda i: (i, 0)),
        out_shape=jax.ShapeDtypeStruct((n_tok, dm), f32),
        compiler_params=params,
        name="mla_out_ln",
    )(ot, xa, mod3, w_o.astype(bf16), ln_g.reshape(1, dm), ln_b.reshape(1, dm))


SSD_DT_PAD = 128
SSD_HPG = SSD_HEADS // SSD_GROUPS
SSD_GW = SSD_HPG * SSD_HEADDIM


def _ssd_in_kernel(x_ref, mod_ref, w_ref, wdt_ref, z_ref, xbc_ref, dt_ref, dtt_ref):
    f32, bf16 = jnp.float32, jnp.bfloat16
    dm = x_ref.shape[1]
    mod = mod_ref[0]
    h = x_ref[...] * (1.0 + mod[:, dm:2 * dm]) + mod[:, 0:dm]
    proj = jnp.dot(h.astype(bf16), w_ref[...], preferred_element_type=f32)
    z_ref[...] = proj[:, :SSD_D_INNER]
    xbc_ref[...] = proj[:, SSD_D_INNER:SSD_D_INNER + SSD_CONV_DIM]
    dt_ref[...] = proj[:, SSD_D_INNER + SSD_CONV_DIM:]
    dtt_ref[...] = jnp.dot(wdt_ref[...], h.T.astype(bf16), preferred_element_type=f32)


def _softplus(v):
    return jnp.maximum(v, 0.0) + jnp.log(1.0 + jnp.exp(-jnp.abs(v)))


def _split3(v):
    f32, bf16 = jnp.float32, jnp.bfloat16
    hi = v.astype(bf16)
    r1 = v - hi.astype(f32)
    mid = r1.astype(bf16)
    lo = (r1 - mid.astype(f32)).astype(bf16)
    return jnp.concatenate([hi, mid, lo], axis=1)


def _ssd_direction(direction, xbc_ref, prev_ref, next_ref, dt_ref, dtt_ref, has_prev, has_next,
                   cw_ref, cb_ref, bias_r_ref, a_r_ref, bias_c_ref, a_c_ref, e_ref, dskip_ref, state_ref, y_ref):
    f32, bf16 = jnp.float32, jnp.bfloat16
    hp = lax.Precision.HIGHEST
    q = xbc_ref.shape[0]
    nh = SSD_HEADS
    xm = xbc_ref[...]
    row = lax.broadcasted_iota(jnp.int32, xm.shape, 0)
    before = jnp.where(has_prev, prev_ref[7:8, :], 0.0)
    after = jnp.where(has_next, next_ref[0:1, :], 0.0)
    x_dn = jnp.where(row == 0, before, pltpu.roll(xm, 1, axis=0))
    x_up = jnp.where(row == q - 1, after, pltpu.roll(xm, q - 1, axis=0))
    conv = cb_ref[...] + x_dn * cw_ref[0:1, :] + xm * cw_ref[1:2, :] + x_up * cw_ref[2:3, :]
    conv = conv * jax.nn.sigmoid(conv)
    xs = conv[:, :SSD_D_INNER]
    gn = SSD_GROUPS * SSD_STATE
    bm, cm = conv[:, SSD_D_INNER:SSD_D_INNER + gn], conv[:, SSD_D_INNER + gn:]
    hs = slice(direction * nh, (direction + 1) * nh)
    dt = _softplus(dt_ref[:, hs] + bias_r_ref[direction:direction + 1, :])
    dtt = _softplus(dtt_ref[hs, :] + bias_c_ref[direction])
    r_i = lax.broadcasted_iota(jnp.int32, (q, q), 0)
    c_i = lax.broadcasted_iota(jnp.int32, (q, q), 1)
    causal = (r_i >= c_i) if direction == 0 else (r_i <= c_i)
    tri = causal.astype(f32)
    tri_t = ((c_i >= r_i) if direction == 0 else (c_i <= r_i)).astype(f32)
    a_cum = jnp.dot(tri, dt * a_r_ref[direction:direction + 1, :], precision=hp, preferred_element_type=f32)
    a_cum_t = jnp.dot(dtt * a_c_ref[direction], tri_t, precision=hp, preferred_element_type=f32)
    last = q - 1 if direction == 0 else 0
    a_exp = jnp.dot(_split3(a_cum), e_ref[...], preferred_element_type=f32)
    dt_exp = jnp.dot(_split3(dt), e_ref[...], preferred_element_type=f32)
    xdt = xs * dt_exp
    a_tot = a_exp[last:last + 1, :]
    xw = (xdt * jnp.exp(a_tot - a_exp)).astype(bf16)
    grow = jnp.exp(a_exp)
    carry = jnp.exp(a_tot)
    xdt_bf = xdt.astype(bf16)
    for g in range(SSD_GROUPS):
        gl = slice(g * SSD_GW, (g + 1) * SSD_GW)
        nl = slice(g * SSD_STATE, (g + 1) * SSD_STATE)
        bm_g, cm_g = bm[:, nl], cm[:, nl].astype(bf16)
        prev = state_ref[direction, g]
        y_g = jnp.dot(cm_g, prev.astype(bf16), preferred_element_type=f32) * grow[:, gl]
        states = jnp.dot(bm_g.T.astype(bf16), xw[:, gl], preferred_element_type=f32)
        state_ref[direction, g] = prev * carry[:, gl] + states
        cb = lax.dot_general(cm_g, bm_g.astype(bf16), (((1,), (1,)), ((), ())), preferred_element_type=f32)
        parts = []
        for hh in range(SSD_HPG):
            hd = g * SSD_HPG + hh
            seg = jnp.broadcast_to(a_cum[:, hd:hd + 1], (q, q)) - a_cum_t[hd:hd + 1, :]
            lmat = (jnp.where(causal, jnp.exp(seg), 0.0) * cb).astype(bf16)
            parts.append(jnp.dot(lmat, xdt_bf[:, hd * SSD_HEADDIM:(hd + 1) * SSD_HEADDIM], preferred_element_type=f32))
        y_g = y_g + jnp.concatenate(parts, axis=1)
        if direction == 0:
            y_g = y_g + dskip_ref[:, gl] * xs[:, gl]
        y_ref[:, gl] = y_g


def _ssd_scan_kernel(xf_ref, pf_ref, nf_ref, dtf_ref, dttf_ref, xb_ref, pb_ref, nb_ref, dtb_ref, dttb_ref,
                     cw_ref, cb_ref, bias_r_ref, a_r_ref, bias_c_ref, a_c_ref, e_ref, dskip_ref,
                     yf_ref, yb_ref, state_ref, *, ctx_chunks, lat_chunks):
    j = pl.program_id(1)

    @pl.when(j == 0)
    def _reset():
        state_ref[...] = jnp.zeros(state_ref.shape, jnp.float32)

    in_ctx = j < ctx_chunks
    pos_f = jnp.where(in_ctx, j, j - ctx_chunks)
    seg_len = jnp.where(in_ctx, ctx_chunks, lat_chunks)
    pos_b = seg_len - 1 - pos_f
    shared = (cw_ref, cb_ref, bias_r_ref, a_r_ref, bias_c_ref, a_c_ref, e_ref, dskip_ref, state_ref)
    _ssd_direction(0, xf_ref, pf_ref, nf_ref, dtf_ref, dttf_ref, pos_f > 0, pos_f < seg_len - 1, *shared, yf_ref)
    _ssd_direction(1, xb_ref, pb_ref, nb_ref, dtb_ref, dttb_ref, pos_b > 0, pos_b < seg_len - 1, *shared, yb_ref)


def _ssd_out_kernel(yf_ref, yb_ref, z_ref, x_ref, mod_ref, nw_ref, wo_ref, lng_ref, lnb_ref, o_ref):
    f32 = jnp.float32
    dm = x_ref.shape[1]
    z = z_ref[...]
    y = (yf_ref[...] + yb_ref[...]) * (z * jax.nn.sigmoid(z))
    y = y * lax.rsqrt(jnp.mean(y * y, axis=-1, keepdims=True) + RMS_EPS) * nw_ref[...]
    out = jnp.dot(y.astype(jnp.bfloat16), wo_ref[...], preferred_element_type=f32)
    mod = mod_ref[0]
    v = DEEPNORM_ALPHA * x_ref[...] + mod[:, 2 * dm:3 * dm] * out
    mu = jnp.mean(v, axis=-1, keepdims=True)
    vc = v - mu
    var = jnp.mean(vc * vc, axis=-1, keepdims=True)
    o_ref[...] = vc * lax.rsqrt(var + LN_EPS) * lng_ref[...] + lnb_ref[...]


def ssd_ln_pallas(xa, mod3, n_lat, seq, bsz, w_in, conv_w, conv_b, dt_bias, a_log, d, norm_w, w_out, ln_g, ln_b):
    f32, bf16 = jnp.float32, jnp.bfloat16
    n_tok, dm = xa.shape
    ctx_len = (n_tok - n_lat) // bsz
    tm = TOK_TM
    q = SSD_CHUNK
    nh = SSD_HEADS
    pad = SSD_DT_PAD - 2 * nh
    w_ext = jnp.concatenate([w_in, jnp.zeros((dm, pad), w_in.dtype)], axis=1).astype(bf16)
    n_in = w_ext.shape[1]
    wdt_t = w_ext[:, SSD_D_INNER + SSD_CONV_DIM:].T
    mod_idx = functools.partial(_mod_row_index, tm=tm, n_lat=n_lat, seq=seq, bsz=bsz)
    whole = lambda shape: pl.BlockSpec(shape, lambda *_: (0,) * len(shape))
    params = pltpu.CompilerParams(dimension_semantics=("arbitrary",), vmem_limit_bytes=VMEM_LIMIT)
    z, xbc, dt, dtt = pl.pallas_call(
        _ssd_in_kernel,
        grid=(n_tok // tm,),
        in_specs=[
            pl.BlockSpec((tm, dm), lambda i: (i, 0)),
            pl.BlockSpec((1, 1, 6 * dm), lambda i: (mod_idx(i), 0, 0)),
            whole((dm, n_in)), whole((SSD_DT_PAD, dm)),
        ],
        out_specs=[
            pl.BlockSpec((tm, SSD_D_INNER), lambda i: (i, 0)),
            pl.BlockSpec((tm, SSD_CONV_DIM), lambda i: (i, 0)),
            pl.BlockSpec((tm, SSD_DT_PAD), lambda i: (i, 0)),
            pl.BlockSpec((SSD_DT_PAD, tm), lambda i: (0, i)),
        ],
        out_shape=[
            jax.ShapeDtypeStruct((n_tok, SSD_D_INNER), f32),
            jax.ShapeDtypeStruct((n_tok, SSD_CONV_DIM), f32),
            jax.ShapeDtypeStruct((n_tok, SSD_DT_PAD), f32),
            jax.ShapeDtypeStruct((SSD_DT_PAD, n_tok), f32),
        ],
        compiler_params=params,
        name="ssd_in_proj",
    )(xa, mod3, w_ext, wdt_t)

    ctx_chunks, lat_chunks = ctx_len // q, seq // q
    ctx_base = n_lat // q
    n_chunk_total = n_tok // q

    def chunk_of(b, j, backward):
        in_ctx = j < ctx_chunks
        pos = jnp.where(in_ctx, j, j - ctx_chunks)
        seg = jnp.where(in_ctx, ctx_chunks, lat_chunks)
        pos = jnp.where(backward, seg - 1 - pos, pos)
        return jnp.where(in_ctx, ctx_base + b * ctx_chunks + pos, b * lat_chunks + pos)

    sub = q // 8
    def dir_specs(backward):
        ch = lambda b, j: chunk_of(b, j, backward)
        return [
            pl.BlockSpec((q, SSD_CONV_DIM), lambda b, j: (ch(b, j), 0)),
            pl.BlockSpec((8, SSD_CONV_DIM), lambda b, j: (jnp.maximum(ch(b, j) * sub - 1, 0), 0)),
            pl.BlockSpec((8, SSD_CONV_DIM), lambda b, j: (jnp.minimum((ch(b, j) + 1) * sub, n_chunk_total * sub - 1), 0)),
            pl.BlockSpec((q, SSD_DT_PAD), lambda b, j: (ch(b, j), 0)),
            pl.BlockSpec((SSD_DT_PAD, q), lambda b, j: (0, ch(b, j))),
        ]

    a = -jnp.exp(a_log.astype(f32))
    expand = jnp.repeat(jnp.eye(nh, dtype=f32), SSD_HEADDIM, axis=1)
    e3 = jnp.concatenate([expand, expand, expand], axis=0).astype(bf16)
    dskip = jnp.repeat(d, SSD_HEADDIM)[None, :]
    yf, yb = pl.pallas_call(
        functools.partial(_ssd_scan_kernel, ctx_chunks=ctx_chunks, lat_chunks=lat_chunks),
        grid=(bsz, ctx_chunks + lat_chunks),
        in_specs=dir_specs(False) + dir_specs(True) + [
            whole((SSD_CONV, SSD_CONV_DIM)), whole((1, SSD_CONV_DIM)),
            whole((2, nh)), whole((2, nh)), whole((2, nh, 1)), whole((2, nh, 1)),
            whole((3 * nh, SSD_D_INNER)), whole((1, SSD_D_INNER)),
        ],
        out_specs=[
            pl.BlockSpec((q, SSD_D_INNER), lambda b, j: (chunk_of(b, j, False), 0)),
            pl.BlockSpec((q, SSD_D_INNER), lambda b, j: (chunk_of(b, j, True), 0)),
        ],
        out_shape=[jax.ShapeDtypeStruct((n_tok, SSD_D_INNER), f32)] * 2,
        scratch_shapes=[pltpu.VMEM((2, SSD_GROUPS, SSD_STATE, SSD_GW), f32)],
        compiler_params=pltpu.CompilerParams(dimension_semantics=("arbitrary", "arbitrary"), vmem_limit_bytes=VMEM_LIMIT),
        name="ssd_scan",
    )(xbc, xbc, xbc, dt, dtt, xbc, xbc, xbc, dt, dtt,
      conv_w, conv_b.reshape(1, -1), dt_bias, a, dt_bias.reshape(2, nh, 1), a.reshape(2, nh, 1), e3, dskip)

    return pl.pallas_call(
        _ssd_out_kernel,
        grid=(n_tok // tm,),
        in_specs=[
            pl.BlockSpec((tm, SSD_D_INNER), lambda i: (i, 0)),
            pl.BlockSpec((tm, SSD_D_INNER), lambda i: (i, 0)),
            pl.BlockSpec((tm, SSD_D_INNER), lambda i: (i, 0)),
            pl.BlockSpec((tm, dm), lambda i: (i, 0)),
            pl.BlockSpec((1, 1, 6 * dm), lambda i: (mod_idx(i), 0, 0)),
            whole((1, SSD_D_INNER)), whole((SSD_D_INNER, dm)), whole((1, dm)), whole((1, dm)),
        ],
        out_specs=pl.BlockSpec((tm, dm), lambda i: (i, 0)),
        out_shape=jax.ShapeDtypeStruct((n_tok, dm), f32),
        compiler_params=params,
        name="ssd_out_ln",
    )(yf, yb, z, xa, mod3, norm_w.reshape(1, -1), w_out.astype(bf16), ln_g.reshape(1, dm), ln_b.reshape(1, dm))


def kernel(x, c, ctx, c_ctx, mod_w, mod_b, ln_g, ln_b,
           s5_a_re, s5_a_im, s5_log_dt, s5_b_re, s5_b_im, s5_c_re, s5_c_im, s5_d, s5_w_gate, s5_w_val,
           ssd_w_in, ssd_conv_w, ssd_conv_b, ssd_dt_bias, ssd_a_log, ssd_d, ssd_norm_w, ssd_w_out,
           mla_w_down, mla_q_norm, mla_kv_norm, mla_w_uq, mla_w_uk, mla_w_uv, mla_w_o,
           peer_w_q, peer_subkeys, peer_u, peer_v):
    ROWS = x.shape[1] // GRID_W
    pos = grid_positions(ROWS)
    ctx_len = ctx.shape[1]
    c_act = jax.nn.silu(c)
    c_ctx_act = jax.nn.silu(c_ctx)
    bsz, seq_len, dm = x.shape
    n_lat, n_ctx = bsz * seq_len, bsz * ctx_len
    xa = jnp.concatenate([x.reshape(n_lat, dm), ctx.reshape(n_ctx, dm)], axis=0)
    for i in range(DEPTH):
        last = i == DEPTH - 1
        mod_tab = jnp.concatenate([c_act, c_ctx_act[None], jnp.zeros((7 - bsz, dm), x.dtype)], axis=0) @ mod_w[i] + mod_b[i]
        mod3 = mod_tab.reshape(8, 1, 6 * dm)
        kind, j = i % N_MIXERS, i // N_MIXERS
        if kind == 0:
            prep = _s5_prep(s5_a_re[j], s5_a_im[j], s5_log_dt[j], s5_b_re[j], s5_b_im[j], s5_c_re[j], s5_c_im[j], s5_d[j])
            ya = s5_ssm_pallas(xa, mod_tab, n_lat, seq_len, bsz, prep)
            if last:
                ya, xa = ya[:n_lat], xa[:n_lat]
            xa = glu_ln_pallas(ya, xa, mod3, n_lat, seq_len, bsz, s5_w_val[j], s5_w_gate[j], ln_g[i, 0], ln_b[i, 0])
        elif kind == 1:
            xa = ssd_ln_pallas(xa, mod3, n_lat, seq_len, bsz, ssd_w_in[j], ssd_conv_w[j], ssd_conv_b[j], ssd_dt_bias[j],
                               ssd_a_log[j], ssd_d[j], ssd_norm_w[j], ssd_w_out[j], ln_g[i, 0], ln_b[i, 0])
            if last:
                xa = xa[:n_lat]
        else:
            xa = mla_ln_pallas(xa, mod3, n_lat, seq_len, bsz, mla_w_down[j], mla_q_norm[j], mla_kv_norm[j],
                               mla_w_uq[j], mla_w_uk[j], mla_w_uv[j], mla_w_o[j], ln_g[i, 0], ln_b[i, 0])
            if last:
                xa = xa[:n_lat]
        xa = peer_ln_pallas(xa, mod3, n_lat, seq_len, bsz, peer_w_q[i], peer_subkeys[i], peer_u[i], peer_v[i],
                            ln_g[i, 1], ln_b[i, 1])
    return xa[:n_lat].reshape(bsz, seq_len, dm)
```

```python
import math
import functools
import jax
import jax.numpy as jnp
from jax import lax
import numpy as np
from jax.experimental import pallas as pl
from jax.experimental.pallas import tpu as pltpu

D_MODEL = 1024
BATCH = 4
SEQ = 4096
DEPTH = 4

GRID_W = 64
CTX_LEN = 256
N_MIXERS = 3
DEEPNORM_ALPHA = (2.0 * DEPTH) ** 0.25
LN_EPS = 1e-5
RMS_EPS = 1e-6
ROPE_BASE = 10000.0

S5_GROUP = 16
S5_GROUPS = D_MODEL // S5_GROUP
S5_STATE = 64

SSD_D_INNER = 2 * D_MODEL
SSD_HEADDIM = 64
SSD_HEADS = SSD_D_INNER // SSD_HEADDIM
SSD_GROUPS = 4
SSD_STATE = 128
SSD_CONV = 3
SSD_CHUNK = 128
SSD_CONV_DIM = SSD_D_INNER + 2 * SSD_GROUPS * SSD_STATE
SSD_IN_DIM = SSD_D_INNER + SSD_CONV_DIM + 2 * SSD_HEADS

MLA_HEADS = 16
MLA_Q_RANK = 256
MLA_KV_RANK = 128
MLA_NOPE = 64
MLA_ROPE = 32
MLA_V = 64
MLA_BLOCK = 128

PEER_HEADS = 8
PEER_KEYS = 128
PEER_EXPERTS = PEER_KEYS * PEER_KEYS
PEER_QDIM = 256
PEER_TOPK = 16
PEER_BLOCK = 128


def layer_norm(x, g, b):
    xf = x.astype(jnp.float32)
    mu = jnp.mean(xf, axis=-1, keepdims=True)
    var = jnp.mean(jnp.square(xf - mu), axis=-1, keepdims=True)
    return ((xf - mu) * lax.rsqrt(var + LN_EPS)).astype(x.dtype) * g + b


def _ln_kernel(x_ref, g_ref, b_ref, o_ref):
    xf = x_ref[...]
    mu = jnp.mean(xf, axis=-1, keepdims=True)
    xc = xf - mu
    var = jnp.mean(xc * xc, axis=-1, keepdims=True)
    o_ref[...] = xc * lax.rsqrt(var + LN_EPS) * g_ref[...] + b_ref[...]


def layer_norm_pallas(x, g, b):
    shp = x.shape
    x2 = x.reshape(-1, shp[-1])
    n, d = x2.shape
    tb = 512
    out = pl.pallas_call(
        _ln_kernel,
        grid=(n // tb,),
        in_specs=[pl.BlockSpec((tb, d), lambda i: (i, 0)),
                  pl.BlockSpec((1, d), lambda i: (0, 0)),
                  pl.BlockSpec((1, d), lambda i: (0, 0))],
        out_specs=pl.BlockSpec((tb, d), lambda i: (i, 0)),
        out_shape=jax.ShapeDtypeStruct((n, d), jnp.float32),
        name="final_ln",
    )(x2, g.reshape(1, d), b.reshape(1, d))
    return out.reshape(shp)


def rms_norm(x, g):
    xf = x.astype(jnp.float32)
    return (xf * lax.rsqrt(jnp.mean(jnp.square(xf), axis=-1, keepdims=True) + RMS_EPS)).astype(x.dtype) * g


def modulate(x, shift, scale):
    return x * (1.0 + scale) + shift


def grid_positions(rows):
    row = jnp.repeat(jnp.arange(rows, dtype=jnp.float32), GRID_W)
    col = jnp.tile(jnp.arange(GRID_W, dtype=jnp.float32), rows)
    return row, col


def rope_axial(x, row, col):
    half = x.shape[-1] // 2
    quarter = half // 2
    freqs = ROPE_BASE ** (-jnp.arange(quarter, dtype=jnp.float32) / quarter)

    def rot(xp, pos):
        ang = pos[:, None] * freqs
        cos = jnp.cos(ang)[None, :, None, :].astype(x.dtype)
        sin = jnp.sin(ang)[None, :, None, :].astype(x.dtype)
        x1, x2 = xp[..., :quarter], xp[..., quarter:]
        return jnp.concatenate([x1 * cos - x2 * sin, x2 * cos + x1 * sin], axis=-1)

    return jnp.concatenate([rot(x[..., :half], row), rot(x[..., half:], col)], axis=-1)


def _lin_rec(left, right):
    a1, b1 = left
    a2, b2 = right
    return a1 * a2, a2 * b1 + b2


def s5_scan(u, a_bar, b_bar, c_mat, init, reverse):
    bu = jnp.einsum('lgc,gpc->lgp', u.astype(jnp.float32), b_bar)
    a = jnp.broadcast_to(a_bar, bu.shape)
    a_cum, s = lax.associative_scan(_lin_rec, (a, bu), reverse=reverse, axis=0)
    s = s + a_cum * init
    y = jnp.einsum('lgp,gcp->lgc', s, c_mat).real
    final = s[0] if reverse else s[-1]
    return y, final


def s5_direction(u_ctx, u_lat, a_re, a_im, log_dt, b_re, b_im, c_re, c_im, reverse):
    lam = lax.complex(a_re.astype(jnp.float32), a_im.astype(jnp.float32))
    a_bar = jnp.exp(lam * jnp.exp(log_dt.astype(jnp.float32))[:, None])
    b_mat = lax.complex(b_re.astype(jnp.float32), b_im.astype(jnp.float32))
    b_bar = ((a_bar - 1.0) / lam)[..., None] * b_mat
    c_mat = lax.complex(c_re.astype(jnp.float32), c_im.astype(jnp.float32))

    def per_sample(args):
        uc, ul = args
        yc, sc = s5_scan(uc, a_bar, b_bar, c_mat, jnp.zeros_like(a_bar), reverse)
        yl, _ = s5_scan(ul, a_bar, b_bar, c_mat, sc, reverse)
        return yc, yl

    return lax.map(per_sample, (u_ctx, u_lat))


def s5_mixer(h, hc, a_re, a_im, log_dt, b_re, b_im, c_re, c_im, d, w_gate, w_val):
    bsz, seq_len, _ = h.shape
    ctx_len = hc.shape[1]
    u = h.reshape(bsz, seq_len, S5_GROUPS, S5_GROUP)
    uc = hc.reshape(bsz, ctx_len, S5_GROUPS, S5_GROUP)
    y_l = d * h
    y_c = d * hc
    for direction in range(2):
        yc, yl = s5_direction(uc, u, a_re[direction], a_im[direction], log_dt[direction],
                              b_re[direction], b_im[direction], c_re[direction], c_im[direction],
                              direction == 1)
        y_l = y_l + yl.reshape(bsz, seq_len, D_MODEL).astype(h.dtype)
        y_c = y_c + yc.reshape(bsz, ctx_len, D_MODEL).astype(h.dtype)

    def glu(y):
        g = jax.nn.gelu(y)
        return (g @ w_val) * jax.nn.sigmoid(g @ w_gate)

    return glu(y_l), glu(y_c)


def depthwise_conv_centred(x, w, b):
    k_w = w.shape[0]
    pad = k_w // 2
    seq_len = x.shape[1]
    xp = jnp.pad(x, ((0, 0), (pad, pad), (0, 0)))
    out = b
    for k in range(k_w):
        out = out + xp[:, k:k + seq_len] * w[k]
    return out


def ssd_scan(x, dt, a, bm, cm, init):
    bsz, seq_len, n_heads, p_dim = x.shape
    n_grp, n_st = bm.shape[2], bm.shape[3]
    hg = n_heads // n_grp
    q_len = SSD_CHUNK
    nc = seq_len // q_len
    xc = x.reshape(bsz, nc, q_len, n_grp, hg, p_dim)
    dtc = dt.reshape(bsz, nc, q_len, n_grp, hg).astype(jnp.float32)
    bc = bm.reshape(bsz, nc, q_len, n_grp, n_st)
    cc = cm.reshape(bsz, nc, q_len, n_grp, n_st)
    a_cum = jnp.cumsum(dtc * a.reshape(n_grp, hg), axis=2)
    xdt = xc * dtc[..., None].astype(x.dtype)
    tri = jnp.tril(jnp.ones((q_len, q_len), dtype=bool))
    seg = a_cum[:, :, :, None] - a_cum[:, :, None]
    decay = jnp.exp(jnp.where(tri[:, :, None, None], seg, -jnp.inf)).astype(x.dtype)
    cb = jnp.einsum('bcqgn,bcsgn->bcgqs', cc, bc)
    y_diag = jnp.einsum('bcgqs,bcqsgh,bcsghp->bcqghp', cb, decay, xdt)
    decay_states = jnp.exp(a_cum[:, :, -1:] - a_cum).astype(x.dtype)
    states = jnp.einsum('bcsgn,bcsgh,bcsghp->bcghpn', bc, decay_states, xdt)
    chunk_decay = jnp.exp(a_cum[:, :, -1]).astype(x.dtype)

    def step(carry, inp):
        dec, st = inp
        return carry * dec[..., None, None] + st, carry

    final, prev = lax.scan(step, init, (jnp.moveaxis(chunk_decay, 1, 0), jnp.moveaxis(states, 1, 0)))
    y_off = jnp.einsum('bcqgn,cbghpn,bcqgh->bcqghp', cc, prev, jnp.exp(a_cum).astype(x.dtype))
    return (y_diag + y_off).reshape(bsz, seq_len, n_heads, p_dim), final


def ssd_mixer(h, hc, w_in, conv_w, conv_b, dt_bias, a_log, d, norm_w, w_out):
    a = -jnp.exp(a_log.astype(jnp.float32))

    def flip(t):
        return jnp.flip(t, axis=1)

    def bidir(t, inits):
        bsz, seq_len = t.shape[0], t.shape[1]
        z, xbc, dt = jnp.split(t @ w_in, [SSD_D_INNER, SSD_D_INNER + SSD_CONV_DIM], axis=-1)
        xbc = jax.nn.silu(depthwise_conv_centred(xbc, conv_w, conv_b))
        xs, bm, cm = jnp.split(xbc, [SSD_D_INNER, SSD_D_INNER + SSD_GROUPS * SSD_STATE], axis=-1)
        xs = xs.reshape(bsz, seq_len, SSD_HEADS, SSD_HEADDIM)
        bm = bm.reshape(bsz, seq_len, SSD_GROUPS, SSD_STATE)
        cm = cm.reshape(bsz, seq_len, SSD_GROUPS, SSD_STATE)
        dt = jax.nn.softplus(dt.reshape(bsz, seq_len, 2, SSD_HEADS) + dt_bias)
        y_f, s_f = ssd_scan(xs, dt[:, :, 0], a[0], bm, cm, inits[0])
        y_b, s_b = ssd_scan(flip(xs), flip(dt[:, :, 1]), a[1], flip(bm), flip(cm), inits[1])
        y = y_f + flip(y_b) + d[:, None] * xs
        y = rms_norm(y.reshape(z.shape) * jax.nn.silu(z), norm_w)
        return y @ w_out, (s_f, s_b)

    zero = jnp.zeros((h.shape[0], SSD_GROUPS, SSD_HEADS // SSD_GROUPS, SSD_HEADDIM, SSD_STATE), h.dtype)
    out_c, states_c = bidir(hc, (zero, zero))
    out_l, _ = bidir(h, states_c)
    return out_l, out_c


def mla_project(t, w_down, q_norm, kv_norm, w_uq, w_uk, w_uv, pos):
    cq, ckv, kr = jnp.split(t @ w_down, [MLA_Q_RANK, MLA_Q_RANK + MLA_KV_RANK], axis=-1)
    q = jnp.einsum('btr,rhd->bthd', rms_norm(cq, q_norm), w_uq)
    ckv = rms_norm(ckv, kv_norm)
    k_nope = jnp.einsum('btr,rhd->bthd', ckv, w_uk)
    v = jnp.einsum('btr,rhd->bthd', ckv, w_uv)
    q_nope, q_rope = q[..., :MLA_NOPE], q[..., MLA_NOPE:]
    kr = kr[:, :, None, :]
    if pos is not None:
        q_rope = rope_axial(q_rope, pos[0], pos[1])
        kr = rope_axial(kr, pos[0], pos[1])
    k = jnp.concatenate([k_nope, jnp.broadcast_to(kr, k_nope.shape[:3] + (MLA_ROPE,))], axis=-1)
    q = jnp.concatenate([q_nope, q_rope], axis=-1)
    return q, k, v


def attend(q, k, v):
    s = jnp.einsum('bqhd,bkhd->bhqk', q, k).astype(jnp.float32) * (MLA_NOPE + MLA_ROPE) ** -0.5
    p = jax.nn.softmax(s, axis=-1).astype(v.dtype)
    return jnp.einsum('bhqk,bkhd->bqhd', p, v)


def mla_mixer(h, hc, pos, w_down, q_norm, kv_norm, w_uq, w_uk, w_uv, w_o):
    bsz, seq_len, _ = h.shape
    qc, kc, vc = mla_project(hc, w_down, q_norm, kv_norm, w_uq, w_uk, w_uv, None)
    ql, kl, vl = mla_project(h, w_down, q_norm, kv_norm, w_uq, w_uk, w_uv, pos)
    out_c = attend(qc, kc, vc).reshape(bsz, hc.shape[1], MLA_HEADS * MLA_V)
    k_all = jnp.concatenate([kc, kl], axis=1)
    v_all = jnp.concatenate([vc, vl], axis=1)
    nb = seq_len // MLA_BLOCK
    qb = jnp.moveaxis(ql.reshape(bsz, nb, MLA_BLOCK, MLA_HEADS, MLA_NOPE + MLA_ROPE), 1, 0)
    out_l = lax.map(lambda qblk: attend(qblk, k_all, v_all), qb)
    out_l = jnp.moveaxis(out_l, 0, 1).reshape(bsz, seq_len, MLA_HEADS * MLA_V)
    return out_l @ w_o, out_c @ w_o


def peer_ffn(h, w_q, subkeys, u_tab, v_tab):
    bsz, seq_len, dm = h.shape
    half = PEER_QDIM // 2
    q = jnp.einsum('btd,dhk->bthk', h, w_q)
    s1 = jnp.einsum('bthk,hnk->bthn', q[..., :half], subkeys[:, 0])
    s2 = jnp.einsum('bthk,hnk->bthn', q[..., half:], subkeys[:, 1])
    v1, i1 = lax.top_k(s1, PEER_TOPK)
    v2, i2 = lax.top_k(s2, PEER_TOPK)
    n_cand = PEER_TOPK * PEER_TOPK
    cand = (v1[..., :, None] + v2[..., None, :]).reshape(bsz, seq_len, PEER_HEADS, n_cand)
    cidx = (i1[..., :, None] * PEER_KEYS + i2[..., None, :]).reshape(bsz, seq_len, PEER_HEADS, n_cand)
    best, sel = lax.top_k(cand, PEER_TOPK)
    eidx = jnp.take_along_axis(cidx, sel, axis=-1)
    gate = jax.nn.softmax(best.astype(jnp.float32), axis=-1).astype(h.dtype)
    n_blk = (bsz * seq_len) // PEER_BLOCK
    n_sel = PEER_HEADS * PEER_TOPK
    hb = h.reshape(n_blk, PEER_BLOCK, dm)
    ib = eidx.reshape(n_blk, PEER_BLOCK, n_sel)
    gb = gate.reshape(n_blk, PEER_BLOCK, n_sel)

    def block(args):
        hk, ik, gk = args
        u = jnp.take(u_tab, ik, axis=0)
        act = jax.nn.gelu(jnp.einsum('td,ted->te', hk, u))
        v = jnp.take(v_tab, ik, axis=0)
        return jnp.einsum('te,ted->td', gk * act, v)

    return lax.map(block, (hb, ib, gb)).reshape(bsz, seq_len, dm)


PEER_TM = 512
PEER_EB = 1024
PEER_ROWS = 16
PEER_LANES = 128
PEER_ACT_ROWS = 128
VMEM_LIMIT = 56 * 1024 * 1024
NEG_INF = float("-inf")
GELU_C0 = math.sqrt(2.0 / math.pi)
GELU_C1 = 0.044715 * GELU_C0


def _top16_sorted(s):
    n = PEER_TOPK
    m = s.shape[0] // 8
    x = [s[8 * r:8 * r + 8, :] for r in range(m)]

    def exchange(i, l, descending):
        hi, lo = jnp.maximum(x[i], x[l]), jnp.minimum(x[i], x[l])
        x[i], x[l] = (hi, lo) if descending else (lo, hi)

    def merge_bitonic(size):
        j = size // 2
        while j >= 1:
            for i in range(size):
                if i ^ j > i:
                    exchange(i, i ^ j, True)
            j //= 2

    k = 2
    while k < m:
        j = k // 2
        while j >= 1:
            for i in range(m):
                if i ^ j > i:
                    exchange(i, i ^ j, (i & k) == 0)
            j //= 2
        k *= 2
    merge_bitonic(m)
    shifts = [4, 2, 1]
    if m < n:
        first = shifts.pop(0)
        x = x + [pltpu.roll(x[m - 1 - r], first, axis=0) for r in range(m)]
        merge_bitonic(n)
    for shift in shifts:
        other = [pltpu.roll(x[n - 1 - r], shift, axis=0) for r in range(n)]
        for r in range(n):
            x[r] = jnp.maximum(x[r], other[r])
        merge_bitonic(n)
    return [x[r][0:1, :] for r in range(n)]


def _dot3(a_hi, a_lo, b_hi, b_lo):
    f32 = jnp.float32
    return (jnp.dot(a_hi, b_hi, preferred_element_type=f32)
            + jnp.dot(a_hi, b_lo, preferred_element_type=f32)
            + jnp.dot(a_lo, b_hi, preferred_element_type=f32))


def _split_bf16(v):
    hi = v.astype(jnp.bfloat16)
    lo = (v - hi.astype(jnp.float32)).astype(jnp.bfloat16)
    return hi, lo


def _peer_kernel(x_ref, mod_ref, wq_ref, skh_ref, skl_ref, u_ref, vt_ref, vtl_ref, lng_ref, lnb_ref,
                 o_ref,
                 ht_ref, s_ref, n1_ref, e1_ref, r2_ref, e2_ref, v1_ref, v2_ref, cand_ref,
                 act_ref, gact_ref, acc_ref):
    f32 = jnp.float32
    bf16 = jnp.bfloat16
    eb = pl.program_id(1)
    n_eb = pl.num_programs(1)
    dm = x_ref.shape[1]
    a_per = PEER_EB // PEER_KEYS
    half = PEER_QDIM // 2

    @pl.when(eb == 0)
    def _prologue():
        mod = mod_ref[0]
        h = x_ref[...] * (1.0 + mod[:, 4 * dm:5 * dm]) + mod[:, 3 * dm:4 * dm]
        ht = h.T.astype(bf16)
        ht_ref[...] = ht
        qt = jnp.dot(wq_ref[...], ht, preferred_element_type=f32)
        cand_ref[...] = jnp.full(cand_ref.shape, NEG_INF, f32)
        gact_ref[...] = jnp.zeros(gact_ref.shape, bf16)
        for hd in range(PEER_HEADS):
            q1_hi, q1_lo = _split_bf16(qt[hd * PEER_QDIM: hd * PEER_QDIM + half])
            q2_hi, q2_lo = _split_bf16(qt[hd * PEER_QDIM + half: (hd + 1) * PEER_QDIM])
            s_ref[0] = _dot3(skh_ref[hd, 0], skl_ref[hd, 0], q1_hi, q1_lo)
            s_ref[1] = _dot3(skh_ref[hd, 1], skl_ref[hd, 1], q2_hi, q2_lo)
            for lg in range(0, x_ref.shape[0], PEER_LANES):
                lanes = slice(lg, lg + PEER_LANES)
                s1, s2 = s_ref[0, :, lanes], s_ref[1, :, lanes]
                top1 = _top16_sorted(s1)
                top2 = _top16_sorted(s2)
                for k in range(PEER_TOPK):
                    v1_ref[k:k + 1, lanes] = top1[k]
                    v2_ref[k:k + 1, lanes] = top2[k]
                off = 0
                for i in range(PEER_TOPK):
                    cnt = PEER_TOPK // (i + 1)
                    cand_ref[off:off + cnt, lanes] = v1_ref[i:i + 1, lanes] + v2_ref[0:cnt, lanes]
                    off += cnt
                best = _top16_sorted(cand_ref[:, lanes])
                z = jnp.zeros_like(best[0])
                for k in range(PEER_TOPK):
                    z = z + jnp.exp(best[k] - best[0])
                tau = best[PEER_TOPK - 1]
                n1 = jnp.zeros(s1.shape, f32)
                for k in range(PEER_TOPK):
                    n1 = jnp.where(s1 + top2[k] >= tau, k + 1.0, n1)
                n1_ref[hd, :, lanes] = n1
                e1_ref[hd, :, lanes] = jnp.exp(s1 - top1[0])
                rank2 = jnp.zeros(s2.shape, f32)
                for k in range(PEER_TOPK):
                    rank2 = jnp.where(top2[k] > s2, k + 1.0, rank2)
                r2_ref[hd, :, lanes] = rank2.astype(bf16)
                e2_ref[hd, :, lanes] = (jnp.exp(s2 - top2[0]) * (0.5 / z)).astype(bf16)
        acc_ref[...] = jnp.zeros(acc_ref.shape, f32)

    par = eb % 2
    for r0 in range(0, PEER_EB, PEER_ACT_ROWS):
        act_ref[r0:r0 + PEER_ACT_ROWS, :] = jnp.dot(u_ref[r0:r0 + PEER_ACT_ROWS, :], ht_ref[...],
                                                    preferred_element_type=f32)
    acc_ref[...] += jnp.dot(vt_ref[0], gact_ref[1 - par], preferred_element_type=f32)
    a_base = pl.multiple_of(eb * a_per, a_per)
    blk = (PEER_ROWS, x_ref.shape[0])
    n_slab = PEER_KEYS // PEER_ROWS
    for a in range(a_per):
        g = [None] * n_slab
        for hd in range(PEER_HEADS):
            n1a = jnp.broadcast_to(n1_ref[hd, pl.ds(a_base, a_per), :][a:a + 1, :].astype(bf16), blk)
            e1a = jnp.broadcast_to(e1_ref[hd, pl.ds(a_base, a_per), :][a:a + 1, :].astype(bf16), blk)
            for i in range(n_slab):
                rs = slice(i * PEER_ROWS, (i + 1) * PEER_ROWS)
                w = jnp.where(r2_ref[hd, rs, :] < n1a, e2_ref[hd, rs, :], jnp.zeros(blk, bf16)) * e1a
                g[i] = w if g[i] is None else g[i] + w
        for i in range(n_slab):
            rows = slice(a * PEER_KEYS + i * PEER_ROWS, a * PEER_KEYS + (i + 1) * PEER_ROWS)
            x = act_ref[rows, :]
            t = jnp.tanh(x * (GELU_C0 + GELU_C1 * (x * x)))
            gact_ref[par, rows, :] = g[i] * (x + x * t).astype(bf16)

    @pl.when(eb == n_eb - 1)
    def _epilogue():
        mod = mod_ref[0]
        ffn_t = acc_ref[...] + jnp.dot(vtl_ref[0], gact_ref[par], preferred_element_type=f32)
        y = DEEPNORM_ALPHA * x_ref[...] + mod[:, 5 * dm:6 * dm] * ffn_t.T
        mu = jnp.mean(y, axis=-1, keepdims=True)
        yc = y - mu
        var = jnp.mean(yc * yc, axis=-1, keepdims=True)
        o_ref[...] = yc * lax.rsqrt(var + LN_EPS) * lng_ref[...] + lnb_ref[...]


def _mod_row_index(i, tm, n_lat, seq, bsz):
    return jnp.where(i * tm < n_lat, (i * tm) // seq, bsz)


def peer_ln_pallas(xa, mod3, n_lat, seq, bsz, w_q, subkeys, u_tab, v_tab, ln_g, ln_b):
    n_tok, dm = xa.shape
    tm = PEER_TM
    f32, bf16 = jnp.float32, jnp.bfloat16
    wq_t = w_q.reshape(dm, PEER_HEADS * PEER_QDIM).T
    wq_bf = wq_t.astype(bf16)
    sk_hi = subkeys.astype(bf16)
    sk_lo = (subkeys - sk_hi.astype(f32)).astype(bf16)
    u_bf = u_tab.astype(bf16)
    n_eb = PEER_EXPERTS // PEER_EB
    vt_bf = v_tab.reshape(n_eb, PEER_EB, dm).transpose(0, 2, 1).astype(bf16)
    qd = PEER_HEADS * PEER_QDIM
    n_cand = sum(PEER_TOPK // (i + 1) for i in range(PEER_TOPK))
    n_cand_pad = 64
    assert n_cand <= n_cand_pad
    mod_idx = functools.partial(_mod_row_index, tm=tm, n_lat=n_lat, seq=seq, bsz=bsz)
    return pl.pallas_call(
        _peer_kernel,
        grid=(n_tok // tm, n_eb),
        in_specs=[
            pl.BlockSpec((tm, dm), lambda i, e: (i, 0)),
            pl.BlockSpec((1, 1, 6 * dm), lambda i, e: (mod_idx(i), 0, 0)),
            pl.BlockSpec((qd, dm), lambda i, e: (0, 0)),
            pl.BlockSpec((PEER_HEADS, 2, PEER_KEYS, PEER_QDIM // 2), lambda i, e: (0, 0, 0, 0)),
            pl.BlockSpec((PEER_HEADS, 2, PEER_KEYS, PEER_QDIM // 2), lambda i, e: (0, 0, 0, 0)),
            pl.BlockSpec((PEER_EB, dm), lambda i, e: (e, 0)),
            pl.BlockSpec((1, dm, PEER_EB), lambda i, e: (jnp.maximum(e - 1, 0), 0, 0)),
            pl.BlockSpec((1, dm, PEER_EB), lambda i, e: (n_eb - 1, 0, 0)),
            pl.BlockSpec((1, dm), lambda i, e: (0, 0)),
            pl.BlockSpec((1, dm), lambda i, e: (0, 0)),
        ],
        out_specs=pl.BlockSpec((tm, dm), lambda i, e: (i, 0)),
        out_shape=jax.ShapeDtypeStruct((n_tok, dm), f32),
        scratch_shapes=[
            pltpu.VMEM((dm, tm), bf16),
            pltpu.VMEM((2, PEER_KEYS, tm), f32),
            pltpu.VMEM((PEER_HEADS, PEER_KEYS, tm), f32),
            pltpu.VMEM((PEER_HEADS, PEER_KEYS, tm), f32),
            pltpu.VMEM((PEER_HEADS, PEER_KEYS, tm), bf16),
            pltpu.VMEM((PEER_HEADS, PEER_KEYS, tm), bf16),
            pltpu.VMEM((PEER_TOPK, tm), f32),
            pltpu.VMEM((PEER_TOPK, tm), f32),
            pltpu.VMEM((n_cand_pad, tm), f32),
            pltpu.VMEM((PEER_EB, tm), f32),
            pltpu.VMEM((2, PEER_EB, tm), bf16),
            pltpu.VMEM((dm, tm), f32),
        ],
        compiler_params=pltpu.CompilerParams(
            dimension_semantics=("arbitrary", "arbitrary"),
            vmem_limit_bytes=VMEM_LIMIT),
        name="peer_ln",
    )(xa, mod3, wq_bf, sk_hi, sk_lo, u_bf, vt_bf, vt_bf, ln_g.reshape(1, dm), ln_b.reshape(1, dm))


S5_CHUNK = 16
S5_LEVELS = 9


def _s5_prep(a_re, a_im, log_dt, b_re, b_im, c_re, c_im, d):
    f32, bf16 = jnp.float32, jnp.bfloat16
    hp = lax.Precision.HIGHEST
    n_g, n_p, q = S5_GROUPS, S5_STATE, S5_CHUNK
    dt = jnp.exp(log_dt.astype(f32))[..., None]

    def apow(n):
        mag = jnp.exp(a_re * dt * n)
        ang = a_im * dt * n
        return mag * jnp.cos(ang), mag * jnp.sin(ang)

    ar1, ai1 = apow(1.0)
    den = a_re * a_re + a_im * a_im
    nr, ni = ar1 - 1.0, ai1
    cr = (nr * a_re + ni * a_im) / den
    ci = (ni * a_re - nr * a_im) / den
    bbr = cr[..., None] * b_re - ci[..., None] * b_im
    bbi = cr[..., None] * b_im + ci[..., None] * b_re
    lag = jnp.arange(q + 1, dtype=f32)[:, None, None, None]
    pr, pi = apow(lag)
    mr = pr[..., None] * bbr - pi[..., None] * bbi
    mi = pr[..., None] * bbi + pi[..., None] * bbr
    kmat = (jnp.einsum('dgcp,ndgpk->ndgck', c_re, mr, precision=hp)
            - jnp.einsum('dgcp,ndgpk->ndgck', c_im, mi, precision=hp))
    r_idx = jnp.arange(q)[:, None]
    t_idx = jnp.arange(q)[None, :]

    def toeplitz(kd, lagm):
        blk = kd[jnp.clip(lagm, 0, q)]
        blk = jnp.where((lagm >= 0)[:, :, None, None, None], blk, 0.0)
        return blk.transpose(2, 0, 4, 1, 3).reshape(n_g, q * S5_GROUP, q * S5_GROUP)

    eye = jnp.eye(q * S5_GROUP, dtype=f32)
    dvec = jnp.tile(d.reshape(n_g, 1, S5_GROUP), (1, q, 1)).reshape(n_g, q * S5_GROUP)
    tsum = toeplitz(kmat[:, 0], t_idx - r_idx) + toeplitz(kmat[:, 1], r_idx - t_idx) + eye[None] * dvec[:, None, :]

    def w_in(direction, exps):
        wr = mr[exps, direction].transpose(1, 0, 3, 2).reshape(n_g, q * S5_GROUP, n_p)
        wi = mi[exps, direction].transpose(1, 0, 3, 2).reshape(n_g, q * S5_GROUP, n_p)
        return wr, wi

    def w_out(direction, exps):
        pre, pim = pr[exps, direction], pi[exps, direction]
        cre, cim = c_re[direction], c_im[direction]
        wre = cre[None] * pre[:, :, None, :] - cim[None] * pim[:, :, None, :]
        wim = -(cre[None] * pim[:, :, None, :] + cim[None] * pre[:, :, None, :])
        return (wre.transpose(1, 3, 0, 2).reshape(n_g, n_p, q * S5_GROUP),
                wim.transpose(1, 3, 0, 2).reshape(n_g, n_p, q * S5_GROUP))

    steps = jnp.arange(q)
    win = [w_in(0, q - 1 - steps), w_in(1, steps)]
    wout = [w_out(0, steps + 1), w_out(1, q - steps)]
    zc = jnp.zeros((n_g, q * S5_GROUP, n_p), f32)
    zr = jnp.zeros((n_g, n_p, q * S5_GROUP), f32)
    win_p, wout_p = [], []
    for direction in range(2):
        wr, wi = win[direction]
        even = jnp.concatenate([wr, zc, wi, zc], axis=2)
        odd = jnp.concatenate([zc, wr, zc, wi], axis=2)
        is_odd = (jnp.arange(n_g) % 2 == 1)[:, None, None]
        win_p.append(jnp.where(is_odd, odd, even))
        vr, vi = wout[direction]
        even = jnp.concatenate([vr, zr, vi, zr], axis=1)
        odd = jnp.concatenate([zr, vr, zr, vi], axis=1)
        wout_p.append(jnp.where(is_odd, odd, even))
    win_p = jnp.stack(win_p, axis=1).reshape(n_g // 2, 2, 2, 4 * n_p, 4 * n_p).transpose(0, 2, 1, 3, 4)
    wout_p = jnp.stack(wout_p, axis=1).reshape(n_g // 2, 2, 2, 4 * n_p, 4 * n_p).transpose(0, 2, 1, 3, 4)
    lvl = (q * 2.0 ** jnp.arange(S5_LEVELS, dtype=f32))[:, None, None, None]
    lr, li = apow(lvl)
    pw = jnp.stack([lr, li], axis=1)
    pw = pw.transpose(3, 2, 0, 1, 4).reshape(n_g // 2, 2, 2, S5_LEVELS, 2, n_p)
    pw = pw.transpose(0, 2, 3, 4, 1, 5).reshape(n_g // 2, 2, 2 * S5_LEVELS, 2 * n_p)
    return tsum.astype(bf16).reshape(n_g // 2, 2, q * S5_GROUP, q * S5_GROUP), win_p.astype(bf16), wout_p.astype(bf16), pw


def _shift_rows(v, r, up):
    n = v.shape[0]
    row = lax.broadcasted_iota(jnp.int32, v.shape, 0)
    if up:
        return jnp.where(row < n - r, pltpu.roll(v, n - r, axis=0), 0.0)
    return jnp.where(row >= r, pltpu.roll(v, r, axis=0), 0.0)


def _chunk_scan(re, im, pw, rows_per_chunk, up):
    n_chunks = re.shape[0] // rows_per_chunk
    level, s = 0, 1
    while s < n_chunks:
        ar, ai = pw[2 * level:2 * level + 1, :], pw[2 * level + 1:2 * level + 2, :]
        sre = _shift_rows(re, s * rows_per_chunk, up)
        sim = _shift_rows(im, s * rows_per_chunk, up)
        re, im = re + ar * sre - ai * sim, im + ar * sim + ai * sre
        level, s = level + 1, 2 * s
    return re, im


def _s5_kernel(u_ref, scl_ref, shl_ref, scc_ref, shc_ref, t_ref, win_ref, wout_ref, pw_ref, y_ref, *, bsz, ctx_chunks):
    f32, bf16 = jnp.float32, jnp.bfloat16
    n = u_ref.shape[1]
    width = u_ref.shape[2]
    rc = ctx_chunks * bsz
    hs = []
    for gi in range(2):
        xv = u_ref[gi].reshape(n // 8, 8, width)
        hl = xv * scl_ref[gi][None] + shl_ref[gi][None]
        hc = xv * scc_ref[gi][None] + shc_ref[gi][None]
        slab = lax.broadcasted_iota(jnp.int32, xv.shape, 0)
        hs.append(jnp.where(slab < rc // 8, hc, hl).reshape(n, width).astype(bf16))
    y = [jnp.dot(hs[gi], t_ref[0, gi], preferred_element_type=f32) for gi in range(2)]
    half = width // 2
    for direction in range(2):
        sloc = (jnp.dot(hs[0], win_ref[0, direction, 0], preferred_element_type=f32)
                + jnp.dot(hs[1], win_ref[0, direction, 1], preferred_element_type=f32))
        re, im = sloc[:, :half], sloc[:, half:]
        pw = pw_ref[0, direction]
        if direction == 0:
            sre, sim = _chunk_scan(_shift_rows(re, bsz, False), _shift_rows(im, bsz, False), pw, bsz, False)
        else:
            cre, cim = _chunk_scan(_shift_rows(re[:rc], bsz, True), _shift_rows(im[:rc], bsz, True), pw, bsz, True)
            ar, ai = pw[0:1, :], pw[1:2, :]
            fre = ar * cre[0:8] - ai * cim[0:8] + re[0:8]
            fim = ar * cim[0:8] + ai * cre[0:8] + im[0:8]
            row8 = lax.broadcasted_iota(jnp.int32, fre.shape, 0)
            tre = jnp.where(row8 >= 8 - bsz, pltpu.roll(fre, 8 - bsz, axis=0), 0.0)
            tim = jnp.where(row8 >= 8 - bsz, pltpu.roll(fim, 8 - bsz, axis=0), 0.0)
            lre, lim = _shift_rows(re[rc:], bsz, True), _shift_rows(im[rc:], bsz, True)
            lre = jnp.concatenate([lre[:-8], lre[-8:] + tre], axis=0)
            lim = jnp.concatenate([lim[:-8], lim[-8:] + tim], axis=0)
            lre, lim = _chunk_scan(lre, lim, pw, bsz, True)
            sre = jnp.concatenate([cre, lre], axis=0)
            sim = jnp.concatenate([cim, lim], axis=0)
        s_in = jnp.concatenate([sre, sim], axis=1).astype(bf16)
        for gi in range(2):
            y[gi] = y[gi] + jnp.dot(s_in, wout_ref[0, direction, gi], preferred_element_type=f32)
    for gi in range(2):
        y_ref[gi] = y[gi]


def s5_ssm_pallas(xa, mod_tab, n_lat, seq, bsz, prep):
    tsum, win_p, wout_p, pw = prep
    f32 = jnp.float32
    n_tok, dm = xa.shape
    ctx_len = (n_tok - n_lat) // bsz
    q = S5_CHUNK
    n_chunks = (seq + ctx_len) // q
    width = q * S5_GROUP
    full = jnp.concatenate([xa[n_lat:].reshape(bsz, ctx_len, dm), xa[:n_lat].reshape(bsz, seq, dm)], axis=1)
    u = full.reshape(bsz, n_chunks, q, S5_GROUPS, S5_GROUP).transpose(3, 1, 0, 2, 4).reshape(S5_GROUPS, n_chunks * bsz, width)

    def tile(vec_rows):
        rows = vec_rows.shape[0]
        t = vec_rows.reshape(rows, S5_GROUPS, 1, S5_GROUP)
        t = jnp.broadcast_to(t, (rows, S5_GROUPS, q, S5_GROUP)).reshape(rows, S5_GROUPS, width)
        return jnp.tile(t.transpose(1, 0, 2), (1, 8 // rows, 1))

    scl, shl = tile(1.0 + mod_tab[:bsz, dm:2 * dm]), tile(mod_tab[:bsz, 0:dm])
    scc = tile(jnp.broadcast_to(1.0 + mod_tab[bsz:bsz + 1, dm:2 * dm], (bsz, dm)))
    shc = tile(jnp.broadcast_to(mod_tab[bsz:bsz + 1, 0:dm], (bsz, dm)))
    n_rows = n_chunks * bsz
    tile_spec = pl.BlockSpec((2, 8, width), lambda p: (p, 0, 0))
    y = pl.pallas_call(
        functools.partial(_s5_kernel, bsz=bsz, ctx_chunks=ctx_len // q),
        grid=(S5_GROUPS // 2,),
        in_specs=[
            pl.BlockSpec((2, n_rows, width), lambda p: (p, 0, 0)),
            tile_spec, tile_spec, tile_spec, tile_spec,
            pl.BlockSpec((1, 2, width, width), lambda p: (p, 0, 0, 0)),
            pl.BlockSpec((1, 2, 2, width, width), lambda p: (p, 0, 0, 0, 0)),
            pl.BlockSpec((1, 2, 2, width, width), lambda p: (p, 0, 0, 0, 0)),
            pl.BlockSpec((1, 2, 2 * S5_LEVELS, 2 * S5_STATE), lambda p: (p, 0, 0, 0)),
        ],
        out_specs=pl.BlockSpec((2, n_rows, width), lambda p: (p, 0, 0)),
        out_shape=jax.ShapeDtypeStruct((S5_GROUPS, n_rows, width), f32),
        compiler_params=pltpu.CompilerParams(dimension_semantics=("arbitrary",), vmem_limit_bytes=VMEM_LIMIT),
        name="s5_ssm",
    )(u, scl, shl, scc, shc, tsum, win_p, wout_p, pw)
    yf = y.reshape(S5_GROUPS, n_chunks, bsz, q, S5_GROUP).transpose(2, 1, 3, 0, 4).reshape(bsz, seq + ctx_len, dm)
    return jnp.concatenate([yf[:, ctx_len:].reshape(n_lat, dm), yf[:, :ctx_len].reshape(bsz * ctx_len, dm)], axis=0)


def _glu_ln_kernel(y_ref, x_ref, mod_ref, wv_ref, wg_ref, lng_ref, lnb_ref, o_ref):
    f32 = jnp.float32
    dm = x_ref.shape[1]
    g = jax.nn.gelu(y_ref[...]).astype(jnp.bfloat16)
    val = jnp.dot(g, wv_ref[...], preferred_element_type=f32)
    gate = jnp.dot(g, wg_ref[...], preferred_element_type=f32)
    out = val * jax.nn.sigmoid(gate)
    mod = mod_ref[0]
    z = DEEPNORM_ALPHA * x_ref[...] + mod[:, 2 * dm:3 * dm] * out
    mu = jnp.mean(z, axis=-1, keepdims=True)
    zc = z - mu
    var = jnp.mean(zc * zc, axis=-1, keepdims=True)
    o_ref[...] = zc * lax.rsqrt(var + LN_EPS) * lng_ref[...] + lnb_ref[...]


TOK_TM = 512


def glu_ln_pallas(ya, xa, mod3, n_lat, seq, bsz, w_val, w_gate, ln_g, ln_b):
    n_tok, dm = ya.shape
    tm = TOK_TM
    bf16 = jnp.bfloat16
    mod_idx = functools.partial(_mod_row_index, tm=tm, n_lat=n_lat, seq=seq, bsz=bsz)
    return pl.pallas_call(
        _glu_ln_kernel,
        grid=(n_tok // tm,),
        in_specs=[
            pl.BlockSpec((tm, dm), lambda i: (i, 0)),
            pl.BlockSpec((tm, dm), lambda i: (i, 0)),
            pl.BlockSpec((1, 1, 6 * dm), lambda i: (mod_idx(i), 0, 0)),
            pl.BlockSpec((dm, dm), lambda i: (0, 0)),
            pl.BlockSpec((dm, dm), lambda i: (0, 0)),
            pl.BlockSpec((1, dm), lambda i: (0, 0)),
            pl.BlockSpec((1, dm), lambda i: (0, 0)),
        ],
        out_specs=pl.BlockSpec((tm, dm), lambda i: (i, 0)),
        out_shape=jax.ShapeDtypeStruct((n_tok, dm), jnp.float32),
        compiler_params=pltpu.CompilerParams(dimension_semantics=("arbitrary",), vmem_limit_bytes=VMEM_LIMIT),
        name="glu_ln",
    )(ya, xa, mod3, w_val.astype(bf16), w_gate.astype(bf16), ln_g.reshape(1, dm), ln_b.reshape(1, dm))


MLA_HD = 128
MLA_TQ = 256
MLA_HB = 4


def _mla_prep(w_down, w_uq, w_uk, w_uv):
    bf16 = jnp.bfloat16
    quarter = MLA_ROPE // 4
    swap = np.concatenate([np.arange(quarter, 2 * quarter), np.arange(0, quarter),
                           np.arange(3 * quarter, 4 * quarter), np.arange(2 * quarter, 3 * quarter)])
    dm = w_down.shape[0]
    w_cq = w_down[:, :MLA_Q_RANK]
    w_ckv = w_down[:, MLA_Q_RANK:MLA_Q_RANK + MLA_KV_RANK]
    w_kr = w_down[:, MLA_Q_RANK + MLA_KV_RANK:]
    wd_t = jnp.concatenate([w_cq, w_ckv, jnp.zeros((dm, MLA_NOPE), w_down.dtype), w_kr, w_kr[:, swap]], axis=1).T
    rope = w_uq[:, :, MLA_NOPE:]
    wuq_t = jnp.concatenate([w_uq, rope[:, :, swap]], axis=2).reshape(MLA_Q_RANK, MLA_HEADS * MLA_HD).T
    wuk = jnp.concatenate([w_uk, jnp.zeros((MLA_KV_RANK, MLA_HEADS, MLA_HD - MLA_NOPE), w_uk.dtype)], axis=2)
    wuk = wuk.reshape(MLA_KV_RANK, MLA_HEADS * MLA_HD)
    wuv_t = w_uv.reshape(MLA_KV_RANK, MLA_HEADS * MLA_V).T
    return wd_t.astype(bf16), wuq_t.astype(bf16), wuk.astype(bf16), wuv_t.astype(bf16)


def _rope_tables(seq, tm):
    quarter = MLA_ROPE // 4
    freqs = ROPE_BASE ** (-jnp.arange(quarter, dtype=jnp.float32) / quarter)
    t = jnp.arange(seq, dtype=jnp.float32)
    row, col = jnp.floor(t / GRID_W), t - GRID_W * jnp.floor(t / GRID_W)
    ang_r, ang_c = freqs[:, None] * row[None, :], freqs[:, None] * col[None, :]
    cos32 = jnp.concatenate([jnp.cos(ang_r), jnp.cos(ang_r), jnp.cos(ang_c), jnp.cos(ang_c)], axis=0)
    sin32 = jnp.concatenate([-jnp.sin(ang_r), jnp.sin(ang_r), -jnp.sin(ang_c), jnp.sin(ang_c)], axis=0)
    ones = jnp.ones((MLA_NOPE, seq + tm), jnp.float32)
    zeros = jnp.zeros((MLA_HD - MLA_NOPE - MLA_ROPE, seq + tm), jnp.float32)
    cos_t = jnp.concatenate([ones, jnp.concatenate([cos32, jnp.ones((MLA_ROPE, tm))], axis=1), zeros], axis=0)
    sin_t = jnp.concatenate([0.0 * ones, jnp.concatenate([sin32, jnp.zeros((MLA_ROPE, tm))], axis=1), zeros], axis=0)
    return cos_t, sin_t


def _mla_proj_kernel(x_ref, mod_ref, cos_ref, sin_ref, wd_ref, wuq_ref, wuk_ref, wuv_ref, qn_ref, kvn_ref,
                     qt_ref, k_ref, vt_ref):
    f32, bf16 = jnp.float32, jnp.bfloat16
    dm = x_ref.shape[1]
    mod = mod_ref[0]
    h = x_ref[...] * (1.0 + mod[:, dm:2 * dm]) + mod[:, 0:dm]
    ht = h.T.astype(bf16)
    dt = jnp.dot(wd_ref[...], ht, preferred_element_type=f32)
    cq, ckv, kr = dt[:MLA_Q_RANK], dt[MLA_Q_RANK:MLA_Q_RANK + MLA_KV_RANK], dt[MLA_Q_RANK + MLA_KV_RANK:]
    cqn = cq * lax.rsqrt(jnp.mean(cq * cq, axis=0, keepdims=True) + RMS_EPS) * qn_ref[...]
    ckvn = ckv * lax.rsqrt(jnp.mean(ckv * ckv, axis=0, keepdims=True) + RMS_EPS) * kvn_ref[...]
    cos_t, sin_t = cos_ref[...], sin_ref[...]

    def rope(v):
        shifted = jnp.concatenate([v[MLA_ROPE:], v[:MLA_ROPE]], axis=0)
        return v * cos_t + shifted * sin_t

    scale = (MLA_NOPE + MLA_ROPE) ** -0.5
    q_all = jnp.dot(wuq_ref[...], cqn.astype(bf16), preferred_element_type=f32)
    for hd in range(MLA_HEADS):
        qt_ref[hd] = (rope(q_all[hd * MLA_HD:(hd + 1) * MLA_HD]) * scale).astype(bf16)
    ckvn_bf = ckvn.astype(bf16)
    vt_ref[...] = jnp.dot(wuv_ref[...], ckvn_bf, preferred_element_type=f32).astype(bf16)
    k_all = jnp.dot(ckvn.T.astype(bf16), wuk_ref[...], preferred_element_type=f32)
    kr_rows = rope(kr).T
    for hd in range(MLA_HEADS):
        k_ref[hd] = (k_all[:, hd * MLA_HD:(hd + 1) * MLA_HD] + kr_rows).astype(bf16)


def _mla_attn_kernel(*refs, with_latent):
    f32, bf16 = jnp.float32, jnp.bfloat16
    if with_latent:
        q_ref, kl_ref, kc_ref, vl_ref, vc_ref, o_ref = refs
    else:
        q_ref, kc_ref, vc_ref, o_ref = refs
    for i in range(q_ref.shape[0]):
        rows = slice(i * MLA_V, (i + 1) * MLA_V)
        q = q_ref[i]
        s_c = jnp.dot(kc_ref[i], q, preferred_element_type=f32)
        m = jnp.max(s_c, axis=0, keepdims=True)
        if with_latent:
            s_l = jnp.dot(kl_ref[i], q, preferred_element_type=f32)
            m = jnp.maximum(m, jnp.max(s_l, axis=0, keepdims=True))
        p_c = jnp.exp(s_c - m)
        den = jnp.sum(p_c, axis=0, keepdims=True)
        o = jnp.dot(vc_ref[rows, :], p_c.astype(bf16), preferred_element_type=f32)
        if with_latent:
            p_l = jnp.exp(s_l - m)
            den = den + jnp.sum(p_l, axis=0, keepdims=True)
            o = o + jnp.dot(vl_ref[rows, :], p_l.astype(bf16), preferred_element_type=f32)
        o_ref[rows, :] = (o / den).astype(bf16)


def _mla_out_kernel(ot_ref, x_ref, mod_ref, wo_ref, lng_ref, lnb_ref, o_ref):
    f32 = jnp.float32
    dm = x_ref.shape[1]
    attn = ot_ref[...].astype(f32).T.astype(jnp.bfloat16)
    out = jnp.dot(attn, wo_ref[...], preferred_element_type=f32)
    mod = mod_ref[0]
    z = DEEPNORM_ALPHA * x_ref[...] + mod[:, 2 * dm:3 * dm] * out
    mu = jnp.mean(z, axis=-1, keepdims=True)
    zc = z - mu
    var = jnp.mean(zc * zc, axis=-1, keepdims=True)
    o_ref[...] = zc * lax.rsqrt(var + LN_EPS) * lng_ref[...] + lnb_ref[...]


def mla_ln_pallas(xa, mod3, n_lat, seq, bsz, w_down, q_norm, kv_norm, w_uq, w_uk, w_uv, w_o, ln_g, ln_b):
    f32, bf16 = jnp.float32, jnp.bfloat16
    n_tok, dm = xa.shape
    n_ctx = n_tok - n_lat
    ctx_len = n_ctx // bsz
    tm = TOK_TM
    wd_t, wuq_t, wuk, wuv_t = _mla_prep(w_down, w_uq, w_uk, w_uv)
    cos_t, sin_t = _rope_tables(seq, tm)
    mod_idx = functools.partial(_mod_row_index, tm=tm, n_lat=n_lat, seq=seq, bsz=bsz)
    tiles_per_sample = seq // tm
    pos_idx = lambda i: jnp.where(i * tm < n_lat, i % tiles_per_sample, tiles_per_sample)
    n_hd, n_dn = MLA_HEADS * MLA_HD, wd_t.shape[0]
    whole = lambda shape: pl.BlockSpec(shape, lambda i: (0,) * len(shape))
    params = pltpu.CompilerParams(dimension_semantics=("arbitrary",), vmem_limit_bytes=VMEM_LIMIT)
    qt, k, vt = pl.pallas_call(
        _mla_proj_kernel,
        grid=(n_tok // tm,),
        in_specs=[
            pl.BlockSpec((tm, dm), lambda i: (i, 0)),
            pl.BlockSpec((1, 1, 6 * dm), lambda i: (mod_idx(i), 0, 0)),
            pl.BlockSpec((MLA_HD, tm), lambda i: (0, pos_idx(i))),
            pl.BlockSpec((MLA_HD, tm), lambda i: (0, pos_idx(i))),
            whole((n_dn, dm)), whole((n_hd, MLA_Q_RANK)), whole((MLA_KV_RANK, n_hd)),
            whole((MLA_HEADS * MLA_V, MLA_KV_RANK)), whole((MLA_Q_RANK, 1)), whole((MLA_KV_RANK, 1)),
        ],
        out_specs=[
            pl.BlockSpec((MLA_HEADS, MLA_HD, tm), lambda i: (0, 0, i)),
            pl.BlockSpec((MLA_HEADS, tm, MLA_HD), lambda i: (0, i, 0)),
            pl.BlockSpec((MLA_HEADS * MLA_V, tm), lambda i: (0, i)),
        ],
        out_shape=[
            jax.ShapeDtypeStruct((MLA_HEADS, MLA_HD, n_tok), bf16),
            jax.ShapeDtypeStruct((MLA_HEADS, n_tok, MLA_HD), bf16),
            jax.ShapeDtypeStruct((MLA_HEADS * MLA_V, n_tok), bf16),
        ],
        compiler_params=params,
        name="mla_proj",
    )(xa, mod3, cos_t, sin_t, wd_t, wuq_t, wuk, wuv_t, q_norm.reshape(-1, 1), kv_norm.reshape(-1, 1))

    tq, hb = MLA_TQ, MLA_HB
    n_qt = seq // tq
    ctx_blk = n_lat // ctx_len
    params3 = pltpu.CompilerParams(dimension_semantics=("arbitrary",) * 3, vmem_limit_bytes=VMEM_LIMIT)
    ot_lat = pl.pallas_call(
        functools.partial(_mla_attn_kernel, with_latent=True),
        grid=(bsz, MLA_HEADS // hb, n_qt),
        in_specs=[
            pl.BlockSpec((hb, MLA_HD, tq), lambda b, hd, t: (hd, 0, b * n_qt + t)),
            pl.BlockSpec((hb, seq, MLA_HD), lambda b, hd, t: (hd, b, 0)),
            pl.BlockSpec((hb, ctx_len, MLA_HD), lambda b, hd, t: (hd, ctx_blk + b, 0)),
            pl.BlockSpec((hb * MLA_V, seq), lambda b, hd, t: (hd, b)),
            pl.BlockSpec((hb * MLA_V, ctx_len), lambda b, hd, t: (hd, ctx_blk + b)),
        ],
        out_specs=pl.BlockSpec((hb * MLA_V, tq), lambda b, hd, t: (hd, b * n_qt + t)),
        out_shape=jax.ShapeDtypeStruct((MLA_HEADS * MLA_V, n_lat), bf16),
        compiler_params=params3,
        name="mla_attn_latent",
    )(qt, k, k, vt, vt)
    ot_ctx = pl.pallas_call(
        functools.partial(_mla_attn_kernel, with_latent=False),
        grid=(bsz, MLA_HEADS // hb, 1),
        in_specs=[
            pl.BlockSpec((hb, MLA_HD, ctx_len), lambda b, hd, t: (hd, 0, ctx_blk + b)),
            pl.BlockSpec((hb, ctx_len, MLA_HD), lambda b, hd, t: (hd, ctx_blk + b, 0)),
            pl.BlockSpec((hb * MLA_V, ctx_len), lambda b, hd, t: (hd, ctx_blk + b)),
        ],
        out_specs=pl.BlockSpec((hb * MLA_V, ctx_len), lambda b, hd, t: (hd, b)),
        out_shape=jax.ShapeDtypeStruct((MLA_HEADS * MLA_V, n_ctx), bf16),
        compiler_params=params3,
        name="mla_attn_context",
    )(qt, k, vt)
    ot = jnp.concatenate([ot_lat, ot_ctx], axis=1)
    return pl.pallas_call(
        _mla_out_kernel,
        grid=(n_tok // tm,),
        in_specs=[
            pl.BlockSpec((MLA_HEADS * MLA_V, tm), lambda i: (0, i)),
            pl.BlockSpec((tm, dm), lambda i: (i, 0)),
            pl.BlockSpec((1, 1, 6 * dm), lambda i: (mod_idx(i), 0, 0)),
            whole((MLA_HEADS * MLA_V, dm)), whole((1, dm)), whole((1, dm)),
        ],
        out_specs=pl.BlockSpec((tm, dm), lambda i: (i, 0)),
        out_shape=jax.ShapeDtypeStruct((n_tok, dm), f32),
        compiler_params=params,
        name="mla_out_ln",
    )(ot, xa, mod3, w_o.astype(bf16), ln_g.reshape(1, dm), ln_b.reshape(1, dm))


SSD_DT_PAD = 128
SSD_HPG = SSD_HEADS // SSD_GROUPS
SSD_GW = SSD_HPG * SSD_HEADDIM


def _ssd_in_kernel(x_ref, mod_ref, w_ref, wdt_ref, z_ref, xbc_ref, dt_ref, dtt_ref):
    f32, bf16 = jnp.float32, jnp.bfloat16
    dm = x_ref.shape[1]
    mod = mod_ref[0]
    h = x_ref[...] * (1.0 + mod[:, dm:2 * dm]) + mod[:, 0:dm]
    proj = jnp.dot(h.astype(bf16), w_ref[...], preferred_element_type=f32)
    z_ref[...] = proj[:, :SSD_D_INNER]
    xbc_ref[...] = proj[:, SSD_D_INNER:SSD_D_INNER + SSD_CONV_DIM]
    dt_ref[...] = proj[:, SSD_D_INNER + SSD_CONV_DIM:]
    dtt_ref[...] = jnp.dot(wdt_ref[...], h.T.astype(bf16), preferred_element_type=f32)


def _softplus(v):
    return jnp.maximum(v, 0.0) + jnp.log(1.0 + jnp.exp(-jnp.abs(v)))


def _split3(v):
    f32, bf16 = jnp.float32, jnp.bfloat16
    hi = v.astype(bf16)
    r1 = v - hi.astype(f32)
    mid = r1.astype(bf16)
    lo = (r1 - mid.astype(f32)).astype(bf16)
    return jnp.concatenate([hi, mid, lo], axis=1)


def _ssd_direction(direction, xbc_ref, prev_ref, next_ref, dt_ref, dtt_ref, has_prev, has_next,
                   cw_ref, cb_ref, bias_r_ref, a_r_ref, bias_c_ref, a_c_ref, e_ref, dskip_ref, state_ref, y_ref):
    f32, bf16 = jnp.float32, jnp.bfloat16
    hp = lax.Precision.HIGHEST
    q = xbc_ref.shape[0]
    nh = SSD_HEADS
    xm = xbc_ref[...]
    row = lax.broadcasted_iota(jnp.int32, xm.shape, 0)
    before = jnp.where(has_prev, prev_ref[7:8, :], 0.0)
    after = jnp.where(has_next, next_ref[0:1, :], 0.0)
    x_dn = jnp.where(row == 0, before, pltpu.roll(xm, 1, axis=0))
    x_up = jnp.where(row == q - 1, after, pltpu.roll(xm, q - 1, axis=0))
    conv = cb_ref[...] + x_dn * cw_ref[0:1, :] + xm * cw_ref[1:2, :] + x_up * cw_ref[2:3, :]
    conv = conv * jax.nn.sigmoid(conv)
    xs = conv[:, :SSD_D_INNER]
    gn = SSD_GROUPS * SSD_STATE
    bm, cm = conv[:, SSD_D_INNER:SSD_D_INNER + gn], conv[:, SSD_D_INNER + gn:]
    hs = slice(direction * nh, (direction + 1) * nh)
    dt = _softplus(dt_ref[:, hs] + bias_r_ref[direction:direction + 1, :])
    dtt = _softplus(dtt_ref[hs, :] + bias_c_ref[direction])
    r_i = lax.broadcasted_iota(jnp.int32, (q, q), 0)
    c_i = lax.broadcasted_iota(jnp.int32, (q, q), 1)
    causal = (r_i >= c_i) if direction == 0 else (r_i <= c_i)
    tri = causal.astype(f32)
    tri_t = ((c_i >= r_i) if direction == 0 else (c_i <= r_i)).astype(f32)
    a_cum = jnp.dot(tri, dt * a_r_ref[direction:direction + 1, :], precision=hp, preferred_element_type=f32)
    a_cum_t = jnp.dot(dtt * a_c_ref[direction], tri_t, precision=hp, preferred_element_type=f32)
    last = q - 1 if direction == 0 else 0
    a_exp = jnp.dot(_split3(a_cum), e_ref[...], preferred_element_type=f32)
    dt_exp = jnp.dot(_split3(dt), e_ref[...], preferred_element_type=f32)
    xdt = xs * dt_exp
    a_tot = a_exp[last:last + 1, :]
    xw = (xdt * jnp.exp(a_tot - a_exp)).astype(bf16)
    grow = jnp.exp(a_exp)
    carry = jnp.exp(a_tot)
    xdt_bf = xdt.astype(bf16)
    for g in range(SSD_GROUPS):
        gl = slice(g * SSD_GW, (g + 1) * SSD_GW)
        nl = slice(g * SSD_STATE, (g + 1) * SSD_STATE)
        bm_g, cm_g = bm[:, nl], cm[:, nl].astype(bf16)
        prev = state_ref[direction, g]
        y_g = jnp.dot(cm_g, prev.astype(bf16), preferred_element_type=f32) * grow[:, gl]
        states = jnp.dot(bm_g.T.astype(bf16), xw[:, gl], preferred_element_type=f32)
        state_ref[direction, g] = prev * carry[:, gl] + states
        cb = lax.dot_general(cm_g, bm_g.astype(bf16), (((1,), (1,)), ((), ())), preferred_element_type=f32)
        parts = []
        for hh in range(SSD_HPG):
            hd = g * SSD_HPG + hh
            seg = jnp.broadcast_to(a_cum[:, hd:hd + 1], (q, q)) - a_cum_t[hd:hd + 1, :]
            lmat = (jnp.where(causal, jnp.exp(seg), 0.0) * cb).astype(bf16)
            parts.append(jnp.dot(lmat, xdt_bf[:, hd * SSD_HEADDIM:(hd + 1) * SSD_HEADDIM], preferred_element_type=f32))
        y_g = y_g + jnp.concatenate(parts, axis=1)
        if direction == 0:
            y_g = y_g + dskip_ref[:, gl] * xs[:, gl]
        y_ref[:, gl] = y_g


def _ssd_scan_kernel(xf_ref, pf_ref, nf_ref, dtf_ref, dttf_ref, xb_ref, pb_ref, nb_ref, dtb_ref, dttb_ref,
                     cw_ref, cb_ref, bias_r_ref, a_r_ref, bias_c_ref, a_c_ref, e_ref, dskip_ref,
                     yf_ref, yb_ref, state_ref, *, ctx_chunks, lat_chunks):
    j = pl.program_id(1)

    @pl.when(j == 0)
    def _reset():
        state_ref[...] = jnp.zeros(state_ref.shape, jnp.float32)

    in_ctx = j < ctx_chunks
    pos_f = jnp.where(in_ctx, j, j - ctx_chunks)
    seg_len = jnp.where(in_ctx, ctx_chunks, lat_chunks)
    pos_b = seg_len - 1 - pos_f
    shared = (cw_ref, cb_ref, bias_r_ref, a_r_ref, bias_c_ref, a_c_ref, e_ref, dskip_ref, state_ref)
    _ssd_direction(0, xf_ref, pf_ref, nf_ref, dtf_ref, dttf_ref, pos_f > 0, pos_f < seg_len - 1, *shared, yf_ref)
    _ssd_direction(1, xb_ref, pb_ref, nb_ref, dtb_ref, dttb_ref, pos_b > 0, pos_b < seg_len - 1, *shared, yb_ref)


def _ssd_out_kernel(yf_ref, yb_ref, z_ref, x_ref, mod_ref, nw_ref, wo_ref, lng_ref, lnb_ref, o_ref):
    f32 = jnp.float32
    dm = x_ref.shape[1]
    z = z_ref[...]
    y = (yf_ref[...] + yb_ref[...]) * (z * jax.nn.sigmoid(z))
    y = y * lax.rsqrt(jnp.mean(y * y, axis=-1, keepdims=True) + RMS_EPS) * nw_ref[...]
    out = jnp.dot(y.astype(jnp.bfloat16), wo_ref[...], preferred_element_type=f32)
    mod = mod_ref[0]
    v = DEEPNORM_ALPHA * x_ref[...] + mod[:, 2 * dm:3 * dm] * out
    mu = jnp.mean(v, axis=-1, keepdims=True)
    vc = v - mu
    var = jnp.mean(vc * vc, axis=-1, keepdims=True)
    o_ref[...] = vc * lax.rsqrt(var + LN_EPS) * lng_ref[...] + lnb_ref[...]


def ssd_ln_pallas(xa, mod3, n_lat, seq, bsz, w_in, conv_w, conv_b, dt_bias, a_log, d, norm_w, w_out, ln_g, ln_b):
    f32, bf16 = jnp.float32, jnp.bfloat16
    n_tok, dm = xa.shape
    ctx_len = (n_tok - n_lat) // bsz
    tm = TOK_TM
    q = SSD_CHUNK
    nh = SSD_HEADS
    pad = SSD_DT_PAD - 2 * nh
    w_ext = jnp.concatenate([w_in, jnp.zeros((dm, pad), w_in.dtype)], axis=1).astype(bf16)
    n_in = w_ext.shape[1]
    wdt_t = w_ext[:, SSD_D_INNER + SSD_CONV_DIM:].T
    mod_idx = functools.partial(_mod_row_index, tm=tm, n_lat=n_lat, seq=seq, bsz=bsz)
    whole = lambda shape: pl.BlockSpec(shape, lambda *_: (0,) * len(shape))
    params = pltpu.CompilerParams(dimension_semantics=("arbitrary",), vmem_limit_bytes=VMEM_LIMIT)
    z, xbc, dt, dtt = pl.pallas_call(
        _ssd_in_kernel,
        grid=(n_tok // tm,),
        in_specs=[
            pl.BlockSpec((tm, dm), lambda i: (i, 0)),
            pl.BlockSpec((1, 1, 6 * dm), lambda i: (mod_idx(i), 0, 0)),
            whole((dm, n_in)), whole((SSD_DT_PAD, dm)),
        ],
        out_specs=[
            pl.BlockSpec((tm, SSD_D_INNER), lambda i: (i, 0)),
            pl.BlockSpec((tm, SSD_CONV_DIM), lambda i: (i, 0)),
            pl.BlockSpec((tm, SSD_DT_PAD), lambda i: (i, 0)),
            pl.BlockSpec((SSD_DT_PAD, tm), lambda i: (0, i)),
        ],
        out_shape=[
            jax.ShapeDtypeStruct((n_tok, SSD_D_INNER), f32),
            jax.ShapeDtypeStruct((n_tok, SSD_CONV_DIM), f32),
            jax.ShapeDtypeStruct((n_tok, SSD_DT_PAD), f32),
            jax.ShapeDtypeStruct((SSD_DT_PAD, n_tok), f32),
        ],
        compiler_params=params,
        name="ssd_in_proj",
    )(xa, mod3, w_ext, wdt_t)

    ctx_chunks, lat_chunks = ctx_len // q, seq // q
    ctx_base = n_lat // q
    n_chunk_total = n_tok // q

    def chunk_of(b, j, backward):
        in_ctx = j < ctx_chunks
        pos = jnp.where(in_ctx, j, j - ctx_chunks)
        seg = jnp.where(in_ctx, ctx_chunks, lat_chunks)
        pos = jnp.where(backward, seg - 1 - pos, pos)
        return jnp.where(in_ctx, ctx_base + b * ctx_chunks + pos, b * lat_chunks + pos)

    sub = q // 8
    def dir_specs(backward):
        ch = lambda b, j: chunk_of(b, j, backward)
        return [
            pl.BlockSpec((q, SSD_CONV_DIM), lambda b, j: (ch(b, j), 0)),
            pl.BlockSpec((8, SSD_CONV_DIM), lambda b, j: (jnp.maximum(ch(b, j) * sub - 1, 0), 0)),
            pl.BlockSpec((8, SSD_CONV_DIM), lambda b, j: (jnp.minimum((ch(b, j) + 1) * sub, n_chunk_total * sub - 1), 0)),
            pl.BlockSpec((q, SSD_DT_PAD), lambda b, j: (ch(b, j), 0)),
            pl.BlockSpec((SSD_DT_PAD, q), lambda b, j: (0, ch(b, j))),
        ]

    a = -jnp.exp(a_log.astype(f32))
    expand = jnp.repeat(jnp.eye(nh, dtype=f32), SSD_HEADDIM, axis=1)
    e3 = jnp.concatenate([expand, expand, expand], axis=0).astype(bf16)
    dskip = jnp.repeat(d, SSD_HEADDIM)[None, :]
    yf, yb = pl.pallas_call(
        functools.partial(_ssd_scan_kernel, ctx_chunks=ctx_chunks, lat_chunks=lat_chunks),
        grid=(bsz, ctx_chunks + lat_chunks),
        in_specs=dir_specs(False) + dir_specs(True) + [
            whole((SSD_CONV, SSD_CONV_DIM)), whole((1, SSD_CONV_DIM)),
            whole((2, nh)), whole((2, nh)), whole((2, nh, 1)), whole((2, nh, 1)),
            whole((3 * nh, SSD_D_INNER)), whole((1, SSD_D_INNER)),
        ],
        out_specs=[
            pl.BlockSpec((q, SSD_D_INNER), lambda b, j: (chunk_of(b, j, False), 0)),
            pl.BlockSpec((q, SSD_D_INNER), lambda b, j: (chunk_of(b, j, True), 0)),
        ],
        out_shape=[jax.ShapeDtypeStruct((n_tok, SSD_D_INNER), f32)] * 2,
        scratch_shapes=[pltpu.VMEM((2, SSD_GROUPS, SSD_STATE, SSD_GW), f32)],
        compiler_params=pltpu.CompilerParams(dimension_semantics=("arbitrary", "arbitrary"), vmem_limit_bytes=VMEM_LIMIT),
        name="ssd_scan",
    )(xbc, xbc, xbc, dt, dtt, xbc, xbc, xbc, dt, dtt,
      conv_w, conv_b.reshape(1, -1), dt_bias, a, dt_bias.reshape(2, nh, 1), a.reshape(2, nh, 1), e3, dskip)

    return pl.pallas_call(
        _ssd_out_kernel,
        grid=(n_tok // tm,),
        in_specs=[
            pl.BlockSpec((tm, SSD_D_INNER), lambda i: (i, 0)),
            pl.BlockSpec((tm, SSD_D_INNER), lambda i: (i, 0)),
            pl.BlockSpec((tm, SSD_D_INNER), lambda i: (i, 0)),
            pl.BlockSpec((tm, dm), lambda i: (i, 0)),
            pl.BlockSpec((1, 1, 6 * dm), lambda i: (mod_idx(i), 0, 0)),
            whole((1, SSD_D_INNER)), whole((SSD_D_INNER, dm)), whole((1, dm)), whole((1, dm)),
        ],
        out_specs=pl.BlockSpec((tm, dm), lambda i: (i, 0)),
        out_shape=jax.ShapeDtypeStruct((n_tok, dm), f32),
        compiler_params=params,
        name="ssd_out_ln",
    )(yf, yb, z, xa, mod3, norm_w.reshape(1, -1), w_out.astype(bf16), ln_g.reshape(1, dm), ln_b.reshape(1, dm))


def _modulation_kernel(c_ref, w_ref, b_ref, o_ref):
    cv = c_ref[...]
    act = cv * jax.nn.sigmoid(cv)
    o_ref[0] = jnp.dot(act, w_ref[0], precision=lax.Precision.HIGHEST,
                       preferred_element_type=jnp.float32) + b_ref[0]


def modulation_pallas(cond, mod_w, mod_b):
    depth, dm, n_out = mod_w.shape
    tn = dm
    return pl.pallas_call(
        _modulation_kernel,
        grid=(depth, n_out // tn),
        in_specs=[
            pl.BlockSpec((8, dm), lambda i, n: (0, 0)),
            pl.BlockSpec((1, dm, tn), lambda i, n: (i, 0, n)),
            pl.BlockSpec((1, 1, tn), lambda i, n: (i, 0, n)),
        ],
        out_specs=pl.BlockSpec((1, 8, tn), lambda i, n: (i, 0, n)),
        out_shape=jax.ShapeDtypeStruct((depth, 8, n_out), jnp.float32),
        compiler_params=pltpu.CompilerParams(dimension_semantics=("arbitrary", "arbitrary"), vmem_limit_bytes=VMEM_LIMIT),
        name="modulation",
    )(cond, mod_w, mod_b.reshape(depth, 1, n_out))


def kernel(x, c, ctx, c_ctx, mod_w, mod_b, ln_g, ln_b,
           s5_a_re, s5_a_im, s5_log_dt, s5_b_re, s5_b_im, s5_c_re, s5_c_im, s5_d, s5_w_gate, s5_w_val,
           ssd_w_in, ssd_conv_w, ssd_conv_b, ssd_dt_bias, ssd_a_log, ssd_d, ssd_norm_w, ssd_w_out,
           mla_w_down, mla_q_norm, mla_kv_norm, mla_w_uq, mla_w_uk, mla_w_uv, mla_w_o,
           peer_w_q, peer_subkeys, peer_u, peer_v):
    ROWS = x.shape[1] // GRID_W
    pos = grid_positions(ROWS)
    ctx_len = ctx.shape[1]
    bsz, seq_len, dm = x.shape
    n_lat, n_ctx = bsz * seq_len, bsz * ctx_len
    xa = jnp.concatenate([x.reshape(n_lat, dm), ctx.reshape(n_ctx, dm)], axis=0)
    cond = jnp.concatenate([c, c_ctx[None], jnp.zeros((7 - bsz, dm), x.dtype)], axis=0)
    mod_all = modulation_pallas(cond, mod_w, mod_b)
    for i in range(DEPTH):
        last = i == DEPTH - 1
        mod_tab = mod_all[i]
        mod3 = mod_tab.reshape(8, 1, 6 * dm)
        kind, j = i % N_MIXERS, i // N_MIXERS
        if kind == 0:
            prep = _s5_prep(s5_a_re[j], s5_a_im[j], s5_log_dt[j], s5_b_re[j], s5_b_im[j], s5_c_re[j], s5_c_im[j], s5_d[j])
            ya = s5_ssm_pallas(xa, mod_tab, n_lat, seq_len, bsz, prep)
            if last:
                ya, xa = ya[:n_lat], xa[:n_lat]
            xa = glu_ln_pallas(ya, xa, mod3, n_lat, seq_len, bsz, s5_w_val[j], s5_w_gate[j], ln_g[i, 0], ln_b[i, 0])
        elif kind == 1:
            xa = ssd_ln_pallas(xa, mod3, n_lat, seq_len, bsz, ssd_w_in[j], ssd_conv_w[j], ssd_conv_b[j], ssd_dt_bias[j],
                               ssd_a_log[j], ssd_d[j], ssd_norm_w[j], ssd_w_out[j], ln_g[i, 0], ln_b[i, 0])
            if last:
                xa = xa[:n_lat]
        else:
            xa = mla_ln_pallas(xa, mod3, n_lat, seq_len, bsz, mla_w_down[j], mla_q_norm[j], mla_kv_norm[j],
                               mla_w_uq[j], mla_w_uk[j], mla_w_uv[j], mla_w_o[j], ln_g[i, 0], ln_b[i, 0])
            if last:
                xa = xa[:n_lat]
        xa = peer_ln_pallas(xa, mod3, n_lat, seq_len, bsz, peer_w_q[i], peer_subkeys[i], peer_u[i], peer_v[i],
                            ln_g[i, 1], ln_b[i, 1])
    return xa[:n_lat].reshape(bsz, seq_len, dm)
```

```python
import math
import functools
import jax
import jax.numpy as jnp
from jax import lax
import numpy as np
from jax.experimental import pallas as pl
from jax.experimental.pallas import tpu as pltpu

D_MODEL = 1024
BATCH = 4
SEQ = 4096
DEPTH = 4

GRID_W = 64
CTX_LEN = 256
N_MIXERS = 3
DEEPNORM_ALPHA = (2.0 * DEPTH) ** 0.25
LN_EPS = 1e-5
RMS_EPS = 1e-6
ROPE_BASE = 10000.0

S5_GROUP = 16
S5_GROUPS = D_MODEL // S5_GROUP
S5_STATE = 64

SSD_D_INNER = 2 * D_MODEL
SSD_HEADDIM = 64
SSD_HEADS = SSD_D_INNER // SSD_HEADDIM
SSD_GROUPS = 4
SSD_STATE = 128
SSD_CONV = 3
SSD_CHUNK = 128
SSD_CONV_DIM = SSD_D_INNER + 2 * SSD_GROUPS * SSD_STATE
SSD_IN_DIM = SSD_D_INNER + SSD_CONV_DIM + 2 * SSD_HEADS

MLA_HEADS = 16
MLA_Q_RANK = 256
MLA_KV_RANK = 128
MLA_NOPE = 64
MLA_ROPE = 32
MLA_V = 64
MLA_BLOCK = 128

PEER_HEADS = 8
PEER_KEYS = 128
PEER_EXPERTS = PEER_KEYS * PEER_KEYS
PEER_QDIM = 256
PEER_TOPK = 16
PEER_BLOCK = 128


def layer_norm(x, g, b):
    xf = x.astype(jnp.float32)
    mu = jnp.mean(xf, axis=-1, keepdims=True)
    var = jnp.mean(jnp.square(xf - mu), axis=-1, keepdims=True)
    return ((xf - mu) * lax.rsqrt(var + LN_EPS)).astype(x.dtype) * g + b


def _ln_kernel(x_ref, g_ref, b_ref, o_ref):
    xf = x_ref[...]
    mu = jnp.mean(xf, axis=-1, keepdims=True)
    xc = xf - mu
    var = jnp.mean(xc * xc, axis=-1, keepdims=True)
    o_ref[...] = xc * lax.rsqrt(var + LN_EPS) * g_ref[...] + b_ref[...]


def layer_norm_pallas(x, g, b):
    shp = x.shape
    x2 = x.reshape(-1, shp[-1])
    n, d = x2.shape
    tb = 512
    out = pl.pallas_call(
        _ln_kernel,
        grid=(n // tb,),
        in_specs=[pl.BlockSpec((tb, d), lambda i: (i, 0)),
                  pl.BlockSpec((1, d), lambda i: (0, 0)),
                  pl.BlockSpec((1, d), lambda i: (0, 0))],
        out_specs=pl.BlockSpec((tb, d), lambda i: (i, 0)),
        out_shape=jax.ShapeDtypeStruct((n, d), jnp.float32),
        name="final_ln",
    )(x2, g.reshape(1, d), b.reshape(1, d))
    return out.reshape(shp)


def rms_norm(x, g):
    xf = x.astype(jnp.float32)
    return (xf * lax.rsqrt(jnp.mean(jnp.square(xf), axis=-1, keepdims=True) + RMS_EPS)).astype(x.dtype) * g


def modulate(x, shift, scale):
    return x * (1.0 + scale) + shift


def grid_positions(rows):
    row = jnp.repeat(jnp.arange(rows, dtype=jnp.float32), GRID_W)
    col = jnp.tile(jnp.arange(GRID_W, dtype=jnp.float32), rows)
    return row, col


def rope_axial(x, row, col):
    half = x.shape[-1] // 2
    quarter = half // 2
    freqs = ROPE_BASE ** (-jnp.arange(quarter, dtype=jnp.float32) / quarter)

    def rot(xp, pos):
        ang = pos[:, None] * freqs
        cos = jnp.cos(ang)[None, :, None, :].astype(x.dtype)
        sin = jnp.sin(ang)[None, :, None, :].astype(x.dtype)
        x1, x2 = xp[..., :quarter], xp[..., quarter:]
        return jnp.concatenate([x1 * cos - x2 * sin, x2 * cos + x1 * sin], axis=-1)

    return jnp.concatenate([rot(x[..., :half], row), rot(x[..., half:], col)], axis=-1)


def _lin_rec(left, right):
    a1, b1 = left
    a2, b2 = right
    return a1 * a2, a2 * b1 + b2


def s5_scan(u, a_bar, b_bar, c_mat, init, reverse):
    bu = jnp.einsum('lgc,gpc->lgp', u.astype(jnp.float32), b_bar)
    a = jnp.broadcast_to(a_bar, bu.shape)
    a_cum, s = lax.associative_scan(_lin_rec, (a, bu), reverse=reverse, axis=0)
    s = s + a_cum * init
    y = jnp.einsum('lgp,gcp->lgc', s, c_mat).real
    final = s[0] if reverse else s[-1]
    return y, final


def s5_direction(u_ctx, u_lat, a_re, a_im, log_dt, b_re, b_im, c_re, c_im, reverse):
    lam = lax.complex(a_re.astype(jnp.float32), a_im.astype(jnp.float32))
    a_bar = jnp.exp(lam * jnp.exp(log_dt.astype(jnp.float32))[:, None])
    b_mat = lax.complex(b_re.astype(jnp.float32), b_im.astype(jnp.float32))
    b_bar = ((a_bar - 1.0) / lam)[..., None] * b_mat
    c_mat = lax.complex(c_re.astype(jnp.float32), c_im.astype(jnp.float32))

    def per_sample(args):
        uc, ul = args
        yc, sc = s5_scan(uc, a_bar, b_bar, c_mat, jnp.zeros_like(a_bar), reverse)
        yl, _ = s5_scan(ul, a_bar, b_bar, c_mat, sc, reverse)
        return yc, yl

    return lax.map(per_sample, (u_ctx, u_lat))


def s5_mixer(h, hc, a_re, a_im, log_dt, b_re, b_im, c_re, c_im, d, w_gate, w_val):
    bsz, seq_len, _ = h.shape
    ctx_len = hc.shape[1]
    u = h.reshape(bsz, seq_len, S5_GROUPS, S5_GROUP)
    uc = hc.reshape(bsz, ctx_len, S5_GROUPS, S5_GROUP)
    y_l = d * h
    y_c = d * hc
    for direction in range(2):
        yc, yl = s5_direction(uc, u, a_re[direction], a_im[direction], log_dt[direction],
                              b_re[direction], b_im[direction], c_re[direction], c_im[direction],
                              direction == 1)
        y_l = y_l + yl.reshape(bsz, seq_len, D_MODEL).astype(h.dtype)
        y_c = y_c + yc.reshape(bsz, ctx_len, D_MODEL).astype(h.dtype)

    def glu(y):
        g = jax.nn.gelu(y)
        return (g @ w_val) * jax.nn.sigmoid(g @ w_gate)

    return glu(y_l), glu(y_c)


def depthwise_conv_centred(x, w, b):
    k_w = w.shape[0]
    pad = k_w // 2
    seq_len = x.shape[1]
    xp = jnp.pad(x, ((0, 0), (pad, pad), (0, 0)))
    out = b
    for k in range(k_w):
        out = out + xp[:, k:k + seq_len] * w[k]
    return out


def ssd_scan(x, dt, a, bm, cm, init):
    bsz, seq_len, n_heads, p_dim = x.shape
    n_grp, n_st = bm.shape[2], bm.shape[3]
    hg = n_heads // n_grp
    q_len = SSD_CHUNK
    nc = seq_len // q_len
    xc = x.reshape(bsz, nc, q_len, n_grp, hg, p_dim)
    dtc = dt.reshape(bsz, nc, q_len, n_grp, hg).astype(jnp.float32)
    bc = bm.reshape(bsz, nc, q_len, n_grp, n_st)
    cc = cm.reshape(bsz, nc, q_len, n_grp, n_st)
    a_cum = jnp.cumsum(dtc * a.reshape(n_grp, hg), axis=2)
    xdt = xc * dtc[..., None].astype(x.dtype)
    tri = jnp.tril(jnp.ones((q_len, q_len), dtype=bool))
    seg = a_cum[:, :, :, None] - a_cum[:, :, None]
    decay = jnp.exp(jnp.where(tri[:, :, None, None], seg, -jnp.inf)).astype(x.dtype)
    cb = jnp.einsum('bcqgn,bcsgn->bcgqs', cc, bc)
    y_diag = jnp.einsum('bcgqs,bcqsgh,bcsghp->bcqghp', cb, decay, xdt)
    decay_states = jnp.exp(a_cum[:, :, -1:] - a_cum).astype(x.dtype)
    states = jnp.einsum('bcsgn,bcsgh,bcsghp->bcghpn', bc, decay_states, xdt)
    chunk_decay = jnp.exp(a_cum[:, :, -1]).astype(x.dtype)

    def step(carry, inp):
        dec, st = inp
        return carry * dec[..., None, None] + st, carry

    final, prev = lax.scan(step, init, (jnp.moveaxis(chunk_decay, 1, 0), jnp.moveaxis(states, 1, 0)))
    y_off = jnp.einsum('bcqgn,cbghpn,bcqgh->bcqghp', cc, prev, jnp.exp(a_cum).astype(x.dtype))
    return (y_diag + y_off).reshape(bsz, seq_len, n_heads, p_dim), final


def ssd_mixer(h, hc, w_in, conv_w, conv_b, dt_bias, a_log, d, norm_w, w_out):
    a = -jnp.exp(a_log.astype(jnp.float32))

    def flip(t):
        return jnp.flip(t, axis=1)

    def bidir(t, inits):
        bsz, seq_len = t.shape[0], t.shape[1]
        z, xbc, dt = jnp.split(t @ w_in, [SSD_D_INNER, SSD_D_INNER + SSD_CONV_DIM], axis=-1)
        xbc = jax.nn.silu(depthwise_conv_centred(xbc, conv_w, conv_b))
        xs, bm, cm = jnp.split(xbc, [SSD_D_INNER, SSD_D_INNER + SSD_GROUPS * SSD_STATE], axis=-1)
        xs = xs.reshape(bsz, seq_len, SSD_HEADS, SSD_HEADDIM)
        bm = bm.reshape(bsz, seq_len, SSD_GROUPS, SSD_STATE)
        cm = cm.reshape(bsz, seq_len, SSD_GROUPS, SSD_STATE)
        dt = jax.nn.softplus(dt.reshape(bsz, seq_len, 2, SSD_HEADS) + dt_bias)
        y_f, s_f = ssd_scan(xs, dt[:, :, 0], a[0], bm, cm, inits[0])
        y_b, s_b = ssd_scan(flip(xs), flip(dt[:, :, 1]), a[1], flip(bm), flip(cm), inits[1])
        y = y_f + flip(y_b) + d[:, None] * xs
        y = rms_norm(y.reshape(z.shape) * jax.nn.silu(z), norm_w)
        return y @ w_out, (s_f, s_b)

    zero = jnp.zeros((h.shape[0], SSD_GROUPS, SSD_HEADS // SSD_GROUPS, SSD_HEADDIM, SSD_STATE), h.dtype)
    out_c, states_c = bidir(hc, (zero, zero))
    out_l, _ = bidir(h, states_c)
    return out_l, out_c


def mla_project(t, w_down, q_norm, kv_norm, w_uq, w_uk, w_uv, pos):
    cq, ckv, kr = jnp.split(t @ w_down, [MLA_Q_RANK, MLA_Q_RANK + MLA_KV_RANK], axis=-1)
    q = jnp.einsum('btr,rhd->bthd', rms_norm(cq, q_norm), w_uq)
    ckv = rms_norm(ckv, kv_norm)
    k_nope = jnp.einsum('btr,rhd->bthd', ckv, w_uk)
    v = jnp.einsum('btr,rhd->bthd', ckv, w_uv)
    q_nope, q_rope = q[..., :MLA_NOPE], q[..., MLA_NOPE:]
    kr = kr[:, :, None, :]
    if pos is not None:
        q_rope = rope_axial(q_rope, pos[0], pos[1])
        kr = rope_axial(kr, pos[0], pos[1])
    k = jnp.concatenate([k_nope, jnp.broadcast_to(kr, k_nope.shape[:3] + (MLA_ROPE,))], axis=-1)
    q = jnp.concatenate([q_nope, q_rope], axis=-1)
    return q, k, v


def attend(q, k, v):
    s = jnp.einsum('bqhd,bkhd->bhqk', q, k).astype(jnp.float32) * (MLA_NOPE + MLA_ROPE) ** -0.5
    p = jax.nn.softmax(s, axis=-1).astype(v.dtype)
    return jnp.einsum('bhqk,bkhd->bqhd', p, v)


def mla_mixer(h, hc, pos, w_down, q_norm, kv_norm, w_uq, w_uk, w_uv, w_o):
    bsz, seq_len, _ = h.shape
    qc, kc, vc = mla_project(hc, w_down, q_norm, kv_norm, w_uq, w_uk, w_uv, None)
    ql, kl, vl = mla_project(h, w_down, q_norm, kv_norm, w_uq, w_uk, w_uv, pos)
    out_c = attend(qc, kc, vc).reshape(bsz, hc.shape[1], MLA_HEADS * MLA_V)
    k_all = jnp.concatenate([kc, kl], axis=1)
    v_all = jnp.concatenate([vc, vl], axis=1)
    nb = seq_len // MLA_BLOCK
    qb = jnp.moveaxis(ql.reshape(bsz, nb, MLA_BLOCK, MLA_HEADS, MLA_NOPE + MLA_ROPE), 1, 0)
    out_l = lax.map(lambda qblk: attend(qblk, k_all, v_all), qb)
    out_l = jnp.moveaxis(out_l, 0, 1).reshape(bsz, seq_len, MLA_HEADS * MLA_V)
    return out_l @ w_o, out_c @ w_o


def peer_ffn(h, w_q, subkeys, u_tab, v_tab):
    bsz, seq_len, dm = h.shape
    half = PEER_QDIM // 2
    q = jnp.einsum('btd,dhk->bthk', h, w_q)
    s1 = jnp.einsum('bthk,hnk->bthn', q[..., :half], subkeys[:, 0])
    s2 = jnp.einsum('bthk,hnk->bthn', q[..., half:], subkeys[:, 1])
    v1, i1 = lax.top_k(s1, PEER_TOPK)
    v2, i2 = lax.top_k(s2, PEER_TOPK)
    n_cand = PEER_TOPK * PEER_TOPK
    cand = (v1[..., :, None] + v2[..., None, :]).reshape(bsz, seq_len, PEER_HEADS, n_cand)
    cidx = (i1[..., :, None] * PEER_KEYS + i2[..., None, :]).reshape(bsz, seq_len, PEER_HEADS, n_cand)
    best, sel = lax.top_k(cand, PEER_TOPK)
    eidx = jnp.take_along_axis(cidx, sel, axis=-1)
    gate = jax.nn.softmax(best.astype(jnp.float32), axis=-1).astype(h.dtype)
    n_blk = (bsz * seq_len) // PEER_BLOCK
    n_sel = PEER_HEADS * PEER_TOPK
    hb = h.reshape(n_blk, PEER_BLOCK, dm)
    ib = eidx.reshape(n_blk, PEER_BLOCK, n_sel)
    gb = gate.reshape(n_blk, PEER_BLOCK, n_sel)

    def block(args):
        hk, ik, gk = args
        u = jnp.take(u_tab, ik, axis=0)
        act = jax.nn.gelu(jnp.einsum('td,ted->te', hk, u))
        v = jnp.take(v_tab, ik, axis=0)
        return jnp.einsum('te,ted->td', gk * act, v)

    return lax.map(block, (hb, ib, gb)).reshape(bsz, seq_len, dm)


PEER_TM = 512
PEER_EB = 1024
PEER_ROWS = 16
PEER_LANES = 128
PEER_ACT_ROWS = 512
VMEM_LIMIT = 56 * 1024 * 1024
NEG_INF = float("-inf")
GELU_C0 = math.sqrt(2.0 / math.pi)
GELU_C1 = 0.044715 * GELU_C0


def _top16_sorted(s):
    n = PEER_TOPK
    m = s.shape[0] // 8
    x = [s[8 * r:8 * r + 8, :] for r in range(m)]

    def exchange(i, l, descending):
        hi, lo = jnp.maximum(x[i], x[l]), jnp.minimum(x[i], x[l])
        x[i], x[l] = (hi, lo) if descending else (lo, hi)

    def merge_bitonic(size):
        j = size // 2
        while j >= 1:
            for i in range(size):
                if i ^ j > i:
                    exchange(i, i ^ j, True)
            j //= 2

    k = 2
    while k < m:
        j = k // 2
        while j >= 1:
            for i in range(m):
                if i ^ j > i:
                    exchange(i, i ^ j, (i & k) == 0)
            j //= 2
        k *= 2
    merge_bitonic(m)
    shifts = [4, 2, 1]
    if m < n:
        first = shifts.pop(0)
        x = x + [pltpu.roll(x[m - 1 - r], first, axis=0) for r in range(m)]
        merge_bitonic(n)
    for shift in shifts:
        other = [pltpu.roll(x[n - 1 - r], shift, axis=0) for r in range(n)]
        for r in range(n):
            x[r] = jnp.maximum(x[r], other[r])
        merge_bitonic(n)
    return [x[r][0:1, :] for r in range(n)]


def _dot3(a_hi, a_lo, b_hi, b_lo):
    f32 = jnp.float32
    return (jnp.dot(a_hi, b_hi, preferred_element_type=f32)
            + jnp.dot(a_hi, b_lo, preferred_element_type=f32)
            + jnp.dot(a_lo, b_hi, preferred_element_type=f32))


def _split_bf16(v):
    hi = v.astype(jnp.bfloat16)
    lo = (v - hi.astype(jnp.float32)).astype(jnp.bfloat16)
    return hi, lo


def _peer_kernel(x_ref, mod_ref, wq_ref, skh_ref, skl_ref, u_ref, vt_ref, vtl_ref, lng_ref, lnb_ref,
                 o_ref,
                 ht_ref, s_ref, n1_ref, e1_ref, r2_ref, e2_ref, v1_ref, v2_ref, cand_ref,
                 act_ref, gact_ref, acc_ref):
    f32 = jnp.float32
    bf16 = jnp.bfloat16
    eb = pl.program_id(1)
    n_eb = pl.num_programs(1)
    dm = x_ref.shape[1]
    a_per = PEER_EB // PEER_KEYS
    half = PEER_QDIM // 2

    @pl.when(eb == 0)
    def _prologue():
        mod = mod_ref[0]
        h = x_ref[...] * (1.0 + mod[:, 4 * dm:5 * dm]) + mod[:, 3 * dm:4 * dm]
        ht = h.T.astype(bf16)
        ht_ref[...] = ht
        qt = jnp.dot(wq_ref[...], ht, preferred_element_type=f32)
        cand_ref[...] = jnp.full(cand_ref.shape, NEG_INF, f32)
        gact_ref[...] = jnp.zeros(gact_ref.shape, bf16)
        for hd in range(PEER_HEADS):
            q1_hi, q1_lo = _split_bf16(qt[hd * PEER_QDIM: hd * PEER_QDIM + half])
            q2_hi, q2_lo = _split_bf16(qt[hd * PEER_QDIM + half: (hd + 1) * PEER_QDIM])
            s_ref[0] = _dot3(skh_ref[hd, 0], skl_ref[hd, 0], q1_hi, q1_lo)
            s_ref[1] = _dot3(skh_ref[hd, 1], skl_ref[hd, 1], q2_hi, q2_lo)
            for lg in range(0, x_ref.shape[0], PEER_LANES):
                lanes = slice(lg, lg + PEER_LANES)
                s1, s2 = s_ref[0, :, lanes], s_ref[1, :, lanes]
                top1 = _top16_sorted(s1)
                top2 = _top16_sorted(s2)
                for k in range(PEER_TOPK):
                    v1_ref[k:k + 1, lanes] = top1[k]
                    v2_ref[k:k + 1, lanes] = top2[k]
                off = 0
                for i in range(PEER_TOPK):
                    cnt = PEER_TOPK // (i + 1)
                    cand_ref[off:off + cnt, lanes] = v1_ref[i:i + 1, lanes] + v2_ref[0:cnt, lanes]
                    off += cnt
                best = _top16_sorted(cand_ref[:, lanes])
                z = jnp.zeros_like(best[0])
                for k in range(PEER_TOPK):
                    z = z + jnp.exp(best[k] - best[0])
                tau = best[PEER_TOPK - 1]
                n1 = jnp.zeros(s1.shape, f32)
                for k in range(PEER_TOPK):
                    n1 = jnp.where(s1 + top2[k] >= tau, k + 1.0, n1)
                n1_ref[hd, :, lanes] = n1
                e1_ref[hd, :, lanes] = jnp.exp(s1 - top1[0])
                rank2 = jnp.zeros(s2.shape, f32)
                for k in range(PEER_TOPK):
                    rank2 = jnp.where(top2[k] > s2, k + 1.0, rank2)
                r2_ref[hd, :, lanes] = rank2.astype(bf16)
                e2_ref[hd, :, lanes] = (jnp.exp(s2 - top2[0]) * (0.5 / z)).astype(bf16)
        acc_ref[...] = jnp.zeros(acc_ref.shape, f32)

    par = eb % 2
    for r0 in range(0, PEER_EB, PEER_ACT_ROWS):
        act_ref[r0:r0 + PEER_ACT_ROWS, :] = jnp.dot(u_ref[r0:r0 + PEER_ACT_ROWS, :], ht_ref[...],
                                                    preferred_element_type=f32)
    acc_ref[...] += jnp.dot(vt_ref[0], gact_ref[1 - par], preferred_element_type=f32)
    a_base = pl.multiple_of(eb * a_per, a_per)
    blk = (PEER_ROWS, x_ref.shape[0])
    n_slab = PEER_KEYS // PEER_ROWS
    for a in range(a_per):
        g = [None] * n_slab
        for hd in range(PEER_HEADS):
            n1a = jnp.broadcast_to(n1_ref[hd, pl.ds(a_base, a_per), :][a:a + 1, :].astype(bf16), blk)
            e1a = jnp.broadcast_to(e1_ref[hd, pl.ds(a_base, a_per), :][a:a + 1, :].astype(bf16), blk)
            for i in range(n_slab):
                rs = slice(i * PEER_ROWS, (i + 1) * PEER_ROWS)
                w = jnp.where(r2_ref[hd, rs, :] < n1a, e2_ref[hd, rs, :], jnp.zeros(blk, bf16)) * e1a
                g[i] = w if g[i] is None else g[i] + w
        for i in range(n_slab):
            rows = slice(a * PEER_KEYS + i * PEER_ROWS, a * PEER_KEYS + (i + 1) * PEER_ROWS)
            x = act_ref[rows, :]
            t = jnp.tanh(x * (GELU_C0 + GELU_C1 * (x * x)))
            gact_ref[par, rows, :] = g[i] * (x + x * t).astype(bf16)

    @pl.when(eb == n_eb - 1)
    def _epilogue():
        mod = mod_ref[0]
        ffn_t = acc_ref[...] + jnp.dot(vtl_ref[0], gact_ref[par], preferred_element_type=f32)
        y = DEEPNORM_ALPHA * x_ref[...] + mod[:, 5 * dm:6 * dm] * ffn_t.T
        mu = jnp.mean(y, axis=-1, keepdims=True)
        yc = y - mu
        var = jnp.mean(yc * yc, axis=-1, keepdims=True)
        o_ref[...] = yc * lax.rsqrt(var + LN_EPS) * lng_ref[...] + lnb_ref[...]


def _mod_row_index(i, tm, n_lat, seq, bsz):
    return jnp.where(i * tm < n_lat, (i * tm) // seq, bsz)


def peer_ln_pallas(xa, mod3, n_lat, seq, bsz, w_q, subkeys, u_tab, v_tab, ln_g, ln_b):
    n_tok, dm = xa.shape
    tm = PEER_TM
    f32, bf16 = jnp.float32, jnp.bfloat16
    wq_t = w_q.reshape(dm, PEER_HEADS * PEER_QDIM).T
    wq_bf = wq_t.astype(bf16)
    sk_hi = subkeys.astype(bf16)
    sk_lo = (subkeys - sk_hi.astype(f32)).astype(bf16)
    u_bf = u_tab.astype(bf16)
    n_eb = PEER_EXPERTS // PEER_EB
    vt_bf = v_tab.reshape(n_eb, PEER_EB, dm).transpose(0, 2, 1).astype(bf16)
    qd = PEER_HEADS * PEER_QDIM
    n_cand = sum(PEER_TOPK // (i + 1) for i in range(PEER_TOPK))
    n_cand_pad = 64
    assert n_cand <= n_cand_pad
    mod_idx = functools.partial(_mod_row_index, tm=tm, n_lat=n_lat, seq=seq, bsz=bsz)
    return pl.pallas_call(
        _peer_kernel,
        grid=(n_tok // tm, n_eb),
        in_specs=[
            pl.BlockSpec((tm, dm), lambda i, e: (i, 0)),
            pl.BlockSpec((1, 1, 6 * dm), lambda i, e: (mod_idx(i), 0, 0)),
            pl.BlockSpec((qd, dm), lambda i, e: (0, 0)),
            pl.BlockSpec((PEER_HEADS, 2, PEER_KEYS, PEER_QDIM // 2), lambda i, e: (0, 0, 0, 0)),
            pl.BlockSpec((PEER_HEADS, 2, PEER_KEYS, PEER_QDIM // 2), lambda i, e: (0, 0, 0, 0)),
            pl.BlockSpec((PEER_EB, dm), lambda i, e: (e, 0)),
            pl.BlockSpec((1, dm, PEER_EB), lambda i, e: (jnp.maximum(e - 1, 0), 0, 0)),
            pl.BlockSpec((1, dm, PEER_EB), lambda i, e: (n_eb - 1, 0, 0)),
            pl.BlockSpec((1, dm), lambda i, e: (0, 0)),
            pl.BlockSpec((1, dm), lambda i, e: (0, 0)),
        ],
        out_specs=pl.BlockSpec((tm, dm), lambda i, e: (i, 0)),
        out_shape=jax.ShapeDtypeStruct((n_tok, dm), f32),
        scratch_shapes=[
            pltpu.VMEM((dm, tm), bf16),
            pltpu.VMEM((2, PEER_KEYS, tm), f32),
            pltpu.VMEM((PEER_HEADS, PEER_KEYS, tm), f32),
            pltpu.VMEM((PEER_HEADS, PEER_KEYS, tm), f32),
            pltpu.VMEM((PEER_HEADS, PEER_KEYS, tm), bf16),
            pltpu.VMEM((PEER_HEADS, PEER_KEYS, tm), bf16),
            pltpu.VMEM((PEER_TOPK, tm), f32),
            pltpu.VMEM((PEER_TOPK, tm), f32),
            pltpu.VMEM((n_cand_pad, tm), f32),
            pltpu.VMEM((PEER_EB, tm), f32),
            pltpu.VMEM((2, PEER_EB, tm), bf16),
            pltpu.VMEM((dm, tm), f32),
        ],
        compiler_params=pltpu.CompilerParams(
            dimension_semantics=("arbitrary", "arbitrary"),
            vmem_limit_bytes=VMEM_LIMIT),
        name="peer_ln",
    )(xa, mod3, wq_bf, sk_hi, sk_lo, u_bf, vt_bf, vt_bf, ln_g.reshape(1, dm), ln_b.reshape(1, dm))


S5_CHUNK = 16
S5_LEVELS = 9


def _s5_prep(a_re, a_im, log_dt, b_re, b_im, c_re, c_im, d):
    f32, bf16 = jnp.float32, jnp.bfloat16
    hp = lax.Precision.HIGHEST
    n_g, n_p, q = S5_GROUPS, S5_STATE, S5_CHUNK
    dt = jnp.exp(log_dt.astype(f32))[..., None]

    def apow(n):
        mag = jnp.exp(a_re * dt * n)
        ang = a_im * dt * n
        return mag * jnp.cos(ang), mag * jnp.sin(ang)

    ar1, ai1 = apow(1.0)
    den = a_re * a_re + a_im * a_im
    nr, ni = ar1 - 1.0, ai1
    cr = (nr * a_re + ni * a_im) / den
    ci = (ni * a_re - nr * a_im) / den
    bbr = cr[..., None] * b_re - ci[..., None] * b_im
    bbi = cr[..., None] * b_im + ci[..., None] * b_re
    lag = jnp.arange(q + 1, dtype=f32)[:, None, None, None]
    pr, pi = apow(lag)
    mr = pr[..., None] * bbr - pi[..., None] * bbi
    mi = pr[..., None] * bbi + pi[..., None] * bbr
    kmat = (jnp.einsum('dgcp,ndgpk->ndgck', c_re, mr, precision=hp)
            - jnp.einsum('dgcp,ndgpk->ndgck', c_im, mi, precision=hp))
    r_idx = jnp.arange(q)[:, None]
    t_idx = jnp.arange(q)[None, :]

    def toeplitz(kd, lagm):
        blk = kd[jnp.clip(lagm, 0, q)]
        blk = jnp.where((lagm >= 0)[:, :, None, None, None], blk, 0.0)
        return blk.transpose(2, 0, 4, 1, 3).reshape(n_g, q * S5_GROUP, q * S5_GROUP)

    eye = jnp.eye(q * S5_GROUP, dtype=f32)
    dvec = jnp.tile(d.reshape(n_g, 1, S5_GROUP), (1, q, 1)).reshape(n_g, q * S5_GROUP)
    tsum = toeplitz(kmat[:, 0], t_idx - r_idx) + toeplitz(kmat[:, 1], r_idx - t_idx) + eye[None] * dvec[:, None, :]

    def w_in(direction, exps):
        wr = mr[exps, direction].transpose(1, 0, 3, 2).reshape(n_g, q * S5_GROUP, n_p)
        wi = mi[exps, direction].transpose(1, 0, 3, 2).reshape(n_g, q * S5_GROUP, n_p)
        return wr, wi

    def w_out(direction, exps):
        pre, pim = pr[exps, direction], pi[exps, direction]
        cre, cim = c_re[direction], c_im[direction]
        wre = cre[None] * pre[:, :, None, :] - cim[None] * pim[:, :, None, :]
        wim = -(cre[None] * pim[:, :, None, :] + cim[None] * pre[:, :, None, :])
        return (wre.transpose(1, 3, 0, 2).reshape(n_g, n_p, q * S5_GROUP),
                wim.transpose(1, 3, 0, 2).reshape(n_g, n_p, q * S5_GROUP))

    steps = jnp.arange(q)
    win = [w_in(0, q - 1 - steps), w_in(1, steps)]
    wout = [w_out(0, steps + 1), w_out(1, q - steps)]
    zc = jnp.zeros((n_g, q * S5_GROUP, n_p), f32)
    zr = jnp.zeros((n_g, n_p, q * S5_GROUP), f32)
    win_p, wout_p = [], []
    for direction in range(2):
        wr, wi = win[direction]
        even = jnp.concatenate([wr, zc, wi, zc], axis=2)
        odd = jnp.concatenate([zc, wr, zc, wi], axis=2)
        is_odd = (jnp.arange(n_g) % 2 == 1)[:, None, None]
        win_p.append(jnp.where(is_odd, odd, even))
        vr, vi = wout[direction]
        even = jnp.concatenate([vr, zr, vi, zr], axis=1)
        odd = jnp.concatenate([zr, vr, zr, vi], axis=1)
        wout_p.append(jnp.where(is_odd, odd, even))
    win_p = jnp.stack(win_p, axis=1).reshape(n_g // 2, 2, 2, 4 * n_p, 4 * n_p).transpose(0, 2, 1, 3, 4)
    wout_p = jnp.stack(wout_p, axis=1).reshape(n_g // 2, 2, 2, 4 * n_p, 4 * n_p).transpose(0, 2, 1, 3, 4)
    lvl = (q * 2.0 ** jnp.arange(S5_LEVELS, dtype=f32))[:, None, None, None]
    lr, li = apow(lvl)
    pw = jnp.stack([lr, li], axis=1)
    pw = pw.transpose(3, 2, 0, 1, 4).reshape(n_g // 2, 2, 2, S5_LEVELS, 2, n_p)
    pw = pw.transpose(0, 2, 3, 4, 1, 5).reshape(n_g // 2, 2, 2 * S5_LEVELS, 2 * n_p)
    return tsum.astype(bf16).reshape(n_g // 2, 2, q * S5_GROUP, q * S5_GROUP), win_p.astype(bf16), wout_p.astype(bf16), pw


def _s5_prep_fast(a_re, a_im, log_dt, b_re, b_im, c_re, c_im, d):
    f32, bf16 = jnp.float32, jnp.bfloat16
    hp = lax.Precision.HIGHEST
    n_g, n_p, q = S5_GROUPS, S5_STATE, S5_CHUNK
    width = q * S5_GROUP
    dt = jnp.exp(log_dt.astype(f32))[..., None]

    def apow(n):
        mag = jnp.exp(a_re * dt * n)
        ang = a_im * dt * n
        return mag * jnp.cos(ang), mag * jnp.sin(ang)

    def cmul(xr, xi, yr, yi):
        return xr * yr - xi * yi, xr * yi + xi * yr

    ar1, ai1 = apow(1.0)
    den = a_re * a_re + a_im * a_im
    nr, ni = ar1 - 1.0, ai1
    cr = (nr * a_re + ni * a_im) / den
    ci = (ni * a_re - nr * a_im) / den
    bbr = cr[..., None] * b_re - ci[..., None] * b_im
    bbi = cr[..., None] * b_im + ci[..., None] * b_re
    steps = jnp.arange(q, dtype=f32)

    def powers(direction, sign):
        mag = jnp.exp(a_re[direction][:, None, :] * dt[direction][:, None, :] * (sign * steps)[None, :, None])
        ang = a_im[direction][:, None, :] * dt[direction][:, None, :] * (sign * steps)[None, :, None]
        return mag * jnp.cos(ang), mag * jnp.sin(ang)

    def factors(direction):
        lsign = -1.0 if direction == 0 else 1.0
        pr_l, pi_l = powers(direction, lsign)
        pr_r, pi_r = powers(direction, -lsign)
        bt_r = bbr[direction].transpose(0, 2, 1)[:, None, :, :]
        bt_i = bbi[direction].transpose(0, 2, 1)[:, None, :, :]
        lt_r, lt_i = cmul(bt_r, bt_i, pr_l[:, :, None, :], pi_l[:, :, None, :])
        ct_r = c_re[direction].transpose(0, 2, 1)[:, :, None, :]
        ct_i = c_im[direction].transpose(0, 2, 1)[:, :, None, :]
        r_r, r_i = cmul(ct_r, ct_i, pr_r.transpose(0, 2, 1)[:, :, :, None], pi_r.transpose(0, 2, 1)[:, :, :, None])
        return (lt_r.reshape(n_g, width, n_p), lt_i.reshape(n_g, width, n_p),
                r_r.reshape(n_g, n_p, width), r_i.reshape(n_g, n_p, width))

    row_t = jnp.arange(width)[:, None] // S5_GROUP
    col_t = jnp.arange(width)[None, :] // S5_GROUP
    tsum = jnp.eye(width, dtype=f32)[None] * jnp.tile(d.reshape(n_g, 1, S5_GROUP), (1, q, 1)).reshape(n_g, 1, width)
    win, wout = [], []
    for direction in range(2):
        lt_r, lt_i, r_r, r_i = factors(direction)
        prod = (jnp.einsum('gxp,gpy->gxy', lt_r, r_r, precision=hp)
                - jnp.einsum('gxp,gpy->gxy', lt_i, r_i, precision=hp))
        mask = (col_t >= row_t) if direction == 0 else (row_t >= col_t)
        tsum = tsum + jnp.where(mask[None], prod, 0.0)
        if direction == 0:
            in_r, in_i = apow(q - 1.0)
            out_r, out_i = ar1, ai1
            w_r, w_i = cmul(lt_r, lt_i, in_r[0][:, None, :], in_i[0][:, None, :])
        else:
            out_r, out_i = apow(float(q))
            w_r, w_i = lt_r, lt_i
        win.append((w_r, w_i))
        o_r, o_i = cmul(r_r, r_i, out_r[direction][:, :, None], out_i[direction][:, :, None])
        wout.append((o_r, -o_i))
    zc = jnp.zeros((n_g, width, n_p), f32)
    zr = jnp.zeros((n_g, n_p, width), f32)
    win_p, wout_p = [], []
    for direction in range(2):
        wr, wi = win[direction]
        even = jnp.concatenate([wr, zc, wi, zc], axis=2)
        odd = jnp.concatenate([zc, wr, zc, wi], axis=2)
        is_odd = (jnp.arange(n_g) % 2 == 1)[:, None, None]
        win_p.append(jnp.where(is_odd, odd, even))
        vr, vi = wout[direction]
        even = jnp.concatenate([vr, zr, vi, zr], axis=1)
        odd = jnp.concatenate([zr, vr, zr, vi], axis=1)
        wout_p.append(jnp.where(is_odd, odd, even))
    win_p = jnp.stack(win_p, axis=1).reshape(n_g // 2, 2, 2, 4 * n_p, 4 * n_p).transpose(0, 2, 1, 3, 4)
    wout_p = jnp.stack(wout_p, axis=1).reshape(n_g // 2, 2, 2, 4 * n_p, 4 * n_p).transpose(0, 2, 1, 3, 4)
    lvl = (q * 2.0 ** jnp.arange(S5_LEVELS, dtype=f32))[:, None, None, None]
    lr, li = apow(lvl)
    pw = jnp.stack([lr, li], axis=1)
    pw = pw.transpose(3, 2, 0, 1, 4).reshape(n_g // 2, 2, 2, S5_LEVELS, 2, n_p)
    pw = pw.transpose(0, 2, 3, 4, 1, 5).reshape(n_g // 2, 2, 2 * S5_LEVELS, 2 * n_p)
    return tsum.astype(bf16).reshape(n_g // 2, 2, width, width), win_p.astype(bf16), wout_p.astype(bf16), pw


def _shift_rows(v, r, up):
    n = v.shape[0]
    row = lax.broadcasted_iota(jnp.int32, v.shape, 0)
    if up:
        return jnp.where(row < n - r, pltpu.roll(v, n - r, axis=0), 0.0)
    return jnp.where(row >= r, pltpu.roll(v, r, axis=0), 0.0)


def _chunk_scan(re, im, pw, rows_per_chunk, up):
    n_chunks = re.shape[0] // rows_per_chunk
    level, s = 0, 1
    while s < n_chunks:
        ar, ai = pw[2 * level:2 * level + 1, :], pw[2 * level + 1:2 * level + 2, :]
        sre = _shift_rows(re, s * rows_per_chunk, up)
        sim = _shift_rows(im, s * rows_per_chunk, up)
        re, im = re + ar * sre - ai * sim, im + ar * sim + ai * sre
        level, s = level + 1, 2 * s
    return re, im


def _s5_kernel(u_ref, scl_ref, shl_ref, scc_ref, shc_ref, t_ref, win_ref, wout_ref, pw_ref, y_ref, *, bsz, ctx_chunks):
    f32, bf16 = jnp.float32, jnp.bfloat16
    n = u_ref.shape[1]
    width = u_ref.shape[2]
    rc = ctx_chunks * bsz
    hs = []
    for gi in range(2):
        xv = u_ref[gi].reshape(n // 8, 8, width)
        hl = xv * scl_ref[gi][None] + shl_ref[gi][None]
        hc = xv * scc_ref[gi][None] + shc_ref[gi][None]
        slab = lax.broadcasted_iota(jnp.int32, xv.shape, 0)
        hs.append(jnp.where(slab < rc // 8, hc, hl).reshape(n, width).astype(bf16))
    y = [jnp.dot(hs[gi], t_ref[0, gi], preferred_element_type=f32) for gi in range(2)]
    half = width // 2
    for direction in range(2):
        sloc = (jnp.dot(hs[0], win_ref[0, direction, 0], preferred_element_type=f32)
                + jnp.dot(hs[1], win_ref[0, direction, 1], preferred_element_type=f32))
        re, im = sloc[:, :half], sloc[:, half:]
        pw = pw_ref[0, direction]
        if direction == 0:
            sre, sim = _chunk_scan(_shift_rows(re, bsz, False), _shift_rows(im, bsz, False), pw, bsz, False)
        else:
            cre, cim = _chunk_scan(_shift_rows(re[:rc], bsz, True), _shift_rows(im[:rc], bsz, True), pw, bsz, True)
            ar, ai = pw[0:1, :], pw[1:2, :]
            fre = ar * cre[0:8] - ai * cim[0:8] + re[0:8]
            fim = ar * cim[0:8] + ai * cre[0:8] + im[0:8]
            row8 = lax.broadcasted_iota(jnp.int32, fre.shape, 0)
            tre = jnp.where(row8 >= 8 - bsz, pltpu.roll(fre, 8 - bsz, axis=0), 0.0)
            tim = jnp.where(row8 >= 8 - bsz, pltpu.roll(fim, 8 - bsz, axis=0), 0.0)
            lre, lim = _shift_rows(re[rc:], bsz, True), _shift_rows(im[rc:], bsz, True)
            lre = jnp.concatenate([lre[:-8], lre[-8:] + tre], axis=0)
            lim = jnp.concatenate([lim[:-8], lim[-8:] + tim], axis=0)
            lre, lim = _chunk_scan(lre, lim, pw, bsz, True)
            sre = jnp.concatenate([cre, lre], axis=0)
            sim = jnp.concatenate([cim, lim], axis=0)
        s_in = jnp.concatenate([sre, sim], axis=1).astype(bf16)
        for gi in range(2):
            y[gi] = y[gi] + jnp.dot(s_in, wout_ref[0, direction, gi], preferred_element_type=f32)
    for gi in range(2):
        y_ref[gi] = y[gi]


def s5_ssm_pallas(xa, mod_tab, n_lat, seq, bsz, prep):
    tsum, win_p, wout_p, pw = prep
    f32 = jnp.float32
    n_tok, dm = xa.shape
    ctx_len = (n_tok - n_lat) // bsz
    q = S5_CHUNK
    n_chunks = (seq + ctx_len) // q
    width = q * S5_GROUP
    full = jnp.concatenate([xa[n_lat:].reshape(bsz, ctx_len, dm), xa[:n_lat].reshape(bsz, seq, dm)], axis=1)
    u = full.reshape(bsz, n_chunks, q, S5_GROUPS, S5_GROUP).transpose(3, 1, 0, 2, 4).reshape(S5_GROUPS, n_chunks * bsz, width)

    def tile(vec_rows):
        rows = vec_rows.shape[0]
        t = vec_rows.reshape(rows, S5_GROUPS, 1, S5_GROUP)
        t = jnp.broadcast_to(t, (rows, S5_GROUPS, q, S5_GROUP)).reshape(rows, S5_GROUPS, width)
        return jnp.tile(t.transpose(1, 0, 2), (1, 8 // rows, 1))

    scl, shl = tile(1.0 + mod_tab[:bsz, dm:2 * dm]), tile(mod_tab[:bsz, 0:dm])
    scc = tile(jnp.broadcast_to(1.0 + mod_tab[bsz:bsz + 1, dm:2 * dm], (bsz, dm)))
    shc = tile(jnp.broadcast_to(mod_tab[bsz:bsz + 1, 0:dm], (bsz, dm)))
    n_rows = n_chunks * bsz
    tile_spec = pl.BlockSpec((2, 8, width), lambda p: (p, 0, 0))
    y = pl.pallas_call(
        functools.partial(_s5_kernel, bsz=bsz, ctx_chunks=ctx_len // q),
        grid=(S5_GROUPS // 2,),
        in_specs=[
            pl.BlockSpec((2, n_rows, width), lambda p: (p, 0, 0)),
            tile_spec, tile_spec, tile_spec, tile_spec,
            pl.BlockSpec((1, 2, width, width), lambda p: (p, 0, 0, 0)),
            pl.BlockSpec((1, 2, 2, width, width), lambda p: (p, 0, 0, 0, 0)),
            pl.BlockSpec((1, 2, 2, width, width), lambda p: (p, 0, 0, 0, 0)),
            pl.BlockSpec((1, 2, 2 * S5_LEVELS, 2 * S5_STATE), lambda p: (p, 0, 0, 0)),
        ],
        out_specs=pl.BlockSpec((2, n_rows, width), lambda p: (p, 0, 0)),
        out_shape=jax.ShapeDtypeStruct((S5_GROUPS, n_rows, width), f32),
        compiler_params=pltpu.CompilerParams(dimension_semantics=("arbitrary",), vmem_limit_bytes=VMEM_LIMIT),
        name="s5_ssm",
    )(u, scl, shl, scc, shc, tsum, win_p, wout_p, pw)
    yf = y.reshape(S5_GROUPS, n_chunks, bsz, q, S5_GROUP).transpose(2, 1, 3, 0, 4).reshape(bsz, seq + ctx_len, dm)
    return jnp.concatenate([yf[:, ctx_len:].reshape(n_lat, dm), yf[:, :ctx_len].reshape(bsz * ctx_len, dm)], axis=0)


def _glu_ln_kernel(y_ref, x_ref, mod_ref, wv_ref, wg_ref, lng_ref, lnb_ref, o_ref):
    f32 = jnp.float32
    dm = x_ref.shape[1]
    g = jax.nn.gelu(y_ref[...]).astype(jnp.bfloat16)
    val = jnp.dot(g, wv_ref[...], preferred_element_type=f32)
    gate = jnp.dot(g, wg_ref[...], preferred_element_type=f32)
    out = val * jax.nn.sigmoid(gate)
    mod = mod_ref[0]
    z = DEEPNORM_ALPHA * x_ref[...] + mod[:, 2 * dm:3 * dm] * out
    mu = jnp.mean(z, axis=-1, keepdims=True)
    zc = z - mu
    var = jnp.mean(zc * zc, axis=-1, keepdims=True)
    o_ref[...] = zc * lax.rsqrt(var + LN_EPS) * lng_ref[...] + lnb_ref[...]


TOK_TM = 512


def glu_ln_pallas(ya, xa, mod3, n_lat, seq, bsz, w_val, w_gate, ln_g, ln_b):
    n_tok, dm = ya.shape
    tm = TOK_TM
    bf16 = jnp.bfloat16
    mod_idx = functools.partial(_mod_row_index, tm=tm, n_lat=n_lat, seq=seq, bsz=bsz)
    return pl.pallas_call(
        _glu_ln_kernel,
        grid=(n_tok // tm,),
        in_specs=[
            pl.BlockSpec((tm, dm), lambda i: (i, 0)),
            pl.BlockSpec((tm, dm), lambda i: (i, 0)),
            pl.BlockSpec((1, 1, 6 * dm), lambda i: (mod_idx(i), 0, 0)),
            pl.BlockSpec((dm, dm), lambda i: (0, 0)),
            pl.BlockSpec((dm, dm), lambda i: (0, 0)),
            pl.BlockSpec((1, dm), lambda i: (0, 0)),
            pl.BlockSpec((1, dm), lambda i: (0, 0)),
        ],
        out_specs=pl.BlockSpec((tm, dm), lambda i: (i, 0)),
        out_shape=jax.ShapeDtypeStruct((n_tok, dm), jnp.float32),
        compiler_params=pltpu.CompilerParams(dimension_semantics=("arbitrary",), vmem_limit_bytes=VMEM_LIMIT),
        name="glu_ln",
    )(ya, xa, mod3, w_val.astype(bf16), w_gate.astype(bf16), ln_g.reshape(1, dm), ln_b.reshape(1, dm))


MLA_HD = 128
MLA_TQ = 256
MLA_HB = 4


def _mla_prep(w_down, w_uq, w_uk, w_uv):
    bf16 = jnp.bfloat16
    quarter = MLA_ROPE // 4
    swap = np.concatenate([np.arange(quarter, 2 * quarter), np.arange(0, quarter),
                           np.arange(3 * quarter, 4 * quarter), np.arange(2 * quarter, 3 * quarter)])
    dm = w_down.shape[0]
    w_cq = w_down[:, :MLA_Q_RANK]
    w_ckv = w_down[:, MLA_Q_RANK:MLA_Q_RANK + MLA_KV_RANK]
    w_kr = w_down[:, MLA_Q_RANK + MLA_KV_RANK:]
    wd_t = jnp.concatenate([w_cq, w_ckv, jnp.zeros((dm, MLA_NOPE), w_down.dtype), w_kr, w_kr[:, swap]], axis=1).T
    rope = w_uq[:, :, MLA_NOPE:]
    wuq_t = jnp.concatenate([w_uq, rope[:, :, swap]], axis=2).reshape(MLA_Q_RANK, MLA_HEADS * MLA_HD).T
    wuk = jnp.concatenate([w_uk, jnp.zeros((MLA_KV_RANK, MLA_HEADS, MLA_HD - MLA_NOPE), w_uk.dtype)], axis=2)
    wuk = wuk.reshape(MLA_KV_RANK, MLA_HEADS * MLA_HD)
    wuv_t = w_uv.reshape(MLA_KV_RANK, MLA_HEADS * MLA_V).T
    return wd_t.astype(bf16), wuq_t.astype(bf16), wuk.astype(bf16), wuv_t.astype(bf16)


def _rope_tables(seq, tm):
    quarter = MLA_ROPE // 4
    freqs = ROPE_BASE ** (-jnp.arange(quarter, dtype=jnp.float32) / quarter)
    t = jnp.arange(seq, dtype=jnp.float32)
    row, col = jnp.floor(t / GRID_W), t - GRID_W * jnp.floor(t / GRID_W)
    ang_r, ang_c = freqs[:, None] * row[None, :], freqs[:, None] * col[None, :]
    cos32 = jnp.concatenate([jnp.cos(ang_r), jnp.cos(ang_r), jnp.cos(ang_c), jnp.cos(ang_c)], axis=0)
    sin32 = jnp.concatenate([-jnp.sin(ang_r), jnp.sin(ang_r), -jnp.sin(ang_c), jnp.sin(ang_c)], axis=0)
    ones = jnp.ones((MLA_NOPE, seq + tm), jnp.float32)
    zeros = jnp.zeros((MLA_HD - MLA_NOPE - MLA_ROPE, seq + tm), jnp.float32)
    cos_t = jnp.concatenate([ones, jnp.concatenate([cos32, jnp.ones((MLA_ROPE, tm))], axis=1), zeros], axis=0)
    sin_t = jnp.concatenate([0.0 * ones, jnp.concatenate([sin32, jnp.zeros((MLA_ROPE, tm))], axis=1), zeros], axis=0)
    return cos_t, sin_t


def _mla_proj_kernel(x_ref, mod_ref, cos_ref, sin_ref, wd_ref, wuq_ref, wuk_ref, wuv_ref, qn_ref, kvn_ref,
                     qt_ref, k_ref, vt_ref):
    f32, bf16 = jnp.float32, jnp.bfloat16
    dm = x_ref.shape[1]
    mod = mod_ref[0]
    h = x_ref[...] * (1.0 + mod[:, dm:2 * dm]) + mod[:, 0:dm]
    ht = h.T.astype(bf16)
    dt = jnp.dot(wd_ref[...], ht, preferred_element_type=f32)
    cq, ckv, kr = dt[:MLA_Q_RANK], dt[MLA_Q_RANK:MLA_Q_RANK + MLA_KV_RANK], dt[MLA_Q_RANK + MLA_KV_RANK:]
    cqn = cq * lax.rsqrt(jnp.mean(cq * cq, axis=0, keepdims=True) + RMS_EPS) * qn_ref[...]
    ckvn = ckv * lax.rsqrt(jnp.mean(ckv * ckv, axis=0, keepdims=True) + RMS_EPS) * kvn_ref[...]
    cos_t, sin_t = cos_ref[...], sin_ref[...]

    def rope(v):
        shifted = jnp.concatenate([v[MLA_ROPE:], v[:MLA_ROPE]], axis=0)
        return v * cos_t + shifted * sin_t

    scale = (MLA_NOPE + MLA_ROPE) ** -0.5
    q_all = jnp.dot(wuq_ref[...], cqn.astype(bf16), preferred_element_type=f32)
    for hd in range(MLA_HEADS):
        qt_ref[hd] = (rope(q_all[hd * MLA_HD:(hd + 1) * MLA_HD]) * scale).astype(bf16)
    ckvn_bf = ckvn.astype(bf16)
    vt_ref[...] = jnp.dot(wuv_ref[...], ckvn_bf, preferred_element_type=f32).astype(bf16)
    k_all = jnp.dot(ckvn.T.astype(bf16), wuk_ref[...], preferred_element_type=f32)
    kr_rows = rope(kr).T
    for hd in range(MLA_HEADS):
        k_ref[hd] = (k_all[:, hd * MLA_HD:(hd + 1) * MLA_HD] + kr_rows).astype(bf16)


def _mla_attn_kernel(*refs, with_latent):
    f32, bf16 = jnp.float32, jnp.bfloat16
    if with_latent:
        q_ref, kl_ref, kc_ref, vl_ref, vc_ref, o_ref = refs
    else:
        q_ref, kc_ref, vc_ref, o_ref = refs
    for i in range(q_ref.shape[0]):
        rows = slice(i * MLA_V, (i + 1) * MLA_V)
        q = q_ref[i]
        s_c = jnp.dot(kc_ref[i], q, preferred_element_type=f32)
        m = jnp.max(s_c, axis=0, keepdims=True)
        if with_latent:
            s_l = jnp.dot(kl_ref[i], q, preferred_element_type=f32)
            m = jnp.maximum(m, jnp.max(s_l, axis=0, keepdims=True))
        p_c = jnp.exp(s_c - m)
        den = jnp.sum(p_c, axis=0, keepdims=True)
        o = jnp.dot(vc_ref[rows, :], p_c.astype(bf16), preferred_element_type=f32)
        if with_latent:
            p_l = jnp.exp(s_l - m)
            den = den + jnp.sum(p_l, axis=0, keepdims=True)
            o = o + jnp.dot(vl_ref[rows, :], p_l.astype(bf16), preferred_element_type=f32)
        o_ref[rows, :] = (o / den).astype(bf16)


def _mla_out_kernel(ot_ref, x_ref, mod_ref, wo_ref, lng_ref, lnb_ref, o_ref):
    f32 = jnp.float32
    dm = x_ref.shape[1]
    attn = ot_ref[...].astype(f32).T.astype(jnp.bfloat16)
    out = jnp.dot(attn, wo_ref[...], preferred_element_type=f32)
    mod = mod_ref[0]
    z = DEEPNORM_ALPHA * x_ref[...] + mod[:, 2 * dm:3 * dm] * out
    mu = jnp.mean(z, axis=-1, keepdims=True)
    zc = z - mu
    var = jnp.mean(zc * zc, axis=-1, keepdims=True)
    o_ref[...] = zc * lax.rsqrt(var + LN_EPS) * lng_ref[...] + lnb_ref[...]


def mla_ln_pallas(xa, mod3, n_lat, seq, bsz, w_down, q_norm, kv_norm, w_uq, w_uk, w_uv, w_o, ln_g, ln_b):
    f32, bf16 = jnp.float32, jnp.bfloat16
    n_tok, dm = xa.shape
    n_ctx = n_tok - n_lat
    ctx_len = n_ctx // bsz
    tm = TOK_TM
    wd_t, wuq_t, wuk, wuv_t = _mla_prep(w_down, w_uq, w_uk, w_uv)
    cos_t, sin_t = _rope_tables(seq, tm)
    mod_idx = functools.partial(_mod_row_index, tm=tm, n_lat=n_lat, seq=seq, bsz=bsz)
    tiles_per_sample = seq // tm
    pos_idx = lambda i: jnp.where(i * tm < n_lat, i % tiles_per_sample, tiles_per_sample)
    n_hd, n_dn = MLA_HEADS * MLA_HD, wd_t.shape[0]
    whole = lambda shape: pl.BlockSpec(shape, lambda i: (0,) * len(shape))
    params = pltpu.CompilerParams(dimension_semantics=("arbitrary",), vmem_limit_bytes=VMEM_LIMIT)
    qt, k, vt = pl.pallas_call(
        _mla_proj_kernel,
        grid=(n_tok // tm,),
        in_specs=[
            pl.BlockSpec((tm, dm), lambda i: (i, 0)),
            pl.BlockSpec((1, 1, 6 * dm), lambda i: (mod_idx(i), 0, 0)),
            pl.BlockSpec((MLA_HD, tm), lambda i: (0, pos_idx(i))),
            pl.BlockSpec((MLA_HD, tm), lambda i: (0, pos_idx(i))),
            whole((n_dn, dm)), whole((n_hd, MLA_Q_RANK)), whole((MLA_KV_RANK, n_hd)),
            whole((MLA_HEADS * MLA_V, MLA_KV_RANK)), whole((MLA_Q_RANK, 1)), whole((MLA_KV_RANK, 1)),
        ],
        out_specs=[
            pl.BlockSpec((MLA_HEADS, MLA_HD, tm), lambda i: (0, 0, i)),
            pl.BlockSpec((MLA_HEADS, tm, MLA_HD), lambda i: (0, i, 0)),
            pl.BlockSpec((MLA_HEADS * MLA_V, tm), lambda i: (0, i)),
        ],
        out_shape=[
            jax.ShapeDtypeStruct((MLA_HEADS, MLA_HD, n_tok), bf16),
            jax.ShapeDtypeStruct((MLA_HEADS, n_tok, MLA_HD), bf16),
            jax.ShapeDtypeStruct((MLA_HEADS * MLA_V, n_tok), bf16),
        ],
        compiler_params=params,
        name="mla_proj",
    )(xa, mod3, cos_t, sin_t, wd_t, wuq_t, wuk, wuv_t, q_norm.reshape(-1, 1), kv_norm.reshape(-1, 1))

    tq, hb = MLA_TQ, MLA_HB
    n_qt = seq // tq
    ctx_blk = n_lat // ctx_len
    params3 = pltpu.CompilerParams(dimension_semantics=("arbitrary",) * 3, vmem_limit_bytes=VMEM_LIMIT)
    ot_lat = pl.pallas_call(
        functools.partial(_mla_attn_kernel, with_latent=True),
        grid=(bsz, MLA_HEADS // hb, n_qt),
        in_specs=[
            pl.BlockSpec((hb, MLA_HD, tq), lambda b, hd, t: (hd, 0, b * n_qt + t)),
            pl.BlockSpec((hb, seq, MLA_HD), lambda b, hd, t: (hd, b, 0)),
            pl.BlockSpec((hb, ctx_len, MLA_HD), lambda b, hd, t: (hd, ctx_blk + b, 0)),
            pl.BlockSpec((hb * MLA_V, seq), lambda b, hd, t: (hd, b)),
            pl.BlockSpec((hb * MLA_V, ctx_len), lambda b, hd, t: (hd, ctx_blk + b)),
        ],
        out_specs=pl.BlockSpec((hb * MLA_V, tq), lambda b, hd, t: (hd, b * n_qt + t)),
        out_shape=jax.ShapeDtypeStruct((MLA_HEADS * MLA_V, n_lat), bf16),
        compiler_params=params3,
        name="mla_attn_latent",
    )(qt, k, k, vt, vt)
    ot_ctx = pl.pallas_call(
        functools.partial(_mla_attn_kernel, with_latent=False),
        grid=(bsz, MLA_HEADS // hb, 1),
        in_specs=[
            pl.BlockSpec((hb, MLA_HD, ctx_len), lambda b, hd, t: (hd, 0, ctx_blk + b)),
            pl.BlockSpec((hb, ctx_len, MLA_HD), lambda b, hd, t: (hd, ctx_blk + b, 0)),
            pl.BlockSpec((hb * MLA_V, ctx_len), lambda b, hd, t: (hd, ctx_blk + b)),
        ],
        out_specs=pl.BlockSpec((hb * MLA_V, ctx_len), lambda b, hd, t: (hd, b)),
        out_shape=jax.ShapeDtypeStruct((MLA_HEADS * MLA_V, n_ctx), bf16),
        compiler_params=params3,
        name="mla_attn_context",
    )(qt, k, vt)
    ot = jnp.concatenate([ot_lat, ot_ctx], axis=1)
    return pl.pallas_call(
        _mla_out_kernel,
        grid=(n_tok // tm,),
        in_specs=[
            pl.BlockSpec((MLA_HEADS * MLA_V, tm), lambda i: (0, i)),
            pl.BlockSpec((tm, dm), lambda i: (i, 0)),
            pl.BlockSpec((1, 1, 6 * dm), lambda i: (mod_idx(i), 0, 0)),
            whole((MLA_HEADS * MLA_V, dm)), whole((1, dm)), whole((1, dm)),
        ],
        out_specs=pl.BlockSpec((tm, dm), lambda i: (i, 0)),
        out_shape=jax.ShapeDtypeStruct((n_tok, dm), f32),
        compiler_params=params,
        name="mla_out_ln",
    )(ot, xa, mod3, w_o.astype(bf16), ln_g.reshape(1, dm), ln_b.reshape(1, dm))


SSD_DT_PAD = 128
SSD_HPG = SSD_HEADS // SSD_GROUPS
SSD_GW = SSD_HPG * SSD_HEADDIM


def _ssd_in_kernel(x_ref, mod_ref, w_ref, wdt_ref, z_ref, xbc_ref, dt_ref, dtt_ref):
    f32, bf16 = jnp.float32, jnp.bfloat16
    dm = x_ref.shape[1]
    mod = mod_ref[0]
    h = x_ref[...] * (1.0 + mod[:, dm:2 * dm]) + mod[:, 0:dm]
    proj = jnp.dot(h.astype(bf16), w_ref[...], preferred_element_type=f32)
    z_ref[...] = proj[:, :SSD_D_INNER]
    xbc_ref[...] = proj[:, SSD_D_INNER:SSD_D_INNER + SSD_CONV_DIM]
    dt_ref[...] = proj[:, SSD_D_INNER + SSD_CONV_DIM:]
    dtt_ref[...] = jnp.dot(wdt_ref[...], h.T.astype(bf16), preferred_element_type=f32)


def _softplus(v):
    return jnp.maximum(v, 0.0) + jnp.log(1.0 + jnp.exp(-jnp.abs(v)))


def _split3(v):
    f32, bf16 = jnp.float32, jnp.bfloat16
    hi = v.astype(bf16)
    r1 = v - hi.astype(f32)
    mid = r1.astype(bf16)
    lo = (r1 - mid.astype(f32)).astype(bf16)
    return jnp.concatenate([hi, mid, lo], axis=1)


def _ssd_direction(direction, xbc_ref, prev_ref, next_ref, dt_ref, dtt_ref, has_prev, has_next,
                   cw_ref, cb_ref, bias_r_ref, a_r_ref, bias_c_ref, a_c_ref, e_ref, dskip_ref, state_ref, y_ref):
    f32, bf16 = jnp.float32, jnp.bfloat16
    hp = lax.Precision.HIGHEST
    q = xbc_ref.shape[0]
    nh = SSD_HEADS
    xm = xbc_ref[...]
    row = lax.broadcasted_iota(jnp.int32, xm.shape, 0)
    before = jnp.where(has_prev, prev_ref[7:8, :], 0.0)
    after = jnp.where(has_next, next_ref[0:1, :], 0.0)
    x_dn = jnp.where(row == 0, before, pltpu.roll(xm, 1, axis=0))
    x_up = jnp.where(row == q - 1, after, pltpu.roll(xm, q - 1, axis=0))
    conv = cb_ref[...] + x_dn * cw_ref[0:1, :] + xm * cw_ref[1:2, :] + x_up * cw_ref[2:3, :]
    conv = conv * jax.nn.sigmoid(conv)
    xs = conv[:, :SSD_D_INNER]
    gn = SSD_GROUPS * SSD_STATE
    bm, cm = conv[:, SSD_D_INNER:SSD_D_INNER + gn], conv[:, SSD_D_INNER + gn:]
    hs = slice(direction * nh, (direction + 1) * nh)
    dt = _softplus(dt_ref[:, hs] + bias_r_ref[direction:direction + 1, :])
    dtt = _softplus(dtt_ref[hs, :] + bias_c_ref[direction])
    r_i = lax.broadcasted_iota(jnp.int32, (q, q), 0)
    c_i = lax.broadcasted_iota(jnp.int32, (q, q), 1)
    causal = (r_i >= c_i) if direction == 0 else (r_i <= c_i)
    tri = causal.astype(f32)
    tri_t = ((c_i >= r_i) if direction == 0 else (c_i <= r_i)).astype(f32)
    a_cum = jnp.dot(tri, dt * a_r_ref[direction:direction + 1, :], precision=hp, preferred_element_type=f32)
    a_cum_t = jnp.dot(dtt * a_c_ref[direction], tri_t, precision=hp, preferred_element_type=f32)
    last = q - 1 if direction == 0 else 0
    a_exp = jnp.dot(_split3(a_cum), e_ref[...], preferred_element_type=f32)
    dt_exp = jnp.dot(_split3(dt), e_ref[...], preferred_element_type=f32)
    xdt = xs * dt_exp
    a_tot = a_exp[last:last + 1, :]
    xw = (xdt * jnp.exp(a_tot - a_exp)).astype(bf16)
    grow = jnp.exp(a_exp)
    carry = jnp.exp(a_tot)
    xdt_bf = xdt.astype(bf16)
    for g in range(SSD_GROUPS):
        gl = slice(g * SSD_GW, (g + 1) * SSD_GW)
        nl = slice(g * SSD_STATE, (g + 1) * SSD_STATE)
        bm_g, cm_g = bm[:, nl], cm[:, nl].astype(bf16)
        prev = state_ref[direction, g]
        y_g = jnp.dot(cm_g, prev.astype(bf16), preferred_element_type=f32) * grow[:, gl]
        states = jnp.dot(bm_g.T.astype(bf16), xw[:, gl], preferred_element_type=f32)
        state_ref[direction, g] = prev * carry[:, gl] + states
        cb = lax.dot_general(cm_g, bm_g.astype(bf16), (((1,), (1,)), ((), ())), preferred_element_type=f32)
        parts = []
        for hh in range(SSD_HPG):
            hd = g * SSD_HPG + hh
            seg = jnp.broadcast_to(a_cum[:, hd:hd + 1], (q, q)) - a_cum_t[hd:hd + 1, :]
            lmat = (jnp.where(causal, jnp.exp(seg), 0.0) * cb).astype(bf16)
            parts.append(jnp.dot(lmat, xdt_bf[:, hd * SSD_HEADDIM:(hd + 1) * SSD_HEADDIM], preferred_element_type=f32))
        y_g = y_g + jnp.concatenate(parts, axis=1)
        if direction == 0:
            y_g = y_g + dskip_ref[:, gl] * xs[:, gl]
        y_ref[:, gl] = y_g


def _ssd_scan_kernel(xf_ref, pf_ref, nf_ref, dtf_ref, dttf_ref, xb_ref, pb_ref, nb_ref, dtb_ref, dttb_ref,
                     cw_ref, cb_ref, bias_r_ref, a_r_ref, bias_c_ref, a_c_ref, e_ref, dskip_ref,
                     yf_ref, yb_ref, state_ref, *, ctx_chunks, lat_chunks):
    j = pl.program_id(1)

    @pl.when(j == 0)
    def _reset():
        state_ref[...] = jnp.zeros(state_ref.shape, jnp.float32)

    in_ctx = j < ctx_chunks
    pos_f = jnp.where(in_ctx, j, j - ctx_chunks)
    seg_len = jnp.where(in_ctx, ctx_chunks, lat_chunks)
    pos_b = seg_len - 1 - pos_f
    shared = (cw_ref, cb_ref, bias_r_ref, a_r_ref, bias_c_ref, a_c_ref, e_ref, dskip_ref, state_ref)
    _ssd_direction(0, xf_ref, pf_ref, nf_ref, dtf_ref, dttf_ref, pos_f > 0, pos_f < seg_len - 1, *shared, yf_ref)
    _ssd_direction(1, xb_ref, pb_ref, nb_ref, dtb_ref, dttb_ref, pos_b > 0, pos_b < seg_len - 1, *shared, yb_ref)


def _ssd_out_kernel(yf_ref, yb_ref, z_ref, x_ref, mod_ref, nw_ref, wo_ref, lng_ref, lnb_ref, o_ref):
    f32 = jnp.float32
    dm = x_ref.shape[1]
    z = z_ref[...]
    y = (yf_ref[...] + yb_ref[...]) * (z * jax.nn.sigmoid(z))
    y = y * lax.rsqrt(jnp.mean(y * y, axis=-1, keepdims=True) + RMS_EPS) * nw_ref[...]
    out = jnp.dot(y.astype(jnp.bfloat16), wo_ref[...], preferred_element_type=f32)
    mod = mod_ref[0]
    v = DEEPNORM_ALPHA * x_ref[...] + mod[:, 2 * dm:3 * dm] * out
    mu = jnp.mean(v, axis=-1, keepdims=True)
    vc = v - mu
    var = jnp.mean(vc * vc, axis=-1, keepdims=True)
    o_ref[...] = vc * lax.rsqrt(var + LN_EPS) * lng_ref[...] + lnb_ref[...]


def ssd_ln_pallas(xa, mod3, n_lat, seq, bsz, w_in, conv_w, conv_b, dt_bias, a_log, d, norm_w, w_out, ln_g, ln_b):
    f32, bf16 = jnp.float32, jnp.bfloat16
    n_tok, dm = xa.shape
    ctx_len = (n_tok - n_lat) // bsz
    tm = TOK_TM
    q = SSD_CHUNK
    nh = SSD_HEADS
    pad = SSD_DT_PAD - 2 * nh
    w_ext = jnp.concatenate([w_in, jnp.zeros((dm, pad), w_in.dtype)], axis=1).astype(bf16)
    n_in = w_ext.shape[1]
    wdt_t = w_ext[:, SSD_D_INNER + SSD_CONV_DIM:].T
    mod_idx = functools.partial(_mod_row_index, tm=tm, n_lat=n_lat, seq=seq, bsz=bsz)
    whole = lambda shape: pl.BlockSpec(shape, lambda *_: (0,) * len(shape))
    params = pltpu.CompilerParams(dimension_semantics=("arbitrary",), vmem_limit_bytes=VMEM_LIMIT)
    z, xbc, dt, dtt = pl.pallas_call(
        _ssd_in_kernel,
        grid=(n_tok // tm,),
        in_specs=[
            pl.BlockSpec((tm, dm), lambda i: (i, 0)),
            pl.BlockSpec((1, 1, 6 * dm), lambda i: (mod_idx(i), 0, 0)),
            whole((dm, n_in)), whole((SSD_DT_PAD, dm)),
        ],
        out_specs=[
            pl.BlockSpec((tm, SSD_D_INNER), lambda i: (i, 0)),
            pl.BlockSpec((tm, SSD_CONV_DIM), lambda i: (i, 0)),
            pl.BlockSpec((tm, SSD_DT_PAD), lambda i: (i, 0)),
            pl.BlockSpec((SSD_DT_PAD, tm), lambda i: (0, i)),
        ],
        out_shape=[
            jax.ShapeDtypeStruct((n_tok, SSD_D_INNER), f32),
            jax.ShapeDtypeStruct((n_tok, SSD_CONV_DIM), f32),
            jax.ShapeDtypeStruct((n_tok, SSD_DT_PAD), f32),
            jax.ShapeDtypeStruct((SSD_DT_PAD, n_tok), f32),
        ],
        compiler_params=params,
        name="ssd_in_proj",
    )(xa, mod3, w_ext, wdt_t)

    ctx_chunks, lat_chunks = ctx_len // q, seq // q
    ctx_base = n_lat // q
    n_chunk_total = n_tok // q

    def chunk_of(b, j, backward):
        in_ctx = j < ctx_chunks
        pos = jnp.where(in_ctx, j, j - ctx_chunks)
        seg = jnp.where(in_ctx, ctx_chunks, lat_chunks)
        pos = jnp.where(backward, seg - 1 - pos, pos)
        return jnp.where(in_ctx, ctx_base + b * ctx_chunks + pos, b * lat_chunks + pos)

    sub = q // 8
    def dir_specs(backward):
        ch = lambda b, j: chunk_of(b, j, backward)
        return [
            pl.BlockSpec((q, SSD_CONV_DIM), lambda b, j: (ch(b, j), 0)),
            pl.BlockSpec((8, SSD_CONV_DIM), lambda b, j: (jnp.maximum(ch(b, j) * sub - 1, 0), 0)),
            pl.BlockSpec((8, SSD_CONV_DIM), lambda b, j: (jnp.minimum((ch(b, j) + 1) * sub, n_chunk_total * sub - 1), 0)),
            pl.BlockSpec((q, SSD_DT_PAD), lambda b, j: (ch(b, j), 0)),
            pl.BlockSpec((SSD_DT_PAD, q), lambda b, j: (0, ch(b, j))),
        ]

    a = -jnp.exp(a_log.astype(f32))
    expand = jnp.repeat(jnp.eye(nh, dtype=f32), SSD_HEADDIM, axis=1)
    e3 = jnp.concatenate([expand, expand, expand], axis=0).astype(bf16)
    dskip = jnp.repeat(d, SSD_HEADDIM)[None, :]
    yf, yb = pl.pallas_call(
        functools.partial(_ssd_scan_kernel, ctx_chunks=ctx_chunks, lat_chunks=lat_chunks),
        grid=(bsz, ctx_chunks + lat_chunks),
        in_specs=dir_specs(False) + dir_specs(True) + [
            whole((SSD_CONV, SSD_CONV_DIM)), whole((1, SSD_CONV_DIM)),
            whole((2, nh)), whole((2, nh)), whole((2, nh, 1)), whole((2, nh, 1)),
            whole((3 * nh, SSD_D_INNER)), whole((1, SSD_D_INNER)),
        ],
        out_specs=[
            pl.BlockSpec((q, SSD_D_INNER), lambda b, j: (chunk_of(b, j, False), 0)),
            pl.BlockSpec((q, SSD_D_INNER), lambda b, j: (chunk_of(b, j, True), 0)),
        ],
        out_shape=[jax.ShapeDtypeStruct((n_tok, SSD_D_INNER), f32)] * 2,
        scratch_shapes=[pltpu.VMEM((2, SSD_GROUPS, SSD_STATE, SSD_GW), f32)],
        compiler_params=pltpu.CompilerParams(dimension_semantics=("arbitrary", "arbitrary"), vmem_limit_bytes=VMEM_LIMIT),
        name="ssd_scan",
    )(xbc, xbc, xbc, dt, dtt, xbc, xbc, xbc, dt, dtt,
      conv_w, conv_b.reshape(1, -1), dt_bias, a, dt_bias.reshape(2, nh, 1), a.reshape(2, nh, 1), e3, dskip)

    return pl.pallas_call(
        _ssd_out_kernel,
        grid=(n_tok // tm,),
        in_specs=[
            pl.BlockSpec((tm, SSD_D_INNER), lambda i: (i, 0)),
            pl.BlockSpec((tm, SSD_D_INNER), lambda i: (i, 0)),
            pl.BlockSpec((tm, SSD_D_INNER), lambda i: (i, 0)),
            pl.BlockSpec((tm, dm), lambda i: (i, 0)),
            pl.BlockSpec((1, 1, 6 * dm), lambda i: (mod_idx(i), 0, 0)),
            whole((1, SSD_D_INNER)), whole((SSD_D_INNER, dm)), whole((1, dm)), whole((1, dm)),
        ],
        out_specs=pl.BlockSpec((tm, dm), lambda i: (i, 0)),
        out_shape=jax.ShapeDtypeStruct((n_tok, dm), f32),
        compiler_params=params,
        name="ssd_out_ln",
    )(yf, yb, z, xa, mod3, norm_w.reshape(1, -1), w_out.astype(bf16), ln_g.reshape(1, dm), ln_b.reshape(1, dm))


def _modulation_kernel(c_ref, w_ref, b_ref, o_ref):
    cv = c_ref[...]
    act = cv * jax.nn.sigmoid(cv)
    o_ref[0] = jnp.dot(act, w_ref[0], precision=lax.Precision.HIGHEST,
                       preferred_element_type=jnp.float32) + b_ref[0]


def modulation_pallas(cond, mod_w, mod_b):
    depth, dm, n_out = mod_w.shape
    tn = dm
    return pl.pallas_call(
        _modulation_kernel,
        grid=(depth, n_out // tn),
        in_specs=[
            pl.BlockSpec((8, dm), lambda i, n: (0, 0)),
            pl.BlockSpec((1, dm, tn), lambda i, n: (i, 0, n)),
            pl.BlockSpec((1, 1, tn), lambda i, n: (i, 0, n)),
        ],
        out_specs=pl.BlockSpec((1, 8, tn), lambda i, n: (i, 0, n)),
        out_shape=jax.ShapeDtypeStruct((depth, 8, n_out), jnp.float32),
        compiler_params=pltpu.CompilerParams(dimension_semantics=("arbitrary", "arbitrary"), vmem_limit_bytes=VMEM_LIMIT),
        name="modulation",
    )(cond, mod_w, mod_b.reshape(depth, 1, n_out))


def kernel(x, c, ctx, c_ctx, mod_w, mod_b, ln_g, ln_b,
           s5_a_re, s5_a_im, s5_log_dt, s5_b_re, s5_b_im, s5_c_re, s5_c_im, s5_d, s5_w_gate, s5_w_val,
           ssd_w_in, ssd_conv_w, ssd_conv_b, ssd_dt_bias, ssd_a_log, ssd_d, ssd_norm_w, ssd_w_out,
           mla_w_down, mla_q_norm, mla_kv_norm, mla_w_uq, mla_w_uk, mla_w_uv, mla_w_o,
           peer_w_q, peer_subkeys, peer_u, peer_v):
    ROWS = x.shape[1] // GRID_W
    pos = grid_positions(ROWS)
    ctx_len = ctx.shape[1]
    bsz, seq_len, dm = x.shape
    n_lat, n_ctx = bsz * seq_len, bsz * ctx_len
    xa = jnp.concatenate([x.reshape(n_lat, dm), ctx.reshape(n_ctx, dm)], axis=0)
    cond = jnp.concatenate([c, c_ctx[None], jnp.zeros((7 - bsz, dm), x.dtype)], axis=0)
    mod_all = modulation_pallas(cond, mod_w, mod_b)
    for i in range(DEPTH):
        last = i == DEPTH - 1
        mod_tab = mod_all[i]
        mod3 = mod_tab.reshape(8, 1, 6 * dm)
        kind, j = i % N_MIXERS, i // N_MIXERS
        if kind == 0:
            prep = _s5_prep_fast(s5_a_re[j], s5_a_im[j], s5_log_dt[j], s5_b_re[j], s5_b_im[j], s5_c_re[j], s5_c_im[j], s5_d[j])
            ya = s5_ssm_pallas(xa, mod_tab, n_lat, seq_len, bsz, prep)
            if last:
                ya, xa = ya[:n_lat], xa[:n_lat]
            xa = glu_ln_pallas(ya, xa, mod3, n_lat, seq_len, bsz, s5_w_val[j], s5_w_gate[j], ln_g[i, 0], ln_b[i, 0])
        elif kind == 1:
            xa = ssd_ln_pallas(xa, mod3, n_lat, seq_len, bsz, ssd_w_in[j], ssd_conv_w[j], ssd_conv_b[j], ssd_dt_bias[j],
                               ssd_a_log[j], ssd_d[j], ssd_norm_w[j], ssd_w_out[j], ln_g[i, 0], ln_b[i, 0])
            if last:
                xa = xa[:n_lat]
        else:
            xa = mla_ln_pallas(xa, mod3, n_lat, seq_len, bsz, mla_w_down[j], mla_q_norm[j], mla_kv_norm[j],
                               mla_w_uq[j], mla_w_uk[j], mla_w_uv[j], mla_w_o[j], ln_g[i, 0], ln_b[i, 0])
            if last:
                xa = xa[:n_lat]
        xa = peer_ln_pallas(xa, mod3, n_lat, seq_len, bsz, peer_w_q[i], peer_subkeys[i], peer_u[i], peer_v[i],
                            ln_g[i, 1], ln_b[i, 1])
    return xa[:n_lat].reshape(bsz, seq_len, dm)
```

```python
import math
import functools
import jax
import jax.numpy as jnp
from jax import lax
import numpy as np
from jax.experimental import pallas as pl
from jax.experimental.pallas import tpu as pltpu

D_MODEL = 1024
BATCH = 4
SEQ = 4096
DEPTH = 4

GRID_W = 64
CTX_LEN = 256
N_MIXERS = 3
DEEPNORM_ALPHA = (2.0 * DEPTH) ** 0.25
LN_EPS = 1e-5
RMS_EPS = 1e-6
ROPE_BASE = 10000.0

S5_GROUP = 16
S5_GROUPS = D_MODEL // S5_GROUP
S5_STATE = 64

SSD_D_INNER = 2 * D_MODEL
SSD_HEADDIM = 64
SSD_HEADS = SSD_D_INNER // SSD_HEADDIM
SSD_GROUPS = 4
SSD_STATE = 128
SSD_CONV = 3
SSD_CHUNK = 128
SSD_CONV_DIM = SSD_D_INNER + 2 * SSD_GROUPS * SSD_STATE
SSD_IN_DIM = SSD_D_INNER + SSD_CONV_DIM + 2 * SSD_HEADS

MLA_HEADS = 16
MLA_Q_RANK = 256
MLA_KV_RANK = 128
MLA_NOPE = 64
MLA_ROPE = 32
MLA_V = 64
MLA_BLOCK = 128

PEER_HEADS = 8
PEER_KEYS = 128
PEER_EXPERTS = PEER_KEYS * PEER_KEYS
PEER_QDIM = 256
PEER_TOPK = 16
PEER_BLOCK = 128


def layer_norm(x, g, b):
    xf = x.astype(jnp.float32)
    mu = jnp.mean(xf, axis=-1, keepdims=True)
    var = jnp.mean(jnp.square(xf - mu), axis=-1, keepdims=True)
    return ((xf - mu) * lax.rsqrt(var + LN_EPS)).astype(x.dtype) * g + b


def _ln_kernel(x_ref, g_ref, b_ref, o_ref):
    xf = x_ref[...]
    mu = jnp.mean(xf, axis=-1, keepdims=True)
    xc = xf - mu
    var = jnp.mean(xc * xc, axis=-1, keepdims=True)
    o_ref[...] = xc * lax.rsqrt(var + LN_EPS) * g_ref[...] + b_ref[...]


def layer_norm_pallas(x, g, b):
    shp = x.shape
    x2 = x.reshape(-1, shp[-1])
    n, d = x2.shape
    tb = 512
    out = pl.pallas_call(
        _ln_kernel,
        grid=(n // tb,),
        in_specs=[pl.BlockSpec((tb, d), lambda i: (i, 0)),
                  pl.BlockSpec((1, d), lambda i: (0, 0)),
                  pl.BlockSpec((1, d), lambda i: (0, 0))],
        out_specs=pl.BlockSpec((tb, d), lambda i: (i, 0)),
        out_shape=jax.ShapeDtypeStruct((n, d), jnp.float32),
        name="final_ln",
    )(x2, g.reshape(1, d), b.reshape(1, d))
    return out.reshape(shp)


def rms_norm(x, g):
    xf = x.astype(jnp.float32)
    return (xf * lax.rsqrt(jnp.mean(jnp.square(xf), axis=-1, keepdims=True) + RMS_EPS)).astype(x.dtype) * g


def modulate(x, shift, scale):
    return x * (1.0 + scale) + shift


def grid_positions(rows):
    row = jnp.repeat(jnp.arange(rows, dtype=jnp.float32), GRID_W)
    col = jnp.tile(jnp.arange(GRID_W, dtype=jnp.float32), rows)
    return row, col


def rope_axial(x, row, col):
    half = x.shape[-1] // 2
    quarter = half // 2
    freqs = ROPE_BASE ** (-jnp.arange(quarter, dtype=jnp.float32) / quarter)

    def rot(xp, pos):
        ang = pos[:, None] * freqs
        cos = jnp.cos(ang)[None, :, None, :].astype(x.dtype)
        sin = jnp.sin(ang)[None, :, None, :].astype(x.dtype)
        x1, x2 = xp[..., :quarter], xp[..., quarter:]
        return jnp.concatenate([x1 * cos - x2 * sin, x2 * cos + x1 * sin], axis=-1)

    return jnp.concatenate([rot(x[..., :half], row), rot(x[..., half:], col)], axis=-1)


def _lin_rec(left, right):
    a1, b1 = left
    a2, b2 = right
    return a1 * a2, a2 * b1 + b2


def s5_scan(u, a_bar, b_bar, c_mat, init, reverse):
    bu = jnp.einsum('lgc,gpc->lgp', u.astype(jnp.float32), b_bar)
    a = jnp.broadcast_to(a_bar, bu.shape)
    a_cum, s = lax.associative_scan(_lin_rec, (a, bu), reverse=reverse, axis=0)
    s = s + a_cum * init
    y = jnp.einsum('lgp,gcp->lgc', s, c_mat).real
    final = s[0] if reverse else s[-1]
    return y, final


def s5_direction(u_ctx, u_lat, a_re, a_im, log_dt, b_re, b_im, c_re, c_im, reverse):
    lam = lax.complex(a_re.astype(jnp.float32), a_im.astype(jnp.float32))
    a_bar = jnp.exp(lam * jnp.exp(log_dt.astype(jnp.float32))[:, None])
    b_mat = lax.complex(b_re.astype(jnp.float32), b_im.astype(jnp.float32))
    b_bar = ((a_bar - 1.0) / lam)[..., None] * b_mat
    c_mat = lax.complex(c_re.astype(jnp.float32), c_im.astype(jnp.float32))

    def per_sample(args):
        uc, ul = args
        yc, sc = s5_scan(uc, a_bar, b_bar, c_mat, jnp.zeros_like(a_bar), reverse)
        yl, _ = s5_scan(ul, a_bar, b_bar, c_mat, sc, reverse)
        return yc, yl

    return lax.map(per_sample, (u_ctx, u_lat))


def s5_mixer(h, hc, a_re, a_im, log_dt, b_re, b_im, c_re, c_im, d, w_gate, w_val):
    bsz, seq_len, _ = h.shape
    ctx_len = hc.shape[1]
    u = h.reshape(bsz, seq_len, S5_GROUPS, S5_GROUP)
    uc = hc.reshape(bsz, ctx_len, S5_GROUPS, S5_GROUP)
    y_l = d * h
    y_c = d * hc
    for direction in range(2):
        yc, yl = s5_direction(uc, u, a_re[direction], a_im[direction], log_dt[direction],
                              b_re[direction], b_im[direction], c_re[direction], c_im[direction],
                              direction == 1)
        y_l = y_l + yl.reshape(bsz, seq_len, D_MODEL).astype(h.dtype)
        y_c = y_c + yc.reshape(bsz, ctx_len, D_MODEL).astype(h.dtype)

    def glu(y):
        g = jax.nn.gelu(y)
        return (g @ w_val) * jax.nn.sigmoid(g @ w_gate)

    return glu(y_l), glu(y_c)


def depthwise_conv_centred(x, w, b):
    k_w = w.shape[0]
    pad = k_w // 2
    seq_len = x.shape[1]
    xp = jnp.pad(x, ((0, 0), (pad, pad), (0, 0)))
    out = b
    for k in range(k_w):
        out = out + xp[:, k:k + seq_len] * w[k]
    return out


def ssd_scan(x, dt, a, bm, cm, init):
    bsz, seq_len, n_heads, p_dim = x.shape
    n_grp, n_st = bm.shape[2], bm.shape[3]
    hg = n_heads // n_grp
    q_len = SSD_CHUNK
    nc = seq_len // q_len
    xc = x.reshape(bsz, nc, q_len, n_grp, hg, p_dim)
    dtc = dt.reshape(bsz, nc, q_len, n_grp, hg).astype(jnp.float32)
    bc = bm.reshape(bsz, nc, q_len, n_grp, n_st)
    cc = cm.reshape(bsz, nc, q_len, n_grp, n_st)
    a_cum = jnp.cumsum(dtc * a.reshape(n_grp, hg), axis=2)
    xdt = xc * dtc[..., None].astype(x.dtype)
    tri = jnp.tril(jnp.ones((q_len, q_len), dtype=bool))
    seg = a_cum[:, :, :, None] - a_cum[:, :, None]
    decay = jnp.exp(jnp.where(tri[:, :, None, None], seg, -jnp.inf)).astype(x.dtype)
    cb = jnp.einsum('bcqgn,bcsgn->bcgqs', cc, bc)
    y_diag = jnp.einsum('bcgqs,bcqsgh,bcsghp->bcqghp', cb, decay, xdt)
    decay_states = jnp.exp(a_cum[:, :, -1:] - a_cum).astype(x.dtype)
    states = jnp.einsum('bcsgn,bcsgh,bcsghp->bcghpn', bc, decay_states, xdt)
    chunk_decay = jnp.exp(a_cum[:, :, -1]).astype(x.dtype)

    def step(carry, inp):
        dec, st = inp
        return carry * dec[..., None, None] + st, carry

    final, prev = lax.scan(step, init, (jnp.moveaxis(chunk_decay, 1, 0), jnp.moveaxis(states, 1, 0)))
    y_off = jnp.einsum('bcqgn,cbghpn,bcqgh->bcqghp', cc, prev, jnp.exp(a_cum).astype(x.dtype))
    return (y_diag + y_off).reshape(bsz, seq_len, n_heads, p_dim), final


def ssd_mixer(h, hc, w_in, conv_w, conv_b, dt_bias, a_log, d, norm_w, w_out):
    a = -jnp.exp(a_log.astype(jnp.float32))

    def flip(t):
        return jnp.flip(t, axis=1)

    def bidir(t, inits):
        bsz, seq_len = t.shape[0], t.shape[1]
        z, xbc, dt = jnp.split(t @ w_in, [SSD_D_INNER, SSD_D_INNER + SSD_CONV_DIM], axis=-1)
        xbc = jax.nn.silu(depthwise_conv_centred(xbc, conv_w, conv_b))
        xs, bm, cm = jnp.split(xbc, [SSD_D_INNER, SSD_D_INNER + SSD_GROUPS * SSD_STATE], axis=-1)
        xs = xs.reshape(bsz, seq_len, SSD_HEADS, SSD_HEADDIM)
        bm = bm.reshape(bsz, seq_len, SSD_GROUPS, SSD_STATE)
        cm = cm.reshape(bsz, seq_len, SSD_GROUPS, SSD_STATE)
        dt = jax.nn.softplus(dt.reshape(bsz, seq_len, 2, SSD_HEADS) + dt_bias)
        y_f, s_f = ssd_scan(xs, dt[:, :, 0], a[0], bm, cm, inits[0])
        y_b, s_b = ssd_scan(flip(xs), flip(dt[:, :, 1]), a[1], flip(bm), flip(cm), inits[1])
        y = y_f + flip(y_b) + d[:, None] * xs
        y = rms_norm(y.reshape(z.shape) * jax.nn.silu(z), norm_w)
        return y @ w_out, (s_f, s_b)

    zero = jnp.zeros((h.shape[0], SSD_GROUPS, SSD_HEADS // SSD_GROUPS, SSD_HEADDIM, SSD_STATE), h.dtype)
    out_c, states_c = bidir(hc, (zero, zero))
    out_l, _ = bidir(h, states_c)
    return out_l, out_c


def mla_project(t, w_down, q_norm, kv_norm, w_uq, w_uk, w_uv, pos):
    cq, ckv, kr = jnp.split(t @ w_down, [MLA_Q_RANK, MLA_Q_RANK + MLA_KV_RANK], axis=-1)
    q = jnp.einsum('btr,rhd->bthd', rms_norm(cq, q_norm), w_uq)
    ckv = rms_norm(ckv, kv_norm)
    k_nope = jnp.einsum('btr,rhd->bthd', ckv, w_uk)
    v = jnp.einsum('btr,rhd->bthd', ckv, w_uv)
    q_nope, q_rope = q[..., :MLA_NOPE], q[..., MLA_NOPE:]
    kr = kr[:, :, None, :]
    if pos is not None:
        q_rope = rope_axial(q_rope, pos[0], pos[1])
        kr = rope_axial(kr, pos[0], pos[1])
    k = jnp.concatenate([k_nope, jnp.broadcast_to(kr, k_nope.shape[:3] + (MLA_ROPE,))], axis=-1)
    q = jnp.concatenate([q_nope, q_rope], axis=-1)
    return q, k, v


def attend(q, k, v):
    s = jnp.einsum('bqhd,bkhd->bhqk', q, k).astype(jnp.float32) * (MLA_NOPE + MLA_ROPE) ** -0.5
    p = jax.nn.softmax(s, axis=-1).astype(v.dtype)
    return jnp.einsum('bhqk,bkhd->bqhd', p, v)


def mla_mixer(h, hc, pos, w_down, q_norm, kv_norm, w_uq, w_uk, w_uv, w_o):
    bsz, seq_len, _ = h.shape
    qc, kc, vc = mla_project(hc, w_down, q_norm, kv_norm, w_uq, w_uk, w_uv, None)
    ql, kl, vl = mla_project(h, w_down, q_norm, kv_norm, w_uq, w_uk, w_uv, pos)
    out_c = attend(qc, kc, vc).reshape(bsz, hc.shape[1], MLA_HEADS * MLA_V)
    k_all = jnp.concatenate([kc, kl], axis=1)
    v_all = jnp.concatenate([vc, vl], axis=1)
    nb = seq_len // MLA_BLOCK
    qb = jnp.moveaxis(ql.reshape(bsz, nb, MLA_BLOCK, MLA_HEADS, MLA_NOPE + MLA_ROPE), 1, 0)
    out_l = lax.map(lambda qblk: attend(qblk, k_all, v_all), qb)
    out_l = jnp.moveaxis(out_l, 0, 1).reshape(bsz, seq_len, MLA_HEADS * MLA_V)
    return out_l @ w_o, out_c @ w_o


def peer_ffn(h, w_q, subkeys, u_tab, v_tab):
    bsz, seq_len, dm = h.shape
    half = PEER_QDIM // 2
    q = jnp.einsum('btd,dhk->bthk', h, w_q)
    s1 = jnp.einsum('bthk,hnk->bthn', q[..., :half], subkeys[:, 0])
    s2 = jnp.einsum('bthk,hnk->bthn', q[..., half:], subkeys[:, 1])
    v1, i1 = lax.top_k(s1, PEER_TOPK)
    v2, i2 = lax.top_k(s2, PEER_TOPK)
    n_cand = PEER_TOPK * PEER_TOPK
    cand = (v1[..., :, None] + v2[..., None, :]).reshape(bsz, seq_len, PEER_HEADS, n_cand)
    cidx = (i1[..., :, None] * PEER_KEYS + i2[..., None, :]).reshape(bsz, seq_len, PEER_HEADS, n_cand)
    best, sel = lax.top_k(cand, PEER_TOPK)
    eidx = jnp.take_along_axis(cidx, sel, axis=-1)
    gate = jax.nn.softmax(best.astype(jnp.float32), axis=-1).astype(h.dtype)
    n_blk = (bsz * seq_len) // PEER_BLOCK
    n_sel = PEER_HEADS * PEER_TOPK
    hb = h.reshape(n_blk, PEER_BLOCK, dm)
    ib = eidx.reshape(n_blk, PEER_BLOCK, n_sel)
    gb = gate.reshape(n_blk, PEER_BLOCK, n_sel)

    def block(args):
        hk, ik, gk = args
        u = jnp.take(u_tab, ik, axis=0)
        act = jax.nn.gelu(jnp.einsum('td,ted->te', hk, u))
        v = jnp.take(v_tab, ik, axis=0)
        return jnp.einsum('te,ted->td', gk * act, v)

    return lax.map(block, (hb, ib, gb)).reshape(bsz, seq_len, dm)


PEER_TM = 512
PEER_EB = 1024
PEER_ROWS = 16
PEER_LANES = 128
PEER_ACT_ROWS = 512
VMEM_LIMIT = 56 * 1024 * 1024
NEG_INF = float("-inf")
GELU_C0 = math.sqrt(2.0 / math.pi)
GELU_C1 = 0.044715 * GELU_C0


def _top16_sorted(s):
    n = PEER_TOPK
    m = s.shape[0] // 8
    x = [s[8 * r:8 * r + 8, :] for r in range(m)]

    def exchange(i, l, descending):
        hi, lo = jnp.maximum(x[i], x[l]), jnp.minimum(x[i], x[l])
        x[i], x[l] = (hi, lo) if descending else (lo, hi)

    def merge_bitonic(size):
        j = size // 2
        while j >= 1:
            for i in range(size):
                if i ^ j > i:
                    exchange(i, i ^ j, True)
            j //= 2

    k = 2
    while k < m:
        j = k // 2
        while j >= 1:
            for i in range(m):
                if i ^ j > i:
                    exchange(i, i ^ j, (i & k) == 0)
            j //= 2
        k *= 2
    merge_bitonic(m)
    shifts = [4, 2, 1]
    if m < n:
        first = shifts.pop(0)
        x = x + [pltpu.roll(x[m - 1 - r], first, axis=0) for r in range(m)]
        merge_bitonic(n)
    for shift in shifts:
        other = [pltpu.roll(x[n - 1 - r], shift, axis=0) for r in range(n)]
        for r in range(n):
            x[r] = jnp.maximum(x[r], other[r])
        merge_bitonic(n)
    return [x[r][0:1, :] for r in range(n)]


def _dot3(a_hi, a_lo, b_hi, b_lo):
    f32 = jnp.float32
    return (jnp.dot(a_hi, b_hi, preferred_element_type=f32)
            + jnp.dot(a_hi, b_lo, preferred_element_type=f32)
            + jnp.dot(a_lo, b_hi, preferred_element_type=f32))


def _split_bf16(v):
    hi = v.astype(jnp.bfloat16)
    lo = (v - hi.astype(jnp.float32)).astype(jnp.bfloat16)
    return hi, lo


def _peer_kernel(x_ref, mod_ref, wq_ref, skh_ref, skl_ref, u_ref, vt_ref, vtl_ref, lng_ref, lnb_ref,
                 o_ref,
                 ht_ref, s_ref, n1_ref, e1_ref, r2_ref, e2_ref, v1_ref, v2_ref, cand_ref,
                 act_ref, gact_ref, acc_ref):
    f32 = jnp.float32
    bf16 = jnp.bfloat16
    eb = pl.program_id(1)
    n_eb = pl.num_programs(1)
    dm = x_ref.shape[1]
    a_per = PEER_EB // PEER_KEYS
    half = PEER_QDIM // 2

    @pl.when(eb == 0)
    def _prologue():
        mod = mod_ref[0]
        h = x_ref[...] * (1.0 + mod[:, 4 * dm:5 * dm]) + mod[:, 3 * dm:4 * dm]
        ht = h.T.astype(bf16)
        ht_ref[...] = ht
        qt = jnp.dot(wq_ref[...], ht, preferred_element_type=f32)
        cand_ref[...] = jnp.full(cand_ref.shape, NEG_INF, f32)
        gact_ref[...] = jnp.zeros(gact_ref.shape, bf16)
        for hd in range(PEER_HEADS):
            q1_hi, q1_lo = _split_bf16(qt[hd * PEER_QDIM: hd * PEER_QDIM + half])
            q2_hi, q2_lo = _split_bf16(qt[hd * PEER_QDIM + half: (hd + 1) * PEER_QDIM])
            s_ref[0] = _dot3(skh_ref[hd, 0], skl_ref[hd, 0], q1_hi, q1_lo)
            s_ref[1] = _dot3(skh_ref[hd, 1], skl_ref[hd, 1], q2_hi, q2_lo)
            for lg in range(0, x_ref.shape[0], PEER_LANES):
                lanes = slice(lg, lg + PEER_LANES)
                s1, s2 = s_ref[0, :, lanes], s_ref[1, :, lanes]
                top1 = _top16_sorted(s1)
                top2 = _top16_sorted(s2)
                for k in range(PEER_TOPK):
                    v1_ref[k:k + 1, lanes] = top1[k]
                    v2_ref[k:k + 1, lanes] = top2[k]
                off = 0
                for i in range(PEER_TOPK):
                    cnt = PEER_TOPK // (i + 1)
                    cand_ref[off:off + cnt, lanes] = v1_ref[i:i + 1, lanes] + v2_ref[0:cnt, lanes]
                    off += cnt
                best = _top16_sorted(cand_ref[:, lanes])
                z = jnp.zeros_like(best[0])
                for k in range(PEER_TOPK):
                    z = z + jnp.exp(best[k] - best[0])
                tau = best[PEER_TOPK - 1]
                n1 = jnp.zeros(s1.shape, f32)
                for k in range(PEER_TOPK):
                    n1 = jnp.where(s1 + top2[k] >= tau, k + 1.0, n1)
                n1_ref[hd, :, lanes] = n1
                e1_ref[hd, :, lanes] = jnp.exp(s1 - top1[0])
                rank2 = jnp.zeros(s2.shape, f32)
                for k in range(PEER_TOPK):
                    rank2 = jnp.where(top2[k] > s2, k + 1.0, rank2)
                r2_ref[hd, :, lanes] = rank2.astype(bf16)
                e2_ref[hd, :, lanes] = (jnp.exp(s2 - top2[0]) * (0.5 / z)).astype(bf16)
        acc_ref[...] = jnp.zeros(acc_ref.shape, f32)

    par = eb % 2
    for r0 in range(0, PEER_EB, PEER_ACT_ROWS):
        act_ref[r0:r0 + PEER_ACT_ROWS, :] = jnp.dot(u_ref[r0:r0 + PEER_ACT_ROWS, :], ht_ref[...],
                                                    preferred_element_type=f32)
    acc_ref[...] += jnp.dot(vt_ref[0], gact_ref[1 - par], preferred_element_type=f32)
    a_base = pl.multiple_of(eb * a_per, a_per)
    blk = (PEER_ROWS, x_ref.shape[0])
    n_slab = PEER_KEYS // PEER_ROWS
    for a in range(a_per):
        g = [None] * n_slab
        for hd in range(PEER_HEADS):
            n1a = jnp.broadcast_to(n1_ref[hd, pl.ds(a_base, a_per), :][a:a + 1, :].astype(bf16), blk)
            e1a = jnp.broadcast_to(e1_ref[hd, pl.ds(a_base, a_per), :][a:a + 1, :].astype(bf16), blk)
            for i in range(n_slab):
                rs = slice(i * PEER_ROWS, (i + 1) * PEER_ROWS)
                w = jnp.where(r2_ref[hd, rs, :] < n1a, e2_ref[hd, rs, :], jnp.zeros(blk, bf16)) * e1a
                g[i] = w if g[i] is None else g[i] + w
        for i in range(n_slab):
            rows = slice(a * PEER_KEYS + i * PEER_ROWS, a * PEER_KEYS + (i + 1) * PEER_ROWS)
            x = act_ref[rows, :]
            t = jnp.tanh(x * (GELU_C0 + GELU_C1 * (x * x)))
            gact_ref[par, rows, :] = g[i] * (x + x * t).astype(bf16)

    @pl.when(eb == n_eb - 1)
    def _epilogue():
        mod = mod_ref[0]
        ffn_t = acc_ref[...] + jnp.dot(vtl_ref[0], gact_ref[par], preferred_element_type=f32)
        y = DEEPNORM_ALPHA * x_ref[...] + mod[:, 5 * dm:6 * dm] * ffn_t.T
        mu = jnp.mean(y, axis=-1, keepdims=True)
        yc = y - mu
        var = jnp.mean(yc * yc, axis=-1, keepdims=True)
        o_ref[...] = yc * lax.rsqrt(var + LN_EPS) * lng_ref[...] + lnb_ref[...]


def _mod_row_index(i, tm, n_lat, seq, bsz):
    return jnp.where(i * tm < n_lat, (i * tm) // seq, bsz)


def peer_ln_pallas(xa, mod3, n_lat, seq, bsz, w_q, subkeys, u_tab, v_tab, ln_g, ln_b):
    n_tok, dm = xa.shape
    tm = PEER_TM
    f32, bf16 = jnp.float32, jnp.bfloat16
    wq_t = w_q.reshape(dm, PEER_HEADS * PEER_QDIM).T
    wq_bf = wq_t.astype(bf16)
    sk_hi = subkeys.astype(bf16)
    sk_lo = (subkeys - sk_hi.astype(f32)).astype(bf16)
    u_bf = u_tab.astype(bf16)
    n_eb = PEER_EXPERTS // PEER_EB
    vt_bf = v_tab.reshape(n_eb, PEER_EB, dm).transpose(0, 2, 1).astype(bf16)
    qd = PEER_HEADS * PEER_QDIM
    n_cand = sum(PEER_TOPK // (i + 1) for i in range(PEER_TOPK))
    n_cand_pad = 64
    assert n_cand <= n_cand_pad
    mod_idx = functools.partial(_mod_row_index, tm=tm, n_lat=n_lat, seq=seq, bsz=bsz)
    return pl.pallas_call(
        _peer_kernel,
        grid=(n_tok // tm, n_eb),
        in_specs=[
            pl.BlockSpec((tm, dm), lambda i, e: (i, 0)),
            pl.BlockSpec((1, 1, 6 * dm), lambda i, e: (mod_idx(i), 0, 0)),
            pl.BlockSpec((qd, dm), lambda i, e: (0, 0)),
            pl.BlockSpec((PEER_HEADS, 2, PEER_KEYS, PEER_QDIM // 2), lambda i, e: (0, 0, 0, 0)),
            pl.BlockSpec((PEER_HEADS, 2, PEER_KEYS, PEER_QDIM // 2), lambda i, e: (0, 0, 0, 0)),
            pl.BlockSpec((PEER_EB, dm), lambda i, e: (e, 0)),
            pl.BlockSpec((1, dm, PEER_EB), lambda i, e: (jnp.maximum(e - 1, 0), 0, 0)),
            pl.BlockSpec((1, dm, PEER_EB), lambda i, e: (n_eb - 1, 0, 0)),
            pl.BlockSpec((1, dm), lambda i, e: (0, 0)),
            pl.BlockSpec((1, dm), lambda i, e: (0, 0)),
        ],
        out_specs=pl.BlockSpec((tm, dm), lambda i, e: (i, 0)),
        out_shape=jax.ShapeDtypeStruct((n_tok, dm), f32),
        scratch_shapes=[
            pltpu.VMEM((dm, tm), bf16),
            pltpu.VMEM((2, PEER_KEYS, tm), f32),
            pltpu.VMEM((PEER_HEADS, PEER_KEYS, tm), f32),
            pltpu.VMEM((PEER_HEADS, PEER_KEYS, tm), f32),
            pltpu.VMEM((PEER_HEADS, PEER_KEYS, tm), bf16),
            pltpu.VMEM((PEER_HEADS, PEER_KEYS, tm), bf16),
            pltpu.VMEM((PEER_TOPK, tm), f32),
            pltpu.VMEM((PEER_TOPK, tm), f32),
            pltpu.VMEM((n_cand_pad, tm), f32),
            pltpu.VMEM((PEER_EB, tm), f32),
            pltpu.VMEM((2, PEER_EB, tm), bf16),
            pltpu.VMEM((dm, tm), f32),
        ],
        compiler_params=pltpu.CompilerParams(
            dimension_semantics=("arbitrary", "arbitrary"),
            vmem_limit_bytes=VMEM_LIMIT),
        name="peer_ln",
    )(xa, mod3, wq_bf, sk_hi, sk_lo, u_bf, vt_bf, vt_bf, ln_g.reshape(1, dm), ln_b.reshape(1, dm))


S5_CHUNK = 16
S5_LEVELS = 9


def _s5_prep(a_re, a_im, log_dt, b_re, b_im, c_re, c_im, d):
    f32, bf16 = jnp.float32, jnp.bfloat16
    hp = lax.Precision.HIGHEST
    n_g, n_p, q = S5_GROUPS, S5_STATE, S5_CHUNK
    dt = jnp.exp(log_dt.astype(f32))[..., None]

    def apow(n):
        mag = jnp.exp(a_re * dt * n)
        ang = a_im * dt * n
        return mag * jnp.cos(ang), mag * jnp.sin(ang)

    ar1, ai1 = apow(1.0)
    den = a_re * a_re + a_im * a_im
    nr, ni = ar1 - 1.0, ai1
    cr = (nr * a_re + ni * a_im) / den
    ci = (ni * a_re - nr * a_im) / den
    bbr = cr[..., None] * b_re - ci[..., None] * b_im
    bbi = cr[..., None] * b_im + ci[..., None] * b_re
    lag = jnp.arange(q + 1, dtype=f32)[:, None, None, None]
    pr, pi = apow(lag)
    mr = pr[..., None] * bbr - pi[..., None] * bbi
    mi = pr[..., None] * bbi + pi[..., None] * bbr
    kmat = (jnp.einsum('dgcp,ndgpk->ndgck', c_re, mr, precision=hp)
            - jnp.einsum('dgcp,ndgpk->ndgck', c_im, mi, precision=hp))
    r_idx = jnp.arange(q)[:, None]
    t_idx = jnp.arange(q)[None, :]

    def toeplitz(kd, lagm):
        blk = kd[jnp.clip(lagm, 0, q)]
        blk = jnp.where((lagm >= 0)[:, :, None, None, None], blk, 0.0)
        return blk.transpose(2, 0, 4, 1, 3).reshape(n_g, q * S5_GROUP, q * S5_GROUP)

    eye = jnp.eye(q * S5_GROUP, dtype=f32)
    dvec = jnp.tile(d.reshape(n_g, 1, S5_GROUP), (1, q, 1)).reshape(n_g, q * S5_GROUP)
    tsum = toeplitz(kmat[:, 0], t_idx - r_idx) + toeplitz(kmat[:, 1], r_idx - t_idx) + eye[None] * dvec[:, None, :]

    def w_in(direction, exps):
        wr = mr[exps, direction].transpose(1, 0, 3, 2).reshape(n_g, q * S5_GROUP, n_p)
        wi = mi[exps, direction].transpose(1, 0, 3, 2).reshape(n_g, q * S5_GROUP, n_p)
        return wr, wi

    def w_out(direction, exps):
        pre, pim = pr[exps, direction], pi[exps, direction]
        cre, cim = c_re[direction], c_im[direction]
        wre = cre[None] * pre[:, :, None, :] - cim[None] * pim[:, :, None, :]
        wim = -(cre[None] * pim[:, :, None, :] + cim[None] * pre[:, :, None, :])
        return (wre.transpose(1, 3, 0, 2).reshape(n_g, n_p, q * S5_GROUP),
                wim.transpose(1, 3, 0, 2).reshape(n_g, n_p, q * S5_GROUP))

    steps = jnp.arange(q)
    win = [w_in(0, q - 1 - steps), w_in(1, steps)]
    wout = [w_out(0, steps + 1), w_out(1, q - steps)]
    zc = jnp.zeros((n_g, q * S5_GROUP, n_p), f32)
    zr = jnp.zeros((n_g, n_p, q * S5_GROUP), f32)
    win_p, wout_p = [], []
    for direction in range(2):
        wr, wi = win[direction]
        even = jnp.concatenate([wr, zc, wi, zc], axis=2)
        odd = jnp.concatenate([zc, wr, zc, wi], axis=2)
        is_odd = (jnp.arange(n_g) % 2 == 1)[:, None, None]
        win_p.append(jnp.where(is_odd, odd, even))
        vr, vi = wout[direction]
        even = jnp.concatenate([vr, zr, vi, zr], axis=1)
        odd = jnp.concatenate([zr, vr, zr, vi], axis=1)
        wout_p.append(jnp.where(is_odd, odd, even))
    win_p = jnp.stack(win_p, axis=1).reshape(n_g // 2, 2, 2, 4 * n_p, 4 * n_p).transpose(0, 2, 1, 3, 4)
    wout_p = jnp.stack(wout_p, axis=1).reshape(n_g // 2, 2, 2, 4 * n_p, 4 * n_p).transpose(0, 2, 1, 3, 4)
    lvl = (q * 2.0 ** jnp.arange(S5_LEVELS, dtype=f32))[:, None, None, None]
    lr, li = apow(lvl)
    pw = jnp.stack([lr, li], axis=1)
    pw = pw.transpose(3, 2, 0, 1, 4).reshape(n_g // 2, 2, 2, S5_LEVELS, 2, n_p)
    pw = pw.transpose(0, 2, 3, 4, 1, 5).reshape(n_g // 2, 2, 2 * S5_LEVELS, 2 * n_p)
    return tsum.astype(bf16).reshape(n_g // 2, 2, q * S5_GROUP, q * S5_GROUP), win_p.astype(bf16), wout_p.astype(bf16), pw


def _s5_prep_fast(a_re, a_im, log_dt, b_re, b_im, c_re, c_im, d):
    f32, bf16 = jnp.float32, jnp.bfloat16
    hp = lax.Precision.HIGHEST
    n_g, n_p, q = S5_GROUPS, S5_STATE, S5_CHUNK
    width = q * S5_GROUP
    dt = jnp.exp(log_dt.astype(f32))[..., None]

    def apow(n):
        mag = jnp.exp(a_re * dt * n)
        ang = a_im * dt * n
        return mag * jnp.cos(ang), mag * jnp.sin(ang)

    def cmul(xr, xi, yr, yi):
        return xr * yr - xi * yi, xr * yi + xi * yr

    ar1, ai1 = apow(1.0)
    den = a_re * a_re + a_im * a_im
    nr, ni = ar1 - 1.0, ai1
    cr = (nr * a_re + ni * a_im) / den
    ci = (ni * a_re - nr * a_im) / den
    bbr = cr[..., None] * b_re - ci[..., None] * b_im
    bbi = cr[..., None] * b_im + ci[..., None] * b_re
    steps = jnp.arange(q, dtype=f32)

    def powers(direction, sign):
        mag = jnp.exp(a_re[direction][:, None, :] * dt[direction][:, None, :] * (sign * steps)[None, :, None])
        ang = a_im[direction][:, None, :] * dt[direction][:, None, :] * (sign * steps)[None, :, None]
        return mag * jnp.cos(ang), mag * jnp.sin(ang)

    def factors(direction):
        lsign = -1.0 if direction == 0 else 1.0
        pr_l, pi_l = powers(direction, lsign)
        pr_r, pi_r = powers(direction, -lsign)
        bt_r = bbr[direction].transpose(0, 2, 1)[:, None, :, :]
        bt_i = bbi[direction].transpose(0, 2, 1)[:, None, :, :]
        lt_r, lt_i = cmul(bt_r, bt_i, pr_l[:, :, None, :], pi_l[:, :, None, :])
        ct_r = c_re[direction].transpose(0, 2, 1)[:, :, None, :]
        ct_i = c_im[direction].transpose(0, 2, 1)[:, :, None, :]
        r_r, r_i = cmul(ct_r, ct_i, pr_r.transpose(0, 2, 1)[:, :, :, None], pi_r.transpose(0, 2, 1)[:, :, :, None])
        return (lt_r.reshape(n_g, width, n_p), lt_i.reshape(n_g, width, n_p),
                r_r.reshape(n_g, n_p, width), r_i.reshape(n_g, n_p, width))

    row_t = jnp.arange(width)[:, None] // S5_GROUP
    col_t = jnp.arange(width)[None, :] // S5_GROUP
    tsum = jnp.eye(width, dtype=f32)[None] * jnp.tile(d.reshape(n_g, 1, S5_GROUP), (1, q, 1)).reshape(n_g, 1, width)
    win, wout = [], []
    for direction in range(2):
        lt_r, lt_i, r_r, r_i = factors(direction)
        prod = (jnp.einsum('gxp,gpy->gxy', lt_r, r_r, precision=hp)
                - jnp.einsum('gxp,gpy->gxy', lt_i, r_i, precision=hp))
        mask = (col_t >= row_t) if direction == 0 else (row_t >= col_t)
        tsum = tsum + jnp.where(mask[None], prod, 0.0)
        if direction == 0:
            in_r, in_i = apow(q - 1.0)
            out_r, out_i = ar1, ai1
            w_r, w_i = cmul(lt_r, lt_i, in_r[0][:, None, :], in_i[0][:, None, :])
        else:
            out_r, out_i = apow(float(q))
            w_r, w_i = lt_r, lt_i
        win.append((w_r, w_i))
        o_r, o_i = cmul(r_r, r_i, out_r[direction][:, :, None], out_i[direction][:, :, None])
        wout.append((o_r, -o_i))
    zc = jnp.zeros((n_g, width, n_p), f32)
    zr = jnp.zeros((n_g, n_p, width), f32)
    win_p, wout_p = [], []
    for direction in range(2):
        wr, wi = win[direction]
        even = jnp.concatenate([wr, zc, wi, zc], axis=2)
        odd = jnp.concatenate([zc, wr, zc, wi], axis=2)
        is_odd = (jnp.arange(n_g) % 2 == 1)[:, None, None]
        win_p.append(jnp.where(is_odd, odd, even))
        vr, vi = wout[direction]
        even = jnp.concatenate([vr, zr, vi, zr], axis=1)
        odd = jnp.concatenate([zr, vr, zr, vi], axis=1)
        wout_p.append(jnp.where(is_odd, odd, even))
    win_p = jnp.stack(win_p, axis=1).reshape(n_g // 2, 2, 2, 4 * n_p, 4 * n_p).transpose(0, 2, 1, 3, 4)
    wout_p = jnp.stack(wout_p, axis=1).reshape(n_g // 2, 2, 2, 4 * n_p, 4 * n_p).transpose(0, 2, 1, 3, 4)
    lvl = (q * 2.0 ** jnp.arange(S5_LEVELS, dtype=f32))[:, None, None, None]
    lr, li = apow(lvl)
    pw = jnp.stack([lr, li], axis=1)
    pw = pw.transpose(3, 2, 0, 1, 4).reshape(n_g // 2, 2, 2, S5_LEVELS, 2, n_p)
    pw = pw.transpose(0, 2, 3, 4, 1, 5).reshape(n_g // 2, 2, 2 * S5_LEVELS, 2 * n_p)
    return tsum.astype(bf16).reshape(n_g // 2, 2, width, width), win_p.astype(bf16), wout_p.astype(bf16), pw


def _shift_rows(v, r, up):
    n = v.shape[0]
    row = lax.broadcasted_iota(jnp.int32, v.shape, 0)
    if up:
        return jnp.where(row < n - r, pltpu.roll(v, n - r, axis=0), 0.0)
    return jnp.where(row >= r, pltpu.roll(v, r, axis=0), 0.0)


def _chunk_scan(re, im, pw, rows_per_chunk, up):
    n_chunks = re.shape[0] // rows_per_chunk
    level, s = 0, 1
    while s < n_chunks:
        ar, ai = pw[2 * level:2 * level + 1, :], pw[2 * level + 1:2 * level + 2, :]
        sre = _shift_rows(re, s * rows_per_chunk, up)
        sim = _shift_rows(im, s * rows_per_chunk, up)
        re, im = re + ar * sre - ai * sim, im + ar * sim + ai * sre
        level, s = level + 1, 2 * s
    return re, im


def _s5_kernel(u_ref, scl_ref, shl_ref, scc_ref, shc_ref, t_ref, win_ref, wout_ref, pw_ref, y_ref, *, bsz, ctx_chunks):
    f32, bf16 = jnp.float32, jnp.bfloat16
    n = u_ref.shape[1]
    width = u_ref.shape[2]
    rc = ctx_chunks * bsz
    hs = []
    for gi in range(2):
        xv = u_ref[gi].reshape(n // 8, 8, width)
        hl = xv * scl_ref[gi][None] + shl_ref[gi][None]
        hc = xv * scc_ref[gi][None] + shc_ref[gi][None]
        slab = lax.broadcasted_iota(jnp.int32, xv.shape, 0)
        hs.append(jnp.where(slab < rc // 8, hc, hl).reshape(n, width).astype(bf16))
    y = [jnp.dot(hs[gi], t_ref[0, gi], preferred_element_type=f32) for gi in range(2)]
    half = width // 2
    for direction in range(2):
        sloc = (jnp.dot(hs[0], win_ref[0, direction, 0], preferred_element_type=f32)
                + jnp.dot(hs[1], win_ref[0, direction, 1], preferred_element_type=f32))
        re, im = sloc[:, :half], sloc[:, half:]
        pw = pw_ref[0, direction]
        if direction == 0:
            sre, sim = _chunk_scan(_shift_rows(re, bsz, False), _shift_rows(im, bsz, False), pw, bsz, False)
        else:
            cre, cim = _chunk_scan(_shift_rows(re[:rc], bsz, True), _shift_rows(im[:rc], bsz, True), pw, bsz, True)
            ar, ai = pw[0:1, :], pw[1:2, :]
            fre = ar * cre[0:8] - ai * cim[0:8] + re[0:8]
            fim = ar * cim[0:8] + ai * cre[0:8] + im[0:8]
            row8 = lax.broadcasted_iota(jnp.int32, fre.shape, 0)
            tre = jnp.where(row8 >= 8 - bsz, pltpu.roll(fre, 8 - bsz, axis=0), 0.0)
            tim = jnp.where(row8 >= 8 - bsz, pltpu.roll(fim, 8 - bsz, axis=0), 0.0)
            lre, lim = _shift_rows(re[rc:], bsz, True), _shift_rows(im[rc:], bsz, True)
            lre = jnp.concatenate([lre[:-8], lre[-8:] + tre], axis=0)
            lim = jnp.concatenate([lim[:-8], lim[-8:] + tim], axis=0)
            lre, lim = _chunk_scan(lre, lim, pw, bsz, True)
            sre = jnp.concatenate([cre, lre], axis=0)
            sim = jnp.concatenate([cim, lim], axis=0)
        s_in = jnp.concatenate([sre, sim], axis=1).astype(bf16)
        for gi in range(2):
            y[gi] = y[gi] + jnp.dot(s_in, wout_ref[0, direction, gi], preferred_element_type=f32)
    for gi in range(2):
        y_ref[gi] = y[gi]


def s5_ssm_pallas(xa, mod_tab, n_lat, seq, bsz, prep):
    tsum, win_p, wout_p, pw = prep
    f32 = jnp.float32
    n_tok, dm = xa.shape
    ctx_len = (n_tok - n_lat) // bsz
    q = S5_CHUNK
    n_chunks = (seq + ctx_len) // q
    width = q * S5_GROUP
    full = jnp.concatenate([xa[n_lat:].reshape(bsz, ctx_len, dm), xa[:n_lat].reshape(bsz, seq, dm)], axis=1)
    u = full.reshape(bsz, n_chunks, q, S5_GROUPS, S5_GROUP).transpose(3, 1, 0, 2, 4).reshape(S5_GROUPS, n_chunks * bsz, width)

    def tile(vec_rows):
        rows = vec_rows.shape[0]
        t = vec_rows.reshape(rows, S5_GROUPS, 1, S5_GROUP)
        t = jnp.broadcast_to(t, (rows, S5_GROUPS, q, S5_GROUP)).reshape(rows, S5_GROUPS, width)
        return jnp.tile(t.transpose(1, 0, 2), (1, 8 // rows, 1))

    scl, shl = tile(1.0 + mod_tab[:bsz, dm:2 * dm]), tile(mod_tab[:bsz, 0:dm])
    scc = tile(jnp.broadcast_to(1.0 + mod_tab[bsz:bsz + 1, dm:2 * dm], (bsz, dm)))
    shc = tile(jnp.broadcast_to(mod_tab[bsz:bsz + 1, 0:dm], (bsz, dm)))
    n_rows = n_chunks * bsz
    tile_spec = pl.BlockSpec((2, 8, width), lambda p: (p, 0, 0))
    y = pl.pallas_call(
        functools.partial(_s5_kernel, bsz=bsz, ctx_chunks=ctx_len // q),
        grid=(S5_GROUPS // 2,),
        in_specs=[
            pl.BlockSpec((2, n_rows, width), lambda p: (p, 0, 0)),
            tile_spec, tile_spec, tile_spec, tile_spec,
            pl.BlockSpec((1, 2, width, width), lambda p: (p, 0, 0, 0)),
            pl.BlockSpec((1, 2, 2, width, width), lambda p: (p, 0, 0, 0, 0)),
            pl.BlockSpec((1, 2, 2, width, width), lambda p: (p, 0, 0, 0, 0)),
            pl.BlockSpec((1, 2, 2 * S5_LEVELS, 2 * S5_STATE), lambda p: (p, 0, 0, 0)),
        ],
        out_specs=pl.BlockSpec((2, n_rows, width), lambda p: (p, 0, 0)),
        out_shape=jax.ShapeDtypeStruct((S5_GROUPS, n_rows, width), f32),
        compiler_params=pltpu.CompilerParams(dimension_semantics=("arbitrary",), vmem_limit_bytes=VMEM_LIMIT),
        name="s5_ssm",
    )(u, scl, shl, scc, shc, tsum, win_p, wout_p, pw)
    yf = y.reshape(S5_GROUPS, n_chunks, bsz, q, S5_GROUP).transpose(2, 1, 3, 0, 4).reshape(bsz, seq + ctx_len, dm)
    return jnp.concatenate([yf[:, ctx_len:].reshape(n_lat, dm), yf[:, :ctx_len].reshape(bsz * ctx_len, dm)], axis=0)


def _glu_ln_kernel(y_ref, x_ref, mod_ref, wv_ref, wg_ref, lng_ref, lnb_ref, o_ref):
    f32 = jnp.float32
    dm = x_ref.shape[1]
    g = jax.nn.gelu(y_ref[...]).astype(jnp.bfloat16)
    val = jnp.dot(g, wv_ref[...], preferred_element_type=f32)
    gate = jnp.dot(g, wg_ref[...], preferred_element_type=f32)
    out = val * jax.nn.sigmoid(gate)
    mod = mod_ref[0]
    z = DEEPNORM_ALPHA * x_ref[...] + mod[:, 2 * dm:3 * dm] * out
    mu = jnp.mean(z, axis=-1, keepdims=True)
    zc = z - mu
    var = jnp.mean(zc * zc, axis=-1, keepdims=True)
    o_ref[...] = zc * lax.rsqrt(var + LN_EPS) * lng_ref[...] + lnb_ref[...]


TOK_TM = 512


def glu_ln_pallas(ya, xa, mod3, n_lat, seq, bsz, w_val, w_gate, ln_g, ln_b):
    n_tok, dm = ya.shape
    tm = TOK_TM
    bf16 = jnp.bfloat16
    mod_idx = functools.partial(_mod_row_index, tm=tm, n_lat=n_lat, seq=seq, bsz=bsz)
    return pl.pallas_call(
        _glu_ln_kernel,
        grid=(n_tok // tm,),
        in_specs=[
            pl.BlockSpec((tm, dm), lambda i: (i, 0)),
            pl.BlockSpec((tm, dm), lambda i: (i, 0)),
            pl.BlockSpec((1, 1, 6 * dm), lambda i: (mod_idx(i), 0, 0)),
            pl.BlockSpec((dm, dm), lambda i: (0, 0)),
            pl.BlockSpec((dm, dm), lambda i: (0, 0)),
            pl.BlockSpec((1, dm), lambda i: (0, 0)),
            pl.BlockSpec((1, dm), lambda i: (0, 0)),
        ],
        out_specs=pl.BlockSpec((tm, dm), lambda i: (i, 0)),
        out_shape=jax.ShapeDtypeStruct((n_tok, dm), jnp.float32),
        compiler_params=pltpu.CompilerParams(dimension_semantics=("arbitrary",), vmem_limit_bytes=VMEM_LIMIT),
        name="glu_ln",
    )(ya, xa, mod3, w_val.astype(bf16), w_gate.astype(bf16), ln_g.reshape(1, dm), ln_b.reshape(1, dm))


MLA_HD = 128
MLA_TQ = 256
MLA_HB = 4
MLA_KSPLIT = 2


def _mla_prep(w_down, w_uq, w_uk, w_uv):
    bf16 = jnp.bfloat16
    quarter = MLA_ROPE // 4
    swap = np.concatenate([np.arange(quarter, 2 * quarter), np.arange(0, quarter),
                           np.arange(3 * quarter, 4 * quarter), np.arange(2 * quarter, 3 * quarter)])
    dm = w_down.shape[0]
    w_cq = w_down[:, :MLA_Q_RANK]
    w_ckv = w_down[:, MLA_Q_RANK:MLA_Q_RANK + MLA_KV_RANK]
    w_kr = w_down[:, MLA_Q_RANK + MLA_KV_RANK:]
    wd_t = jnp.concatenate([w_cq, w_ckv, jnp.zeros((dm, MLA_NOPE), w_down.dtype), w_kr, w_kr[:, swap]], axis=1).T
    rope = w_uq[:, :, MLA_NOPE:]
    wuq_t = jnp.concatenate([w_uq, rope[:, :, swap]], axis=2).reshape(MLA_Q_RANK, MLA_HEADS * MLA_HD).T
    wuk = jnp.concatenate([w_uk, jnp.zeros((MLA_KV_RANK, MLA_HEADS, MLA_HD - MLA_NOPE), w_uk.dtype)], axis=2)
    wuk = wuk.reshape(MLA_KV_RANK, MLA_HEADS * MLA_HD)
    wuv_t = w_uv.reshape(MLA_KV_RANK, MLA_HEADS * MLA_V).T
    return wd_t.astype(bf16), wuq_t.astype(bf16), wuk.astype(bf16), wuv_t.astype(bf16)


def _rope_tables(seq, tm):
    quarter = MLA_ROPE // 4
    freqs = ROPE_BASE ** (-jnp.arange(quarter, dtype=jnp.float32) / quarter)
    t = jnp.arange(seq, dtype=jnp.float32)
    row, col = jnp.floor(t / GRID_W), t - GRID_W * jnp.floor(t / GRID_W)
    ang_r, ang_c = freqs[:, None] * row[None, :], freqs[:, None] * col[None, :]
    cos32 = jnp.concatenate([jnp.cos(ang_r), jnp.cos(ang_r), jnp.cos(ang_c), jnp.cos(ang_c)], axis=0)
    sin32 = jnp.concatenate([-jnp.sin(ang_r), jnp.sin(ang_r), -jnp.sin(ang_c), jnp.sin(ang_c)], axis=0)
    ones = jnp.ones((MLA_NOPE, seq + tm), jnp.float32)
    zeros = jnp.zeros((MLA_HD - MLA_NOPE - MLA_ROPE, seq + tm), jnp.float32)
    cos_t = jnp.concatenate([ones, jnp.concatenate([cos32, jnp.ones((MLA_ROPE, tm))], axis=1), zeros], axis=0)
    sin_t = jnp.concatenate([0.0 * ones, jnp.concatenate([sin32, jnp.zeros((MLA_ROPE, tm))], axis=1), zeros], axis=0)
    return cos_t, sin_t


def _mla_proj_kernel(x_ref, mod_ref, cos_ref, sin_ref, wd_ref, wuq_ref, wuk_ref, wuv_ref, qn_ref, kvn_ref,
                     qt_ref, k_ref, vt_ref):
    f32, bf16 = jnp.float32, jnp.bfloat16
    dm = x_ref.shape[1]
    mod = mod_ref[0]
    h = x_ref[...] * (1.0 + mod[:, dm:2 * dm]) + mod[:, 0:dm]
    ht = h.T.astype(bf16)
    dt = jnp.dot(wd_ref[...], ht, preferred_element_type=f32)
    cq, ckv, kr = dt[:MLA_Q_RANK], dt[MLA_Q_RANK:MLA_Q_RANK + MLA_KV_RANK], dt[MLA_Q_RANK + MLA_KV_RANK:]
    cqn = cq * lax.rsqrt(jnp.mean(cq * cq, axis=0, keepdims=True) + RMS_EPS) * qn_ref[...]
    ckvn = ckv * lax.rsqrt(jnp.mean(ckv * ckv, axis=0, keepdims=True) + RMS_EPS) * kvn_ref[...]
    cos_t, sin_t = cos_ref[...], sin_ref[...]

    def rope(v):
        shifted = jnp.concatenate([v[MLA_ROPE:], v[:MLA_ROPE]], axis=0)
        return v * cos_t + shifted * sin_t

    scale = (MLA_NOPE + MLA_ROPE) ** -0.5
    q_all = jnp.dot(wuq_ref[...], cqn.astype(bf16), preferred_element_type=f32)
    for hd in range(MLA_HEADS):
        qt_ref[hd] = (rope(q_all[hd * MLA_HD:(hd + 1) * MLA_HD]) * scale).astype(bf16)
    ckvn_bf = ckvn.astype(bf16)
    vt_ref[...] = jnp.dot(wuv_ref[...], ckvn_bf, preferred_element_type=f32).astype(bf16)
    k_all = jnp.dot(ckvn.T.astype(bf16), wuk_ref[...], preferred_element_type=f32)
    kr_rows = rope(kr).T
    for hd in range(MLA_HEADS):
        k_ref[hd] = (k_all[:, hd * MLA_HD:(hd + 1) * MLA_HD] + kr_rows).astype(bf16)


def _mla_attn_kernel(*refs, with_latent):
    f32, bf16 = jnp.float32, jnp.bfloat16
    if with_latent:
        q_ref, kl_ref, kc_ref, vl_ref, vc_ref, o_ref = refs
    else:
        q_ref, kc_ref, vc_ref, o_ref = refs
    for i in range(q_ref.shape[0]):
        rows = slice(i * MLA_V, (i + 1) * MLA_V)
        q = q_ref[i]
        s_c = jnp.dot(kc_ref[i], q, preferred_element_type=f32)
        m = jnp.max(s_c, axis=0, keepdims=True)
        parts = []
        if with_latent:
            n_keys = kl_ref.shape[1]
            for k0 in range(0, n_keys, n_keys // MLA_KSPLIT):
                ks = slice(k0, k0 + n_keys // MLA_KSPLIT)
                s_l = jnp.dot(kl_ref[i, ks, :], q, preferred_element_type=f32)
                m = jnp.maximum(m, jnp.max(s_l, axis=0, keepdims=True))
                parts.append((ks, s_l))
        p_c = jnp.exp(s_c - m)
        den = jnp.sum(p_c, axis=0, keepdims=True)
        o = jnp.dot(vc_ref[rows, :], p_c.astype(bf16), preferred_element_type=f32)
        for ks, s_l in parts:
            p_l = jnp.exp(s_l - m)
            den = den + jnp.sum(p_l, axis=0, keepdims=True)
            o = o + jnp.dot(vl_ref[rows, ks], p_l.astype(bf16), preferred_element_type=f32)
        o_ref[rows, :] = (o / den).astype(bf16)


def _mla_out_kernel(ot_ref, x_ref, mod_ref, wo_ref, lng_ref, lnb_ref, o_ref):
    f32 = jnp.float32
    dm = x_ref.shape[1]
    attn = ot_ref[...].astype(f32).T.astype(jnp.bfloat16)
    out = jnp.dot(attn, wo_ref[...], preferred_element_type=f32)
    mod = mod_ref[0]
    z = DEEPNORM_ALPHA * x_ref[...] + mod[:, 2 * dm:3 * dm] * out
    mu = jnp.mean(z, axis=-1, keepdims=True)
    zc = z - mu
    var = jnp.mean(zc * zc, axis=-1, keepdims=True)
    o_ref[...] = zc * lax.rsqrt(var + LN_EPS) * lng_ref[...] + lnb_ref[...]


def mla_ln_pallas(xa, mod3, n_lat, seq, bsz, w_down, q_norm, kv_norm, w_uq, w_uk, w_uv, w_o, ln_g, ln_b):
    f32, bf16 = jnp.float32, jnp.bfloat16
    n_tok, dm = xa.shape
    n_ctx = n_tok - n_lat
    ctx_len = n_ctx // bsz
    tm = TOK_TM
    wd_t, wuq_t, wuk, wuv_t = _mla_prep(w_down, w_uq, w_uk, w_uv)
    cos_t, sin_t = _rope_tables(seq, tm)
    mod_idx = functools.partial(_mod_row_index, tm=tm, n_lat=n_lat, seq=seq, bsz=bsz)
    tiles_per_sample = seq // tm
    pos_idx = lambda i: jnp.where(i * tm < n_lat, i % tiles_per_sample, tiles_per_sample)
    n_hd, n_dn = MLA_HEADS * MLA_HD, wd_t.shape[0]
    whole = lambda shape: pl.BlockSpec(shape, lambda i: (0,) * len(shape))
    params = pltpu.CompilerParams(dimension_semantics=("arbitrary",), vmem_limit_bytes=VMEM_LIMIT)
    qt, k, vt = pl.pallas_call(
        _mla_proj_kernel,
        grid=(n_tok // tm,),
        in_specs=[
            pl.BlockSpec((tm, dm), lambda i: (i, 0)),
            pl.BlockSpec((1, 1, 6 * dm), lambda i: (mod_idx(i), 0, 0)),
            pl.BlockSpec((MLA_HD, tm), lambda i: (0, pos_idx(i))),
            pl.BlockSpec((MLA_HD, tm), lambda i: (0, pos_idx(i))),
            whole((n_dn, dm)), whole((n_hd, MLA_Q_RANK)), whole((MLA_KV_RANK, n_hd)),
            whole((MLA_HEADS * MLA_V, MLA_KV_RANK)), whole((MLA_Q_RANK, 1)), whole((MLA_KV_RANK, 1)),
        ],
        out_specs=[
            pl.BlockSpec((MLA_HEADS, MLA_HD, tm), lambda i: (0, 0, i)),
            pl.BlockSpec((MLA_HEADS, tm, MLA_HD), lambda i: (0, i, 0)),
            pl.BlockSpec((MLA_HEADS * MLA_V, tm), lambda i: (0, i)),
        ],
        out_shape=[
            jax.ShapeDtypeStruct((MLA_HEADS, MLA_HD, n_tok), bf16),
            jax.ShapeDtypeStruct((MLA_HEADS, n_tok, MLA_HD), bf16),
            jax.ShapeDtypeStruct((MLA_HEADS * MLA_V, n_tok), bf16),
        ],
        compiler_params=params,
        name="mla_proj",
    )(xa, mod3, cos_t, sin_t, wd_t, wuq_t, wuk, wuv_t, q_norm.reshape(-1, 1), kv_norm.reshape(-1, 1))

    tq, hb = MLA_TQ, MLA_HB
    n_qt = seq // tq
    ctx_blk = n_lat // ctx_len
    params3 = pltpu.CompilerParams(dimension_semantics=("arbitrary",) * 3, vmem_limit_bytes=VMEM_LIMIT)
    ot_lat = pl.pallas_call(
        functools.partial(_mla_attn_kernel, with_latent=True),
        grid=(bsz, MLA_HEADS // hb, n_qt),
        in_specs=[
            pl.BlockSpec((hb, MLA_HD, tq), lambda b, hd, t: (hd, 0, b * n_qt + t)),
            pl.BlockSpec((hb, seq, MLA_HD), lambda b, hd, t: (hd, b, 0)),
            pl.BlockSpec((hb, ctx_len, MLA_HD), lambda b, hd, t: (hd, ctx_blk + b, 0)),
            pl.BlockSpec((hb * MLA_V, seq), lambda b, hd, t: (hd, b)),
            pl.BlockSpec((hb * MLA_V, ctx_len), lambda b, hd, t: (hd, ctx_blk + b)),
        ],
        out_specs=pl.BlockSpec((hb * MLA_V, tq), lambda b, hd, t: (hd, b * n_qt + t)),
        out_shape=jax.ShapeDtypeStruct((MLA_HEADS * MLA_V, n_lat), bf16),
        compiler_params=params3,
        name="mla_attn_latent",
    )(qt, k, k, vt, vt)
    ot_ctx = pl.pallas_call(
        functools.partial(_mla_attn_kernel, with_latent=False),
        grid=(bsz, MLA_HEADS // hb, 1),
        in_specs=[
            pl.BlockSpec((hb, MLA_HD, ctx_len), lambda b, hd, t: (hd, 0, ctx_blk + b)),
            pl.BlockSpec((hb, ctx_len, MLA_HD), lambda b, hd, t: (hd, ctx_blk + b, 0)),
            pl.BlockSpec((hb * MLA_V, ctx_len), lambda b, hd, t: (hd, ctx_blk + b)),
        ],
        out_specs=pl.BlockSpec((hb * MLA_V, ctx_len), lambda b, hd, t: (hd, b)),
        out_shape=jax.ShapeDtypeStruct((MLA_HEADS * MLA_V, n_ctx), bf16),
        compiler_params=params3,
        name="mla_attn_context",
    )(qt, k, vt)
    ot = jnp.concatenate([ot_lat, ot_ctx], axis=1)
    return pl.pallas_call(
        _mla_out_kernel,
        grid=(n_tok // tm,),
        in_specs=[
            pl.BlockSpec((MLA_HEADS * MLA_V, tm), lambda i: (0, i)),
            pl.BlockSpec((tm, dm), lambda i: (i, 0)),
            pl.BlockSpec((1, 1, 6 * dm), lambda i: (mod_idx(i), 0, 0)),
            whole((MLA_HEADS * MLA_V, dm)), whole((1, dm)), whole((1, dm)),
        ],
        out_specs=pl.BlockSpec((tm, dm), lambda i: (i, 0)),
        out_shape=jax.ShapeDtypeStruct((n_tok, dm), f32),
        compiler_params=params,
        name="mla_out_ln",
    )(ot, xa, mod3, w_o.astype(bf16), ln_g.reshape(1, dm), ln_b.reshape(1, dm))


SSD_DT_PAD = 128
SSD_HPG = SSD_HEADS // SSD_GROUPS
SSD_GW = SSD_HPG * SSD_HEADDIM


def _ssd_in_kernel(x_ref, mod_ref, w_ref, wdt_ref, z_ref, xbc_ref, dt_ref, dtt_ref):
    f32, bf16 = jnp.float32, jnp.bfloat16
    dm = x_ref.shape[1]
    mod = mod_ref[0]
    h = x_ref[...] * (1.0 + mod[:, dm:2 * dm]) + mod[:, 0:dm]
    proj = jnp.dot(h.astype(bf16), w_ref[...], preferred_element_type=f32)
    z_ref[...] = proj[:, :SSD_D_INNER]
    xbc_ref[...] = proj[:, SSD_D_INNER:SSD_D_INNER + SSD_CONV_DIM]
    dt_ref[...] = proj[:, SSD_D_INNER + SSD_CONV_DIM:]
    dtt_ref[...] = jnp.dot(wdt_ref[...], h.T.astype(bf16), preferred_element_type=f32)


def _softplus(v):
    return jnp.maximum(v, 0.0) + jnp.log(1.0 + jnp.exp(-jnp.abs(v)))


def _split3(v):
    f32, bf16 = jnp.float32, jnp.bfloat16
    hi = v.astype(bf16)
    r1 = v - hi.astype(f32)
    mid = r1.astype(bf16)
    lo = (r1 - mid.astype(f32)).astype(bf16)
    return jnp.concatenate([hi, mid, lo], axis=1)


def _ssd_direction(direction, xbc_ref, prev_ref, next_ref, dt_ref, dtt_ref, has_prev, has_next,
                   cw_ref, cb_ref, bias_r_ref, a_r_ref, bias_c_ref, a_c_ref, e_ref, dskip_ref, state_ref, y_ref):
    f32, bf16 = jnp.float32, jnp.bfloat16
    hp = lax.Precision.HIGHEST
    q = xbc_ref.shape[0]
    nh = SSD_HEADS
    xm = xbc_ref[...]
    row = lax.broadcasted_iota(jnp.int32, xm.shape, 0)
    before = jnp.where(has_prev, prev_ref[7:8, :], 0.0)
    after = jnp.where(has_next, next_ref[0:1, :], 0.0)
    x_dn = jnp.where(row == 0, before, pltpu.roll(xm, 1, axis=0))
    x_up = jnp.where(row == q - 1, after, pltpu.roll(xm, q - 1, axis=0))
    conv = cb_ref[...] + x_dn * cw_ref[0:1, :] + xm * cw_ref[1:2, :] + x_up * cw_ref[2:3, :]
    conv = conv * jax.nn.sigmoid(conv)
    xs = conv[:, :SSD_D_INNER]
    gn = SSD_GROUPS * SSD_STATE
    bm, cm = conv[:, SSD_D_INNER:SSD_D_INNER + gn], conv[:, SSD_D_INNER + gn:]
    hs = slice(direction * nh, (direction + 1) * nh)
    dt = _softplus(dt_ref[:, hs] + bias_r_ref[direction:direction + 1, :])
    dtt = _softplus(dtt_ref[hs, :] + bias_c_ref[direction])
    r_i = lax.broadcasted_iota(jnp.int32, (q, q), 0)
    c_i = lax.broadcasted_iota(jnp.int32, (q, q), 1)
    causal = (r_i >= c_i) if direction == 0 else (r_i <= c_i)
    tri = causal.astype(f32)
    tri_t = ((c_i >= r_i) if direction == 0 else (c_i <= r_i)).astype(f32)
    a_cum = jnp.dot(tri, dt * a_r_ref[direction:direction + 1, :], precision=hp, preferred_element_type=f32)
    a_cum_t = jnp.dot(dtt * a_c_ref[direction], tri_t, precision=hp, preferred_element_type=f32)
    last = q - 1 if direction == 0 else 0
    a_exp = jnp.dot(_split3(a_cum), e_ref[...], preferred_element_type=f32)
    dt_exp = jnp.dot(_split3(dt), e_ref[...], preferred_element_type=f32)
    xdt = xs * dt_exp
    a_tot = a_exp[last:last + 1, :]
    xw = (xdt * jnp.exp(a_tot - a_exp)).astype(bf16)
    grow = jnp.exp(a_exp)
    carry = jnp.exp(a_tot)
    xdt_bf = xdt.astype(bf16)
    for g in range(SSD_GROUPS):
        gl = slice(g * SSD_GW, (g + 1) * SSD_GW)
        nl = slice(g * SSD_STATE, (g + 1) * SSD_STATE)
        bm_g, cm_g = bm[:, nl], cm[:, nl].astype(bf16)
        prev = state_ref[direction, g]
        y_g = jnp.dot(cm_g, prev.astype(bf16), preferred_element_type=f32) * grow[:, gl]
        states = jnp.dot(bm_g.T.astype(bf16), xw[:, gl], preferred_element_type=f32)
        state_ref[direction, g] = prev * carry[:, gl] + states
        cb = lax.dot_general(cm_g, bm_g.astype(bf16), (((1,), (1,)), ((), ())), preferred_element_type=f32)
        parts = []
        for hh in range(SSD_HPG):
            hd = g * SSD_HPG + hh
            seg = jnp.broadcast_to(a_cum[:, hd:hd + 1], (q, q)) - a_cum_t[hd:hd + 1, :]
            lmat = (jnp.where(causal, jnp.exp(seg), 0.0) * cb).astype(bf16)
            parts.append(jnp.dot(lmat, xdt_bf[:, hd * SSD_HEADDIM:(hd + 1) * SSD_HEADDIM], preferred_element_type=f32))
        y_g = y_g + jnp.concatenate(parts, axis=1)
        if direction == 0:
            y_g = y_g + dskip_ref[:, gl] * xs[:, gl]
        y_ref[:, gl] = y_g


def _ssd_scan_kernel(xf_ref, pf_ref, nf_ref, dtf_ref, dttf_ref, xb_ref, pb_ref, nb_ref, dtb_ref, dttb_ref,
                     cw_ref, cb_ref, bias_r_ref, a_r_ref, bias_c_ref, a_c_ref, e_ref, dskip_ref,
                     yf_ref, yb_ref, state_ref, *, ctx_chunks, lat_chunks):
    j = pl.program_id(1)

    @pl.when(j == 0)
    def _reset():
        state_ref[...] = jnp.zeros(state_ref.shape, jnp.float32)

    in_ctx = j < ctx_chunks
    pos_f = jnp.where(in_ctx, j, j - ctx_chunks)
    seg_len = jnp.where(in_ctx, ctx_chunks, lat_chunks)
    pos_b = seg_len - 1 - pos_f
    shared = (cw_ref, cb_ref, bias_r_ref, a_r_ref, bias_c_ref, a_c_ref, e_ref, dskip_ref, state_ref)
    _ssd_direction(0, xf_ref, pf_ref, nf_ref, dtf_ref, dttf_ref, pos_f > 0, pos_f < seg_len - 1, *shared, yf_ref)
    _ssd_direction(1, xb_ref, pb_ref, nb_ref, dtb_ref, dttb_ref, pos_b > 0, pos_b < seg_len - 1, *shared, yb_ref)


def _ssd_out_kernel(yf_ref, yb_ref, z_ref, x_ref, mod_ref, nw_ref, wo_ref, lng_ref, lnb_ref, o_ref):
    f32 = jnp.float32
    dm = x_ref.shape[1]
    z = z_ref[...]
    y = (yf_ref[...] + yb_ref[...]) * (z * jax.nn.sigmoid(z))
    y = y * lax.rsqrt(jnp.mean(y * y, axis=-1, keepdims=True) + RMS_EPS) * nw_ref[...]
    out = jnp.dot(y.astype(jnp.bfloat16), wo_ref[...], preferred_element_type=f32)
    mod = mod_ref[0]
    v = DEEPNORM_ALPHA * x_ref[...] + mod[:, 2 * dm:3 * dm] * out
    mu = jnp.mean(v, axis=-1, keepdims=True)
    vc = v - mu
    var = jnp.mean(vc * vc, axis=-1, keepdims=True)
    o_ref[...] = vc * lax.rsqrt(var + LN_EPS) * lng_ref[...] + lnb_ref[...]


def ssd_ln_pallas(xa, mod3, n_lat, seq, bsz, w_in, conv_w, conv_b, dt_bias, a_log, d, norm_w, w_out, ln_g, ln_b):
    f32, bf16 = jnp.float32, jnp.bfloat16
    n_tok, dm = xa.shape
    ctx_len = (n_tok - n_lat) // bsz
    tm = TOK_TM
    q = SSD_CHUNK
    nh = SSD_HEADS
    pad = SSD_DT_PAD - 2 * nh
    w_ext = jnp.concatenate([w_in, jnp.zeros((dm, pad), w_in.dtype)], axis=1).astype(bf16)
    n_in = w_ext.shape[1]
    wdt_t = w_ext[:, SSD_D_INNER + SSD_CONV_DIM:].T
    mod_idx = functools.partial(_mod_row_index, tm=tm, n_lat=n_lat, seq=seq, bsz=bsz)
    whole = lambda shape: pl.BlockSpec(shape, lambda *_: (0,) * len(shape))
    params = pltpu.CompilerParams(dimension_semantics=("arbitrary",), vmem_limit_bytes=VMEM_LIMIT)
    z, xbc, dt, dtt = pl.pallas_call(
        _ssd_in_kernel,
        grid=(n_tok // tm,),
        in_specs=[
            pl.BlockSpec((tm, dm), lambda i: (i, 0)),
            pl.BlockSpec((1, 1, 6 * dm), lambda i: (mod_idx(i), 0, 0)),
            whole((dm, n_in)), whole((SSD_DT_PAD, dm)),
        ],
        out_specs=[
            pl.BlockSpec((tm, SSD_D_INNER), lambda i: (i, 0)),
            pl.BlockSpec((tm, SSD_CONV_DIM), lambda i: (i, 0)),
            pl.BlockSpec((tm, SSD_DT_PAD), lambda i: (i, 0)),
            pl.BlockSpec((SSD_DT_PAD, tm), lambda i: (0, i)),
        ],
        out_shape=[
            jax.ShapeDtypeStruct((n_tok, SSD_D_INNER), f32),
            jax.ShapeDtypeStruct((n_tok, SSD_CONV_DIM), f32),
            jax.ShapeDtypeStruct((n_tok, SSD_DT_PAD), f32),
            jax.ShapeDtypeStruct((SSD_DT_PAD, n_tok), f32),
        ],
        compiler_params=params,
        name="ssd_in_proj",
    )(xa, mod3, w_ext, wdt_t)

    ctx_chunks, lat_chunks = ctx_len // q, seq // q
    ctx_base = n_lat // q
    n_chunk_total = n_tok // q

    def chunk_of(b, j, backward):
        in_ctx = j < ctx_chunks
        pos = jnp.where(in_ctx, j, j - ctx_chunks)
        seg = jnp.where(in_ctx, ctx_chunks, lat_chunks)
        pos = jnp.where(backward, seg - 1 - pos, pos)
        return jnp.where(in_ctx, ctx_base + b * ctx_chunks + pos, b * lat_chunks + pos)

    sub = q // 8
    def dir_specs(backward):
        ch = lambda b, j: chunk_of(b, j, backward)
        return [
            pl.BlockSpec((q, SSD_CONV_DIM), lambda b, j: (ch(b, j), 0)),
            pl.BlockSpec((8, SSD_CONV_DIM), lambda b, j: (jnp.maximum(ch(b, j) * sub - 1, 0), 0)),
            pl.BlockSpec((8, SSD_CONV_DIM), lambda b, j: (jnp.minimum((ch(b, j) + 1) * sub, n_chunk_total * sub - 1), 0)),
            pl.BlockSpec((q, SSD_DT_PAD), lambda b, j: (ch(b, j), 0)),
            pl.BlockSpec((SSD_DT_PAD, q), lambda b, j: (0, ch(b, j))),
        ]

    a = -jnp.exp(a_log.astype(f32))
    expand = jnp.repeat(jnp.eye(nh, dtype=f32), SSD_HEADDIM, axis=1)
    e3 = jnp.concatenate([expand, expand, expand], axis=0).astype(bf16)
    dskip = jnp.repeat(d, SSD_HEADDIM)[None, :]
    yf, yb = pl.pallas_call(
        functools.partial(_ssd_scan_kernel, ctx_chunks=ctx_chunks, lat_chunks=lat_chunks),
        grid=(bsz, ctx_chunks + lat_chunks),
        in_specs=dir_specs(False) + dir_specs(True) + [
            whole((SSD_CONV, SSD_CONV_DIM)), whole((1, SSD_CONV_DIM)),
            whole((2, nh)), whole((2, nh)), whole((2, nh, 1)), whole((2, nh, 1)),
            whole((3 * nh, SSD_D_INNER)), whole((1, SSD_D_INNER)),
        ],
        out_specs=[
            pl.BlockSpec((q, SSD_D_INNER), lambda b, j: (chunk_of(b, j, False), 0)),
            pl.BlockSpec((q, SSD_D_INNER), lambda b, j: (chunk_of(b, j, True), 0)),
        ],
        out_shape=[jax.ShapeDtypeStruct((n_tok, SSD_D_INNER), f32)] * 2,
        scratch_shapes=[pltpu.VMEM((2, SSD_GROUPS, SSD_STATE, SSD_GW), f32)],
        compiler_params=pltpu.CompilerParams(dimension_semantics=("arbitrary", "arbitrary"), vmem_limit_bytes=VMEM_LIMIT),
        name="ssd_scan",
    )(xbc, xbc, xbc, dt, dtt, xbc, xbc, xbc, dt, dtt,
      conv_w, conv_b.reshape(1, -1), dt_bias, a, dt_bias.reshape(2, nh, 1), a.reshape(2, nh, 1), e3, dskip)

    return pl.pallas_call(
        _ssd_out_kernel,
        grid=(n_tok // tm,),
        in_specs=[
            pl.BlockSpec((tm, SSD_D_INNER), lambda i: (i, 0)),
            pl.BlockSpec((tm, SSD_D_INNER), lambda i: (i, 0)),
            pl.BlockSpec((tm, SSD_D_INNER), lambda i: (i, 0)),
            pl.BlockSpec((tm, dm), lambda i: (i, 0)),
            pl.BlockSpec((1, 1, 6 * dm), lambda i: (mod_idx(i), 0, 0)),
            whole((1, SSD_D_INNER)), whole((SSD_D_INNER, dm)), whole((1, dm)), whole((1, dm)),
        ],
        out_specs=pl.BlockSpec((tm, dm), lambda i: (i, 0)),
        out_shape=jax.ShapeDtypeStruct((n_tok, dm), f32),
        compiler_params=params,
        name="ssd_out_ln",
    )(yf, yb, z, xa, mod3, norm_w.reshape(1, -1), w_out.astype(bf16), ln_g.reshape(1, dm), ln_b.reshape(1, dm))


def _modulation_kernel(c_ref, w_ref, b_ref, o_ref):
    cv = c_ref[...]
    act = cv * jax.nn.sigmoid(cv)
    o_ref[0] = jnp.dot(act, w_ref[0], precision=lax.Precision.HIGHEST,
                       preferred_element_type=jnp.float32) + b_ref[0]


def modulation_pallas(cond, mod_w, mod_b):
    depth, dm, n_out = mod_w.shape
    tn = dm
    return pl.pallas_call(
        _modulation_kernel,
        grid=(depth, n_out // tn),
        in_specs=[
            pl.BlockSpec((8, dm), lambda i, n: (0, 0)),
            pl.BlockSpec((1, dm, tn), lambda i, n: (i, 0, n)),
            pl.BlockSpec((1, 1, tn), lambda i, n: (i, 0, n)),
        ],
        out_specs=pl.BlockSpec((1, 8, tn), lambda i, n: (i, 0, n)),
        out_shape=jax.ShapeDtypeStruct((depth, 8, n_out), jnp.float32),
        compiler_params=pltpu.CompilerParams(dimension_semantics=("arbitrary", "arbitrary"), vmem_limit_bytes=VMEM_LIMIT),
        name="modulation",
    )(cond, mod_w, mod_b.reshape(depth, 1, n_out))


def kernel(x, c, ctx, c_ctx, mod_w, mod_b, ln_g, ln_b,
           s5_a_re, s5_a_im, s5_log_dt, s5_b_re, s5_b_im, s5_c_re, s5_c_im, s5_d, s5_w_gate, s5_w_val,
           ssd_w_in, ssd_conv_w, ssd_conv_b, ssd_dt_bias, ssd_a_log, ssd_d, ssd_norm_w, ssd_w_out,
           mla_w_down, mla_q_norm, mla_kv_norm, mla_w_uq, mla_w_uk, mla_w_uv, mla_w_o,
           peer_w_q, peer_subkeys, peer_u, peer_v):
    ROWS = x.shape[1] // GRID_W
    pos = grid_positions(ROWS)
    ctx_len = ctx.shape[1]
    bsz, seq_len, dm = x.shape
    n_lat, n_ctx = bsz * seq_len, bsz * ctx_len
    xa = jnp.concatenate([x.reshape(n_lat, dm), ctx.reshape(n_ctx, dm)], axis=0)
    cond = jnp.concatenate([c, c_ctx[None], jnp.zeros((7 - bsz, dm), x.dtype)], axis=0)
    mod_all = modulation_pallas(cond, mod_w, mod_b)
    for i in range(DEPTH):
        last = i == DEPTH - 1
        mod_tab = mod_all[i]
        mod3 = mod_tab.reshape(8, 1, 6 * dm)
        kind, j = i % N_MIXERS, i // N_MIXERS
        if kind == 0:
            prep = _s5_prep_fast(s5_a_re[j], s5_a_im[j], s5_log_dt[j], s5_b_re[j], s5_b_im[j], s5_c_re[j], s5_c_im[j], s5_d[j])
            ya = s5_ssm_pallas(xa, mod_tab, n_lat, seq_len, bsz, prep)
            if last:
                ya, xa = ya[:n_lat], xa[:n_lat]
            xa = glu_ln_pallas(ya, xa, mod3, n_lat, seq_len, bsz, s5_w_val[j], s5_w_gate[j], ln_g[i, 0], ln_b[i, 0])
        elif kind == 1:
            xa = ssd_ln_pallas(xa, mod3, n_lat, seq_len, bsz, ssd_w_in[j], ssd_conv_w[j], ssd_conv_b[j], ssd_dt_bias[j],
                               ssd_a_log[j], ssd_d[j], ssd_norm_w[j], ssd_w_out[j], ln_g[i, 0], ln_b[i, 0])
            if last:
                xa = xa[:n_lat]
        else:
            xa = mla_ln_pallas(xa, mod3, n_lat, seq_len, bsz, mla_w_down[j], mla_q_norm[j], mla_kv_norm[j],
                               mla_w_uq[j], mla_w_uk[j], mla_w_uv[j], mla_w_o[j], ln_g[i, 0], ln_b[i, 0])
            if last:
                xa = xa[:n_lat]
        xa = peer_ln_pallas(xa, mod3, n_lat, seq_len, bsz, peer_w_q[i], peer_subkeys[i], peer_u[i], peer_v[i],
                            ln_g[i, 1], ln_b[i, 1])
    return xa[:n_lat].reshape(bsz, seq_len, dm)
```

```python
import math
import functools
import jax
import jax.numpy as jnp
from jax import lax
import numpy as np
from jax.experimental import pallas as pl
from jax.experimental.pallas import tpu as pltpu

D_MODEL = 1024
BATCH = 4
SEQ = 4096
DEPTH = 4

GRID_W = 64
CTX_LEN = 256
N_MIXERS = 3
DEEPNORM_ALPHA = (2.0 * DEPTH) ** 0.25
LN_EPS = 1e-5
RMS_EPS = 1e-6
ROPE_BASE = 10000.0

S5_GROUP = 16
S5_GROUPS = D_MODEL // S5_GROUP
S5_STATE = 64

SSD_D_INNER = 2 * D_MODEL
SSD_HEADDIM = 64
SSD_HEADS = SSD_D_INNER // SSD_HEADDIM
SSD_GROUPS = 4
SSD_STATE = 128
SSD_CONV = 3
SSD_CHUNK = 128
SSD_CONV_DIM = SSD_D_INNER + 2 * SSD_GROUPS * SSD_STATE
SSD_IN_DIM = SSD_D_INNER + SSD_CONV_DIM + 2 * SSD_HEADS

MLA_HEADS = 16
MLA_Q_RANK = 256
MLA_KV_RANK = 128
MLA_NOPE = 64
MLA_ROPE = 32
MLA_V = 64
MLA_BLOCK = 128

PEER_HEADS = 8
PEER_KEYS = 128
PEER_EXPERTS = PEER_KEYS * PEER_KEYS
PEER_QDIM = 256
PEER_TOPK = 16
PEER_BLOCK = 128


def layer_norm(x, g, b):
    xf = x.astype(jnp.float32)
    mu = jnp.mean(xf, axis=-1, keepdims=True)
    var = jnp.mean(jnp.square(xf - mu), axis=-1, keepdims=True)
    return ((xf - mu) * lax.rsqrt(var + LN_EPS)).astype(x.dtype) * g + b


def _ln_kernel(x_ref, g_ref, b_ref, o_ref):
    xf = x_ref[...]
    mu = jnp.mean(xf, axis=-1, keepdims=True)
    xc = xf - mu
    var = jnp.mean(xc * xc, axis=-1, keepdims=True)
    o_ref[...] = xc * lax.rsqrt(var + LN_EPS) * g_ref[...] + b_ref[...]


def layer_norm_pallas(x, g, b):
    shp = x.shape
    x2 = x.reshape(-1, shp[-1])
    n, d = x2.shape
    tb = 512
    out = pl.pallas_call(
        _ln_kernel,
        grid=(n // tb,),
        in_specs=[pl.BlockSpec((tb, d), lambda i: (i, 0)),
                  pl.BlockSpec((1, d), lambda i: (0, 0)),
                  pl.BlockSpec((1, d), lambda i: (0, 0))],
        out_specs=pl.BlockSpec((tb, d), lambda i: (i, 0)),
        out_shape=jax.ShapeDtypeStruct((n, d), jnp.float32),
        name="final_ln",
    )(x2, g.reshape(1, d), b.reshape(1, d))
    return out.reshape(shp)


def rms_norm(x, g):
    xf = x.astype(jnp.float32)
    return (xf * lax.rsqrt(jnp.mean(jnp.square(xf), axis=-1, keepdims=True) + RMS_EPS)).astype(x.dtype) * g


def modulate(x, shift, scale):
    return x * (1.0 + scale) + shift


def grid_positions(rows):
    row = jnp.repeat(jnp.arange(rows, dtype=jnp.float32), GRID_W)
    col = jnp.tile(jnp.arange(GRID_W, dtype=jnp.float32), rows)
    return row, col


def rope_axial(x, row, col):
    half = x.shape[-1] // 2
    quarter = half // 2
    freqs = ROPE_BASE ** (-jnp.arange(quarter, dtype=jnp.float32) / quarter)

    def rot(xp, pos):
        ang = pos[:, None] * freqs
        cos = jnp.cos(ang)[None, :, None, :].astype(x.dtype)
        sin = jnp.sin(ang)[None, :, None, :].astype(x.dtype)
        x1, x2 = xp[..., :quarter], xp[..., quarter:]
        return jnp.concatenate([x1 * cos - x2 * sin, x2 * cos + x1 * sin], axis=-1)

    return jnp.concatenate([rot(x[..., :half], row), rot(x[..., half:], col)], axis=-1)


def _lin_rec(left, right):
    a1, b1 = left
    a2, b2 = right
    return a1 * a2, a2 * b1 + b2


def s5_scan(u, a_bar, b_bar, c_mat, init, reverse):
    bu = jnp.einsum('lgc,gpc->lgp', u.astype(jnp.float32), b_bar)
    a = jnp.broadcast_to(a_bar, bu.shape)
    a_cum, s = lax.associative_scan(_lin_rec, (a, bu), reverse=reverse, axis=0)
    s = s + a_cum * init
    y = jnp.einsum('lgp,gcp->lgc', s, c_mat).real
    final = s[0] if reverse else s[-1]
    return y, final


def s5_direction(u_ctx, u_lat, a_re, a_im, log_dt, b_re, b_im, c_re, c_im, reverse):
    lam = lax.complex(a_re.astype(jnp.float32), a_im.astype(jnp.float32))
    a_bar = jnp.exp(lam * jnp.exp(log_dt.astype(jnp.float32))[:, None])
    b_mat = lax.complex(b_re.astype(jnp.float32), b_im.astype(jnp.float32))
    b_bar = ((a_bar - 1.0) / lam)[..., None] * b_mat
    c_mat = lax.complex(c_re.astype(jnp.float32), c_im.astype(jnp.float32))

    def per_sample(args):
        uc, ul = args
        yc, sc = s5_scan(uc, a_bar, b_bar, c_mat, jnp.zeros_like(a_bar), reverse)
        yl, _ = s5_scan(ul, a_bar, b_bar, c_mat, sc, reverse)
        return yc, yl

    return lax.map(per_sample, (u_ctx, u_lat))


def s5_mixer(h, hc, a_re, a_im, log_dt, b_re, b_im, c_re, c_im, d, w_gate, w_val):
    bsz, seq_len, _ = h.shape
    ctx_len = hc.shape[1]
    u = h.reshape(bsz, seq_len, S5_GROUPS, S5_GROUP)
    uc = hc.reshape(bsz, ctx_len, S5_GROUPS, S5_GROUP)
    y_l = d * h
    y_c = d * hc
    for direction in range(2):
        yc, yl = s5_direction(uc, u, a_re[direction], a_im[direction], log_dt[direction],
                              b_re[direction], b_im[direction], c_re[direction], c_im[direction],
                              direction == 1)
        y_l = y_l + yl.reshape(bsz, seq_len, D_MODEL).astype(h.dtype)
        y_c = y_c + yc.reshape(bsz, ctx_len, D_MODEL).astype(h.dtype)

    def glu(y):
        g = jax.nn.gelu(y)
        return (g @ w_val) * jax.nn.sigmoid(g @ w_gate)

    return glu(y_l), glu(y_c)


def depthwise_conv_centred(x, w, b):
    k_w = w.shape[0]
    pad = k_w // 2
    seq_len = x.shape[1]
    xp = jnp.pad(x, ((0, 0), (pad, pad), (0, 0)))
    out = b
    for k in range(k_w):
        out = out + xp[:, k:k + seq_len] * w[k]
    return out


def ssd_scan(x, dt, a, bm, cm, init):
    bsz, seq_len, n_heads, p_dim = x.shape
    n_grp, n_st = bm.shape[2], bm.shape[3]
    hg = n_heads // n_grp
    q_len = SSD_CHUNK
    nc = seq_len // q_len
    xc = x.reshape(bsz, nc, q_len, n_grp, hg, p_dim)
    dtc = dt.reshape(bsz, nc, q_len, n_grp, hg).astype(jnp.float32)
    bc = bm.reshape(bsz, nc, q_len, n_grp, n_st)
    cc = cm.reshape(bsz, nc, q_len, n_grp, n_st)
    a_cum = jnp.cumsum(dtc * a.reshape(n_grp, hg), axis=2)
    xdt = xc * dtc[..., None].astype(x.dtype)
    tri = jnp.tril(jnp.ones((q_len, q_len), dtype=bool))
    seg = a_cum[:, :, :, None] - a_cum[:, :, None]
    decay = jnp.exp(jnp.where(tri[:, :, None, None], seg, -jnp.inf)).astype(x.dtype)
    cb = jnp.einsum('bcqgn,bcsgn->bcgqs', cc, bc)
    y_diag = jnp.einsum('bcgqs,bcqsgh,bcsghp->bcqghp', cb, decay, xdt)
    decay_states = jnp.exp(a_cum[:, :, -1:] - a_cum).astype(x.dtype)
    states = jnp.einsum('bcsgn,bcsgh,bcsghp->bcghpn', bc, decay_states, xdt)
    chunk_decay = jnp.exp(a_cum[:, :, -1]).astype(x.dtype)

    def step(carry, inp):
        dec, st = inp
        return carry * dec[..., None, None] + st, carry

    final, prev = lax.scan(step, init, (jnp.moveaxis(chunk_decay, 1, 0), jnp.moveaxis(states, 1, 0)))
    y_off = jnp.einsum('bcqgn,cbghpn,bcqgh->bcqghp', cc, prev, jnp.exp(a_cum).astype(x.dtype))
    return (y_diag + y_off).reshape(bsz, seq_len, n_heads, p_dim), final


def ssd_mixer(h, hc, w_in, conv_w, conv_b, dt_bias, a_log, d, norm_w, w_out):
    a = -jnp.exp(a_log.astype(jnp.float32))

    def flip(t):
        return jnp.flip(t, axis=1)

    def bidir(t, inits):
        bsz, seq_len = t.shape[0], t.shape[1]
        z, xbc, dt = jnp.split(t @ w_in, [SSD_D_INNER, SSD_D_INNER + SSD_CONV_DIM], axis=-1)
        xbc = jax.nn.silu(depthwise_conv_centred(xbc, conv_w, conv_b))
        xs, bm, cm = jnp.split(xbc, [SSD_D_INNER, SSD_D_INNER + SSD_GROUPS * SSD_STATE], axis=-1)
        xs = xs.reshape(bsz, seq_len, SSD_HEADS, SSD_HEADDIM)
        bm = bm.reshape(bsz, seq_len, SSD_GROUPS, SSD_STATE)
        cm = cm.reshape(bsz, seq_len, SSD_GROUPS, SSD_STATE)
        dt = jax.nn.softplus(dt.reshape(bsz, seq_len, 2, SSD_HEADS) + dt_bias)
        y_f, s_f = ssd_scan(xs, dt[:, :, 0], a[0], bm, cm, inits[0])
        y_b, s_b = ssd_scan(flip(xs), flip(dt[:, :, 1]), a[1], flip(bm), flip(cm), inits[1])
        y = y_f + flip(y_b) + d[:, None] * xs
        y = rms_norm(y.reshape(z.shape) * jax.nn.silu(z), norm_w)
        return y @ w_out, (s_f, s_b)

    zero = jnp.zeros((h.shape[0], SSD_GROUPS, SSD_HEADS // SSD_GROUPS, SSD_HEADDIM, SSD_STATE), h.dtype)
    out_c, states_c = bidir(hc, (zero, zero))
    out_l, _ = bidir(h, states_c)
    return out_l, out_c


def mla_project(t, w_down, q_norm, kv_norm, w_uq, w_uk, w_uv, pos):
    cq, ckv, kr = jnp.split(t @ w_down, [MLA_Q_RANK, MLA_Q_RANK + MLA_KV_RANK], axis=-1)
    q = jnp.einsum('btr,rhd->bthd', rms_norm(cq, q_norm), w_uq)
    ckv = rms_norm(ckv, kv_norm)
    k_nope = jnp.einsum('btr,rhd->bthd', ckv, w_uk)
    v = jnp.einsum('btr,rhd->bthd', ckv, w_uv)
    q_nope, q_rope = q[..., :MLA_NOPE], q[..., MLA_NOPE:]
    kr = kr[:, :, None, :]
    if pos is not None:
        q_rope = rope_axial(q_rope, pos[0], pos[1])
        kr = rope_axial(kr, pos[0], pos[1])
    k = jnp.concatenate([k_nope, jnp.broadcast_to(kr, k_nope.shape[:3] + (MLA_ROPE,))], axis=-1)
    q = jnp.concatenate([q_nope, q_rope], axis=-1)
    return q, k, v


def attend(q, k, v):
    s = jnp.einsum('bqhd,bkhd->bhqk', q, k).astype(jnp.float32) * (MLA_NOPE + MLA_ROPE) ** -0.5
    p = jax.nn.softmax(s, axis=-1).astype(v.dtype)
    return jnp.einsum('bhqk,bkhd->bqhd', p, v)


def mla_mixer(h, hc, pos, w_down, q_norm, kv_norm, w_uq, w_uk, w_uv, w_o):
    bsz, seq_len, _ = h.shape
    qc, kc, vc = mla_project(hc, w_down, q_norm, kv_norm, w_uq, w_uk, w_uv, None)
    ql, kl, vl = mla_project(h, w_down, q_norm, kv_norm, w_uq, w_uk, w_uv, pos)
    out_c = attend(qc, kc, vc).reshape(bsz, hc.shape[1], MLA_HEADS * MLA_V)
    k_all = jnp.concatenate([kc, kl], axis=1)
    v_all = jnp.concatenate([vc, vl], axis=1)
    nb = seq_len // MLA_BLOCK
    qb = jnp.moveaxis(ql.reshape(bsz, nb, MLA_BLOCK, MLA_HEADS, MLA_NOPE + MLA_ROPE), 1, 0)
    out_l = lax.map(lambda qblk: attend(qblk, k_all, v_all), qb)
    out_l = jnp.moveaxis(out_l, 0, 1).reshape(bsz, seq_len, MLA_HEADS * MLA_V)
    return out_l @ w_o, out_c @ w_o


def peer_ffn(h, w_q, subkeys, u_tab, v_tab):
    bsz, seq_len, dm = h.shape
    half = PEER_QDIM // 2
    q = jnp.einsum('btd,dhk->bthk', h, w_q)
    s1 = jnp.einsum('bthk,hnk->bthn', q[..., :half], subkeys[:, 0])
    s2 = jnp.einsum('bthk,hnk->bthn', q[..., half:], subkeys[:, 1])
    v1, i1 = lax.top_k(s1, PEER_TOPK)
    v2, i2 = lax.top_k(s2, PEER_TOPK)
    n_cand = PEER_TOPK * PEER_TOPK
    cand = (v1[..., :, None] + v2[..., None, :]).reshape(bsz, seq_len, PEER_HEADS, n_cand)
    cidx = (i1[..., :, None] * PEER_KEYS + i2[..., None, :]).reshape(bsz, seq_len, PEER_HEADS, n_cand)
    best, sel = lax.top_k(cand, PEER_TOPK)
    eidx = jnp.take_along_axis(cidx, sel, axis=-1)
    gate = jax.nn.softmax(best.astype(jnp.float32), axis=-1).astype(h.dtype)
    n_blk = (bsz * seq_len) // PEER_BLOCK
    n_sel = PEER_HEADS * PEER_TOPK
    hb = h.reshape(n_blk, PEER_BLOCK, dm)
    ib = eidx.reshape(n_blk, PEER_BLOCK, n_sel)
    gb = gate.reshape(n_blk, PEER_BLOCK, n_sel)

    def block(args):
        hk, ik, gk = args
        u = jnp.take(u_tab, ik, axis=0)
        act = jax.nn.gelu(jnp.einsum('td,ted->te', hk, u))
        v = jnp.take(v_tab, ik, axis=0)
        return jnp.einsum('te,ted->td', gk * act, v)

    return lax.map(block, (hb, ib, gb)).reshape(bsz, seq_len, dm)


PEER_TM = 512
PEER_EB = 1024
PEER_ROWS = 16
PEER_LANES = 128
PEER_ACT_ROWS = 512
VMEM_LIMIT = 56 * 1024 * 1024
NEG_INF = float("-inf")
GELU_C0 = math.sqrt(2.0 / math.pi)
GELU_C1 = 0.044715 * GELU_C0


def _top16_sorted(s):
    n = PEER_TOPK
    m = s.shape[0] // 8
    x = [s[8 * r:8 * r + 8, :] for r in range(m)]

    def exchange(i, l, descending):
        hi, lo = jnp.maximum(x[i], x[l]), jnp.minimum(x[i], x[l])
        x[i], x[l] = (hi, lo) if descending else (lo, hi)

    def merge_bitonic(size):
        j = size // 2
        while j >= 1:
            for i in range(size):
                if i ^ j > i:
                    exchange(i, i ^ j, True)
            j //= 2

    k = 2
    while k < m:
        j = k // 2
        while j >= 1:
            for i in range(m):
                if i ^ j > i:
                    exchange(i, i ^ j, (i & k) == 0)
            j //= 2
        k *= 2
    merge_bitonic(m)
    shifts = [4, 2, 1]
    if m < n:
        first = shifts.pop(0)
        x = x + [pltpu.roll(x[m - 1 - r], first, axis=0) for r in range(m)]
        merge_bitonic(n)
    for shift in shifts:
        other = [pltpu.roll(x[n - 1 - r], shift, axis=0) for r in range(n)]
        for r in range(n):
            x[r] = jnp.maximum(x[r], other[r])
        merge_bitonic(n)
    return [x[r][0:1, :] for r in range(n)]


def _dot3(a_hi, a_lo, b_hi, b_lo):
    f32 = jnp.float32
    return (jnp.dot(a_hi, b_hi, preferred_element_type=f32)
            + jnp.dot(a_hi, b_lo, preferred_element_type=f32)
            + jnp.dot(a_lo, b_hi, preferred_element_type=f32))


def _split_bf16(v):
    hi = v.astype(jnp.bfloat16)
    lo = (v - hi.astype(jnp.float32)).astype(jnp.bfloat16)
    return hi, lo


def _peer_kernel(x_ref, mod_ref, wq_ref, skh_ref, skl_ref, u_ref, vt_ref, vtl_ref, lng_ref, lnb_ref,
                 o_ref,
                 ht_ref, s_ref, n1_ref, e1_ref, r2_ref, e2_ref, v1_ref, v2_ref, cand_ref,
                 act_ref, gact_ref, acc_ref):
    f32 = jnp.float32
    bf16 = jnp.bfloat16
    eb = pl.program_id(1)
    n_eb = pl.num_programs(1)
    dm = x_ref.shape[1]
    a_per = PEER_EB // PEER_KEYS
    half = PEER_QDIM // 2

    @pl.when(eb == 0)
    def _prologue():
        mod = mod_ref[0]
        h = x_ref[...] * (1.0 + mod[:, 4 * dm:5 * dm]) + mod[:, 3 * dm:4 * dm]
        ht = h.T.astype(bf16)
        ht_ref[...] = ht
        qt = jnp.dot(wq_ref[...], ht, preferred_element_type=f32)
        cand_ref[...] = jnp.full(cand_ref.shape, NEG_INF, f32)
        gact_ref[...] = jnp.zeros(gact_ref.shape, bf16)
        for hd in range(PEER_HEADS):
            q1_hi, q1_lo = _split_bf16(qt[hd * PEER_QDIM: hd * PEER_QDIM + half])
            q2_hi, q2_lo = _split_bf16(qt[hd * PEER_QDIM + half: (hd + 1) * PEER_QDIM])
            s_ref[0] = _dot3(skh_ref[hd, 0], skl_ref[hd, 0], q1_hi, q1_lo)
            s_ref[1] = _dot3(skh_ref[hd, 1], skl_ref[hd, 1], q2_hi, q2_lo)
            for lg in range(0, x_ref.shape[0], PEER_LANES):
                lanes = slice(lg, lg + PEER_LANES)
                s1, s2 = s_ref[0, :, lanes], s_ref[1, :, lanes]
                top1 = _top16_sorted(s1)
                top2 = _top16_sorted(s2)
                for k in range(PEER_TOPK):
                    v1_ref[k:k + 1, lanes] = top1[k]
                    v2_ref[k:k + 1, lanes] = top2[k]
                off = 0
                for i in range(PEER_TOPK):
                    cnt = PEER_TOPK // (i + 1)
                    cand_ref[off:off + cnt, lanes] = v1_ref[i:i + 1, lanes] + v2_ref[0:cnt, lanes]
                    off += cnt
                best = _top16_sorted(cand_ref[:, lanes])
                z = jnp.zeros_like(best[0])
                for k in range(PEER_TOPK):
                    z = z + jnp.exp(best[k] - best[0])
                tau = best[PEER_TOPK - 1]
                n1 = jnp.zeros(s1.shape, f32)
                for k in range(PEER_TOPK):
                    n1 = jnp.where(s1 + top2[k] >= tau, k + 1.0, n1)
                n1_ref[hd, :, lanes] = n1
                e1_ref[hd, :, lanes] = jnp.exp(s1 - top1[0])
                rank2 = jnp.zeros(s2.shape, f32)
                for k in range(PEER_TOPK):
                    rank2 = jnp.where(top2[k] > s2, k + 1.0, rank2)
                r2_ref[hd, :, lanes] = rank2.astype(bf16)
                e2_ref[hd, :, lanes] = (jnp.exp(s2 - top2[0]) * (0.5 / z)).astype(bf16)
        acc_ref[...] = jnp.zeros(acc_ref.shape, f32)

    par = eb % 2
    for r0 in range(0, PEER_EB, PEER_ACT_ROWS):
        act_ref[r0:r0 + PEER_ACT_ROWS, :] = jnp.dot(u_ref[r0:r0 + PEER_ACT_ROWS, :], ht_ref[...],
                                                    preferred_element_type=f32)
    acc_ref[...] += jnp.dot(vt_ref[0], gact_ref[1 - par], preferred_element_type=f32)
    a_base = pl.multiple_of(eb * a_per, a_per)
    blk = (PEER_ROWS, x_ref.shape[0])
    n_slab = PEER_KEYS // PEER_ROWS
    for a in range(a_per):
        g = [None] * n_slab
        for hd in range(PEER_HEADS):
            n1a = jnp.broadcast_to(n1_ref[hd, pl.ds(a_base, a_per), :][a:a + 1, :].astype(bf16), blk)
            e1a = jnp.broadcast_to(e1_ref[hd, pl.ds(a_base, a_per), :][a:a + 1, :].astype(bf16), blk)
            for i in range(n_slab):
                rs = slice(i * PEER_ROWS, (i + 1) * PEER_ROWS)
                w = jnp.where(r2_ref[hd, rs, :] < n1a, e2_ref[hd, rs, :], jnp.zeros(blk, bf16)) * e1a
                g[i] = w if g[i] is None else g[i] + w
        for i in range(n_slab):
            rows = slice(a * PEER_KEYS + i * PEER_ROWS, a * PEER_KEYS + (i + 1) * PEER_ROWS)
            x = act_ref[rows, :]
            t = jnp.tanh(x * (GELU_C0 + GELU_C1 * (x * x)))
            gact_ref[par, rows, :] = g[i] * (x + x * t).astype(bf16)

    @pl.when(eb == n_eb - 1)
    def _epilogue():
        mod = mod_ref[0]
        ffn_t = acc_ref[...] + jnp.dot(vtl_ref[0], gact_ref[par], preferred_element_type=f32)
        y = DEEPNORM_ALPHA * x_ref[...] + mod[:, 5 * dm:6 * dm] * ffn_t.T
        mu = jnp.mean(y, axis=-1, keepdims=True)
        yc = y - mu
        var = jnp.mean(yc * yc, axis=-1, keepdims=True)
        o_ref[...] = yc * lax.rsqrt(var + LN_EPS) * lng_ref[...] + lnb_ref[...]


def _mod_row_index(i, tm, n_lat, seq, bsz):
    return jnp.where(i * tm < n_lat, (i * tm) // seq, bsz)


def peer_ln_pallas(xa, mod3, n_lat, seq, bsz, w_q, subkeys, u_tab, v_tab, ln_g, ln_b):
    n_tok, dm = xa.shape
    tm = PEER_TM
    f32, bf16 = jnp.float32, jnp.bfloat16
    wq_t = w_q.reshape(dm, PEER_HEADS * PEER_QDIM).T
    wq_bf = wq_t.astype(bf16)
    sk_hi = subkeys.astype(bf16)
    sk_lo = (subkeys - sk_hi.astype(f32)).astype(bf16)
    u_bf = u_tab.astype(bf16)
    n_eb = PEER_EXPERTS // PEER_EB
    vt_bf = v_tab.reshape(n_eb, PEER_EB, dm).transpose(0, 2, 1).astype(bf16)
    qd = PEER_HEADS * PEER_QDIM
    n_cand = sum(PEER_TOPK // (i + 1) for i in range(PEER_TOPK))
    n_cand_pad = 64
    assert n_cand <= n_cand_pad
    mod_idx = functools.partial(_mod_row_index, tm=tm, n_lat=n_lat, seq=seq, bsz=bsz)
    return pl.pallas_call(
        _peer_kernel,
        grid=(n_tok // tm, n_eb),
        in_specs=[
            pl.BlockSpec((tm, dm), lambda i, e: (i, 0)),
            pl.BlockSpec((1, 1, 6 * dm), lambda i, e: (mod_idx(i), 0, 0)),
            pl.BlockSpec((qd, dm), lambda i, e: (0, 0)),
            pl.BlockSpec((PEER_HEADS, 2, PEER_KEYS, PEER_QDIM // 2), lambda i, e: (0, 0, 0, 0)),
            pl.BlockSpec((PEER_HEADS, 2, PEER_KEYS, PEER_QDIM // 2), lambda i, e: (0, 0, 0, 0)),
            pl.BlockSpec((PEER_EB, dm), lambda i, e: (e, 0)),
            pl.BlockSpec((1, dm, PEER_EB), lambda i, e: (jnp.maximum(e - 1, 0), 0, 0)),
            pl.BlockSpec((1, dm, PEER_EB), lambda i, e: (n_eb - 1, 0, 0)),
            pl.BlockSpec((1, dm), lambda i, e: (0, 0)),
            pl.BlockSpec((1, dm), lambda i, e: (0, 0)),
        ],
        out_specs=pl.BlockSpec((tm, dm), lambda i, e: (i, 0)),
        out_shape=jax.ShapeDtypeStruct((n_tok, dm), f32),
        scratch_shapes=[
            pltpu.VMEM((dm, tm), bf16),
            pltpu.VMEM((2, PEER_KEYS, tm), f32),
            pltpu.VMEM((PEER_HEADS, PEER_KEYS, tm), f32),
            pltpu.VMEM((PEER_HEADS, PEER_KEYS, tm), f32),
            pltpu.VMEM((PEER_HEADS, PEER_KEYS, tm), bf16),
            pltpu.VMEM((PEER_HEADS, PEER_KEYS, tm), bf16),
            pltpu.VMEM((PEER_TOPK, tm), f32),
            pltpu.VMEM((PEER_TOPK, tm), f32),
            pltpu.VMEM((n_cand_pad, tm), f32),
            pltpu.VMEM((PEER_EB, tm), f32),
            pltpu.VMEM((2, PEER_EB, tm), bf16),
            pltpu.VMEM((dm, tm), f32),
        ],
        compiler_params=pltpu.CompilerParams(
            dimension_semantics=("arbitrary", "arbitrary"),
            vmem_limit_bytes=VMEM_LIMIT),
        name="peer_ln",
    )(xa, mod3, wq_bf, sk_hi, sk_lo, u_bf, vt_bf, vt_bf, ln_g.reshape(1, dm), ln_b.reshape(1, dm))


S5_CHUNK = 16
S5_LEVELS = 9


def _s5_prep(a_re, a_im, log_dt, b_re, b_im, c_re, c_im, d):
    f32, bf16 = jnp.float32, jnp.bfloat16
    hp = lax.Precision.HIGHEST
    n_g, n_p, q = S5_GROUPS, S5_STATE, S5_CHUNK
    dt = jnp.exp(log_dt.astype(f32))[..., None]

    def apow(n):
        mag = jnp.exp(a_re * dt * n)
        ang = a_im * dt * n
        return mag * jnp.cos(ang), mag * jnp.sin(ang)

    ar1, ai1 = apow(1.0)
    den = a_re * a_re + a_im * a_im
    nr, ni = ar1 - 1.0, ai1
    cr = (nr * a_re + ni * a_im) / den
    ci = (ni * a_re - nr * a_im) / den
    bbr = cr[..., None] * b_re - ci[..., None] * b_im
    bbi = cr[..., None] * b_im + ci[..., None] * b_re
    lag = jnp.arange(q + 1, dtype=f32)[:, None, None, None]
    pr, pi = apow(lag)
    mr = pr[..., None] * bbr - pi[..., None] * bbi
    mi = pr[..., None] * bbi + pi[..., None] * bbr
    kmat = (jnp.einsum('dgcp,ndgpk->ndgck', c_re, mr, precision=hp)
            - jnp.einsum('dgcp,ndgpk->ndgck', c_im, mi, precision=hp))
    r_idx = jnp.arange(q)[:, None]
    t_idx = jnp.arange(q)[None, :]

    def toeplitz(kd, lagm):
        blk = kd[jnp.clip(lagm, 0, q)]
        blk = jnp.where((lagm >= 0)[:, :, None, None, None], blk, 0.0)
        return blk.transpose(2, 0, 4, 1, 3).reshape(n_g, q * S5_GROUP, q * S5_GROUP)

    eye = jnp.eye(q * S5_GROUP, dtype=f32)
    dvec = jnp.tile(d.reshape(n_g, 1, S5_GROUP), (1, q, 1)).reshape(n_g, q * S5_GROUP)
    tsum = toeplitz(kmat[:, 0], t_idx - r_idx) + toeplitz(kmat[:, 1], r_idx - t_idx) + eye[None] * dvec[:, None, :]

    def w_in(direction, exps):
        wr = mr[exps, direction].transpose(1, 0, 3, 2).reshape(n_g, q * S5_GROUP, n_p)
        wi = mi[exps, direction].transpose(1, 0, 3, 2).reshape(n_g, q * S5_GROUP, n_p)
        return wr, wi

    def w_out(direction, exps):
        pre, pim = pr[exps, direction], pi[exps, direction]
        cre, cim = c_re[direction], c_im[direction]
        wre = cre[None] * pre[:, :, None, :] - cim[None] * pim[:, :, None, :]
        wim = -(cre[None] * pim[:, :, None, :] + cim[None] * pre[:, :, None, :])
        return (wre.transpose(1, 3, 0, 2).reshape(n_g, n_p, q * S5_GROUP),
                wim.transpose(1, 3, 0, 2).reshape(n_g, n_p, q * S5_GROUP))

    steps = jnp.arange(q)
    win = [w_in(0, q - 1 - steps), w_in(1, steps)]
    wout = [w_out(0, steps + 1), w_out(1, q - steps)]
    zc = jnp.zeros((n_g, q * S5_GROUP, n_p), f32)
    zr = jnp.zeros((n_g, n_p, q * S5_GROUP), f32)
    win_p, wout_p = [], []
    for direction in range(2):
        wr, wi = win[direction]
        even = jnp.concatenate([wr, zc, wi, zc], axis=2)
        odd = jnp.concatenate([zc, wr, zc, wi], axis=2)
        is_odd = (jnp.arange(n_g) % 2 == 1)[:, None, None]
        win_p.append(jnp.where(is_odd, odd, even))
        vr, vi = wout[direction]
        even = jnp.concatenate([vr, zr, vi, zr], axis=1)
        odd = jnp.concatenate([zr, vr, zr, vi], axis=1)
        wout_p.append(jnp.where(is_odd, odd, even))
    win_p = jnp.stack(win_p, axis=1).reshape(n_g // 2, 2, 2, 4 * n_p, 4 * n_p).transpose(0, 2, 1, 3, 4)
    wout_p = jnp.stack(wout_p, axis=1).reshape(n_g // 2, 2, 2, 4 * n_p, 4 * n_p).transpose(0, 2, 1, 3, 4)
    lvl = (q * 2.0 ** jnp.arange(S5_LEVELS, dtype=f32))[:, None, None, None]
    lr, li = apow(lvl)
    pw = jnp.stack([lr, li], axis=1)
    pw = pw.transpose(3, 2, 0, 1, 4).reshape(n_g // 2, 2, 2, S5_LEVELS, 2, n_p)
    pw = pw.transpose(0, 2, 3, 4, 1, 5).reshape(n_g // 2, 2, 2 * S5_LEVELS, 2 * n_p)
    return tsum.astype(bf16).reshape(n_g // 2, 2, q * S5_GROUP, q * S5_GROUP), win_p.astype(bf16), wout_p.astype(bf16), pw


def _s5_prep_fast(a_re, a_im, log_dt, b_re, b_im, c_re, c_im, d):
    f32, bf16 = jnp.float32, jnp.bfloat16
    hp = lax.Precision.HIGHEST
    n_g, n_p, q = S5_GROUPS, S5_STATE, S5_CHUNK
    width = q * S5_GROUP
    dt = jnp.exp(log_dt.astype(f32))[..., None]

    def apow(n):
        mag = jnp.exp(a_re * dt * n)
        ang = a_im * dt * n
        return mag * jnp.cos(ang), mag * jnp.sin(ang)

    def cmul(xr, xi, yr, yi):
        return xr * yr - xi * yi, xr * yi + xi * yr

    ar1, ai1 = apow(1.0)
    den = a_re * a_re + a_im * a_im
    nr, ni = ar1 - 1.0, ai1
    cr = (nr * a_re + ni * a_im) / den
    ci = (ni * a_re - nr * a_im) / den
    bbr = cr[..., None] * b_re - ci[..., None] * b_im
    bbi = cr[..., None] * b_im + ci[..., None] * b_re
    steps = jnp.arange(q, dtype=f32)

    def powers(direction, sign):
        mag = jnp.exp(a_re[direction][:, None, :] * dt[direction][:, None, :] * (sign * steps)[None, :, None])
        ang = a_im[direction][:, None, :] * dt[direction][:, None, :] * (sign * steps)[None, :, None]
        return mag * jnp.cos(ang), mag * jnp.sin(ang)

    def factors(direction):
        lsign = -1.0 if direction == 0 else 1.0
        pr_l, pi_l = powers(direction, lsign)
        pr_r, pi_r = powers(direction, -lsign)
        bt_r = bbr[direction].transpose(0, 2, 1)[:, None, :, :]
        bt_i = bbi[direction].transpose(0, 2, 1)[:, None, :, :]
        lt_r, lt_i = cmul(bt_r, bt_i, pr_l[:, :, None, :], pi_l[:, :, None, :])
        ct_r = c_re[direction].transpose(0, 2, 1)[:, :, None, :]
        ct_i = c_im[direction].transpose(0, 2, 1)[:, :, None, :]
        r_r, r_i = cmul(ct_r, ct_i, pr_r.transpose(0, 2, 1)[:, :, :, None], pi_r.transpose(0, 2, 1)[:, :, :, None])
        return (lt_r.reshape(n_g, width, n_p), lt_i.reshape(n_g, width, n_p),
                r_r.reshape(n_g, n_p, width), r_i.reshape(n_g, n_p, width))

    row_t = jnp.arange(width)[:, None] // S5_GROUP
    col_t = jnp.arange(width)[None, :] // S5_GROUP
    tsum = jnp.eye(width, dtype=f32)[None] * jnp.tile(d.reshape(n_g, 1, S5_GROUP), (1, q, 1)).reshape(n_g, 1, width)
    win, wout = [], []
    for direction in range(2):
        lt_r, lt_i, r_r, r_i = factors(direction)
        prod = (jnp.einsum('gxp,gpy->gxy', lt_r, r_r, precision=hp)
                - jnp.einsum('gxp,gpy->gxy', lt_i, r_i, precision=hp))
        mask = (col_t >= row_t) if direction == 0 else (row_t >= col_t)
        tsum = tsum + jnp.where(mask[None], prod, 0.0)
        if direction == 0:
            in_r, in_i = apow(q - 1.0)
            out_r, out_i = ar1, ai1
            w_r, w_i = cmul(lt_r, lt_i, in_r[0][:, None, :], in_i[0][:, None, :])
        else:
            out_r, out_i = apow(float(q))
            w_r, w_i = lt_r, lt_i
        win.append((w_r, w_i))
        o_r, o_i = cmul(r_r, r_i, out_r[direction][:, :, None], out_i[direction][:, :, None])
        wout.append((o_r, -o_i))
    zc = jnp.zeros((n_g, width, n_p), f32)
    zr = jnp.zeros((n_g, n_p, width), f32)
    win_p, wout_p = [], []
    for direction in range(2):
        wr, wi = win[direction]
        even = jnp.concatenate([wr, zc, wi, zc], axis=2)
        odd = jnp.concatenate([zc, wr, zc, wi], axis=2)
        is_odd = (jnp.arange(n_g) % 2 == 1)[:, None, None]
        win_p.append(jnp.where(is_odd, odd, even))
        vr, vi = wout[direction]
        even = jnp.concatenate([vr, zr, vi, zr], axis=1)
        odd = jnp.concatenate([zr, vr, zr, vi], axis=1)
        wout_p.append(jnp.where(is_odd, odd, even))
    win_p = jnp.stack(win_p, axis=1).reshape(n_g // 2, 2, 2, 4 * n_p, 4 * n_p).transpose(0, 2, 1, 3, 4)
    wout_p = jnp.stack(wout_p, axis=1).reshape(n_g // 2, 2, 2, 4 * n_p, 4 * n_p).transpose(0, 2, 1, 3, 4)
    lvl = (q * 2.0 ** jnp.arange(S5_LEVELS, dtype=f32))[:, None, None, None]
    lr, li = apow(lvl)
    pw = jnp.stack([lr, li], axis=1)
    pw = pw.transpose(3, 2, 0, 1, 4).reshape(n_g // 2, 2, 2, S5_LEVELS, 2, n_p)
    pw = pw.transpose(0, 2, 3, 4, 1, 5).reshape(n_g // 2, 2, 2 * S5_LEVELS, 2 * n_p)
    return tsum.astype(bf16).reshape(n_g // 2, 2, width, width), win_p.astype(bf16), wout_p.astype(bf16), pw


def _shift_rows(v, r, up):
    n = v.shape[0]
    row = lax.broadcasted_iota(jnp.int32, v.shape, 0)
    if up:
        return jnp.where(row < n - r, pltpu.roll(v, n - r, axis=0), 0.0)
    return jnp.where(row >= r, pltpu.roll(v, r, axis=0), 0.0)


def _chunk_scan(re, im, pw, rows_per_chunk, up):
    n_chunks = re.shape[0] // rows_per_chunk
    level, s = 0, 1
    while s < n_chunks:
        ar, ai = pw[2 * level:2 * level + 1, :], pw[2 * level + 1:2 * level + 2, :]
        sre = _shift_rows(re, s * rows_per_chunk, up)
        sim = _shift_rows(im, s * rows_per_chunk, up)
        re, im = re + ar * sre - ai * sim, im + ar * sim + ai * sre
        level, s = level + 1, 2 * s
    return re, im


def _s5_kernel(u_ref, scl_ref, shl_ref, scc_ref, shc_ref, t_ref, win_ref, wout_ref, pw_ref, y_ref, *, bsz, ctx_chunks):
    f32, bf16 = jnp.float32, jnp.bfloat16
    n = u_ref.shape[1]
    width = u_ref.shape[2]
    rc = ctx_chunks * bsz
    hs = []
    for gi in range(2):
        xv = u_ref[gi].reshape(n // 8, 8, width)
        hl = xv * scl_ref[gi][None] + shl_ref[gi][None]
        hc = xv * scc_ref[gi][None] + shc_ref[gi][None]
        slab = lax.broadcasted_iota(jnp.int32, xv.shape, 0)
        hs.append(jnp.where(slab < rc // 8, hc, hl).reshape(n, width).astype(bf16))
    y = [jnp.dot(hs[gi], t_ref[0, gi], preferred_element_type=f32) for gi in range(2)]
    half = width // 2
    for direction in range(2):
        sloc = (jnp.dot(hs[0], win_ref[0, direction, 0], preferred_element_type=f32)
                + jnp.dot(hs[1], win_ref[0, direction, 1], preferred_element_type=f32))
        re, im = sloc[:, :half], sloc[:, half:]
        pw = pw_ref[0, direction]
        if direction == 0:
            sre, sim = _chunk_scan(_shift_rows(re, bsz, False), _shift_rows(im, bsz, False), pw, bsz, False)
        else:
            cre, cim = _chunk_scan(_shift_rows(re[:rc], bsz, True), _shift_rows(im[:rc], bsz, True), pw, bsz, True)
            ar, ai = pw[0:1, :], pw[1:2, :]
            fre = ar * cre[0:8] - ai * cim[0:8] + re[0:8]
            fim = ar * cim[0:8] + ai * cre[0:8] + im[0:8]
            row8 = lax.broadcasted_iota(jnp.int32, fre.shape, 0)
            tre = jnp.where(row8 >= 8 - bsz, pltpu.roll(fre, 8 - bsz, axis=0), 0.0)
            tim = jnp.where(row8 >= 8 - bsz, pltpu.roll(fim, 8 - bsz, axis=0), 0.0)
            lre, lim = _shift_rows(re[rc:], bsz, True), _shift_rows(im[rc:], bsz, True)
            lre = jnp.concatenate([lre[:-8], lre[-8:] + tre], axis=0)
            lim = jnp.concatenate([lim[:-8], lim[-8:] + tim], axis=0)
            lre, lim = _chunk_scan(lre, lim, pw, bsz, True)
            sre = jnp.concatenate([cre, lre], axis=0)
            sim = jnp.concatenate([cim, lim], axis=0)
        s_in = jnp.concatenate([sre, sim], axis=1).astype(bf16)
        for gi in range(2):
            y[gi] = y[gi] + jnp.dot(s_in, wout_ref[0, direction, gi], preferred_element_type=f32)
    for gi in range(2):
        y_ref[gi] = y[gi]


def s5_ssm_pallas(xa, mod_tab, n_lat, seq, bsz, prep):
    tsum, win_p, wout_p, pw = prep
    f32 = jnp.float32
    n_tok, dm = xa.shape
    ctx_len = (n_tok - n_lat) // bsz
    q = S5_CHUNK
    n_chunks = (seq + ctx_len) // q
    width = q * S5_GROUP
    full = jnp.concatenate([xa[n_lat:].reshape(bsz, ctx_len, dm), xa[:n_lat].reshape(bsz, seq, dm)], axis=1)
    u = full.reshape(bsz, n_chunks, q, S5_GROUPS, S5_GROUP).transpose(3, 1, 0, 2, 4).reshape(S5_GROUPS, n_chunks * bsz, width)

    def tile(vec_rows):
        rows = vec_rows.shape[0]
        t = vec_rows.reshape(rows, S5_GROUPS, 1, S5_GROUP)
        t = jnp.broadcast_to(t, (rows, S5_GROUPS, q, S5_GROUP)).reshape(rows, S5_GROUPS, width)
        return jnp.tile(t.transpose(1, 0, 2), (1, 8 // rows, 1))

    scl, shl = tile(1.0 + mod_tab[:bsz, dm:2 * dm]), tile(mod_tab[:bsz, 0:dm])
    scc = tile(jnp.broadcast_to(1.0 + mod_tab[bsz:bsz + 1, dm:2 * dm], (bsz, dm)))
    shc = tile(jnp.broadcast_to(mod_tab[bsz:bsz + 1, 0:dm], (bsz, dm)))
    n_rows = n_chunks * bsz
    tile_spec = pl.BlockSpec((2, 8, width), lambda p: (p, 0, 0))
    y = pl.pallas_call(
        functools.partial(_s5_kernel, bsz=bsz, ctx_chunks=ctx_len // q),
        grid=(S5_GROUPS // 2,),
        in_specs=[
            pl.BlockSpec((2, n_rows, width), lambda p: (p, 0, 0)),
            tile_spec, tile_spec, tile_spec, tile_spec,
            pl.BlockSpec((1, 2, width, width), lambda p: (p, 0, 0, 0)),
            pl.BlockSpec((1, 2, 2, width, width), lambda p: (p, 0, 0, 0, 0)),
            pl.BlockSpec((1, 2, 2, width, width), lambda p: (p, 0, 0, 0, 0)),
            pl.BlockSpec((1, 2, 2 * S5_LEVELS, 2 * S5_STATE), lambda p: (p, 0, 0, 0)),
        ],
        out_specs=pl.BlockSpec((2, n_rows, width), lambda p: (p, 0, 0)),
        out_shape=jax.ShapeDtypeStruct((S5_GROUPS, n_rows, width), f32),
        compiler_params=pltpu.CompilerParams(dimension_semantics=("arbitrary",), vmem_limit_bytes=VMEM_LIMIT),
        name="s5_ssm",
    )(u, scl, shl, scc, shc, tsum, win_p, wout_p, pw)
    yf = y.reshape(S5_GROUPS, n_chunks, bsz, q, S5_GROUP).transpose(2, 1, 3, 0, 4).reshape(bsz, seq + ctx_len, dm)
    return jnp.concatenate([yf[:, ctx_len:].reshape(n_lat, dm), yf[:, :ctx_len].reshape(bsz * ctx_len, dm)], axis=0)


def _glu_ln_kernel(y_ref, x_ref, mod_ref, wv_ref, wg_ref, lng_ref, lnb_ref, o_ref):
    f32 = jnp.float32
    dm = x_ref.shape[1]
    g = jax.nn.gelu(y_ref[...]).astype(jnp.bfloat16)
    val = jnp.dot(g, wv_ref[...], preferred_element_type=f32)
    gate = jnp.dot(g, wg_ref[...], preferred_element_type=f32)
    out = val * jax.nn.sigmoid(gate)
    mod = mod_ref[0]
    z = DEEPNORM_ALPHA * x_ref[...] + mod[:, 2 * dm:3 * dm] * out
    mu = jnp.mean(z, axis=-1, keepdims=True)
    zc = z - mu
    var = jnp.mean(zc * zc, axis=-1, keepdims=True)
    o_ref[...] = zc * lax.rsqrt(var + LN_EPS) * lng_ref[...] + lnb_ref[...]


TOK_TM = 512


def glu_ln_pallas(ya, xa, mod3, n_lat, seq, bsz, w_val, w_gate, ln_g, ln_b):
    n_tok, dm = ya.shape
    tm = TOK_TM
    bf16 = jnp.bfloat16
    mod_idx = functools.partial(_mod_row_index, tm=tm, n_lat=n_lat, seq=seq, bsz=bsz)
    return pl.pallas_call(
        _glu_ln_kernel,
        grid=(n_tok // tm,),
        in_specs=[
            pl.BlockSpec((tm, dm), lambda i: (i, 0)),
            pl.BlockSpec((tm, dm), lambda i: (i, 0)),
            pl.BlockSpec((1, 1, 6 * dm), lambda i: (mod_idx(i), 0, 0)),
            pl.BlockSpec((dm, dm), lambda i: (0, 0)),
            pl.BlockSpec((dm, dm), lambda i: (0, 0)),
            pl.BlockSpec((1, dm), lambda i: (0, 0)),
            pl.BlockSpec((1, dm), lambda i: (0, 0)),
        ],
        out_specs=pl.BlockSpec((tm, dm), lambda i: (i, 0)),
        out_shape=jax.ShapeDtypeStruct((n_tok, dm), jnp.float32),
        compiler_params=pltpu.CompilerParams(dimension_semantics=("arbitrary",), vmem_limit_bytes=VMEM_LIMIT),
        name="glu_ln",
    )(ya, xa, mod3, w_val.astype(bf16), w_gate.astype(bf16), ln_g.reshape(1, dm), ln_b.reshape(1, dm))


MLA_HD = 128
MLA_TQ = 256
MLA_HB = 4
MLA_KSPLIT = 4


def _mla_prep(w_down, w_uq, w_uk, w_uv):
    bf16 = jnp.bfloat16
    quarter = MLA_ROPE // 4
    swap = np.concatenate([np.arange(quarter, 2 * quarter), np.arange(0, quarter),
                           np.arange(3 * quarter, 4 * quarter), np.arange(2 * quarter, 3 * quarter)])
    dm = w_down.shape[0]
    w_cq = w_down[:, :MLA_Q_RANK]
    w_ckv = w_down[:, MLA_Q_RANK:MLA_Q_RANK + MLA_KV_RANK]
    w_kr = w_down[:, MLA_Q_RANK + MLA_KV_RANK:]
    wd_t = jnp.concatenate([w_cq, w_ckv, jnp.zeros((dm, MLA_NOPE), w_down.dtype), w_kr, w_kr[:, swap]], axis=1).T
    rope = w_uq[:, :, MLA_NOPE:]
    wuq_t = jnp.concatenate([w_uq, rope[:, :, swap]], axis=2).reshape(MLA_Q_RANK, MLA_HEADS * MLA_HD).T
    wuk = jnp.concatenate([w_uk, jnp.zeros((MLA_KV_RANK, MLA_HEADS, MLA_HD - MLA_NOPE), w_uk.dtype)], axis=2)
    wuk = wuk.reshape(MLA_KV_RANK, MLA_HEADS * MLA_HD)
    wuv_t = w_uv.reshape(MLA_KV_RANK, MLA_HEADS * MLA_V).T
    return wd_t.astype(bf16), wuq_t.astype(bf16), wuk.astype(bf16), wuv_t.astype(bf16)


def _rope_tables(seq, tm):
    quarter = MLA_ROPE // 4
    freqs = ROPE_BASE ** (-jnp.arange(quarter, dtype=jnp.float32) / quarter)
    t = jnp.arange(seq, dtype=jnp.float32)
    row, col = jnp.floor(t / GRID_W), t - GRID_W * jnp.floor(t / GRID_W)
    ang_r, ang_c = freqs[:, None] * row[None, :], freqs[:, None] * col[None, :]
    cos32 = jnp.concatenate([jnp.cos(ang_r), jnp.cos(ang_r), jnp.cos(ang_c), jnp.cos(ang_c)], axis=0)
    sin32 = jnp.concatenate([-jnp.sin(ang_r), jnp.sin(ang_r), -jnp.sin(ang_c), jnp.sin(ang_c)], axis=0)
    ones = jnp.ones((MLA_NOPE, seq + tm), jnp.float32)
    zeros = jnp.zeros((MLA_HD - MLA_NOPE - MLA_ROPE, seq + tm), jnp.float32)
    cos_t = jnp.concatenate([ones, jnp.concatenate([cos32, jnp.ones((MLA_ROPE, tm))], axis=1), zeros], axis=0)
    sin_t = jnp.concatenate([0.0 * ones, jnp.concatenate([sin32, jnp.zeros((MLA_ROPE, tm))], axis=1), zeros], axis=0)
    return cos_t, sin_t


def _mla_proj_kernel(x_ref, mod_ref, cos_ref, sin_ref, wd_ref, wuq_ref, wuk_ref, wuv_ref, qn_ref, kvn_ref,
                     qt_ref, k_ref, vt_ref):
    f32, bf16 = jnp.float32, jnp.bfloat16
    dm = x_ref.shape[1]
    mod = mod_ref[0]
    h = x_ref[...] * (1.0 + mod[:, dm:2 * dm]) + mod[:, 0:dm]
    ht = h.T.astype(bf16)
    dt = jnp.dot(wd_ref[...], ht, preferred_element_type=f32)
    cq, ckv, kr = dt[:MLA_Q_RANK], dt[MLA_Q_RANK:MLA_Q_RANK + MLA_KV_RANK], dt[MLA_Q_RANK + MLA_KV_RANK:]
    cqn = cq * lax.rsqrt(jnp.mean(cq * cq, axis=0, keepdims=True) + RMS_EPS) * qn_ref[...]
    ckvn = ckv * lax.rsqrt(jnp.mean(ckv * ckv, axis=0, keepdims=True) + RMS_EPS) * kvn_ref[...]
    cos_t, sin_t = cos_ref[...], sin_ref[...]

    def rope(v):
        shifted = jnp.concatenate([v[MLA_ROPE:], v[:MLA_ROPE]], axis=0)
        return v * cos_t + shifted * sin_t

    scale = (MLA_NOPE + MLA_ROPE) ** -0.5
    q_all = jnp.dot(wuq_ref[...], cqn.astype(bf16), preferred_element_type=f32)
    for hd in range(MLA_HEADS):
        qt_ref[hd] = (rope(q_all[hd * MLA_HD:(hd + 1) * MLA_HD]) * scale).astype(bf16)
    ckvn_bf = ckvn.astype(bf16)
    vt_ref[...] = jnp.dot(wuv_ref[...], ckvn_bf, preferred_element_type=f32).astype(bf16)
    k_all = jnp.dot(ckvn.T.astype(bf16), wuk_ref[...], preferred_element_type=f32)
    kr_rows = rope(kr).T
    for hd in range(MLA_HEADS):
        k_ref[hd] = (k_all[:, hd * MLA_HD:(hd + 1) * MLA_HD] + kr_rows).astype(bf16)


def _mla_attn_kernel(*refs, with_latent):
    f32, bf16 = jnp.float32, jnp.bfloat16
    if with_latent:
        q_ref, kl_ref, kc_ref, vl_ref, vc_ref, o_ref = refs
    else:
        q_ref, kc_ref, vc_ref, o_ref = refs
    for i in range(q_ref.shape[0]):
        rows = slice(i * MLA_V, (i + 1) * MLA_V)
        q = q_ref[i]
        s_c = jnp.dot(kc_ref[i], q, preferred_element_type=f32)
        m = jnp.max(s_c, axis=0, keepdims=True)
        parts = []
        if with_latent:
            n_keys = kl_ref.shape[1]
            for k0 in range(0, n_keys, n_keys // MLA_KSPLIT):
                ks = slice(k0, k0 + n_keys // MLA_KSPLIT)
                s_l = jnp.dot(kl_ref[i, ks, :], q, preferred_element_type=f32)
                m = jnp.maximum(m, jnp.max(s_l, axis=0, keepdims=True))
                parts.append((ks, s_l))
        p_c = jnp.exp(s_c - m)
        den = jnp.sum(p_c, axis=0, keepdims=True)
        o = jnp.dot(vc_ref[rows, :], p_c.astype(bf16), preferred_element_type=f32)
        for ks, s_l in parts:
            p_l = jnp.exp(s_l - m)
            den = den + jnp.sum(p_l, axis=0, keepdims=True)
            o = o + jnp.dot(vl_ref[rows, ks], p_l.astype(bf16), preferred_element_type=f32)
        o_ref[rows, :] = (o / den).astype(bf16)


def _mla_out_kernel(ot_ref, x_ref, mod_ref, wo_ref, lng_ref, lnb_ref, o_ref):
    f32 = jnp.float32
    dm = x_ref.shape[1]
    attn = ot_ref[...].astype(f32).T.astype(jnp.bfloat16)
    out = jnp.dot(attn, wo_ref[...], preferred_element_type=f32)
    mod = mod_ref[0]
    z = DEEPNORM_ALPHA * x_ref[...] + mod[:, 2 * dm:3 * dm] * out
    mu = jnp.mean(z, axis=-1, keepdims=True)
    zc = z - mu
    var = jnp.mean(zc * zc, axis=-1, keepdims=True)
    o_ref[...] = zc * lax.rsqrt(var + LN_EPS) * lng_ref[...] + lnb_ref[...]


def mla_ln_pallas(xa, mod3, n_lat, seq, bsz, w_down, q_norm, kv_norm, w_uq, w_uk, w_uv, w_o, ln_g, ln_b):
    f32, bf16 = jnp.float32, jnp.bfloat16
    n_tok, dm = xa.shape
    n_ctx = n_tok - n_lat
    ctx_len = n_ctx // bsz
    tm = TOK_TM
    wd_t, wuq_t, wuk, wuv_t = _mla_prep(w_down, w_uq, w_uk, w_uv)
    cos_t, sin_t = _rope_tables(seq, tm)
    mod_idx = functools.partial(_mod_row_index, tm=tm, n_lat=n_lat, seq=seq, bsz=bsz)
    tiles_per_sample = seq // tm
    pos_idx = lambda i: jnp.where(i * tm < n_lat, i % tiles_per_sample, tiles_per_sample)
    n_hd, n_dn = MLA_HEADS * MLA_HD, wd_t.shape[0]
    whole = lambda shape: pl.BlockSpec(shape, lambda i: (0,) * len(shape))
    params = pltpu.CompilerParams(dimension_semantics=("arbitrary",), vmem_limit_bytes=VMEM_LIMIT)
    qt, k, vt = pl.pallas_call(
        _mla_proj_kernel,
        grid=(n_tok // tm,),
        in_specs=[
            pl.BlockSpec((tm, dm), lambda i: (i, 0)),
            pl.BlockSpec((1, 1, 6 * dm), lambda i: (mod_idx(i), 0, 0)),
            pl.BlockSpec((MLA_HD, tm), lambda i: (0, pos_idx(i))),
            pl.BlockSpec((MLA_HD, tm), lambda i: (0, pos_idx(i))),
            whole((n_dn, dm)), whole((n_hd, MLA_Q_RANK)), whole((MLA_KV_RANK, n_hd)),
            whole((MLA_HEADS * MLA_V, MLA_KV_RANK)), whole((MLA_Q_RANK, 1)), whole((MLA_KV_RANK, 1)),
        ],
        out_specs=[
            pl.BlockSpec((MLA_HEADS, MLA_HD, tm), lambda i: (0, 0, i)),
            pl.BlockSpec((MLA_HEADS, tm, MLA_HD), lambda i: (0, i, 0)),
            pl.BlockSpec((MLA_HEADS * MLA_V, tm), lambda i: (0, i)),
        ],
        out_shape=[
            jax.ShapeDtypeStruct((MLA_HEADS, MLA_HD, n_tok), bf16),
            jax.ShapeDtypeStruct((MLA_HEADS, n_tok, MLA_HD), bf16),
            jax.ShapeDtypeStruct((MLA_HEADS * MLA_V, n_tok), bf16),
        ],
        compiler_params=params,
        name="mla_proj",
    )(xa, mod3, cos_t, sin_t, wd_t, wuq_t, wuk, wuv_t, q_norm.reshape(-1, 1), kv_norm.reshape(-1, 1))

    tq, hb = MLA_TQ, MLA_HB
    n_qt = seq // tq
    ctx_blk = n_lat // ctx_len
    params3 = pltpu.CompilerParams(dimension_semantics=("arbitrary",) * 3, vmem_limit_bytes=VMEM_LIMIT)
    ot_lat = pl.pallas_call(
        functools.partial(_mla_attn_kernel, with_latent=True),
        grid=(bsz, MLA_HEADS // hb, n_qt),
        in_specs=[
            pl.BlockSpec((hb, MLA_HD, tq), lambda b, hd, t: (hd, 0, b * n_qt + t)),
            pl.BlockSpec((hb, seq, MLA_HD), lambda b, hd, t: (hd, b, 0)),
            pl.BlockSpec((hb, ctx_len, MLA_HD), lambda b, hd, t: (hd, ctx_blk + b, 0)),
            pl.BlockSpec((hb * MLA_V, seq), lambda b, hd, t: (hd, b)),
            pl.BlockSpec((hb * MLA_V, ctx_len), lambda b, hd, t: (hd, ctx_blk + b)),
        ],
        out_specs=pl.BlockSpec((hb * MLA_V, tq), lambda b, hd, t: (hd, b * n_qt + t)),
        out_shape=jax.ShapeDtypeStruct((MLA_HEADS * MLA_V, n_lat), bf16),
        compiler_params=params3,
        name="mla_attn_latent",
    )(qt, k, k, vt, vt)
    ot_ctx = pl.pallas_call(
        functools.partial(_mla_attn_kernel, with_latent=False),
        grid=(bsz, MLA_HEADS // hb, 1),
        in_specs=[
            pl.BlockSpec((hb, MLA_HD, ctx_len), lambda b, hd, t: (hd, 0, ctx_blk + b)),
            pl.BlockSpec((hb, ctx_len, MLA_HD), lambda b, hd, t: (hd, ctx_blk + b, 0)),
            pl.BlockSpec((hb * MLA_V, ctx_len), lambda b, hd, t: (hd, ctx_blk + b)),
        ],
        out_specs=pl.BlockSpec((hb * MLA_V, ctx_len), lambda b, hd, t: (hd, b)),
        out_shape=jax.ShapeDtypeStruct((MLA_HEADS * MLA_V, n_ctx), bf16),
        compiler_params=params3,
        name="mla_attn_context",
    )(qt, k, vt)
    ot = jnp.concatenate([ot_lat, ot_ctx], axis=1)
    return pl.pallas_call(
        _mla_out_kernel,
        grid=(n_tok // tm,),
        in_specs=[
            pl.BlockSpec((MLA_HEADS * MLA_V, tm), lambda i: (0, i)),
            pl.BlockSpec((tm, dm), lambda i: (i, 0)),
            pl.BlockSpec((1, 1, 6 * dm), lambda i: (mod_idx(i), 0, 0)),
            whole((MLA_HEADS * MLA_V, dm)), whole((1, dm)), whole((1, dm)),
        ],
        out_specs=pl.BlockSpec((tm, dm), lambda i: (i, 0)),
        out_shape=jax.ShapeDtypeStruct((n_tok, dm), f32),
        compiler_params=params,
        name="mla_out_ln",
    )(ot, xa, mod3, w_o.astype(bf16), ln_g.reshape(1, dm), ln_b.reshape(1, dm))


SSD_DT_PAD = 128
SSD_HPG = SSD_HEADS // SSD_GROUPS
SSD_GW = SSD_HPG * SSD_HEADDIM


def _ssd_in_kernel(x_ref, mod_ref, w_ref, wdt_ref, z_ref, xbc_ref, dt_ref, dtt_ref):
    f32, bf16 = jnp.float32, jnp.bfloat16
    dm = x_ref.shape[1]
    mod = mod_ref[0]
    h = x_ref[...] * (1.0 + mod[:, dm:2 * dm]) + mod[:, 0:dm]
    proj = jnp.dot(h.astype(bf16), w_ref[...], preferred_element_type=f32)
    z_ref[...] = proj[:, :SSD_D_INNER]
    xbc_ref[...] = proj[:, SSD_D_INNER:SSD_D_INNER + SSD_CONV_DIM]
    dt_ref[...] = proj[:, SSD_D_INNER + SSD_CONV_DIM:]
    dtt_ref[...] = jnp.dot(wdt_ref[...], h.T.astype(bf16), preferred_element_type=f32)


def _softplus(v):
    return jnp.maximum(v, 0.0) + jnp.log(1.0 + jnp.exp(-jnp.abs(v)))


def _split3(v):
    f32, bf16 = jnp.float32, jnp.bfloat16
    hi = v.astype(bf16)
    r1 = v - hi.astype(f32)
    mid = r1.astype(bf16)
    lo = (r1 - mid.astype(f32)).astype(bf16)
    return jnp.concatenate([hi, mid, lo], axis=1)


def _ssd_direction(direction, xbc_ref, prev_ref, next_ref, dt_ref, dtt_ref, has_prev, has_next,
                   cw_ref, cb_ref, bias_r_ref, a_r_ref, bias_c_ref, a_c_ref, e_ref, dskip_ref, state_ref, y_ref):
    f32, bf16 = jnp.float32, jnp.bfloat16
    hp = lax.Precision.HIGHEST
    q = xbc_ref.shape[0]
    nh = SSD_HEADS
    xm = xbc_ref[...]
    row = lax.broadcasted_iota(jnp.int32, xm.shape, 0)
    before = jnp.where(has_prev, prev_ref[7:8, :], 0.0)
    after = jnp.where(has_next, next_ref[0:1, :], 0.0)
    x_dn = jnp.where(row == 0, before, pltpu.roll(xm, 1, axis=0))
    x_up = jnp.where(row == q - 1, after, pltpu.roll(xm, q - 1, axis=0))
    conv = cb_ref[...] + x_dn * cw_ref[0:1, :] + xm * cw_ref[1:2, :] + x_up * cw_ref[2:3, :]
    conv = conv * jax.nn.sigmoid(conv)
    xs = conv[:, :SSD_D_INNER]
    gn = SSD_GROUPS * SSD_STATE
    bm, cm = conv[:, SSD_D_INNER:SSD_D_INNER + gn], conv[:, SSD_D_INNER + gn:]
    hs = slice(direction * nh, (direction + 1) * nh)
    dt = _softplus(dt_ref[:, hs] + bias_r_ref[direction:direction + 1, :])
    dtt = _softplus(dtt_ref[hs, :] + bias_c_ref[direction])
    r_i = lax.broadcasted_iota(jnp.int32, (q, q), 0)
    c_i = lax.broadcasted_iota(jnp.int32, (q, q), 1)
    causal = (r_i >= c_i) if direction == 0 else (r_i <= c_i)
    tri = causal.astype(f32)
    tri_t = ((c_i >= r_i) if direction == 0 else (c_i <= r_i)).astype(f32)
    a_cum = jnp.dot(tri, dt * a_r_ref[direction:direction + 1, :], precision=hp, preferred_element_type=f32)
    a_cum_t = jnp.dot(dtt * a_c_ref[direction], tri_t, precision=hp, preferred_element_type=f32)
    last = q - 1 if direction == 0 else 0
    a_exp = jnp.dot(_split3(a_cum), e_ref[...], preferred_element_type=f32)
    dt_exp = jnp.dot(_split3(dt), e_ref[...], preferred_element_type=f32)
    xdt = xs * dt_exp
    a_tot = a_exp[last:last + 1, :]
    xw = (xdt * jnp.exp(a_tot - a_exp)).astype(bf16)
    grow = jnp.exp(a_exp)
    carry = jnp.exp(a_tot)
    xdt_bf = xdt.astype(bf16)
    for g in range(SSD_GROUPS):
        gl = slice(g * SSD_GW, (g + 1) * SSD_GW)
        nl = slice(g * SSD_STATE, (g + 1) * SSD_STATE)
        bm_g, cm_g = bm[:, nl], cm[:, nl].astype(bf16)
        prev = state_ref[direction, g]
        y_g = jnp.dot(cm_g, prev.astype(bf16), preferred_element_type=f32) * grow[:, gl]
        states = jnp.dot(bm_g.T.astype(bf16), xw[:, gl], preferred_element_type=f32)
        state_ref[direction, g] = prev * carry[:, gl] + states
        cb = lax.dot_general(cm_g, bm_g.astype(bf16), (((1,), (1,)), ((), ())), preferred_element_type=f32)
        parts = []
        for hh in range(SSD_HPG):
            hd = g * SSD_HPG + hh
            seg = jnp.broadcast_to(a_cum[:, hd:hd + 1], (q, q)) - a_cum_t[hd:hd + 1, :]
            lmat = (jnp.where(causal, jnp.exp(seg), 0.0) * cb).astype(bf16)
            parts.append(jnp.dot(lmat, xdt_bf[:, hd * SSD_HEADDIM:(hd + 1) * SSD_HEADDIM], preferred_element_type=f32))
        y_g = y_g + jnp.concatenate(parts, axis=1)
        if direction == 0:
            y_g = y_g + dskip_ref[:, gl] * xs[:, gl]
        y_ref[:, gl] = y_g


def _ssd_scan_kernel(xf_ref, pf_ref, nf_ref, dtf_ref, dttf_ref, xb_ref, pb_ref, nb_ref, dtb_ref, dttb_ref,
                     cw_ref, cb_ref, bias_r_ref, a_r_ref, bias_c_ref, a_c_ref, e_ref, dskip_ref,
                     yf_ref, yb_ref, state_ref, *, ctx_chunks, lat_chunks):
    j = pl.program_id(1)

    @pl.when(j == 0)
    def _reset():
        state_ref[...] = jnp.zeros(state_ref.shape, jnp.float32)

    in_ctx = j < ctx_chunks
    pos_f = jnp.where(in_ctx, j, j - ctx_chunks)
    seg_len = jnp.where(in_ctx, ctx_chunks, lat_chunks)
    pos_b = seg_len - 1 - pos_f
    shared = (cw_ref, cb_ref, bias_r_ref, a_r_ref, bias_c_ref, a_c_ref, e_ref, dskip_ref, state_ref)
    _ssd_direction(0, xf_ref, pf_ref, nf_ref, dtf_ref, dttf_ref, pos_f > 0, pos_f < seg_len - 1, *shared, yf_ref)
    _ssd_direction(1, xb_ref, pb_ref, nb_ref, dtb_ref, dttb_ref, pos_b > 0, pos_b < seg_len - 1, *shared, yb_ref)


def _ssd_out_kernel(yf_ref, yb_ref, z_ref, x_ref, mod_ref, nw_ref, wo_ref, lng_ref, lnb_ref, o_ref):
    f32 = jnp.float32
    dm = x_ref.shape[1]
    z = z_ref[...]
    y = (yf_ref[...] + yb_ref[...]) * (z * jax.nn.sigmoid(z))
    y = y * lax.rsqrt(jnp.mean(y * y, axis=-1, keepdims=True) + RMS_EPS) * nw_ref[...]
    out = jnp.dot(y.astype(jnp.bfloat16), wo_ref[...], preferred_element_type=f32)
    mod = mod_ref[0]
    v = DEEPNORM_ALPHA * x_ref[...] + mod[:, 2 * dm:3 * dm] * out
    mu = jnp.mean(v, axis=-1, keepdims=True)
    vc = v - mu
    var = jnp.mean(vc * vc, axis=-1, keepdims=True)
    o_ref[...] = vc * lax.rsqrt(var + LN_EPS) * lng_ref[...] + lnb_ref[...]


def ssd_ln_pallas(xa, mod3, n_lat, seq, bsz, w_in, conv_w, conv_b, dt_bias, a_log, d, norm_w, w_out, ln_g, ln_b):
    f32, bf16 = jnp.float32, jnp.bfloat16
    n_tok, dm = xa.shape
    ctx_len = (n_tok - n_lat) // bsz
    tm = TOK_TM
    q = SSD_CHUNK
    nh = SSD_HEADS
    pad = SSD_DT_PAD - 2 * nh
    w_ext = jnp.concatenate([w_in, jnp.zeros((dm, pad), w_in.dtype)], axis=1).astype(bf16)
    n_in = w_ext.shape[1]
    wdt_t = w_ext[:, SSD_D_INNER + SSD_CONV_DIM:].T
    mod_idx = functools.partial(_mod_row_index, tm=tm, n_lat=n_lat, seq=seq, bsz=bsz)
    whole = lambda shape: pl.BlockSpec(shape, lambda *_: (0,) * len(shape))
    params = pltpu.CompilerParams(dimension_semantics=("arbitrary",), vmem_limit_bytes=VMEM_LIMIT)
    z, xbc, dt, dtt = pl.pallas_call(
        _ssd_in_kernel,
        grid=(n_tok // tm,),
        in_specs=[
            pl.BlockSpec((tm, dm), lambda i: (i, 0)),
            pl.BlockSpec((1, 1, 6 * dm), lambda i: (mod_idx(i), 0, 0)),
            whole((dm, n_in)), whole((SSD_DT_PAD, dm)),
        ],
        out_specs=[
            pl.BlockSpec((tm, SSD_D_INNER), lambda i: (i, 0)),
            pl.BlockSpec((tm, SSD_CONV_DIM), lambda i: (i, 0)),
            pl.BlockSpec((tm, SSD_DT_PAD), lambda i: (i, 0)),
            pl.BlockSpec((SSD_DT_PAD, tm), lambda i: (0, i)),
        ],
        out_shape=[
            jax.ShapeDtypeStruct((n_tok, SSD_D_INNER), f32),
            jax.ShapeDtypeStruct((n_tok, SSD_CONV_DIM), f32),
            jax.ShapeDtypeStruct((n_tok, SSD_DT_PAD), f32),
            jax.ShapeDtypeStruct((SSD_DT_PAD, n_tok), f32),
        ],
        compiler_params=params,
        name="ssd_in_proj",
    )(xa, mod3, w_ext, wdt_t)

    ctx_chunks, lat_chunks = ctx_len // q, seq // q
    ctx_base = n_lat // q
    n_chunk_total = n_tok // q

    def chunk_of(b, j, backward):
        in_ctx = j < ctx_chunks
        pos = jnp.where(in_ctx, j, j - ctx_chunks)
        seg = jnp.where(in_ctx, ctx_chunks, lat_chunks)
        pos = jnp.where(backward, seg - 1 - pos, pos)
        return jnp.where(in_ctx, ctx_base + b * ctx_chunks + pos, b * lat_chunks + pos)

    sub = q // 8
    def dir_specs(backward):
        ch = lambda b, j: chunk_of(b, j, backward)
        return [
            pl.BlockSpec((q, SSD_CONV_DIM), lambda b, j: (ch(b, j), 0)),
            pl.BlockSpec((8, SSD_CONV_DIM), lambda b, j: (jnp.maximum(ch(b, j) * sub - 1, 0), 0)),
            pl.BlockSpec((8, SSD_CONV_DIM), lambda b, j: (jnp.minimum((ch(b, j) + 1) * sub, n_chunk_total * sub - 1), 0)),
            pl.BlockSpec((q, SSD_DT_PAD), lambda b, j: (ch(b, j), 0)),
            pl.BlockSpec((SSD_DT_PAD, q), lambda b, j: (0, ch(b, j))),
        ]

    a = -jnp.exp(a_log.astype(f32))
    expand = jnp.repeat(jnp.eye(nh, dtype=f32), SSD_HEADDIM, axis=1)
    e3 = jnp.concatenate([expand, expand, expand], axis=0).astype(bf16)
    dskip = jnp.repeat(d, SSD_HEADDIM)[None, :]
    yf, yb = pl.pallas_call(
        functools.partial(_ssd_scan_kernel, ctx_chunks=ctx_chunks, lat_chunks=lat_chunks),
        grid=(bsz, ctx_chunks + lat_chunks),
        in_specs=dir_specs(False) + dir_specs(True) + [
            whole((SSD_CONV, SSD_CONV_DIM)), whole((1, SSD_CONV_DIM)),
            whole((2, nh)), whole((2, nh)), whole((2, nh, 1)), whole((2, nh, 1)),
            whole((3 * nh, SSD_D_INNER)), whole((1, SSD_D_INNER)),
        ],
        out_specs=[
            pl.BlockSpec((q, SSD_D_INNER), lambda b, j: (chunk_of(b, j, False), 0)),
            pl.BlockSpec((q, SSD_D_INNER), lambda b, j: (chunk_of(b, j, True), 0)),
        ],
        out_shape=[jax.ShapeDtypeStruct((n_tok, SSD_D_INNER), f32)] * 2,
        scratch_shapes=[pltpu.VMEM((2, SSD_GROUPS, SSD_STATE, SSD_GW), f32)],
        compiler_params=pltpu.CompilerParams(dimension_semantics=("arbitrary", "arbitrary"), vmem_limit_bytes=VMEM_LIMIT),
        name="ssd_scan",
    )(xbc, xbc, xbc, dt, dtt, xbc, xbc, xbc, dt, dtt,
      conv_w, conv_b.reshape(1, -1), dt_bias, a, dt_bias.reshape(2, nh, 1), a.reshape(2, nh, 1), e3, dskip)

    return pl.pallas_call(
        _ssd_out_kernel,
        grid=(n_tok // tm,),
        in_specs=[
            pl.BlockSpec((tm, SSD_D_INNER), lambda i: (i, 0)),
            pl.BlockSpec((tm, SSD_D_INNER), lambda i: (i, 0)),
            pl.BlockSpec((tm, SSD_D_INNER), lambda i: (i, 0)),
            pl.BlockSpec((tm, dm), lambda i: (i, 0)),
            pl.BlockSpec((1, 1, 6 * dm), lambda i: (mod_idx(i), 0, 0)),
            whole((1, SSD_D_INNER)), whole((SSD_D_INNER, dm)), whole((1, dm)), whole((1, dm)),
        ],
        out_specs=pl.BlockSpec((tm, dm), lambda i: (i, 0)),
        out_shape=jax.ShapeDtypeStruct((n_tok, dm), f32),
        compiler_params=params,
        name="ssd_out_ln",
    )(yf, yb, z, xa, mod3, norm_w.reshape(1, -1), w_out.astype(bf16), ln_g.reshape(1, dm), ln_b.reshape(1, dm))


def _modulation_kernel(c_ref, w_ref, b_ref, o_ref):
    cv = c_ref[...]
    act = cv * jax.nn.sigmoid(cv)
    o_ref[0] = jnp.dot(act, w_ref[0], precision=lax.Precision.HIGHEST,
                       preferred_element_type=jnp.float32) + b_ref[0]


def modulation_pallas(cond, mod_w, mod_b):
    depth, dm, n_out = mod_w.shape
    tn = dm
    return pl.pallas_call(
        _modulation_kernel,
        grid=(depth, n_out // tn),
        in_specs=[
            pl.BlockSpec((8, dm), lambda i, n: (0, 0)),
            pl.BlockSpec((1, dm, tn), lambda i, n: (i, 0, n)),
            pl.BlockSpec((1, 1, tn), lambda i, n: (i, 0, n)),
        ],
        out_specs=pl.BlockSpec((1, 8, tn), lambda i, n: (i, 0, n)),
        out_shape=jax.ShapeDtypeStruct((depth, 8, n_out), jnp.float32),
        compiler_params=pltpu.CompilerParams(dimension_semantics=("arbitrary", "arbitrary"), vmem_limit_bytes=VMEM_LIMIT),
        name="modulation",
    )(cond, mod_w, mod_b.reshape(depth, 1, n_out))


def kernel(x, c, ctx, c_ctx, mod_w, mod_b, ln_g, ln_b,
           s5_a_re, s5_a_im, s5_log_dt, s5_b_re, s5_b_im, s5_c_re, s5_c_im, s5_d, s5_w_gate, s5_w_val,
           ssd_w_in, ssd_conv_w, ssd_conv_b, ssd_dt_bias, ssd_a_log, ssd_d, ssd_norm_w, ssd_w_out,
           mla_w_down, mla_q_norm, mla_kv_norm, mla_w_uq, mla_w_uk, mla_w_uv, mla_w_o,
           peer_w_q, peer_subkeys, peer_u, peer_v):
    ROWS = x.shape[1] // GRID_W
    pos = grid_positions(ROWS)
    ctx_len = ctx.shape[1]
    bsz, seq_len, dm = x.shape
    n_lat, n_ctx = bsz * seq_len, bsz * ctx_len
    xa = jnp.concatenate([x.reshape(n_lat, dm), ctx.reshape(n_ctx, dm)], axis=0)
    cond = jnp.concatenate([c, c_ctx[None], jnp.zeros((7 - bsz, dm), x.dtype)], axis=0)
    mod_all = modulation_pallas(cond, mod_w, mod_b)
    for i in range(DEPTH):
        last = i == DEPTH - 1
        mod_tab = mod_all[i]
        mod3 = mod_tab.reshape(8, 1, 6 * dm)
        kind, j = i % N_MIXERS, i // N_MIXERS
        if kind == 0:
            prep = _s5_prep_fast(s5_a_re[j], s5_a_im[j], s5_log_dt[j], s5_b_re[j], s5_b_im[j], s5_c_re[j], s5_c_im[j], s5_d[j])
            ya = s5_ssm_pallas(xa, mod_tab, n_lat, seq_len, bsz, prep)
            if last:
                ya, xa = ya[:n_lat], xa[:n_lat]
            xa = glu_ln_pallas(ya, xa, mod3, n_lat, seq_len, bsz, s5_w_val[j], s5_w_gate[j], ln_g[i, 0], ln_b[i, 0])
        elif kind == 1:
            xa = ssd_ln_pallas(xa, mod3, n_lat, seq_len, bsz, ssd_w_in[j], ssd_conv_w[j], ssd_conv_b[j], ssd_dt_bias[j],
                               ssd_a_log[j], ssd_d[j], ssd_norm_w[j], ssd_w_out[j], ln_g[i, 0], ln_b[i, 0])
            if last:
                xa = xa[:n_lat]
        else:
            xa = mla_ln_pallas(xa, mod3, n_lat, seq_len, bsz, mla_w_down[j], mla_q_norm[j], mla_kv_norm[j],
                               mla_w_uq[j], mla_w_uk[j], mla_w_uv[j], mla_w_o[j], ln_g[i, 0], ln_b[i, 0])
            if last:
                xa = xa[:n_lat]
        xa = peer_ln_pallas(xa, mod3, n_lat, seq_len, bsz, peer_w_q[i], peer_subkeys[i], peer_u[i], peer_v[i],
                            ln_g[i, 1], ln_b[i, 1])
    return xa[:n_lat].reshape(bsz, seq_len, dm)
```

```python
import math
import functools
import jax
import jax.numpy as jnp
from jax import lax
import numpy as np
from jax.experimental import pallas as pl
from jax.experimental.pallas import tpu as pltpu

D_MODEL = 1024
BATCH = 4
SEQ = 4096
DEPTH = 4

GRID_W = 64
CTX_LEN = 256
N_MIXERS = 3
DEEPNORM_ALPHA = (2.0 * DEPTH) ** 0.25
LN_EPS = 1e-5
RMS_EPS = 1e-6
ROPE_BASE = 10000.0

S5_GROUP = 16
S5_GROUPS = D_MODEL // S5_GROUP
S5_STATE = 64

SSD_D_INNER = 2 * D_MODEL
SSD_HEADDIM = 64
SSD_HEADS = SSD_D_INNER // SSD_HEADDIM
SSD_GROUPS = 4
SSD_STATE = 128
SSD_CONV = 3
SSD_CHUNK = 128
SSD_CONV_DIM = SSD_D_INNER + 2 * SSD_GROUPS * SSD_STATE
SSD_IN_DIM = SSD_D_INNER + SSD_CONV_DIM + 2 * SSD_HEADS

MLA_HEADS = 16
MLA_Q_RANK = 256
MLA_KV_RANK = 128
MLA_NOPE = 64
MLA_ROPE = 32
MLA_V = 64
MLA_BLOCK = 128

PEER_HEADS = 8
PEER_KEYS = 128
PEER_EXPERTS = PEER_KEYS * PEER_KEYS
PEER_QDIM = 256
PEER_TOPK = 16
PEER_BLOCK = 128


def layer_norm(x, g, b):
    xf = x.astype(jnp.float32)
    mu = jnp.mean(xf, axis=-1, keepdims=True)
    var = jnp.mean(jnp.square(xf - mu), axis=-1, keepdims=True)
    return ((xf - mu) * lax.rsqrt(var + LN_EPS)).astype(x.dtype) * g + b


def _ln_kernel(x_ref, g_ref, b_ref, o_ref):
    xf = x_ref[...]
    mu = jnp.mean(xf, axis=-1, keepdims=True)
    xc = xf - mu
    var = jnp.mean(xc * xc, axis=-1, keepdims=True)
    o_ref[...] = xc * lax.rsqrt(var + LN_EPS) * g_ref[...] + b_ref[...]


def layer_norm_pallas(x, g, b):
    shp = x.shape
    x2 = x.reshape(-1, shp[-1])
    n, d = x2.shape
    tb = 512
    out = pl.pallas_call(
        _ln_kernel,
        grid=(n // tb,),
        in_specs=[pl.BlockSpec((tb, d), lambda i: (i, 0)),
                  pl.BlockSpec((1, d), lambda i: (0, 0)),
                  pl.BlockSpec((1, d), lambda i: (0, 0))],
        out_specs=pl.BlockSpec((tb, d), lambda i: (i, 0)),
        out_shape=jax.ShapeDtypeStruct((n, d), jnp.float32),
        name="final_ln",
    )(x2, g.reshape(1, d), b.reshape(1, d))
    return out.reshape(shp)


def rms_norm(x, g):
    xf = x.astype(jnp.float32)
    return (xf * lax.rsqrt(jnp.mean(jnp.square(xf), axis=-1, keepdims=True) + RMS_EPS)).astype(x.dtype) * g


def modulate(x, shift, scale):
    return x * (1.0 + scale) + shift


def grid_positions(rows):
    row = jnp.repeat(jnp.arange(rows, dtype=jnp.float32), GRID_W)
    col = jnp.tile(jnp.arange(GRID_W, dtype=jnp.float32), rows)
    return row, col


def rope_axial(x, row, col):
    half = x.shape[-1] // 2
    quarter = half // 2
    freqs = ROPE_BASE ** (-jnp.arange(quarter, dtype=jnp.float32) / quarter)

    def rot(xp, pos):
        ang = pos[:, None] * freqs
        cos = jnp.cos(ang)[None, :, None, :].astype(x.dtype)
        sin = jnp.sin(ang)[None, :, None, :].astype(x.dtype)
        x1, x2 = xp[..., :quarter], xp[..., quarter:]
        return jnp.concatenate([x1 * cos - x2 * sin, x2 * cos + x1 * sin], axis=-1)

    return jnp.concatenate([rot(x[..., :half], row), rot(x[..., half:], col)], axis=-1)


def _lin_rec(left, right):
    a1, b1 = left
    a2, b2 = right
    return a1 * a2, a2 * b1 + b2


def s5_scan(u, a_bar, b_bar, c_mat, init, reverse):
    bu = jnp.einsum('lgc,gpc->lgp', u.astype(jnp.float32), b_bar)
    a = jnp.broadcast_to(a_bar, bu.shape)
    a_cum, s = lax.associative_scan(_lin_rec, (a, bu), reverse=reverse, axis=0)
    s = s + a_cum * init
    y = jnp.einsum('lgp,gcp->lgc', s, c_mat).real
    final = s[0] if reverse else s[-1]
    return y, final


def s5_direction(u_ctx, u_lat, a_re, a_im, log_dt, b_re, b_im, c_re, c_im, reverse):
    lam = lax.complex(a_re.astype(jnp.float32), a_im.astype(jnp.float32))
    a_bar = jnp.exp(lam * jnp.exp(log_dt.astype(jnp.float32))[:, None])
    b_mat = lax.complex(b_re.astype(jnp.float32), b_im.astype(jnp.float32))
    b_bar = ((a_bar - 1.0) / lam)[..., None] * b_mat
    c_mat = lax.complex(c_re.astype(jnp.float32), c_im.astype(jnp.float32))

    def per_sample(args):
        uc, ul = args
        yc, sc = s5_scan(uc, a_bar, b_bar, c_mat, jnp.zeros_like(a_bar), reverse)
        yl, _ = s5_scan(ul, a_bar, b_bar, c_mat, sc, reverse)
        return yc, yl

    return lax.map(per_sample, (u_ctx, u_lat))


def s5_mixer(h, hc, a_re, a_im, log_dt, b_re, b_im, c_re, c_im, d, w_gate, w_val):
    bsz, seq_len, _ = h.shape
    ctx_len = hc.shape[1]
    u = h.reshape(bsz, seq_len, S5_GROUPS, S5_GROUP)
    uc = hc.reshape(bsz, ctx_len, S5_GROUPS, S5_GROUP)
    y_l = d * h
    y_c = d * hc
    for direction in range(2):
        yc, yl = s5_direction(uc, u, a_re[direction], a_im[direction], log_dt[direction],
                              b_re[direction], b_im[direction], c_re[direction], c_im[direction],
                              direction == 1)
        y_l = y_l + yl.reshape(bsz, seq_len, D_MODEL).astype(h.dtype)
        y_c = y_c + yc.reshape(bsz, ctx_len, D_MODEL).astype(h.dtype)

    def glu(y):
        g = jax.nn.gelu(y)
        return (g @ w_val) * jax.nn.sigmoid(g @ w_gate)

    return glu(y_l), glu(y_c)


def depthwise_conv_centred(x, w, b):
    k_w = w.shape[0]
    pad = k_w // 2
    seq_len = x.shape[1]
    xp = jnp.pad(x, ((0, 0), (pad, pad), (0, 0)))
    out = b
    for k in range(k_w):
        out = out + xp[:, k:k + seq_len] * w[k]
    return out


def ssd_scan(x, dt, a, bm, cm, init):
    bsz, seq_len, n_heads, p_dim = x.shape
    n_grp, n_st = bm.shape[2], bm.shape[3]
    hg = n_heads // n_grp
    q_len = SSD_CHUNK
    nc = seq_len // q_len
    xc = x.reshape(bsz, nc, q_len, n_grp, hg, p_dim)
    dtc = dt.reshape(bsz, nc, q_len, n_grp, hg).astype(jnp.float32)
    bc = bm.reshape(bsz, nc, q_len, n_grp, n_st)
    cc = cm.reshape(bsz, nc, q_len, n_grp, n_st)
    a_cum = jnp.cumsum(dtc * a.reshape(n_grp, hg), axis=2)
    xdt = xc * dtc[..., None].astype(x.dtype)
    tri = jnp.tril(jnp.ones((q_len, q_len), dtype=bool))
    seg = a_cum[:, :, :, None] - a_cum[:, :, None]
    decay = jnp.exp(jnp.where(tri[:, :, None, None], seg, -jnp.inf)).astype(x.dtype)
    cb = jnp.einsum('bcqgn,bcsgn->bcgqs', cc, bc)
    y_diag = jnp.einsum('bcgqs,bcqsgh,bcsghp->bcqghp', cb, decay, xdt)
    decay_states = jnp.exp(a_cum[:, :, -1:] - a_cum).astype(x.dtype)
    states = jnp.einsum('bcsgn,bcsgh,bcsghp->bcghpn', bc, decay_states, xdt)
    chunk_decay = jnp.exp(a_cum[:, :, -1]).astype(x.dtype)

    def step(carry, inp):
        dec, st = inp
        return carry * dec[..., None, None] + st, carry

    final, prev = lax.scan(step, init, (jnp.moveaxis(chunk_decay, 1, 0), jnp.moveaxis(states, 1, 0)))
    y_off = jnp.einsum('bcqgn,cbghpn,bcqgh->bcqghp', cc, prev, jnp.exp(a_cum).astype(x.dtype))
    return (y_diag + y_off).reshape(bsz, seq_len, n_heads, p_dim), final


def ssd_mixer(h, hc, w_in, conv_w, conv_b, dt_bias, a_log, d, norm_w, w_out):
    a = -jnp.exp(a_log.astype(jnp.float32))

    def flip(t):
        return jnp.flip(t, axis=1)

    def bidir(t, inits):
        bsz, seq_len = t.shape[0], t.shape[1]
        z, xbc, dt = jnp.split(t @ w_in, [SSD_D_INNER, SSD_D_INNER + SSD_CONV_DIM], axis=-1)
        xbc = jax.nn.silu(depthwise_conv_centred(xbc, conv_w, conv_b))
        xs, bm, cm = jnp.split(xbc, [SSD_D_INNER, SSD_D_INNER + SSD_GROUPS * SSD_STATE], axis=-1)
        xs = xs.reshape(bsz, seq_len, SSD_HEADS, SSD_HEADDIM)
        bm = bm.reshape(bsz, seq_len, SSD_GROUPS, SSD_STATE)
        cm = cm.reshape(bsz, seq_len, SSD_GROUPS, SSD_STATE)
        dt = jax.nn.softplus(dt.reshape(bsz, seq_len, 2, SSD_HEADS) + dt_bias)
        y_f, s_f = ssd_scan(xs, dt[:, :, 0], a[0], bm, cm, inits[0])
        y_b, s_b = ssd_scan(flip(xs), flip(dt[:, :, 1]), a[1], flip(bm), flip(cm), inits[1])
        y = y_f + flip(y_b) + d[:, None] * xs
        y = rms_norm(y.reshape(z.shape) * jax.nn.silu(z), norm_w)
        return y @ w_out, (s_f, s_b)

    zero = jnp.zeros((h.shape[0], SSD_GROUPS, SSD_HEADS // SSD_GROUPS, SSD_HEADDIM, SSD_STATE), h.dtype)
    out_c, states_c = bidir(hc, (zero, zero))
    out_l, _ = bidir(h, states_c)
    return out_l, out_c


def mla_project(t, w_down, q_norm, kv_norm, w_uq, w_uk, w_uv, pos):
    cq, ckv, kr = jnp.split(t @ w_down, [MLA_Q_RANK, MLA_Q_RANK + MLA_KV_RANK], axis=-1)
    q = jnp.einsum('btr,rhd->bthd', rms_norm(cq, q_norm), w_uq)
    ckv = rms_norm(ckv, kv_norm)
    k_nope = jnp.einsum('btr,rhd->bthd', ckv, w_uk)
    v = jnp.einsum('btr,rhd->bthd', ckv, w_uv)
    q_nope, q_rope = q[..., :MLA_NOPE], q[..., MLA_NOPE:]
    kr = kr[:, :, None, :]
    if pos is not None:
        q_rope = rope_axial(q_rope, pos[0], pos[1])
        kr = rope_axial(kr, pos[0], pos[1])
    k = jnp.concatenate([k_nope, jnp.broadcast_to(kr, k_nope.shape[:3] + (MLA_ROPE,))], axis=-1)
    q = jnp.concatenate([q_nope, q_rope], axis=-1)
    return q, k, v


def attend(q, k, v):
    s = jnp.einsum('bqhd,bkhd->bhqk', q, k).astype(jnp.float32) * (MLA_NOPE + MLA_ROPE) ** -0.5
    p = jax.nn.softmax(s, axis=-1).astype(v.dtype)
    return jnp.einsum('bhqk,bkhd->bqhd', p, v)


def mla_mixer(h, hc, pos, w_down, q_norm, kv_norm, w_uq, w_uk, w_uv, w_o):
    bsz, seq_len, _ = h.shape
    qc, kc, vc = mla_project(hc, w_down, q_norm, kv_norm, w_uq, w_uk, w_uv, None)
    ql, kl, vl = mla_project(h, w_down, q_norm, kv_norm, w_uq, w_uk, w_uv, pos)
    out_c = attend(qc, kc, vc).reshape(bsz, hc.shape[1], MLA_HEADS * MLA_V)
    k_all = jnp.concatenate([kc, kl], axis=1)
    v_all = jnp.concatenate([vc, vl], axis=1)
    nb = seq_len // MLA_BLOCK
    qb = jnp.moveaxis(ql.reshape(bsz, nb, MLA_BLOCK, MLA_HEADS, MLA_NOPE + MLA_ROPE), 1, 0)
    out_l = lax.map(lambda qblk: attend(qblk, k_all, v_all), qb)
    out_l = jnp.moveaxis(out_l, 0, 1).reshape(bsz, seq_len, MLA_HEADS * MLA_V)
    return out_l @ w_o, out_c @ w_o


def peer_ffn(h, w_q, subkeys, u_tab, v_tab):
    bsz, seq_len, dm = h.shape
    half = PEER_QDIM // 2
    q = jnp.einsum('btd,dhk->bthk', h, w_q)
    s1 = jnp.einsum('bthk,hnk->bthn', q[..., :half], subkeys[:, 0])
    s2 = jnp.einsum('bthk,hnk->bthn', q[..., half:], subkeys[:, 1])
    v1, i1 = lax.top_k(s1, PEER_TOPK)
    v2, i2 = lax.top_k(s2, PEER_TOPK)
    n_cand = PEER_TOPK * PEER_TOPK
    cand = (v1[..., :, None] + v2[..., None, :]).reshape(bsz, seq_len, PEER_HEADS, n_cand)
    cidx = (i1[..., :, None] * PEER_KEYS + i2[..., None, :]).reshape(bsz, seq_len, PEER_HEADS, n_cand)
    best, sel = lax.top_k(cand, PEER_TOPK)
    eidx = jnp.take_along_axis(cidx, sel, axis=-1)
    gate = jax.nn.softmax(best.astype(jnp.float32), axis=-1).astype(h.dtype)
    n_blk = (bsz * seq_len) // PEER_BLOCK
    n_sel = PEER_HEADS * PEER_TOPK
    hb = h.reshape(n_blk, PEER_BLOCK, dm)
    ib = eidx.reshape(n_blk, PEER_BLOCK, n_sel)
    gb = gate.reshape(n_blk, PEER_BLOCK, n_sel)

    def block(args):
        hk, ik, gk = args
        u = jnp.take(u_tab, ik, axis=0)
        act = jax.nn.gelu(jnp.einsum('td,ted->te', hk, u))
        v = jnp.take(v_tab, ik, axis=0)
        return jnp.einsum('te,ted->td', gk * act, v)

    return lax.map(block, (hb, ib, gb)).reshape(bsz, seq_len, dm)


PEER_TM = 512
PEER_EB = 1024
PEER_ROWS = 16
PEER_LANES = 128
PEER_ACT_ROWS = 512
VMEM_LIMIT = 56 * 1024 * 1024
NEG_INF = float("-inf")
GELU_C0 = math.sqrt(2.0 / math.pi)
GELU_C1 = 0.044715 * GELU_C0


def _top16_sorted(s):
    n = PEER_TOPK
    m = s.shape[0] // 8
    x = [s[8 * r:8 * r + 8, :] for r in range(m)]

    def exchange(i, l, descending):
        hi, lo = jnp.maximum(x[i], x[l]), jnp.minimum(x[i], x[l])
        x[i], x[l] = (hi, lo) if descending else (lo, hi)

    def merge_bitonic(size):
        j = size // 2
        while j >= 1:
            for i in range(size):
                if i ^ j > i:
                    exchange(i, i ^ j, True)
            j //= 2

    k = 2
    while k < m:
        j = k // 2
        while j >= 1:
            for i in range(m):
                if i ^ j > i:
                    exchange(i, i ^ j, (i & k) == 0)
            j //= 2
        k *= 2
    merge_bitonic(m)
    shifts = [4, 2, 1]
    if m < n:
        first = shifts.pop(0)
        x = x + [pltpu.roll(x[m - 1 - r], first, axis=0) for r in range(m)]
        merge_bitonic(n)
    for shift in shifts:
        other = [pltpu.roll(x[n - 1 - r], shift, axis=0) for r in range(n)]
        for r in range(n):
            x[r] = jnp.maximum(x[r], other[r])
        merge_bitonic(n)
    return [x[r][0:1, :] for r in range(n)]


def _dot3(a_hi, a_lo, b_hi, b_lo):
    f32 = jnp.float32
    return (jnp.dot(a_hi, b_hi, preferred_element_type=f32)
            + jnp.dot(a_hi, b_lo, preferred_element_type=f32)
            + jnp.dot(a_lo, b_hi, preferred_element_type=f32))


def _split_bf16(v):
    hi = v.astype(jnp.bfloat16)
    lo = (v - hi.astype(jnp.float32)).astype(jnp.bfloat16)
    return hi, lo


def _peer_kernel(x_ref, mod_ref, wq_ref, skh_ref, skl_ref, u_ref, vt_ref, vtl_ref, lng_ref, lnb_ref,
                 o_ref,
                 ht_ref, s_ref, n1_ref, e1_ref, r2_ref, e2_ref, v1_ref, v2_ref, cand_ref,
                 act_ref, gact_ref, acc_ref):
    f32 = jnp.float32
    bf16 = jnp.bfloat16
    eb = pl.program_id(1)
    n_eb = pl.num_programs(1)
    dm = x_ref.shape[1]
    a_per = PEER_EB // PEER_KEYS
    half = PEER_QDIM // 2

    @pl.when(eb == 0)
    def _prologue():
        mod = mod_ref[0]
        h = x_ref[...] * (1.0 + mod[:, 4 * dm:5 * dm]) + mod[:, 3 * dm:4 * dm]
        ht = h.T.astype(bf16)
        ht_ref[...] = ht
        qt = jnp.dot(wq_ref[...], ht, preferred_element_type=f32)
        cand_ref[...] = jnp.full(cand_ref.shape, NEG_INF, f32)
        gact_ref[...] = jnp.zeros(gact_ref.shape, bf16)
        for hd in range(PEER_HEADS):
            q1_hi, q1_lo = _split_bf16(qt[hd * PEER_QDIM: hd * PEER_QDIM + half])
            q2_hi, q2_lo = _split_bf16(qt[hd * PEER_QDIM + half: (hd + 1) * PEER_QDIM])
            s_ref[0] = _dot3(skh_ref[hd, 0], skl_ref[hd, 0], q1_hi, q1_lo)
            s_ref[1] = _dot3(skh_ref[hd, 1], skl_ref[hd, 1], q2_hi, q2_lo)
            for lg in range(0, x_ref.shape[0], PEER_LANES):
                lanes = slice(lg, lg + PEER_LANES)
                s1, s2 = s_ref[0, :, lanes], s_ref[1, :, lanes]
                top1 = _top16_sorted(s1)
                top2 = _top16_sorted(s2)
                for k in range(PEER_TOPK):
                    v1_ref[k:k + 1, lanes] = top1[k]
                    v2_ref[k:k + 1, lanes] = top2[k]
                off = 0
                for i in range(PEER_TOPK):
                    cnt = PEER_TOPK // (i + 1)
                    cand_ref[off:off + cnt, lanes] = v1_ref[i:i + 1, lanes] + v2_ref[0:cnt, lanes]
                    off += cnt
                best = _top16_sorted(cand_ref[:, lanes])
                z = jnp.zeros_like(best[0])
                for k in range(PEER_TOPK):
                    z = z + jnp.exp(best[k] - best[0])
                tau = best[PEER_TOPK - 1]
                n1 = jnp.zeros(s1.shape, f32)
                for k in range(PEER_TOPK):
                    n1 = jnp.where(s1 + top2[k] >= tau, k + 1.0, n1)
                n1_ref[hd, :, lanes] = n1
                e1_ref[hd, :, lanes] = jnp.exp(s1 - top1[0])
                rank2 = jnp.zeros(s2.shape, f32)
                for k in range(PEER_TOPK):
                    rank2 = jnp.where(top2[k] > s2, k + 1.0, rank2)
                r2_ref[hd, :, lanes] = rank2.astype(bf16)
                e2_ref[hd, :, lanes] = (jnp.exp(s2 - top2[0]) * (0.5 / z)).astype(bf16)
        acc_ref[...] = jnp.zeros(acc_ref.shape, f32)

    par = eb % 2
    for r0 in range(0, PEER_EB, PEER_ACT_ROWS):
        act_ref[r0:r0 + PEER_ACT_ROWS, :] = jnp.dot(u_ref[r0:r0 + PEER_ACT_ROWS, :], ht_ref[...],
                                                    preferred_element_type=f32)
    acc_ref[...] += jnp.dot(vt_ref[0], gact_ref[1 - par], preferred_element_type=f32)
    a_base = pl.multiple_of(eb * a_per, a_per)
    blk = (PEER_ROWS, x_ref.shape[0])
    n_slab = PEER_KEYS // PEER_ROWS
    for a in range(a_per):
        g = [None] * n_slab
        for hd in range(PEER_HEADS):
            n1a = jnp.broadcast_to(n1_ref[hd, pl.ds(a_base, a_per), :][a:a + 1, :].astype(bf16), blk)
            e1a = jnp.broadcast_to(e1_ref[hd, pl.ds(a_base, a_per), :][a:a + 1, :].astype(bf16), blk)
            for i in range(n_slab):
                rs = slice(i * PEER_ROWS, (i + 1) * PEER_ROWS)
                w = jnp.where(r2_ref[hd, rs, :] < n1a, e2_ref[hd, rs, :], jnp.zeros(blk, bf16)) * e1a
                g[i] = w if g[i] is None else g[i] + w
        for i in range(n_slab):
            rows = slice(a * PEER_KEYS + i * PEER_ROWS, a * PEER_KEYS + (i + 1) * PEER_ROWS)
            x = act_ref[rows, :]
            t = jnp.tanh(x * (GELU_C0 + GELU_C1 * (x * x)))
            gact_ref[par, rows, :] = g[i] * (x + x * t).astype(bf16)

    @pl.when(eb == n_eb - 1)
    def _epilogue():
        mod = mod_ref[0]
        ffn_t = acc_ref[...] + jnp.dot(vtl_ref[0], gact_ref[par], preferred_element_type=f32)
        y = DEEPNORM_ALPHA * x_ref[...] + mod[:, 5 * dm:6 * dm] * ffn_t.T
        mu = jnp.mean(y, axis=-1, keepdims=True)
        yc = y - mu
        var = jnp.mean(yc * yc, axis=-1, keepdims=True)
        o_ref[...] = yc * lax.rsqrt(var + LN_EPS) * lng_ref[...] + lnb_ref[...]


def _mod_row_index(i, tm, n_lat, seq, bsz):
    return jnp.where(i * tm < n_lat, (i * tm) // seq, bsz)


def peer_ln_pallas(xa, mod3, n_lat, seq, bsz, w_q, subkeys, u_tab, v_tab, ln_g, ln_b):
    n_tok, dm = xa.shape
    tm = PEER_TM
    f32, bf16 = jnp.float32, jnp.bfloat16
    wq_t = w_q.reshape(dm, PEER_HEADS * PEER_QDIM).T
    wq_bf = wq_t.astype(bf16)
    sk_hi = subkeys.astype(bf16)
    sk_lo = (subkeys - sk_hi.astype(f32)).astype(bf16)
    u_bf = u_tab.astype(bf16)
    n_eb = PEER_EXPERTS // PEER_EB
    vt_bf = v_tab.reshape(n_eb, PEER_EB, dm).transpose(0, 2, 1).astype(bf16)
    qd = PEER_HEADS * PEER_QDIM
    n_cand = sum(PEER_TOPK // (i + 1) for i in range(PEER_TOPK))
    n_cand_pad = 64
    assert n_cand <= n_cand_pad
    mod_idx = functools.partial(_mod_row_index, tm=tm, n_lat=n_lat, seq=seq, bsz=bsz)
    return pl.pallas_call(
        _peer_kernel,
        grid=(n_tok // tm, n_eb),
        in_specs=[
            pl.BlockSpec((tm, dm), lambda i, e: (i, 0)),
            pl.BlockSpec((1, 1, 6 * dm), lambda i, e: (mod_idx(i), 0, 0)),
            pl.BlockSpec((qd, dm), lambda i, e: (0, 0)),
            pl.BlockSpec((PEER_HEADS, 2, PEER_KEYS, PEER_QDIM // 2), lambda i, e: (0, 0, 0, 0)),
            pl.BlockSpec((PEER_HEADS, 2, PEER_KEYS, PEER_QDIM // 2), lambda i, e: (0, 0, 0, 0)),
            pl.BlockSpec((PEER_EB, dm), lambda i, e: (e, 0)),
            pl.BlockSpec((1, dm, PEER_EB), lambda i, e: (jnp.maximum(e - 1, 0), 0, 0)),
            pl.BlockSpec((1, dm, PEER_EB), lambda i, e: (n_eb - 1, 0, 0)),
            pl.BlockSpec((1, dm), lambda i, e: (0, 0)),
            pl.BlockSpec((1, dm), lambda i, e: (0, 0)),
        ],
        out_specs=pl.BlockSpec((tm, dm), lambda i, e: (i, 0)),
        out_shape=jax.ShapeDtypeStruct((n_tok, dm), f32),
        scratch_shapes=[
            pltpu.VMEM((dm, tm), bf16),
            pltpu.VMEM((2, PEER_KEYS, tm), f32),
            pltpu.VMEM((PEER_HEADS, PEER_KEYS, tm), f32),
            pltpu.VMEM((PEER_HEADS, PEER_KEYS, tm), f32),
            pltpu.VMEM((PEER_HEADS, PEER_KEYS, tm), bf16),
            pltpu.VMEM((PEER_HEADS, PEER_KEYS, tm), bf16),
            pltpu.VMEM((PEER_TOPK, tm), f32),
            pltpu.VMEM((PEER_TOPK, tm), f32),
            pltpu.VMEM((n_cand_pad, tm), f32),
            pltpu.VMEM((PEER_EB, tm), f32),
            pltpu.VMEM((2, PEER_EB, tm), bf16),
            pltpu.VMEM((dm, tm), f32),
        ],
        compiler_params=pltpu.CompilerParams(
            dimension_semantics=("arbitrary", "arbitrary"),
            vmem_limit_bytes=VMEM_LIMIT),
        name="peer_ln",
    )(xa, mod3, wq_bf, sk_hi, sk_lo, u_bf, vt_bf, vt_bf, ln_g.reshape(1, dm), ln_b.reshape(1, dm))


S5_CHUNK = 16
S5_LEVELS = 9


def _s5_prep(a_re, a_im, log_dt, b_re, b_im, c_re, c_im, d):
    f32, bf16 = jnp.float32, jnp.bfloat16
    hp = lax.Precision.HIGHEST
    n_g, n_p, q = S5_GROUPS, S5_STATE, S5_CHUNK
    dt = jnp.exp(log_dt.astype(f32))[..., None]

    def apow(n):
        mag = jnp.exp(a_re * dt * n)
        ang = a_im * dt * n
        return mag * jnp.cos(ang), mag * jnp.sin(ang)

    ar1, ai1 = apow(1.0)
    den = a_re * a_re + a_im * a_im
    nr, ni = ar1 - 1.0, ai1
    cr = (nr * a_re + ni * a_im) / den
    ci = (ni * a_re - nr * a_im) / den
    bbr = cr[..., None] * b_re - ci[..., None] * b_im
    bbi = cr[..., None] * b_im + ci[..., None] * b_re
    lag = jnp.arange(q + 1, dtype=f32)[:, None, None, None]
    pr, pi = apow(lag)
    mr = pr[..., None] * bbr - pi[..., None] * bbi
    mi = pr[..., None] * bbi + pi[..., None] * bbr
    kmat = (jnp.einsum('dgcp,ndgpk->ndgck', c_re, mr, precision=hp)
            - jnp.einsum('dgcp,ndgpk->ndgck', c_im, mi, precision=hp))
    r_idx = jnp.arange(q)[:, None]
    t_idx = jnp.arange(q)[None, :]

    def toeplitz(kd, lagm):
        blk = kd[jnp.clip(lagm, 0, q)]
        blk = jnp.where((lagm >= 0)[:, :, None, None, None], blk, 0.0)
        return blk.transpose(2, 0, 4, 1, 3).reshape(n_g, q * S5_GROUP, q * S5_GROUP)

    eye = jnp.eye(q * S5_GROUP, dtype=f32)
    dvec = jnp.tile(d.reshape(n_g, 1, S5_GROUP), (1, q, 1)).reshape(n_g, q * S5_GROUP)
    tsum = toeplitz(kmat[:, 0], t_idx - r_idx) + toeplitz(kmat[:, 1], r_idx - t_idx) + eye[None] * dvec[:, None, :]

    def w_in(direction, exps):
        wr = mr[exps, direction].transpose(1, 0, 3, 2).reshape(n_g, q * S5_GROUP, n_p)
        wi = mi[exps, direction].transpose(1, 0, 3, 2).reshape(n_g, q * S5_GROUP, n_p)
        return wr, wi

    def w_out(direction, exps):
        pre, pim = pr[exps, direction], pi[exps, direction]
        cre, cim = c_re[direction], c_im[direction]
        wre = cre[None] * pre[:, :, None, :] - cim[None] * pim[:, :, None, :]
        wim = -(cre[None] * pim[:, :, None, :] + cim[None] * pre[:, :, None, :])
        return (wre.transpose(1, 3, 0, 2).reshape(n_g, n_p, q * S5_GROUP),
                wim.transpose(1, 3, 0, 2).reshape(n_g, n_p, q * S5_GROUP))

    steps = jnp.arange(q)
    win = [w_in(0, q - 1 - steps), w_in(1, steps)]
    wout = [w_out(0, steps + 1), w_out(1, q - steps)]
    zc = jnp.zeros((n_g, q * S5_GROUP, n_p), f32)
    zr = jnp.zeros((n_g, n_p, q * S5_GROUP), f32)
    win_p, wout_p = [], []
    for direction in range(2):
        wr, wi = win[direction]
        even = jnp.concatenate([wr, zc, wi, zc], axis=2)
        odd = jnp.concatenate([zc, wr, zc, wi], axis=2)
        is_odd = (jnp.arange(n_g) % 2 == 1)[:, None, None]
        win_p.append(jnp.where(is_odd, odd, even))
        vr, vi = wout[direction]
        even = jnp.concatenate([vr, zr, vi, zr], axis=1)
        odd = jnp.concatenate([zr, vr, zr, vi], axis=1)
        wout_p.append(jnp.where(is_odd, odd, even))
    win_p = jnp.stack(win_p, axis=1).reshape(n_g // 2, 2, 2, 4 * n_p, 4 * n_p).transpose(0, 2, 1, 3, 4)
    wout_p = jnp.stack(wout_p, axis=1).reshape(n_g // 2, 2, 2, 4 * n_p, 4 * n_p).transpose(0, 2, 1, 3, 4)
    lvl = (q * 2.0 ** jnp.arange(S5_LEVELS, dtype=f32))[:, None, None, None]
    lr, li = apow(lvl)
    pw = jnp.stack([lr, li], axis=1)
    pw = pw.transpose(3, 2, 0, 1, 4).reshape(n_g // 2, 2, 2, S5_LEVELS, 2, n_p)
    pw = pw.transpose(0, 2, 3, 4, 1, 5).reshape(n_g // 2, 2, 2 * S5_LEVELS, 2 * n_p)
    return tsum.astype(bf16).reshape(n_g // 2, 2, q * S5_GROUP, q * S5_GROUP), win_p.astype(bf16), wout_p.astype(bf16), pw


def _s5_prep_fast(a_re, a_im, log_dt, b_re, b_im, c_re, c_im, d):
    f32, bf16 = jnp.float32, jnp.bfloat16
    hp = lax.Precision.HIGHEST
    n_g, n_p, q = S5_GROUPS, S5_STATE, S5_CHUNK
    width = q * S5_GROUP
    dt = jnp.exp(log_dt.astype(f32))[..., None]

    def apow(n):
        mag = jnp.exp(a_re * dt * n)
        ang = a_im * dt * n
        return mag * jnp.cos(ang), mag * jnp.sin(ang)

    def cmul(xr, xi, yr, yi):
        return xr * yr - xi * yi, xr * yi + xi * yr

    ar1, ai1 = apow(1.0)
    den = a_re * a_re + a_im * a_im
    nr, ni = ar1 - 1.0, ai1
    cr = (nr * a_re + ni * a_im) / den
    ci = (ni * a_re - nr * a_im) / den
    bbr = cr[..., None] * b_re - ci[..., None] * b_im
    bbi = cr[..., None] * b_im + ci[..., None] * b_re
    steps = jnp.arange(q, dtype=f32)

    def powers(direction, sign):
        mag = jnp.exp(a_re[direction][:, None, :] * dt[direction][:, None, :] * (sign * steps)[None, :, None])
        ang = a_im[direction][:, None, :] * dt[direction][:, None, :] * (sign * steps)[None, :, None]
        return mag * jnp.cos(ang), mag * jnp.sin(ang)

    def factors(direction):
        lsign = -1.0 if direction == 0 else 1.0
        pr_l, pi_l = powers(direction, lsign)
        pr_r, pi_r = powers(direction, -lsign)
        bt_r = bbr[direction].transpose(0, 2, 1)[:, None, :, :]
        bt_i = bbi[direction].transpose(0, 2, 1)[:, None, :, :]
        lt_r, lt_i = cmul(bt_r, bt_i, pr_l[:, :, None, :], pi_l[:, :, None, :])
        ct_r = c_re[direction].transpose(0, 2, 1)[:, :, None, :]
        ct_i = c_im[direction].transpose(0, 2, 1)[:, :, None, :]
        r_r, r_i = cmul(ct_r, ct_i, pr_r.transpose(0, 2, 1)[:, :, :, None], pi_r.transpose(0, 2, 1)[:, :, :, None])
        return (lt_r.reshape(n_g, width, n_p), lt_i.reshape(n_g, width, n_p),
                r_r.reshape(n_g, n_p, width), r_i.reshape(n_g, n_p, width))

    row_t = jnp.arange(width)[:, None] // S5_GROUP
    col_t = jnp.arange(width)[None, :] // S5_GROUP
    tsum = jnp.eye(width, dtype=f32)[None] * jnp.tile(d.reshape(n_g, 1, S5_GROUP), (1, q, 1)).reshape(n_g, 1, width)
    win, wout = [], []
    for direction in range(2):
        lt_r, lt_i, r_r, r_i = factors(direction)
        prod = (jnp.einsum('gxp,gpy->gxy', lt_r, r_r, precision=hp)
                - jnp.einsum('gxp,gpy->gxy', lt_i, r_i, precision=hp))
        mask = (col_t >= row_t) if direction == 0 else (row_t >= col_t)
        tsum = tsum + jnp.where(mask[None], prod, 0.0)
        if direction == 0:
            in_r, in_i = apow(q - 1.0)
            out_r, out_i = ar1, ai1
            w_r, w_i = cmul(lt_r, lt_i, in_r[0][:, None, :], in_i[0][:, None, :])
        else:
            out_r, out_i = apow(float(q))
            w_r, w_i = lt_r, lt_i
        win.append((w_r, w_i))
        o_r, o_i = cmul(r_r, r_i, out_r[direction][:, :, None], out_i[direction][:, :, None])
        wout.append((o_r, -o_i))
    zc = jnp.zeros((n_g, width, n_p), f32)
    zr = jnp.zeros((n_g, n_p, width), f32)
    win_p, wout_p = [], []
    for direction in range(2):
        wr, wi = win[direction]
        even = jnp.concatenate([wr, zc, wi, zc], axis=2)
        odd = jnp.concatenate([zc, wr, zc, wi], axis=2)
        is_odd = (jnp.arange(n_g) % 2 == 1)[:, None, None]
        win_p.append(jnp.where(is_odd, odd, even))
        vr, vi = wout[direction]
        even = jnp.concatenate([vr, zr, vi, zr], axis=1)
        odd = jnp.concatenate([zr, vr, zr, vi], axis=1)
        wout_p.append(jnp.where(is_odd, odd, even))
    win_p = jnp.stack(win_p, axis=1).reshape(n_g // 2, 2, 2, 4 * n_p, 4 * n_p).transpose(0, 2, 1, 3, 4)
    wout_p = jnp.stack(wout_p, axis=1).reshape(n_g // 2, 2, 2, 4 * n_p, 4 * n_p).transpose(0, 2, 1, 3, 4)
    lvl = (q * 2.0 ** jnp.arange(S5_LEVELS, dtype=f32))[:, None, None, None]
    lr, li = apow(lvl)
    pw = jnp.stack([lr, li], axis=1)
    pw = pw.transpose(3, 2, 0, 1, 4).reshape(n_g // 2, 2, 2, S5_LEVELS, 2, n_p)
    pw = pw.transpose(0, 2, 3, 4, 1, 5).reshape(n_g // 2, 2, 2 * S5_LEVELS, 2 * n_p)
    return tsum.astype(bf16).reshape(n_g // 2, 2, width, width), win_p.astype(bf16), wout_p.astype(bf16), pw


def _shift_rows(v, r, up):
    n = v.shape[0]
    row = lax.broadcasted_iota(jnp.int32, v.shape, 0)
    if up:
        return jnp.where(row < n - r, pltpu.roll(v, n - r, axis=0), 0.0)
    return jnp.where(row >= r, pltpu.roll(v, r, axis=0), 0.0)


def _chunk_scan(re, im, pw, rows_per_chunk, up):
    n_chunks = re.shape[0] // rows_per_chunk
    level, s = 0, 1
    while s < n_chunks:
        ar, ai = pw[2 * level:2 * level + 1, :], pw[2 * level + 1:2 * level + 2, :]
        sre = _shift_rows(re, s * rows_per_chunk, up)
        sim = _shift_rows(im, s * rows_per_chunk, up)
        re, im = re + ar * sre - ai * sim, im + ar * sim + ai * sre
        level, s = level + 1, 2 * s
    return re, im


def _s5_kernel(u_ref, scl_ref, shl_ref, scc_ref, shc_ref, t_ref, win_ref, wout_ref, pw_ref, y_ref, *, bsz, ctx_chunks):
    f32, bf16 = jnp.float32, jnp.bfloat16
    n = u_ref.shape[1]
    width = u_ref.shape[2]
    rc = ctx_chunks * bsz
    hs = []
    for gi in range(2):
        xv = u_ref[gi].reshape(n // 8, 8, width)
        hl = xv * scl_ref[gi][None] + shl_ref[gi][None]
        hc = xv * scc_ref[gi][None] + shc_ref[gi][None]
        slab = lax.broadcasted_iota(jnp.int32, xv.shape, 0)
        hs.append(jnp.where(slab < rc // 8, hc, hl).reshape(n, width).astype(bf16))
    y = [jnp.dot(hs[gi], t_ref[0, gi], preferred_element_type=f32) for gi in range(2)]
    half = width // 2
    for direction in range(2):
        sloc = (jnp.dot(hs[0], win_ref[0, direction, 0], preferred_element_type=f32)
                + jnp.dot(hs[1], win_ref[0, direction, 1], preferred_element_type=f32))
        re, im = sloc[:, :half], sloc[:, half:]
        pw = pw_ref[0, direction]
        if direction == 0:
            sre, sim = _chunk_scan(_shift_rows(re, bsz, False), _shift_rows(im, bsz, False), pw, bsz, False)
        else:
            cre, cim = _chunk_scan(_shift_rows(re[:rc], bsz, True), _shift_rows(im[:rc], bsz, True), pw, bsz, True)
            ar, ai = pw[0:1, :], pw[1:2, :]
            fre = ar * cre[0:8] - ai * cim[0:8] + re[0:8]
            fim = ar * cim[0:8] + ai * cre[0:8] + im[0:8]
            row8 = lax.broadcasted_iota(jnp.int32, fre.shape, 0)
            tre = jnp.where(row8 >= 8 - bsz, pltpu.roll(fre, 8 - bsz, axis=0), 0.0)
            tim = jnp.where(row8 >= 8 - bsz, pltpu.roll(fim, 8 - bsz, axis=0), 0.0)
            lre, lim = _shift_rows(re[rc:], bsz, True), _shift_rows(im[rc:], bsz, True)
            lre = jnp.concatenate([lre[:-8], lre[-8:] + tre], axis=0)
            lim = jnp.concatenate([lim[:-8], lim[-8:] + tim], axis=0)
            lre, lim = _chunk_scan(lre, lim, pw, bsz, True)
            sre = jnp.concatenate([cre, lre], axis=0)
            sim = jnp.concatenate([cim, lim], axis=0)
        s_in = jnp.concatenate([sre, sim], axis=1).astype(bf16)
        for gi in range(2):
            y[gi] = y[gi] + jnp.dot(s_in, wout_ref[0, direction, gi], preferred_element_type=f32)
    for gi in range(2):
        y_ref[gi] = y[gi].astype(y_ref.dtype)


def s5_ssm_pallas(xa, mod_tab, n_lat, seq, bsz, prep):
    tsum, win_p, wout_p, pw = prep
    f32 = jnp.float32
    n_tok, dm = xa.shape
    ctx_len = (n_tok - n_lat) // bsz
    q = S5_CHUNK
    n_chunks = (seq + ctx_len) // q
    width = q * S5_GROUP
    full = jnp.concatenate([xa[n_lat:].reshape(bsz, ctx_len, dm), xa[:n_lat].reshape(bsz, seq, dm)], axis=1)
    u = full.reshape(bsz, n_chunks, q, S5_GROUPS, S5_GROUP).transpose(3, 1, 0, 2, 4).reshape(S5_GROUPS, n_chunks * bsz, width)

    def tile(vec_rows):
        rows = vec_rows.shape[0]
        t = vec_rows.reshape(rows, S5_GROUPS, 1, S5_GROUP)
        t = jnp.broadcast_to(t, (rows, S5_GROUPS, q, S5_GROUP)).reshape(rows, S5_GROUPS, width)
        return jnp.tile(t.transpose(1, 0, 2), (1, 8 // rows, 1))

    scl, shl = tile(1.0 + mod_tab[:bsz, dm:2 * dm]), tile(mod_tab[:bsz, 0:dm])
    scc = tile(jnp.broadcast_to(1.0 + mod_tab[bsz:bsz + 1, dm:2 * dm], (bsz, dm)))
    shc = tile(jnp.broadcast_to(mod_tab[bsz:bsz + 1, 0:dm], (bsz, dm)))
    n_rows = n_chunks * bsz
    tile_spec = pl.BlockSpec((2, 8, width), lambda p: (p, 0, 0))
    y = pl.pallas_call(
        functools.partial(_s5_kernel, bsz=bsz, ctx_chunks=ctx_len // q),
        grid=(S5_GROUPS // 2,),
        in_specs=[
            pl.BlockSpec((2, n_rows, width), lambda p: (p, 0, 0)),
            tile_spec, tile_spec, tile_spec, tile_spec,
            pl.BlockSpec((1, 2, width, width), lambda p: (p, 0, 0, 0)),
            pl.BlockSpec((1, 2, 2, width, width), lambda p: (p, 0, 0, 0, 0)),
            pl.BlockSpec((1, 2, 2, width, width), lambda p: (p, 0, 0, 0, 0)),
            pl.BlockSpec((1, 2, 2 * S5_LEVELS, 2 * S5_STATE), lambda p: (p, 0, 0, 0)),
        ],
        out_specs=pl.BlockSpec((2, n_rows, width), lambda p: (p, 0, 0)),
        out_shape=jax.ShapeDtypeStruct((S5_GROUPS, n_rows, width), jnp.bfloat16),
        compiler_params=pltpu.CompilerParams(dimension_semantics=("arbitrary",), vmem_limit_bytes=VMEM_LIMIT),
        name="s5_ssm",
    )(u, scl, shl, scc, shc, tsum, win_p, wout_p, pw)
    yf = y.reshape(S5_GROUPS, n_chunks, bsz, q, S5_GROUP).transpose(2, 1, 3, 0, 4).reshape(bsz, seq + ctx_len, dm)
    return jnp.concatenate([yf[:, ctx_len:].reshape(n_lat, dm), yf[:, :ctx_len].reshape(bsz * ctx_len, dm)], axis=0)


def _glu_ln_kernel(y_ref, x_ref, mod_ref, wv_ref, wg_ref, lng_ref, lnb_ref, o_ref):
    f32 = jnp.float32
    dm = x_ref.shape[1]
    g = jax.nn.gelu(y_ref[...].astype(f32)).astype(jnp.bfloat16)
    val = jnp.dot(g, wv_ref[...], preferred_element_type=f32)
    gate = jnp.dot(g, wg_ref[...], preferred_element_type=f32)
    out = val * jax.nn.sigmoid(gate)
    mod = mod_ref[0]
    z = DEEPNORM_ALPHA * x_ref[...] + mod[:, 2 * dm:3 * dm] * out
    mu = jnp.mean(z, axis=-1, keepdims=True)
    zc = z - mu
    var = jnp.mean(zc * zc, axis=-1, keepdims=True)
    o_ref[...] = zc * lax.rsqrt(var + LN_EPS) * lng_ref[...] + lnb_ref[...]


TOK_TM = 512


def glu_ln_pallas(ya, xa, mod3, n_lat, seq, bsz, w_val, w_gate, ln_g, ln_b):
    n_tok, dm = ya.shape
    tm = TOK_TM
    bf16 = jnp.bfloat16
    mod_idx = functools.partial(_mod_row_index, tm=tm, n_lat=n_lat, seq=seq, bsz=bsz)
    return pl.pallas_call(
        _glu_ln_kernel,
        grid=(n_tok // tm,),
        in_specs=[
            pl.BlockSpec((tm, dm), lambda i: (i, 0)),
            pl.BlockSpec((tm, dm), lambda i: (i, 0)),
            pl.BlockSpec((1, 1, 6 * dm), lambda i: (mod_idx(i), 0, 0)),
            pl.BlockSpec((dm, dm), lambda i: (0, 0)),
            pl.BlockSpec((dm, dm), lambda i: (0, 0)),
            pl.BlockSpec((1, dm), lambda i: (0, 0)),
            pl.BlockSpec((1, dm), lambda i: (0, 0)),
        ],
        out_specs=pl.BlockSpec((tm, dm), lambda i: (i, 0)),
        out_shape=jax.ShapeDtypeStruct((n_tok, dm), jnp.float32),
        compiler_params=pltpu.CompilerParams(dimension_semantics=("arbitrary",), vmem_limit_bytes=VMEM_LIMIT),
        name="glu_ln",
    )(ya, xa, mod3, w_val.astype(bf16), w_gate.astype(bf16), ln_g.reshape(1, dm), ln_b.reshape(1, dm))


MLA_HD = 128
MLA_TQ = 256
MLA_HB = 4
MLA_KSPLIT = 4


def _mla_prep(w_down, w_uq, w_uk, w_uv):
    bf16 = jnp.bfloat16
    quarter = MLA_ROPE // 4
    swap = np.concatenate([np.arange(quarter, 2 * quarter), np.arange(0, quarter),
                           np.arange(3 * quarter, 4 * quarter), np.arange(2 * quarter, 3 * quarter)])
    dm = w_down.shape[0]
    w_cq = w_down[:, :MLA_Q_RANK]
    w_ckv = w_down[:, MLA_Q_RANK:MLA_Q_RANK + MLA_KV_RANK]
    w_kr = w_down[:, MLA_Q_RANK + MLA_KV_RANK:]
    wd_t = jnp.concatenate([w_cq, w_ckv, jnp.zeros((dm, MLA_NOPE), w_down.dtype), w_kr, w_kr[:, swap]], axis=1).T
    rope = w_uq[:, :, MLA_NOPE:]
    wuq_t = jnp.concatenate([w_uq, rope[:, :, swap]], axis=2).reshape(MLA_Q_RANK, MLA_HEADS * MLA_HD).T
    wuk = jnp.concatenate([w_uk, jnp.zeros((MLA_KV_RANK, MLA_HEADS, MLA_HD - MLA_NOPE), w_uk.dtype)], axis=2)
    wuk = wuk.reshape(MLA_KV_RANK, MLA_HEADS * MLA_HD)
    wuv_t = w_uv.reshape(MLA_KV_RANK, MLA_HEADS * MLA_V).T
    return wd_t.astype(bf16), wuq_t.astype(bf16), wuk.astype(bf16), wuv_t.astype(bf16)


def _rope_tables(seq, tm):
    quarter = MLA_ROPE // 4
    freqs = ROPE_BASE ** (-jnp.arange(quarter, dtype=jnp.float32) / quarter)
    t = jnp.arange(seq, dtype=jnp.float32)
    row, col = jnp.floor(t / GRID_W), t - GRID_W * jnp.floor(t / GRID_W)
    ang_r, ang_c = freqs[:, None] * row[None, :], freqs[:, None] * col[None, :]
    cos32 = jnp.concatenate([jnp.cos(ang_r), jnp.cos(ang_r), jnp.cos(ang_c), jnp.cos(ang_c)], axis=0)
    sin32 = jnp.concatenate([-jnp.sin(ang_r), jnp.sin(ang_r), -jnp.sin(ang_c), jnp.sin(ang_c)], axis=0)
    ones = jnp.ones((MLA_NOPE, seq + tm), jnp.float32)
    zeros = jnp.zeros((MLA_HD - MLA_NOPE - MLA_ROPE, seq + tm), jnp.float32)
    cos_t = jnp.concatenate([ones, jnp.concatenate([cos32, jnp.ones((MLA_ROPE, tm))], axis=1), zeros], axis=0)
    sin_t = jnp.concatenate([0.0 * ones, jnp.concatenate([sin32, jnp.zeros((MLA_ROPE, tm))], axis=1), zeros], axis=0)
    return cos_t, sin_t


def _mla_proj_kernel(x_ref, mod_ref, cos_ref, sin_ref, wd_ref, wuq_ref, wuk_ref, wuv_ref, qn_ref, kvn_ref,
                     qt_ref, k_ref, vt_ref):
    f32, bf16 = jnp.float32, jnp.bfloat16
    dm = x_ref.shape[1]
    mod = mod_ref[0]
    h = x_ref[...] * (1.0 + mod[:, dm:2 * dm]) + mod[:, 0:dm]
    ht = h.T.astype(bf16)
    dt = jnp.dot(wd_ref[...], ht, preferred_element_type=f32)
    cq, ckv, kr = dt[:MLA_Q_RANK], dt[MLA_Q_RANK:MLA_Q_RANK + MLA_KV_RANK], dt[MLA_Q_RANK + MLA_KV_RANK:]
    cqn = cq * lax.rsqrt(jnp.mean(cq * cq, axis=0, keepdims=True) + RMS_EPS) * qn_ref[...]
    ckvn = ckv * lax.rsqrt(jnp.mean(ckv * ckv, axis=0, keepdims=True) + RMS_EPS) * kvn_ref[...]
    cos_t, sin_t = cos_ref[...], sin_ref[...]

    def rope(v):
        shifted = jnp.concatenate([v[MLA_ROPE:], v[:MLA_ROPE]], axis=0)
        return v * cos_t + shifted * sin_t

    scale = (MLA_NOPE + MLA_ROPE) ** -0.5
    q_all = jnp.dot(wuq_ref[...], cqn.astype(bf16), preferred_element_type=f32)
    for hd in range(MLA_HEADS):
        qt_ref[hd] = (rope(q_all[hd * MLA_HD:(hd + 1) * MLA_HD]) * scale).astype(bf16)
    ckvn_bf = ckvn.astype(bf16)
    vt_ref[...] = jnp.dot(wuv_ref[...], ckvn_bf, preferred_element_type=f32).astype(bf16)
    k_all = jnp.dot(ckvn.T.astype(bf16), wuk_ref[...], preferred_element_type=f32)
    kr_rows = rope(kr).T
    for hd in range(MLA_HEADS):
        k_ref[hd] = (k_all[:, hd * MLA_HD:(hd + 1) * MLA_HD] + kr_rows).astype(bf16)


def _mla_attn_kernel(*refs, with_latent):
    f32, bf16 = jnp.float32, jnp.bfloat16
    if with_latent:
        q_ref, kl_ref, kc_ref, vl_ref, vc_ref, o_ref = refs
    else:
        q_ref, kc_ref, vc_ref, o_ref = refs
    for i in range(q_ref.shape[0]):
        rows = slice(i * MLA_V, (i + 1) * MLA_V)
        q = q_ref[i]
        s_c = jnp.dot(kc_ref[i], q, preferred_element_type=f32)
        m = jnp.max(s_c, axis=0, keepdims=True)
        parts = []
        if with_latent:
            n_keys = kl_ref.shape[1]
            for k0 in range(0, n_keys, n_keys // MLA_KSPLIT):
                ks = slice(k0, k0 + n_keys // MLA_KSPLIT)
                s_l = jnp.dot(kl_ref[i, ks, :], q, preferred_element_type=f32)
                m = jnp.maximum(m, jnp.max(s_l, axis=0, keepdims=True))
                parts.append((ks, s_l))
        p_c = jnp.exp(s_c - m)
        den = jnp.sum(p_c, axis=0, keepdims=True)
        o = jnp.dot(vc_ref[rows, :], p_c.astype(bf16), preferred_element_type=f32)
        for ks, s_l in parts:
            p_l = jnp.exp(s_l - m)
            den = den + jnp.sum(p_l, axis=0, keepdims=True)
            o = o + jnp.dot(vl_ref[rows, ks], p_l.astype(bf16), preferred_element_type=f32)
        o_ref[rows, :] = (o / den).astype(bf16)


def _mla_out_kernel(ot_ref, x_ref, mod_ref, wo_ref, lng_ref, lnb_ref, o_ref):
    f32 = jnp.float32
    dm = x_ref.shape[1]
    attn = ot_ref[...].astype(f32).T.astype(jnp.bfloat16)
    out = jnp.dot(attn, wo_ref[...], preferred_element_type=f32)
    mod = mod_ref[0]
    z = DEEPNORM_ALPHA * x_ref[...] + mod[:, 2 * dm:3 * dm] * out
    mu = jnp.mean(z, axis=-1, keepdims=True)
    zc = z - mu
    var = jnp.mean(zc * zc, axis=-1, keepdims=True)
    o_ref[...] = zc * lax.rsqrt(var + LN_EPS) * lng_ref[...] + lnb_ref[...]


def mla_ln_pallas(xa, mod3, n_lat, seq, bsz, w_down, q_norm, kv_norm, w_uq, w_uk, w_uv, w_o, ln_g, ln_b):
    f32, bf16 = jnp.float32, jnp.bfloat16
    n_tok, dm = xa.shape
    n_ctx = n_tok - n_lat
    ctx_len = n_ctx // bsz
    tm = TOK_TM
    wd_t, wuq_t, wuk, wuv_t = _mla_prep(w_down, w_uq, w_uk, w_uv)
    cos_t, sin_t = _rope_tables(seq, tm)
    mod_idx = functools.partial(_mod_row_index, tm=tm, n_lat=n_lat, seq=seq, bsz=bsz)
    tiles_per_sample = seq // tm
    pos_idx = lambda i: jnp.where(i * tm < n_lat, i % tiles_per_sample, tiles_per_sample)
    n_hd, n_dn = MLA_HEADS * MLA_HD, wd_t.shape[0]
    whole = lambda shape: pl.BlockSpec(shape, lambda i: (0,) * len(shape))
    params = pltpu.CompilerParams(dimension_semantics=("arbitrary",), vmem_limit_bytes=VMEM_LIMIT)
    qt, k, vt = pl.pallas_call(
        _mla_proj_kernel,
        grid=(n_tok // tm,),
        in_specs=[
            pl.BlockSpec((tm, dm), lambda i: (i, 0)),
            pl.BlockSpec((1, 1, 6 * dm), lambda i: (mod_idx(i), 0, 0)),
            pl.BlockSpec((MLA_HD, tm), lambda i: (0, pos_idx(i))),
            pl.BlockSpec((MLA_HD, tm), lambda i: (0, pos_idx(i))),
            whole((n_dn, dm)), whole((n_hd, MLA_Q_RANK)), whole((MLA_KV_RANK, n_hd)),
            whole((MLA_HEADS * MLA_V, MLA_KV_RANK)), whole((MLA_Q_RANK, 1)), whole((MLA_KV_RANK, 1)),
        ],
        out_specs=[
            pl.BlockSpec((MLA_HEADS, MLA_HD, tm), lambda i: (0, 0, i)),
            pl.BlockSpec((MLA_HEADS, tm, MLA_HD), lambda i: (0, i, 0)),
            pl.BlockSpec((MLA_HEADS * MLA_V, tm), lambda i: (0, i)),
        ],
        out_shape=[
            jax.ShapeDtypeStruct((MLA_HEADS, MLA_HD, n_tok), bf16),
            jax.ShapeDtypeStruct((MLA_HEADS, n_tok, MLA_HD), bf16),
            jax.ShapeDtypeStruct((MLA_HEADS * MLA_V, n_tok), bf16),
        ],
        compiler_params=params,
        name="mla_proj",
    )(xa, mod3, cos_t, sin_t, wd_t, wuq_t, wuk, wuv_t, q_norm.reshape(-1, 1), kv_norm.reshape(-1, 1))

    tq, hb = MLA_TQ, MLA_HB
    n_qt = seq // tq
    ctx_blk = n_lat // ctx_len
    params3 = pltpu.CompilerParams(dimension_semantics=("arbitrary",) * 3, vmem_limit_bytes=VMEM_LIMIT)
    ot_lat = pl.pallas_call(
        functools.partial(_mla_attn_kernel, with_latent=True),
        grid=(bsz, MLA_HEADS // hb, n_qt),
        in_specs=[
            pl.BlockSpec((hb, MLA_HD, tq), lambda b, hd, t: (hd, 0, b * n_qt + t)),
            pl.BlockSpec((hb, seq, MLA_HD), lambda b, hd, t: (hd, b, 0)),
            pl.BlockSpec((hb, ctx_len, MLA_HD), lambda b, hd, t: (hd, ctx_blk + b, 0)),
            pl.BlockSpec((hb * MLA_V, seq), lambda b, hd, t: (hd, b)),
            pl.BlockSpec((hb * MLA_V, ctx_len), lambda b, hd, t: (hd, ctx_blk + b)),
        ],
        out_specs=pl.BlockSpec((hb * MLA_V, tq), lambda b, hd, t: (hd, b * n_qt + t)),
        out_shape=jax.ShapeDtypeStruct((MLA_HEADS * MLA_V, n_lat), bf16),
        compiler_params=params3,
        name="mla_attn_latent",
    )(qt, k, k, vt, vt)
    ot_ctx = pl.pallas_call(
        functools.partial(_mla_attn_kernel, with_latent=False),
        grid=(bsz, MLA_HEADS // hb, 1),
        in_specs=[
            pl.BlockSpec((hb, MLA_HD, ctx_len), lambda b, hd, t: (hd, 0, ctx_blk + b)),
            pl.BlockSpec((hb, ctx_len, MLA_HD), lambda b, hd, t: (hd, ctx_blk + b, 0)),
            pl.BlockSpec((hb * MLA_V, ctx_len), lambda b, hd, t: (hd, ctx_blk + b)),
        ],
        out_specs=pl.BlockSpec((hb * MLA_V, ctx_len), lambda b, hd, t: (hd, b)),
        out_shape=jax.ShapeDtypeStruct((MLA_HEADS * MLA_V, n_ctx), bf16),
        compiler_params=params3,
        name="mla_attn_context",
    )(qt, k, vt)
    ot = jnp.concatenate([ot_lat, ot_ctx], axis=1)
    return pl.pallas_call(
        _mla_out_kernel,
        grid=(n_tok // tm,),
        in_specs=[
            pl.BlockSpec((MLA_HEADS * MLA_V, tm), lambda i: (0, i)),
            pl.BlockSpec((tm, dm), lambda i: (i, 0)),
            pl.BlockSpec((1, 1, 6 * dm), lambda i: (mod_idx(i), 0, 0)),
            whole((MLA_HEADS * MLA_V, dm)), whole((1, dm)), whole((1, dm)),
        ],
        out_specs=pl.BlockSpec((tm, dm), lambda i: (i, 0)),
        out_shape=jax.ShapeDtypeStruct((n_tok, dm), f32),
        compiler_params=params,
        name="mla_out_ln",
    )(ot, xa, mod3, w_o.astype(bf16), ln_g.reshape(1, dm), ln_b.reshape(1, dm))


SSD_DT_PAD = 128
SSD_HPG = SSD_HEADS // SSD_GROUPS
SSD_GW = SSD_HPG * SSD_HEADDIM


def _ssd_in_kernel(x_ref, mod_ref, w_ref, wdt_ref, z_ref, xbc_ref, dt_ref, dtt_ref):
    f32, bf16 = jnp.float32, jnp.bfloat16
    dm = x_ref.shape[1]
    mod = mod_ref[0]
    h = x_ref[...] * (1.0 + mod[:, dm:2 * dm]) + mod[:, 0:dm]
    proj = jnp.dot(h.astype(bf16), w_ref[...], preferred_element_type=f32)
    z_ref[...] = proj[:, :SSD_D_INNER]
    xbc_ref[...] = proj[:, SSD_D_INNER:SSD_D_INNER + SSD_CONV_DIM]
    dt_ref[...] = proj[:, SSD_D_INNER + SSD_CONV_DIM:]
    dtt_ref[...] = jnp.dot(wdt_ref[...], h.T.astype(bf16), preferred_element_type=f32)


def _softplus(v):
    return jnp.maximum(v, 0.0) + jnp.log(1.0 + jnp.exp(-jnp.abs(v)))


def _split3(v):
    f32, bf16 = jnp.float32, jnp.bfloat16
    hi = v.astype(bf16)
    r1 = v - hi.astype(f32)
    mid = r1.astype(bf16)
    lo = (r1 - mid.astype(f32)).astype(bf16)
    return jnp.concatenate([hi, mid, lo], axis=1)


def _ssd_direction(direction, xbc_ref, prev_ref, next_ref, dt_ref, dtt_ref, has_prev, has_next,
                   cw_ref, cb_ref, bias_r_ref, a_r_ref, bias_c_ref, a_c_ref, e_ref, dskip_ref, state_ref, y_ref):
    f32, bf16 = jnp.float32, jnp.bfloat16
    hp = lax.Precision.HIGHEST
    q = xbc_ref.shape[0]
    nh = SSD_HEADS
    xm = xbc_ref[...]
    row = lax.broadcasted_iota(jnp.int32, xm.shape, 0)
    before = jnp.where(has_prev, prev_ref[7:8, :], 0.0)
    after = jnp.where(has_next, next_ref[0:1, :], 0.0)
    x_dn = jnp.where(row == 0, before, pltpu.roll(xm, 1, axis=0))
    x_up = jnp.where(row == q - 1, after, pltpu.roll(xm, q - 1, axis=0))
    conv = cb_ref[...] + x_dn * cw_ref[0:1, :] + xm * cw_ref[1:2, :] + x_up * cw_ref[2:3, :]
    conv = conv * jax.nn.sigmoid(conv)
    xs = conv[:, :SSD_D_INNER]
    gn = SSD_GROUPS * SSD_STATE
    bm, cm = conv[:, SSD_D_INNER:SSD_D_INNER + gn], conv[:, SSD_D_INNER + gn:]
    hs = slice(direction * nh, (direction + 1) * nh)
    dt = _softplus(dt_ref[:, hs] + bias_r_ref[direction:direction + 1, :])
    dtt = _softplus(dtt_ref[hs, :] + bias_c_ref[direction])
    r_i = lax.broadcasted_iota(jnp.int32, (q, q), 0)
    c_i = lax.broadcasted_iota(jnp.int32, (q, q), 1)
    causal = (r_i >= c_i) if direction == 0 else (r_i <= c_i)
    tri = causal.astype(f32)
    tri_t = ((c_i >= r_i) if direction == 0 else (c_i <= r_i)).astype(f32)
    a_cum = jnp.dot(tri, dt * a_r_ref[direction:direction + 1, :], precision=hp, preferred_element_type=f32)
    a_cum_t = jnp.dot(dtt * a_c_ref[direction], tri_t, precision=hp, preferred_element_type=f32)
    last = q - 1 if direction == 0 else 0
    a_exp = jnp.dot(_split3(a_cum), e_ref[...], preferred_element_type=f32)
    dt_exp = jnp.dot(_split3(dt), e_ref[...], preferred_element_type=f32)
    xdt = xs * dt_exp
    a_tot = a_exp[last:last + 1, :]
    xw = (xdt * jnp.exp(a_tot - a_exp)).astype(bf16)
    grow = jnp.exp(a_exp)
    carry = jnp.exp(a_tot)
    xdt_bf = xdt.astype(bf16)
    for g in range(SSD_GROUPS):
        gl = slice(g * SSD_GW, (g + 1) * SSD_GW)
        nl = slice(g * SSD_STATE, (g + 1) * SSD_STATE)
        bm_g, cm_g = bm[:, nl], cm[:, nl].astype(bf16)
        prev = state_ref[direction, g]
        y_g = jnp.dot(cm_g, prev.astype(bf16), preferred_element_type=f32) * grow[:, gl]
        states = jnp.dot(bm_g.T.astype(bf16), xw[:, gl], preferred_element_type=f32)
        state_ref[direction, g] = prev * carry[:, gl] + states
        cb = lax.dot_general(cm_g, bm_g.astype(bf16), (((1,), (1,)), ((), ())), preferred_element_type=f32)
        parts = []
        for hh in range(SSD_HPG):
            hd = g * SSD_HPG + hh
            seg = jnp.broadcast_to(a_cum[:, hd:hd + 1], (q, q)) - a_cum_t[hd:hd + 1, :]
            lmat = (jnp.where(causal, jnp.exp(seg), 0.0) * cb).astype(bf16)
            parts.append(jnp.dot(lmat, xdt_bf[:, hd * SSD_HEADDIM:(hd + 1) * SSD_HEADDIM], preferred_element_type=f32))
        y_g = y_g + jnp.concatenate(parts, axis=1)
        if direction == 0:
            y_g = y_g + dskip_ref[:, gl] * xs[:, gl]
        y_ref[:, gl] = y_g


def _ssd_scan_kernel(xf_ref, pf_ref, nf_ref, dtf_ref, dttf_ref, xb_ref, pb_ref, nb_ref, dtb_ref, dttb_ref,
                     cw_ref, cb_ref, bias_r_ref, a_r_ref, bias_c_ref, a_c_ref, e_ref, dskip_ref,
                     yf_ref, yb_ref, state_ref, *, ctx_chunks, lat_chunks):
    j = pl.program_id(1)

    @pl.when(j == 0)
    def _reset():
        state_ref[...] = jnp.zeros(state_ref.shape, jnp.float32)

    in_ctx = j < ctx_chunks
    pos_f = jnp.where(in_ctx, j, j - ctx_chunks)
    seg_len = jnp.where(in_ctx, ctx_chunks, lat_chunks)
    pos_b = seg_len - 1 - pos_f
    shared = (cw_ref, cb_ref, bias_r_ref, a_r_ref, bias_c_ref, a_c_ref, e_ref, dskip_ref, state_ref)
    _ssd_direction(0, xf_ref, pf_ref, nf_ref, dtf_ref, dttf_ref, pos_f > 0, pos_f < seg_len - 1, *shared, yf_ref)
    _ssd_direction(1, xb_ref, pb_ref, nb_ref, dtb_ref, dttb_ref, pos_b > 0, pos_b < seg_len - 1, *shared, yb_ref)


def _ssd_out_kernel(yf_ref, yb_ref, z_ref, x_ref, mod_ref, nw_ref, wo_ref, lng_ref, lnb_ref, o_ref):
    f32 = jnp.float32
    dm = x_ref.shape[1]
    z = z_ref[...]
    y = (yf_ref[...] + yb_ref[...]) * (z * jax.nn.sigmoid(z))
    y = y * lax.rsqrt(jnp.mean(y * y, axis=-1, keepdims=True) + RMS_EPS) * nw_ref[...]
    out = jnp.dot(y.astype(jnp.bfloat16), wo_ref[...], preferred_element_type=f32)
    mod = mod_ref[0]
    v = DEEPNORM_ALPHA * x_ref[...] + mod[:, 2 * dm:3 * dm] * out
    mu = jnp.mean(v, axis=-1, keepdims=True)
    vc = v - mu
    var = jnp.mean(vc * vc, axis=-1, keepdims=True)
    o_ref[...] = vc * lax.rsqrt(var + LN_EPS) * lng_ref[...] + lnb_ref[...]


def ssd_ln_pallas(xa, mod3, n_lat, seq, bsz, w_in, conv_w, conv_b, dt_bias, a_log, d, norm_w, w_out, ln_g, ln_b):
    f32, bf16 = jnp.float32, jnp.bfloat16
    n_tok, dm = xa.shape
    ctx_len = (n_tok - n_lat) // bsz
    tm = TOK_TM
    q = SSD_CHUNK
    nh = SSD_HEADS
    pad = SSD_DT_PAD - 2 * nh
    w_ext = jnp.concatenate([w_in, jnp.zeros((dm, pad), w_in.dtype)], axis=1).astype(bf16)
    n_in = w_ext.shape[1]
    wdt_t = w_ext[:, SSD_D_INNER + SSD_CONV_DIM:].T
    mod_idx = functools.partial(_mod_row_index, tm=tm, n_lat=n_lat, seq=seq, bsz=bsz)
    whole = lambda shape: pl.BlockSpec(shape, lambda *_: (0,) * len(shape))
    params = pltpu.CompilerParams(dimension_semantics=("arbitrary",), vmem_limit_bytes=VMEM_LIMIT)
    z, xbc, dt, dtt = pl.pallas_call(
        _ssd_in_kernel,
        grid=(n_tok // tm,),
        in_specs=[
            pl.BlockSpec((tm, dm), lambda i: (i, 0)),
            pl.BlockSpec((1, 1, 6 * dm), lambda i: (mod_idx(i), 0, 0)),
            whole((dm, n_in)), whole((SSD_DT_PAD, dm)),
        ],
        out_specs=[
            pl.BlockSpec((tm, SSD_D_INNER), lambda i: (i, 0)),
            pl.BlockSpec((tm, SSD_CONV_DIM), lambda i: (i, 0)),
            pl.BlockSpec((tm, SSD_DT_PAD), lambda i: (i, 0)),
            pl.BlockSpec((SSD_DT_PAD, tm), lambda i: (0, i)),
        ],
        out_shape=[
            jax.ShapeDtypeStruct((n_tok, SSD_D_INNER), f32),
            jax.ShapeDtypeStruct((n_tok, SSD_CONV_DIM), f32),
            jax.ShapeDtypeStruct((n_tok, SSD_DT_PAD), f32),
            jax.ShapeDtypeStruct((SSD_DT_PAD, n_tok), f32),
        ],
        compiler_params=params,
        name="ssd_in_proj",
    )(xa, mod3, w_ext, wdt_t)

    ctx_chunks, lat_chunks = ctx_len // q, seq // q
    ctx_base = n_lat // q
    n_chunk_total = n_tok // q

    def chunk_of(b, j, backward):
        in_ctx = j < ctx_chunks
        pos = jnp.where(in_ctx, j, j - ctx_chunks)
        seg = jnp.where(in_ctx, ctx_chunks, lat_chunks)
        pos = jnp.where(backward, seg - 1 - pos, pos)
        return jnp.where(in_ctx, ctx_base + b * ctx_chunks + pos, b * lat_chunks + pos)

    sub = q // 8
    def dir_specs(backward):
        ch = lambda b, j: chunk_of(b, j, backward)
        return [
            pl.BlockSpec((q, SSD_CONV_DIM), lambda b, j: (ch(b, j), 0)),
            pl.BlockSpec((8, SSD_CONV_DIM), lambda b, j: (jnp.maximum(ch(b, j) * sub - 1, 0), 0)),
            pl.BlockSpec((8, SSD_CONV_DIM), lambda b, j: (jnp.minimum((ch(b, j) + 1) * sub, n_chunk_total * sub - 1), 0)),
            pl.BlockSpec((q, SSD_DT_PAD), lambda b, j: (ch(b, j), 0)),
            pl.BlockSpec((SSD_DT_PAD, q), lambda b, j: (0, ch(b, j))),
        ]

    a = -jnp.exp(a_log.astype(f32))
    expand = jnp.repeat(jnp.eye(nh, dtype=f32), SSD_HEADDIM, axis=1)
    e3 = jnp.concatenate([expand, expand, expand], axis=0).astype(bf16)
    dskip = jnp.repeat(d, SSD_HEADDIM)[None, :]
    yf, yb = pl.pallas_call(
        functools.partial(_ssd_scan_kernel, ctx_chunks=ctx_chunks, lat_chunks=lat_chunks),
        grid=(bsz, ctx_chunks + lat_chunks),
        in_specs=dir_specs(False) + dir_specs(True) + [
            whole((SSD_CONV, SSD_CONV_DIM)), whole((1, SSD_CONV_DIM)),
            whole((2, nh)), whole((2, nh)), whole((2, nh, 1)), whole((2, nh, 1)),
            whole((3 * nh, SSD_D_INNER)), whole((1, SSD_D_INNER)),
        ],
        out_specs=[
            pl.BlockSpec((q, SSD_D_INNER), lambda b, j: (chunk_of(b, j, False), 0)),
            pl.BlockSpec((q, SSD_D_INNER), lambda b, j: (chunk_of(b, j, True), 0)),
        ],
        out_shape=[jax.ShapeDtypeStruct((n_tok, SSD_D_INNER), f32)] * 2,
        scratch_shapes=[pltpu.VMEM((2, SSD_GROUPS, SSD_STATE, SSD_GW), f32)],
        compiler_params=pltpu.CompilerParams(dimension_semantics=("arbitrary", "arbitrary"), vmem_limit_bytes=VMEM_LIMIT),
        name="ssd_scan",
    )(xbc, xbc, xbc, dt, dtt, xbc, xbc, xbc, dt, dtt,
      conv_w, conv_b.reshape(1, -1), dt_bias, a, dt_bias.reshape(2, nh, 1), a.reshape(2, nh, 1), e3, dskip)

    return pl.pallas_call(
        _ssd_out_kernel,
        grid=(n_tok // tm,),
        in_specs=[
            pl.BlockSpec((tm, SSD_D_INNER), lambda i: (i, 0)),
            pl.BlockSpec((tm, SSD_D_INNER), lambda i: (i, 0)),
            pl.BlockSpec((tm, SSD_D_INNER), lambda i: (i, 0)),
            pl.BlockSpec((tm, dm), lambda i: (i, 0)),
            pl.BlockSpec((1, 1, 6 * dm), lambda i: (mod_idx(i), 0, 0)),
            whole((1, SSD_D_INNER)), whole((SSD_D_INNER, dm)), whole((1, dm)), whole((1, dm)),
        ],
        out_specs=pl.BlockSpec((tm, dm), lambda i: (i, 0)),
        out_shape=jax.ShapeDtypeStruct((n_tok, dm), f32),
        compiler_params=params,
        name="ssd_out_ln",
    )(yf, yb, z, xa, mod3, norm_w.reshape(1, -1), w_out.astype(bf16), ln_g.reshape(1, dm), ln_b.reshape(1, dm))


def _modulation_kernel(c_ref, w_ref, b_ref, o_ref):
    cv = c_ref[...]
    act = cv * jax.nn.sigmoid(cv)
    o_ref[0] = jnp.dot(act, w_ref[0], precision=lax.Precision.HIGHEST,
                       preferred_element_type=jnp.float32) + b_ref[0]


def modulation_pallas(cond, mod_w, mod_b):
    depth, dm, n_out = mod_w.shape
    tn = dm
    return pl.pallas_call(
        _modulation_kernel,
        grid=(depth, n_out // tn),
        in_specs=[
            pl.BlockSpec((8, dm), lambda i, n: (0, 0)),
            pl.BlockSpec((1, dm, tn), lambda i, n: (i, 0, n)),
            pl.BlockSpec((1, 1, tn), lambda i, n: (i, 0, n)),
        ],
        out_specs=pl.BlockSpec((1, 8, tn), lambda i, n: (i, 0, n)),
        out_shape=jax.ShapeDtypeStruct((depth, 8, n_out), jnp.float32),
        compiler_params=pltpu.CompilerParams(dimension_semantics=("arbitrary", "arbitrary"), vmem_limit_bytes=VMEM_LIMIT),
        name="modulation",
    )(cond, mod_w, mod_b.reshape(depth, 1, n_out))


def kernel(x, c, ctx, c_ctx, mod_w, mod_b, ln_g, ln_b,
           s5_a_re, s5_a_im, s5_log_dt, s5_b_re, s5_b_im, s5_c_re, s5_c_im, s5_d, s5_w_gate, s5_w_val,
           ssd_w_in, ssd_conv_w, ssd_conv_b, ssd_dt_bias, ssd_a_log, ssd_d, ssd_norm_w, ssd_w_out,
           mla_w_down, mla_q_norm, mla_kv_norm, mla_w_uq, mla_w_uk, mla_w_uv, mla_w_o,
           peer_w_q, peer_subkeys, peer_u, peer_v):
    ROWS = x.shape[1] // GRID_W
    pos = grid_positions(ROWS)
    ctx_len = ctx.shape[1]
    bsz, seq_len, dm = x.shape
    n_lat, n_ctx = bsz * seq_len, bsz * ctx_len
    xa = jnp.concatenate([x.reshape(n_lat, dm), ctx.reshape(n_ctx, dm)], axis=0)
    cond = jnp.concatenate([c, c_ctx[None], jnp.zeros((7 - bsz, dm), x.dtype)], axis=0)
    mod_all = modulation_pallas(cond, mod_w, mod_b)
    for i in range(DEPTH):
        last = i == DEPTH - 1
        mod_tab = mod_all[i]
        mod3 = mod_tab.reshape(8, 1, 6 * dm)
        kind, j = i % N_MIXERS, i // N_MIXERS
        if kind == 0:
            prep = _s5_prep_fast(s5_a_re[j], s5_a_im[j], s5_log_dt[j], s5_b_re[j], s5_b_im[j], s5_c_re[j], s5_c_im[j], s5_d[j])
            ya = s5_ssm_pallas(xa, mod_tab, n_lat, seq_len, bsz, prep)
            if last:
                ya, xa = ya[:n_lat], xa[:n_lat]
            xa = glu_ln_pallas(ya, xa, mod3, n_lat, seq_len, bsz, s5_w_val[j], s5_w_gate[j], ln_g[i, 0], ln_b[i, 0])
        elif kind == 1:
            xa = ssd_ln_pallas(xa, mod3, n_lat, seq_len, bsz, ssd_w_in[j], ssd_conv_w[j], ssd_conv_b[j], ssd_dt_bias[j],
                               ssd_a_log[j], ssd_d[j], ssd_norm_w[j], ssd_w_out[j], ln_g[i, 0], ln_b[i, 0])
            if last:
                xa = xa[:n_lat]
        else:
            xa = mla_ln_pallas(xa, mod3, n_lat, seq_len, bsz, mla_w_down[j], mla_q_norm[j], mla_kv_norm[j],
                               mla_w_uq[j], mla_w_uk[j], mla_w_uv[j], mla_w_o[j], ln_g[i, 0], ln_b[i, 0])
            if last:
                xa = xa[:n_lat]
        xa = peer_ln_pallas(xa, mod3, n_lat, seq_len, bsz, peer_w_q[i], peer_subkeys[i], peer_u[i], peer_v[i],
                            ln_g[i, 1], ln_b[i, 1])
    return xa[:n_lat].reshape(bsz, seq_len, dm)
```

```python
import math
import functools
import jax
import jax.numpy as jnp
from jax import lax
import numpy as np
from jax.experimental import pallas as pl
from jax.experimental.pallas import tpu as pltpu

D_MODEL = 1024
DEPTH = 4

GRID_W = 64
N_MIXERS = 3
DEEPNORM_ALPHA = (2.0 * DEPTH) ** 0.25
LN_EPS = 1e-5
RMS_EPS = 1e-6
ROPE_BASE = 10000.0

S5_GROUP = 16
S5_GROUPS = D_MODEL // S5_GROUP
S5_STATE = 64

SSD_D_INNER = 2 * D_MODEL
SSD_HEADDIM = 64
SSD_HEADS = SSD_D_INNER // SSD_HEADDIM
SSD_GROUPS = 4
SSD_STATE = 128
SSD_CONV = 3
SSD_CHUNK = 128
SSD_CONV_DIM = SSD_D_INNER + 2 * SSD_GROUPS * SSD_STATE

MLA_HEADS = 16
MLA_Q_RANK = 256
MLA_KV_RANK = 128
MLA_NOPE = 64
MLA_ROPE = 32
MLA_V = 64

PEER_HEADS = 8
PEER_KEYS = 128
PEER_EXPERTS = PEER_KEYS * PEER_KEYS
PEER_QDIM = 256
PEER_TOPK = 16


PEER_TM = 512
PEER_EB = 1024
PEER_ROWS = 16
PEER_LANES = 128
PEER_ACT_ROWS = 512
VMEM_LIMIT = 56 * 1024 * 1024
NEG_INF = float("-inf")
GELU_C0 = math.sqrt(2.0 / math.pi)
GELU_C1 = 0.044715 * GELU_C0


def _top16_sorted(s):
    n = PEER_TOPK
    m = s.shape[0] // 8
    x = [s[8 * r:8 * r + 8, :] for r in range(m)]

    def exchange(i, l, descending):
        hi, lo = jnp.maximum(x[i], x[l]), jnp.minimum(x[i], x[l])
        x[i], x[l] = (hi, lo) if descending else (lo, hi)

    def merge_bitonic(size):
        j = size // 2
        while j >= 1:
            for i in range(size):
                if i ^ j > i:
                    exchange(i, i ^ j, True)
            j //= 2

    k = 2
    while k < m:
        j = k // 2
        while j >= 1:
            for i in range(m):
                if i ^ j > i:
                    exchange(i, i ^ j, (i & k) == 0)
            j //= 2
        k *= 2
    merge_bitonic(m)
    shifts = [4, 2, 1]
    if m < n:
        first = shifts.pop(0)
        x = x + [pltpu.roll(x[m - 1 - r], first, axis=0) for r in range(m)]
        merge_bitonic(n)
    for shift in shifts:
        other = [pltpu.roll(x[n - 1 - r], shift, axis=0) for r in range(n)]
        for r in range(n):
            x[r] = jnp.maximum(x[r], other[r])
        merge_bitonic(n)
    return [x[r][0:1, :] for r in range(n)]


def _dot3(a_hi, a_lo, b_hi, b_lo):
    f32 = jnp.float32
    return (jnp.dot(a_hi, b_hi, preferred_element_type=f32)
            + jnp.dot(a_hi, b_lo, preferred_element_type=f32)
            + jnp.dot(a_lo, b_hi, preferred_element_type=f32))


def _split_bf16(v):
    hi = v.astype(jnp.bfloat16)
    lo = (v - hi.astype(jnp.float32)).astype(jnp.bfloat16)
    return hi, lo


def _peer_kernel(x_ref, mod_ref, wq_ref, skh_ref, skl_ref, u_ref, vt_ref, vtl_ref, lng_ref, lnb_ref,
                 o_ref,
                 ht_ref, s_ref, n1_ref, e1_ref, r2_ref, e2_ref, v1_ref, v2_ref, cand_ref,
                 act_ref, gact_ref, acc_ref):
    f32 = jnp.float32
    bf16 = jnp.bfloat16
    eb = pl.program_id(1)
    n_eb = pl.num_programs(1)
    dm = x_ref.shape[1]
    a_per = PEER_EB // PEER_KEYS
    half = PEER_QDIM // 2

    @pl.when(eb == 0)
    def _prologue():
        mod = mod_ref[0]
        h = x_ref[...] * (1.0 + mod[:, 4 * dm:5 * dm]) + mod[:, 3 * dm:4 * dm]
        ht = h.T.astype(bf16)
        ht_ref[...] = ht
        qt = jnp.dot(wq_ref[...], ht, preferred_element_type=f32)
        cand_ref[...] = jnp.full(cand_ref.shape, NEG_INF, f32)
        gact_ref[...] = jnp.zeros(gact_ref.shape, bf16)
        for hd in range(PEER_HEADS):
            q1_hi, q1_lo = _split_bf16(qt[hd * PEER_QDIM: hd * PEER_QDIM + half])
            q2_hi, q2_lo = _split_bf16(qt[hd * PEER_QDIM + half: (hd + 1) * PEER_QDIM])
            s_ref[0] = _dot3(skh_ref[hd, 0], skl_ref[hd, 0], q1_hi, q1_lo)
            s_ref[1] = _dot3(skh_ref[hd, 1], skl_ref[hd, 1], q2_hi, q2_lo)
            for lg in range(0, x_ref.shape[0], PEER_LANES):
                lanes = slice(lg, lg + PEER_LANES)
                s1, s2 = s_ref[0, :, lanes], s_ref[1, :, lanes]
                top1 = _top16_sorted(s1)
                top2 = _top16_sorted(s2)
                for k in range(PEER_TOPK):
                    v1_ref[k:k + 1, lanes] = top1[k]
                    v2_ref[k:k + 1, lanes] = top2[k]
                off = 0
                for i in range(PEER_TOPK):
                    cnt = PEER_TOPK // (i + 1)
                    cand_ref[off:off + cnt, lanes] = v1_ref[i:i + 1, lanes] + v2_ref[0:cnt, lanes]
                    off += cnt
                best = _top16_sorted(cand_ref[:, lanes])
                z = jnp.zeros_like(best[0])
                for k in range(PEER_TOPK):
                    z = z + jnp.exp(best[k] - best[0])
                tau = best[PEER_TOPK - 1]
                n1 = jnp.zeros(s1.shape, f32)
                for k in range(PEER_TOPK):
                    n1 = jnp.where(s1 + top2[k] >= tau, k + 1.0, n1)
                n1_ref[hd, :, lanes] = n1
                e1_ref[hd, :, lanes] = jnp.exp(s1 - top1[0])
                rank2 = jnp.zeros(s2.shape, f32)
                for k in range(PEER_TOPK):
                    rank2 = jnp.where(top2[k] > s2, k + 1.0, rank2)
                r2_ref[hd, :, lanes] = rank2.astype(bf16)
                e2_ref[hd, :, lanes] = (jnp.exp(s2 - top2[0]) * (0.5 / z)).astype(bf16)
        acc_ref[...] = jnp.zeros(acc_ref.shape, f32)

    par = eb % 2
    for r0 in range(0, PEER_EB, PEER_ACT_ROWS):
        act_ref[r0:r0 + PEER_ACT_ROWS, :] = jnp.dot(u_ref[r0:r0 + PEER_ACT_ROWS, :], ht_ref[...],
                                                    preferred_element_type=f32)
    acc_ref[...] += jnp.dot(vt_ref[0], gact_ref[1 - par], preferred_element_type=f32)
    a_base = pl.multiple_of(eb * a_per, a_per)
    blk = (PEER_ROWS, x_ref.shape[0])
    n_slab = PEER_KEYS // PEER_ROWS
    for a in range(a_per):
        g = [None] * n_slab
        for hd in range(PEER_HEADS):
            n1a = jnp.broadcast_to(n1_ref[hd, pl.ds(a_base, a_per), :][a:a + 1, :].astype(bf16), blk)
            e1a = jnp.broadcast_to(e1_ref[hd, pl.ds(a_base, a_per), :][a:a + 1, :].astype(bf16), blk)
            for i in range(n_slab):
                rs = slice(i * PEER_ROWS, (i + 1) * PEER_ROWS)
                w = jnp.where(r2_ref[hd, rs, :] < n1a, e2_ref[hd, rs, :], jnp.zeros(blk, bf16)) * e1a
                g[i] = w if g[i] is None else g[i] + w
        for i in range(n_slab):
            rows = slice(a * PEER_KEYS + i * PEER_ROWS, a * PEER_KEYS + (i + 1) * PEER_ROWS)
            x = act_ref[rows, :]
            t = jnp.tanh(x * (GELU_C0 + GELU_C1 * (x * x)))
            gact_ref[par, rows, :] = g[i] * (x + x * t).astype(bf16)

    @pl.when(eb == n_eb - 1)
    def _epilogue():
        mod = mod_ref[0]
        ffn_t = acc_ref[...] + jnp.dot(vtl_ref[0], gact_ref[par], preferred_element_type=f32)
        y = DEEPNORM_ALPHA * x_ref[...] + mod[:, 5 * dm:6 * dm] * ffn_t.T
        mu = jnp.mean(y, axis=-1, keepdims=True)
        yc = y - mu
        var = jnp.mean(yc * yc, axis=-1, keepdims=True)
        o_ref[...] = yc * lax.rsqrt(var + LN_EPS) * lng_ref[...] + lnb_ref[...]


def _mod_row_index(i, tm, n_lat, seq, bsz):
    return jnp.where(i * tm < n_lat, (i * tm) // seq, bsz)


def peer_ln_pallas(xa, mod3, n_lat, seq, bsz, w_q, subkeys, u_tab, v_tab, ln_g, ln_b):
    n_tok, dm = xa.shape
    tm = PEER_TM
    f32, bf16 = jnp.float32, jnp.bfloat16
    wq_t = w_q.reshape(dm, PEER_HEADS * PEER_QDIM).T
    wq_bf = wq_t.astype(bf16)
    sk_hi = subkeys.astype(bf16)
    sk_lo = (subkeys - sk_hi.astype(f32)).astype(bf16)
    u_bf = u_tab.astype(bf16)
    n_eb = PEER_EXPERTS // PEER_EB
    vt_bf = v_tab.reshape(n_eb, PEER_EB, dm).transpose(0, 2, 1).astype(bf16)
    qd = PEER_HEADS * PEER_QDIM
    n_cand = sum(PEER_TOPK // (i + 1) for i in range(PEER_TOPK))
    n_cand_pad = 64
    assert n_cand <= n_cand_pad
    mod_idx = functools.partial(_mod_row_index, tm=tm, n_lat=n_lat, seq=seq, bsz=bsz)
    return pl.pallas_call(
        _peer_kernel,
        grid=(n_tok // tm, n_eb),
        in_specs=[
            pl.BlockSpec((tm, dm), lambda i, e: (i, 0)),
            pl.BlockSpec((1, 1, 6 * dm), lambda i, e: (mod_idx(i), 0, 0)),
            pl.BlockSpec((qd, dm), lambda i, e: (0, 0)),
            pl.BlockSpec((PEER_HEADS, 2, PEER_KEYS, PEER_QDIM // 2), lambda i, e: (0, 0, 0, 0)),
            pl.BlockSpec((PEER_HEADS, 2, PEER_KEYS, PEER_QDIM // 2), lambda i, e: (0, 0, 0, 0)),
            pl.BlockSpec((PEER_EB, dm), lambda i, e: (e, 0)),
            pl.BlockSpec((1, dm, PEER_EB), lambda i, e: (jnp.maximum(e - 1, 0), 0, 0)),
            pl.BlockSpec((1, dm, PEER_EB), lambda i, e: (n_eb - 1, 0, 0)),
            pl.BlockSpec((1, dm), lambda i, e: (0, 0)),
            pl.BlockSpec((1, dm), lambda i, e: (0, 0)),
        ],
        out_specs=pl.BlockSpec((tm, dm), lambda i, e: (i, 0)),
        out_shape=jax.ShapeDtypeStruct((n_tok, dm), f32),
        scratch_shapes=[
            pltpu.VMEM((dm, tm), bf16),
            pltpu.VMEM((2, PEER_KEYS, tm), f32),
            pltpu.VMEM((PEER_HEADS, PEER_KEYS, tm), f32),
            pltpu.VMEM((PEER_HEADS, PEER_KEYS, tm), f32),
            pltpu.VMEM((PEER_HEADS, PEER_KEYS, tm), bf16),
            pltpu.VMEM((PEER_HEADS, PEER_KEYS, tm), bf16),
            pltpu.VMEM((PEER_TOPK, tm), f32),
            pltpu.VMEM((PEER_TOPK, tm), f32),
            pltpu.VMEM((n_cand_pad, tm), f32),
            pltpu.VMEM((PEER_EB, tm), f32),
            pltpu.VMEM((2, PEER_EB, tm), bf16),
            pltpu.VMEM((dm, tm), f32),
        ],
        compiler_params=pltpu.CompilerParams(
            dimension_semantics=("arbitrary", "arbitrary"),
            vmem_limit_bytes=VMEM_LIMIT),
        name="peer_ln",
    )(xa, mod3, wq_bf, sk_hi, sk_lo, u_bf, vt_bf, vt_bf, ln_g.reshape(1, dm), ln_b.reshape(1, dm))


S5_CHUNK = 16
S5_LEVELS = 9


def _s5_prep(a_re, a_im, log_dt, b_re, b_im, c_re, c_im, d):
    f32, bf16 = jnp.float32, jnp.bfloat16
    hp = lax.Precision.HIGHEST
    n_g, n_p, q = S5_GROUPS, S5_STATE, S5_CHUNK
    width = q * S5_GROUP
    dt = jnp.exp(log_dt.astype(f32))[..., None]

    def apow(n):
        mag = jnp.exp(a_re * dt * n)
        ang = a_im * dt * n
        return mag * jnp.cos(ang), mag * jnp.sin(ang)

    def cmul(xr, xi, yr, yi):
        return xr * yr - xi * yi, xr * yi + xi * yr

    ar1, ai1 = apow(1.0)
    den = a_re * a_re + a_im * a_im
    nr, ni = ar1 - 1.0, ai1
    cr = (nr * a_re + ni * a_im) / den
    ci = (ni * a_re - nr * a_im) / den
    bbr = cr[..., None] * b_re - ci[..., None] * b_im
    bbi = cr[..., None] * b_im + ci[..., None] * b_re
    steps = jnp.arange(q, dtype=f32)

    mid = q // 2

    def powers(direction, sign):
        expo = (sign * (steps - mid))[None, :, None]
        mag = jnp.exp(a_re[direction][:, None, :] * dt[direction][:, None, :] * expo)
        ang = a_im[direction][:, None, :] * dt[direction][:, None, :] * expo
        return mag * jnp.cos(ang), mag * jnp.sin(ang)

    def factors(direction):
        lsign = -1.0 if direction == 0 else 1.0
        pr_l, pi_l = powers(direction, lsign)
        pr_r, pi_r = powers(direction, -lsign)
        bt_r = bbr[direction].transpose(0, 2, 1)[:, None, :, :]
        bt_i = bbi[direction].transpose(0, 2, 1)[:, None, :, :]
        lt_r, lt_i = cmul(bt_r, bt_i, pr_l[:, :, None, :], pi_l[:, :, None, :])
        ct_r = c_re[direction].transpose(0, 2, 1)[:, :, None, :]
        ct_i = c_im[direction].transpose(0, 2, 1)[:, :, None, :]
        r_r, r_i = cmul(ct_r, ct_i, pr_r.transpose(0, 2, 1)[:, :, :, None], pi_r.transpose(0, 2, 1)[:, :, :, None])
        return (lt_r.reshape(n_g, width, n_p), lt_i.reshape(n_g, width, n_p),
                r_r.reshape(n_g, n_p, width), r_i.reshape(n_g, n_p, width))

    row_t = jnp.arange(width)[:, None] // S5_GROUP
    col_t = jnp.arange(width)[None, :] // S5_GROUP
    tsum = jnp.eye(width, dtype=f32)[None] * jnp.tile(d.reshape(n_g, 1, S5_GROUP), (1, q, 1)).reshape(n_g, 1, width)
    win, wout = [], []
    for direction in range(2):
        lt_r, lt_i, r_r, r_i = factors(direction)
        prod = (jnp.einsum('gxp,gpy->gxy', lt_r, r_r, precision=hp)
                - jnp.einsum('gxp,gpy->gxy', lt_i, r_i, precision=hp))
        mask = (col_t >= row_t) if direction == 0 else (row_t >= col_t)
        tsum = tsum + jnp.where(mask[None], prod, 0.0)
        if direction == 0:
            in_r, in_i = apow(q - 1.0 - mid)
            out_r, out_i = apow(1.0 + mid)
        else:
            in_r, in_i = apow(float(mid))
            out_r, out_i = apow(float(q - mid))
        w_r, w_i = cmul(lt_r, lt_i, in_r[direction][:, None, :], in_i[direction][:, None, :])
        win.append((w_r, w_i))
        o_r, o_i = cmul(r_r, r_i, out_r[direction][:, :, None], out_i[direction][:, :, None])
        wout.append((o_r, -o_i))
    zc = jnp.zeros((n_g, width, n_p), f32)
    zr = jnp.zeros((n_g, n_p, width), f32)
    win_p, wout_p = [], []
    for direction in range(2):
        wr, wi = win[direction]
        even = jnp.concatenate([wr, zc, wi, zc], axis=2)
        odd = jnp.concatenate([zc, wr, zc, wi], axis=2)
        is_odd = (jnp.arange(n_g) % 2 == 1)[:, None, None]
        win_p.append(jnp.where(is_odd, odd, even))
        vr, vi = wout[direction]
        even = jnp.concatenate([vr, zr, vi, zr], axis=1)
        odd = jnp.concatenate([zr, vr, zr, vi], axis=1)
        wout_p.append(jnp.where(is_odd, odd, even))
    win_p = jnp.stack(win_p, axis=1).reshape(n_g // 2, 2, 2, 4 * n_p, 4 * n_p).transpose(0, 2, 1, 3, 4)
    wout_p = jnp.stack(wout_p, axis=1).reshape(n_g // 2, 2, 2, 4 * n_p, 4 * n_p).transpose(0, 2, 1, 3, 4)
    lvl = (q * 2.0 ** jnp.arange(S5_LEVELS, dtype=f32))[:, None, None, None]
    lr, li = apow(lvl)
    pw = jnp.stack([lr, li], axis=1)
    pw = pw.transpose(3, 2, 0, 1, 4).reshape(n_g // 2, 2, 2, S5_LEVELS, 2, n_p)
    pw = pw.transpose(0, 2, 3, 4, 1, 5).reshape(n_g // 2, 2, 2 * S5_LEVELS, 2 * n_p)
    return tsum.astype(bf16).reshape(n_g // 2, 2, width, width), win_p.astype(bf16), wout_p.astype(bf16), pw


def _shift_rows(v, r, up):
    n = v.shape[0]
    row = lax.broadcasted_iota(jnp.int32, v.shape, 0)
    if up:
        return jnp.where(row < n - r, pltpu.roll(v, n - r, axis=0), 0.0)
    return jnp.where(row >= r, pltpu.roll(v, r, axis=0), 0.0)


def _chunk_scan(re, im, pw, rows_per_chunk, up):
    n_chunks = re.shape[0] // rows_per_chunk
    level, s = 0, 1
    while s < n_chunks:
        ar, ai = pw[2 * level:2 * level + 1, :], pw[2 * level + 1:2 * level + 2, :]
        sre = _shift_rows(re, s * rows_per_chunk, up)
        sim = _shift_rows(im, s * rows_per_chunk, up)
        re, im = re + ar * sre - ai * sim, im + ar * sim + ai * sre
        level, s = level + 1, 2 * s
    return re, im


S5_GB = 8


def _s5_pair(hs, t_blk, win_blk, wout_blk, pw_blk, rows_per_chunk, ctx_chunks):
    f32, bf16 = jnp.float32, jnp.bfloat16
    bsz = rows_per_chunk
    rc = ctx_chunks * bsz
    half = hs[0].shape[1] // 2
    y = [jnp.dot(hs[gi], t_blk[gi], preferred_element_type=f32) for gi in range(2)]
    for direction in range(2):
        sloc = (jnp.dot(hs[0], win_blk[direction, 0], preferred_element_type=f32)
                + jnp.dot(hs[1], win_blk[direction, 1], preferred_element_type=f32))
        re, im = sloc[:, :half], sloc[:, half:]
        pw = pw_blk[direction]
        if direction == 0:
            sre, sim = _chunk_scan(_shift_rows(re, bsz, False), _shift_rows(im, bsz, False), pw, bsz, False)
        else:
            cre, cim = _chunk_scan(_shift_rows(re[:rc], bsz, True), _shift_rows(im[:rc], bsz, True), pw, bsz, True)
            ar, ai = pw[0:1, :], pw[1:2, :]
            fre = ar * cre[0:8] - ai * cim[0:8] + re[0:8]
            fim = ar * cim[0:8] + ai * cre[0:8] + im[0:8]
            row8 = lax.broadcasted_iota(jnp.int32, fre.shape, 0)
            tre = jnp.where(row8 >= 8 - bsz, pltpu.roll(fre, 8 - bsz, axis=0), 0.0)
            tim = jnp.where(row8 >= 8 - bsz, pltpu.roll(fim, 8 - bsz, axis=0), 0.0)
            lre, lim = _shift_rows(re[rc:], bsz, True), _shift_rows(im[rc:], bsz, True)
            lre = jnp.concatenate([lre[:-8], lre[-8:] + tre], axis=0)
            lim = jnp.concatenate([lim[:-8], lim[-8:] + tim], axis=0)
            lre, lim = _chunk_scan(lre, lim, pw, bsz, True)
            sre = jnp.concatenate([cre, lre], axis=0)
            sim = jnp.concatenate([cim, lim], axis=0)
        s_in = jnp.concatenate([sre, sim], axis=1).astype(bf16)
        for gi in range(2):
            y[gi] = y[gi] + jnp.dot(s_in, wout_blk[direction, gi], preferred_element_type=f32)
    return y


def _s5_rows_kernel(xl_ref, xc_ref, sh_ref, sc_ref, t_ref, win_ref, wout_ref, pw_ref, yl_ref, yc_ref, *, bsz):
    f32, bf16 = jnp.float32, jnp.bfloat16
    q, gw = S5_CHUNK, S5_GROUP
    b = pl.program_id(1)
    lat_chunks, ctx_chunks = xl_ref.shape[0] // q, xc_ref.shape[0] // q
    scale_l, shift_l = 1.0 + sc_ref[pl.ds(b, 1), :], sh_ref[pl.ds(b, 1), :]
    scale_c, shift_c = 1.0 + sc_ref[bsz:bsz + 1, :], sh_ref[bsz:bsz + 1, :]
    pieces = []
    for t in range(q):
        hc = xc_ref[pl.ds(t, ctx_chunks, stride=q), :] * scale_c + shift_c
        hl = xl_ref[pl.ds(t, lat_chunks, stride=q), :] * scale_l + shift_l
        pieces.append(jnp.concatenate([hc, hl], axis=0))
    rows = [jnp.concatenate([pieces[t][:, g * gw:(g + 1) * gw] for t in range(q)], axis=1).astype(bf16)
            for g in range(S5_GB)]
    ys = []
    for p in range(S5_GB // 2):
        ys += _s5_pair(rows[2 * p:2 * p + 2], t_ref[p], win_ref[p], wout_ref[p], pw_ref[p], 1, ctx_chunks)
    for t in range(q):
        yt = jnp.concatenate([ys[g][:, t * gw:(t + 1) * gw] for g in range(S5_GB)], axis=1)
        yc_ref[pl.ds(t, ctx_chunks, stride=q), :] = yt[:ctx_chunks]
        yl_ref[pl.ds(t, lat_chunks, stride=q), :] = yt[ctx_chunks:]


def s5_ssm_rows_pallas(xa, mod_tab, n_lat, seq, bsz, prep):
    tsum, win_p, wout_p, pw = prep
    f32 = jnp.float32
    n_tok, dm = xa.shape
    n_ctx = n_tok - n_lat
    ctx_len = n_ctx // bsz
    lanes = S5_GB * S5_GROUP
    n_gb = dm // lanes
    pairs = S5_GB // 2
    width = S5_CHUNK * S5_GROUP
    ctx_blk = n_lat // ctx_len
    yl, yc = pl.pallas_call(
        functools.partial(_s5_rows_kernel, bsz=bsz),
        grid=(n_gb, bsz),
        in_specs=[
            pl.BlockSpec((seq, lanes), lambda g, b: (b, g)),
            pl.BlockSpec((ctx_len, lanes), lambda g, b: (ctx_blk + b, g)),
            pl.BlockSpec((8, lanes), lambda g, b: (0, g)),
            pl.BlockSpec((8, lanes), lambda g, b: (0, n_gb + g)),
            pl.BlockSpec((pairs, 2, width, width), lambda g, b: (g, 0, 0, 0)),
            pl.BlockSpec((pairs, 2, 2, width, width), lambda g, b: (g, 0, 0, 0, 0)),
            pl.BlockSpec((pairs, 2, 2, width, width), lambda g, b: (g, 0, 0, 0, 0)),
            pl.BlockSpec((pairs, 2, 2 * S5_LEVELS, 2 * S5_STATE), lambda g, b: (g, 0, 0, 0)),
        ],
        out_specs=[
            pl.BlockSpec((seq, lanes), lambda g, b: (b, g)),
            pl.BlockSpec((ctx_len, lanes), lambda g, b: (b, g)),
        ],
        out_shape=[jax.ShapeDtypeStruct((n_lat, dm), f32), jax.ShapeDtypeStruct((n_ctx, dm), f32)],
        compiler_params=pltpu.CompilerParams(dimension_semantics=("arbitrary", "arbitrary"), vmem_limit_bytes=VMEM_LIMIT),
        name="s5_ssm_rows",
    )(xa, xa, mod_tab, mod_tab, tsum, win_p, wout_p, pw)
    return jnp.concatenate([yl, yc], axis=0)


def _glu_ln_kernel(y_ref, x_ref, mod_ref, wv_ref, wg_ref, lng_ref, lnb_ref, o_ref):
    f32 = jnp.float32
    dm = x_ref.shape[1]
    g = jax.nn.gelu(y_ref[...].astype(f32)).astype(jnp.bfloat16)
    val = jnp.dot(g, wv_ref[...], preferred_element_type=f32)
    gate = jnp.dot(g, wg_ref[...], preferred_element_type=f32)
    out = val * jax.nn.sigmoid(gate)
    mod = mod_ref[0]
    z = DEEPNORM_ALPHA * x_ref[...] + mod[:, 2 * dm:3 * dm] * out
    mu = jnp.mean(z, axis=-1, keepdims=True)
    zc = z - mu
    var = jnp.mean(zc * zc, axis=-1, keepdims=True)
    o_ref[...] = zc * lax.rsqrt(var + LN_EPS) * lng_ref[...] + lnb_ref[...]


TOK_TM = 512


def glu_ln_pallas(ya, xa, mod3, n_lat, seq, bsz, w_val, w_gate, ln_g, ln_b):
    n_tok, dm = ya.shape
    tm = TOK_TM
    bf16 = jnp.bfloat16
    mod_idx = functools.partial(_mod_row_index, tm=tm, n_lat=n_lat, seq=seq, bsz=bsz)
    return pl.pallas_call(
        _glu_ln_kernel,
        grid=(n_tok // tm,),
        in_specs=[
            pl.BlockSpec((tm, dm), lambda i: (i, 0)),
            pl.BlockSpec((tm, dm), lambda i: (i, 0)),
            pl.BlockSpec((1, 1, 6 * dm), lambda i: (mod_idx(i), 0, 0)),
            pl.BlockSpec((dm, dm), lambda i: (0, 0)),
            pl.BlockSpec((dm, dm), lambda i: (0, 0)),
            pl.BlockSpec((1, dm), lambda i: (0, 0)),
            pl.BlockSpec((1, dm), lambda i: (0, 0)),
        ],
        out_specs=pl.BlockSpec((tm, dm), lambda i: (i, 0)),
        out_shape=jax.ShapeDtypeStruct((n_tok, dm), jnp.float32),
        compiler_params=pltpu.CompilerParams(dimension_semantics=("arbitrary",), vmem_limit_bytes=VMEM_LIMIT),
        name="glu_ln",
    )(ya, xa, mod3, w_val.astype(bf16), w_gate.astype(bf16), ln_g.reshape(1, dm), ln_b.reshape(1, dm))


MLA_HD = 128
MLA_TQ = 256
MLA_HB = 8
MLA_KSPLIT = 4


def _mla_prep(w_down, w_uq, w_uk, w_uv):
    bf16 = jnp.bfloat16
    quarter = MLA_ROPE // 4
    swap = np.concatenate([np.arange(quarter, 2 * quarter), np.arange(0, quarter),
                           np.arange(3 * quarter, 4 * quarter), np.arange(2 * quarter, 3 * quarter)])
    dm = w_down.shape[0]
    w_cq = w_down[:, :MLA_Q_RANK]
    w_ckv = w_down[:, MLA_Q_RANK:MLA_Q_RANK + MLA_KV_RANK]
    w_kr = w_down[:, MLA_Q_RANK + MLA_KV_RANK:]
    wd_t = jnp.concatenate([w_cq, w_ckv, jnp.zeros((dm, MLA_NOPE), w_down.dtype), w_kr, w_kr[:, swap]], axis=1).T
    rope = w_uq[:, :, MLA_NOPE:]
    wuq_t = jnp.concatenate([w_uq, rope[:, :, swap]], axis=2).reshape(MLA_Q_RANK, MLA_HEADS * MLA_HD).T
    wuk = jnp.concatenate([w_uk, jnp.zeros((MLA_KV_RANK, MLA_HEADS, MLA_HD - MLA_NOPE), w_uk.dtype)], axis=2)
    wuk = wuk.reshape(MLA_KV_RANK, MLA_HEADS * MLA_HD)
    wuv_t = w_uv.reshape(MLA_KV_RANK, MLA_HEADS * MLA_V).T
    return wd_t.astype(bf16), wuq_t.astype(bf16), wuk.astype(bf16), wuv_t.astype(bf16)


def _rope_tables(seq, tm):
    quarter = MLA_ROPE // 4
    freqs = ROPE_BASE ** (-jnp.arange(quarter, dtype=jnp.float32) / quarter)
    t = jnp.arange(seq, dtype=jnp.float32)
    row, col = jnp.floor(t / GRID_W), t - GRID_W * jnp.floor(t / GRID_W)
    ang_r, ang_c = freqs[:, None] * row[None, :], freqs[:, None] * col[None, :]
    cos32 = jnp.concatenate([jnp.cos(ang_r), jnp.cos(ang_r), jnp.cos(ang_c), jnp.cos(ang_c)], axis=0)
    sin32 = jnp.concatenate([-jnp.sin(ang_r), jnp.sin(ang_r), -jnp.sin(ang_c), jnp.sin(ang_c)], axis=0)
    ones = jnp.ones((MLA_NOPE, seq + tm), jnp.float32)
    zeros = jnp.zeros((MLA_HD - MLA_NOPE - MLA_ROPE, seq + tm), jnp.float32)
    cos_t = jnp.concatenate([ones, jnp.concatenate([cos32, jnp.ones((MLA_ROPE, tm))], axis=1), zeros], axis=0)
    sin_t = jnp.concatenate([0.0 * ones, jnp.concatenate([sin32, jnp.zeros((MLA_ROPE, tm))], axis=1), zeros], axis=0)
    return cos_t, sin_t


def _mla_proj_kernel(x_ref, mod_ref, cos_ref, sin_ref, wd_ref, wuq_ref, wuk_ref, wuv_ref, qn_ref, kvn_ref,
                     qt_ref, k_ref, vt_ref):
    f32, bf16 = jnp.float32, jnp.bfloat16
    dm = x_ref.shape[1]
    mod = mod_ref[0]
    h = x_ref[...] * (1.0 + mod[:, dm:2 * dm]) + mod[:, 0:dm]
    ht = h.T.astype(bf16)
    dt = jnp.dot(wd_ref[...], ht, preferred_element_type=f32)
    cq, ckv, kr = dt[:MLA_Q_RANK], dt[MLA_Q_RANK:MLA_Q_RANK + MLA_KV_RANK], dt[MLA_Q_RANK + MLA_KV_RANK:]
    cqn = cq * lax.rsqrt(jnp.mean(cq * cq, axis=0, keepdims=True) + RMS_EPS) * qn_ref[...]
    ckvn = ckv * lax.rsqrt(jnp.mean(ckv * ckv, axis=0, keepdims=True) + RMS_EPS) * kvn_ref[...]
    cos_t, sin_t = cos_ref[...], sin_ref[...]

    def rope(v):
        shifted = jnp.concatenate([v[MLA_ROPE:], v[:MLA_ROPE]], axis=0)
        return v * cos_t + shifted * sin_t

    scale = (MLA_NOPE + MLA_ROPE) ** -0.5
    q_all = jnp.dot(wuq_ref[...], cqn.astype(bf16), preferred_element_type=f32)
    for hd in range(MLA_HEADS):
        qt_ref[hd] = (rope(q_all[hd * MLA_HD:(hd + 1) * MLA_HD]) * scale).astype(bf16)
    ckvn_bf = ckvn.astype(bf16)
    vt_ref[...] = jnp.dot(wuv_ref[...], ckvn_bf, preferred_element_type=f32).astype(bf16)
    k_all = jnp.dot(ckvn.T.astype(bf16), wuk_ref[...], preferred_element_type=f32)
    kr_rows = rope(kr).T
    for hd in range(MLA_HEADS):
        k_ref[hd] = (k_all[:, hd * MLA_HD:(hd + 1) * MLA_HD] + kr_rows).astype(bf16)


def _mla_attn_kernel(*refs, with_latent):
    f32, bf16 = jnp.float32, jnp.bfloat16
    if with_latent:
        q_ref, kl_ref, kc_ref, vl_ref, vc_ref, o_ref = refs
    else:
        q_ref, kc_ref, vc_ref, o_ref = refs
    for i in range(q_ref.shape[0]):
        rows = slice(i * MLA_V, (i + 1) * MLA_V)
        q = q_ref[i]
        s_c = jnp.dot(kc_ref[i], q, preferred_element_type=f32)
        m = jnp.max(s_c, axis=0, keepdims=True)
        parts = []
        if with_latent:
            n_keys = kl_ref.shape[1]
            for k0 in range(0, n_keys, n_keys // MLA_KSPLIT):
                ks = slice(k0, k0 + n_keys // MLA_KSPLIT)
                s_l = jnp.dot(kl_ref[i, ks, :], q, preferred_element_type=f32)
                m = jnp.maximum(m, jnp.max(s_l, axis=0, keepdims=True))
                parts.append((ks, s_l))
        p_c = jnp.exp(s_c - m)
        den = jnp.sum(p_c, axis=0, keepdims=True)
        o = jnp.dot(vc_ref[rows, :], p_c.astype(bf16), preferred_element_type=f32)
        for ks, s_l in parts:
            p_l = jnp.exp(s_l - m)
            den = den + jnp.sum(p_l, axis=0, keepdims=True)
            o = o + jnp.dot(vl_ref[rows, ks], p_l.astype(bf16), preferred_element_type=f32)
        o_ref[rows, :] = (o / den).astype(bf16)


def _mla_out_kernel(ot_ref, x_ref, mod_ref, wo_ref, lng_ref, lnb_ref, o_ref):
    f32 = jnp.float32
    dm = x_ref.shape[1]
    attn = ot_ref[...].astype(f32).T.astype(jnp.bfloat16)
    out = jnp.dot(attn, wo_ref[...], preferred_element_type=f32)
    mod = mod_ref[0]
    z = DEEPNORM_ALPHA * x_ref[...] + mod[:, 2 * dm:3 * dm] * out
    mu = jnp.mean(z, axis=-1, keepdims=True)
    zc = z - mu
    var = jnp.mean(zc * zc, axis=-1, keepdims=True)
    o_ref[...] = zc * lax.rsqrt(var + LN_EPS) * lng_ref[...] + lnb_ref[...]


def mla_ln_pallas(xa, mod3, n_lat, seq, bsz, w_down, q_norm, kv_norm, w_uq, w_uk, w_uv, w_o, ln_g, ln_b):
    f32, bf16 = jnp.float32, jnp.bfloat16
    n_tok, dm = xa.shape
    n_ctx = n_tok - n_lat
    ctx_len = n_ctx // bsz
    tm = TOK_TM
    wd_t, wuq_t, wuk, wuv_t = _mla_prep(w_down, w_uq, w_uk, w_uv)
    cos_t, sin_t = _rope_tables(seq, tm)
    mod_idx = functools.partial(_mod_row_index, tm=tm, n_lat=n_lat, seq=seq, bsz=bsz)
    tiles_per_sample = seq // tm
    pos_idx = lambda i: jnp.where(i * tm < n_lat, i % tiles_per_sample, tiles_per_sample)
    n_hd, n_dn = MLA_HEADS * MLA_HD, wd_t.shape[0]
    whole = lambda shape: pl.BlockSpec(shape, lambda i: (0,) * len(shape))
    params = pltpu.CompilerParams(dimension_semantics=("arbitrary",), vmem_limit_bytes=VMEM_LIMIT)
    qt, k, vt = pl.pallas_call(
        _mla_proj_kernel,
        grid=(n_tok // tm,),
        in_specs=[
            pl.BlockSpec((tm, dm), lambda i: (i, 0)),
            pl.BlockSpec((1, 1, 6 * dm), lambda i: (mod_idx(i), 0, 0)),
            pl.BlockSpec((MLA_HD, tm), lambda i: (0, pos_idx(i))),
            pl.BlockSpec((MLA_HD, tm), lambda i: (0, pos_idx(i))),
            whole((n_dn, dm)), whole((n_hd, MLA_Q_RANK)), whole((MLA_KV_RANK, n_hd)),
            whole((MLA_HEADS * MLA_V, MLA_KV_RANK)), whole((MLA_Q_RANK, 1)), whole((MLA_KV_RANK, 1)),
        ],
        out_specs=[
            pl.BlockSpec((MLA_HEADS, MLA_HD, tm), lambda i: (0, 0, i)),
            pl.BlockSpec((MLA_HEADS, tm, MLA_HD), lambda i: (0, i, 0)),
            pl.BlockSpec((MLA_HEADS * MLA_V, tm), lambda i: (0, i)),
        ],
        out_shape=[
            jax.ShapeDtypeStruct((MLA_HEADS, MLA_HD, n_tok), bf16),
            jax.ShapeDtypeStruct((MLA_HEADS, n_tok, MLA_HD), bf16),
            jax.ShapeDtypeStruct((MLA_HEADS * MLA_V, n_tok), bf16),
        ],
        compiler_params=params,
        name="mla_proj",
    )(xa, mod3, cos_t, sin_t, wd_t, wuq_t, wuk, wuv_t, q_norm.reshape(-1, 1), kv_norm.reshape(-1, 1))

    tq, hb = MLA_TQ, MLA_HB
    n_qt = seq // tq
    ctx_blk = n_lat // ctx_len
    params3 = pltpu.CompilerParams(dimension_semantics=("arbitrary",) * 3, vmem_limit_bytes=VMEM_LIMIT)
    ot_lat = pl.pallas_call(
        functools.partial(_mla_attn_kernel, with_latent=True),
        grid=(bsz, MLA_HEADS // hb, n_qt),
        in_specs=[
            pl.BlockSpec((hb, MLA_HD, tq), lambda b, hd, t: (hd, 0, b * n_qt + t)),
            pl.BlockSpec((hb, seq, MLA_HD), lambda b, hd, t: (hd, b, 0)),
            pl.BlockSpec((hb, ctx_len, MLA_HD), lambda b, hd, t: (hd, ctx_blk + b, 0)),
            pl.BlockSpec((hb * MLA_V, seq), lambda b, hd, t: (hd, b)),
            pl.BlockSpec((hb * MLA_V, ctx_len), lambda b, hd, t: (hd, ctx_blk + b)),
        ],
        out_specs=pl.BlockSpec((hb * MLA_V, tq), lambda b, hd, t: (hd, b * n_qt + t)),
        out_shape=jax.ShapeDtypeStruct((MLA_HEADS * MLA_V, n_lat), bf16),
        compiler_params=params3,
        name="mla_attn_latent",
    )(qt, k, k, vt, vt)
    ot_ctx = pl.pallas_call(
        functools.partial(_mla_attn_kernel, with_latent=False),
        grid=(bsz, MLA_HEADS // hb, 1),
        in_specs=[
            pl.BlockSpec((hb, MLA_HD, ctx_len), lambda b, hd, t: (hd, 0, ctx_blk + b)),
            pl.BlockSpec((hb, ctx_len, MLA_HD), lambda b, hd, t: (hd, ctx_blk + b, 0)),
            pl.BlockSpec((hb * MLA_V, ctx_len), lambda b, hd, t: (hd, ctx_blk + b)),
        ],
        out_specs=pl.BlockSpec((hb * MLA_V, ctx_len), lambda b, hd, t: (hd, b)),
        out_shape=jax.ShapeDtypeStruct((MLA_HEADS * MLA_V, n_ctx), bf16),
        compiler_params=params3,
        name="mla_attn_context",
    )(qt, k, vt)
    ot = jnp.concatenate([ot_lat, ot_ctx], axis=1)
    return pl.pallas_call(
        _mla_out_kernel,
        grid=(n_tok // tm,),
        in_specs=[
            pl.BlockSpec((MLA_HEADS * MLA_V, tm), lambda i: (0, i)),
            pl.BlockSpec((tm, dm), lambda i: (i, 0)),
            pl.BlockSpec((1, 1, 6 * dm), lambda i: (mod_idx(i), 0, 0)),
            whole((MLA_HEADS * MLA_V, dm)), whole((1, dm)), whole((1, dm)),
        ],
        out_specs=pl.BlockSpec((tm, dm), lambda i: (i, 0)),
        out_shape=jax.ShapeDtypeStruct((n_tok, dm), f32),
        compiler_params=params,
        name="mla_out_ln",
    )(ot, xa, mod3, w_o.astype(bf16), ln_g.reshape(1, dm), ln_b.reshape(1, dm))


SSD_DT_PAD = 128
SSD_HPG = SSD_HEADS // SSD_GROUPS
SSD_GW = SSD_HPG * SSD_HEADDIM


def _ssd_in_kernel(x_ref, mod_ref, w_ref, wdt_ref, z_ref, xbc_ref, dt_ref, dtt_ref):
    f32, bf16 = jnp.float32, jnp.bfloat16
    dm = x_ref.shape[1]
    mod = mod_ref[0]
    h = x_ref[...] * (1.0 + mod[:, dm:2 * dm]) + mod[:, 0:dm]
    proj = jnp.dot(h.astype(bf16), w_ref[...], preferred_element_type=f32)
    z_ref[...] = proj[:, :SSD_D_INNER]
    xbc_ref[...] = proj[:, SSD_D_INNER:SSD_D_INNER + SSD_CONV_DIM]
    dt_ref[...] = proj[:, SSD_D_INNER + SSD_CONV_DIM:]
    dtt_ref[...] = jnp.dot(wdt_ref[...], h.T.astype(bf16), preferred_element_type=f32)


def _softplus(v):
    return jnp.maximum(v, 0.0) + jnp.log(1.0 + jnp.exp(-jnp.abs(v)))


def _split3(v):
    f32, bf16 = jnp.float32, jnp.bfloat16
    hi = v.astype(bf16)
    r1 = v - hi.astype(f32)
    mid = r1.astype(bf16)
    lo = (r1 - mid.astype(f32)).astype(bf16)
    return jnp.concatenate([hi, mid, lo], axis=1)


def _ssd_direction(direction, xbc_ref, prev_ref, next_ref, dt_ref, dtt_ref, has_prev, has_next,
                   cw_ref, cb_ref, bias_r_ref, a_r_ref, bias_c_ref, a_c_ref, e_ref, dskip_ref, state_ref, y_ref):
    f32, bf16 = jnp.float32, jnp.bfloat16
    hp = lax.Precision.HIGHEST
    q = xbc_ref.shape[0]
    nh = SSD_HEADS
    xm = xbc_ref[...]
    row = lax.broadcasted_iota(jnp.int32, xm.shape, 0)
    before = jnp.where(has_prev, prev_ref[7:8, :], 0.0)
    after = jnp.where(has_next, next_ref[0:1, :], 0.0)
    x_dn = jnp.where(row == 0, before, pltpu.roll(xm, 1, axis=0))
    x_up = jnp.where(row == q - 1, after, pltpu.roll(xm, q - 1, axis=0))
    conv = cb_ref[...] + x_dn * cw_ref[0:1, :] + xm * cw_ref[1:2, :] + x_up * cw_ref[2:3, :]
    conv = conv * jax.nn.sigmoid(conv)
    xs = conv[:, :SSD_D_INNER]
    gn = SSD_GROUPS * SSD_STATE
    bm, cm = conv[:, SSD_D_INNER:SSD_D_INNER + gn], conv[:, SSD_D_INNER + gn:]
    hs = slice(direction * nh, (direction + 1) * nh)
    dt = _softplus(dt_ref[:, hs] + bias_r_ref[direction:direction + 1, :])
    dtt = _softplus(dtt_ref[hs, :] + bias_c_ref[direction])
    r_i = lax.broadcasted_iota(jnp.int32, (q, q), 0)
    c_i = lax.broadcasted_iota(jnp.int32, (q, q), 1)
    causal = (r_i >= c_i) if direction == 0 else (r_i <= c_i)
    tri = causal.astype(f32)
    tri_t = ((c_i >= r_i) if direction == 0 else (c_i <= r_i)).astype(f32)
    a_cum = jnp.dot(tri, dt * a_r_ref[direction:direction + 1, :], precision=hp, preferred_element_type=f32)
    a_cum_t = jnp.dot(dtt * a_c_ref[direction], tri_t, precision=hp, preferred_element_type=f32)
    last = q - 1 if direction == 0 else 0
    a_exp = jnp.dot(_split3(a_cum), e_ref[...], preferred_element_type=f32)
    dt_exp = jnp.dot(_split3(dt), e_ref[...], preferred_element_type=f32)
    xdt = xs * dt_exp
    a_tot = a_exp[last:last + 1, :]
    xw = (xdt * jnp.exp(a_tot - a_exp)).astype(bf16)
    grow = jnp.exp(a_exp)
    carry = jnp.exp(a_tot)
    xdt_bf = xdt.astype(bf16)
    for g in range(SSD_GROUPS):
        gl = slice(g * SSD_GW, (g + 1) * SSD_GW)
        nl = slice(g * SSD_STATE, (g + 1) * SSD_STATE)
        bm_g, cm_g = bm[:, nl], cm[:, nl].astype(bf16)
        prev = state_ref[direction, g]
        y_g = jnp.dot(cm_g, prev.astype(bf16), preferred_element_type=f32) * grow[:, gl]
        states = jnp.dot(bm_g.T.astype(bf16), xw[:, gl], preferred_element_type=f32)
        state_ref[direction, g] = prev * carry[:, gl] + states
        cb = lax.dot_general(cm_g, bm_g.astype(bf16), (((1,), (1,)), ((), ())), preferred_element_type=f32)
        parts = []
        for hh in range(SSD_HPG):
            hd = g * SSD_HPG + hh
            seg = jnp.broadcast_to(a_cum[:, hd:hd + 1], (q, q)) - a_cum_t[hd:hd + 1, :]
            lmat = (jnp.where(causal, jnp.exp(seg), 0.0) * cb).astype(bf16)
            parts.append(jnp.dot(lmat, xdt_bf[:, hd * SSD_HEADDIM:(hd + 1) * SSD_HEADDIM], preferred_element_type=f32))
        y_g = y_g + jnp.concatenate(parts, axis=1)
        if direction == 0:
            y_g = y_g + dskip_ref[:, gl] * xs[:, gl]
        y_ref[:, gl] = y_g


def _ssd_scan_kernel(xf_ref, pf_ref, nf_ref, dtf_ref, dttf_ref, xb_ref, pb_ref, nb_ref, dtb_ref, dttb_ref,
                     cw_ref, cb_ref, bias_r_ref, a_r_ref, bias_c_ref, a_c_ref, e_ref, dskip_ref,
                     yf_ref, yb_ref, state_ref, *, ctx_chunks, lat_chunks):
    j = pl.program_id(1)

    @pl.when(j == 0)
    def _reset():
        state_ref[...] = jnp.zeros(state_ref.shape, jnp.float32)

    in_ctx = j < ctx_chunks
    pos_f = jnp.where(in_ctx, j, j - ctx_chunks)
    seg_len = jnp.where(in_ctx, ctx_chunks, lat_chunks)
    pos_b = seg_len - 1 - pos_f
    shared = (cw_ref, cb_ref, bias_r_ref, a_r_ref, bias_c_ref, a_c_ref, e_ref, dskip_ref, state_ref)
    _ssd_direction(0, xf_ref, pf_ref, nf_ref, dtf_ref, dttf_ref, pos_f > 0, pos_f < seg_len - 1, *shared, yf_ref)
    _ssd_direction(1, xb_ref, pb_ref, nb_ref, dtb_ref, dttb_ref, pos_b > 0, pos_b < seg_len - 1, *shared, yb_ref)


def _ssd_out_kernel(yf_ref, yb_ref, z_ref, x_ref, mod_ref, nw_ref, wo_ref, lng_ref, lnb_ref, o_ref):
    f32 = jnp.float32
    dm = x_ref.shape[1]
    z = z_ref[...]
    y = (yf_ref[...] + yb_ref[...]) * (z * jax.nn.sigmoid(z))
    y = y * lax.rsqrt(jnp.mean(y * y, axis=-1, keepdims=True) + RMS_EPS) * nw_ref[...]
    out = jnp.dot(y.astype(jnp.bfloat16), wo_ref[...], preferred_element_type=f32)
    mod = mod_ref[0]
    v = DEEPNORM_ALPHA * x_ref[...] + mod[:, 2 * dm:3 * dm] * out
    mu = jnp.mean(v, axis=-1, keepdims=True)
    vc = v - mu
    var = jnp.mean(vc * vc, axis=-1, keepdims=True)
    o_ref[...] = vc * lax.rsqrt(var + LN_EPS) * lng_ref[...] + lnb_ref[...]


def ssd_ln_pallas(xa, mod3, n_lat, seq, bsz, w_in, conv_w, conv_b, dt_bias, a_log, d, norm_w, w_out, ln_g, ln_b):
    f32, bf16 = jnp.float32, jnp.bfloat16
    n_tok, dm = xa.shape
    ctx_len = (n_tok - n_lat) // bsz
    tm = TOK_TM
    q = SSD_CHUNK
    nh = SSD_HEADS
    pad = SSD_DT_PAD - 2 * nh
    w_ext = jnp.concatenate([w_in, jnp.zeros((dm, pad), w_in.dtype)], axis=1).astype(bf16)
    n_in = w_ext.shape[1]
    wdt_t = w_ext[:, SSD_D_INNER + SSD_CONV_DIM:].T
    mod_idx = functools.partial(_mod_row_index, tm=tm, n_lat=n_lat, seq=seq, bsz=bsz)
    whole = lambda shape: pl.BlockSpec(shape, lambda *_: (0,) * len(shape))
    params = pltpu.CompilerParams(dimension_semantics=("arbitrary",), vmem_limit_bytes=VMEM_LIMIT)
    z, xbc, dt, dtt = pl.pallas_call(
        _ssd_in_kernel,
        grid=(n_tok // tm,),
        in_specs=[
            pl.BlockSpec((tm, dm), lambda i: (i, 0)),
            pl.BlockSpec((1, 1, 6 * dm), lambda i: (mod_idx(i), 0, 0)),
            whole((dm, n_in)), whole((SSD_DT_PAD, dm)),
        ],
        out_specs=[
            pl.BlockSpec((tm, SSD_D_INNER), lambda i: (i, 0)),
            pl.BlockSpec((tm, SSD_CONV_DIM), lambda i: (i, 0)),
            pl.BlockSpec((tm, SSD_DT_PAD), lambda i: (i, 0)),
            pl.BlockSpec((SSD_DT_PAD, tm), lambda i: (0, i)),
        ],
        out_shape=[
            jax.ShapeDtypeStruct((n_tok, SSD_D_INNER), f32),
            jax.ShapeDtypeStruct((n_tok, SSD_CONV_DIM), f32),
            jax.ShapeDtypeStruct((n_tok, SSD_DT_PAD), f32),
            jax.ShapeDtypeStruct((SSD_DT_PAD, n_tok), f32),
        ],
        compiler_params=params,
        name="ssd_in_proj",
    )(xa, mod3, w_ext, wdt_t)

    ctx_chunks, lat_chunks = ctx_len // q, seq // q
    ctx_base = n_lat // q
    n_chunk_total = n_tok // q

    def chunk_of(b, j, backward):
        in_ctx = j < ctx_chunks
        pos = jnp.where(in_ctx, j, j - ctx_chunks)
        seg = jnp.where(in_ctx, ctx_chunks, lat_chunks)
        pos = jnp.where(backward, seg - 1 - pos, pos)
        return jnp.where(in_ctx, ctx_base + b * ctx_chunks + pos, b * lat_chunks + pos)

    sub = q // 8
    def dir_specs(backward):
        ch = lambda b, j: chunk_of(b, j, backward)
        return [
            pl.BlockSpec((q, SSD_CONV_DIM), lambda b, j: (ch(b, j), 0)),
            pl.BlockSpec((8, SSD_CONV_DIM), lambda b, j: (jnp.maximum(ch(b, j) * sub - 1, 0), 0)),
            pl.BlockSpec((8, SSD_CONV_DIM), lambda b, j: (jnp.minimum((ch(b, j) + 1) * sub, n_chunk_total * sub - 1), 0)),
            pl.BlockSpec((q, SSD_DT_PAD), lambda b, j: (ch(b, j), 0)),
            pl.BlockSpec((SSD_DT_PAD, q), lambda b, j: (0, ch(b, j))),
        ]

    a = -jnp.exp(a_log.astype(f32))
    expand = jnp.repeat(jnp.eye(nh, dtype=f32), SSD_HEADDIM, axis=1)
    e3 = jnp.concatenate([expand, expand, expand], axis=0).astype(bf16)
    dskip = jnp.repeat(d, SSD_HEADDIM)[None, :]
    yf, yb = pl.pallas_call(
        functools.partial(_ssd_scan_kernel, ctx_chunks=ctx_chunks, lat_chunks=lat_chunks),
        grid=(bsz, ctx_chunks + lat_chunks),
        in_specs=dir_specs(False) + dir_specs(True) + [
            whole((SSD_CONV, SSD_CONV_DIM)), whole((1, SSD_CONV_DIM)),
            whole((2, nh)), whole((2, nh)), whole((2, nh, 1)), whole((2, nh, 1)),
            whole((3 * nh, SSD_D_INNER)), whole((1, SSD_D_INNER)),
        ],
        out_specs=[
            pl.BlockSpec((q, SSD_D_INNER), lambda b, j: (chunk_of(b, j, False), 0)),
            pl.BlockSpec((q, SSD_D_INNER), lambda b, j: (chunk_of(b, j, True), 0)),
        ],
        out_shape=[jax.ShapeDtypeStruct((n_tok, SSD_D_INNER), f32)] * 2,
        scratch_shapes=[pltpu.VMEM((2, SSD_GROUPS, SSD_STATE, SSD_GW), f32)],
        compiler_params=pltpu.CompilerParams(dimension_semantics=("arbitrary", "arbitrary"), vmem_limit_bytes=VMEM_LIMIT),
        name="ssd_scan",
    )(xbc, xbc, xbc, dt, dtt, xbc, xbc, xbc, dt, dtt,
      conv_w, conv_b.reshape(1, -1), dt_bias, a, dt_bias.reshape(2, nh, 1), a.reshape(2, nh, 1), e3, dskip)

    return pl.pallas_call(
        _ssd_out_kernel,
        grid=(n_tok // tm,),
        in_specs=[
            pl.BlockSpec((tm, SSD_D_INNER), lambda i: (i, 0)),
            pl.BlockSpec((tm, SSD_D_INNER), lambda i: (i, 0)),
            pl.BlockSpec((tm, SSD_D_INNER), lambda i: (i, 0)),
            pl.BlockSpec((tm, dm), lambda i: (i, 0)),
            pl.BlockSpec((1, 1, 6 * dm), lambda i: (mod_idx(i), 0, 0)),
            whole((1, SSD_D_INNER)), whole((SSD_D_INNER, dm)), whole((1, dm)), whole((1, dm)),
        ],
        out_specs=pl.BlockSpec((tm, dm), lambda i: (i, 0)),
        out_shape=jax.ShapeDtypeStruct((n_tok, dm), f32),
        compiler_params=params,
        name="ssd_out_ln",
    )(yf, yb, z, xa, mod3, norm_w.reshape(1, -1), w_out.astype(bf16), ln_g.reshape(1, dm), ln_b.reshape(1, dm))


def _modulation_kernel(c_ref, w_ref, b_ref, o_ref):
    cv = c_ref[...]
    act = cv * jax.nn.sigmoid(cv)
    o_ref[0] = jnp.dot(act, w_ref[0], precision=lax.Precision.HIGHEST,
                       preferred_element_type=jnp.float32) + b_ref[0]


def modulation_pallas(cond, mod_w, mod_b):
    depth, dm, n_out = mod_w.shape
    tn = dm
    return pl.pallas_call(
        _modulation_kernel,
        grid=(depth, n_out // tn),
        in_specs=[
            pl.BlockSpec((8, dm), lambda i, n: (0, 0)),
            pl.BlockSpec((1, dm, tn), lambda i, n: (i, 0, n)),
            pl.BlockSpec((1, 1, tn), lambda i, n: (i, 0, n)),
        ],
        out_specs=pl.BlockSpec((1, 8, tn), lambda i, n: (i, 0, n)),
        out_shape=jax.ShapeDtypeStruct((depth, 8, n_out), jnp.float32),
        compiler_params=pltpu.CompilerParams(dimension_semantics=("arbitrary", "arbitrary"), vmem_limit_bytes=VMEM_LIMIT),
        name="modulation",
    )(cond, mod_w, mod_b.reshape(depth, 1, n_out))


def kernel(x, c, ctx, c_ctx, mod_w, mod_b, ln_g, ln_b,
           s5_a_re, s5_a_im, s5_log_dt, s5_b_re, s5_b_im, s5_c_re, s5_c_im, s5_d, s5_w_gate, s5_w_val,
           ssd_w_in, ssd_conv_w, ssd_conv_b, ssd_dt_bias, ssd_a_log, ssd_d, ssd_norm_w, ssd_w_out,
           mla_w_down, mla_q_norm, mla_kv_norm, mla_w_uq, mla_w_uk, mla_w_uv, mla_w_o,
           peer_w_q, peer_subkeys, peer_u, peer_v):
    ctx_len = ctx.shape[1]
    bsz, seq_len, dm = x.shape
    n_lat, n_ctx = bsz * seq_len, bsz * ctx_len
    xa = jnp.concatenate([x.reshape(n_lat, dm), ctx.reshape(n_ctx, dm)], axis=0)
    cond = jnp.concatenate([c, c_ctx[None], jnp.zeros((7 - bsz, dm), x.dtype)], axis=0)
    mod_all = modulation_pallas(cond, mod_w, mod_b)
    for i in range(DEPTH):
        last = i == DEPTH - 1
        mod_tab = mod_all[i]
        mod3 = mod_tab.reshape(8, 1, 6 * dm)
        kind, j = i % N_MIXERS, i // N_MIXERS
        if kind == 0:
            prep = _s5_prep(s5_a_re[j], s5_a_im[j], s5_log_dt[j], s5_b_re[j], s5_b_im[j], s5_c_re[j], s5_c_im[j], s5_d[j])
            ya = s5_ssm_rows_pallas(xa, mod_tab, n_lat, seq_len, bsz, prep)
            if last:
                ya, xa = ya[:n_lat], xa[:n_lat]
            xa = glu_ln_pallas(ya, xa, mod3, n_lat, seq_len, bsz, s5_w_val[j], s5_w_gate[j], ln_g[i, 0], ln_b[i, 0])
        elif kind == 1:
            xa = ssd_ln_pallas(xa, mod3, n_lat, seq_len, bsz, ssd_w_in[j], ssd_conv_w[j], ssd_conv_b[j], ssd_dt_bias[j],
                               ssd_a_log[j], ssd_d[j], ssd_norm_w[j], ssd_w_out[j], ln_g[i, 0], ln_b[i, 0])
            if last:
                xa = xa[:n_lat]
        else:
            xa = mla_ln_pallas(xa, mod3, n_lat, seq_len, bsz, mla_w_down[j], mla_q_norm[j], mla_kv_norm[j],
                               mla_w_uq[j], mla_w_uk[j], mla_w_uv[j], mla_w_o[j], ln_g[i, 0], ln_b[i, 0])
            if last:
                xa = xa[:n_lat]
        xa = peer_ln_pallas(xa, mod3, n_lat, seq_len, bsz, peer_w_q[i], peer_subkeys[i], peer_u[i], peer_v[i],
                            ln_g[i, 1], ln_b[i, 1])
    return xa[:n_lat].reshape(bsz, seq_len, dm)
```
